```python
import math
import jax, jax.numpy as jnp
from jax import lax
import numpy as np

D_MODEL = 1024
BATCH = 1
SEQ = 16384
DEPTH = 4

GRID_W = 64
CTX_LEN = 256
EPS = 1e-6

SSD_HEADS = 4
SSD_HEAD_DIM = 64
SSD_INNER = SSD_HEADS * SSD_HEAD_DIM
SSD_GROUPS = 2
SSD_STATE = 64
SSD_CONV = 3
SSD_CHUNK = 128
SSD_CONV_DIM = SSD_INNER + 2 * SSD_GROUPS * SSD_STATE

S5_WIDTH = 256
S5_GROUP = 16
S5_GROUPS = S5_WIDTH // S5_GROUP
S5_STATE = 64

DA_HEADS = 8
DA_HEAD_DIM = 32
DA_V_DIM = 2 * DA_HEAD_DIM
DA_WIDTH = DA_HEADS * DA_V_DIM
DA_QK_WIDTH = DA_HEADS * 2 * DA_HEAD_DIM
Q_BLOCK = 128
ROPE_THETA = 10000.0
ROPE_FREQS = DA_HEAD_DIM // 4

MIX_WIDTH = SSD_INNER + S5_WIDTH + DA_WIDTH
IN_SSD = SSD_INNER + SSD_CONV_DIM + 2 * SSD_HEADS
IN_S5 = S5_WIDTH
IN_DA = 2 * DA_QK_WIDTH + DA_WIDTH
IN_WIDTH = IN_SSD + IN_S5 + IN_DA

D_FF = ((8 * D_MODEL // 3 + 255) // 256) * 256

kernel_name = "hybrid_ssd_s5_diffattn_prefix_dit"


def rmsnorm(x, g):
    xf = x.astype(jnp.float32)
    y = xf * lax.rsqrt(jnp.mean(xf * xf, axis=-1, keepdims=True) + EPS) * g.astype(jnp.float32)
    return y.astype(x.dtype)


def dwconv(x, w, b):
    out = lax.conv_general_dilated(x, w[:, None, :], window_strides=(1,), padding='SAME',
                                   dimension_numbers=('NWC', 'WIO', 'NWC'),
                                   feature_group_count=x.shape[-1])
    return out + b


def segsum(a):
    t = a.shape[-1]
    cs = jnp.cumsum(a, axis=-1)
    diff = cs[..., :, None] - cs[..., None, :]
    return jnp.where(jnp.tril(jnp.ones((t, t), bool)), diff, -jnp.inf)


def ssd_scan(x, dt, a, b_in, c_in, h0):
    bsz, L, H, P = x.shape
    nc = L // SSD_CHUNK
    rep = H // b_in.shape[2]
    bh = jnp.repeat(b_in, rep, axis=2).reshape(bsz, nc, SSD_CHUNK, H, -1)
    ch = jnp.repeat(c_in, rep, axis=2).reshape(bsz, nc, SSD_CHUNK, H, -1)
    xd = (x * dt[..., None]).reshape(bsz, nc, SSD_CHUNK, H, P)
    a_dt = (dt * a).reshape(bsz, nc, SSD_CHUNK, H).transpose(0, 3, 1, 2)
    a_cs = jnp.cumsum(a_dt, axis=-1)
    decay = jnp.exp(segsum(a_dt))
    y_diag = jnp.einsum('bclhn,bcshn,bhcls,bcshp->bclhp', ch, bh, decay, xd)
    decay_states = jnp.exp(a_cs[..., -1:] - a_cs)
    states = jnp.einsum('bclhn,bhcl,bclhp->bchpn', bh, decay_states, xd)
    chunk_decay = jnp.exp(a_cs[..., -1])

    def step(s, inp):
        st, dec = inp
        return s * dec[..., None, None] + st, s

    final, prev = lax.scan(step, h0, (states.transpose(1, 0, 2, 3, 4), chunk_decay.transpose(2, 0, 1)))
    prev = prev.transpose(1, 0, 2, 3, 4)
    y_off = jnp.einsum('bclhn,bchpn,bhcl->bclhp', ch, prev, jnp.exp(a_cs))
    return (y_diag + y_off).reshape(bsz, L, H, P), final


def ssd_branch(p, conv_w, conv_b, a_log, dt_bias, d_skip, norm_g, h0_fwd, h0_bwd):
    bsz, L, _ = p.shape
    z, xbc, dt = jnp.split(p, [SSD_INNER, SSD_INNER + SSD_CONV_DIM], axis=-1)
    xbc = jax.nn.silu(dwconv(xbc, conv_w, conv_b)).astype(jnp.float32)
    xs, bs, cs = jnp.split(xbc, [SSD_INNER, SSD_INNER + SSD_GROUPS * SSD_STATE], axis=-1)
    xs = xs.reshape(bsz, L, SSD_HEADS, SSD_HEAD_DIM)
    bs = bs.reshape(bsz, L, SSD_GROUPS, SSD_STATE)
    cs = cs.reshape(bsz, L, SSD_GROUPS, SSD_STATE)
    dt = jax.nn.softplus(dt.astype(jnp.float32).reshape(bsz, L, 2, SSD_HEADS) + dt_bias.astype(jnp.float32))
    a = -jnp.exp(a_log.astype(jnp.float32))
    rev = lambda t: t[:, ::-1]
    y_f, s_f = ssd_scan(xs, dt[:, :, 0], a[0], bs, cs, h0_fwd)
    y_b, s_b = ssd_scan(rev(xs), rev(dt[:, :, 1]), a[1], rev(bs), rev(cs), h0_bwd)
    y = y_f + rev(y_b) + d_skip.astype(jnp.float32)[:, None] * xs
    y = y.reshape(bsz, L, SSD_INNER) * jax.nn.silu(z.astype(jnp.float32))
    return rmsnorm(y, norm_g).astype(p.dtype), s_f, s_b


def s5_direction(ug, lam_re, lam_im, log_step, b_re, b_im, c_re, c_im, h0):
    lam = lax.complex(lam_re.astype(jnp.float32), lam_im.astype(jnp.float32))
    step = jnp.exp(log_step.astype(jnp.float32))[:, None]
    lam_bar = jnp.exp(lam * step)
    b = lax.complex(b_re.astype(jnp.float32), b_im.astype(jnp.float32))
    b_bar = ((lam_bar - 1.0) / lam)[..., None] * b
    cm = lax.complex(c_re.astype(jnp.float32), c_im.astype(jnp.float32))
    bu = jnp.einsum('blgc,gpc->blgp', ug, b_bar)
    bu = bu.at[:, 0].add(lam_bar * h0)
    lam_seq = jnp.broadcast_to(lam_bar, bu.shape)

    def combine(e1, e2):
        a1, v1 = e1
        a2, v2 = e2
        return a1 * a2, a2 * v1 + v2

    _, states = lax.associative_scan(combine, (lam_seq, bu), axis=1)
    y = jnp.einsum('blgp,gcp->blgc', states, cm).real
    return y, states[:, -1]


def s5_branch(u, lam_re, lam_im, log_step, b_re, b_im, c_re, c_im, d_skip, glu_w, glu_b, h0_fwd, h0_bwd):
    bsz, L, _ = u.shape
    uf = u.astype(jnp.float32)
    ug = uf.reshape(bsz, L, S5_GROUPS, S5_GROUP)
    y_f, t_f = s5_direction(ug, lam_re[0], lam_im[0], log_step[0], b_re[0], b_im[0], c_re[0], c_im[0], h0_fwd)
    y_b, t_b = s5_direction(ug[:, ::-1], lam_re[1], lam_im[1], log_step[1], b_re[1], b_im[1], c_re[1], c_im[1], h0_bwd)
    y = (y_f + y_b[:, ::-1]).reshape(bsz, L, S5_WIDTH) + d_skip.astype(jnp.float32) * uf
    y = jax.nn.gelu(y)
    y = y * jax.nn.sigmoid(y @ glu_w.astype(jnp.float32) + glu_b.astype(jnp.float32))
    return y.astype(u.dtype), t_f, t_b


def axial_rope_tables(row, col):
    freqs = ROPE_THETA ** (-jnp.arange(ROPE_FREQS, dtype=jnp.float32) / ROPE_FREQS)
    ang = jnp.stack([row.astype(jnp.float32)[:, None] * freqs,
                     col.astype(jnp.float32)[:, None] * freqs], axis=1)
    return jnp.cos(ang), jnp.sin(ang)


def apply_rope(x, cos, sin):
    xr = x.astype(jnp.float32).reshape(*x.shape[:-1], 2, 2, ROPE_FREQS)
    x1, x2 = xr[..., 0, :], xr[..., 1, :]
    cb, sb = cos[:, None, None], sin[:, None, None]
    out = jnp.stack([x1 * cb - x2 * sb, x1 * sb + x2 * cb], axis=-2)
    return out.reshape(x.shape).astype(x.dtype)


def diff_attend(q, k, v, lam):
    s = jnp.einsum('bqhcd,bkhcd->bhcqk', q, k).astype(jnp.float32) * (DA_HEAD_DIM ** -0.5)
    pr = jax.nn.softmax(s, axis=-1)
    a = pr[:, :, 0] - lam * pr[:, :, 1]
    return jnp.einsum('bhqk,bkhe->bqhe', a.astype(v.dtype), v)


def diffattn_branch(p_l, p_c, q_norm, k_norm, lam_vec, sub_norm, cos, sin, lam_init):
    def qkv(p):
        bsz, L, _ = p.shape
        q, k, v = jnp.split(p, [DA_QK_WIDTH, 2 * DA_QK_WIDTH], axis=-1)
        q = rmsnorm(q.reshape(bsz, L, DA_HEADS, 2, DA_HEAD_DIM), q_norm)
        k = rmsnorm(k.reshape(bsz, L, DA_HEADS, 2, DA_HEAD_DIM), k_norm)
        return q, k, v.reshape(bsz, L, DA_HEADS, DA_V_DIM)

    q_c, k_c, v_c = qkv(p_c)
    q_l, k_l, v_l = qkv(p_l)
    q_l = apply_rope(q_l, cos, sin)
    k_l = apply_rope(k_l, cos, sin)
    lf = lam_vec.astype(jnp.float32)
    lam = jnp.exp(jnp.sum(lf[0] * lf[1])) - jnp.exp(jnp.sum(lf[2] * lf[3])) + lam_init

    o_c = diff_attend(q_c, k_c, v_c, lam)
    k_all = jnp.concatenate([k_c, k_l], axis=1)
    v_all = jnp.concatenate([v_c, v_l], axis=1)
    bsz, L = q_l.shape[:2]
    qb = q_l.reshape(bsz, L // Q_BLOCK, Q_BLOCK, DA_HEADS, 2, DA_HEAD_DIM).transpose(1, 0, 2, 3, 4, 5)
    o_l = lax.map(lambda qq: diff_attend(qq, k_all, v_all, lam), qb)
    o_l = o_l.transpose(1, 0, 2, 3, 4).reshape(bsz, L, DA_HEADS, DA_V_DIM)

    def post(o):
        return (rmsnorm(o, sub_norm) * (1.0 - lam_init)).reshape(o.shape[0], o.shape[1], DA_WIDTH)

    return post(o_l), post(o_c)


def token_mixer(h_l, h_c, w_in, w_out, ssd_p, s5_p, da_p, cos, sin, lam_init):
    p_l = h_l @ w_in
    p_c = h_c @ w_in
    cuts = [IN_SSD, IN_SSD + IN_S5]
    ssd_l, s5_l, da_l = jnp.split(p_l, cuts, axis=-1)
    ssd_c, s5_c, da_c = jnp.split(p_c, cuts, axis=-1)
    bsz = p_l.shape[0]

    z_ssd = jnp.zeros((bsz, SSD_HEADS, SSD_HEAD_DIM, SSD_STATE), jnp.float32)
    y_ssd_c, sf, sb = ssd_branch(ssd_c, *ssd_p, z_ssd, z_ssd)
    y_ssd_l, _, _ = ssd_branch(ssd_l, *ssd_p, sf, sb)

    z_s5 = jnp.zeros((bsz, S5_GROUPS, S5_STATE), jnp.complex64)
    y_s5_c, tf, tb = s5_branch(s5_c, *s5_p, z_s5, z_s5)
    y_s5_l, _, _ = s5_branch(s5_l, *s5_p, tf, tb)

    y_da_l, y_da_c = diffattn_branch(da_l, da_c, *da_p, cos, sin, lam_init)

    o_l = jnp.concatenate([y_ssd_l, y_s5_l, y_da_l.astype(y_ssd_l.dtype)], axis=-1) @ w_out
    o_c = jnp.concatenate([y_ssd_c, y_s5_c, y_da_c.astype(y_ssd_c.dtype)], axis=-1) @ w_out
    return o_l, o_c


def swiglu(h, w1, w3, w2):
    return (jax.nn.silu(h @ w1) * (h @ w3)) @ w2


def setup_inputs(seed: int = 0) -> dict:
    key = jax.random.key(seed)
    ks = iter(jax.random.split(key, 48))
    f32 = jnp.float32

    def nrm(shape, scale):
        return scale * jax.random.normal(next(ks), shape, f32)

    def uni(shape, lo, hi):
        return jax.random.uniform(next(ks), shape, f32, lo, hi)

    x = nrm((BATCH, SEQ, D_MODEL), 1.0)
    c = nrm((BATCH, D_MODEL), 1.0)
    ctx = nrm((BATCH, CTX_LEN, D_MODEL), 1.0)
    c_ctx = nrm((D_MODEL,), 1.0)
    w_mod = nrm((DEPTH, D_MODEL, 6 * D_MODEL), 0.5 * D_MODEL ** -0.5)
    b_mod = nrm((DEPTH, 6 * D_MODEL), 0.02)
    norm1 = 1.0 + nrm((DEPTH, D_MODEL), 0.02)
    norm2 = 1.0 + nrm((DEPTH, D_MODEL), 0.02)
    w_in = nrm((DEPTH, D_MODEL, IN_WIDTH), D_MODEL ** -0.5)
    w_out = nrm((DEPTH, MIX_WIDTH, D_MODEL), MIX_WIDTH ** -0.5)
    ssd_conv_w = nrm((DEPTH, SSD_CONV, SSD_CONV_DIM), SSD_CONV ** -0.5)
    ssd_conv_b = nrm((DEPTH, SSD_CONV_DIM), 0.02)
    ssd_a_log = jnp.log(uni((DEPTH, 2, SSD_HEADS), 1.0, 16.0))
    dt0 = jnp.exp(uni((DEPTH, 2, SSD_HEADS), math.log(1e-3), math.log(1e-1)))
    ssd_dt_bias = dt0 + jnp.log(-jnp.expm1(-dt0))
    ssd_d = 1.0 + nrm((DEPTH, SSD_HEADS), 0.1)
    ssd_norm = 1.0 + nrm((DEPTH, SSD_INNER), 0.02)
    n_idx = jnp.arange(S5_STATE, dtype=f32)
    s5_lam_re = -0.5 + nrm((DEPTH, 2, S5_GROUPS, S5_STATE), 0.01)
    s5_lam_im = math.pi * n_idx + nrm((DEPTH, 2, S5_GROUPS, S5_STATE), 0.01)
    s5_log_step = uni((DEPTH, 2, S5_GROUPS), math.log(1e-3), math.log(1e-1))
    s5_b_re = nrm((DEPTH, 2, S5_GROUPS, S5_STATE, S5_GROUP), (2 * S5_GROUP) ** -0.5)
    s5_b_im = nrm((DEPTH, 2, S5_GROUPS, S5_STATE, S5_GROUP), (2 * S5_GROUP) ** -0.5)
    s5_c_re = nrm((DEPTH, 2, S5_GROUPS, S5_GROUP, S5_STATE), S5_STATE ** -0.5)
    s5_c_im = nrm((DEPTH, 2, S5_GROUPS, S5_GROUP, S5_STATE), S5_STATE ** -0.5)
    s5_d = nrm((DEPTH, S5_WIDTH), 1.0)
    s5_glu_w = nrm((DEPTH, S5_WIDTH, S5_WIDTH), S5_WIDTH ** -0.5)
    s5_glu_b = nrm((DEPTH, S5_WIDTH), 0.02)
    da_q_norm = 1.0 + nrm((DEPTH, DA_HEAD_DIM), 0.02)
    da_k_norm = 1.0 + nrm((DEPTH, DA_HEAD_DIM), 0.02)
    da_lambda = nrm((DEPTH, 4, DA_HEAD_DIM), 0.1)
    da_sub_norm = 1.0 + nrm((DEPTH, DA_V_DIM), 0.02)
    ffn_w1 = nrm((DEPTH, D_MODEL, D_FF), D_MODEL ** -0.5)
    ffn_w3 = nrm((DEPTH, D_MODEL, D_FF), D_MODEL ** -0.5)
    ffn_w2 = nrm((DEPTH, D_FF, D_MODEL), D_FF ** -0.5)
    return {"x": x, "c": c, "ctx": ctx, "c_ctx": c_ctx,
            "w_mod": w_mod, "b_mod": b_mod, "norm1": norm1, "norm2": norm2,
            "w_in": w_in, "w_out": w_out,
            "ssd_conv_w": ssd_conv_w, "ssd_conv_b": ssd_conv_b, "ssd_a_log": ssd_a_log,
            "ssd_dt_bias": ssd_dt_bias, "ssd_d": ssd_d, "ssd_norm": ssd_norm,
            "s5_lam_re": s5_lam_re, "s5_lam_im": s5_lam_im, "s5_log_step": s5_log_step,
            "s5_b_re": s5_b_re, "s5_b_im": s5_b_im, "s5_c_re": s5_c_re, "s5_c_im": s5_c_im,
            "s5_d": s5_d, "s5_glu_w": s5_glu_w, "s5_glu_b": s5_glu_b,
            "da_q_norm": da_q_norm, "da_k_norm": da_k_norm, "da_lambda": da_lambda,
            "da_sub_norm": da_sub_norm,
            "ffn_w1": ffn_w1, "ffn_w3": ffn_w3, "ffn_w2": ffn_w2}


def reference(x, c, ctx, c_ctx, w_mod, b_mod, norm1, norm2, w_in, w_out,
              ssd_conv_w, ssd_conv_b, ssd_a_log, ssd_dt_bias, ssd_d, ssd_norm,
              s5_lam_re, s5_lam_im, s5_log_step, s5_b_re, s5_b_im, s5_c_re, s5_c_im,
              s5_d, s5_glu_w, s5_glu_b,
              da_q_norm, da_k_norm, da_lambda, da_sub_norm,
              ffn_w1, ffn_w3, ffn_w2):
    L = x.shape[1]
    rows = L // GRID_W
    row = jnp.repeat(jnp.arange(rows), GRID_W)
    col = jnp.tile(jnp.arange(GRID_W), rows)
    cos, sin = axial_rope_tables(row, col)
    cx = ctx
    for i in range(DEPTH):
        mod_l = (jax.nn.silu(c) @ w_mod[i] + b_mod[i])[:, None, :]
        mod_c = jax.nn.silu(c_ctx) @ w_mod[i] + b_mod[i]
        sh1_l, sc1_l, g1_l, sh2_l, sc2_l, g2_l = jnp.split(mod_l, 6, axis=-1)
        sh1_c, sc1_c, g1_c, sh2_c, sc2_c, g2_c = jnp.split(mod_c, 6, axis=-1)
        lam_init = 0.8 - 0.6 * math.exp(-0.3 * i)

        h_l = rmsnorm(x, norm1[i]) * (1.0 + sc1_l) + sh1_l
        h_c = rmsnorm(cx, norm1[i]) * (1.0 + sc1_c) + sh1_c
        ssd_p = (ssd_conv_w[i], ssd_conv_b[i], ssd_a_log[i], ssd_dt_bias[i], ssd_d[i], ssd_norm[i])
        s5_p = (s5_lam_re[i], s5_lam_im[i], s5_log_step[i], s5_b_re[i], s5_b_im[i],
                s5_c_re[i], s5_c_im[i], s5_d[i], s5_glu_w[i], s5_glu_b[i])
        da_p = (da_q_norm[i], da_k_norm[i], da_lambda[i], da_sub_norm[i])
        o_l, o_c = token_mixer(h_l, h_c, w_in[i], w_out[i], ssd_p, s5_p, da_p, cos, sin, lam_init)
        x = x + g1_l * o_l
        h2_l = rmsnorm(x, norm2[i]) * (1.0 + sc2_l) + sh2_l
        x = x + g2_l * swiglu(h2_l, ffn_w1[i], ffn_w3[i], ffn_w2[i])
        if i < DEPTH - 1:
            cx = cx + g1_c * o_c
            h2_c = rmsnorm(cx, norm2[i]) * (1.0 + sc2_c) + sh2_c
            cx = cx + g2_c * swiglu(h2_c, ffn_w1[i], ffn_w3[i], ffn_w2[i])
    return x
```

```python
import functools
import math

import jax
import jax.numpy as jnp
from jax import lax
from jax.experimental import pallas as pl
from jax.experimental.pallas import tpu as pltpu

F32 = jnp.float32
BF16 = jnp.bfloat16
HI = lax.Precision.HIGHEST

EPS = 1e-6
GRID_W = 64
ROPE_THETA = 10000.0
ROPE_FREQS = 8

SSD_HEADS = 4
SSD_HEAD_DIM = 64
SSD_INNER = 256
SSD_GROUPS = 2
SSD_STATE = 64
SSD_CHUNK = 128
SSD_CONV_DIM = 512

S5_WIDTH = 256
S5_GROUP = 16
S5_GROUPS = 16
S5_STATE = 64
S5_CHUNK = 16
S5_PAIRS = S5_GROUPS // 2
S5_SCAN_TILE = 16

DA_HEADS = 8
DA_HEAD_DIM = 32
DA_V_DIM = 64
DA_WIDTH = 512

LANES = 128
ROW_TILE = 256
IN_PAD = 2688
VMEM_LIMIT = 56 * 1024 * 1024


def _cparams(*sem):
    return pltpu.CompilerParams(dimension_semantics=sem, vmem_limit_bytes=VMEM_LIMIT)


def _dot(a, b, precision=None):
    return jnp.dot(a, b, preferred_element_type=F32, precision=precision)


def _dot_nt(a, b, precision=None):
    return lax.dot_general(a, b, (((1,), (1,)), ((), ())), preferred_element_type=F32,
                           precision=precision)


def _dot_tn(a, b, precision=None):
    return lax.dot_general(a, b, (((0,), (0,)), ((), ())), preferred_element_type=F32,
                           precision=precision)


def _silu(x):
    return x * jax.nn.sigmoid(x)


def _mod_body(cv_ref, w_ref, b_ref, o_ref):
    o_ref[0] = _dot(_silu(cv_ref[...]), w_ref[0], HI) + b_ref[0]


def _mod_call(cv, w_mod, b_mod):
    depth, d, n = w_mod.shape
    tn = 1024
    return pl.pallas_call(
        _mod_body,
        grid=(depth, n // tn),
        in_specs=[pl.BlockSpec((8, d), lambda l, j: (0, 0)),
                  pl.BlockSpec((1, d, tn), lambda l, j: (l, 0, j)),
                  pl.BlockSpec((1, 1, tn), lambda l, j: (l, 0, j))],
        out_specs=pl.BlockSpec((1, 8, tn), lambda l, j: (l, 0, j)),
        out_shape=jax.ShapeDtypeStruct((depth, 8, n), F32),
        compiler_params=_cparams("arbitrary", "arbitrary"),
        name="adaln_mod",
    )(cv, w_mod, b_mod.reshape(depth, 1, n))


def _inproj_body(x_ref, g_ref, mod_ref, w_ref, gq_ref, gk_ref, gm_ref, cos_ref, sa_ref, sb_ref,
                 z_ref, xbc_ref, u_ref, dt_ref, q_ref, k_ref, v_ref):
    x = x_ref[...]
    mod = mod_ref[0]
    ms = jnp.mean(x * x, axis=-1, keepdims=True)
    h = x * lax.rsqrt(ms + EPS) * g_ref[...] * (1.0 + mod[1:2]) + mod[0:1]
    p = _dot(h.astype(BF16), w_ref[...])
    z_ref[...] = p[:, 0:256]
    xbc_ref[...] = p[:, 256:768]
    u_ref[...] = p[:, 768:1024]
    dt_ref[...] = p[:, 2560:2688]
    v_ref[...] = p[:, 2048:2560].astype(BF16)

    gm = gm_ref[...]
    cos = cos_ref[...]
    sa = sa_ref[...]
    sb = sb_ref[...]

    def norm_rope(t, gain, scale):
        t2 = t * t
        hi = t2.astype(BF16)
        lo = (t2 - hi.astype(F32)).astype(BF16)
        ms32 = _dot(hi, gm) + _dot(lo, gm)
        tn = t * lax.rsqrt(ms32 + EPS) * gain
        outs = []
        for j in range(DA_WIDTH // LANES):
            tb = tn[:, j * LANES:(j + 1) * LANES]
            ob = tb * cos + pltpu.roll(tb, LANES - ROPE_FREQS, 1) * sa + pltpu.roll(tb, ROPE_FREQS, 1) * sb
            outs.append(ob * scale)
        return jnp.concatenate(outs, axis=1).astype(BF16)

    q_ref[...] = norm_rope(p[:, 1024:1536], gq_ref[...], DA_HEAD_DIM ** -0.5)
    k_ref[...] = norm_rope(p[:, 1536:2048], gk_ref[...], 1.0)


def _inproj_call(xs, g1, modl, w_r, gq, gk, gmat, cos_t, sa_t, sb_t, n_ctx_tiles):
    t, d = xs.shape
    tm = ROW_TILE
    row = lambda i: (i, 0)
    const = lambda i: (0, 0)
    seg = lambda i: ((i >= n_ctx_tiles).astype(jnp.int32), 0, 0)
    outs = [(256, F32), (512, F32), (256, F32), (LANES, F32), (512, BF16), (512, BF16), (512, BF16)]
    return pl.pallas_call(
        _inproj_body,
        grid=(t // tm,),
        in_specs=[pl.BlockSpec((tm, d), row),
                  pl.BlockSpec((1, d), const),
                  pl.BlockSpec((1, 8, d), seg),
                  pl.BlockSpec((d, IN_PAD), const),
                  pl.BlockSpec((1, DA_WIDTH), const),
                  pl.BlockSpec((1, DA_WIDTH), const),
                  pl.BlockSpec((DA_WIDTH, DA_WIDTH), const),
                  pl.BlockSpec((tm, LANES), row),
                  pl.BlockSpec((tm, LANES), row),
                  pl.BlockSpec((tm, LANES), row)],
        out_specs=[pl.BlockSpec((tm, w), row) for w, _ in outs],
        out_shape=[jax.ShapeDtypeStruct((t, w), dt) for w, dt in outs],
        compiler_params=_cparams("arbitrary"),
        name="in_proj",
    )(xs, g1, modl, w_r, gq, gk, gmat, cos_t, sa_t, sb_t)


def _softplus(x):
    return jnp.maximum(x, 0.0) + jnp.log1p(jnp.exp(-jnp.abs(x)))


def _ssd_chunk_index(i, n_ctx, n_chunks, reverse):
    if not reverse:
        return i
    return jnp.where(i < n_ctx, n_ctx - 1 - i, n_chunks - 1 - (i - n_ctx))


def _ssd_body(n_ctx, n_chunks, reverse, xc_ref, xp_ref, xn_ref, dtc_ref, dtr_ref, cw_ref, cb_ref,
              bias_c_ref, a_c_ref, bias_r_ref, a_r_ref, dsk_ref, y_ref, st_ref):
    i = pl.program_id(0)
    c = _ssd_chunk_index(i, n_ctx, n_chunks, reverse)
    L = SSD_CHUNK

    @pl.when(i == 0)
    def _():
        st_ref[...] = jnp.zeros_like(st_ref)

    x = xc_ref[...]
    seg_first = jnp.logical_or(c == 0, c == n_ctx)
    seg_last = jnp.logical_or(c == n_ctx - 1, c == n_chunks - 1)
    prow = jnp.where(seg_first, 0.0, xp_ref[7:8, :])
    nrow = jnp.where(seg_last, 0.0, xn_ref[0:1, :])
    ridx = lax.broadcasted_iota(jnp.int32, (L, 1), 0)
    xprev = jnp.where(ridx == 0, prow, pltpu.roll(x, 1, 0))
    xnext = jnp.where(ridx == L - 1, nrow, pltpu.roll(x, L - 1, 0))
    cw = cw_ref[...]
    conv = xprev * cw[0:1] + x * cw[1:2] + xnext * cw[2:3] + cb_ref[...]
    xbc = _silu(conv)

    off = SSD_HEADS if reverse else 0
    dt_c = _softplus(dtc_ref[...] + bias_c_ref[...])
    dt_r = _softplus(dtr_ref[...] + bias_r_ref[...])
    adt_c = dt_c * a_c_ref[...]
    adt_r = dt_r * a_r_ref[...]

    li = lax.broadcasted_iota(jnp.int32, (L, L), 0)
    si = lax.broadcasted_iota(jnp.int32, (L, L), 1)
    mask = (li <= si) if reverse else (li >= si)
    mask_t = (li >= si) if reverse else (li <= si)
    cs_c = _dot(mask.astype(F32), adt_c, HI)
    cs_r = _dot(adt_r, mask_t.astype(F32), HI)
    edge = 0 if reverse else L - 1

    ys = []
    for g in range(SSD_GROUPS):
        bg = xbc[:, SSD_INNER + g * SSD_STATE:SSD_INNER + (g + 1) * SSD_STATE]
        cg = xbc[:, SSD_INNER + (SSD_GROUPS + g) * SSD_STATE:SSD_INNER + (SSD_GROUPS + g + 1) * SSD_STATE]
        gmat = _dot_nt(cg, bg, HI)
        for hh in range(SSD_HEADS // SSD_GROUPS):
            h = g * (SSD_HEADS // SSD_GROUPS) + hh
            col = off + h
            csc = cs_c[:, col:col + 1]
            csr = cs_r[col:col + 1, :]
            dec = jnp.exp(jnp.where(mask, csc - csr, -jnp.inf))
            xh = xbc[:, h * SSD_HEAD_DIM:(h + 1) * SSD_HEAD_DIM]
            xd = xh * dt_c[:, col:col + 1]
            st = st_ref[h]
            y = _dot(gmat * dec, xd, HI) + jnp.exp(csc) * _dot_nt(cg, st, HI)
            tot = cs_c[edge:edge + 1, col:col + 1]
            st_ref[h] = st * jnp.exp(tot) + _dot_tn(xd * jnp.exp(tot - csc), bg, HI)
            ys.append(y)
    y = jnp.concatenate(ys, axis=1)
    if not reverse:
        y = y + dsk_ref[...] * xbc[:, :SSD_INNER]
    y_ref[...] = y


def _ssd_call(xbc_raw, dt_c, dt_r, cw, cb, bias_c, a_c, bias_r, a_r, dsk, n_ctx, reverse):
    t = xbc_raw.shape[0]
    L = SSD_CHUNK
    n_chunks = t // L
    sub = L // 8
    n_sub = t // 8
    cidx = functools.partial(_ssd_chunk_index, n_ctx=n_ctx, n_chunks=n_chunks, reverse=reverse)
    const = lambda i: (0, 0)
    return pl.pallas_call(
        functools.partial(_ssd_body, n_ctx, n_chunks, reverse),
        grid=(n_chunks,),
        in_specs=[pl.BlockSpec((L, SSD_CONV_DIM), lambda i: (cidx(i), 0)),
                  pl.BlockSpec((8, SSD_CONV_DIM), lambda i: (jnp.maximum(cidx(i) * sub - 1, 0), 0)),
                  pl.BlockSpec((8, SSD_CONV_DIM), lambda i: (jnp.minimum((cidx(i) + 1) * sub, n_sub - 1), 0)),
                  pl.BlockSpec((L, LANES), lambda i: (cidx(i), 0)),
                  pl.BlockSpec((8, L), lambda i: (0, cidx(i))),
                  pl.BlockSpec((8, SSD_CONV_DIM), const),
                  pl.BlockSpec((1, SSD_CONV_DIM), const),
                  pl.BlockSpec((1, LANES), const),
                  pl.BlockSpec((1, LANES), const),
                  pl.BlockSpec((8, L), const),
                  pl.BlockSpec((8, L), const),
                  pl.BlockSpec((1, SSD_INNER), const)],
        out_specs=pl.BlockSpec((L, SSD_INNER), lambda i: (cidx(i), 0)),
        out_shape=jax.ShapeDtypeStruct((t, SSD_INNER), F32),
        scratch_shapes=[pltpu.VMEM((SSD_HEADS, SSD_HEAD_DIM, SSD_STATE), F32)],
        compiler_params=_cparams("arbitrary"),
        name="ssd_bwd" if reverse else "ssd_fwd",
    )(xbc_raw, xbc_raw, xbc_raw, dt_c, dt_r, cw, cb, bias_c, a_c, bias_r, a_r, dsk)


def _s5_in_body(u_ref, w_ref, efr_ref, efi_ref, ebr_ref, ebi_ref):
    e = _dot(u_ref[0], w_ref[0], HI)
    efr_ref[...] = e[:, 0:128]
    efi_ref[...] = e[:, 128:256]
    ebr_ref[...] = e[:, 256:384]
    ebi_ref[...] = e[:, 384:512]


def _s5_in_call(u_pair, w_in):
    npair, nch, width = u_pair.shape
    col = lambda r: (0, r)
    return pl.pallas_call(
        _s5_in_body,
        grid=(npair,),
        in_specs=[pl.BlockSpec((1, nch, width), lambda r: (r, 0, 0)),
                  pl.BlockSpec((1, width, width), lambda r: (r, 0, 0))],
        out_specs=[pl.BlockSpec((nch, LANES), col)] * 4,
        out_shape=[jax.ShapeDtypeStruct((nch, npair * LANES), F32)] * 4,
        compiler_params=_cparams("arbitrary"),
        name="s5_chunk_in",
    )(u_pair, w_in)


def _s5_tile_index(i, n_tiles, reverse):
    if not reverse:
        return i
    return jnp.where(i == 0, 0, n_tiles - i)


def _s5_scan_body(a_ref, efr_ref, efi_ref, ebr_ref, ebi_ref, hfr_ref, hfi_ref, hbr_ref, hbi_ref, st_ref):
    @pl.when(pl.program_id(0) == 0)
    def _():
        st_ref[...] = jnp.zeros_like(st_ref)

    afr, afi, abr, abi = a_ref[0], a_ref[1], a_ref[2], a_ref[3]
    fr, fi, br, bi = st_ref[0], st_ref[1], st_ref[2], st_ref[3]
    for j in range(S5_SCAN_TILE):
        hfr_ref[j] = fr
        hfi_ref[j] = fi
        fr, fi = afr * fr - afi * fi + efr_ref[j], afr * fi + afi * fr + efi_ref[j]
        jb = S5_SCAN_TILE - 1 - j
        hbr_ref[jb] = br
        hbi_ref[jb] = bi
        br, bi = abr * br - abi * bi + ebr_ref[jb], abr * bi + abi * br + ebi_ref[jb]
    st_ref[0] = fr
    st_ref[1] = fi
    st_ref[2] = br
    st_ref[3] = bi


def _s5_scan_call(a_pow, efr, efi, ebr, ebi):
    nch, npair, _ = efr.shape
    n_tiles = nch // S5_SCAN_TILE
    blk = (S5_SCAN_TILE, npair, LANES)
    fwd = lambda i: (i, 0, 0)
    bwd = lambda i: (_s5_tile_index(i, n_tiles, True), 0, 0)
    shp = jax.ShapeDtypeStruct((nch, npair, LANES), F32)
    return pl.pallas_call(
        _s5_scan_body,
        grid=(n_tiles,),
        in_specs=[pl.BlockSpec((4, npair, LANES), lambda i: (0, 0, 0)),
                  pl.BlockSpec(blk, fwd), pl.BlockSpec(blk, fwd),
                  pl.BlockSpec(blk, bwd), pl.BlockSpec(blk, bwd)],
        out_specs=[pl.BlockSpec(blk, fwd), pl.BlockSpec(blk, fwd),
                   pl.BlockSpec(blk, bwd), pl.BlockSpec(blk, bwd)],
        out_shape=[shp] * 4,
        scratch_shapes=[pltpu.VMEM((4, npair, LANES), F32)],
        compiler_params=_cparams("arbitrary"),
        name="s5_state_scan",
    )(a_pow, efr, efi, ebr, ebi)


def _s5_out_body(u_ref, hfr_ref, hfi_ref, hbr_ref, hbi_ref, m_ref, q_ref, y_ref):
    y = _dot(u_ref[0], m_ref[0], HI)
    y += _dot(hfr_ref[...], q_ref[0, 0], HI)
    y += _dot(hfi_ref[...], q_ref[0, 1], HI)
    y += _dot(hbr_ref[...], q_ref[0, 2], HI)
    y += _dot(hbi_ref[...], q_ref[0, 3], HI)
    y_ref[0] = y


def _s5_out_call(u_pair, hfr, hfi, hbr, hbi, w_m, w_q):
    npair, nch, width = u_pair.shape
    col = lambda r: (0, r)
    return pl.pallas_call(
        _s5_out_body,
        grid=(npair,),
        in_specs=[pl.BlockSpec((1, nch, width), lambda r: (r, 0, 0))]
                 + [pl.BlockSpec((nch, LANES), col)] * 4
                 + [pl.BlockSpec((1, width, width), lambda r: (r, 0, 0)),
                    pl.BlockSpec((1, 4, LANES, width), lambda r: (r, 0, 0, 0))],
        out_specs=pl.BlockSpec((1, nch, width), lambda r: (r, 0, 0)),
        out_shape=jax.ShapeDtypeStruct((npair, nch, width), F32),
        compiler_params=_cparams("arbitrary"),
        name="s5_chunk_out",
    )(u_pair, hfr, hfi, hbr, hbi, w_m, w_q)


def _s5_weights(lam_re, lam_im, log_step, b_re, b_im, c_re, c_im):
    tc = S5_CHUNK
    taus = jnp.arange(tc + 1, dtype=F32)
    eye2 = jnp.eye(2, dtype=F32)
    ks, ps, qs, a_pows = [], [], [], []
    for d in range(2):
        lam = lax.complex(lam_re[d], lam_im[d])
        step = jnp.exp(log_step[d])[:, None]
        lam_bar = jnp.exp(lam * step)
        b_bar = ((lam_bar - 1.0) / lam)[..., None] * lax.complex(b_re[d], b_im[d])
        cm = lax.complex(c_re[d], c_im[d])
        lam_pow = jnp.exp((lam * step)[None] * taus[:, None, None])
        ks.append(jnp.einsum('gcp,tgp,gpk->tgck', cm, lam_pow[:tc], b_bar).real)
        p_pow = lam_pow[:tc][::-1] if d == 0 else lam_pow[:tc]
        ps.append(jnp.einsum('igp,gpk->gikp', p_pow, b_bar))
        q_pow = lam_pow[1:] if d == 0 else lam_pow[1:][::-1]
        qs.append(jnp.einsum('gcp,jgp->gpjc', cm, q_pow))
        a_pows.append(lam_pow[tc])

    ii = jnp.arange(tc)[:, None]
    jj = jnp.arange(tc)[None, :]
    kf = jnp.where((jj >= ii)[:, :, None, None, None], ks[0][jnp.clip(jj - ii, 0)], 0.0)
    kb = jnp.where((ii >= jj)[:, :, None, None, None], ks[1][jnp.clip(ii - jj, 0)], 0.0)
    m = jnp.transpose(kf + kb, (2, 0, 4, 1, 3))

    def pair_rows_cols(w):
        w = w.reshape(S5_PAIRS, 2, tc, S5_GROUP, tc, S5_GROUP)
        w = jnp.einsum('rgicjd,gh->rigcjhd', w, eye2)
        return w.reshape(S5_PAIRS, tc * 2 * S5_GROUP, tc * 2 * S5_GROUP)

    def pair_rows(w):
        w = w.reshape(S5_PAIRS, 2, tc, S5_GROUP, S5_STATE)
        w = jnp.einsum('rgicp,gh->rigchp', w, eye2)
        return w.reshape(S5_PAIRS, tc * 2 * S5_GROUP, 2 * S5_STATE)

    def pair_cols(w):
        w = w.reshape(S5_PAIRS, 2, S5_STATE, tc, S5_GROUP)
        w = jnp.einsum('rgpjc,gh->rgpjhc', w, eye2)
        return w.reshape(S5_PAIRS, 2 * S5_STATE, tc * 2 * S5_GROUP)

    w_m = pair_rows_cols(m)
    w_in = jnp.concatenate([pair_rows(ps[0].real), pair_rows(ps[0].imag),
                            pair_rows(ps[1].real), pair_rows(ps[1].imag)], axis=-1)
    w_q = jnp.stack([pair_cols(qs[0].real), pair_cols(-qs[0].imag),
                     pair_cols(qs[1].real), pair_cols(-qs[1].imag)], axis=1)
    lanes = lambda a: a.reshape(S5_PAIRS, 2 * S5_STATE)
    a_pow = jnp.stack([lanes(a_pows[0].real), lanes(a_pows[0].imag),
                       lanes(a_pows[1].real), lanes(a_pows[1].imag)], axis=0)
    return w_in.astype(F32), w_m.astype(F32), w_q.astype(F32), a_pow.astype(F32)


def _s5_mix(u, weights):
    w_in, w_m, w_q, a_pow = weights
    t = u.shape[0]
    nch = t // S5_CHUNK
    u_pair = u.reshape(nch, S5_CHUNK, S5_PAIRS, 2 * S5_GROUP).transpose(2, 0, 1, 3)
    u_pair = u_pair.reshape(S5_PAIRS, nch, S5_CHUNK * 2 * S5_GROUP)
    es = _s5_in_call(u_pair, w_in)
    es = [e.reshape(nch, S5_PAIRS, LANES) for e in es]
    hs = _s5_scan_call(a_pow, *es)
    hs = [h.reshape(nch, S5_PAIRS * LANES) for h in hs]
    y_pair = _s5_out_call(u_pair, *hs, w_m, w_q)
    y = y_pair.reshape(S5_PAIRS, nch, S5_CHUNK, 2 * S5_GROUP).transpose(1, 2, 0, 3)
    return y.reshape(t, S5_WIDTH)


def _attn_body(lam_ref, q_ref, k_ref, v_ref, g_ref, o_ref, m_ref, acc_ref, *, post_scale):
    j = pl.program_id(2)

    @pl.when(j == 0)
    def _():
        m_ref[...] = jnp.full_like(m_ref, -jnp.inf)
        acc_ref[...] = jnp.zeros_like(acc_ref)

    q = q_ref[0]
    kt = k_ref[0]
    v = v_ref[0]
    for c in range(2):
        s = _dot(q[:, c * DA_HEAD_DIM:(c + 1) * DA_HEAD_DIM], kt[c * DA_HEAD_DIM:(c + 1) * DA_HEAD_DIM, :])
        m_old = m_ref[c]
        m_new = jnp.maximum(m_old, jnp.max(s, axis=1, keepdims=True))
        p = jnp.exp(s - m_new[:, 0:1])
        acc_ref[c] = jnp.exp(m_old - m_new) * acc_ref[c] + _dot(p.astype(BF16), v)
        m_ref[c] = m_new

    @pl.when(j == pl.num_programs(2) - 1)
    def _():
        a0 = acc_ref[0]
        a1 = acc_ref[1]
        o = a0[:, :DA_V_DIM] / a0[:, DA_V_DIM:DA_V_DIM + 1] \
            - lam_ref[0] * (a1[:, :DA_V_DIM] / a1[:, DA_V_DIM:DA_V_DIM + 1])
        ms = jnp.mean(o * o, axis=-1, keepdims=True)
        o_ref[0] = o * lax.rsqrt(ms + EPS) * g_ref[...] * post_scale


def _pick_tile(n, candidates):
    for c in candidates:
        if n % c == 0:
            return c
    return n


def _attn_call(lam, qh, kth, vh, g_sub, post_scale):
    nh, tq_all, _ = qh.shape
    tk_all = kth.shape[2]
    tq = _pick_tile(tq_all, (512, 256))
    tk = _pick_tile(tk_all, (1280, 1024, 640, 512, 256))
    return pl.pallas_call(
        functools.partial(_attn_body, post_scale=post_scale),
        grid=(nh, tq_all // tq, tk_all // tk),
        in_specs=[pl.BlockSpec(memory_space=pltpu.SMEM),
                  pl.BlockSpec((1, tq, 2 * DA_HEAD_DIM), lambda h, i, j: (h, i, 0)),
                  pl.BlockSpec((1, 2 * DA_HEAD_DIM, tk), lambda h, i, j: (h, 0, j)),
                  pl.BlockSpec((1, tk, LANES), lambda h, i, j: (h, j, 0)),
                  pl.BlockSpec((1, DA_V_DIM), lambda h, i, j: (0, 0))],
        out_specs=pl.BlockSpec((1, tq, DA_V_DIM), lambda h, i, j: (h, i, 0)),
        out_shape=jax.ShapeDtypeStruct((nh, tq_all, DA_V_DIM), F32),
        scratch_shapes=[pltpu.VMEM((2, tq, LANES), F32), pltpu.VMEM((2, tq, LANES), F32)],
        compiler_params=_cparams("arbitrary", "arbitrary", "arbitrary"),
        name="diff_attn",
    )(lam, qh, kth, vh, g_sub)


def _gelu_tanh(x):
    return 0.5 * x * (1.0 + jnp.tanh(math.sqrt(2.0 / math.pi) * (x + 0.044715 * (x * x * x))))


def _outffn_body(x_ref, yf_ref, yb_ref, z_ref, gssd_ref, ys5_ref, u_ref, d5_ref, gw_ref, gb_ref, yda_ref,
                 mod_ref, g2_ref, wo_ref, w1_ref, w3_ref, w2_ref, o_ref):
    mod = mod_ref[0]
    y = (yf_ref[...] + yb_ref[...]) * _silu(z_ref[...])
    y_ssd = y * lax.rsqrt(jnp.mean(y * y, axis=-1, keepdims=True) + EPS) * gssd_ref[...]
    y5 = _gelu_tanh(ys5_ref[...] + d5_ref[...] * u_ref[...])
    y5 = y5 * jax.nn.sigmoid(_dot(y5, gw_ref[...], HI) + gb_ref[...])
    wo = wo_ref[...]
    o = _dot(y_ssd.astype(BF16), wo[0:256])
    o += _dot(y5.astype(BF16), wo[256:512])
    o += _dot(yda_ref[...].astype(BF16), wo[512:1024])
    x = x_ref[...] + mod[2:3] * o
    h = x * lax.rsqrt(jnp.mean(x * x, axis=-1, keepdims=True) + EPS) * g2_ref[...] * (1.0 + mod[4:5]) + mod[3:4]
    hb = h.astype(BF16)
    f = _silu(_dot(hb, w1_ref[...])) * _dot(hb, w3_ref[...])
    o_ref[...] = x + mod[5:6] * _dot(f.astype(BF16), w2_ref[...])


def _outffn_call(xs, y_f, y_b, z, g_ssd, y_s5, u, d5, glu_w, glu_b, y_da, modl, g2, w_o, w1, w3, w2,
                 n_ctx_tiles):
    t, d = xs.shape
    dff = w1.shape[1]
    tm = ROW_TILE
    row = lambda i: (i, 0)
    const = lambda i: (0, 0)
    seg = lambda i: ((i >= n_ctx_tiles).astype(jnp.int32), 0, 0)
    one = pl.Buffered(1)
    return pl.pallas_call(
        _outffn_body,
        grid=(t // tm,),
        in_specs=[pl.BlockSpec((tm, d), row),
                  pl.BlockSpec((tm, 256), row), pl.BlockSpec((tm, 256), row), pl.BlockSpec((tm, 256), row),
                  pl.BlockSpec((1, 256), const),
                  pl.BlockSpec((tm, 256), row), pl.BlockSpec((tm, 256), row),
                  pl.BlockSpec((1, 256), const),
                  pl.BlockSpec((256, 256), const), pl.BlockSpec((1, 256), const),
                  pl.BlockSpec((tm, 512), row),
                  pl.BlockSpec((1, 8, d), seg),
                  pl.BlockSpec((1, d), const),
                  pl.BlockSpec((d, d), const, pipeline_mode=one),
                  pl.BlockSpec((d, dff), const, pipeline_mode=one),
                  pl.BlockSpec((d, dff), const, pipeline_mode=one),
                  pl.BlockSpec((dff, d), const, pipeline_mode=one)],
        out_specs=pl.BlockSpec((tm, d), row),
        out_shape=jax.ShapeDtypeStruct((t, d), F32),
        compiler_params=_cparams("arbitrary"),
        name="out_proj_ffn",
    )(xs, y_f, y_b, z, g_ssd, y_s5, u, d5, glu_w, glu_b, y_da, modl, g2, w_o, w1, w3, w2)


def _rope_tables(n_ctx, n_lat):
    pos = jnp.arange(n_lat)
    lane = jnp.arange(LANES)
    d = lane % DA_HEAD_DIM
    axis = d // (2 * ROPE_FREQS)
    half = (d % (2 * ROPE_FREQS)) // ROPE_FREQS
    freqs = ROPE_THETA ** (-jnp.arange(ROPE_FREQS, dtype=F32) / ROPE_FREQS)
    coord = jnp.where(axis[None, :] == 0, (pos // GRID_W)[:, None], (pos % GRID_W)[:, None]).astype(F32)
    ang = coord * freqs[d % ROPE_FREQS][None, :]
    cos, sin = jnp.cos(ang), jnp.sin(ang)
    sa = jnp.where(half[None, :] == 0, -sin, 0.0)
    sb = jnp.where(half[None, :] == 1, sin, 0.0)
    pad = lambda a, v: jnp.concatenate([jnp.full((n_ctx, LANES), v, F32), a], axis=0)
    return pad(cos, 1.0), pad(sa, 0.0), pad(sb, 0.0)


def _lane_row(vals, width=LANES):
    vals = vals.reshape(-1).astype(F32)
    return jnp.zeros((1, width), F32).at[0, :vals.shape[0]].set(vals)


def kernel(x, c, ctx, c_ctx, w_mod, b_mod, norm1, norm2, w_in, w_out, ssd_conv_w, ssd_conv_b, ssd_a_log, ssd_dt_bias, ssd_d, ssd_norm, s5_lam_re, s5_lam_im, s5_log_step, s5_b_re, s5_b_im, s5_c_re, s5_c_im, s5_d, s5_glu_w, s5_glu_b, da_q_norm, da_k_norm, da_lambda, da_sub_norm, ffn_w1, ffn_w3, ffn_w2):
    depth = w_mod.shape[0]
    bsz, n_lat, d = x.shape
    n_ctx = ctx.shape[1]
    assert bsz == 1 and n_ctx % ROW_TILE == 0 and n_lat % ROW_TILE == 0
    assert n_ctx % (S5_CHUNK * S5_SCAN_TILE) == 0 and n_lat % (S5_CHUNK * S5_SCAN_TILE) == 0
    t = n_ctx + n_lat
    n_ctx_tiles = n_ctx // ROW_TILE

    xs = jnp.concatenate([ctx[0], x[0]], axis=0)
    cv = jnp.zeros((8, d), F32).at[0].set(c_ctx).at[1].set(c[0])
    mod_all = _mod_call(cv, w_mod, b_mod)
    cos_t, sa_t, sb_t = _rope_tables(n_ctx, n_lat)
    gidx = jnp.arange(DA_WIDTH) // DA_HEAD_DIM
    gmat = (gidx[:, None] == gidx[None, :]).astype(BF16) * (1.0 / DA_HEAD_DIM)
    ones_col = jnp.zeros((DA_HEADS, t, LANES - DA_V_DIM), BF16).at[:, :, 0].set(1.0)

    for i in range(depth):
        lam_init = 0.8 - 0.6 * math.exp(-0.3 * i)
        modl = jnp.zeros((2, 8, d), F32).at[:, :6].set(mod_all[i, :2].reshape(2, 6, d))

        wi = w_in[i]
        w_r = jnp.concatenate([wi[:, 0:768], wi[:, 776:1032], wi[:, 1032:2568], wi[:, 768:776],
                               jnp.zeros((d, IN_PAD - 2568), F32)], axis=1).astype(BF16)
        gq = jnp.tile(da_q_norm[i], DA_WIDTH // DA_HEAD_DIM)[None, :]
        gk = jnp.tile(da_k_norm[i], DA_WIDTH // DA_HEAD_DIM)[None, :]
        z, xbc_raw, u, dt_raw, q, k, v = _inproj_call(
            xs, norm1[i][None, :], modl, w_r, gq, gk, gmat, cos_t, sa_t, sb_t, n_ctx_tiles)

        dt_r = dt_raw[:, :8].T
        cw = jnp.zeros((8, SSD_CONV_DIM), F32).at[:3].set(ssd_conv_w[i])
        a_neg = -jnp.exp(ssd_a_log[i])
        bias_c = _lane_row(ssd_dt_bias[i])
        a_c = _lane_row(a_neg)
        bias_r = jnp.broadcast_to(ssd_dt_bias[i].reshape(8, 1), (8, SSD_CHUNK))
        a_r = jnp.broadcast_to(a_neg.reshape(8, 1), (8, SSD_CHUNK))
        dsk = jnp.repeat(ssd_d[i], SSD_HEAD_DIM)[None, :]
        ssd_args = (xbc_raw, dt_raw, dt_r, cw, ssd_conv_b[i][None, :], bias_c, a_c, bias_r, a_r, dsk,
                    n_ctx // SSD_CHUNK)
        y_f = _ssd_call(*ssd_args, reverse=False)
        y_b = _ssd_call(*ssd_args, reverse=True)

        y_s5 = _s5_mix(u, _s5_weights(s5_lam_re[i], s5_lam_im[i], s5_log_step[i], s5_b_re[i], s5_b_im[i],
                                      s5_c_re[i], s5_c_im[i]))

        lf = da_lambda[i]
        lam = (jnp.exp(jnp.sum(lf[0] * lf[1])) - jnp.exp(jnp.sum(lf[2] * lf[3])) + lam_init).reshape(1)
        qh = q.reshape(t, DA_HEADS, 2 * DA_HEAD_DIM).transpose(1, 0, 2)
        kth = k.reshape(t, DA_HEADS, 2 * DA_HEAD_DIM).transpose(1, 2, 0)
        vh = jnp.concatenate([v.reshape(t, DA_HEADS, DA_V_DIM).transpose(1, 0, 2), ones_col], axis=2)
        g_sub = da_sub_norm[i][None, :]
        o_c = _attn_call(lam, qh[:, :n_ctx], kth[:, :, :n_ctx], vh[:, :n_ctx], g_sub, 1.0 - lam_init)
        o_l = _attn_call(lam, qh[:, n_ctx:], kth, vh, g_sub, 1.0 - lam_init)
        y_da = jnp.concatenate([o_c, o_l], axis=1).transpose(1, 0, 2).reshape(t, DA_WIDTH)

        xs = _outffn_call(xs, y_f, y_b, z, ssd_norm[i][None, :], y_s5, u, s5_d[i][None, :], s5_glu_w[i],
                          s5_glu_b[i][None, :], y_da, modl, norm2[i][None, :], w_out[i].astype(BF16),
                          ffn_w1[i].astype(BF16), ffn_w3[i].astype(BF16), ffn_w2[i].astype(BF16),
                          n_ctx_tiles)
    return xs[n_ctx:][None]
```

```python
import functools
import math

import jax
import jax.numpy as jnp
from jax import lax
from jax.experimental import pallas as pl
from jax.experimental.pallas import tpu as pltpu

F32 = jnp.float32
BF16 = jnp.bfloat16
HI = lax.Precision.HIGHEST

EPS = 1e-6
GRID_W = 64
ROPE_THETA = 10000.0
ROPE_FREQS = 8

SSD_HEADS = 4
SSD_HEAD_DIM = 64
SSD_INNER = 256
SSD_GROUPS = 2
SSD_STATE = 64
SSD_CHUNK = 128
SSD_CONV_DIM = 512

S5_WIDTH = 256
S5_GROUP = 16
S5_GROUPS = 16
S5_STATE = 64
S5_CHUNK = 16
S5_PAIRS = S5_GROUPS // 2
S5_SCAN_TILE = 16

DA_HEADS = 8
DA_HEAD_DIM = 32
DA_V_DIM = 64
DA_WIDTH = 512
ATTN_TQ = 512
ATTN_TK = 1280
LOG2E = 1.4426950408889634
SCORE_LOG2_LIMIT = 100.0

LANES = 128
ROW_TILE = 256
IN_PAD = 2688
VMEM_LIMIT = 56 * 1024 * 1024


def _cparams(*sem):
    return pltpu.CompilerParams(dimension_semantics=sem, vmem_limit_bytes=VMEM_LIMIT)


def _dot(a, b, precision=None):
    return jnp.dot(a, b, preferred_element_type=F32, precision=precision)


def _dot_nt(a, b, precision=None):
    return lax.dot_general(a, b, (((1,), (1,)), ((), ())), preferred_element_type=F32,
                           precision=precision)


def _dot_tn(a, b, precision=None):
    return lax.dot_general(a, b, (((0,), (0,)), ((), ())), preferred_element_type=F32,
                           precision=precision)


def _silu(x):
    return x * jax.nn.sigmoid(x)


def _mod_body(cv_ref, w_ref, b_ref, o_ref):
    o_ref[0] = _dot(_silu(cv_ref[...]), w_ref[0], HI) + b_ref[0]


def _mod_call(cv, w_mod, b_mod):
    depth, d, n = w_mod.shape
    tn = 1024
    return pl.pallas_call(
        _mod_body,
        grid=(depth, n // tn),
        in_specs=[pl.BlockSpec((8, d), lambda l, j: (0, 0)),
                  pl.BlockSpec((1, d, tn), lambda l, j: (l, 0, j)),
                  pl.BlockSpec((1, 1, tn), lambda l, j: (l, 0, j))],
        out_specs=pl.BlockSpec((1, 8, tn), lambda l, j: (l, 0, j)),
        out_shape=jax.ShapeDtypeStruct((depth, 8, n), F32),
        compiler_params=_cparams("arbitrary", "arbitrary"),
        name="adaln_mod",
    )(cv, w_mod, b_mod.reshape(depth, 1, n))


def _inproj_body(x_ref, g_ref, mod_ref, w_ref, gq_ref, gk_ref, gm_ref, cos_ref, sa_ref, sb_ref,
                 z_ref, xbc_ref, u_ref, dt_ref, q_ref, k_ref, v_ref):
    x = x_ref[...]
    mod = mod_ref[0]
    ms = jnp.mean(x * x, axis=-1, keepdims=True)
    h = x * lax.rsqrt(ms + EPS) * g_ref[...] * (1.0 + mod[1:2]) + mod[0:1]
    p = _dot(h.astype(BF16), w_ref[...])
    z_ref[...] = p[:, 0:256]
    xbc_ref[...] = p[:, 256:768]
    u_ref[...] = p[:, 768:1024]
    dt_ref[...] = p[:, 2560:2688]
    v_ref[...] = p[:, 2048:2560].astype(BF16)

    gm = gm_ref[...]
    cos = cos_ref[...]
    sa = sa_ref[...]
    sb = sb_ref[...]

    def norm_rope(t, gain, scale):
        t2 = t * t
        hi = t2.astype(BF16)
        lo = (t2 - hi.astype(F32)).astype(BF16)
        ms32 = _dot(hi, gm) + _dot(lo, gm)
        tn = t * lax.rsqrt(ms32 + EPS) * gain
        outs = []
        for j in range(DA_WIDTH // LANES):
            tb = tn[:, j * LANES:(j + 1) * LANES]
            ob = tb * cos + pltpu.roll(tb, LANES - ROPE_FREQS, 1) * sa + pltpu.roll(tb, ROPE_FREQS, 1) * sb
            outs.append(ob * scale)
        return jnp.concatenate(outs, axis=1).astype(BF16)

    q_ref[...] = norm_rope(p[:, 1024:1536], gq_ref[...], LOG2E * DA_HEAD_DIM ** -0.5)
    k_ref[...] = norm_rope(p[:, 1536:2048], gk_ref[...], 1.0)


def _inproj_call(xs, g1, modl, w_r, gq, gk, gmat, cos_t, sa_t, sb_t, n_ctx_tiles):
    t, d = xs.shape
    tm = ROW_TILE
    row = lambda i: (i, 0)
    const = lambda i: (0, 0)
    seg = lambda i: ((i >= n_ctx_tiles).astype(jnp.int32), 0, 0)
    outs = [(256, F32), (512, F32), (256, F32), (LANES, F32), (512, BF16), (512, BF16), (512, BF16)]
    return pl.pallas_call(
        _inproj_body,
        grid=(t // tm,),
        in_specs=[pl.BlockSpec((tm, d), row),
                  pl.BlockSpec((1, d), const),
                  pl.BlockSpec((1, 8, d), seg),
                  pl.BlockSpec((d, IN_PAD), const),
                  pl.BlockSpec((1, DA_WIDTH), const),
                  pl.BlockSpec((1, DA_WIDTH), const),
                  pl.BlockSpec((DA_WIDTH, DA_WIDTH), const),
                  pl.BlockSpec((tm, LANES), row),
                  pl.BlockSpec((tm, LANES), row),
                  pl.BlockSpec((tm, LANES), row)],
        out_specs=[pl.BlockSpec((tm, w), row) for w, _ in outs],
        out_shape=[jax.ShapeDtypeStruct((t, w), dt) for w, dt in outs],
        compiler_params=_cparams("arbitrary"),
        name="in_proj",
    )(xs, g1, modl, w_r, gq, gk, gmat, cos_t, sa_t, sb_t)


def _softplus(x):
    return jnp.maximum(x, 0.0) + jnp.log1p(jnp.exp(-jnp.abs(x)))


def _ssd_chunk_index(i, n_ctx, n_chunks, reverse):
    if not reverse:
        return i
    return jnp.where(i < n_ctx, n_ctx - 1 - i, n_chunks - 1 - (i - n_ctx))


def _ssd_body(n_ctx, n_chunks, reverse, xc_ref, xp_ref, xn_ref, dtc_ref, dtr_ref, cw_ref, cb_ref,
              bias_c_ref, a_c_ref, bias_r_ref, a_r_ref, dsk_ref, y_ref, st_ref):
    i = pl.program_id(0)
    c = _ssd_chunk_index(i, n_ctx, n_chunks, reverse)
    L = SSD_CHUNK

    @pl.when(i == 0)
    def _():
        st_ref[...] = jnp.zeros_like(st_ref)

    x = xc_ref[...]
    seg_first = jnp.logical_or(c == 0, c == n_ctx)
    seg_last = jnp.logical_or(c == n_ctx - 1, c == n_chunks - 1)
    prow = jnp.where(seg_first, 0.0, xp_ref[7:8, :])
    nrow = jnp.where(seg_last, 0.0, xn_ref[0:1, :])
    ridx = lax.broadcasted_iota(jnp.int32, (L, 1), 0)
    xprev = jnp.where(ridx == 0, prow, pltpu.roll(x, 1, 0))
    xnext = jnp.where(ridx == L - 1, nrow, pltpu.roll(x, L - 1, 0))
    cw = cw_ref[...]
    conv = xprev * cw[0:1] + x * cw[1:2] + xnext * cw[2:3] + cb_ref[...]
    xbc = _silu(conv)

    off = SSD_HEADS if reverse else 0
    dt_c = _softplus(dtc_ref[...] + bias_c_ref[...])
    dt_r = _softplus(dtr_ref[...] + bias_r_ref[...])
    adt_c = dt_c * a_c_ref[...]
    adt_r = dt_r * a_r_ref[...]

    li = lax.broadcasted_iota(jnp.int32, (L, L), 0)
    si = lax.broadcasted_iota(jnp.int32, (L, L), 1)
    mask = (li <= si) if reverse else (li >= si)
    mask_t = (li >= si) if reverse else (li <= si)
    cs_c = _dot(mask.astype(F32), adt_c, HI)
    cs_r = _dot(adt_r, mask_t.astype(F32), HI)
    edge = 0 if reverse else L - 1

    ys = []
    for g in range(SSD_GROUPS):
        bg = xbc[:, SSD_INNER + g * SSD_STATE:SSD_INNER + (g + 1) * SSD_STATE]
        cg = xbc[:, SSD_INNER + (SSD_GROUPS + g) * SSD_STATE:SSD_INNER + (SSD_GROUPS + g + 1) * SSD_STATE]
        gmat = _dot_nt(cg, bg, HI)
        for hh in range(SSD_HEADS // SSD_GROUPS):
            h = g * (SSD_HEADS // SSD_GROUPS) + hh
            col = off + h
            csc = cs_c[:, col:col + 1]
            csr = cs_r[col:col + 1, :]
            dec = jnp.exp(jnp.where(mask, csc - csr, -jnp.inf))
            xh = xbc[:, h * SSD_HEAD_DIM:(h + 1) * SSD_HEAD_DIM]
            xd = xh * dt_c[:, col:col + 1]
            st = st_ref[h]
            y = _dot(gmat * dec, xd, HI) + jnp.exp(csc) * _dot_nt(cg, st, HI)
            tot = cs_c[edge:edge + 1, col:col + 1]
            st_ref[h] = st * jnp.exp(tot) + _dot_tn(xd * jnp.exp(tot - csc), bg, HI)
            ys.append(y)
    y = jnp.concatenate(ys, axis=1)
    if not reverse:
        y = y + dsk_ref[...] * xbc[:, :SSD_INNER]
    y_ref[...] = y


def _ssd_call(xbc_raw, dt_c, dt_r, cw, cb, bias_c, a_c, bias_r, a_r, dsk, n_ctx, reverse):
    t = xbc_raw.shape[0]
    L = SSD_CHUNK
    n_chunks = t // L
    sub = L // 8
    n_sub = t // 8
    cidx = functools.partial(_ssd_chunk_index, n_ctx=n_ctx, n_chunks=n_chunks, reverse=reverse)
    const = lambda i: (0, 0)
    return pl.pallas_call(
        functools.partial(_ssd_body, n_ctx, n_chunks, reverse),
        grid=(n_chunks,),
        in_specs=[pl.BlockSpec((L, SSD_CONV_DIM), lambda i: (cidx(i), 0)),
                  pl.BlockSpec((8, SSD_CONV_DIM), lambda i: (jnp.maximum(cidx(i) * sub - 1, 0), 0)),
                  pl.BlockSpec((8, SSD_CONV_DIM), lambda i: (jnp.minimum((cidx(i) + 1) * sub, n_sub - 1), 0)),
                  pl.BlockSpec((L, LANES), lambda i: (cidx(i), 0)),
                  pl.BlockSpec((8, L), lambda i: (0, cidx(i))),
                  pl.BlockSpec((8, SSD_CONV_DIM), const),
                  pl.BlockSpec((1, SSD_CONV_DIM), const),
                  pl.BlockSpec((1, LANES), const),
                  pl.BlockSpec((1, LANES), const),
                  pl.BlockSpec((8, L), const),
                  pl.BlockSpec((8, L), const),
                  pl.BlockSpec((1, SSD_INNER), const)],
        out_specs=pl.BlockSpec((L, SSD_INNER), lambda i: (cidx(i), 0)),
        out_shape=jax.ShapeDtypeStruct((t, SSD_INNER), F32),
        scratch_shapes=[pltpu.VMEM((SSD_HEADS, SSD_HEAD_DIM, SSD_STATE), F32)],
        compiler_params=_cparams("arbitrary"),
        name="ssd_bwd" if reverse else "ssd_fwd",
    )(xbc_raw, xbc_raw, xbc_raw, dt_c, dt_r, cw, cb, bias_c, a_c, bias_r, a_r, dsk)


def _s5_in_body(u_ref, w_ref, efr_ref, efi_ref, ebr_ref, ebi_ref):
    e = _dot(u_ref[0], w_ref[0], HI)
    efr_ref[...] = e[:, 0:128]
    efi_ref[...] = e[:, 128:256]
    ebr_ref[...] = e[:, 256:384]
    ebi_ref[...] = e[:, 384:512]


def _s5_in_call(u_pair, w_in):
    npair, nch, width = u_pair.shape
    col = lambda r: (0, r)
    return pl.pallas_call(
        _s5_in_body,
        grid=(npair,),
        in_specs=[pl.BlockSpec((1, nch, width), lambda r: (r, 0, 0)),
                  pl.BlockSpec((1, width, width), lambda r: (r, 0, 0))],
        out_specs=[pl.BlockSpec((nch, LANES), col)] * 4,
        out_shape=[jax.ShapeDtypeStruct((nch, npair * LANES), F32)] * 4,
        compiler_params=_cparams("arbitrary"),
        name="s5_chunk_in",
    )(u_pair, w_in)


def _s5_tile_index(i, n_tiles, reverse):
    if not reverse:
        return i
    return jnp.where(i == 0, 0, n_tiles - i)


def _s5_scan_body(a_ref, efr_ref, efi_ref, ebr_ref, ebi_ref, hfr_ref, hfi_ref, hbr_ref, hbi_ref, st_ref):
    @pl.when(pl.program_id(0) == 0)
    def _():
        st_ref[...] = jnp.zeros_like(st_ref)

    afr, afi, abr, abi = a_ref[0], a_ref[1], a_ref[2], a_ref[3]
    fr, fi, br, bi = st_ref[0], st_ref[1], st_ref[2], st_ref[3]
    for j in range(S5_SCAN_TILE):
        hfr_ref[j] = fr
        hfi_ref[j] = fi
        fr, fi = afr * fr - afi * fi + efr_ref[j], afr * fi + afi * fr + efi_ref[j]
        jb = S5_SCAN_TILE - 1 - j
        hbr_ref[jb] = br
        hbi_ref[jb] = bi
        br, bi = abr * br - abi * bi + ebr_ref[jb], abr * bi + abi * br + ebi_ref[jb]
    st_ref[0] = fr
    st_ref[1] = fi
    st_ref[2] = br
    st_ref[3] = bi


def _s5_scan_call(a_pow, efr, efi, ebr, ebi):
    nch, npair, _ = efr.shape
    n_tiles = nch // S5_SCAN_TILE
    blk = (S5_SCAN_TILE, npair, LANES)
    fwd = lambda i: (i, 0, 0)
    bwd = lambda i: (_s5_tile_index(i, n_tiles, True), 0, 0)
    shp = jax.ShapeDtypeStruct((nch, npair, LANES), F32)
    return pl.pallas_call(
        _s5_scan_body,
        grid=(n_tiles,),
        in_specs=[pl.BlockSpec((4, npair, LANES), lambda i: (0, 0, 0)),
                  pl.BlockSpec(blk, fwd), pl.BlockSpec(blk, fwd),
                  pl.BlockSpec(blk, bwd), pl.BlockSpec(blk, bwd)],
        out_specs=[pl.BlockSpec(blk, fwd), pl.BlockSpec(blk, fwd),
                   pl.BlockSpec(blk, bwd), pl.BlockSpec(blk, bwd)],
        out_shape=[shp] * 4,
        scratch_shapes=[pltpu.VMEM((4, npair, LANES), F32)],
        compiler_params=_cparams("arbitrary"),
        name="s5_state_scan",
    )(a_pow, efr, efi, ebr, ebi)


def _s5_out_body(u_ref, hfr_ref, hfi_ref, hbr_ref, hbi_ref, m_ref, q_ref, y_ref):
    y = _dot(u_ref[0], m_ref[0], HI)
    y += _dot(hfr_ref[...], q_ref[0, 0], HI)
    y += _dot(hfi_ref[...], q_ref[0, 1], HI)
    y += _dot(hbr_ref[...], q_ref[0, 2], HI)
    y += _dot(hbi_ref[...], q_ref[0, 3], HI)
    y_ref[0] = y


def _s5_out_call(u_pair, hfr, hfi, hbr, hbi, w_m, w_q):
    npair, nch, width = u_pair.shape
    col = lambda r: (0, r)
    return pl.pallas_call(
        _s5_out_body,
        grid=(npair,),
        in_specs=[pl.BlockSpec((1, nch, width), lambda r: (r, 0, 0))]
                 + [pl.BlockSpec((nch, LANES), col)] * 4
                 + [pl.BlockSpec((1, width, width), lambda r: (r, 0, 0)),
                    pl.BlockSpec((1, 4, LANES, width), lambda r: (r, 0, 0, 0))],
        out_specs=pl.BlockSpec((1, nch, width), lambda r: (r, 0, 0)),
        out_shape=jax.ShapeDtypeStruct((npair, nch, width), F32),
        compiler_params=_cparams("arbitrary"),
        name="s5_chunk_out",
    )(u_pair, hfr, hfi, hbr, hbi, w_m, w_q)


def _s5_weights(lam_re, lam_im, log_step, b_re, b_im, c_re, c_im):
    tc = S5_CHUNK
    taus = jnp.arange(tc + 1, dtype=F32)
    eye2 = jnp.eye(2, dtype=F32)
    ks, ps, qs, a_pows = [], [], [], []
    for d in range(2):
        lam = lax.complex(lam_re[d], lam_im[d])
        step = jnp.exp(log_step[d])[:, None]
        lam_bar = jnp.exp(lam * step)
        b_bar = ((lam_bar - 1.0) / lam)[..., None] * lax.complex(b_re[d], b_im[d])
        cm = lax.complex(c_re[d], c_im[d])
        lam_pow = jnp.exp((lam * step)[None] * taus[:, None, None])
        ks.append(jnp.einsum('gcp,tgp,gpk->tgck', cm, lam_pow[:tc], b_bar).real)
        p_pow = lam_pow[:tc][::-1] if d == 0 else lam_pow[:tc]
        ps.append(jnp.einsum('igp,gpk->gikp', p_pow, b_bar))
        q_pow = lam_pow[1:] if d == 0 else lam_pow[1:][::-1]
        qs.append(jnp.einsum('gcp,jgp->gpjc', cm, q_pow))
        a_pows.append(lam_pow[tc])

    ii = jnp.arange(tc)[:, None]
    jj = jnp.arange(tc)[None, :]
    kf = jnp.where((jj >= ii)[:, :, None, None, None], ks[0][jnp.clip(jj - ii, 0)], 0.0)
    kb = jnp.where((ii >= jj)[:, :, None, None, None], ks[1][jnp.clip(ii - jj, 0)], 0.0)
    m = jnp.transpose(kf + kb, (2, 0, 4, 1, 3))

    def pair_rows_cols(w):
        w = w.reshape(S5_PAIRS, 2, tc, S5_GROUP, tc, S5_GROUP)
        w = jnp.einsum('rgicjd,gh->rigcjhd', w, eye2)
        return w.reshape(S5_PAIRS, tc * 2 * S5_GROUP, tc * 2 * S5_GROUP)

    def pair_rows(w):
        w = w.reshape(S5_PAIRS, 2, tc, S5_GROUP, S5_STATE)
        w = jnp.einsum('rgicp,gh->rigchp', w, eye2)
        return w.reshape(S5_PAIRS, tc * 2 * S5_GROUP, 2 * S5_STATE)

    def pair_cols(w):
        w = w.reshape(S5_PAIRS, 2, S5_STATE, tc, S5_GROUP)
        w = jnp.einsum('rgpjc,gh->rgpjhc', w, eye2)
        return w.reshape(S5_PAIRS, 2 * S5_STATE, tc * 2 * S5_GROUP)

    w_m = pair_rows_cols(m)
    w_in = jnp.concatenate([pair_rows(ps[0].real), pair_rows(ps[0].imag),
                            pair_rows(ps[1].real), pair_rows(ps[1].imag)], axis=-1)
    w_q = jnp.stack([pair_cols(qs[0].real), pair_cols(-qs[0].imag),
                     pair_cols(qs[1].real), pair_cols(-qs[1].imag)], axis=1)
    lanes = lambda a: a.reshape(S5_PAIRS, 2 * S5_STATE)
    a_pow = jnp.stack([lanes(a_pows[0].real), lanes(a_pows[0].imag),
                       lanes(a_pows[1].real), lanes(a_pows[1].imag)], axis=0)
    return w_in.astype(F32), w_m.astype(F32), w_q.astype(F32), a_pow.astype(F32)


def _s5_mix(u, weights):
    w_in, w_m, w_q, a_pow = weights
    t = u.shape[0]
    nch = t // S5_CHUNK
    u_pair = u.reshape(nch, S5_CHUNK, S5_PAIRS, 2 * S5_GROUP).transpose(2, 0, 1, 3)
    u_pair = u_pair.reshape(S5_PAIRS, nch, S5_CHUNK * 2 * S5_GROUP)
    es = _s5_in_call(u_pair, w_in)
    es = [e.reshape(nch, S5_PAIRS, LANES) for e in es]
    hs = _s5_scan_call(a_pow, *es)
    hs = [h.reshape(nch, S5_PAIRS * LANES) for h in hs]
    y_pair = _s5_out_call(u_pair, *hs, w_m, w_q)
    y = y_pair.reshape(S5_PAIRS, nch, S5_CHUNK, 2 * S5_GROUP).transpose(1, 2, 0, 3)
    return y.reshape(t, S5_WIDTH)


def _attn_body(flag_ref, lam_ref, q_ref, k_ref, v_ref, g_ref, o_ref, acc_ref, m_ref, *, n_kv, post_scale):
    q = q_ref[0]
    acc_ref[...] = jnp.zeros_like(acc_ref)
    bounded = flag_ref[0] == 1

    @pl.when(bounded)
    def _():
        def step(b, carry):
            v = v_ref[0, b]
            for c in range(2):
                p = jnp.exp2(_dot(q, k_ref[0, b, c])).astype(BF16)
                acc_ref[c] += _dot(p, v)
            return carry
        lax.fori_loop(0, n_kv, step, 0)

    @pl.when(jnp.logical_not(bounded))
    def _():
        m_ref[...] = jnp.full_like(m_ref, -jnp.inf)

        def step(b, carry):
            v = v_ref[0, b]
            for c in range(2):
                s = _dot(q, k_ref[0, b, c])
                m_old = m_ref[c]
                m_new = jnp.maximum(m_old, jnp.max(s, axis=1, keepdims=True))
                p = jnp.exp2(s - m_new[:, 0:1]).astype(BF16)
                acc_ref[c] = jnp.exp2(m_old - m_new) * acc_ref[c] + _dot(p, v)
                m_ref[c] = m_new
            return carry
        lax.fori_loop(0, n_kv, step, 0)

    a0 = acc_ref[0]
    a1 = acc_ref[1]
    o = a0[:, :DA_V_DIM] / a0[:, DA_V_DIM:DA_V_DIM + 1] \
        - lam_ref[0] * (a1[:, :DA_V_DIM] / a1[:, DA_V_DIM:DA_V_DIM + 1])
    ms = jnp.mean(o * o, axis=-1, keepdims=True)
    o_ref[0] = o * lax.rsqrt(ms + EPS) * g_ref[...] * post_scale


def _pick_tile(n, candidates):
    for c in candidates:
        if n % c == 0:
            return c
    return n


def _attn_call(flag, lam, qh, kzb, vb, g_sub, post_scale):
    nh, tq_all, _ = qh.shape
    n_kv, tk = kzb.shape[1], kzb.shape[4]
    tq = _pick_tile(tq_all, (ATTN_TQ, 256))
    smem = pl.BlockSpec(memory_space=pltpu.SMEM)
    return pl.pallas_call(
        functools.partial(_attn_body, n_kv=n_kv, post_scale=post_scale),
        grid=(nh, tq_all // tq),
        in_specs=[smem, smem,
                  pl.BlockSpec((1, tq, 2 * DA_HEAD_DIM), lambda h, i: (h, i, 0)),
                  pl.BlockSpec((1, n_kv, 2, 2 * DA_HEAD_DIM, tk), lambda h, i: (h, 0, 0, 0, 0)),
                  pl.BlockSpec((1, n_kv, tk, LANES), lambda h, i: (h, 0, 0, 0)),
                  pl.BlockSpec((1, DA_V_DIM), lambda h, i: (0, 0))],
        out_specs=pl.BlockSpec((1, tq, DA_V_DIM), lambda h, i: (h, i, 0)),
        out_shape=jax.ShapeDtypeStruct((nh, tq_all, DA_V_DIM), F32),
        scratch_shapes=[pltpu.VMEM((2, tq, LANES), F32), pltpu.VMEM((2, tq, LANES), F32)],
        compiler_params=_cparams("arbitrary", "arbitrary"),
        name="diff_attn",
    )(flag, lam, qh, kzb, vb, g_sub)


def _kv_blocks(kth, vh, tk):
    nh, dk, tk_all = kth.shape
    n_kv = tk_all // tk
    top = (jnp.arange(dk) < DA_HEAD_DIM)[None, :, None]
    kz = jnp.stack([jnp.where(top, kth, 0), jnp.where(top, 0, kth)], axis=1)
    kzb = kz.reshape(nh, 2, dk, n_kv, tk).transpose(0, 3, 1, 2, 4)
    return kzb, vh.reshape(nh, n_kv, tk, LANES)


def _gelu_tanh(x):
    return 0.5 * x * (1.0 + jnp.tanh(math.sqrt(2.0 / math.pi) * (x + 0.044715 * (x * x * x))))


def _outffn_body(x_ref, yf_ref, yb_ref, z_ref, gssd_ref, ys5_ref, u_ref, d5_ref, gw_ref, gb_ref, yda_ref,
                 mod_ref, g2_ref, wo_ref, w1_ref, w3_ref, w2_ref, o_ref):
    mod = mod_ref[0]
    y = (yf_ref[...] + yb_ref[...]) * _silu(z_ref[...])
    y_ssd = y * lax.rsqrt(jnp.mean(y * y, axis=-1, keepdims=True) + EPS) * gssd_ref[...]
    y5 = _gelu_tanh(ys5_ref[...] + d5_ref[...] * u_ref[...])
    y5 = y5 * jax.nn.sigmoid(_dot(y5, gw_ref[...], HI) + gb_ref[...])
    wo = wo_ref[...]
    o = _dot(y_ssd.astype(BF16), wo[0:256])
    o += _dot(y5.astype(BF16), wo[256:512])
    o += _dot(yda_ref[...].astype(BF16), wo[512:1024])
    x = x_ref[...] + mod[2:3] * o
    h = x * lax.rsqrt(jnp.mean(x * x, axis=-1, keepdims=True) + EPS) * g2_ref[...] * (1.0 + mod[4:5]) + mod[3:4]
    hb = h.astype(BF16)
    f = _silu(_dot(hb, w1_ref[...])) * _dot(hb, w3_ref[...])
    o_ref[...] = x + mod[5:6] * _dot(f.astype(BF16), w2_ref[...])


def _outffn_call(xs, y_f, y_b, z, g_ssd, y_s5, u, d5, glu_w, glu_b, y_da, modl, g2, w_o, w1, w3, w2,
                 n_ctx_tiles):
    t, d = xs.shape
    dff = w1.shape[1]
    tm = ROW_TILE
    row = lambda i: (i, 0)
    const = lambda i: (0, 0)
    seg = lambda i: ((i >= n_ctx_tiles).astype(jnp.int32), 0, 0)
    one = pl.Buffered(1)
    return pl.pallas_call(
        _outffn_body,
        grid=(t // tm,),
        in_specs=[pl.BlockSpec((tm, d), row),
                  pl.BlockSpec((tm, 256), row), pl.BlockSpec((tm, 256), row), pl.BlockSpec((tm, 256), row),
                  pl.BlockSpec((1, 256), const),
                  pl.BlockSpec((tm, 256), row), pl.BlockSpec((tm, 256), row),
                  pl.BlockSpec((1, 256), const),
                  pl.BlockSpec((256, 256), const), pl.BlockSpec((1, 256), const),
                  pl.BlockSpec((tm, 512), row),
                  pl.BlockSpec((1, 8, d), seg),
                  pl.BlockSpec((1, d), const),
                  pl.BlockSpec((d, d), const, pipeline_mode=one),
                  pl.BlockSpec((d, dff), const, pipeline_mode=one),
                  pl.BlockSpec((d, dff), const, pipeline_mode=one),
                  pl.BlockSpec((dff, d), const, pipeline_mode=one)],
        out_specs=pl.BlockSpec((tm, d), row),
        out_shape=jax.ShapeDtypeStruct((t, d), F32),
        compiler_params=_cparams("arbitrary"),
        name="out_proj_ffn",
    )(xs, y_f, y_b, z, g_ssd, y_s5, u, d5, glu_w, glu_b, y_da, modl, g2, w_o, w1, w3, w2)


def _rope_tables(n_ctx, n_lat):
    pos = jnp.arange(n_lat)
    lane = jnp.arange(LANES)
    d = lane % DA_HEAD_DIM
    axis = d // (2 * ROPE_FREQS)
    half = (d % (2 * ROPE_FREQS)) // ROPE_FREQS
    freqs = ROPE_THETA ** (-jnp.arange(ROPE_FREQS, dtype=F32) / ROPE_FREQS)
    coord = jnp.where(axis[None, :] == 0, (pos // GRID_W)[:, None], (pos % GRID_W)[:, None]).astype(F32)
    ang = coord * freqs[d % ROPE_FREQS][None, :]
    cos, sin = jnp.cos(ang), jnp.sin(ang)
    sa = jnp.where(half[None, :] == 0, -sin, 0.0)
    sb = jnp.where(half[None, :] == 1, sin, 0.0)
    pad = lambda a, v: jnp.concatenate([jnp.full((n_ctx, LANES), v, F32), a], axis=0)
    return pad(cos, 1.0), pad(sa, 0.0), pad(sb, 0.0)


def _lane_row(vals, width=LANES):
    vals = vals.reshape(-1).astype(F32)
    return jnp.zeros((1, width), F32).at[0, :vals.shape[0]].set(vals)


def kernel(x, c, ctx, c_ctx, w_mod, b_mod, norm1, norm2, w_in, w_out, ssd_conv_w, ssd_conv_b, ssd_a_log, ssd_dt_bias, ssd_d, ssd_norm, s5_lam_re, s5_lam_im, s5_log_step, s5_b_re, s5_b_im, s5_c_re, s5_c_im, s5_d, s5_glu_w, s5_glu_b, da_q_norm, da_k_norm, da_lambda, da_sub_norm, ffn_w1, ffn_w3, ffn_w2):
    depth = w_mod.shape[0]
    bsz, n_lat, d = x.shape
    n_ctx = ctx.shape[1]
    assert bsz == 1 and n_ctx % ROW_TILE == 0 and n_lat % ROW_TILE == 0
    assert n_ctx % (S5_CHUNK * S5_SCAN_TILE) == 0 and n_lat % (S5_CHUNK * S5_SCAN_TILE) == 0
    t = n_ctx + n_lat
    n_ctx_tiles = n_ctx // ROW_TILE

    xs = jnp.concatenate([ctx[0], x[0]], axis=0)
    cv = jnp.zeros((8, d), F32).at[0].set(c_ctx).at[1].set(c[0])
    mod_all = _mod_call(cv, w_mod, b_mod)
    cos_t, sa_t, sb_t = _rope_tables(n_ctx, n_lat)
    gidx = jnp.arange(DA_WIDTH) // DA_HEAD_DIM
    gmat = (gidx[:, None] == gidx[None, :]).astype(BF16) * (1.0 / DA_HEAD_DIM)
    ones_col = jnp.zeros((DA_HEADS, t, LANES - DA_V_DIM), BF16).at[:, :, 0].set(1.0)

    for i in range(depth):
        lam_init = 0.8 - 0.6 * math.exp(-0.3 * i)
        modl = jnp.zeros((2, 8, d), F32).at[:, :6].set(mod_all[i, :2].reshape(2, 6, d))

        wi = w_in[i]
        w_r = jnp.concatenate([wi[:, 0:768], wi[:, 776:1032], wi[:, 1032:2568], wi[:, 768:776],
                               jnp.zeros((d, IN_PAD - 2568), F32)], axis=1).astype(BF16)
        gq = jnp.tile(da_q_norm[i], DA_WIDTH // DA_HEAD_DIM)[None, :]
        gk = jnp.tile(da_k_norm[i], DA_WIDTH // DA_HEAD_DIM)[None, :]
        z, xbc_raw, u, dt_raw, q, k, v = _inproj_call(
            xs, norm1[i][None, :], modl, w_r, gq, gk, gmat, cos_t, sa_t, sb_t, n_ctx_tiles)

        dt_r = dt_raw[:, :8].T
        cw = jnp.zeros((8, SSD_CONV_DIM), F32).at[:3].set(ssd_conv_w[i])
        a_neg = -jnp.exp(ssd_a_log[i])
        bias_c = _lane_row(ssd_dt_bias[i])
        a_c = _lane_row(a_neg)
        bias_r = jnp.broadcast_to(ssd_dt_bias[i].reshape(8, 1), (8, SSD_CHUNK))
        a_r = jnp.broadcast_to(a_neg.reshape(8, 1), (8, SSD_CHUNK))
        dsk = jnp.repeat(ssd_d[i], SSD_HEAD_DIM)[None, :]
        ssd_args = (xbc_raw, dt_raw, dt_r, cw, ssd_conv_b[i][None, :], bias_c, a_c, bias_r, a_r, dsk,
                    n_ctx // SSD_CHUNK)
        y_f = _ssd_call(*ssd_args, reverse=False)
        y_b = _ssd_call(*ssd_args, reverse=True)

        y_s5 = _s5_mix(u, _s5_weights(s5_lam_re[i], s5_lam_im[i], s5_log_step[i], s5_b_re[i], s5_b_im[i],
                                      s5_c_re[i], s5_c_im[i]))

        lf = da_lambda[i]
        lam = (jnp.exp(jnp.sum(lf[0] * lf[1])) - jnp.exp(jnp.sum(lf[2] * lf[3])) + lam_init).reshape(1)
        qh = q.reshape(t, DA_HEADS, 2 * DA_HEAD_DIM).transpose(1, 0, 2)
        kth = k.reshape(t, DA_HEADS, 2 * DA_HEAD_DIM).transpose(1, 2, 0)
        vh = jnp.concatenate([v.reshape(t, DA_HEADS, DA_V_DIM).transpose(1, 0, 2), ones_col], axis=2)
        g_sub = da_sub_norm[i][None, :]
        score_bound = LOG2E * math.sqrt(DA_HEAD_DIM) * jnp.max(jnp.abs(da_q_norm[i])) * jnp.max(jnp.abs(da_k_norm[i]))
        flag = (score_bound <= SCORE_LOG2_LIMIT).astype(jnp.int32).reshape(1)
        kz_c, v_c = _kv_blocks(kth[:, :, :n_ctx], vh[:, :n_ctx], n_ctx)
        kz_l, v_l = _kv_blocks(kth, vh, _pick_tile(t, (ATTN_TK, 1024, 640, 512, 256)))
        o_c = _attn_call(flag, lam, qh[:, :n_ctx], kz_c, v_c, g_sub, 1.0 - lam_init)
        o_l = _attn_call(flag, lam, qh[:, n_ctx:], kz_l, v_l, g_sub, 1.0 - lam_init)
        y_da = jnp.concatenate([o_c, o_l], axis=1).transpose(1, 0, 2).reshape(t, DA_WIDTH)

        xs = _outffn_call(xs, y_f, y_b, z, ssd_norm[i][None, :], y_s5, u, s5_d[i][None, :], s5_glu_w[i],
                          s5_glu_b[i][None, :], y_da, modl, norm2[i][None, :], w_out[i].astype(BF16),
                          ffn_w1[i].astype(BF16), ffn_w3[i].astype(BF16), ffn_w2[i].astype(BF16),
                          n_ctx_tiles)
    return xs[n_ctx:][None]
```

```python
import functools
import math

import jax
import jax.numpy as jnp
from jax import lax
from jax.experimental import pallas as pl
from jax.experimental.pallas import tpu as pltpu

F32 = jnp.float32
BF16 = jnp.bfloat16
HI = lax.Precision.HIGHEST

EPS = 1e-6
GRID_W = 64
ROPE_THETA = 10000.0
ROPE_FREQS = 8

SSD_HEADS = 4
SSD_HEAD_DIM = 64
SSD_INNER = 256
SSD_GROUPS = 2
SSD_STATE = 64
SSD_CHUNK = 128
SSD_CONV_DIM = 512

S5_WIDTH = 256
S5_GROUP = 16
S5_GROUPS = 16
S5_STATE = 64
S5_CHUNK = 16
S5_HALF_GROUPS = S5_GROUPS // 2
S5_HALF_STATE = S5_HALF_GROUPS * S5_STATE
S5_SCAN_TILE = 16
S5_TOEPLITZ_BLOCKS = 2 * S5_CHUNK - 1

DA_HEADS = 8
DA_HEAD_DIM = 32
DA_V_DIM = 64
DA_WIDTH = 512
ATTN_HEADS_PER_STEP = 2
ATTN_TQ = 1024
ATTN_TK = 1280
LOG2E = 1.4426950408889634
SCORE_LOG2_LIMIT = 100.0

LANES = 128
ROW_TILE = 256
IN_PAD = 2688
VMEM_LIMIT = 56 * 1024 * 1024


def _cparams(*sem):
    return pltpu.CompilerParams(dimension_semantics=sem, vmem_limit_bytes=VMEM_LIMIT)


def _dot(a, b, precision=None):
    return jnp.dot(a, b, preferred_element_type=F32, precision=precision)


def _dot_nt(a, b, precision=None):
    return lax.dot_general(a, b, (((1,), (1,)), ((), ())), preferred_element_type=F32,
                           precision=precision)


def _dot_tn(a, b, precision=None):
    return lax.dot_general(a, b, (((0,), (0,)), ((), ())), preferred_element_type=F32,
                           precision=precision)


def _silu(x):
    return x * jax.nn.sigmoid(x)


def _pick_tile(n, candidates):
    for c in candidates:
        if n % c == 0:
            return c
    return n


def _segment_order(i, n_lat, n_all, reverse):
    n_ctx = n_all - n_lat
    if not reverse:
        return jnp.where(i < n_ctx, n_lat + i, i - n_ctx)
    return jnp.where(i < n_ctx, n_all - 1 - i, n_lat - 1 - (i - n_ctx))


def _mod_body(cv_ref, w_ref, b_ref, o_ref):
    o_ref[0] = _dot(_silu(cv_ref[...]), w_ref[0], HI) + b_ref[0]


def _mod_call(cv, w_mod, b_mod):
    depth, d, n = w_mod.shape
    tn = 1024
    return pl.pallas_call(
        _mod_body,
        grid=(depth, n // tn),
        in_specs=[pl.BlockSpec((8, d), lambda l, j: (0, 0)),
                  pl.BlockSpec((1, d, tn), lambda l, j: (l, 0, j)),
                  pl.BlockSpec((1, 1, tn), lambda l, j: (l, 0, j))],
        out_specs=pl.BlockSpec((1, 8, tn), lambda l, j: (l, 0, j)),
        out_shape=jax.ShapeDtypeStruct((depth, 8, n), F32),
        compiler_params=_cparams("arbitrary", "arbitrary"),
        name="adaln_mod",
    )(cv, w_mod, b_mod.reshape(depth, 1, n))


def _inproj_body(x_ref, g_ref, mod_ref, w_ref, gq_ref, gk_ref, gm_ref, cos_ref, sa_ref, sb_ref,
                 z_ref, xbc_ref, u_ref, dt_ref, q_ref, k_ref, v_ref):
    x = x_ref[...]
    rows = x.shape[0]
    mod = mod_ref[0]
    ms = jnp.mean(x * x, axis=-1, keepdims=True)
    h = x * lax.rsqrt(ms + EPS) * g_ref[...] * (1.0 + mod[1:2]) + mod[0:1]
    p = _dot(h.astype(BF16), w_ref[...])
    z_ref[...] = p[:, 0:256]
    xbc_ref[...] = p[:, 256:768]
    u_ref[...] = p[:, 768:1024]
    dt_ref[...] = p[:, 2560:2688]

    gm = gm_ref[...]
    cos = cos_ref[...]
    sa = sa_ref[...]
    sb = sb_ref[...]

    def norm_rope(t, gain, scale):
        t2 = t * t
        hi = t2.astype(BF16)
        lo = (t2 - hi.astype(F32)).astype(BF16)
        ms32 = _dot(hi, gm) + _dot(lo, gm)
        tn = t * lax.rsqrt(ms32 + EPS) * gain
        outs = []
        for j in range(DA_WIDTH // LANES):
            tb = tn[:, j * LANES:(j + 1) * LANES]
            ob = tb * cos + pltpu.roll(tb, LANES - ROPE_FREQS, 1) * sa + pltpu.roll(tb, ROPE_FREQS, 1) * sb
            outs.append(ob * scale)
        return jnp.concatenate(outs, axis=1)

    qn = norm_rope(p[:, 1024:1536], gq_ref[...], LOG2E * DA_HEAD_DIM ** -0.5).astype(BF16)
    kt = norm_rope(p[:, 1536:2048], gk_ref[...], 1.0).T.astype(BF16)
    vv = p[:, 2048:2560].astype(BF16)
    lane = lax.broadcasted_iota(jnp.int32, (rows, LANES - DA_V_DIM), 1)
    ones_col = jnp.where(lane == 0, 1.0, 0.0).astype(BF16)
    zeros_k = jnp.zeros((DA_HEAD_DIM, rows), BF16)
    for hd in range(DA_HEADS):
        lo_, mid, hi_ = hd * DA_V_DIM, hd * DA_V_DIM + DA_HEAD_DIM, (hd + 1) * DA_V_DIM
        q_ref[hd] = qn[:, lo_:hi_]
        k_ref[hd, 0, 0, 0:DA_HEAD_DIM, :] = kt[lo_:mid]
        k_ref[hd, 0, 0, DA_HEAD_DIM:, :] = zeros_k
        k_ref[hd, 0, 1, 0:DA_HEAD_DIM, :] = zeros_k
        k_ref[hd, 0, 1, DA_HEAD_DIM:, :] = kt[mid:hi_]
        v_ref[hd, 0] = jnp.concatenate([vv[:, lo_:hi_], ones_col], axis=1)


def _inproj_call(xs, g1, modl, w_r, gq, gk, gmat, cos_t, sa_t, sb_t, n_lat_tiles, tk):
    t, d = xs.shape
    tm = ROW_TILE
    per = tk // tm
    row = lambda i: (i, 0)
    const = lambda i: (0, 0)
    seg = lambda i: ((i >= n_lat_tiles).astype(jnp.int32), 0, 0)
    flat = [(256, F32), (512, F32), (256, F32), (LANES, F32)]
    return pl.pallas_call(
        _inproj_body,
        grid=(t // tm,),
        in_specs=[pl.BlockSpec((tm, d), row),
                  pl.BlockSpec((1, d), const),
                  pl.BlockSpec((1, 8, d), seg),
                  pl.BlockSpec((d, IN_PAD), const),
                  pl.BlockSpec((1, DA_WIDTH), const),
                  pl.BlockSpec((1, DA_WIDTH), const),
                  pl.BlockSpec((DA_WIDTH, DA_WIDTH), const),
                  pl.BlockSpec((tm, LANES), row),
                  pl.BlockSpec((tm, LANES), row),
                  pl.BlockSpec((tm, LANES), row)],
        out_specs=[pl.BlockSpec((tm, w), row) for w, _ in flat]
                  + [pl.BlockSpec((DA_HEADS, tm, 2 * DA_HEAD_DIM), lambda i: (0, i, 0)),
                     pl.BlockSpec((DA_HEADS, 1, 2, 2 * DA_HEAD_DIM, tm), lambda i: (0, i // per, 0, 0, i % per)),
                     pl.BlockSpec((DA_HEADS, 1, tm, LANES), lambda i: (0, i // per, i % per, 0))],
        out_shape=[jax.ShapeDtypeStruct((t, w), dt) for w, dt in flat]
                  + [jax.ShapeDtypeStruct((DA_HEADS, t, 2 * DA_HEAD_DIM), BF16),
                     jax.ShapeDtypeStruct((DA_HEADS, t // tk, 2, 2 * DA_HEAD_DIM, tk), BF16),
                     jax.ShapeDtypeStruct((DA_HEADS, t // tk, tk, LANES), BF16)],
        compiler_params=_cparams("arbitrary"),
        name="in_proj",
    )(xs, g1, modl, w_r, gq, gk, gmat, cos_t, sa_t, sb_t)


def _softplus(x):
    return jnp.maximum(x, 0.0) + jnp.log1p(jnp.exp(-jnp.abs(x)))


def _ssd_body(n_lat, n_chunks, reverse, xc_ref, xp_ref, xn_ref, dtc_ref, dtr_ref, cw_ref, cb_ref,
              bias_c_ref, a_c_ref, bias_r_ref, a_r_ref, dsk_ref, y_ref, st_ref):
    i = pl.program_id(0)
    c = _segment_order(i, n_lat, n_chunks, reverse)
    L = SSD_CHUNK

    @pl.when(i == 0)
    def _():
        st_ref[...] = jnp.zeros_like(st_ref)

    x = xc_ref[...]
    seg_first = jnp.logical_or(c == 0, c == n_lat)
    seg_last = jnp.logical_or(c == n_lat - 1, c == n_chunks - 1)
    prow = jnp.where(seg_first, 0.0, xp_ref[7:8, :])
    nrow = jnp.where(seg_last, 0.0, xn_ref[0:1, :])
    ridx = lax.broadcasted_iota(jnp.int32, (L, 1), 0)
    xprev = jnp.where(ridx == 0, prow, pltpu.roll(x, 1, 0))
    xnext = jnp.where(ridx == L - 1, nrow, pltpu.roll(x, L - 1, 0))
    cw = cw_ref[...]
    conv = xprev * cw[0:1] + x * cw[1:2] + xnext * cw[2:3] + cb_ref[...]
    xbc = _silu(conv)

    off = SSD_HEADS if reverse else 0
    dt_c = _softplus(dtc_ref[...] + bias_c_ref[...])
    dt_r = _softplus(dtr_ref[...] + bias_r_ref[...])
    adt_c = dt_c * a_c_ref[...]
    adt_r = dt_r * a_r_ref[...]

    li = lax.broadcasted_iota(jnp.int32, (L, L), 0)
    si = lax.broadcasted_iota(jnp.int32, (L, L), 1)
    mask = (li <= si) if reverse else (li >= si)
    mask_t = (li >= si) if reverse else (li <= si)
    cs_c = _dot(mask.astype(F32), adt_c, HI)
    cs_r = _dot(adt_r, mask_t.astype(F32), HI)
    edge = 0 if reverse else L - 1

    ys = []
    for g in range(SSD_GROUPS):
        bg = xbc[:, SSD_INNER + g * SSD_STATE:SSD_INNER + (g + 1) * SSD_STATE]
        cg = xbc[:, SSD_INNER + (SSD_GROUPS + g) * SSD_STATE:SSD_INNER + (SSD_GROUPS + g + 1) * SSD_STATE]
        gmat = _dot_nt(cg, bg, HI)
        for hh in range(SSD_HEADS // SSD_GROUPS):
            h = g * (SSD_HEADS // SSD_GROUPS) + hh
            col = off + h
            csc = cs_c[:, col:col + 1]
            csr = cs_r[col:col + 1, :]
            dec = jnp.exp(jnp.where(mask, csc - csr, -jnp.inf))
            xh = xbc[:, h * SSD_HEAD_DIM:(h + 1) * SSD_HEAD_DIM]
            xd = xh * dt_c[:, col:col + 1]
            st = st_ref[h]
            y = _dot(gmat * dec, xd, HI) + jnp.exp(csc) * _dot_nt(cg, st, HI)
            tot = cs_c[edge:edge + 1, col:col + 1]
            st_ref[h] = st * jnp.exp(tot) + _dot_tn(xd * jnp.exp(tot - csc), bg, HI)
            ys.append(y)
    y = jnp.concatenate(ys, axis=1)
    if not reverse:
        y = y + dsk_ref[...] * xbc[:, :SSD_INNER]
    y_ref[...] = y


def _ssd_call(xbc_raw, dt_c, dt_r, cw, cb, bias_c, a_c, bias_r, a_r, dsk, n_lat, reverse):
    t = xbc_raw.shape[0]
    L = SSD_CHUNK
    n_chunks = t // L
    sub = L // 8
    n_sub = t // 8
    cidx = functools.partial(_segment_order, n_lat=n_lat, n_all=n_chunks, reverse=reverse)
    const = lambda i: (0, 0)
    return pl.pallas_call(
        functools.partial(_ssd_body, n_lat, n_chunks, reverse),
        grid=(n_chunks,),
        in_specs=[pl.BlockSpec((L, SSD_CONV_DIM), lambda i: (cidx(i), 0)),
                  pl.BlockSpec((8, SSD_CONV_DIM), lambda i: (jnp.maximum(cidx(i) * sub - 1, 0), 0)),
                  pl.BlockSpec((8, SSD_CONV_DIM), lambda i: (jnp.minimum((cidx(i) + 1) * sub, n_sub - 1), 0)),
                  pl.BlockSpec((L, LANES), lambda i: (cidx(i), 0)),
                  pl.BlockSpec((8, L), lambda i: (0, cidx(i))),
                  pl.BlockSpec((8, SSD_CONV_DIM), const),
                  pl.BlockSpec((1, SSD_CONV_DIM), const),
                  pl.BlockSpec((1, LANES), const),
                  pl.BlockSpec((1, LANES), const),
                  pl.BlockSpec((8, L), const),
                  pl.BlockSpec((8, L), const),
                  pl.BlockSpec((1, SSD_INNER), const)],
        out_specs=pl.BlockSpec((L, SSD_INNER), lambda i: (cidx(i), 0)),
        out_shape=jax.ShapeDtypeStruct((t, SSD_INNER), F32),
        scratch_shapes=[pltpu.VMEM((SSD_HEADS, SSD_HEAD_DIM, SSD_STATE), F32)],
        compiler_params=_cparams("arbitrary"),
        name="ssd_bwd" if reverse else "ssd_fwd",
    )(xbc_raw, xbc_raw, xbc_raw, dt_c, dt_r, cw, cb, bias_c, a_c, bias_r, a_r, dsk)


def _cmul(ar, ai, br, bi):
    return ar * br - ai * bi, ar * bi + ai * br


def _s5_in_body(u_ref, w_ref, lam_ref, efr_ref, efi_ref, ebr_ref, ebi_ref, xs_ref, bu_ref):
    tn = u_ref.shape[0]
    hs = S5_HALF_STATE
    for i in range(S5_CHUNK):
        xs_ref[i * tn:(i + 1) * tn, :] = u_ref[:, i, :]
    bu_ref[...] = _dot(xs_ref[...].astype(BF16), w_ref[0])
    lam = lam_ref[0]
    fr, fi, br, bi = lam[0:1], lam[1:2], lam[2:3], lam[3:4]
    ar = jnp.zeros((tn, hs), F32)
    ai = jnp.zeros((tn, hs), F32)
    for i in range(S5_CHUNK):
        ar, ai = _cmul(ar, ai, fr, fi)
        ar = ar + bu_ref[i * tn:(i + 1) * tn, 0:hs]
        ai = ai + bu_ref[i * tn:(i + 1) * tn, hs:2 * hs]
    efr_ref[...] = ar
    efi_ref[...] = ai
    ar = jnp.zeros((tn, hs), F32)
    ai = jnp.zeros((tn, hs), F32)
    for i in reversed(range(S5_CHUNK)):
        ar, ai = _cmul(ar, ai, br, bi)
        ar = ar + bu_ref[i * tn:(i + 1) * tn, 2 * hs:3 * hs]
        ai = ai + bu_ref[i * tn:(i + 1) * tn, 3 * hs:4 * hs]
    ebr_ref[...] = ar
    ebi_ref[...] = ai


def _s5_in_call(u3, w_b, lam_rows):
    nch = u3.shape[0]
    tn = _pick_tile(nch, (104, 80, 40))
    hs = S5_HALF_STATE
    col = lambda h, n: (n, h)
    return pl.pallas_call(
        _s5_in_body,
        grid=(2, nch // tn),
        in_specs=[pl.BlockSpec((tn, S5_CHUNK, LANES), lambda h, n: (n, 0, h)),
                  pl.BlockSpec((1, LANES, 4 * hs), lambda h, n: (h, 0, 0)),
                  pl.BlockSpec((1, 4, hs), lambda h, n: (h, 0, 0))],
        out_specs=[pl.BlockSpec((tn, hs), col)] * 4,
        out_shape=[jax.ShapeDtypeStruct((nch, 2 * hs), F32)] * 4,
        scratch_shapes=[pltpu.VMEM((tn * S5_CHUNK, LANES), F32), pltpu.VMEM((tn * S5_CHUNK, 4 * hs), F32)],
        compiler_params=_cparams("arbitrary", "arbitrary"),
        name="s5_chunk_in",
    )(u3, w_b, lam_rows)


def _s5_scan_body(a_ref, efr_ref, efi_ref, ebr_ref, ebi_ref, hfr_ref, hfi_ref, hbr_ref, hbi_ref, st_ref):
    @pl.when(pl.program_id(0) == 0)
    def _():
        st_ref[...] = jnp.zeros_like(st_ref)

    afr, afi, abr, abi = a_ref[0], a_ref[1], a_ref[2], a_ref[3]
    fr, fi, br, bi = st_ref[0], st_ref[1], st_ref[2], st_ref[3]
    for j in range(S5_SCAN_TILE):
        hfr_ref[j] = fr
        hfi_ref[j] = fi
        fr, fi = afr * fr - afi * fi + efr_ref[j], afr * fi + afi * fr + efi_ref[j]
        jb = S5_SCAN_TILE - 1 - j
        hbr_ref[jb] = br
        hbi_ref[jb] = bi
        br, bi = abr * br - abi * bi + ebr_ref[jb], abr * bi + abi * br + ebi_ref[jb]
    st_ref[0] = fr
    st_ref[1] = fi
    st_ref[2] = br
    st_ref[3] = bi


def _s5_scan_call(a_pow, efr, efi, ebr, ebi, n_lat_tiles):
    nch, sub, _ = efr.shape
    n_tiles = nch // S5_SCAN_TILE
    blk = (S5_SCAN_TILE, sub, LANES)
    fwd = lambda i: (_segment_order(i, n_lat_tiles, n_tiles, False), 0, 0)
    bwd = lambda i: (_segment_order(i, n_lat_tiles, n_tiles, True), 0, 0)
    shp = jax.ShapeDtypeStruct((nch, sub, LANES), F32)
    return pl.pallas_call(
        _s5_scan_body,
        grid=(n_tiles,),
        in_specs=[pl.BlockSpec((4, sub, LANES), lambda i: (0, 0, 0)),
                  pl.BlockSpec(blk, fwd), pl.BlockSpec(blk, fwd),
                  pl.BlockSpec(blk, bwd), pl.BlockSpec(blk, bwd)],
        out_specs=[pl.BlockSpec(blk, fwd), pl.BlockSpec(blk, fwd),
                   pl.BlockSpec(blk, bwd), pl.BlockSpec(blk, bwd)],
        out_shape=[shp] * 4,
        scratch_shapes=[pltpu.VMEM((4, sub, LANES), F32)],
        compiler_params=_cparams("arbitrary"),
        name="s5_state_scan",
    )(a_pow, efr, efi, ebr, ebi)


def _s5_out_body(u_ref, hfr_ref, hfi_ref, hbr_ref, hbi_ref, z_ref, c_ref, lam_ref, y_ref, gb_ref):
    hs = S5_HALF_STATE
    width = S5_CHUNK * LANES
    lam = lam_ref[0]
    fr, fi, br, bi = lam[0:1], lam[1:2], lam[2:3], lam[3:4]
    ytz = None
    for i in range(S5_CHUNK):
        xi = u_ref[:, i, :].astype(BF16)
        lo = (S5_CHUNK - 1 - i) * LANES
        part = _dot(xi, z_ref[0, :, lo:lo + width])
        ytz = part if ytz is None else ytz + part
    gr, gi = hbr_ref[...], hbi_ref[...]
    for j in reversed(range(S5_CHUNK)):
        gr, gi = _cmul(gr, gi, br, bi)
        gb_ref[j, :, 0:hs] = gr
        gb_ref[j, :, hs:] = gi
    cw = c_ref[0]
    gr, gi = hfr_ref[...], hfi_ref[...]
    for j in range(S5_CHUNK):
        gr, gi = _cmul(gr, gi, fr, fi)
        g_all = jnp.concatenate([gr, gi, gb_ref[j]], axis=1).astype(BF16)
        y_ref[:, j, :] = ytz[:, j * LANES:(j + 1) * LANES] + _dot(g_all, cw)


def _s5_out_call(u3, hfr, hfi, hbr, hbi, w_z, w_c, lam_rows):
    nch = u3.shape[0]
    tn = _pick_tile(nch, (208, 80, 40))
    hs = S5_HALF_STATE
    col = lambda h, n: (n, h)
    blk3 = pl.BlockSpec((tn, S5_CHUNK, LANES), lambda h, n: (n, 0, h))
    return pl.pallas_call(
        _s5_out_body,
        grid=(2, nch // tn),
        in_specs=[blk3] + [pl.BlockSpec((tn, hs), col)] * 4
                 + [pl.BlockSpec((1, LANES, S5_TOEPLITZ_BLOCKS * LANES), lambda h, n: (h, 0, 0)),
                    pl.BlockSpec((1, 4 * hs, LANES), lambda h, n: (h, 0, 0)),
                    pl.BlockSpec((1, 4, hs), lambda h, n: (h, 0, 0))],
        out_specs=blk3,
        out_shape=jax.ShapeDtypeStruct(u3.shape, F32),
        scratch_shapes=[pltpu.VMEM((S5_CHUNK, tn, 2 * hs), F32)],
        compiler_params=_cparams("arbitrary", "arbitrary"),
        name="s5_chunk_out",
    )(u3, hfr, hfi, hbr, hbi, w_z, w_c, lam_rows)


def _s5_weights(lam_re, lam_im, log_step, b_re, b_im, c_re, c_im):
    tc = S5_CHUNK
    nl = lam_re.shape[0]
    hg = S5_HALF_GROUPS
    ein = functools.partial(jnp.einsum, precision=HI)
    step = jnp.exp(log_step)[..., None]
    er, ei = lam_re * step, lam_im * step
    mag = jnp.exp(er)
    lbr, lbi = mag * jnp.cos(ei), mag * jnp.sin(ei)
    den = lam_re * lam_re + lam_im * lam_im
    nr, ni = lbr - 1.0, lbi
    cr, ci = (nr * lam_re + ni * lam_im) / den, (ni * lam_re - nr * lam_im) / den
    bbr = cr[..., None] * b_re - ci[..., None] * b_im
    bbi = cr[..., None] * b_im + ci[..., None] * b_re
    taus = jnp.arange(tc + 1, dtype=F32)[:, None, None]
    pmag = jnp.exp(er[:, :, None] * taus)
    pr, pi = pmag * jnp.cos(ei[:, :, None] * taus), pmag * jnp.sin(ei[:, :, None] * taus)

    wr = pr[:, :, :tc, :, :, None] * bbr[:, :, None] - pi[:, :, :tc, :, :, None] * bbi[:, :, None]
    wi = pr[:, :, :tc, :, :, None] * bbi[:, :, None] + pi[:, :, :tc, :, :, None] * bbr[:, :, None]
    kk = ein('ldgcp,ldtgpk->ldtgck', c_re, wr) - ein('ldgcp,ldtgpk->ldtgck', c_im, wi)
    eye = jnp.eye(hg, dtype=F32)
    kk = kk.reshape(nl, 2, tc, 2, hg, S5_GROUP, S5_GROUP)
    bd = jnp.einsum('ldthgck,gj->ldthgkjc', kk, eye).reshape(nl, 2, tc, 2, LANES, LANES)
    bdf, bdb = bd[:, 0], bd[:, 1]
    strip = jnp.concatenate([bdb[:, :0:-1], (bdf[:, :1] + bdb[:, :1]), bdf[:, 1:]], axis=1)
    w_z = strip.transpose(0, 2, 3, 1, 4).reshape(nl, 2, LANES, S5_TOEPLITZ_BLOCKS * LANES)

    planes_b = jnp.stack([bbr[:, 0], bbi[:, 0], bbr[:, 1], bbi[:, 1]], axis=1)
    planes_b = planes_b.reshape(nl, 4, 2, hg, S5_STATE, S5_GROUP)
    w_b = jnp.einsum('lqhgpc,gj->lhgcqjp', planes_b, eye).reshape(nl, 2, LANES, 4 * S5_HALF_STATE)
    planes_c = jnp.stack([c_re[:, 0], -c_im[:, 0], c_re[:, 1], -c_im[:, 1]], axis=1)
    planes_c = planes_c.reshape(nl, 4, 2, hg, S5_GROUP, S5_STATE)
    w_c = jnp.einsum('lqhgcp,gj->lhqjpgc', planes_c, eye).reshape(nl, 2, 4 * S5_HALF_STATE, LANES)

    halves = lambda a: a.reshape(nl, 2, S5_HALF_STATE)
    lam_rows = jnp.stack([halves(lbr[:, 0]), halves(lbi[:, 0]), halves(lbr[:, 1]), halves(lbi[:, 1])], axis=2)
    full = lambda a: a.reshape(nl, S5_GROUPS * S5_STATE // LANES, LANES)
    a_pow = jnp.stack([full(pr[:, 0, tc]), full(pi[:, 0, tc]), full(pr[:, 1, tc]), full(pi[:, 1, tc])], axis=1)
    return w_b.astype(BF16), w_z.astype(BF16), w_c.astype(BF16), lam_rows, a_pow


def _s5_mix(u, w_b, w_z, w_c, lam_rows, a_pow, n_lat):
    t = u.shape[0]
    nch = t // S5_CHUNK
    sub = S5_GROUPS * S5_STATE // LANES
    u3 = u.reshape(nch, S5_CHUNK, S5_WIDTH)
    es = _s5_in_call(u3, w_b, lam_rows)
    es = [e.reshape(nch, sub, LANES) for e in es]
    hs = _s5_scan_call(a_pow, *es, n_lat // (S5_CHUNK * S5_SCAN_TILE))
    hs = [h.reshape(nch, sub * LANES) for h in hs]
    return _s5_out_call(u3, *hs, w_z, w_c, lam_rows).reshape(t, S5_WIDTH)


def _attn_body(flag_ref, lam_ref, q_ref, k_ref, v_ref, g_ref, o_ref, acc_ref, m_ref, oh_ref, *, n_kv, post_scale):
    bounded = flag_ref[0] == 1

    def head(hh, carry):
        q = q_ref[hh]
        acc_ref[...] = jnp.zeros_like(acc_ref)

        @pl.when(bounded)
        def _():
            def step(b, carry2):
                v = v_ref[hh, b]
                for c in range(2):
                    p = jnp.exp2(_dot(q, k_ref[hh, b, c])).astype(BF16)
                    acc_ref[c] += _dot(p, v)
                return carry2
            lax.fori_loop(0, n_kv, step, 0)

        @pl.when(jnp.logical_not(bounded))
        def _():
            m_ref[...] = jnp.full_like(m_ref, -jnp.inf)

            def step(b, carry2):
                v = v_ref[hh, b]
                for c in range(2):
                    s = _dot(q, k_ref[hh, b, c])
                    m_old = m_ref[c]
                    m_new = jnp.maximum(m_old, jnp.max(s, axis=1, keepdims=True))
                    p = jnp.exp2(s - m_new[:, 0:1]).astype(BF16)
                    acc_ref[c] = jnp.exp2(m_old - m_new) * acc_ref[c] + _dot(p, v)
                    m_ref[c] = m_new
                return carry2
            lax.fori_loop(0, n_kv, step, 0)

        a0 = acc_ref[0]
        a1 = acc_ref[1]
        o = a0[:, :DA_V_DIM] / a0[:, DA_V_DIM:DA_V_DIM + 1] \
            - lam_ref[0] * (a1[:, :DA_V_DIM] / a1[:, DA_V_DIM:DA_V_DIM + 1])
        ms = jnp.mean(o * o, axis=-1, keepdims=True)
        oh_ref[hh] = o * lax.rsqrt(ms + EPS) * g_ref[...] * post_scale
        return carry

    lax.fori_loop(0, ATTN_HEADS_PER_STEP, head, 0)
    o_ref[...] = jnp.concatenate([oh_ref[hh] for hh in range(ATTN_HEADS_PER_STEP)], axis=1)


def _attn_call(flag, lam, qh, kzb, vb, g_sub, post_scale, q_rows, q_row0, kv_block, kv_cols, kv_col0):
    nh = qh.shape[0]
    hp = ATTN_HEADS_PER_STEP
    tk = kzb.shape[4]
    tq = _pick_tile(q_rows, (ATTN_TQ, 256))
    assert q_row0 % tq == 0
    q0 = q_row0 // tq
    if kv_block is None:
        n_kv = kzb.shape[1]
        one = pl.Buffered(1)
        k_spec = pl.BlockSpec((hp, n_kv, 2, 2 * DA_HEAD_DIM, tk), lambda h, i: (h, 0, 0, 0, 0), pipeline_mode=one)
        v_spec = pl.BlockSpec((hp, n_kv, tk, LANES), lambda h, i: (h, 0, 0, 0), pipeline_mode=one)
    else:
        assert kv_col0 % kv_cols == 0 and tk % kv_cols == 0
        n_kv, cb = 1, kv_col0 // kv_cols
        k_spec = pl.BlockSpec((hp, 1, 2, 2 * DA_HEAD_DIM, kv_cols), lambda h, i: (h, kv_block, 0, 0, cb))
        v_spec = pl.BlockSpec((hp, 1, kv_cols, LANES), lambda h, i: (h, kv_block, cb, 0))
    smem = pl.BlockSpec(memory_space=pltpu.SMEM)
    return pl.pallas_call(
        functools.partial(_attn_body, n_kv=n_kv, post_scale=post_scale),
        grid=(nh // hp, q_rows // tq),
        in_specs=[smem, smem,
                  pl.BlockSpec((hp, tq, 2 * DA_HEAD_DIM), lambda h, i: (h, q0 + i, 0)),
                  k_spec, v_spec,
                  pl.BlockSpec((1, DA_V_DIM), lambda h, i: (0, 0))],
        out_specs=pl.BlockSpec((tq, hp * DA_V_DIM), lambda h, i: (i, h)),
        out_shape=jax.ShapeDtypeStruct((q_rows, nh * DA_V_DIM), F32),
        scratch_shapes=[pltpu.VMEM((2, tq, LANES), F32), pltpu.VMEM((2, tq, LANES), F32),
                        pltpu.VMEM((hp, tq, DA_V_DIM), F32)],
        compiler_params=_cparams("arbitrary", "arbitrary"),
        name="diff_attn",
    )(flag, lam, qh, kzb, vb, g_sub)


def _gelu_tanh(x):
    return 0.5 * x * (1.0 + jnp.tanh(math.sqrt(2.0 / math.pi) * (x + 0.044715 * (x * x * x))))


def _outffn_body(n_lat_tiles, x_ref, yf_ref, yb_ref, z_ref, gssd_ref, ys5_ref, u_ref, d5_ref, gw_ref, gb_ref,
                 ydl_ref, ydc_ref, mod_ref, g2_ref, wo_ref, w1_ref, w3_ref, w2_ref, o_ref):
    mod = mod_ref[0]
    y = (yf_ref[...] + yb_ref[...]) * _silu(z_ref[...])
    y_ssd = y * lax.rsqrt(jnp.mean(y * y, axis=-1, keepdims=True) + EPS) * gssd_ref[...]
    y5 = _gelu_tanh(ys5_ref[...] + d5_ref[...] * u_ref[...])
    y5 = y5 * jax.nn.sigmoid(_dot(y5, gw_ref[...], HI) + gb_ref[...])
    y_da = jnp.where(pl.program_id(0) >= n_lat_tiles, ydc_ref[...], ydl_ref[...])
    wo = wo_ref[...]
    o = _dot(y_ssd.astype(BF16), wo[0:256])
    o += _dot(y5.astype(BF16), wo[256:512])
    o += _dot(y_da.astype(BF16), wo[512:1024])
    x = x_ref[...] + mod[2:3] * o
    h = x * lax.rsqrt(jnp.mean(x * x, axis=-1, keepdims=True) + EPS) * g2_ref[...] * (1.0 + mod[4:5]) + mod[3:4]
    hb = h.astype(BF16)
    f = _silu(_dot(hb, w1_ref[...])) * _dot(hb, w3_ref[...])
    o_ref[...] = x + mod[5:6] * _dot(f.astype(BF16), w2_ref[...])


def _outffn_call(xs, y_f, y_b, z, g_ssd, y_s5, u, d5, glu_w, glu_b, y_da_l, y_da_c, modl, g2, w_o, w1, w3, w2,
                 n_lat_tiles):
    t, d = xs.shape
    dff = w1.shape[1]
    tm = ROW_TILE
    row = lambda i: (i, 0)
    const = lambda i: (0, 0)
    seg = lambda i: ((i >= n_lat_tiles).astype(jnp.int32), 0, 0)
    one = pl.Buffered(1)
    return pl.pallas_call(
        functools.partial(_outffn_body, n_lat_tiles),
        grid=(t // tm,),
        in_specs=[pl.BlockSpec((tm, d), row),
                  pl.BlockSpec((tm, 256), row), pl.BlockSpec((tm, 256), row), pl.BlockSpec((tm, 256), row),
                  pl.BlockSpec((1, 256), const),
                  pl.BlockSpec((tm, 256), row), pl.BlockSpec((tm, 256), row),
                  pl.BlockSpec((1, 256), const),
                  pl.BlockSpec((256, 256), const), pl.BlockSpec((1, 256), const),
                  pl.BlockSpec((tm, DA_WIDTH), lambda i: (jnp.minimum(i, n_lat_tiles - 1), 0)),
                  pl.BlockSpec((tm, DA_WIDTH), lambda i: (jnp.maximum(i - n_lat_tiles, 0), 0)),
                  pl.BlockSpec((1, 8, d), seg),
                  pl.BlockSpec((1, d), const),
                  pl.BlockSpec((d, d), const, pipeline_mode=one),
                  pl.BlockSpec((d, dff), const, pipeline_mode=one),
                  pl.BlockSpec((d, dff), const, pipeline_mode=one),
                  pl.BlockSpec((dff, d), const, pipeline_mode=one)],
        out_specs=pl.BlockSpec((tm, d), row),
        out_shape=jax.ShapeDtypeStruct((t, d), F32),
        compiler_params=_cparams("arbitrary"),
        name="out_proj_ffn",
    )(xs, y_f, y_b, z, g_ssd, y_s5, u, d5, glu_w, glu_b, y_da_l, y_da_c, modl, g2, w_o, w1, w3, w2)


def _rope_tables(n_lat, n_ctx):
    pos = jnp.arange(n_lat)
    lane = jnp.arange(LANES)
    d = lane % DA_HEAD_DIM
    axis = d // (2 * ROPE_FREQS)
    half = (d % (2 * ROPE_FREQS)) // ROPE_FREQS
    freqs = ROPE_THETA ** (-jnp.arange(ROPE_FREQS, dtype=F32) / ROPE_FREQS)
    coord = jnp.where(axis[None, :] == 0, (pos // GRID_W)[:, None], (pos % GRID_W)[:, None]).astype(F32)
    ang = coord * freqs[d % ROPE_FREQS][None, :]
    cos, sin = jnp.cos(ang), jnp.sin(ang)
    sa = jnp.where(half[None, :] == 0, -sin, 0.0)
    sb = jnp.where(half[None, :] == 1, sin, 0.0)
    pad = lambda a, v: jnp.concatenate([a, jnp.full((n_ctx, LANES), v, F32)], axis=0)
    return pad(cos, 1.0), pad(sa, 0.0), pad(sb, 0.0)


def _lane_row(vals, width=LANES):
    vals = vals.reshape(-1).astype(F32)
    return jnp.zeros((1, width), F32).at[0, :vals.shape[0]].set(vals)


def kernel(x, c, ctx, c_ctx, w_mod, b_mod, norm1, norm2, w_in, w_out, ssd_conv_w, ssd_conv_b, ssd_a_log, ssd_dt_bias, ssd_d, ssd_norm, s5_lam_re, s5_lam_im, s5_log_step, s5_b_re, s5_b_im, s5_c_re, s5_c_im, s5_d, s5_glu_w, s5_glu_b, da_q_norm, da_k_norm, da_lambda, da_sub_norm, ffn_w1, ffn_w3, ffn_w2):
    depth = w_mod.shape[0]
    bsz, n_lat, d = x.shape
    n_ctx = ctx.shape[1]
    t = n_ctx + n_lat
    tk = _pick_tile(t, (ATTN_TK, 1024, 640, 512, 256))
    assert bsz == 1 and n_ctx % ROW_TILE == 0 and n_lat % ROW_TILE == 0 and tk % ROW_TILE == 0
    assert n_ctx % (S5_CHUNK * S5_SCAN_TILE) == 0 and n_lat % (S5_CHUNK * S5_SCAN_TILE) == 0
    assert n_lat % n_ctx == 0 and tk % n_ctx == 0
    n_lat_tiles = n_lat // ROW_TILE

    xs = jnp.concatenate([x[0], ctx[0]], axis=0)
    cv = jnp.zeros((8, d), F32).at[0].set(c[0]).at[1].set(c_ctx)
    mod_all = _mod_call(cv, w_mod, b_mod)
    cos_t, sa_t, sb_t = _rope_tables(n_lat, n_ctx)
    gidx = jnp.arange(DA_WIDTH) // DA_HEAD_DIM
    gmat = (gidx[:, None] == gidx[None, :]).astype(BF16) * (1.0 / DA_HEAD_DIM)
    s5_w = _s5_weights(s5_lam_re, s5_lam_im, s5_log_step, s5_b_re, s5_b_im, s5_c_re, s5_c_im)

    for i in range(depth):
        lam_init = 0.8 - 0.6 * math.exp(-0.3 * i)
        modl = jnp.zeros((2, 8, d), F32).at[:, :6].set(mod_all[i, :2].reshape(2, 6, d))

        wi = w_in[i]
        w_r = jnp.concatenate([wi[:, 0:768], wi[:, 776:1032], wi[:, 1032:2568], wi[:, 768:776],
                               jnp.zeros((d, IN_PAD - 2568), F32)], axis=1).astype(BF16)
        gq = jnp.tile(da_q_norm[i], DA_WIDTH // DA_HEAD_DIM)[None, :]
        gk = jnp.tile(da_k_norm[i], DA_WIDTH // DA_HEAD_DIM)[None, :]
        z, xbc_raw, u, dt_raw, qh, kzb, vb = _inproj_call(
            xs, norm1[i][None, :], modl, w_r, gq, gk, gmat, cos_t, sa_t, sb_t, n_lat_tiles, tk)

        dt_r = dt_raw[:, :8].T
        cw = jnp.zeros((8, SSD_CONV_DIM), F32).at[:3].set(ssd_conv_w[i])
        a_neg = -jnp.exp(ssd_a_log[i])
        bias_c = _lane_row(ssd_dt_bias[i])
        a_c = _lane_row(a_neg)
        bias_r = jnp.broadcast_to(ssd_dt_bias[i].reshape(8, 1), (8, SSD_CHUNK))
        a_r = jnp.broadcast_to(a_neg.reshape(8, 1), (8, SSD_CHUNK))
        dsk = jnp.repeat(ssd_d[i], SSD_HEAD_DIM)[None, :]
        ssd_args = (xbc_raw, dt_raw, dt_r, cw, ssd_conv_b[i][None, :], bias_c, a_c, bias_r, a_r, dsk,
                    n_lat // SSD_CHUNK)
        y_f = _ssd_call(*ssd_args, reverse=False)
        y_b = _ssd_call(*ssd_args, reverse=True)

        y_s5 = _s5_mix(u, *(w[i] for w in s5_w), n_lat)

        lf = da_lambda[i]
        lam = (jnp.exp(jnp.sum(lf[0] * lf[1])) - jnp.exp(jnp.sum(lf[2] * lf[3])) + lam_init).reshape(1)
        g_sub = da_sub_norm[i][None, :]
        score_bound = LOG2E * math.sqrt(DA_HEAD_DIM) * jnp.max(jnp.abs(da_q_norm[i])) * jnp.max(jnp.abs(da_k_norm[i]))
        flag = (score_bound <= SCORE_LOG2_LIMIT).astype(jnp.int32).reshape(1)
        y_da_l = _attn_call(flag, lam, qh, kzb, vb, g_sub, 1.0 - lam_init, n_lat, 0, None, None, None)
        y_da_c = _attn_call(flag, lam, qh, kzb, vb, g_sub, 1.0 - lam_init, n_ctx, n_lat,
                            n_lat // tk, n_ctx, n_lat % tk)

        xs = _outffn_call(xs, y_f, y_b, z, ssd_norm[i][None, :], y_s5, u, s5_d[i][None, :], s5_glu_w[i],
                          s5_glu_b[i][None, :], y_da_l, y_da_c, modl, norm2[i][None, :], w_out[i].astype(BF16),
                          ffn_w1[i].astype(BF16), ffn_w3[i].astype(BF16), ffn_w2[i].astype(BF16),
                          n_lat_tiles)
    return xs[:n_lat][None]
```

```python
import functools
import math

import jax
import jax.numpy as jnp
from jax import lax
from jax.experimental import pallas as pl
from jax.experimental.pallas import tpu as pltpu

F32 = jnp.float32
BF16 = jnp.bfloat16
HI = lax.Precision.HIGHEST

EPS = 1e-6
GRID_W = 64
ROPE_THETA = 10000.0
ROPE_FREQS = 8

SSD_HEADS = 4
SSD_HEAD_DIM = 64
SSD_INNER = 256
SSD_GROUPS = 2
SSD_STATE = 64
SSD_CHUNK = 128
SSD_CONV_DIM = 512

S5_WIDTH = 256
S5_GROUP = 16
S5_GROUPS = 16
S5_STATE = 64
S5_CHUNK = 16
S5_HALF_GROUPS = S5_GROUPS // 2
S5_HALF_STATE = S5_HALF_GROUPS * S5_STATE
S5_SCAN_TILE = 16
S5_TOEPLITZ_BLOCKS = 2 * S5_CHUNK - 1

DA_HEADS = 8
DA_HEAD_DIM = 32
DA_V_DIM = 64
DA_WIDTH = 512
ATTN_HEADS_PER_STEP = 2
ATTN_TQ = 1024
ATTN_TK = 1280
LOG2E = 1.4426950408889634
SCORE_LOG2_LIMIT = 100.0

LANES = 128
ROW_TILE = 256
IN_PAD = 2688
VMEM_LIMIT = 56 * 1024 * 1024


def _cparams(*sem):
    return pltpu.CompilerParams(dimension_semantics=sem, vmem_limit_bytes=VMEM_LIMIT)


def _dot(a, b, precision=None):
    return jnp.dot(a, b, preferred_element_type=F32, precision=precision)


def _dot_nt(a, b, precision=None):
    return lax.dot_general(a, b, (((1,), (1,)), ((), ())), preferred_element_type=F32,
                           precision=precision)


def _dot_tn(a, b, precision=None):
    return lax.dot_general(a, b, (((0,), (0,)), ((), ())), preferred_element_type=F32,
                           precision=precision)


def _silu(x):
    return x * jax.nn.sigmoid(x)


def _pick_tile(n, candidates):
    for c in candidates:
        if n % c == 0:
            return c
    return n


def _segment_order(i, n_lat, n_all, reverse):
    n_ctx = n_all - n_lat
    if not reverse:
        return jnp.where(i < n_ctx, n_lat + i, i - n_ctx)
    return jnp.where(i < n_ctx, n_all - 1 - i, n_lat - 1 - (i - n_ctx))


def _mod_body(cv_ref, w_ref, b_ref, o_ref):
    o_ref[0] = _dot(_silu(cv_ref[...]), w_ref[0], HI) + b_ref[0]


def _mod_call(cv, w_mod, b_mod):
    depth, d, n = w_mod.shape
    tn = 1024
    return pl.pallas_call(
        _mod_body,
        grid=(depth, n // tn),
        in_specs=[pl.BlockSpec((8, d), lambda l, j: (0, 0)),
                  pl.BlockSpec((1, d, tn), lambda l, j: (l, 0, j)),
                  pl.BlockSpec((1, 1, tn), lambda l, j: (l, 0, j))],
        out_specs=pl.BlockSpec((1, 8, tn), lambda l, j: (l, 0, j)),
        out_shape=jax.ShapeDtypeStruct((depth, 8, n), F32),
        compiler_params=_cparams("arbitrary", "arbitrary"),
        name="adaln_mod",
    )(cv, w_mod, b_mod.reshape(depth, 1, n))


def _inproj_body(x_ref, g_ref, mod_ref, w_ref, gq_ref, gk_ref, gm_ref, cos_ref, sa_ref, sb_ref,
                 z_ref, xbc_ref, u_ref, dt_ref, q_ref, k_ref, v_ref):
    x = x_ref[...]
    rows = x.shape[0]
    mod = mod_ref[0]
    ms = jnp.mean(x * x, axis=-1, keepdims=True)
    h = x * lax.rsqrt(ms + EPS) * g_ref[...] * (1.0 + mod[1:2]) + mod[0:1]
    p = _dot(h.astype(BF16), w_ref[...])
    z_ref[...] = p[:, 0:256]
    xbc_ref[...] = p[:, 256:768]
    u_ref[...] = p[:, 768:1024]
    dt_ref[...] = p[:, 2560:2688]

    gm = gm_ref[...]
    cos = cos_ref[...]
    sa = sa_ref[...]
    sb = sb_ref[...]

    def norm_rope(t, gain, scale):
        t2 = t * t
        hi = t2.astype(BF16)
        lo = (t2 - hi.astype(F32)).astype(BF16)
        ms32 = _dot(hi, gm) + _dot(lo, gm)
        tn = t * lax.rsqrt(ms32 + EPS) * gain
        outs = []
        for j in range(DA_WIDTH // LANES):
            tb = tn[:, j * LANES:(j + 1) * LANES]
            ob = tb * cos + pltpu.roll(tb, LANES - ROPE_FREQS, 1) * sa + pltpu.roll(tb, ROPE_FREQS, 1) * sb
            outs.append(ob * scale)
        return jnp.concatenate(outs, axis=1)

    qn = norm_rope(p[:, 1024:1536], gq_ref[...], LOG2E * DA_HEAD_DIM ** -0.5).astype(BF16)
    kt = norm_rope(p[:, 1536:2048], gk_ref[...], 1.0).T.astype(BF16)
    vv = p[:, 2048:2560].astype(BF16)
    lane = lax.broadcasted_iota(jnp.int32, (rows, LANES - DA_V_DIM), 1)
    ones_col = jnp.where(lane == 0, 1.0, 0.0).astype(BF16)
    zeros_k = jnp.zeros((DA_HEAD_DIM, rows), BF16)
    for hd in range(DA_HEADS):
        lo_, mid, hi_ = hd * DA_V_DIM, hd * DA_V_DIM + DA_HEAD_DIM, (hd + 1) * DA_V_DIM
        q_ref[hd] = qn[:, lo_:hi_]
        k_ref[hd, 0, 0, 0:DA_HEAD_DIM, :] = kt[lo_:mid]
        k_ref[hd, 0, 0, DA_HEAD_DIM:, :] = zeros_k
        k_ref[hd, 0, 1, 0:DA_HEAD_DIM, :] = zeros_k
        k_ref[hd, 0, 1, DA_HEAD_DIM:, :] = kt[mid:hi_]
        v_ref[hd, 0] = jnp.concatenate([vv[:, lo_:hi_], ones_col], axis=1)


def _inproj_call(xs, g1, modl, w_r, gq, gk, gmat, cos_t, sa_t, sb_t, n_lat_tiles, tk):
    t, d = xs.shape
    tm = ROW_TILE
    per = tk // tm
    row = lambda i: (i, 0)
    const = lambda i: (0, 0)
    seg = lambda i: ((i >= n_lat_tiles).astype(jnp.int32), 0, 0)
    flat = [(256, F32), (512, F32), (256, F32), (LANES, F32)]
    return pl.pallas_call(
        _inproj_body,
        grid=(t // tm,),
        in_specs=[pl.BlockSpec((tm, d), row),
                  pl.BlockSpec((1, d), const),
                  pl.BlockSpec((1, 8, d), seg),
                  pl.BlockSpec((d, IN_PAD), const),
                  pl.BlockSpec((1, DA_WIDTH), const),
                  pl.BlockSpec((1, DA_WIDTH), const),
                  pl.BlockSpec((DA_WIDTH, DA_WIDTH), const),
                  pl.BlockSpec((tm, LANES), row),
                  pl.BlockSpec((tm, LANES), row),
                  pl.BlockSpec((tm, LANES), row)],
        out_specs=[pl.BlockSpec((tm, w), row) for w, _ in flat]
                  + [pl.BlockSpec((DA_HEADS, tm, 2 * DA_HEAD_DIM), lambda i: (0, i, 0)),
                     pl.BlockSpec((DA_HEADS, 1, 2, 2 * DA_HEAD_DIM, tm), lambda i: (0, i // per, 0, 0, i % per)),
                     pl.BlockSpec((DA_HEADS, 1, tm, LANES), lambda i: (0, i // per, i % per, 0))],
        out_shape=[jax.ShapeDtypeStruct((t, w), dt) for w, dt in flat]
                  + [jax.ShapeDtypeStruct((DA_HEADS, t, 2 * DA_HEAD_DIM), BF16),
                     jax.ShapeDtypeStruct((DA_HEADS, t // tk, 2, 2 * DA_HEAD_DIM, tk), BF16),
                     jax.ShapeDtypeStruct((DA_HEADS, t // tk, tk, LANES), BF16)],
        compiler_params=_cparams("arbitrary"),
        name="in_proj",
    )(xs, g1, modl, w_r, gq, gk, gmat, cos_t, sa_t, sb_t)


def _softplus(x):
    return jnp.maximum(x, 0.0) + jnp.log1p(jnp.exp(-jnp.abs(x)))


def _ssd_chunk(c, n_lat, n_chunks, reverse, xc_ref, xp_ref, xn_ref, dtc_ref, dtr_ref, cw_ref, cb_ref,
               bias_c_ref, a_c_ref, bias_r_ref, a_r_ref, st_ref):
    L = SSD_CHUNK
    x = xc_ref[...]
    seg_first = jnp.logical_or(c == 0, c == n_lat)
    seg_last = jnp.logical_or(c == n_lat - 1, c == n_chunks - 1)
    prow = jnp.where(seg_first, 0.0, xp_ref[7:8, :])
    nrow = jnp.where(seg_last, 0.0, xn_ref[0:1, :])
    ridx = lax.broadcasted_iota(jnp.int32, (L, 1), 0)
    xprev = jnp.where(ridx == 0, prow, pltpu.roll(x, 1, 0))
    xnext = jnp.where(ridx == L - 1, nrow, pltpu.roll(x, L - 1, 0))
    cw = cw_ref[...]
    conv = xprev * cw[0:1] + x * cw[1:2] + xnext * cw[2:3] + cb_ref[...]
    xbc = _silu(conv)

    off = SSD_HEADS if reverse else 0
    dt_c = _softplus(dtc_ref[...] + bias_c_ref[...])
    dt_r = _softplus(dtr_ref[...] + bias_r_ref[...])
    adt_c = dt_c * a_c_ref[...]
    adt_r = dt_r * a_r_ref[...]

    li = lax.broadcasted_iota(jnp.int32, (L, L), 0)
    si = lax.broadcasted_iota(jnp.int32, (L, L), 1)
    mask = (li <= si) if reverse else (li >= si)
    mask_t = (li >= si) if reverse else (li <= si)
    cs_c = _dot(mask.astype(F32), adt_c, HI)
    cs_r = _dot(adt_r, mask_t.astype(F32), HI)
    edge = 0 if reverse else L - 1

    ys = []
    for g in range(SSD_GROUPS):
        bg = xbc[:, SSD_INNER + g * SSD_STATE:SSD_INNER + (g + 1) * SSD_STATE].astype(BF16)
        cg = xbc[:, SSD_INNER + (SSD_GROUPS + g) * SSD_STATE:SSD_INNER + (SSD_GROUPS + g + 1) * SSD_STATE]
        cg = cg.astype(BF16)
        gmat = _dot_nt(cg, bg)
        for hh in range(SSD_HEADS // SSD_GROUPS):
            h = g * (SSD_HEADS // SSD_GROUPS) + hh
            col = off + h
            csc = cs_c[:, col:col + 1]
            csr = cs_r[col:col + 1, :]
            dec = jnp.exp(jnp.where(mask, csc - csr, -jnp.inf))
            xh = xbc[:, h * SSD_HEAD_DIM:(h + 1) * SSD_HEAD_DIM]
            xd = xh * dt_c[:, col:col + 1]
            st = st_ref[h]
            y = _dot((gmat * dec).astype(BF16), xd.astype(BF16))
            y += jnp.exp(csc) * _dot_nt(cg, st.astype(BF16))
            tot = cs_c[edge:edge + 1, col:col + 1]
            xdt = (xd * jnp.exp(tot - csc)).T.astype(BF16)
            st_ref[h] = st * jnp.exp(tot) + _dot(xdt, bg)
            ys.append(y)
    return jnp.concatenate(ys, axis=1), xbc[:, :SSD_INNER]


def _ssd_body(n_lat, n_chunks, *refs):
    fwd_in, bwd_in = refs[0:5], refs[5:10]
    cw_ref, cb_ref, bias_c_ref, a_c_ref, bias_r_ref, a_r_ref, dsk_ref, yf_ref, yb_ref, st_ref = refs[10:]
    i = pl.program_id(0)

    @pl.when(i == 0)
    def _():
        st_ref[...] = jnp.zeros_like(st_ref)

    shared = (cw_ref, cb_ref, bias_c_ref, a_c_ref, bias_r_ref, a_r_ref)
    y, xs = _ssd_chunk(_segment_order(i, n_lat, n_chunks, False), n_lat, n_chunks, False, *fwd_in, *shared,
                       st_ref.at[0])
    yf_ref[...] = y + dsk_ref[...] * xs
    y, _ = _ssd_chunk(_segment_order(i, n_lat, n_chunks, True), n_lat, n_chunks, True, *bwd_in, *shared,
                      st_ref.at[1])
    yb_ref[...] = y


def _ssd_call(xbc_raw, dt_c, dt_r, cw, cb, bias_c, a_c, bias_r, a_r, dsk, n_lat):
    t = xbc_raw.shape[0]
    L = SSD_CHUNK
    n_chunks = t // L
    sub = L // 8
    n_sub = t // 8
    const = lambda i: (0, 0)

    def chunk_specs(reverse):
        cidx = functools.partial(_segment_order, n_lat=n_lat, n_all=n_chunks, reverse=reverse)
        return [pl.BlockSpec((L, SSD_CONV_DIM), lambda i: (cidx(i), 0)),
                pl.BlockSpec((8, SSD_CONV_DIM), lambda i: (jnp.maximum(cidx(i) * sub - 1, 0), 0)),
                pl.BlockSpec((8, SSD_CONV_DIM), lambda i: (jnp.minimum((cidx(i) + 1) * sub, n_sub - 1), 0)),
                pl.BlockSpec((L, LANES), lambda i: (cidx(i), 0)),
                pl.BlockSpec((8, L), lambda i: (0, cidx(i)))]

    def out_spec(reverse):
        cidx = functools.partial(_segment_order, n_lat=n_lat, n_all=n_chunks, reverse=reverse)
        return pl.BlockSpec((L, SSD_INNER), lambda i: (cidx(i), 0))

    chunk_args = (xbc_raw, xbc_raw, xbc_raw, dt_c, dt_r)
    return pl.pallas_call(
        functools.partial(_ssd_body, n_lat, n_chunks),
        grid=(n_chunks,),
        in_specs=chunk_specs(False) + chunk_specs(True)
                 + [pl.BlockSpec((8, SSD_CONV_DIM), const),
                    pl.BlockSpec((1, SSD_CONV_DIM), const),
                    pl.BlockSpec((1, LANES), const),
                    pl.BlockSpec((1, LANES), const),
                    pl.BlockSpec((8, L), const),
                    pl.BlockSpec((8, L), const),
                    pl.BlockSpec((1, SSD_INNER), const)],
        out_specs=[out_spec(False), out_spec(True)],
        out_shape=[jax.ShapeDtypeStruct((t, SSD_INNER), F32)] * 2,
        scratch_shapes=[pltpu.VMEM((2, SSD_HEADS, SSD_HEAD_DIM, SSD_STATE), F32)],
        compiler_params=_cparams("arbitrary"),
        name="ssd_scan",
    )(*chunk_args, *chunk_args, cw, cb, bias_c, a_c, bias_r, a_r, dsk)


def _cmul(ar, ai, br, bi):
    return ar * br - ai * bi, ar * bi + ai * br


def _s5_in_body(u_ref, w_ref, lam_ref, efr_ref, efi_ref, ebr_ref, ebi_ref, xs_ref, bu_ref):
    tn = u_ref.shape[0]
    hs = S5_HALF_STATE
    for i in range(S5_CHUNK):
        xs_ref[i * tn:(i + 1) * tn, :] = u_ref[:, i, :]
    bu_ref[...] = _dot(xs_ref[...].astype(BF16), w_ref[0])
    lam = lam_ref[0]
    fr, fi, br, bi = lam[0:1], lam[1:2], lam[2:3], lam[3:4]
    ar = jnp.zeros((tn, hs), F32)
    ai = jnp.zeros((tn, hs), F32)
    for i in range(S5_CHUNK):
        ar, ai = _cmul(ar, ai, fr, fi)
        ar = ar + bu_ref[i * tn:(i + 1) * tn, 0:hs]
        ai = ai + bu_ref[i * tn:(i + 1) * tn, hs:2 * hs]
    efr_ref[...] = ar
    efi_ref[...] = ai
    ar = jnp.zeros((tn, hs), F32)
    ai = jnp.zeros((tn, hs), F32)
    for i in reversed(range(S5_CHUNK)):
        ar, ai = _cmul(ar, ai, br, bi)
        ar = ar + bu_ref[i * tn:(i + 1) * tn, 2 * hs:3 * hs]
        ai = ai + bu_ref[i * tn:(i + 1) * tn, 3 * hs:4 * hs]
    ebr_ref[...] = ar
    ebi_ref[...] = ai


def _s5_in_call(u3, w_b, lam_rows):
    nch = u3.shape[0]
    tn = _pick_tile(nch, (104, 80, 40))
    hs = S5_HALF_STATE
    col = lambda h, n: (n, h)
    return pl.pallas_call(
        _s5_in_body,
        grid=(2, nch // tn),
        in_specs=[pl.BlockSpec((tn, S5_CHUNK, LANES), lambda h, n: (n, 0, h)),
                  pl.BlockSpec((1, LANES, 4 * hs), lambda h, n: (h, 0, 0)),
                  pl.BlockSpec((1, 4, hs), lambda h, n: (h, 0, 0))],
        out_specs=[pl.BlockSpec((tn, hs), col)] * 4,
        out_shape=[jax.ShapeDtypeStruct((nch, 2 * hs), F32)] * 4,
        scratch_shapes=[pltpu.VMEM((tn * S5_CHUNK, LANES), F32), pltpu.VMEM((tn * S5_CHUNK, 4 * hs), F32)],
        compiler_params=_cparams("arbitrary", "arbitrary"),
        name="s5_chunk_in",
    )(u3, w_b, lam_rows)


def _s5_scan_body(a_ref, efr_ref, efi_ref, ebr_ref, ebi_ref, hfr_ref, hfi_ref, hbr_ref, hbi_ref, st_ref):
    @pl.when(pl.program_id(0) == 0)
    def _():
        st_ref[...] = jnp.zeros_like(st_ref)

    afr, afi, abr, abi = a_ref[0], a_ref[1], a_ref[2], a_ref[3]
    fr, fi, br, bi = st_ref[0], st_ref[1], st_ref[2], st_ref[3]
    for j in range(S5_SCAN_TILE):
        hfr_ref[j] = fr
        hfi_ref[j] = fi
        fr, fi = afr * fr - afi * fi + efr_ref[j], afr * fi + afi * fr + efi_ref[j]
        jb = S5_SCAN_TILE - 1 - j
        hbr_ref[jb] = br
        hbi_ref[jb] = bi
        br, bi = abr * br - abi * bi + ebr_ref[jb], abr * bi + abi * br + ebi_ref[jb]
    st_ref[0] = fr
    st_ref[1] = fi
    st_ref[2] = br
    st_ref[3] = bi


def _s5_scan_call(a_pow, efr, efi, ebr, ebi, n_lat_tiles):
    nch, sub, _ = efr.shape
    n_tiles = nch // S5_SCAN_TILE
    blk = (S5_SCAN_TILE, sub, LANES)
    fwd = lambda i: (_segment_order(i, n_lat_tiles, n_tiles, False), 0, 0)
    bwd = lambda i: (_segment_order(i, n_lat_tiles, n_tiles, True), 0, 0)
    shp = jax.ShapeDtypeStruct((nch, sub, LANES), F32)
    return pl.pallas_call(
        _s5_scan_body,
        grid=(n_tiles,),
        in_specs=[pl.BlockSpec((4, sub, LANES), lambda i: (0, 0, 0)),
                  pl.BlockSpec(blk, fwd), pl.BlockSpec(blk, fwd),
                  pl.BlockSpec(blk, bwd), pl.BlockSpec(blk, bwd)],
        out_specs=[pl.BlockSpec(blk, fwd), pl.BlockSpec(blk, fwd),
                   pl.BlockSpec(blk, bwd), pl.BlockSpec(blk, bwd)],
        out_shape=[shp] * 4,
        scratch_shapes=[pltpu.VMEM((4, sub, LANES), F32)],
        compiler_params=_cparams("arbitrary"),
        name="s5_state_scan",
    )(a_pow, efr, efi, ebr, ebi)


def _s5_out_body(u_ref, hfr_ref, hfi_ref, hbr_ref, hbi_ref, z_ref, c_ref, lam_ref, y_ref, gb_ref):
    hs = S5_HALF_STATE
    width = S5_CHUNK * LANES
    lam = lam_ref[0]
    fr, fi, br, bi = lam[0:1], lam[1:2], lam[2:3], lam[3:4]
    ytz = None
    for i in range(S5_CHUNK):
        xi = u_ref[:, i, :].astype(BF16)
        lo = (S5_CHUNK - 1 - i) * LANES
        part = _dot(xi, z_ref[0, :, lo:lo + width])
        ytz = part if ytz is None else ytz + part
    gr, gi = hbr_ref[...], hbi_ref[...]
    for j in reversed(range(S5_CHUNK)):
        gr, gi = _cmul(gr, gi, br, bi)
        gb_ref[j, :, 0:hs] = gr
        gb_ref[j, :, hs:] = gi
    cw = c_ref[0]
    gr, gi = hfr_ref[...], hfi_ref[...]
    for j in range(S5_CHUNK):
        gr, gi = _cmul(gr, gi, fr, fi)
        g_all = jnp.concatenate([gr, gi, gb_ref[j]], axis=1).astype(BF16)
        y_ref[:, j, :] = ytz[:, j * LANES:(j + 1) * LANES] + _dot(g_all, cw)


def _s5_out_call(u3, hfr, hfi, hbr, hbi, w_z, w_c, lam_rows):
    nch = u3.shape[0]
    tn = _pick_tile(nch, (208, 80, 40))
    hs = S5_HALF_STATE
    col = lambda h, n: (n, h)
    blk3 = pl.BlockSpec((tn, S5_CHUNK, LANES), lambda h, n: (n, 0, h))
    return pl.pallas_call(
        _s5_out_body,
        grid=(2, nch // tn),
        in_specs=[blk3] + [pl.BlockSpec((tn, hs), col)] * 4
                 + [pl.BlockSpec((1, LANES, S5_TOEPLITZ_BLOCKS * LANES), lambda h, n: (h, 0, 0)),
                    pl.BlockSpec((1, 4 * hs, LANES), lambda h, n: (h, 0, 0)),
                    pl.BlockSpec((1, 4, hs), lambda h, n: (h, 0, 0))],
        out_specs=blk3,
        out_shape=jax.ShapeDtypeStruct(u3.shape, F32),
        scratch_shapes=[pltpu.VMEM((S5_CHUNK, tn, 2 * hs), F32)],
        compiler_params=_cparams("arbitrary", "arbitrary"),
        name="s5_chunk_out",
    )(u3, hfr, hfi, hbr, hbi, w_z, w_c, lam_rows)


def _s5_weights(lam_re, lam_im, log_step, b_re, b_im, c_re, c_im):
    tc = S5_CHUNK
    nl = lam_re.shape[0]
    hg = S5_HALF_GROUPS
    ein = functools.partial(jnp.einsum, precision=HI)
    step = jnp.exp(log_step)[..., None]
    er, ei = lam_re * step, lam_im * step
    mag = jnp.exp(er)
    lbr, lbi = mag * jnp.cos(ei), mag * jnp.sin(ei)
    den = lam_re * lam_re + lam_im * lam_im
    nr, ni = lbr - 1.0, lbi
    cr, ci = (nr * lam_re + ni * lam_im) / den, (ni * lam_re - nr * lam_im) / den
    bbr = cr[..., None] * b_re - ci[..., None] * b_im
    bbi = cr[..., None] * b_im + ci[..., None] * b_re
    taus = jnp.arange(tc + 1, dtype=F32)[:, None, None]
    pmag = jnp.exp(er[:, :, None] * taus)
    pr, pi = pmag * jnp.cos(ei[:, :, None] * taus), pmag * jnp.sin(ei[:, :, None] * taus)

    wr = pr[:, :, :tc, :, :, None] * bbr[:, :, None] - pi[:, :, :tc, :, :, None] * bbi[:, :, None]
    wi = pr[:, :, :tc, :, :, None] * bbi[:, :, None] + pi[:, :, :tc, :, :, None] * bbr[:, :, None]
    kk = ein('ldgcp,ldtgpk->ldtgck', c_re, wr) - ein('ldgcp,ldtgpk->ldtgck', c_im, wi)
    eye = jnp.eye(hg, dtype=F32)
    kk = kk.reshape(nl, 2, tc, 2, hg, S5_GROUP, S5_GROUP)
    bd = jnp.einsum('ldthgck,gj->ldthgkjc', kk, eye).reshape(nl, 2, tc, 2, LANES, LANES)
    bdf, bdb = bd[:, 0], bd[:, 1]
    strip = jnp.concatenate([bdb[:, :0:-1], (bdf[:, :1] + bdb[:, :1]), bdf[:, 1:]], axis=1)
    w_z = strip.transpose(0, 2, 3, 1, 4).reshape(nl, 2, LANES, S5_TOEPLITZ_BLOCKS * LANES)

    planes_b = jnp.stack([bbr[:, 0], bbi[:, 0], bbr[:, 1], bbi[:, 1]], axis=1)
    planes_b = planes_b.reshape(nl, 4, 2, hg, S5_STATE, S5_GROUP)
    w_b = jnp.einsum('lqhgpc,gj->lhgcqjp', planes_b, eye).reshape(nl, 2, LANES, 4 * S5_HALF_STATE)
    planes_c = jnp.stack([c_re[:, 0], -c_im[:, 0], c_re[:, 1], -c_im[:, 1]], axis=1)
    planes_c = planes_c.reshape(nl, 4, 2, hg, S5_GROUP, S5_STATE)
    w_c = jnp.einsum('lqhgcp,gj->lhqjpgc', planes_c, eye).reshape(nl, 2, 4 * S5_HALF_STATE, LANES)

    halves = lambda a: a.reshape(nl, 2, S5_HALF_STATE)
    lam_rows = jnp.stack([halves(lbr[:, 0]), halves(lbi[:, 0]), halves(lbr[:, 1]), halves(lbi[:, 1])], axis=2)
    full = lambda a: a.reshape(nl, S5_GROUPS * S5_STATE // LANES, LANES)
    a_pow = jnp.stack([full(pr[:, 0, tc]), full(pi[:, 0, tc]), full(pr[:, 1, tc]), full(pi[:, 1, tc])], axis=1)
    return w_b.astype(BF16), w_z.astype(BF16), w_c.astype(BF16), lam_rows, a_pow


def _s5_mix(u, w_b, w_z, w_c, lam_rows, a_pow, n_lat):
    t = u.shape[0]
    nch = t // S5_CHUNK
    sub = S5_GROUPS * S5_STATE // LANES
    u3 = u.reshape(nch, S5_CHUNK, S5_WIDTH)
    es = _s5_in_call(u3, w_b, lam_rows)
    es = [e.reshape(nch, sub, LANES) for e in es]
    hs = _s5_scan_call(a_pow, *es, n_lat // (S5_CHUNK * S5_SCAN_TILE))
    hs = [h.reshape(nch, sub * LANES) for h in hs]
    return _s5_out_call(u3, *hs, w_z, w_c, lam_rows).reshape(t, S5_WIDTH)


def _attn_body(flag_ref, lam_ref, q_ref, k_ref, v_ref, g_ref, o_ref, acc_ref, m_ref, oh_ref, *, n_kv, post_scale):
    bounded = flag_ref[0] == 1

    def head(hh, carry):
        q = q_ref[hh]
        acc_ref[...] = jnp.zeros_like(acc_ref)

        @pl.when(bounded)
        def _():
            def step(b, carry2):
                v = v_ref[hh, b]
                for c in range(2):
                    p = jnp.exp2(_dot(q, k_ref[hh, b, c])).astype(BF16)
                    acc_ref[c] += _dot(p, v)
                return carry2
            lax.fori_loop(0, n_kv, step, 0, unroll=True)

        @pl.when(jnp.logical_not(bounded))
        def _():
            m_ref[...] = jnp.full_like(m_ref, -jnp.inf)

            def step(b, carry2):
                v = v_ref[hh, b]
                for c in range(2):
                    s = _dot(q, k_ref[hh, b, c])
                    m_old = m_ref[c]
                    m_new = jnp.maximum(m_old, jnp.max(s, axis=1, keepdims=True))
                    p = jnp.exp2(s - m_new[:, 0:1]).astype(BF16)
                    acc_ref[c] = jnp.exp2(m_old - m_new) * acc_ref[c] + _dot(p, v)
                    m_ref[c] = m_new
                return carry2
            lax.fori_loop(0, n_kv, step, 0)

        a0 = acc_ref[0]
        a1 = acc_ref[1]
        o = a0[:, :DA_V_DIM] / a0[:, DA_V_DIM:DA_V_DIM + 1] \
            - lam_ref[0] * (a1[:, :DA_V_DIM] / a1[:, DA_V_DIM:DA_V_DIM + 1])
        ms = jnp.mean(o * o, axis=-1, keepdims=True)
        oh_ref[hh] = o * lax.rsqrt(ms + EPS) * g_ref[...] * post_scale
        return carry

    lax.fori_loop(0, ATTN_HEADS_PER_STEP, head, 0)
    o_ref[...] = jnp.concatenate([oh_ref[hh] for hh in range(ATTN_HEADS_PER_STEP)], axis=1)


def _attn_call(flag, lam, qh, kzb, vb, g_sub, post_scale, q_rows, q_row0, kv_block, kv_cols, kv_col0):
    nh = qh.shape[0]
    hp = ATTN_HEADS_PER_STEP
    tk = kzb.shape[4]
    tq = _pick_tile(q_rows, (ATTN_TQ, 256))
    assert q_row0 % tq == 0
    q0 = q_row0 // tq
    if kv_block is None:
        n_kv = kzb.shape[1]
        one = pl.Buffered(1)
        k_spec = pl.BlockSpec((hp, n_kv, 2, 2 * DA_HEAD_DIM, tk), lambda h, i: (h, 0, 0, 0, 0), pipeline_mode=one)
        v_spec = pl.BlockSpec((hp, n_kv, tk, LANES), lambda h, i: (h, 0, 0, 0), pipeline_mode=one)
    else:
        assert kv_col0 % kv_cols == 0 and tk % kv_cols == 0
        n_kv, cb = 1, kv_col0 // kv_cols
        k_spec = pl.BlockSpec((hp, 1, 2, 2 * DA_HEAD_DIM, kv_cols), lambda h, i: (h, kv_block, 0, 0, cb))
        v_spec = pl.BlockSpec((hp, 1, kv_cols, LANES), lambda h, i: (h, kv_block, cb, 0))
    smem = pl.BlockSpec(memory_space=pltpu.SMEM)
    return pl.pallas_call(
        functools.partial(_attn_body, n_kv=n_kv, post_scale=post_scale),
        grid=(nh // hp, q_rows // tq),
        in_specs=[smem, smem,
                  pl.BlockSpec((hp, tq, 2 * DA_HEAD_DIM), lambda h, i: (h, q0 + i, 0)),
                  k_spec, v_spec,
                  pl.BlockSpec((1, DA_V_DIM), lambda h, i: (0, 0))],
        out_specs=pl.BlockSpec((tq, hp * DA_V_DIM), lambda h, i: (i, h)),
        out_shape=jax.ShapeDtypeStruct((q_rows, nh * DA_V_DIM), F32),
        scratch_shapes=[pltpu.VMEM((2, tq, LANES), F32), pltpu.VMEM((2, tq, LANES), F32),
                        pltpu.VMEM((hp, tq, DA_V_DIM), F32)],
        compiler_params=_cparams("arbitrary", "arbitrary"),
        name="diff_attn",
    )(flag, lam, qh, kzb, vb, g_sub)


def _gelu_tanh(x):
    return 0.5 * x * (1.0 + jnp.tanh(math.sqrt(2.0 / math.pi) * (x + 0.044715 * (x * x * x))))


def _outffn_body(n_lat_tiles, x_ref, yf_ref, yb_ref, z_ref, gssd_ref, ys5_ref, u_ref, d5_ref, gw_ref, gb_ref,
                 ydl_ref, ydc_ref, mod_ref, g2_ref, wo_ref, w1_ref, w3_ref, w2_ref, o_ref):
    mod = mod_ref[0]
    y = (yf_ref[...] + yb_ref[...]) * _silu(z_ref[...])
    y_ssd = y * lax.rsqrt(jnp.mean(y * y, axis=-1, keepdims=True) + EPS) * gssd_ref[...]
    y5 = _gelu_tanh(ys5_ref[...] + d5_ref[...] * u_ref[...])
    y5 = y5 * jax.nn.sigmoid(_dot(y5, gw_ref[...], HI) + gb_ref[...])
    y_da = jnp.where(pl.program_id(0) >= n_lat_tiles, ydc_ref[...], ydl_ref[...])
    wo = wo_ref[...]
    o = _dot(y_ssd.astype(BF16), wo[0:256])
    o += _dot(y5.astype(BF16), wo[256:512])
    o += _dot(y_da.astype(BF16), wo[512:1024])
    x = x_ref[...] + mod[2:3] * o
    h = x * lax.rsqrt(jnp.mean(x * x, axis=-1, keepdims=True) + EPS) * g2_ref[...] * (1.0 + mod[4:5]) + mod[3:4]
    hb = h.astype(BF16)
    f = _silu(_dot(hb, w1_ref[...])) * _dot(hb, w3_ref[...])
    o_ref[...] = x + mod[5:6] * _dot(f.astype(BF16), w2_ref[...])


def _outffn_call(xs, y_f, y_b, z, g_ssd, y_s5, u, d5, glu_w, glu_b, y_da_l, y_da_c, modl, g2, w_o, w1, w3, w2,
                 n_lat_tiles):
    t, d = xs.shape
    dff = w1.shape[1]
    tm = ROW_TILE
    row = lambda i: (i, 0)
    const = lambda i: (0, 0)
    seg = lambda i: ((i >= n_lat_tiles).astype(jnp.int32), 0, 0)
    one = pl.Buffered(1)
    return pl.pallas_call(
        functools.partial(_outffn_body, n_lat_tiles),
        grid=(t // tm,),
        in_specs=[pl.BlockSpec((tm, d), row),
                  pl.BlockSpec((tm, 256), row), pl.BlockSpec((tm, 256), row), pl.BlockSpec((tm, 256), row),
                  pl.BlockSpec((1, 256), const),
                  pl.BlockSpec((tm, 256), row), pl.BlockSpec((tm, 256), row),
                  pl.BlockSpec((1, 256), const),
                  pl.BlockSpec((256, 256), const), pl.BlockSpec((1, 256), const),
                  pl.BlockSpec((tm, DA_WIDTH), lambda i: (jnp.minimum(i, n_lat_tiles - 1), 0)),
                  pl.BlockSpec((tm, DA_WIDTH), lambda i: (jnp.maximum(i - n_lat_tiles, 0), 0)),
                  pl.BlockSpec((1, 8, d), seg),
                  pl.BlockSpec((1, d), const),
                  pl.BlockSpec((d, d), const, pipeline_mode=one),
                  pl.BlockSpec((d, dff), const, pipeline_mode=one),
                  pl.BlockSpec((d, dff), const, pipeline_mode=one),
                  pl.BlockSpec((dff, d), const, pipeline_mode=one)],
        out_specs=pl.BlockSpec((tm, d), row),
        out_shape=jax.ShapeDtypeStruct((t, d), F32),
        compiler_params=_cparams("arbitrary"),
        name="out_proj_ffn",
    )(xs, y_f, y_b, z, g_ssd, y_s5, u, d5, glu_w, glu_b, y_da_l, y_da_c, modl, g2, w_o, w1, w3, w2)


def _rope_tables(n_lat, n_ctx):
    pos = jnp.arange(n_lat)
    lane = jnp.arange(LANES)
    d = lane % DA_HEAD_DIM
    axis = d // (2 * ROPE_FREQS)
    half = (d % (2 * ROPE_FREQS)) // ROPE_FREQS
    freqs = ROPE_THETA ** (-jnp.arange(ROPE_FREQS, dtype=F32) / ROPE_FREQS)
    coord = jnp.where(axis[None, :] == 0, (pos // GRID_W)[:, None], (pos % GRID_W)[:, None]).astype(F32)
    ang = coord * freqs[d % ROPE_FREQS][None, :]
    cos, sin = jnp.cos(ang), jnp.sin(ang)
    sa = jnp.where(half[None, :] == 0, -sin, 0.0)
    sb = jnp.where(half[None, :] == 1, sin, 0.0)
    pad = lambda a, v: jnp.concatenate([a, jnp.full((n_ctx, LANES), v, F32)], axis=0)
    return pad(cos, 1.0), pad(sa, 0.0), pad(sb, 0.0)


def _lane_row(vals, width=LANES):
    vals = vals.reshape(-1).astype(F32)
    return jnp.zeros((1, width), F32).at[0, :vals.shape[0]].set(vals)


def kernel(x, c, ctx, c_ctx, w_mod, b_mod, norm1, norm2, w_in, w_out, ssd_conv_w, ssd_conv_b, ssd_a_log, ssd_dt_bias, ssd_d, ssd_norm, s5_lam_re, s5_lam_im, s5_log_step, s5_b_re, s5_b_im, s5_c_re, s5_c_im, s5_d, s5_glu_w, s5_glu_b, da_q_norm, da_k_norm, da_lambda, da_sub_norm, ffn_w1, ffn_w3, ffn_w2):
    depth = w_mod.shape[0]
    bsz, n_lat, d = x.shape
    n_ctx = ctx.shape[1]
    t = n_ctx + n_lat
    tk = _pick_tile(t, (ATTN_TK, 1024, 640, 512, 256))
    assert bsz == 1 and n_ctx % ROW_TILE == 0 and n_lat % ROW_TILE == 0 and tk % ROW_TILE == 0
    assert n_ctx % (S5_CHUNK * S5_SCAN_TILE) == 0 and n_lat % (S5_CHUNK * S5_SCAN_TILE) == 0
    assert n_lat % n_ctx == 0 and tk % n_ctx == 0
    n_lat_tiles = n_lat // ROW_TILE

    xs = jnp.concatenate([x[0], ctx[0]], axis=0)
    cv = jnp.zeros((8, d), F32).at[0].set(c[0]).at[1].set(c_ctx)
    mod_all = _mod_call(cv, w_mod, b_mod)
    cos_t, sa_t, sb_t = _rope_tables(n_lat, n_ctx)
    gidx = jnp.arange(DA_WIDTH) // DA_HEAD_DIM
    gmat = (gidx[:, None] == gidx[None, :]).astype(BF16) * (1.0 / DA_HEAD_DIM)
    s5_w = _s5_weights(s5_lam_re, s5_lam_im, s5_log_step, s5_b_re, s5_b_im, s5_c_re, s5_c_im)

    for i in range(depth):
        lam_init = 0.8 - 0.6 * math.exp(-0.3 * i)
        modl = jnp.zeros((2, 8, d), F32).at[:, :6].set(mod_all[i, :2].reshape(2, 6, d))

        wi = w_in[i]
        w_r = jnp.concatenate([wi[:, 0:768], wi[:, 776:1032], wi[:, 1032:2568], wi[:, 768:776],
                               jnp.zeros((d, IN_PAD - 2568), F32)], axis=1).astype(BF16)
        gq = jnp.tile(da_q_norm[i], DA_WIDTH // DA_HEAD_DIM)[None, :]
        gk = jnp.tile(da_k_norm[i], DA_WIDTH // DA_HEAD_DIM)[None, :]
        z, xbc_raw, u, dt_raw, qh, kzb, vb = _inproj_call(
            xs, norm1[i][None, :], modl, w_r, gq, gk, gmat, cos_t, sa_t, sb_t, n_lat_tiles, tk)

        dt_r = dt_raw[:, :8].T
        cw = jnp.zeros((8, SSD_CONV_DIM), F32).at[:3].set(ssd_conv_w[i])
        a_neg = -jnp.exp(ssd_a_log[i])
        bias_c = _lane_row(ssd_dt_bias[i])
        a_c = _lane_row(a_neg)
        bias_r = jnp.broadcast_to(ssd_dt_bias[i].reshape(8, 1), (8, SSD_CHUNK))
        a_r = jnp.broadcast_to(a_neg.reshape(8, 1), (8, SSD_CHUNK))
        dsk = jnp.repeat(ssd_d[i], SSD_HEAD_DIM)[None, :]
        ssd_args = (xbc_raw, dt_raw, dt_r, cw, ssd_conv_b[i][None, :], bias_c, a_c, bias_r, a_r, dsk,
                    n_lat // SSD_CHUNK)
        y_f, y_b = _ssd_call(*ssd_args)

        y_s5 = _s5_mix(u, *(w[i] for w in s5_w), n_lat)

        lf = da_lambda[i]
        lam = (jnp.exp(jnp.sum(lf[0] * lf[1])) - jnp.exp(jnp.sum(lf[2] * lf[3])) + lam_init).reshape(1)
        g_sub = da_sub_norm[i][None, :]
        score_bound = LOG2E * math.sqrt(DA_HEAD_DIM) * jnp.max(jnp.abs(da_q_norm[i])) * jnp.max(jnp.abs(da_k_norm[i]))
        flag = (score_bound <= SCORE_LOG2_LIMIT).astype(jnp.int32).reshape(1)
        y_da_l = _attn_call(flag, lam, qh, kzb, vb, g_sub, 1.0 - lam_init, n_lat, 0, None, None, None)
        y_da_c = _attn_call(flag, lam, qh, kzb, vb, g_sub, 1.0 - lam_init, n_ctx, n_lat,
                            n_lat // tk, n_ctx, n_lat % tk)

        xs = _outffn_call(xs, y_f, y_b, z, ssd_norm[i][None, :], y_s5, u, s5_d[i][None, :], s5_glu_w[i],
                          s5_glu_b[i][None, :], y_da_l, y_da_c, modl, norm2[i][None, :], w_out[i].astype(BF16),
                          ffn_w1[i].astype(BF16), ffn_w3[i].astype(BF16), ffn_w2[i].astype(BF16),
                          n_lat_tiles)
    return xs[:n_lat][None]
```

```python
import functools
import math

import jax
import jax.numpy as jnp
from jax import lax
from jax.experimental import pallas as pl
from jax.experimental.pallas import tpu as pltpu

F32 = jnp.float32
BF16 = jnp.bfloat16
HI = lax.Precision.HIGHEST

EPS = 1e-6
GRID_W = 64
ROPE_THETA = 10000.0
ROPE_FREQS = 8

SSD_HEADS = 4
SSD_HEAD_DIM = 64
SSD_INNER = 256
SSD_GROUPS = 2
SSD_STATE = 64
SSD_CHUNK = 128
SSD_STEP_CHUNKS = 2
SSD_CONV_DIM = 512

S5_WIDTH = 256
S5_GROUP = 16
S5_GROUPS = 16
S5_STATE = 64
S5_CHUNK = 16
S5_HALF_GROUPS = S5_GROUPS // 2
S5_HALF_STATE = S5_HALF_GROUPS * S5_STATE
S5_SCAN_TILE = 16
S5_TOEPLITZ_BLOCKS = 2 * S5_CHUNK - 1

DA_HEADS = 8
DA_HEAD_DIM = 32
DA_V_DIM = 64
DA_WIDTH = 512
ATTN_HEADS_PER_STEP = 2
ATTN_TQ = 2048
ATTN_TK = 1280
LOG2E = 1.4426950408889634
SCORE_LOG2_LIMIT = 100.0

LANES = 128
ROW_TILE = 256
FFN_ROW_TILE = 256
IN_PAD = 2688
VMEM_LIMIT = 56 * 1024 * 1024


def _cparams(*sem):
    return pltpu.CompilerParams(dimension_semantics=sem, vmem_limit_bytes=VMEM_LIMIT)


def _dot(a, b, precision=None):
    return jnp.dot(a, b, preferred_element_type=F32, precision=precision)


def _dot_nt(a, b, precision=None):
    return lax.dot_general(a, b, (((1,), (1,)), ((), ())), preferred_element_type=F32,
                           precision=precision)


def _dot_tn(a, b, precision=None):
    return lax.dot_general(a, b, (((0,), (0,)), ((), ())), preferred_element_type=F32,
                           precision=precision)


def _silu(x):
    return x * jax.nn.sigmoid(x)


def _pick_tile(n, candidates):
    for c in candidates:
        if n % c == 0:
            return c
    return n


def _segment_order(i, n_lat, n_all, reverse):
    n_ctx = n_all - n_lat
    if not reverse:
        return jnp.where(i < n_ctx, n_lat + i, i - n_ctx)
    return jnp.where(i < n_ctx, n_all - 1 - i, n_lat - 1 - (i - n_ctx))


def _mod_body(cv_ref, w_ref, b_ref, o_ref):
    o_ref[0] = _dot(_silu(cv_ref[...]), w_ref[0], HI) + b_ref[0]


def _mod_call(cv, w_mod, b_mod):
    depth, d, n = w_mod.shape
    tn = 1024
    return pl.pallas_call(
        _mod_body,
        grid=(depth, n // tn),
        in_specs=[pl.BlockSpec((8, d), lambda l, j: (0, 0)),
                  pl.BlockSpec((1, d, tn), lambda l, j: (l, 0, j)),
                  pl.BlockSpec((1, 1, tn), lambda l, j: (l, 0, j))],
        out_specs=pl.BlockSpec((1, 8, tn), lambda l, j: (l, 0, j)),
        out_shape=jax.ShapeDtypeStruct((depth, 8, n), F32),
        compiler_params=_cparams("arbitrary", "arbitrary"),
        name="adaln_mod",
    )(cv, w_mod, b_mod.reshape(depth, 1, n))


def _inproj_body(x_ref, g_ref, mod_ref, w_ref, gq_ref, gk_ref, gm_ref, cos_ref, sa_ref, sb_ref,
                 z_ref, xbc_ref, u_ref, dt_ref, q_ref, k_ref, v_ref):
    x = x_ref[...]
    rows = x.shape[0]
    mod = mod_ref[0]
    ms = jnp.mean(x * x, axis=-1, keepdims=True)
    h = x * lax.rsqrt(ms + EPS) * g_ref[...] * (1.0 + mod[1:2]) + mod[0:1]
    p = _dot(h.astype(BF16), w_ref[...])
    z_ref[...] = p[:, 0:256]
    xbc_ref[...] = p[:, 256:768]
    u_ref[...] = p[:, 768:1024]
    dt_ref[...] = p[:, 2560:2688]

    gm = gm_ref[...]
    cos = cos_ref[...]
    sa = sa_ref[...]
    sb = sb_ref[...]

    def norm_rope(t, gain, scale):
        t2 = t * t
        hi = t2.astype(BF16)
        lo = (t2 - hi.astype(F32)).astype(BF16)
        ms32 = _dot(hi, gm) + _dot(lo, gm)
        tn = t * lax.rsqrt(ms32 + EPS) * gain
        outs = []
        for j in range(DA_WIDTH // LANES):
            tb = tn[:, j * LANES:(j + 1) * LANES]
            ob = tb * cos + pltpu.roll(tb, LANES - ROPE_FREQS, 1) * sa + pltpu.roll(tb, ROPE_FREQS, 1) * sb
            outs.append(ob * scale)
        return jnp.concatenate(outs, axis=1)

    qn = norm_rope(p[:, 1024:1536], gq_ref[...], LOG2E * DA_HEAD_DIM ** -0.5).astype(BF16)
    kt = norm_rope(p[:, 1536:2048], gk_ref[...], 1.0).T.astype(BF16)
    vv = p[:, 2048:2560].astype(BF16)
    lane = lax.broadcasted_iota(jnp.int32, (rows, LANES - DA_V_DIM), 1)
    ones_col = jnp.where(lane == 0, 1.0, 0.0).astype(BF16)
    zeros_k = jnp.zeros((DA_HEAD_DIM, rows), BF16)
    for hd in range(DA_HEADS):
        lo_, mid, hi_ = hd * DA_V_DIM, hd * DA_V_DIM + DA_HEAD_DIM, (hd + 1) * DA_V_DIM
        q_ref[hd] = qn[:, lo_:hi_]
        k_ref[hd, 0, 0, 0:DA_HEAD_DIM, :] = kt[lo_:mid]
        k_ref[hd, 0, 0, DA_HEAD_DIM:, :] = zeros_k
        k_ref[hd, 0, 1, 0:DA_HEAD_DIM, :] = zeros_k
        k_ref[hd, 0, 1, DA_HEAD_DIM:, :] = kt[mid:hi_]
        v_ref[hd, 0] = jnp.concatenate([vv[:, lo_:hi_], ones_col], axis=1)


def _inproj_call(xs, g1, modl, w_r, gq, gk, gmat, cos_t, sa_t, sb_t, n_lat_tiles, tk):
    t, d = xs.shape
    tm = ROW_TILE
    per = tk // tm
    row = lambda i: (i, 0)
    const = lambda i: (0, 0)
    seg = lambda i: ((i >= n_lat_tiles).astype(jnp.int32), 0, 0)
    flat = [(256, F32), (512, F32), (256, F32), (LANES, F32)]
    return pl.pallas_call(
        _inproj_body,
        grid=(t // tm,),
        in_specs=[pl.BlockSpec((tm, d), row),
                  pl.BlockSpec((1, d), const),
                  pl.BlockSpec((1, 8, d), seg),
                  pl.BlockSpec((d, IN_PAD), const),
                  pl.BlockSpec((1, DA_WIDTH), const),
                  pl.BlockSpec((1, DA_WIDTH), const),
                  pl.BlockSpec((DA_WIDTH, DA_WIDTH), const),
                  pl.BlockSpec((tm, LANES), row),
                  pl.BlockSpec((tm, LANES), row),
                  pl.BlockSpec((tm, LANES), row)],
        out_specs=[pl.BlockSpec((tm, w), row) for w, _ in flat]
                  + [pl.BlockSpec((DA_HEADS, tm, 2 * DA_HEAD_DIM), lambda i: (0, i, 0)),
                     pl.BlockSpec((DA_HEADS, 1, 2, 2 * DA_HEAD_DIM, tm), lambda i: (0, i // per, 0, 0, i % per)),
                     pl.BlockSpec((DA_HEADS, 1, tm, LANES), lambda i: (0, i // per, i % per, 0))],
        out_shape=[jax.ShapeDtypeStruct((t, w), dt) for w, dt in flat]
                  + [jax.ShapeDtypeStruct((DA_HEADS, t, 2 * DA_HEAD_DIM), BF16),
                     jax.ShapeDtypeStruct((DA_HEADS, t // tk, 2, 2 * DA_HEAD_DIM, tk), BF16),
                     jax.ShapeDtypeStruct((DA_HEADS, t // tk, tk, LANES), BF16)],
        compiler_params=_cparams("arbitrary"),
        name="in_proj",
    )(xs, g1, modl, w_r, gq, gk, gmat, cos_t, sa_t, sb_t)


def _softplus(x):
    return jnp.maximum(x, 0.0) + jnp.log1p(jnp.exp(-jnp.abs(x)))


def _ssd_block(c, n_lat, n_blocks, reverse, xc_ref, xp_ref, xn_ref, dtc_ref, dtr_ref, cw_ref, cb_ref,
               bias_c_ref, a_c_ref, bias_r_ref, a_r_ref, st_ref):
    L = SSD_CHUNK
    rows = xc_ref.shape[0]
    x = xc_ref[...]
    seg_first = jnp.logical_or(c == 0, c == n_lat)
    seg_last = jnp.logical_or(c == n_lat - 1, c == n_blocks - 1)
    prow = jnp.where(seg_first, 0.0, xp_ref[7:8, :])
    nrow = jnp.where(seg_last, 0.0, xn_ref[0:1, :])
    ridx = lax.broadcasted_iota(jnp.int32, (rows, 1), 0)
    xprev = jnp.where(ridx == 0, prow, pltpu.roll(x, 1, 0))
    xnext = jnp.where(ridx == rows - 1, nrow, pltpu.roll(x, rows - 1, 0))
    cw = cw_ref[...]
    conv = xprev * cw[0:1] + x * cw[1:2] + xnext * cw[2:3] + cb_ref[...]
    xbc = _silu(conv)

    dt_c = _softplus(dtc_ref[...] + bias_c_ref[...])
    dt_r = _softplus(dtr_ref[...] + bias_r_ref[...])
    adt_c = dt_c * a_c_ref[...]
    adt_r = dt_r * a_r_ref[...]

    li = lax.broadcasted_iota(jnp.int32, (L, L), 0)
    si = lax.broadcasted_iota(jnp.int32, (L, L), 1)
    mask = (li <= si) if reverse else (li >= si)
    mask_t = (li >= si) if reverse else (li <= si)
    n_sub = rows // L
    ys = [None] * n_sub
    for s in (reversed(range(n_sub)) if reverse else range(n_sub)):
        r = slice(s * L, (s + 1) * L)
        ys[s] = _ssd_chunk(xbc[r], dt_c[r], adt_c[r], adt_r[:, r], mask, mask_t, st_ref, reverse)
    return jnp.concatenate(ys, axis=0), xbc[:, :SSD_INNER]


def _ssd_chunk(xbc, dt_c, adt_c, adt_r, mask, mask_t, st_ref, reverse):
    L = SSD_CHUNK
    off = SSD_HEADS if reverse else 0
    cs_c = _dot(mask.astype(F32), adt_c, HI)
    cs_r = _dot(adt_r, mask_t.astype(F32), HI)
    edge = 0 if reverse else L - 1

    ys = []
    for g in range(SSD_GROUPS):
        bg = xbc[:, SSD_INNER + g * SSD_STATE:SSD_INNER + (g + 1) * SSD_STATE].astype(BF16)
        cg = xbc[:, SSD_INNER + (SSD_GROUPS + g) * SSD_STATE:SSD_INNER + (SSD_GROUPS + g + 1) * SSD_STATE]
        cg = cg.astype(BF16)
        gmat = _dot_nt(cg, bg)
        for hh in range(SSD_HEADS // SSD_GROUPS):
            h = g * (SSD_HEADS // SSD_GROUPS) + hh
            col = off + h
            csc = cs_c[:, col:col + 1]
            csr = cs_r[col:col + 1, :]
            dec = jnp.exp(jnp.where(mask, csc - csr, -jnp.inf))
            xh = xbc[:, h * SSD_HEAD_DIM:(h + 1) * SSD_HEAD_DIM]
            xd = xh * dt_c[:, col:col + 1]
            st = st_ref[h]
            y = _dot((gmat * dec).astype(BF16), xd.astype(BF16))
            y += jnp.exp(csc) * _dot_nt(cg, st.astype(BF16))
            tot = cs_c[edge:edge + 1, col:col + 1]
            xdt = (xd * jnp.exp(tot - csc)).T.astype(BF16)
            st_ref[h] = st * jnp.exp(tot) + _dot(xdt, bg)
            ys.append(y)
    return jnp.concatenate(ys, axis=1)


def _ssd_body(n_lat, n_chunks, *refs):
    fwd_in, bwd_in = refs[0:5], refs[5:10]
    cw_ref, cb_ref, bias_c_ref, a_c_ref, bias_r_ref, a_r_ref, dsk_ref, yf_ref, yb_ref, st_ref = refs[10:]
    i = pl.program_id(0)

    @pl.when(i == 0)
    def _():
        st_ref[...] = jnp.zeros_like(st_ref)

    shared = (cw_ref, cb_ref, bias_c_ref, a_c_ref, bias_r_ref, a_r_ref)
    y, xs = _ssd_block(_segment_order(i, n_lat, n_chunks, False), n_lat, n_chunks, False, *fwd_in, *shared,
                       st_ref.at[0])
    yf_ref[...] = y + dsk_ref[...] * xs
    y, _ = _ssd_block(_segment_order(i, n_lat, n_chunks, True), n_lat, n_chunks, True, *bwd_in, *shared,
                      st_ref.at[1])
    yb_ref[...] = y


def _ssd_call(xbc_raw, dt_c, dt_r, cw, cb, bias_c, a_c, bias_r, a_r, dsk, n_lat_rows):
    t = xbc_raw.shape[0]
    L = SSD_STEP_CHUNKS * SSD_CHUNK
    assert t % L == 0 and n_lat_rows % L == 0
    n_chunks = t // L
    n_lat = n_lat_rows // L
    sub = L // 8
    n_sub = t // 8
    const = lambda i: (0, 0)

    def chunk_specs(reverse):
        cidx = functools.partial(_segment_order, n_lat=n_lat, n_all=n_chunks, reverse=reverse)
        return [pl.BlockSpec((L, SSD_CONV_DIM), lambda i: (cidx(i), 0)),
                pl.BlockSpec((8, SSD_CONV_DIM), lambda i: (jnp.maximum(cidx(i) * sub - 1, 0), 0)),
                pl.BlockSpec((8, SSD_CONV_DIM), lambda i: (jnp.minimum((cidx(i) + 1) * sub, n_sub - 1), 0)),
                pl.BlockSpec((L, LANES), lambda i: (cidx(i), 0)),
                pl.BlockSpec((8, L), lambda i: (0, cidx(i)))]

    def out_spec(reverse):
        cidx = functools.partial(_segment_order, n_lat=n_lat, n_all=n_chunks, reverse=reverse)
        return pl.BlockSpec((L, SSD_INNER), lambda i: (cidx(i), 0))

    chunk_args = (xbc_raw, xbc_raw, xbc_raw, dt_c, dt_r)
    return pl.pallas_call(
        functools.partial(_ssd_body, n_lat, n_chunks),
        grid=(n_chunks,),
        in_specs=chunk_specs(False) + chunk_specs(True)
                 + [pl.BlockSpec((8, SSD_CONV_DIM), const),
                    pl.BlockSpec((1, SSD_CONV_DIM), const),
                    pl.BlockSpec((1, LANES), const),
                    pl.BlockSpec((1, LANES), const),
                    pl.BlockSpec((8, L), const),
                    pl.BlockSpec((8, L), const),
                    pl.BlockSpec((1, SSD_INNER), const)],
        out_specs=[out_spec(False), out_spec(True)],
        out_shape=[jax.ShapeDtypeStruct((t, SSD_INNER), F32)] * 2,
        scratch_shapes=[pltpu.VMEM((2, SSD_HEADS, SSD_HEAD_DIM, SSD_STATE), F32)],
        compiler_params=_cparams("arbitrary"),
        name="ssd_scan",
    )(*chunk_args, *chunk_args, cw, cb, bias_c, a_c, bias_r, a_r, dsk)


def _cmul(ar, ai, br, bi):
    return ar * br - ai * bi, ar * bi + ai * br


def _s5_in_body(u_ref, w_ref, lam_ref, efr_ref, efi_ref, ebr_ref, ebi_ref, xs_ref, bu_ref):
    tn = u_ref.shape[0]
    hs = S5_HALF_STATE
    for i in range(S5_CHUNK):
        xs_ref[i * tn:(i + 1) * tn, :] = u_ref[:, i, :]
    bu_ref[...] = _dot(xs_ref[...].astype(BF16), w_ref[0])
    lam = lam_ref[0]
    fr, fi, br, bi = lam[0:1], lam[1:2], lam[2:3], lam[3:4]
    ar = jnp.zeros((tn, hs), F32)
    ai = jnp.zeros((tn, hs), F32)
    for i in range(S5_CHUNK):
        ar, ai = _cmul(ar, ai, fr, fi)
        ar = ar + bu_ref[i * tn:(i + 1) * tn, 0:hs]
        ai = ai + bu_ref[i * tn:(i + 1) * tn, hs:2 * hs]
    efr_ref[...] = ar
    efi_ref[...] = ai
    ar = jnp.zeros((tn, hs), F32)
    ai = jnp.zeros((tn, hs), F32)
    for i in reversed(range(S5_CHUNK)):
        ar, ai = _cmul(ar, ai, br, bi)
        ar = ar + bu_ref[i * tn:(i + 1) * tn, 2 * hs:3 * hs]
        ai = ai + bu_ref[i * tn:(i + 1) * tn, 3 * hs:4 * hs]
    ebr_ref[...] = ar
    ebi_ref[...] = ai


def _s5_in_call(u3, w_b, lam_rows):
    nch = u3.shape[0]
    tn = _pick_tile(nch, (104, 80, 40))
    hs = S5_HALF_STATE
    col = lambda h, n: (n, h)
    return pl.pallas_call(
        _s5_in_body,
        grid=(2, nch // tn),
        in_specs=[pl.BlockSpec((tn, S5_CHUNK, LANES), lambda h, n: (n, 0, h)),
                  pl.BlockSpec((1, LANES, 4 * hs), lambda h, n: (h, 0, 0)),
                  pl.BlockSpec((1, 4, hs), lambda h, n: (h, 0, 0))],
        out_specs=[pl.BlockSpec((tn, hs), col)] * 4,
        out_shape=[jax.ShapeDtypeStruct((nch, 2 * hs), F32)] * 4,
        scratch_shapes=[pltpu.VMEM((tn * S5_CHUNK, LANES), F32), pltpu.VMEM((tn * S5_CHUNK, 4 * hs), F32)],
        compiler_params=_cparams("arbitrary", "arbitrary"),
        name="s5_chunk_in",
    )(u3, w_b, lam_rows)


def _s5_scan_body(a_ref, efr_ref, efi_ref, ebr_ref, ebi_ref, hfr_ref, hfi_ref, hbr_ref, hbi_ref, st_ref):
    @pl.when(pl.program_id(0) == 0)
    def _():
        st_ref[...] = jnp.zeros_like(st_ref)

    afr, afi, abr, abi = a_ref[0], a_ref[1], a_ref[2], a_ref[3]
    fr, fi, br, bi = st_ref[0], st_ref[1], st_ref[2], st_ref[3]
    for j in range(S5_SCAN_TILE):
        hfr_ref[j] = fr
        hfi_ref[j] = fi
        fr, fi = afr * fr - afi * fi + efr_ref[j], afr * fi + afi * fr + efi_ref[j]
        jb = S5_SCAN_TILE - 1 - j
        hbr_ref[jb] = br
        hbi_ref[jb] = bi
        br, bi = abr * br - abi * bi + ebr_ref[jb], abr * bi + abi * br + ebi_ref[jb]
    st_ref[0] = fr
    st_ref[1] = fi
    st_ref[2] = br
    st_ref[3] = bi


def _s5_scan_call(a_pow, efr, efi, ebr, ebi, n_lat_tiles):
    nch, sub, _ = efr.shape
    n_tiles = nch // S5_SCAN_TILE
    blk = (S5_SCAN_TILE, sub, LANES)
    fwd = lambda i: (_segment_order(i, n_lat_tiles, n_tiles, False), 0, 0)
    bwd = lambda i: (_segment_order(i, n_lat_tiles, n_tiles, True), 0, 0)
    shp = jax.ShapeDtypeStruct((nch, sub, LANES), F32)
    return pl.pallas_call(
        _s5_scan_body,
        grid=(n_tiles,),
        in_specs=[pl.BlockSpec((4, sub, LANES), lambda i: (0, 0, 0)),
                  pl.BlockSpec(blk, fwd), pl.BlockSpec(blk, fwd),
                  pl.BlockSpec(blk, bwd), pl.BlockSpec(blk, bwd)],
        out_specs=[pl.BlockSpec(blk, fwd), pl.BlockSpec(blk, fwd),
                   pl.BlockSpec(blk, bwd), pl.BlockSpec(blk, bwd)],
        out_shape=[shp] * 4,
        scratch_shapes=[pltpu.VMEM((4, sub, LANES), F32)],
        compiler_params=_cparams("arbitrary"),
        name="s5_state_scan",
    )(a_pow, efr, efi, ebr, ebi)


def _s5_out_body(u_ref, hfr_ref, hfi_ref, hbr_ref, hbi_ref, z_ref, c_ref, lam_ref, y_ref, gb_ref):
    hs = S5_HALF_STATE
    width = S5_CHUNK * LANES
    lam = lam_ref[0]
    fr, fi, br, bi = lam[0:1], lam[1:2], lam[2:3], lam[3:4]
    ytz = None
    for i in range(S5_CHUNK):
        xi = u_ref[:, i, :].astype(BF16)
        lo = (S5_CHUNK - 1 - i) * LANES
        part = _dot(xi, z_ref[0, :, lo:lo + width])
        ytz = part if ytz is None else ytz + part
    gr, gi = hbr_ref[...], hbi_ref[...]
    for j in reversed(range(S5_CHUNK)):
        gr, gi = _cmul(gr, gi, br, bi)
        gb_ref[j, :, 0:hs] = gr
        gb_ref[j, :, hs:] = gi
    cw = c_ref[0]
    gr, gi = hfr_ref[...], hfi_ref[...]
    for j in range(S5_CHUNK):
        gr, gi = _cmul(gr, gi, fr, fi)
        g_all = jnp.concatenate([gr, gi, gb_ref[j]], axis=1).astype(BF16)
        y_ref[:, j, :] = ytz[:, j * LANES:(j + 1) * LANES] + _dot(g_all, cw)


def _s5_out_call(u3, hfr, hfi, hbr, hbi, w_z, w_c, lam_rows):
    nch = u3.shape[0]
    tn = _pick_tile(nch, (208, 80, 40))
    hs = S5_HALF_STATE
    col = lambda h, n: (n, h)
    blk3 = pl.BlockSpec((tn, S5_CHUNK, LANES), lambda h, n: (n, 0, h))
    return pl.pallas_call(
        _s5_out_body,
        grid=(2, nch // tn),
        in_specs=[blk3] + [pl.BlockSpec((tn, hs), col)] * 4
                 + [pl.BlockSpec((1, LANES, S5_TOEPLITZ_BLOCKS * LANES), lambda h, n: (h, 0, 0)),
                    pl.BlockSpec((1, 4 * hs, LANES), lambda h, n: (h, 0, 0)),
                    pl.BlockSpec((1, 4, hs), lambda h, n: (h, 0, 0))],
        out_specs=blk3,
        out_shape=jax.ShapeDtypeStruct(u3.shape, F32),
        scratch_shapes=[pltpu.VMEM((S5_CHUNK, tn, 2 * hs), F32)],
        compiler_params=_cparams("arbitrary", "arbitrary"),
        name="s5_chunk_out",
    )(u3, hfr, hfi, hbr, hbi, w_z, w_c, lam_rows)


def _s5_weights(lam_re, lam_im, log_step, b_re, b_im, c_re, c_im):
    tc = S5_CHUNK
    nl = lam_re.shape[0]
    hg = S5_HALF_GROUPS
    ein = functools.partial(jnp.einsum, precision=HI)
    step = jnp.exp(log_step)[..., None]
    er, ei = lam_re * step, lam_im * step
    mag = jnp.exp(er)
    lbr, lbi = mag * jnp.cos(ei), mag * jnp.sin(ei)
    den = lam_re * lam_re + lam_im * lam_im
    nr, ni = lbr - 1.0, lbi
    cr, ci = (nr * lam_re + ni * lam_im) / den, (ni * lam_re - nr * lam_im) / den
    bbr = cr[..., None] * b_re - ci[..., None] * b_im
    bbi = cr[..., None] * b_im + ci[..., None] * b_re
    taus = jnp.arange(tc + 1, dtype=F32)[:, None, None]
    pmag = jnp.exp(er[:, :, None] * taus)
    pr, pi = pmag * jnp.cos(ei[:, :, None] * taus), pmag * jnp.sin(ei[:, :, None] * taus)

    wr = pr[:, :, :tc, :, :, None] * bbr[:, :, None] - pi[:, :, :tc, :, :, None] * bbi[:, :, None]
    wi = pr[:, :, :tc, :, :, None] * bbi[:, :, None] + pi[:, :, :tc, :, :, None] * bbr[:, :, None]
    kk = ein('ldgcp,ldtgpk->ldtgck', c_re, wr) - ein('ldgcp,ldtgpk->ldtgck', c_im, wi)
    eye = jnp.eye(hg, dtype=F32)
    kk = kk.reshape(nl, 2, tc, 2, hg, S5_GROUP, S5_GROUP)
    bd = jnp.einsum('ldthgck,gj->ldthgkjc', kk, eye).reshape(nl, 2, tc, 2, LANES, LANES)
    bdf, bdb = bd[:, 0], bd[:, 1]
    strip = jnp.concatenate([bdb[:, :0:-1], (bdf[:, :1] + bdb[:, :1]), bdf[:, 1:]], axis=1)
    w_z = strip.transpose(0, 2, 3, 1, 4).reshape(nl, 2, LANES, S5_TOEPLITZ_BLOCKS * LANES)

    planes_b = jnp.stack([bbr[:, 0], bbi[:, 0], bbr[:, 1], bbi[:, 1]], axis=1)
    planes_b = planes_b.reshape(nl, 4, 2, hg, S5_STATE, S5_GROUP)
    w_b = jnp.einsum('lqhgpc,gj->lhgcqjp', planes_b, eye).reshape(nl, 2, LANES, 4 * S5_HALF_STATE)
    planes_c = jnp.stack([c_re[:, 0], -c_im[:, 0], c_re[:, 1], -c_im[:, 1]], axis=1)
    planes_c = planes_c.reshape(nl, 4, 2, hg, S5_GROUP, S5_STATE)
    w_c = jnp.einsum('lqhgcp,gj->lhqjpgc', planes_c, eye).reshape(nl, 2, 4 * S5_HALF_STATE, LANES)

    halves = lambda a: a.reshape(nl, 2, S5_HALF_STATE)
    lam_rows = jnp.stack([halves(lbr[:, 0]), halves(lbi[:, 0]), halves(lbr[:, 1]), halves(lbi[:, 1])], axis=2)
    full = lambda a: a.reshape(nl, S5_GROUPS * S5_STATE // LANES, LANES)
    a_pow = jnp.stack([full(pr[:, 0, tc]), full(pi[:, 0, tc]), full(pr[:, 1, tc]), full(pi[:, 1, tc])], axis=1)
    return w_b.astype(BF16), w_z.astype(BF16), w_c.astype(BF16), lam_rows, a_pow


def _s5_mix(u, w_b, w_z, w_c, lam_rows, a_pow, n_lat):
    t = u.shape[0]
    nch = t // S5_CHUNK
    sub = S5_GROUPS * S5_STATE // LANES
    u3 = u.reshape(nch, S5_CHUNK, S5_WIDTH)
    es = _s5_in_call(u3, w_b, lam_rows)
    es = [e.reshape(nch, sub, LANES) for e in es]
    hs = _s5_scan_call(a_pow, *es, n_lat // (S5_CHUNK * S5_SCAN_TILE))
    hs = [h.reshape(nch, sub * LANES) for h in hs]
    return _s5_out_call(u3, *hs, w_z, w_c, lam_rows).reshape(t, S5_WIDTH)


def _attn_body(flag_ref, lam_ref, q_ref, k_ref, v_ref, g_ref, o_ref, acc_ref, m_ref, oh_ref, *, n_kv, post_scale):
    bounded = flag_ref[0] == 1

    def head(hh, carry):
        q = q_ref[hh]
        acc_ref[...] = jnp.zeros_like(acc_ref)

        @pl.when(bounded)
        def _():
            def step(b, carry2):
                v = v_ref[hh, b]
                for c in range(2):
                    p = jnp.exp2(_dot(q, k_ref[hh, b, c])).astype(BF16)
                    acc_ref[c] += _dot(p, v)
                return carry2
            lax.fori_loop(0, n_kv, step, 0)

        @pl.when(jnp.logical_not(bounded))
        def _():
            m_ref[...] = jnp.full_like(m_ref, -jnp.inf)

            def step(b, carry2):
                v = v_ref[hh, b]
                for c in range(2):
                    s = _dot(q, k_ref[hh, b, c])
                    m_old = m_ref[c]
                    m_new = jnp.maximum(m_old, jnp.max(s, axis=1, keepdims=True))
                    p = jnp.exp2(s - m_new[:, 0:1]).astype(BF16)
                    acc_ref[c] = jnp.exp2(m_old - m_new) * acc_ref[c] + _dot(p, v)
                    m_ref[c] = m_new
                return carry2
            lax.fori_loop(0, n_kv, step, 0)

        a0 = acc_ref[0]
        a1 = acc_ref[1]
        o = a0[:, :DA_V_DIM] / a0[:, DA_V_DIM:DA_V_DIM + 1] \
            - lam_ref[0] * (a1[:, :DA_V_DIM] / a1[:, DA_V_DIM:DA_V_DIM + 1])
        ms = jnp.mean(o * o, axis=-1, keepdims=True)
        oh_ref[hh] = o * lax.rsqrt(ms + EPS) * g_ref[...] * post_scale
        return carry

    lax.fori_loop(0, ATTN_HEADS_PER_STEP, head, 0)
    o_ref[...] = jnp.concatenate([oh_ref[hh] for hh in range(ATTN_HEADS_PER_STEP)], axis=1)


def _attn_body_keep_rest(flag_ref, lam_ref, q_ref, k_ref, v_ref, g_ref, rest_ref, *refs, **kw):
    del rest_ref
    _attn_body(flag_ref, lam_ref, q_ref, k_ref, v_ref, g_ref, *refs, **kw)


def _attn_call(flag, lam, qh, kzb, vb, g_sub, post_scale, q_rows, q_row0, kv_block, kv_cols, kv_col0, y_rest=None):
    nh = qh.shape[0]
    hp = ATTN_HEADS_PER_STEP
    tk = kzb.shape[4]
    tq = _pick_tile(q_rows, (ATTN_TQ, 256))
    assert q_row0 % tq == 0
    q0 = q_row0 // tq
    if kv_block is None:
        n_kv = kzb.shape[1]
        one = pl.Buffered(1)
        k_spec = pl.BlockSpec((hp, n_kv, 2, 2 * DA_HEAD_DIM, tk), lambda h, i: (h, 0, 0, 0, 0), pipeline_mode=one)
        v_spec = pl.BlockSpec((hp, n_kv, tk, LANES), lambda h, i: (h, 0, 0, 0), pipeline_mode=one)
    else:
        assert kv_col0 % kv_cols == 0 and tk % kv_cols == 0
        n_kv, cb = 1, kv_col0 // kv_cols
        k_spec = pl.BlockSpec((hp, 1, 2, 2 * DA_HEAD_DIM, kv_cols), lambda h, i: (h, kv_block, 0, 0, cb))
        v_spec = pl.BlockSpec((hp, 1, kv_cols, LANES), lambda h, i: (h, kv_block, cb, 0))
    smem = pl.BlockSpec(memory_space=pltpu.SMEM)
    in_specs = [smem, smem,
                pl.BlockSpec((hp, tq, 2 * DA_HEAD_DIM), lambda h, i: (h, q0 + i, 0)),
                k_spec, v_spec,
                pl.BlockSpec((1, DA_V_DIM), lambda h, i: (0, 0))]
    args = (flag, lam, qh, kzb, vb, g_sub)
    body = functools.partial(_attn_body, n_kv=n_kv, post_scale=post_scale)
    aliases = {}
    if y_rest is not None:
        in_specs.append(pl.BlockSpec(memory_space=pl.ANY))
        args += (y_rest,)
        body = functools.partial(_attn_body_keep_rest, n_kv=n_kv, post_scale=post_scale)
        aliases = {len(args) - 1: 0}
    return pl.pallas_call(
        body,
        grid=(nh // hp, q_rows // tq),
        in_specs=in_specs,
        out_specs=pl.BlockSpec((tq, hp * DA_V_DIM), lambda h, i: (q0 + i, h)),
        out_shape=jax.ShapeDtypeStruct((qh.shape[1], nh * DA_V_DIM), F32),
        scratch_shapes=[pltpu.VMEM((2, tq, LANES), F32), pltpu.VMEM((2, tq, LANES), F32),
                        pltpu.VMEM((hp, tq, DA_V_DIM), F32)],
        input_output_aliases=aliases,
        compiler_params=_cparams("arbitrary", "arbitrary"),
        name="diff_attn",
    )(*args)


def _gelu_tanh(x):
    return 0.5 * x * (1.0 + jnp.tanh(math.sqrt(2.0 / math.pi) * (x + 0.044715 * (x * x * x))))


def _outffn_body(x_ref, yf_ref, yb_ref, z_ref, gssd_ref, ys5_ref, u_ref, d5_ref, gw_ref, gb_ref,
                 yda_ref, mod_ref, g2_ref, wo_ref, w1_ref, w3_ref, w2_ref, o_ref):
    mod = mod_ref[0]
    y = (yf_ref[...] + yb_ref[...]) * _silu(z_ref[...])
    y_ssd = y * lax.rsqrt(jnp.mean(y * y, axis=-1, keepdims=True) + EPS) * gssd_ref[...]
    y5 = _gelu_tanh(ys5_ref[...] + d5_ref[...] * u_ref[...])
    y5 = y5 * jax.nn.sigmoid(_dot(y5, gw_ref[...], HI) + gb_ref[...])
    wo = wo_ref[...]
    o = _dot(y_ssd.astype(BF16), wo[0:256])
    o += _dot(y5.astype(BF16), wo[256:512])
    o += _dot(yda_ref[...].astype(BF16), wo[512:1024])
    x = x_ref[...] + mod[2:3] * o
    h = x * lax.rsqrt(jnp.mean(x * x, axis=-1, keepdims=True) + EPS) * g2_ref[...] * (1.0 + mod[4:5]) + mod[3:4]
    hb = h.astype(BF16)
    f = _silu(_dot(hb, w1_ref[...])) * _dot(hb, w3_ref[...])
    o_ref[...] = x + mod[5:6] * _dot(f.astype(BF16), w2_ref[...])


def _outffn_call(xs, y_f, y_b, z, g_ssd, y_s5, u, d5, glu_w, glu_b, y_da, modl, g2, w_o, w1, w3, w2, n_lat):
    t, d = xs.shape
    dff = w1.shape[1]
    tm = FFN_ROW_TILE
    assert n_lat % tm == 0
    n_lat_tiles = n_lat // tm
    row = lambda i: (i, 0)
    const = lambda i: (0, 0)
    seg = lambda i: ((i >= n_lat_tiles).astype(jnp.int32), 0, 0)
    one = pl.Buffered(1)
    return pl.pallas_call(
        _outffn_body,
        grid=(pl.cdiv(t, tm),),
        in_specs=[pl.BlockSpec((tm, d), row),
                  pl.BlockSpec((tm, 256), row), pl.BlockSpec((tm, 256), row), pl.BlockSpec((tm, 256), row),
                  pl.BlockSpec((1, 256), const),
                  pl.BlockSpec((tm, 256), row), pl.BlockSpec((tm, 256), row),
                  pl.BlockSpec((1, 256), const),
                  pl.BlockSpec((256, 256), const), pl.BlockSpec((1, 256), const),
                  pl.BlockSpec((tm, DA_WIDTH), row),
                  pl.BlockSpec((1, 8, d), seg),
                  pl.BlockSpec((1, d), const),
                  pl.BlockSpec((d, d), const, pipeline_mode=one),
                  pl.BlockSpec((d, dff), const, pipeline_mode=one),
                  pl.BlockSpec((d, dff), const, pipeline_mode=one),
                  pl.BlockSpec((dff, d), const, pipeline_mode=one)],
        out_specs=pl.BlockSpec((tm, d), row),
        out_shape=jax.ShapeDtypeStruct((t, d), F32),
        compiler_params=_cparams("arbitrary"),
        name="out_proj_ffn",
    )(xs, y_f, y_b, z, g_ssd, y_s5, u, d5, glu_w, glu_b, y_da, modl, g2, w_o, w1, w3, w2)


def _rope_tables(n_lat, n_ctx):
    pos = jnp.arange(n_lat)
    lane = jnp.arange(LANES)
    d = lane % DA_HEAD_DIM
    axis = d // (2 * ROPE_FREQS)
    half = (d % (2 * ROPE_FREQS)) // ROPE_FREQS
    freqs = ROPE_THETA ** (-jnp.arange(ROPE_FREQS, dtype=F32) / ROPE_FREQS)
    coord = jnp.where(axis[None, :] == 0, (pos // GRID_W)[:, None], (pos % GRID_W)[:, None]).astype(F32)
    ang = coord * freqs[d % ROPE_FREQS][None, :]
    cos, sin = jnp.cos(ang), jnp.sin(ang)
    sa = jnp.where(half[None, :] == 0, -sin, 0.0)
    sb = jnp.where(half[None, :] == 1, sin, 0.0)
    pad = lambda a, v: jnp.concatenate([a, jnp.full((n_ctx, LANES), v, F32)], axis=0)
    return pad(cos, 1.0), pad(sa, 0.0), pad(sb, 0.0)


def _lane_row(vals, width=LANES):
    vals = vals.reshape(-1).astype(F32)
    return jnp.zeros((1, width), F32).at[0, :vals.shape[0]].set(vals)


def kernel(x, c, ctx, c_ctx, w_mod, b_mod, norm1, norm2, w_in, w_out, ssd_conv_w, ssd_conv_b, ssd_a_log, ssd_dt_bias, ssd_d, ssd_norm, s5_lam_re, s5_lam_im, s5_log_step, s5_b_re, s5_b_im, s5_c_re, s5_c_im, s5_d, s5_glu_w, s5_glu_b, da_q_norm, da_k_norm, da_lambda, da_sub_norm, ffn_w1, ffn_w3, ffn_w2):
    depth = w_mod.shape[0]
    bsz, n_lat, d = x.shape
    n_ctx = ctx.shape[1]
    t = n_ctx + n_lat
    tk = _pick_tile(t, (ATTN_TK, 1024, 640, 512, 256))
    assert bsz == 1 and n_ctx % ROW_TILE == 0 and n_lat % ROW_TILE == 0 and tk % ROW_TILE == 0
    assert n_ctx % (S5_CHUNK * S5_SCAN_TILE) == 0 and n_lat % (S5_CHUNK * S5_SCAN_TILE) == 0
    assert n_lat % n_ctx == 0 and tk % n_ctx == 0
    n_lat_tiles = n_lat // ROW_TILE

    xs = jnp.concatenate([x[0], ctx[0]], axis=0)
    cv = jnp.zeros((8, d), F32).at[0].set(c[0]).at[1].set(c_ctx)
    mod_all = _mod_call(cv, w_mod, b_mod)
    cos_t, sa_t, sb_t = _rope_tables(n_lat, n_ctx)
    gidx = jnp.arange(DA_WIDTH) // DA_HEAD_DIM
    gmat = (gidx[:, None] == gidx[None, :]).astype(BF16) * (1.0 / DA_HEAD_DIM)
    s5_w = _s5_weights(s5_lam_re, s5_lam_im, s5_log_step, s5_b_re, s5_b_im, s5_c_re, s5_c_im)

    for i in range(depth):
        lam_init = 0.8 - 0.6 * math.exp(-0.3 * i)
        modl = jnp.zeros((2, 8, d), F32).at[:, :6].set(mod_all[i, :2].reshape(2, 6, d))

        wi = w_in[i]
        w_r = jnp.concatenate([wi[:, 0:768], wi[:, 776:1032], wi[:, 1032:2568], wi[:, 768:776],
                               jnp.zeros((d, IN_PAD - 2568), F32)], axis=1).astype(BF16)
        gq = jnp.tile(da_q_norm[i], DA_WIDTH // DA_HEAD_DIM)[None, :]
        gk = jnp.tile(da_k_norm[i], DA_WIDTH // DA_HEAD_DIM)[None, :]
        z, xbc_raw, u, dt_raw, qh, kzb, vb = _inproj_call(
            xs, norm1[i][None, :], modl, w_r, gq, gk, gmat, cos_t, sa_t, sb_t, n_lat_tiles, tk)

        dt_r = dt_raw[:, :8].T
        cw = jnp.zeros((8, SSD_CONV_DIM), F32).at[:3].set(ssd_conv_w[i])
        a_neg = -jnp.exp(ssd_a_log[i])
        bias_c = _lane_row(ssd_dt_bias[i])
        a_c = _lane_row(a_neg)
        bias_r = jnp.broadcast_to(ssd_dt_bias[i].reshape(8, 1), (8, SSD_STEP_CHUNKS * SSD_CHUNK))
        a_r = jnp.broadcast_to(a_neg.reshape(8, 1), (8, SSD_STEP_CHUNKS * SSD_CHUNK))
        dsk = jnp.repeat(ssd_d[i], SSD_HEAD_DIM)[None, :]
        y_f, y_b = _ssd_call(xbc_raw, dt_raw, dt_r, cw, ssd_conv_b[i][None, :], bias_c, a_c, bias_r, a_r, dsk,
                             n_lat)

        y_s5 = _s5_mix(u, *(w[i] for w in s5_w), n_lat)

        lf = da_lambda[i]
        lam = (jnp.exp(jnp.sum(lf[0] * lf[1])) - jnp.exp(jnp.sum(lf[2] * lf[3])) + lam_init).reshape(1)
        g_sub = da_sub_norm[i][None, :]
        score_bound = LOG2E * math.sqrt(DA_HEAD_DIM) * jnp.max(jnp.abs(da_q_norm[i])) * jnp.max(jnp.abs(da_k_norm[i]))
        flag = (score_bound <= SCORE_LOG2_LIMIT).astype(jnp.int32).reshape(1)
        y_da = _attn_call(flag, lam, qh, kzb, vb, g_sub, 1.0 - lam_init, n_lat, 0, None, None, None)
        y_da = _attn_call(flag, lam, qh, kzb, vb, g_sub, 1.0 - lam_init, n_ctx, n_lat,
                          n_lat // tk, n_ctx, n_lat % tk, y_rest=y_da)

        xs = _outffn_call(xs, y_f, y_b, z, ssd_norm[i][None, :], y_s5, u, s5_d[i][None, :], s5_glu_w[i],
                          s5_glu_b[i][None, :], y_da, modl, norm2[i][None, :], w_out[i].astype(BF16),
                          ffn_w1[i].astype(BF16), ffn_w3[i].astype(BF16), ffn_w2[i].astype(BF16), n_lat)
    return xs[:n_lat][None]
```

```python
import functools
import math

import jax
import jax.numpy as jnp
from jax import lax
from jax.experimental import pallas as pl
from jax.experimental.pallas import tpu as pltpu

F32 = jnp.float32
BF16 = jnp.bfloat16
HI = lax.Precision.HIGHEST

EPS = 1e-6
GRID_W = 64
ROPE_THETA = 10000.0
ROPE_FREQS = 8

SSD_HEADS = 4
SSD_HEAD_DIM = 64
SSD_INNER = 256
SSD_GROUPS = 2
SSD_STATE = 64
SSD_CHUNK = 128
SSD_STEP_CHUNKS = 2
SSD_CONV_DIM = 512

S5_WIDTH = 256
S5_GROUP = 16
S5_GROUPS = 16
S5_STATE = 64
S5_CHUNK = 16
S5_HALF_GROUPS = S5_GROUPS // 2
S5_HALF_STATE = S5_HALF_GROUPS * S5_STATE
S5_SCAN_TILE = 16
S5_TOEPLITZ_BLOCKS = 2 * S5_CHUNK - 1

DA_HEADS = 8
DA_HEAD_DIM = 32
DA_V_DIM = 64
DA_WIDTH = 512
ATTN_HEADS_PER_STEP = 2
ATTN_TQ = 2048
ATTN_TK = 1280
LOG2E = 1.4426950408889634
SCORE_LOG2_LIMIT = 100.0

LANES = 128
ROW_TILE = 256
FFN_ROW_TILE = 256
IN_PAD = 2688
VMEM_LIMIT = 56 * 1024 * 1024


def _cparams(*sem):
    return pltpu.CompilerParams(dimension_semantics=sem, vmem_limit_bytes=VMEM_LIMIT)


def _dot(a, b, precision=None):
    return jnp.dot(a, b, preferred_element_type=F32, precision=precision)


def _dot_nt(a, b, precision=None):
    return lax.dot_general(a, b, (((1,), (1,)), ((), ())), preferred_element_type=F32,
                           precision=precision)


def _dot_tn(a, b, precision=None):
    return lax.dot_general(a, b, (((0,), (0,)), ((), ())), preferred_element_type=F32,
                           precision=precision)


def _silu(x):
    return x * jax.nn.sigmoid(x)


def _pick_tile(n, candidates):
    for c in candidates:
        if n % c == 0:
            return c
    return n


def _segment_order(i, n_lat, n_all, reverse):
    n_ctx = n_all - n_lat
    if not reverse:
        return jnp.where(i < n_ctx, n_lat + i, i - n_ctx)
    return jnp.where(i < n_ctx, n_all - 1 - i, n_lat - 1 - (i - n_ctx))


def _mod_body(cv_ref, w_ref, b_ref, o_ref):
    o_ref[0] = _dot(_silu(cv_ref[...]), w_ref[0], HI) + b_ref[0]


def _mod_call(cv, w_mod, b_mod):
    depth, d, n = w_mod.shape
    tn = 1024
    return pl.pallas_call(
        _mod_body,
        grid=(depth, n // tn),
        in_specs=[pl.BlockSpec((8, d), lambda l, j: (0, 0)),
                  pl.BlockSpec((1, d, tn), lambda l, j: (l, 0, j)),
                  pl.BlockSpec((1, 1, tn), lambda l, j: (l, 0, j))],
        out_specs=pl.BlockSpec((1, 8, tn), lambda l, j: (l, 0, j)),
        out_shape=jax.ShapeDtypeStruct((depth, 8, n), F32),
        compiler_params=_cparams("arbitrary", "arbitrary"),
        name="adaln_mod",
    )(cv, w_mod, b_mod.reshape(depth, 1, n))


def _inproj_project(x_ref, g_ref, mod_ref, w_ref, z_ref, xbc_ref, u_ref, dt_ref, qkv_ref):
    x = x_ref[...]
    mod = mod_ref[0]
    ms = jnp.mean(x * x, axis=-1, keepdims=True)
    h = x * lax.rsqrt(ms + EPS) * g_ref[...] * (1.0 + mod[1:2]) + mod[0:1]
    p = _dot(h.astype(BF16), w_ref[...])
    z_ref[...] = p[:, 0:256]
    xbc_ref[...] = p[:, 256:768]
    u_ref[...] = p[:, 768:1024]
    dt_ref[...] = p[:, 2560:2688]
    qkv_ref[...] = p[:, 1024:2560]


def _inproj_attn_operands(qkv_ref, gq_ref, gk_ref, gm_ref, cos_ref, sa_ref, sb_ref, q_ref, k_ref, v_ref):
    p = qkv_ref[...]
    rows = p.shape[0]
    gm = gm_ref[...]
    cos = cos_ref[...]
    sa = sa_ref[...]
    sb = sb_ref[...]

    def norm_rope(t, gain, scale):
        ms32 = _dot((t * t).astype(BF16), gm)
        tn = t * lax.rsqrt(ms32 + EPS) * gain
        outs = []
        for j in range(DA_WIDTH // LANES):
            tb = tn[:, j * LANES:(j + 1) * LANES]
            ob = tb * cos + pltpu.roll(tb, LANES - ROPE_FREQS, 1) * sa + pltpu.roll(tb, ROPE_FREQS, 1) * sb
            outs.append(ob * scale)
        return jnp.concatenate(outs, axis=1)

    qn = norm_rope(p[:, 0:512], gq_ref[...], LOG2E * DA_HEAD_DIM ** -0.5).astype(BF16)
    kt = norm_rope(p[:, 512:1024], gk_ref[...], 1.0).T.astype(BF16)
    vv = p[:, 1024:1536].astype(BF16)
    lane = lax.broadcasted_iota(jnp.int32, (rows, LANES - DA_V_DIM), 1)
    ones_col = jnp.where(lane == 0, 1.0, 0.0).astype(BF16)
    zeros_k = jnp.zeros((DA_HEAD_DIM, rows), BF16)
    for hd in range(DA_HEADS):
        lo_, mid, hi_ = hd * DA_V_DIM, hd * DA_V_DIM + DA_HEAD_DIM, (hd + 1) * DA_V_DIM
        q_ref[hd] = qn[:, lo_:hi_]
        k_ref[hd, 0, 0, 0:DA_HEAD_DIM, :] = kt[lo_:mid]
        k_ref[hd, 0, 0, DA_HEAD_DIM:, :] = zeros_k
        k_ref[hd, 0, 1, 0:DA_HEAD_DIM, :] = zeros_k
        k_ref[hd, 0, 1, DA_HEAD_DIM:, :] = kt[mid:hi_]
        v_ref[hd, 0] = jnp.concatenate([vv[:, lo_:hi_], ones_col], axis=1)


def _inproj_body(x_ref, g_ref, mod_ref, w_ref, gq_ref, gk_ref, gm_ref, cos_ref, sa_ref, sb_ref,
                 z_ref, xbc_ref, u_ref, dt_ref, q_ref, k_ref, v_ref, qkv_a_ref, qkv_b_ref):
    i = pl.program_id(0)

    @pl.when(i == 0)
    def _():
        qkv_b_ref[...] = jnp.zeros_like(qkv_b_ref)

    def step(cur_ref, prev_ref):
        _inproj_attn_operands(prev_ref, gq_ref, gk_ref, gm_ref, cos_ref, sa_ref, sb_ref, q_ref, k_ref, v_ref)
        _inproj_project(x_ref, g_ref, mod_ref, w_ref, z_ref, xbc_ref, u_ref, dt_ref, cur_ref)

    @pl.when(i % 2 == 0)
    def _():
        step(qkv_a_ref, qkv_b_ref)

    @pl.when(i % 2 == 1)
    def _():
        step(qkv_b_ref, qkv_a_ref)


def _inproj_call(xs, g1, modl, w_r, gq, gk, gmat, cos_t, sa_t, sb_t, n_lat_tiles, tk):
    t, d = xs.shape
    tm = ROW_TILE
    n_tiles = t // tm
    per = tk // tm
    row = lambda i: (jnp.minimum(i, n_tiles - 1), 0)
    done = lambda i: (jnp.maximum(i - 1, 0), 0)
    const = lambda i: (0, 0)
    seg = lambda i: ((jnp.minimum(i, n_tiles - 1) >= n_lat_tiles).astype(jnp.int32), 0, 0)
    flat = [(256, F32), (512, F32), (256, F32), (LANES, F32)]

    def k_block(i):
        j = jnp.maximum(i - 1, 0)
        return (0, j // per, 0, 0, j % per)

    def v_block(i):
        j = jnp.maximum(i - 1, 0)
        return (0, j // per, j % per, 0)

    return pl.pallas_call(
        _inproj_body,
        grid=(n_tiles + 1,),
        in_specs=[pl.BlockSpec((tm, d), row),
                  pl.BlockSpec((1, d), const),
                  pl.BlockSpec((1, 8, d), seg),
                  pl.BlockSpec((d, IN_PAD), const),
                  pl.BlockSpec((1, DA_WIDTH), const),
                  pl.BlockSpec((1, DA_WIDTH), const),
                  pl.BlockSpec((DA_WIDTH, DA_WIDTH), const),
                  pl.BlockSpec((tm, LANES), done),
                  pl.BlockSpec((tm, LANES), done),
                  pl.BlockSpec((tm, LANES), done)],
        out_specs=[pl.BlockSpec((tm, w), row) for w, _ in flat]
                  + [pl.BlockSpec((DA_HEADS, tm, 2 * DA_HEAD_DIM), lambda i: (0, jnp.maximum(i - 1, 0), 0)),
                     pl.BlockSpec((DA_HEADS, 1, 2, 2 * DA_HEAD_DIM, tm), k_block),
                     pl.BlockSpec((DA_HEADS, 1, tm, LANES), v_block)],
        out_shape=[jax.ShapeDtypeStruct((t, w), dt) for w, dt in flat]
                  + [jax.ShapeDtypeStruct((DA_HEADS, t, 2 * DA_HEAD_DIM), BF16),
                     jax.ShapeDtypeStruct((DA_HEADS, t // tk, 2, 2 * DA_HEAD_DIM, tk), BF16),
                     jax.ShapeDtypeStruct((DA_HEADS, t // tk, tk, LANES), BF16)],
        scratch_shapes=[pltpu.VMEM((tm, 3 * DA_WIDTH), F32)] * 2,
        compiler_params=_cparams("arbitrary"),
        name="in_proj",
    )(xs, g1, modl, w_r, gq, gk, gmat, cos_t, sa_t, sb_t)


def _softplus(x):
    return jnp.maximum(x, 0.0) + jnp.log1p(jnp.exp(-jnp.abs(x)))


def _ssd_block(c, n_lat, n_blocks, reverse, xc_ref, xp_ref, xn_ref, dtc_ref, dtr_ref, cw_ref, cb_ref,
               bias_c_ref, a_c_ref, bias_r_ref, a_r_ref, st_ref):
    L = SSD_CHUNK
    rows = xc_ref.shape[0]
    x = xc_ref[...]
    seg_first = jnp.logical_or(c == 0, c == n_lat)
    seg_last = jnp.logical_or(c == n_lat - 1, c == n_blocks - 1)
    prow = jnp.where(seg_first, 0.0, xp_ref[7:8, :])
    nrow = jnp.where(seg_last, 0.0, xn_ref[0:1, :])
    ridx = lax.broadcasted_iota(jnp.int32, (rows, 1), 0)
    xprev = jnp.where(ridx == 0, prow, pltpu.roll(x, 1, 0))
    xnext = jnp.where(ridx == rows - 1, nrow, pltpu.roll(x, rows - 1, 0))
    cw = cw_ref[...]
    conv = xprev * cw[0:1] + x * cw[1:2] + xnext * cw[2:3] + cb_ref[...]
    xbc = _silu(conv)

    dt_c = _softplus(dtc_ref[...] + bias_c_ref[...])
    dt_r = _softplus(dtr_ref[...] + bias_r_ref[...])
    adt_c = dt_c * a_c_ref[...]
    adt_r = dt_r * a_r_ref[...]

    li = lax.broadcasted_iota(jnp.int32, (L, L), 0)
    si = lax.broadcasted_iota(jnp.int32, (L, L), 1)
    mask = (li <= si) if reverse else (li >= si)
    mask_t = (li >= si) if reverse else (li <= si)
    n_sub = rows // L
    ys = [None] * n_sub
    for s in (reversed(range(n_sub)) if reverse else range(n_sub)):
        r = slice(s * L, (s + 1) * L)
        ys[s] = _ssd_chunk(xbc[r], dt_c[r], adt_c[r], adt_r[:, r], mask, mask_t, st_ref, reverse)
    return jnp.concatenate(ys, axis=0), xbc[:, :SSD_INNER]


def _ssd_chunk(xbc, dt_c, adt_c, adt_r, mask, mask_t, st_ref, reverse):
    L = SSD_CHUNK
    off = SSD_HEADS if reverse else 0
    cs_c = _dot(mask.astype(F32), adt_c, HI)
    cs_r = _dot(adt_r, mask_t.astype(F32), HI)
    edge = 0 if reverse else L - 1

    ys = []
    for g in range(SSD_GROUPS):
        bg = xbc[:, SSD_INNER + g * SSD_STATE:SSD_INNER + (g + 1) * SSD_STATE].astype(BF16)
        cg = xbc[:, SSD_INNER + (SSD_GROUPS + g) * SSD_STATE:SSD_INNER + (SSD_GROUPS + g + 1) * SSD_STATE]
        cg = cg.astype(BF16)
        gmat = _dot_nt(cg, bg)
        for hh in range(SSD_HEADS // SSD_GROUPS):
            h = g * (SSD_HEADS // SSD_GROUPS) + hh
            col = off + h
            csc = cs_c[:, col:col + 1]
            csr = cs_r[col:col + 1, :]
            dec = jnp.exp(jnp.where(mask, csc - csr, -jnp.inf))
            xh = xbc[:, h * SSD_HEAD_DIM:(h + 1) * SSD_HEAD_DIM]
            xd = xh * dt_c[:, col:col + 1]
            st = st_ref[h]
            y = _dot((gmat * dec).astype(BF16), xd.astype(BF16))
            y += jnp.exp(csc) * _dot_nt(cg, st.astype(BF16))
            tot = cs_c[edge:edge + 1, col:col + 1]
            xdt = (xd * jnp.exp(tot - csc)).T.astype(BF16)
            st_ref[h] = st * jnp.exp(tot) + _dot(xdt, bg)
            ys.append(y)
    return jnp.concatenate(ys, axis=1)


def _ssd_body(n_lat, n_chunks, *refs):
    fwd_in, bwd_in = refs[0:5], refs[5:10]
    cw_ref, cb_ref, bias_c_ref, a_c_ref, bias_r_ref, a_r_ref, dsk_ref, yf_ref, yb_ref, st_ref = refs[10:]
    i = pl.program_id(0)

    @pl.when(i == 0)
    def _():
        st_ref[...] = jnp.zeros_like(st_ref)

    shared = (cw_ref, cb_ref, bias_c_ref, a_c_ref, bias_r_ref, a_r_ref)
    y, xs = _ssd_block(_segment_order(i, n_lat, n_chunks, False), n_lat, n_chunks, False, *fwd_in, *shared,
                       st_ref.at[0])
    yf_ref[...] = y + dsk_ref[...] * xs
    y, _ = _ssd_block(_segment_order(i, n_lat, n_chunks, True), n_lat, n_chunks, True, *bwd_in, *shared,
                      st_ref.at[1])
    yb_ref[...] = y


def _ssd_call(xbc_raw, dt_c, dt_r, cw, cb, bias_c, a_c, bias_r, a_r, dsk, n_lat_rows):
    t = xbc_raw.shape[0]
    L = SSD_STEP_CHUNKS * SSD_CHUNK
    assert t % L == 0 and n_lat_rows % L == 0
    n_chunks = t // L
    n_lat = n_lat_rows // L
    sub = L // 8
    n_sub = t // 8
    const = lambda i: (0, 0)

    def chunk_specs(reverse):
        cidx = functools.partial(_segment_order, n_lat=n_lat, n_all=n_chunks, reverse=reverse)
        return [pl.BlockSpec((L, SSD_CONV_DIM), lambda i: (cidx(i), 0)),
                pl.BlockSpec((8, SSD_CONV_DIM), lambda i: (jnp.maximum(cidx(i) * sub - 1, 0), 0)),
                pl.BlockSpec((8, SSD_CONV_DIM), lambda i: (jnp.minimum((cidx(i) + 1) * sub, n_sub - 1), 0)),
                pl.BlockSpec((L, LANES), lambda i: (cidx(i), 0)),
                pl.BlockSpec((8, L), lambda i: (0, cidx(i)))]

    def out_spec(reverse):
        cidx = functools.partial(_segment_order, n_lat=n_lat, n_all=n_chunks, reverse=reverse)
        return pl.BlockSpec((L, SSD_INNER), lambda i: (cidx(i), 0))

    chunk_args = (xbc_raw, xbc_raw, xbc_raw, dt_c, dt_r)
    return pl.pallas_call(
        functools.partial(_ssd_body, n_lat, n_chunks),
        grid=(n_chunks,),
        in_specs=chunk_specs(False) + chunk_specs(True)
                 + [pl.BlockSpec((8, SSD_CONV_DIM), const),
                    pl.BlockSpec((1, SSD_CONV_DIM), const),
                    pl.BlockSpec((1, LANES), const),
                    pl.BlockSpec((1, LANES), const),
                    pl.BlockSpec((8, L), const),
                    pl.BlockSpec((8, L), const),
                    pl.BlockSpec((1, SSD_INNER), const)],
        out_specs=[out_spec(False), out_spec(True)],
        out_shape=[jax.ShapeDtypeStruct((t, SSD_INNER), F32)] * 2,
        scratch_shapes=[pltpu.VMEM((2, SSD_HEADS, SSD_HEAD_DIM, SSD_STATE), F32)],
        compiler_params=_cparams("arbitrary"),
        name="ssd_scan",
    )(*chunk_args, *chunk_args, cw, cb, bias_c, a_c, bias_r, a_r, dsk)


def _cmul(ar, ai, br, bi):
    return ar * br - ai * bi, ar * bi + ai * br


def _s5_in_body(u_ref, w_ref, lam_ref, efr_ref, efi_ref, ebr_ref, ebi_ref, xs_ref, bu_ref):
    tn = u_ref.shape[0]
    hs = S5_HALF_STATE
    for i in range(S5_CHUNK):
        xs_ref[i * tn:(i + 1) * tn, :] = u_ref[:, i, :]
    bu_ref[...] = _dot(xs_ref[...].astype(BF16), w_ref[0])
    lam = lam_ref[0]
    fr, fi, br, bi = lam[0:1], lam[1:2], lam[2:3], lam[3:4]
    ar = jnp.zeros((tn, hs), F32)
    ai = jnp.zeros((tn, hs), F32)
    for i in range(S5_CHUNK):
        ar, ai = _cmul(ar, ai, fr, fi)
        ar = ar + bu_ref[i * tn:(i + 1) * tn, 0:hs]
        ai = ai + bu_ref[i * tn:(i + 1) * tn, hs:2 * hs]
    efr_ref[...] = ar
    efi_ref[...] = ai
    ar = jnp.zeros((tn, hs), F32)
    ai = jnp.zeros((tn, hs), F32)
    for i in reversed(range(S5_CHUNK)):
        ar, ai = _cmul(ar, ai, br, bi)
        ar = ar + bu_ref[i * tn:(i + 1) * tn, 2 * hs:3 * hs]
        ai = ai + bu_ref[i * tn:(i + 1) * tn, 3 * hs:4 * hs]
    ebr_ref[...] = ar
    ebi_ref[...] = ai


def _s5_in_call(u3, w_b, lam_rows):
    nch = u3.shape[0]
    tn = _pick_tile(nch, (104, 80, 40))
    hs = S5_HALF_STATE
    col = lambda h, n: (n, h)
    return pl.pallas_call(
        _s5_in_body,
        grid=(2, nch // tn),
        in_specs=[pl.BlockSpec((tn, S5_CHUNK, LANES), lambda h, n: (n, 0, h)),
                  pl.BlockSpec((1, LANES, 4 * hs), lambda h, n: (h, 0, 0)),
                  pl.BlockSpec((1, 4, hs), lambda h, n: (h, 0, 0))],
        out_specs=[pl.BlockSpec((tn, hs), col)] * 4,
        out_shape=[jax.ShapeDtypeStruct((nch, 2 * hs), F32)] * 4,
        scratch_shapes=[pltpu.VMEM((tn * S5_CHUNK, LANES), F32), pltpu.VMEM((tn * S5_CHUNK, 4 * hs), F32)],
        compiler_params=_cparams("arbitrary", "arbitrary"),
        name="s5_chunk_in",
    )(u3, w_b, lam_rows)


def _s5_scan_body(a_ref, efr_ref, efi_ref, ebr_ref, ebi_ref, hfr_ref, hfi_ref, hbr_ref, hbi_ref, st_ref):
    @pl.when(pl.program_id(0) == 0)
    def _():
        st_ref[...] = jnp.zeros_like(st_ref)

    afr, afi, abr, abi = a_ref[0], a_ref[1], a_ref[2], a_ref[3]
    fr, fi, br, bi = st_ref[0], st_ref[1], st_ref[2], st_ref[3]
    for j in range(S5_SCAN_TILE):
        hfr_ref[j] = fr
        hfi_ref[j] = fi
        fr, fi = afr * fr - afi * fi + efr_ref[j], afr * fi + afi * fr + efi_ref[j]
        jb = S5_SCAN_TILE - 1 - j
        hbr_ref[jb] = br
        hbi_ref[jb] = bi
        br, bi = abr * br - abi * bi + ebr_ref[jb], abr * bi + abi * br + ebi_ref[jb]
    st_ref[0] = fr
    st_ref[1] = fi
    st_ref[2] = br
    st_ref[3] = bi


def _s5_scan_call(a_pow, efr, efi, ebr, ebi, n_lat_tiles):
    nch, sub, _ = efr.shape
    n_tiles = nch // S5_SCAN_TILE
    blk = (S5_SCAN_TILE, sub, LANES)
    fwd = lambda i: (_segment_order(i, n_lat_tiles, n_tiles, False), 0, 0)
    bwd = lambda i: (_segment_order(i, n_lat_tiles, n_tiles, True), 0, 0)
    shp = jax.ShapeDtypeStruct((nch, sub, LANES), F32)
    return pl.pallas_call(
        _s5_scan_body,
        grid=(n_tiles,),
        in_specs=[pl.BlockSpec((4, sub, LANES), lambda i: (0, 0, 0)),
                  pl.BlockSpec(blk, fwd), pl.BlockSpec(blk, fwd),
                  pl.BlockSpec(blk, bwd), pl.BlockSpec(blk, bwd)],
        out_specs=[pl.BlockSpec(blk, fwd), pl.BlockSpec(blk, fwd),
                   pl.BlockSpec(blk, bwd), pl.BlockSpec(blk, bwd)],
        out_shape=[shp] * 4,
        scratch_shapes=[pltpu.VMEM((4, sub, LANES), F32)],
        compiler_params=_cparams("arbitrary"),
        name="s5_state_scan",
    )(a_pow, efr, efi, ebr, ebi)


def _s5_out_body(u_ref, hfr_ref, hfi_ref, hbr_ref, hbi_ref, z_ref, c_ref, lam_ref, y_ref, gb_ref):
    hs = S5_HALF_STATE
    width = S5_CHUNK * LANES
    lam = lam_ref[0]
    fr, fi, br, bi = lam[0:1], lam[1:2], lam[2:3], lam[3:4]
    ytz = None
    for i in range(S5_CHUNK):
        xi = u_ref[:, i, :].astype(BF16)
        lo = (S5_CHUNK - 1 - i) * LANES
        part = _dot(xi, z_ref[0, :, lo:lo + width])
        ytz = part if ytz is None else ytz + part
    gr, gi = hbr_ref[...], hbi_ref[...]
    for j in reversed(range(S5_CHUNK)):
        gr, gi = _cmul(gr, gi, br, bi)
        gb_ref[j, :, 0:hs] = gr
        gb_ref[j, :, hs:] = gi
    cw = c_ref[0]
    gr, gi = hfr_ref[...], hfi_ref[...]
    for j in range(S5_CHUNK):
        gr, gi = _cmul(gr, gi, fr, fi)
        g_all = jnp.concatenate([gr, gi, gb_ref[j]], axis=1).astype(BF16)
        y_ref[:, j, :] = ytz[:, j * LANES:(j + 1) * LANES] + _dot(g_all, cw)


def _s5_out_call(u3, hfr, hfi, hbr, hbi, w_z, w_c, lam_rows):
    nch = u3.shape[0]
    tn = _pick_tile(nch, (208, 80, 40))
    hs = S5_HALF_STATE
    col = lambda h, n: (n, h)
    blk3 = pl.BlockSpec((tn, S5_CHUNK, LANES), lambda h, n: (n, 0, h))
    return pl.pallas_call(
        _s5_out_body,
        grid=(2, nch // tn),
        in_specs=[blk3] + [pl.BlockSpec((tn, hs), col)] * 4
                 + [pl.BlockSpec((1, LANES, S5_TOEPLITZ_BLOCKS * LANES), lambda h, n: (h, 0, 0)),
                    pl.BlockSpec((1, 4 * hs, LANES), lambda h, n: (h, 0, 0)),
                    pl.BlockSpec((1, 4, hs), lambda h, n: (h, 0, 0))],
        out_specs=blk3,
        out_shape=jax.ShapeDtypeStruct(u3.shape, F32),
        scratch_shapes=[pltpu.VMEM((S5_CHUNK, tn, 2 * hs), F32)],
        compiler_params=_cparams("arbitrary", "arbitrary"),
        name="s5_chunk_out",
    )(u3, hfr, hfi, hbr, hbi, w_z, w_c, lam_rows)


def _s5_weights(lam_re, lam_im, log_step, b_re, b_im, c_re, c_im):
    tc = S5_CHUNK
    nl = lam_re.shape[0]
    hg = S5_HALF_GROUPS
    ein = functools.partial(jnp.einsum, precision=HI)
    step = jnp.exp(log_step)[..., None]
    er, ei = lam_re * step, lam_im * step
    mag = jnp.exp(er)
    lbr, lbi = mag * jnp.cos(ei), mag * jnp.sin(ei)
    den = lam_re * lam_re + lam_im * lam_im
    nr, ni = lbr - 1.0, lbi
    cr, ci = (nr * lam_re + ni * lam_im) / den, (ni * lam_re - nr * lam_im) / den
    bbr = cr[..., None] * b_re - ci[..., None] * b_im
    bbi = cr[..., None] * b_im + ci[..., None] * b_re
    taus = jnp.arange(tc + 1, dtype=F32)[:, None, None]
    pmag = jnp.exp(er[:, :, None] * taus)
    pr, pi = pmag * jnp.cos(ei[:, :, None] * taus), pmag * jnp.sin(ei[:, :, None] * taus)

    wr = pr[:, :, :tc, :, :, None] * bbr[:, :, None] - pi[:, :, :tc, :, :, None] * bbi[:, :, None]
    wi = pr[:, :, :tc, :, :, None] * bbi[:, :, None] + pi[:, :, :tc, :, :, None] * bbr[:, :, None]
    kk = ein('ldgcp,ldtgpk->ldtgck', c_re, wr) - ein('ldgcp,ldtgpk->ldtgck', c_im, wi)
    eye = jnp.eye(hg, dtype=F32)
    kk = kk.reshape(nl, 2, tc, 2, hg, S5_GROUP, S5_GROUP)
    bd = jnp.einsum('ldthgck,gj->ldthgkjc', kk, eye).reshape(nl, 2, tc, 2, LANES, LANES)
    bdf, bdb = bd[:, 0], bd[:, 1]
    strip = jnp.concatenate([bdb[:, :0:-1], (bdf[:, :1] + bdb[:, :1]), bdf[:, 1:]], axis=1)
    w_z = strip.transpose(0, 2, 3, 1, 4).reshape(nl, 2, LANES, S5_TOEPLITZ_BLOCKS * LANES)

    planes_b = jnp.stack([bbr[:, 0], bbi[:, 0], bbr[:, 1], bbi[:, 1]], axis=1)
    planes_b = planes_b.reshape(nl, 4, 2, hg, S5_STATE, S5_GROUP)
    w_b = jnp.einsum('lqhgpc,gj->lhgcqjp', planes_b, eye).reshape(nl, 2, LANES, 4 * S5_HALF_STATE)
    planes_c = jnp.stack([c_re[:, 0], -c_im[:, 0], c_re[:, 1], -c_im[:, 1]], axis=1)
    planes_c = planes_c.reshape(nl, 4, 2, hg, S5_GROUP, S5_STATE)
    w_c = jnp.einsum('lqhgcp,gj->lhqjpgc', planes_c, eye).reshape(nl, 2, 4 * S5_HALF_STATE, LANES)

    halves = lambda a: a.reshape(nl, 2, S5_HALF_STATE)
    lam_rows = jnp.stack([halves(lbr[:, 0]), halves(lbi[:, 0]), halves(lbr[:, 1]), halves(lbi[:, 1])], axis=2)
    full = lambda a: a.reshape(nl, S5_GROUPS * S5_STATE // LANES, LANES)
    a_pow = jnp.stack([full(pr[:, 0, tc]), full(pi[:, 0, tc]), full(pr[:, 1, tc]), full(pi[:, 1, tc])], axis=1)
    return w_b.astype(BF16), w_z.astype(BF16), w_c.astype(BF16), lam_rows, a_pow


def _s5_mix(u, w_b, w_z, w_c, lam_rows, a_pow, n_lat):
    t = u.shape[0]
    nch = t // S5_CHUNK
    sub = S5_GROUPS * S5_STATE // LANES
    u3 = u.reshape(nch, S5_CHUNK, S5_WIDTH)
    es = _s5_in_call(u3, w_b, lam_rows)
    es = [e.reshape(nch, sub, LANES) for e in es]
    hs = _s5_scan_call(a_pow, *es, n_lat // (S5_CHUNK * S5_SCAN_TILE))
    hs = [h.reshape(nch, sub * LANES) for h in hs]
    return _s5_out_call(u3, *hs, w_z, w_c, lam_rows).reshape(t, S5_WIDTH)


def _attn_body(flag_ref, lam_ref, q_ref, k_ref, v_ref, g_ref, o_ref, acc_ref, m_ref, oh_ref, *, n_kv, post_scale):
    bounded = flag_ref[0] == 1

    def head(hh, carry):
        q = q_ref[hh]
        acc_ref[...] = jnp.zeros_like(acc_ref)

        @pl.when(bounded)
        def _():
            def step(b, carry2):
                v = v_ref[hh, b]
                for c in range(2):
                    p = jnp.exp2(_dot(q, k_ref[hh, b, c])).astype(BF16)
                    acc_ref[c] += _dot(p, v)
                return carry2
            lax.fori_loop(0, n_kv, step, 0)

        @pl.when(jnp.logical_not(bounded))
        def _():
            m_ref[...] = jnp.full_like(m_ref, -jnp.inf)

            def step(b, carry2):
                v = v_ref[hh, b]
                for c in range(2):
                    s = _dot(q, k_ref[hh, b, c])
                    m_old = m_ref[c]
                    m_new = jnp.maximum(m_old, jnp.max(s, axis=1, keepdims=True))
                    p = jnp.exp2(s - m_new[:, 0:1]).astype(BF16)
                    acc_ref[c] = jnp.exp2(m_old - m_new) * acc_ref[c] + _dot(p, v)
                    m_ref[c] = m_new
                return carry2
            lax.fori_loop(0, n_kv, step, 0)

        a0 = acc_ref[0]
        a1 = acc_ref[1]
        o = a0[:, :DA_V_DIM] / a0[:, DA_V_DIM:DA_V_DIM + 1] \
            - lam_ref[0] * (a1[:, :DA_V_DIM] / a1[:, DA_V_DIM:DA_V_DIM + 1])
        ms = jnp.mean(o * o, axis=-1, keepdims=True)
        oh_ref[hh] = o * lax.rsqrt(ms + EPS) * g_ref[...] * post_scale
        return carry

    lax.fori_loop(0, ATTN_HEADS_PER_STEP, head, 0)
    o_ref[...] = jnp.concatenate([oh_ref[hh] for hh in range(ATTN_HEADS_PER_STEP)], axis=1)


def _attn_body_keep_rest(flag_ref, lam_ref, q_ref, k_ref, v_ref, g_ref, rest_ref, *refs, **kw):
    del rest_ref
    _attn_body(flag_ref, lam_ref, q_ref, k_ref, v_ref, g_ref, *refs, **kw)


def _attn_call(flag, lam, qh, kzb, vb, g_sub, post_scale, q_rows, q_row0, kv_block, kv_cols, kv_col0, y_rest=None):
    nh = qh.shape[0]
    hp = ATTN_HEADS_PER_STEP
    tk = kzb.shape[4]
    tq = _pick_tile(q_rows, (ATTN_TQ, 256))
    assert q_row0 % tq == 0
    q0 = q_row0 // tq
    if kv_block is None:
        n_kv = kzb.shape[1]
        one = pl.Buffered(1)
        k_spec = pl.BlockSpec((hp, n_kv, 2, 2 * DA_HEAD_DIM, tk), lambda h, i: (h, 0, 0, 0, 0), pipeline_mode=one)
        v_spec = pl.BlockSpec((hp, n_kv, tk, LANES), lambda h, i: (h, 0, 0, 0), pipeline_mode=one)
    else:
        assert kv_col0 % kv_cols == 0 and tk % kv_cols == 0
        n_kv, cb = 1, kv_col0 // kv_cols
        k_spec = pl.BlockSpec((hp, 1, 2, 2 * DA_HEAD_DIM, kv_cols), lambda h, i: (h, kv_block, 0, 0, cb))
        v_spec = pl.BlockSpec((hp, 1, kv_cols, LANES), lambda h, i: (h, kv_block, cb, 0))
    smem = pl.BlockSpec(memory_space=pltpu.SMEM)
    in_specs = [smem, smem,
                pl.BlockSpec((hp, tq, 2 * DA_HEAD_DIM), lambda h, i: (h, q0 + i, 0)),
                k_spec, v_spec,
                pl.BlockSpec((1, DA_V_DIM), lambda h, i: (0, 0))]
    args = (flag, lam, qh, kzb, vb, g_sub)
    body = functools.partial(_attn_body, n_kv=n_kv, post_scale=post_scale)
    aliases = {}
    if y_rest is not None:
        in_specs.append(pl.BlockSpec(memory_space=pl.ANY))
        args += (y_rest,)
        body = functools.partial(_attn_body_keep_rest, n_kv=n_kv, post_scale=post_scale)
        aliases = {len(args) - 1: 0}
    return pl.pallas_call(
        body,
        grid=(nh // hp, q_rows // tq),
        in_specs=in_specs,
        out_specs=pl.BlockSpec((tq, hp * DA_V_DIM), lambda h, i: (q0 + i, h)),
        out_shape=jax.ShapeDtypeStruct((qh.shape[1], nh * DA_V_DIM), F32),
        scratch_shapes=[pltpu.VMEM((2, tq, LANES), F32), pltpu.VMEM((2, tq, LANES), F32),
                        pltpu.VMEM((hp, tq, DA_V_DIM), F32)],
        input_output_aliases=aliases,
        compiler_params=_cparams("arbitrary", "arbitrary"),
        name="diff_attn",
    )(*args)


def _gelu_tanh(x):
    return 0.5 * x * (1.0 + jnp.tanh(math.sqrt(2.0 / math.pi) * (x + 0.044715 * (x * x * x))))


def _outffn_body(x_ref, yf_ref, yb_ref, z_ref, gssd_ref, ys5_ref, u_ref, d5_ref, gw_ref, gb_ref,
                 yda_ref, mod_ref, modp_ref, g2_ref, wo_ref, w1_ref, w3_ref, w2_ref, o_ref, xm_ref, hb_ref):
    i = pl.program_id(0)
    cur = i % 2
    prev = 1 - cur

    @pl.when(i == 0)
    def _():
        xm_ref[...] = jnp.zeros_like(xm_ref)
        hb_ref[...] = jnp.zeros_like(hb_ref)

    hp = hb_ref[prev]
    f = _silu(_dot(hp, w1_ref[...])) * _dot(hp, w3_ref[...])
    o_ref[...] = xm_ref[prev] + modp_ref[0][5:6] * _dot(f.astype(BF16), w2_ref[...])

    mod = mod_ref[0]
    y = (yf_ref[...] + yb_ref[...]) * _silu(z_ref[...])
    y_ssd = y * lax.rsqrt(jnp.mean(y * y, axis=-1, keepdims=True) + EPS) * gssd_ref[...]
    y5 = _gelu_tanh(ys5_ref[...] + d5_ref[...] * u_ref[...])
    y5 = y5 * jax.nn.sigmoid(_dot(y5.astype(BF16), gw_ref[...].astype(BF16)) + gb_ref[...])
    wo = wo_ref[...]
    o = _dot(y_ssd.astype(BF16), wo[0:256])
    o += _dot(y5.astype(BF16), wo[256:512])
    o += _dot(yda_ref[...].astype(BF16), wo[512:1024])
    x = x_ref[...] + mod[2:3] * o
    h = x * lax.rsqrt(jnp.mean(x * x, axis=-1, keepdims=True) + EPS) * g2_ref[...] * (1.0 + mod[4:5]) + mod[3:4]
    xm_ref[cur] = x
    hb_ref[cur] = h.astype(BF16)


def _outffn_call(xs, y_f, y_b, z, g_ssd, y_s5, u, d5, glu_w, glu_b, y_da, modl, g2, w_o, w1, w3, w2, n_lat):
    t, d = xs.shape
    dff = w1.shape[1]
    tm = FFN_ROW_TILE
    assert n_lat % tm == 0 and t % tm == 0
    n_lat_tiles = n_lat // tm
    n_tiles = t // tm
    row = lambda i: (jnp.minimum(i, n_tiles - 1), 0)
    done = lambda i: (jnp.maximum(i - 1, 0), 0)
    const = lambda i: (0, 0)
    seg = lambda i: ((jnp.minimum(i, n_tiles - 1) >= n_lat_tiles).astype(jnp.int32), 0, 0)
    seg_done = lambda i: ((i - 1 >= n_lat_tiles).astype(jnp.int32), 0, 0)
    one = pl.Buffered(1)
    return pl.pallas_call(
        _outffn_body,
        grid=(n_tiles + 1,),
        in_specs=[pl.BlockSpec((tm, d), row),
                  pl.BlockSpec((tm, 256), row), pl.BlockSpec((tm, 256), row), pl.BlockSpec((tm, 256), row),
                  pl.BlockSpec((1, 256), const),
                  pl.BlockSpec((tm, 256), row), pl.BlockSpec((tm, 256), row),
                  pl.BlockSpec((1, 256), const),
                  pl.BlockSpec((256, 256), const), pl.BlockSpec((1, 256), const),
                  pl.BlockSpec((tm, DA_WIDTH), row),
                  pl.BlockSpec((1, 8, d), seg),
                  pl.BlockSpec((1, 8, d), seg_done),
                  pl.BlockSpec((1, d), const),
                  pl.BlockSpec((d, d), const, pipeline_mode=one),
                  pl.BlockSpec((d, dff), const, pipeline_mode=one),
                  pl.BlockSpec((d, dff), const, pipeline_mode=one),
                  pl.BlockSpec((dff, d), const, pipeline_mode=one)],
        out_specs=pl.BlockSpec((tm, d), done),
        out_shape=jax.ShapeDtypeStruct((t, d), F32),
        scratch_shapes=[pltpu.VMEM((2, tm, d), F32), pltpu.VMEM((2, tm, d), BF16)],
        compiler_params=_cparams("arbitrary"),
        name="out_proj_ffn",
    )(xs, y_f, y_b, z, g_ssd, y_s5, u, d5, glu_w, glu_b, y_da, modl, modl, g2, w_o, w1, w3, w2)


def _rope_tables(n_lat, n_ctx):
    pos = jnp.arange(n_lat)
    lane = jnp.arange(LANES)
    d = lane % DA_HEAD_DIM
    axis = d // (2 * ROPE_FREQS)
    half = (d % (2 * ROPE_FREQS)) // ROPE_FREQS
    freqs = ROPE_THETA ** (-jnp.arange(ROPE_FREQS, dtype=F32) / ROPE_FREQS)
    coord = jnp.where(axis[None, :] == 0, (pos // GRID_W)[:, None], (pos % GRID_W)[:, None]).astype(F32)
    ang = coord * freqs[d % ROPE_FREQS][None, :]
    cos, sin = jnp.cos(ang), jnp.sin(ang)
    sa = jnp.where(half[None, :] == 0, -sin, 0.0)
    sb = jnp.where(half[None, :] == 1, sin, 0.0)
    pad = lambda a, v: jnp.concatenate([a, jnp.full((n_ctx, LANES), v, F32)], axis=0)
    return pad(cos, 1.0), pad(sa, 0.0), pad(sb, 0.0)


def _lane_row(vals, width=LANES):
    vals = vals.reshape(-1).astype(F32)
    return jnp.zeros((1, width), F32).at[0, :vals.shape[0]].set(vals)


def kernel(x, c, ctx, c_ctx, w_mod, b_mod, norm1, norm2, w_in, w_out, ssd_conv_w, ssd_conv_b, ssd_a_log, ssd_dt_bias, ssd_d, ssd_norm, s5_lam_re, s5_lam_im, s5_log_step, s5_b_re, s5_b_im, s5_c_re, s5_c_im, s5_d, s5_glu_w, s5_glu_b, da_q_norm, da_k_norm, da_lambda, da_sub_norm, ffn_w1, ffn_w3, ffn_w2):
    depth = w_mod.shape[0]
    bsz, n_lat, d = x.shape
    n_ctx = ctx.shape[1]
    t = n_ctx + n_lat
    tk = _pick_tile(t, (ATTN_TK, 1024, 640, 512, 256))
    assert bsz == 1 and n_ctx % ROW_TILE == 0 and n_lat % ROW_TILE == 0 and tk % ROW_TILE == 0
    assert n_ctx % (S5_CHUNK * S5_SCAN_TILE) == 0 and n_lat % (S5_CHUNK * S5_SCAN_TILE) == 0
    assert n_lat % n_ctx == 0 and tk % n_ctx == 0
    n_lat_tiles = n_lat // ROW_TILE

    xs = jnp.concatenate([x[0], ctx[0]], axis=0)
    cv = jnp.zeros((8, d), F32).at[0].set(c[0]).at[1].set(c_ctx)
    mod_all = _mod_call(cv, w_mod, b_mod)
    cos_t, sa_t, sb_t = _rope_tables(n_lat, n_ctx)
    gidx = jnp.arange(DA_WIDTH) // DA_HEAD_DIM
    gmat = (gidx[:, None] == gidx[None, :]).astype(BF16) * (1.0 / DA_HEAD_DIM)
    s5_w = _s5_weights(s5_lam_re, s5_lam_im, s5_log_step, s5_b_re, s5_b_im, s5_c_re, s5_c_im)

    for i in range(depth):
        lam_init = 0.8 - 0.6 * math.exp(-0.3 * i)
        modl = jnp.zeros((2, 8, d), F32).at[:, :6].set(mod_all[i, :2].reshape(2, 6, d))

        wi = w_in[i]
        w_r = jnp.concatenate([wi[:, 0:768], wi[:, 776:1032], wi[:, 1032:2568], wi[:, 768:776],
                               jnp.zeros((d, IN_PAD - 2568), F32)], axis=1).astype(BF16)
        gq = jnp.tile(da_q_norm[i], DA_WIDTH // DA_HEAD_DIM)[None, :]
        gk = jnp.tile(da_k_norm[i], DA_WIDTH // DA_HEAD_DIM)[None, :]
        z, xbc_raw, u, dt_raw, qh, kzb, vb = _inproj_call(
            xs, norm1[i][None, :], modl, w_r, gq, gk, gmat, cos_t, sa_t, sb_t, n_lat_tiles, tk)

        dt_r = dt_raw[:, :8].T
        cw = jnp.zeros((8, SSD_CONV_DIM), F32).at[:3].set(ssd_conv_w[i])
        a_neg = -jnp.exp(ssd_a_log[i])
        bias_c = _lane_row(ssd_dt_bias[i])
        a_c = _lane_row(a_neg)
        bias_r = jnp.broadcast_to(ssd_dt_bias[i].reshape(8, 1), (8, SSD_STEP_CHUNKS * SSD_CHUNK))
        a_r = jnp.broadcast_to(a_neg.reshape(8, 1), (8, SSD_STEP_CHUNKS * SSD_CHUNK))
        dsk = jnp.repeat(ssd_d[i], SSD_HEAD_DIM)[None, :]
        y_f, y_b = _ssd_call(xbc_raw, dt_raw, dt_r, cw, ssd_conv_b[i][None, :], bias_c, a_c, bias_r, a_r, dsk,
                             n_lat)

        y_s5 = _s5_mix(u, *(w[i] for w in s5_w), n_lat)

        lf = da_lambda[i]
        lam = (jnp.exp(jnp.sum(lf[0] * lf[1])) - jnp.exp(jnp.sum(lf[2] * lf[3])) + lam_init).reshape(1)
        g_sub = da_sub_norm[i][None, :]
        score_bound = LOG2E * math.sqrt(DA_HEAD_DIM) * jnp.max(jnp.abs(da_q_norm[i])) * jnp.max(jnp.abs(da_k_norm[i]))
        flag = (score_bound <= SCORE_LOG2_LIMIT).astype(jnp.int32).reshape(1)
        y_da = _attn_call(flag, lam, qh, kzb, vb, g_sub, 1.0 - lam_init, n_lat, 0, None, None, None)
        y_da = _attn_call(flag, lam, qh, kzb, vb, g_sub, 1.0 - lam_init, n_ctx, n_lat,
                          n_lat // tk, n_ctx, n_lat % tk, y_rest=y_da)

        xs = _outffn_call(xs, y_f, y_b, z, ssd_norm[i][None, :], y_s5, u, s5_d[i][None, :], s5_glu_w[i],
                          s5_glu_b[i][None, :], y_da, modl, norm2[i][None, :], w_out[i].astype(BF16),
                          ffn_w1[i].astype(BF16), ffn_w3[i].astype(BF16), ffn_w2[i].astype(BF16), n_lat)
    return xs[:n_lat][None]
```

```python
import functools
import math

import jax
import jax.numpy as jnp
from jax import lax
from jax.experimental import pallas as pl
from jax.experimental.pallas import tpu as pltpu

F32 = jnp.float32
BF16 = jnp.bfloat16
HI = lax.Precision.HIGHEST

EPS = 1e-6
GRID_W = 64
ROPE_THETA = 10000.0
ROPE_FREQS = 8

SSD_HEADS = 4
SSD_HEAD_DIM = 64
SSD_INNER = 256
SSD_GROUPS = 2
SSD_STATE = 64
SSD_CHUNK = 128
SSD_STEP_CHUNKS = 2
SSD_CONV_DIM = 512

S5_WIDTH = 256
S5_GROUP = 16
S5_GROUPS = 16
S5_STATE = 64
S5_CHUNK = 16
S5_HALF_GROUPS = S5_GROUPS // 2
S5_HALF_STATE = S5_HALF_GROUPS * S5_STATE
S5_SCAN_TILE = 16
S5_TOEPLITZ_BLOCKS = 2 * S5_CHUNK - 1

DA_HEADS = 8
DA_HEAD_DIM = 32
DA_V_DIM = 64
DA_WIDTH = 512
ATTN_HEADS_PER_STEP = 2
ATTN_TQ = 2048
ATTN_TK = 1280
LOG2E = 1.4426950408889634
SCORE_LOG2_LIMIT = 100.0

LANES = 128
ROW_TILE = 256
FFN_ROW_TILE = 256
IN_PAD = 2688
VMEM_LIMIT = 56 * 1024 * 1024


def _cparams(*sem):
    return pltpu.CompilerParams(dimension_semantics=sem, vmem_limit_bytes=VMEM_LIMIT)


def _dot(a, b, precision=None):
    return jnp.dot(a, b, preferred_element_type=F32, precision=precision)


def _dot_nt(a, b, precision=None):
    return lax.dot_general(a, b, (((1,), (1,)), ((), ())), preferred_element_type=F32,
                           precision=precision)


def _dot_tn(a, b, precision=None):
    return lax.dot_general(a, b, (((0,), (0,)), ((), ())), preferred_element_type=F32,
                           precision=precision)


def _silu(x):
    return x * jax.nn.sigmoid(x)


def _pick_tile(n, candidates):
    for c in candidates:
        if n % c == 0:
            return c
    return n


def _segment_order(i, n_lat, n_all, reverse):
    n_ctx = n_all - n_lat
    if not reverse:
        return jnp.where(i < n_ctx, n_lat + i, i - n_ctx)
    return jnp.where(i < n_ctx, n_all - 1 - i, n_lat - 1 - (i - n_ctx))


def _mod_body(cv_ref, w_ref, b_ref, o_ref):
    o_ref[0] = _dot(_silu(cv_ref[...]), w_ref[0], HI) + b_ref[0]


def _mod_call(cv, w_mod, b_mod):
    depth, d, n = w_mod.shape
    tn = 1024
    return pl.pallas_call(
        _mod_body,
        grid=(depth, n // tn),
        in_specs=[pl.BlockSpec((8, d), lambda l, j: (0, 0)),
                  pl.BlockSpec((1, d, tn), lambda l, j: (l, 0, j)),
                  pl.BlockSpec((1, 1, tn), lambda l, j: (l, 0, j))],
        out_specs=pl.BlockSpec((1, 8, tn), lambda l, j: (l, 0, j)),
        out_shape=jax.ShapeDtypeStruct((depth, 8, n), F32),
        compiler_params=_cparams("arbitrary", "arbitrary"),
        name="adaln_mod",
    )(cv, w_mod, b_mod.reshape(depth, 1, n))


def _inproj_project(x_ref, g_ref, mod_ref, w_ref, z_ref, xbc_ref, u_ref, dt_ref, qkv_ref):
    x = x_ref[...]
    mod = mod_ref[0]
    ms = jnp.mean(x * x, axis=-1, keepdims=True)
    h = x * lax.rsqrt(ms + EPS) * g_ref[...] * (1.0 + mod[1:2]) + mod[0:1]
    p = _dot(h.astype(BF16), w_ref[...])
    z_ref[...] = p[:, 0:256]
    xbc_ref[...] = p[:, 256:768]
    u_ref[...] = p[:, 768:1024]
    dt_ref[...] = p[:, 2560:2688]
    qkv_ref[...] = p[:, 1024:2560]


def _inproj_attn_operands(qkv_ref, gq_ref, gk_ref, gm_ref, cos_ref, sa_ref, sb_ref, q_ref, k_ref, v_ref):
    p = qkv_ref[...]
    rows = p.shape[0]
    gm = gm_ref[...]
    cos = cos_ref[...]
    sa = sa_ref[...]
    sb = sb_ref[...]

    def norm_rope(t, gain, scale):
        ms32 = _dot((t * t).astype(BF16), gm)
        tn = t * lax.rsqrt(ms32 + EPS) * gain
        outs = []
        for j in range(DA_WIDTH // LANES):
            tb = tn[:, j * LANES:(j + 1) * LANES]
            ob = tb * cos + pltpu.roll(tb, LANES - ROPE_FREQS, 1) * sa + pltpu.roll(tb, ROPE_FREQS, 1) * sb
            outs.append(ob * scale)
        return jnp.concatenate(outs, axis=1)

    qn = norm_rope(p[:, 0:512], gq_ref[...], LOG2E * DA_HEAD_DIM ** -0.5).astype(BF16)
    kt = norm_rope(p[:, 512:1024], gk_ref[...], 1.0).T.astype(BF16)
    vv = p[:, 1024:1536].astype(BF16)
    lane = lax.broadcasted_iota(jnp.int32, (rows, LANES - DA_V_DIM), 1)
    ones_col = jnp.where(lane == 0, 1.0, 0.0).astype(BF16)
    zeros_k = jnp.zeros((DA_HEAD_DIM, rows), BF16)
    for hd in range(DA_HEADS):
        lo_, mid, hi_ = hd * DA_V_DIM, hd * DA_V_DIM + DA_HEAD_DIM, (hd + 1) * DA_V_DIM
        q_ref[hd] = qn[:, lo_:hi_]
        k_ref[hd, 0, 0, 0:DA_HEAD_DIM, :] = kt[lo_:mid]
        k_ref[hd, 0, 0, DA_HEAD_DIM:, :] = zeros_k
        k_ref[hd, 0, 1, 0:DA_HEAD_DIM, :] = zeros_k
        k_ref[hd, 0, 1, DA_HEAD_DIM:, :] = kt[mid:hi_]
        v_ref[hd, 0] = jnp.concatenate([vv[:, lo_:hi_], ones_col], axis=1)


def _inproj_body(x_ref, g_ref, mod_ref, w_ref, gq_ref, gk_ref, gm_ref, cos_ref, sa_ref, sb_ref,
                 z_ref, xbc_ref, u_ref, dt_ref, q_ref, k_ref, v_ref, qkv_a_ref, qkv_b_ref):
    i = pl.program_id(0)

    @pl.when(i == 0)
    def _():
        qkv_b_ref[...] = jnp.zeros_like(qkv_b_ref)

    def step(cur_ref, prev_ref):
        _inproj_attn_operands(prev_ref, gq_ref, gk_ref, gm_ref, cos_ref, sa_ref, sb_ref, q_ref, k_ref, v_ref)
        _inproj_project(x_ref, g_ref, mod_ref, w_ref, z_ref, xbc_ref, u_ref, dt_ref, cur_ref)

    @pl.when(i % 2 == 0)
    def _():
        step(qkv_a_ref, qkv_b_ref)

    @pl.when(i % 2 == 1)
    def _():
        step(qkv_b_ref, qkv_a_ref)


def _inproj_call(xs, g1, modl, w_r, gq, gk, gmat, cos_t, sa_t, sb_t, n_lat_tiles, tk):
    t, d = xs.shape
    tm = ROW_TILE
    n_tiles = t // tm
    per = tk // tm
    row = lambda i: (jnp.minimum(i, n_tiles - 1), 0)
    done = lambda i: (jnp.maximum(i - 1, 0), 0)
    const = lambda i: (0, 0)
    seg = lambda i: ((jnp.minimum(i, n_tiles - 1) >= n_lat_tiles).astype(jnp.int32), 0, 0)
    flat = [(256, F32), (512, F32), (256, F32), (LANES, F32)]

    def k_block(i):
        j = jnp.maximum(i - 1, 0)
        return (0, j // per, 0, 0, j % per)

    def v_block(i):
        j = jnp.maximum(i - 1, 0)
        return (0, j // per, j % per, 0)

    return pl.pallas_call(
        _inproj_body,
        grid=(n_tiles + 1,),
        in_specs=[pl.BlockSpec((tm, d), row),
                  pl.BlockSpec((1, d), const),
                  pl.BlockSpec((1, 8, d), seg),
                  pl.BlockSpec((d, IN_PAD), const),
                  pl.BlockSpec((1, DA_WIDTH), const),
                  pl.BlockSpec((1, DA_WIDTH), const),
                  pl.BlockSpec((DA_WIDTH, DA_WIDTH), const),
                  pl.BlockSpec((tm, LANES), done),
                  pl.BlockSpec((tm, LANES), done),
                  pl.BlockSpec((tm, LANES), done)],
        out_specs=[pl.BlockSpec((tm, w), row) for w, _ in flat]
                  + [pl.BlockSpec((DA_HEADS, tm, 2 * DA_HEAD_DIM), lambda i: (0, jnp.maximum(i - 1, 0), 0)),
                     pl.BlockSpec((DA_HEADS, 1, 2, 2 * DA_HEAD_DIM, tm), k_block),
                     pl.BlockSpec((DA_HEADS, 1, tm, LANES), v_block)],
        out_shape=[jax.ShapeDtypeStruct((t, w), dt) for w, dt in flat]
                  + [jax.ShapeDtypeStruct((DA_HEADS, t, 2 * DA_HEAD_DIM), BF16),
                     jax.ShapeDtypeStruct((DA_HEADS, t // tk, 2, 2 * DA_HEAD_DIM, tk), BF16),
                     jax.ShapeDtypeStruct((DA_HEADS, t // tk, tk, LANES), BF16)],
        scratch_shapes=[pltpu.VMEM((tm, 3 * DA_WIDTH), F32)] * 2,
        compiler_params=_cparams("arbitrary"),
        name="in_proj",
    )(xs, g1, modl, w_r, gq, gk, gmat, cos_t, sa_t, sb_t)


def _softplus(x):
    return jnp.maximum(x, 0.0) + jnp.log1p(jnp.exp(-jnp.abs(x)))


def _ssd_block(c, n_lat, n_blocks, reverse, xc_ref, xp_ref, xn_ref, dtc_ref, dtr_ref, cw_ref, cb_ref,
               bias_c_ref, a_c_ref, bias_r_ref, a_r_ref, st_ref):
    L = SSD_CHUNK
    rows = xc_ref.shape[0]
    x = xc_ref[...]
    seg_first = jnp.logical_or(c == 0, c == n_lat)
    seg_last = jnp.logical_or(c == n_lat - 1, c == n_blocks - 1)
    prow = jnp.where(seg_first, 0.0, xp_ref[7:8, :])
    nrow = jnp.where(seg_last, 0.0, xn_ref[0:1, :])
    ridx = lax.broadcasted_iota(jnp.int32, (rows, 1), 0)
    xprev = jnp.where(ridx == 0, prow, pltpu.roll(x, 1, 0))
    xnext = jnp.where(ridx == rows - 1, nrow, pltpu.roll(x, rows - 1, 0))
    cw = cw_ref[...]
    conv = xprev * cw[0:1] + x * cw[1:2] + xnext * cw[2:3] + cb_ref[...]
    xbc = _silu(conv)

    dt_c = _softplus(dtc_ref[...] + bias_c_ref[...])
    dt_r = _softplus(dtr_ref[...] + bias_r_ref[...])
    adt_c = dt_c * a_c_ref[...]
    adt_r = dt_r * a_r_ref[...]

    li = lax.broadcasted_iota(jnp.int32, (L, L), 0)
    si = lax.broadcasted_iota(jnp.int32, (L, L), 1)
    mask = (li <= si) if reverse else (li >= si)
    mask_t = (li >= si) if reverse else (li <= si)
    n_sub = rows // L
    ys = [None] * n_sub
    for s in (reversed(range(n_sub)) if reverse else range(n_sub)):
        r = slice(s * L, (s + 1) * L)
        ys[s] = _ssd_chunk(xbc[r], dt_c[r], adt_c[r], adt_r[:, r], mask, mask_t, st_ref, reverse)
    return jnp.concatenate(ys, axis=0), xbc[:, :SSD_INNER]


def _split3(x):
    hi = x.astype(BF16)
    r1 = x - hi.astype(F32)
    mid = r1.astype(BF16)
    lo = (r1 - mid.astype(F32)).astype(BF16)
    return hi, mid, lo


def _ssd_chunk(xbc, dt_c, adt_c, adt_r, mask, mask_t, st_ref, reverse):
    L = SSD_CHUNK
    hpg = SSD_HEADS // SSD_GROUPS
    off = SSD_HEADS if reverse else 0
    m_b, mt_b = mask.astype(BF16), mask_t.astype(BF16)
    cs_c = sum(_dot(m_b, part) for part in _split3(adt_c))
    cs_r = sum(_dot(part, mt_b) for part in _split3(adt_r))
    edge = 0 if reverse else L - 1
    tot = cs_c[edge:edge + 1, :]

    kk = lax.broadcasted_iota(jnp.int32, (LANES, SSD_INNER), 0)
    jj = lax.broadcasted_iota(jnp.int32, (LANES, SSD_INNER), 1)
    spread = (kk == off + jj // SSD_HEAD_DIM).astype(BF16)
    cols = jnp.concatenate([dt_c, jnp.exp(cs_c), jnp.exp(tot - cs_c)], axis=0)
    c_hi = cols.astype(BF16)
    c_lo = (cols - c_hi.astype(F32)).astype(BF16)
    wide = _dot(c_hi, spread) + _dot(c_lo, spread)
    dt_w, ecs_w, dst_w = wide[0:L], wide[L:2 * L], wide[2 * L:3 * L]

    xs = xbc[:, :SSD_INNER]
    b_all = xbc[:, SSD_INNER:SSD_INNER + LANES]
    c_all = xbc[:, SSD_INNER + LANES:SSD_INNER + 2 * LANES]
    xd = xs * dt_w
    xd_b = xd.astype(BF16)
    b_b = b_all.astype(BF16)
    c_b = c_all.astype(BF16)
    lane = lax.broadcasted_iota(jnp.int32, (L, LANES), 1)
    col = lax.broadcasted_iota(jnp.int32, (L, SSD_INNER), 1)

    y = ecs_w * _dot(c_b, st_ref[...].astype(BF16))
    for g in range(SSD_GROUPS):
        c_g = jnp.where(lane // SSD_STATE == g, c_b, jnp.zeros_like(c_b))
        gmat = _dot_nt(c_g, b_b)
        for hh in range(hpg):
            h = g * hpg + hh
            dec = jnp.exp(jnp.where(mask, cs_c[:, off + h:off + h + 1] - cs_r[off + h:off + h + 1, :], -jnp.inf))
            xd_h = jnp.where(col // SSD_HEAD_DIM == h, xd_b, jnp.zeros_like(xd_b))
            y += _dot((gmat * dec).astype(BF16), xd_h)

    kr = lax.broadcasted_iota(jnp.int32, (LANES, SSD_INNER), 0)
    kc = lax.broadcasted_iota(jnp.int32, (LANES, SSD_INNER), 1)
    block = kr // SSD_STATE == kc // (hpg * SSD_HEAD_DIM)
    upd = _dot(b_all.T.astype(BF16), (xd * dst_w).astype(BF16))
    st_ref[...] = st_ref[...] * ecs_w[edge:edge + 1, :] + jnp.where(block, upd, 0.0)
    return y


def _ssd_body(n_lat, n_chunks, *refs):
    fwd_in, bwd_in = refs[0:5], refs[5:10]
    cw_ref, cb_ref, bias_c_ref, a_c_ref, bias_r_ref, a_r_ref, dsk_ref, yf_ref, yb_ref, st_ref = refs[10:]
    i = pl.program_id(0)

    @pl.when(i == 0)
    def _():
        st_ref[...] = jnp.zeros_like(st_ref)

    shared = (cw_ref, cb_ref, bias_c_ref, a_c_ref, bias_r_ref, a_r_ref)
    y, xs = _ssd_block(_segment_order(i, n_lat, n_chunks, False), n_lat, n_chunks, False, *fwd_in, *shared,
                       st_ref.at[0])
    yf_ref[...] = y + dsk_ref[...] * xs
    y, _ = _ssd_block(_segment_order(i, n_lat, n_chunks, True), n_lat, n_chunks, True, *bwd_in, *shared,
                      st_ref.at[1])
    yb_ref[...] = y


def _ssd_call(xbc_raw, dt_c, dt_r, cw, cb, bias_c, a_c, bias_r, a_r, dsk, n_lat_rows):
    t = xbc_raw.shape[0]
    L = SSD_STEP_CHUNKS * SSD_CHUNK
    assert t % L == 0 and n_lat_rows % L == 0
    n_chunks = t // L
    n_lat = n_lat_rows // L
    sub = L // 8
    n_sub = t // 8
    const = lambda i: (0, 0)

    def chunk_specs(reverse):
        cidx = functools.partial(_segment_order, n_lat=n_lat, n_all=n_chunks, reverse=reverse)
        return [pl.BlockSpec((L, SSD_CONV_DIM), lambda i: (cidx(i), 0)),
                pl.BlockSpec((8, SSD_CONV_DIM), lambda i: (jnp.maximum(cidx(i) * sub - 1, 0), 0)),
                pl.BlockSpec((8, SSD_CONV_DIM), lambda i: (jnp.minimum((cidx(i) + 1) * sub, n_sub - 1), 0)),
                pl.BlockSpec((L, LANES), lambda i: (cidx(i), 0)),
                pl.BlockSpec((8, L), lambda i: (0, cidx(i)))]

    def out_spec(reverse):
        cidx = functools.partial(_segment_order, n_lat=n_lat, n_all=n_chunks, reverse=reverse)
        return pl.BlockSpec((L, SSD_INNER), lambda i: (cidx(i), 0))

    chunk_args = (xbc_raw, xbc_raw, xbc_raw, dt_c, dt_r)
    return pl.pallas_call(
        functools.partial(_ssd_body, n_lat, n_chunks),
        grid=(n_chunks,),
        in_specs=chunk_specs(False) + chunk_specs(True)
                 + [pl.BlockSpec((8, SSD_CONV_DIM), const),
                    pl.BlockSpec((1, SSD_CONV_DIM), const),
                    pl.BlockSpec((1, LANES), const),
                    pl.BlockSpec((1, LANES), const),
                    pl.BlockSpec((8, L), const),
                    pl.BlockSpec((8, L), const),
                    pl.BlockSpec((1, SSD_INNER), const)],
        out_specs=[out_spec(False), out_spec(True)],
        out_shape=[jax.ShapeDtypeStruct((t, SSD_INNER), F32)] * 2,
        scratch_shapes=[pltpu.VMEM((2, SSD_GROUPS * SSD_STATE, SSD_INNER), F32)],
        compiler_params=_cparams("arbitrary"),
        name="ssd_scan",
    )(*chunk_args, *chunk_args, cw, cb, bias_c, a_c, bias_r, a_r, dsk)


def _cmul(ar, ai, br, bi):
    return ar * br - ai * bi, ar * bi + ai * br


def _s5_in_body(u_ref, w_ref, lam_ref, efr_ref, efi_ref, ebr_ref, ebi_ref, xs_ref, bu_ref):
    tn = u_ref.shape[0]
    hs = S5_HALF_STATE
    for i in range(S5_CHUNK):
        xs_ref[i * tn:(i + 1) * tn, :] = u_ref[:, i, :]
    bu_ref[...] = _dot(xs_ref[...].astype(BF16), w_ref[0])
    lam = lam_ref[0]
    fr, fi, br, bi = lam[0:1], lam[1:2], lam[2:3], lam[3:4]
    ar = jnp.zeros((tn, hs), F32)
    ai = jnp.zeros((tn, hs), F32)
    for i in range(S5_CHUNK):
        ar, ai = _cmul(ar, ai, fr, fi)
        ar = ar + bu_ref[i * tn:(i + 1) * tn, 0:hs]
        ai = ai + bu_ref[i * tn:(i + 1) * tn, hs:2 * hs]
    efr_ref[...] = ar
    efi_ref[...] = ai
    ar = jnp.zeros((tn, hs), F32)
    ai = jnp.zeros((tn, hs), F32)
    for i in reversed(range(S5_CHUNK)):
        ar, ai = _cmul(ar, ai, br, bi)
        ar = ar + bu_ref[i * tn:(i + 1) * tn, 2 * hs:3 * hs]
        ai = ai + bu_ref[i * tn:(i + 1) * tn, 3 * hs:4 * hs]
    ebr_ref[...] = ar
    ebi_ref[...] = ai


def _s5_in_call(u3, w_b, lam_rows):
    nch = u3.shape[0]
    tn = _pick_tile(nch, (104, 80, 40))
    hs = S5_HALF_STATE
    col = lambda h, n: (n, h)
    return pl.pallas_call(
        _s5_in_body,
        grid=(2, nch // tn),
        in_specs=[pl.BlockSpec((tn, S5_CHUNK, LANES), lambda h, n: (n, 0, h)),
                  pl.BlockSpec((1, LANES, 4 * hs), lambda h, n: (h, 0, 0)),
                  pl.BlockSpec((1, 4, hs), lambda h, n: (h, 0, 0))],
        out_specs=[pl.BlockSpec((tn, hs), col)] * 4,
        out_shape=[jax.ShapeDtypeStruct((nch, 2 * hs), F32)] * 4,
        scratch_shapes=[pltpu.VMEM((tn * S5_CHUNK, LANES), F32), pltpu.VMEM((tn * S5_CHUNK, 4 * hs), F32)],
        compiler_params=_cparams("arbitrary", "arbitrary"),
        name="s5_chunk_in",
    )(u3, w_b, lam_rows)


def _s5_scan_body(a_ref, efr_ref, efi_ref, ebr_ref, ebi_ref, hfr_ref, hfi_ref, hbr_ref, hbi_ref, st_ref):
    @pl.when(pl.program_id(0) == 0)
    def _():
        st_ref[...] = jnp.zeros_like(st_ref)

    afr, afi, abr, abi = a_ref[0], a_ref[1], a_ref[2], a_ref[3]
    fr, fi, br, bi = st_ref[0], st_ref[1], st_ref[2], st_ref[3]
    for j in range(S5_SCAN_TILE):
        hfr_ref[j] = fr
        hfi_ref[j] = fi
        fr, fi = afr * fr - afi * fi + efr_ref[j], afr * fi + afi * fr + efi_ref[j]
        jb = S5_SCAN_TILE - 1 - j
        hbr_ref[jb] = br
        hbi_ref[jb] = bi
        br, bi = abr * br - abi * bi + ebr_ref[jb], abr * bi + abi * br + ebi_ref[jb]
    st_ref[0] = fr
    st_ref[1] = fi
    st_ref[2] = br
    st_ref[3] = bi


def _s5_scan_call(a_pow, efr, efi, ebr, ebi, n_lat_tiles):
    nch, sub, _ = efr.shape
    n_tiles = nch // S5_SCAN_TILE
    blk = (S5_SCAN_TILE, sub, LANES)
    fwd = lambda i: (_segment_order(i, n_lat_tiles, n_tiles, False), 0, 0)
    bwd = lambda i: (_segment_order(i, n_lat_tiles, n_tiles, True), 0, 0)
    shp = jax.ShapeDtypeStruct((nch, sub, LANES), F32)
    return pl.pallas_call(
        _s5_scan_body,
        grid=(n_tiles,),
        in_specs=[pl.BlockSpec((4, sub, LANES), lambda i: (0, 0, 0)),
                  pl.BlockSpec(blk, fwd), pl.BlockSpec(blk, fwd),
                  pl.BlockSpec(blk, bwd), pl.BlockSpec(blk, bwd)],
        out_specs=[pl.BlockSpec(blk, fwd), pl.BlockSpec(blk, fwd),
                   pl.BlockSpec(blk, bwd), pl.BlockSpec(blk, bwd)],
        out_shape=[shp] * 4,
        scratch_shapes=[pltpu.VMEM((4, sub, LANES), F32)],
        compiler_params=_cparams("arbitrary"),
        name="s5_state_scan",
    )(a_pow, efr, efi, ebr, ebi)


def _s5_out_body(u_ref, hfr_ref, hfi_ref, hbr_ref, hbi_ref, z_ref, c_ref, lam_ref, y_ref, gb_ref):
    hs = S5_HALF_STATE
    width = S5_CHUNK * LANES
    lam = lam_ref[0]
    fr, fi, br, bi = lam[0:1], lam[1:2], lam[2:3], lam[3:4]
    ytz = None
    for i in range(S5_CHUNK):
        xi = u_ref[:, i, :].astype(BF16)
        lo = (S5_CHUNK - 1 - i) * LANES
        part = _dot(xi, z_ref[0, :, lo:lo + width])
        ytz = part if ytz is None else ytz + part
    gr, gi = hbr_ref[...], hbi_ref[...]
    for j in reversed(range(S5_CHUNK)):
        gr, gi = _cmul(gr, gi, br, bi)
        gb_ref[j, :, 0:hs] = gr
        gb_ref[j, :, hs:] = gi
    cw = c_ref[0]
    gr, gi = hfr_ref[...], hfi_ref[...]
    for j in range(S5_CHUNK):
        gr, gi = _cmul(gr, gi, fr, fi)
        g_all = jnp.concatenate([gr, gi, gb_ref[j]], axis=1).astype(BF16)
        y_ref[:, j, :] = ytz[:, j * LANES:(j + 1) * LANES] + _dot(g_all, cw)


def _s5_out_call(u3, hfr, hfi, hbr, hbi, w_z, w_c, lam_rows):
    nch = u3.shape[0]
    tn = _pick_tile(nch, (208, 80, 40))
    hs = S5_HALF_STATE
    col = lambda h, n: (n, h)
    blk3 = pl.BlockSpec((tn, S5_CHUNK, LANES), lambda h, n: (n, 0, h))
    return pl.pallas_call(
        _s5_out_body,
        grid=(2, nch // tn),
        in_specs=[blk3] + [pl.BlockSpec((tn, hs), col)] * 4
                 + [pl.BlockSpec((1, LANES, S5_TOEPLITZ_BLOCKS * LANES), lambda h, n: (h, 0, 0)),
                    pl.BlockSpec((1, 4 * hs, LANES), lambda h, n: (h, 0, 0)),
                    pl.BlockSpec((1, 4, hs), lambda h, n: (h, 0, 0))],
        out_specs=blk3,
        out_shape=jax.ShapeDtypeStruct(u3.shape, F32),
        scratch_shapes=[pltpu.VMEM((S5_CHUNK, tn, 2 * hs), F32)],
        compiler_params=_cparams("arbitrary", "arbitrary"),
        name="s5_chunk_out",
    )(u3, hfr, hfi, hbr, hbi, w_z, w_c, lam_rows)


def _s5_weights(lam_re, lam_im, log_step, b_re, b_im, c_re, c_im):
    tc = S5_CHUNK
    nl = lam_re.shape[0]
    hg = S5_HALF_GROUPS
    ein = functools.partial(jnp.einsum, precision=HI)
    step = jnp.exp(log_step)[..., None]
    er, ei = lam_re * step, lam_im * step
    mag = jnp.exp(er)
    lbr, lbi = mag * jnp.cos(ei), mag * jnp.sin(ei)
    den = lam_re * lam_re + lam_im * lam_im
    nr, ni = lbr - 1.0, lbi
    cr, ci = (nr * lam_re + ni * lam_im) / den, (ni * lam_re - nr * lam_im) / den
    bbr = cr[..., None] * b_re - ci[..., None] * b_im
    bbi = cr[..., None] * b_im + ci[..., None] * b_re
    taus = jnp.arange(tc + 1, dtype=F32)[:, None, None]
    pmag = jnp.exp(er[:, :, None] * taus)
    pr, pi = pmag * jnp.cos(ei[:, :, None] * taus), pmag * jnp.sin(ei[:, :, None] * taus)

    wr = pr[:, :, :tc, :, :, None] * bbr[:, :, None] - pi[:, :, :tc, :, :, None] * bbi[:, :, None]
    wi = pr[:, :, :tc, :, :, None] * bbi[:, :, None] + pi[:, :, :tc, :, :, None] * bbr[:, :, None]
    kk = ein('ldgcp,ldtgpk->ldtgck', c_re, wr) - ein('ldgcp,ldtgpk->ldtgck', c_im, wi)
    eye = jnp.eye(hg, dtype=F32)
    kk = kk.reshape(nl, 2, tc, 2, hg, S5_GROUP, S5_GROUP)
    bd = jnp.einsum('ldthgck,gj->ldthgkjc', kk, eye).reshape(nl, 2, tc, 2, LANES, LANES)
    bdf, bdb = bd[:, 0], bd[:, 1]
    strip = jnp.concatenate([bdb[:, :0:-1], (bdf[:, :1] + bdb[:, :1]), bdf[:, 1:]], axis=1)
    w_z = strip.transpose(0, 2, 3, 1, 4).reshape(nl, 2, LANES, S5_TOEPLITZ_BLOCKS * LANES)

    planes_b = jnp.stack([bbr[:, 0], bbi[:, 0], bbr[:, 1], bbi[:, 1]], axis=1)
    planes_b = planes_b.reshape(nl, 4, 2, hg, S5_STATE, S5_GROUP)
    w_b = jnp.einsum('lqhgpc,gj->lhgcqjp', planes_b, eye).reshape(nl, 2, LANES, 4 * S5_HALF_STATE)
    planes_c = jnp.stack([c_re[:, 0], -c_im[:, 0], c_re[:, 1], -c_im[:, 1]], axis=1)
    planes_c = planes_c.reshape(nl, 4, 2, hg, S5_GROUP, S5_STATE)
    w_c = jnp.einsum('lqhgcp,gj->lhqjpgc', planes_c, eye).reshape(nl, 2, 4 * S5_HALF_STATE, LANES)

    halves = lambda a: a.reshape(nl, 2, S5_HALF_STATE)
    lam_rows = jnp.stack([halves(lbr[:, 0]), halves(lbi[:, 0]), halves(lbr[:, 1]), halves(lbi[:, 1])], axis=2)
    full = lambda a: a.reshape(nl, S5_GROUPS * S5_STATE // LANES, LANES)
    a_pow = jnp.stack([full(pr[:, 0, tc]), full(pi[:, 0, tc]), full(pr[:, 1, tc]), full(pi[:, 1, tc])], axis=1)
    return w_b.astype(BF16), w_z.astype(BF16), w_c.astype(BF16), lam_rows, a_pow


def _s5_mix(u, w_b, w_z, w_c, lam_rows, a_pow, n_lat):
    t = u.shape[0]
    nch = t // S5_CHUNK
    sub = S5_GROUPS * S5_STATE // LANES
    u3 = u.reshape(nch, S5_CHUNK, S5_WIDTH)
    es = _s5_in_call(u3, w_b, lam_rows)
    es = [e.reshape(nch, sub, LANES) for e in es]
    hs = _s5_scan_call(a_pow, *es, n_lat // (S5_CHUNK * S5_SCAN_TILE))
    hs = [h.reshape(nch, sub * LANES) for h in hs]
    return _s5_out_call(u3, *hs, w_z, w_c, lam_rows).reshape(t, S5_WIDTH)


def _attn_body(flag_ref, lam_ref, q_ref, k_ref, v_ref, g_ref, o_ref, acc_ref, m_ref, oh_ref, *, n_kv, post_scale):
    bounded = flag_ref[0] == 1

    def head(hh, carry):
        q = q_ref[hh]
        acc_ref[...] = jnp.zeros_like(acc_ref)

        @pl.when(bounded)
        def _():
            def step(b, carry2):
                v = v_ref[hh, b]
                for c in range(2):
                    p = jnp.exp2(_dot(q, k_ref[hh, b, c])).astype(BF16)
                    acc_ref[c] += _dot(p, v)
                return carry2
            lax.fori_loop(0, n_kv, step, 0)

        @pl.when(jnp.logical_not(bounded))
        def _():
            m_ref[...] = jnp.full_like(m_ref, -jnp.inf)

            def step(b, carry2):
                v = v_ref[hh, b]
                for c in range(2):
                    s = _dot(q, k_ref[hh, b, c])
                    m_old = m_ref[c]
                    m_new = jnp.maximum(m_old, jnp.max(s, axis=1, keepdims=True))
                    p = jnp.exp2(s - m_new[:, 0:1]).astype(BF16)
                    acc_ref[c] = jnp.exp2(m_old - m_new) * acc_ref[c] + _dot(p, v)
                    m_ref[c] = m_new
                return carry2
            lax.fori_loop(0, n_kv, step, 0)

        a0 = acc_ref[0]
        a1 = acc_ref[1]
        o = a0[:, :DA_V_DIM] / a0[:, DA_V_DIM:DA_V_DIM + 1] \
            - lam_ref[0] * (a1[:, :DA_V_DIM] / a1[:, DA_V_DIM:DA_V_DIM + 1])
        ms = jnp.mean(o * o, axis=-1, keepdims=True)
        oh_ref[hh] = o * lax.rsqrt(ms + EPS) * g_ref[...] * post_scale
        return carry

    lax.fori_loop(0, ATTN_HEADS_PER_STEP, head, 0)
    o_ref[...] = jnp.concatenate([oh_ref[hh] for hh in range(ATTN_HEADS_PER_STEP)], axis=1)


def _attn_body_keep_rest(flag_ref, lam_ref, q_ref, k_ref, v_ref, g_ref, rest_ref, *refs, **kw):
    del rest_ref
    _attn_body(flag_ref, lam_ref, q_ref, k_ref, v_ref, g_ref, *refs, **kw)


def _attn_call(flag, lam, qh, kzb, vb, g_sub, post_scale, q_rows, q_row0, kv_block, kv_cols, kv_col0, y_rest=None):
    nh = qh.shape[0]
    hp = ATTN_HEADS_PER_STEP
    tk = kzb.shape[4]
    tq = _pick_tile(q_rows, (ATTN_TQ, 256))
    assert q_row0 % tq == 0
    q0 = q_row0 // tq
    if kv_block is None:
        n_kv = kzb.shape[1]
        one = pl.Buffered(1)
        k_spec = pl.BlockSpec((hp, n_kv, 2, 2 * DA_HEAD_DIM, tk), lambda h, i: (h, 0, 0, 0, 0), pipeline_mode=one)
        v_spec = pl.BlockSpec((hp, n_kv, tk, LANES), lambda h, i: (h, 0, 0, 0), pipeline_mode=one)
    else:
        assert kv_col0 % kv_cols == 0 and tk % kv_cols == 0
        n_kv, cb = 1, kv_col0 // kv_cols
        k_spec = pl.BlockSpec((hp, 1, 2, 2 * DA_HEAD_DIM, kv_cols), lambda h, i: (h, kv_block, 0, 0, cb))
        v_spec = pl.BlockSpec((hp, 1, kv_cols, LANES), lambda h, i: (h, kv_block, cb, 0))
    smem = pl.BlockSpec(memory_space=pltpu.SMEM)
    in_specs = [smem, smem,
                pl.BlockSpec((hp, tq, 2 * DA_HEAD_DIM), lambda h, i: (h, q0 + i, 0)),
                k_spec, v_spec,
                pl.BlockSpec((1, DA_V_DIM), lambda h, i: (0, 0))]
    args = (flag, lam, qh, kzb, vb, g_sub)
    body = functools.partial(_attn_body, n_kv=n_kv, post_scale=post_scale)
    aliases = {}
    if y_rest is not None:
        in_specs.append(pl.BlockSpec(memory_space=pl.ANY))
        args += (y_rest,)
        body = functools.partial(_attn_body_keep_rest, n_kv=n_kv, post_scale=post_scale)
        aliases = {len(args) - 1: 0}
    return pl.pallas_call(
        body,
        grid=(nh // hp, q_rows // tq),
        in_specs=in_specs,
        out_specs=pl.BlockSpec((tq, hp * DA_V_DIM), lambda h, i: (q0 + i, h)),
        out_shape=jax.ShapeDtypeStruct((qh.shape[1], nh * DA_V_DIM), F32),
        scratch_shapes=[pltpu.VMEM((2, tq, LANES), F32), pltpu.VMEM((2, tq, LANES), F32),
                        pltpu.VMEM((hp, tq, DA_V_DIM), F32)],
        input_output_aliases=aliases,
        compiler_params=_cparams("arbitrary", "arbitrary"),
        name="diff_attn",
    )(*args)


def _gelu_tanh(x):
    return 0.5 * x * (1.0 + jnp.tanh(math.sqrt(2.0 / math.pi) * (x + 0.044715 * (x * x * x))))


def _outffn_body(x_ref, yf_ref, yb_ref, z_ref, gssd_ref, ys5_ref, u_ref, d5_ref, gw_ref, gb_ref,
                 yda_ref, mod_ref, modp_ref, g2_ref, wo_ref, w1_ref, w3_ref, w2_ref, o_ref, xm_ref, hb_ref):
    i = pl.program_id(0)
    cur = i % 2
    prev = 1 - cur

    @pl.when(i == 0)
    def _():
        xm_ref[...] = jnp.zeros_like(xm_ref)
        hb_ref[...] = jnp.zeros_like(hb_ref)

    hp = hb_ref[prev]
    f = _silu(_dot(hp, w1_ref[...])) * _dot(hp, w3_ref[...])
    o_ref[...] = xm_ref[prev] + modp_ref[0][5:6] * _dot(f.astype(BF16), w2_ref[...])

    mod = mod_ref[0]
    y = (yf_ref[...] + yb_ref[...]) * _silu(z_ref[...])
    y_ssd = y * lax.rsqrt(jnp.mean(y * y, axis=-1, keepdims=True) + EPS) * gssd_ref[...]
    y5 = _gelu_tanh(ys5_ref[...] + d5_ref[...] * u_ref[...])
    y5 = y5 * jax.nn.sigmoid(_dot(y5.astype(BF16), gw_ref[...].astype(BF16)) + gb_ref[...])
    wo = wo_ref[...]
    o = _dot(y_ssd.astype(BF16), wo[0:256])
    o += _dot(y5.astype(BF16), wo[256:512])
    o += _dot(yda_ref[...].astype(BF16), wo[512:1024])
    x = x_ref[...] + mod[2:3] * o
    h = x * lax.rsqrt(jnp.mean(x * x, axis=-1, keepdims=True) + EPS) * g2_ref[...] * (1.0 + mod[4:5]) + mod[3:4]
    xm_ref[cur] = x
    hb_ref[cur] = h.astype(BF16)


def _outffn_call(xs, y_f, y_b, z, g_ssd, y_s5, u, d5, glu_w, glu_b, y_da, modl, g2, w_o, w1, w3, w2, n_lat):
    t, d = xs.shape
    dff = w1.shape[1]
    tm = FFN_ROW_TILE
    assert n_lat % tm == 0 and t % tm == 0
    n_lat_tiles = n_lat // tm
    n_tiles = t // tm
    row = lambda i: (jnp.minimum(i, n_tiles - 1), 0)
    done = lambda i: (jnp.maximum(i - 1, 0), 0)
    const = lambda i: (0, 0)
    seg = lambda i: ((jnp.minimum(i, n_tiles - 1) >= n_lat_tiles).astype(jnp.int32), 0, 0)
    seg_done = lambda i: ((i - 1 >= n_lat_tiles).astype(jnp.int32), 0, 0)
    one = pl.Buffered(1)
    return pl.pallas_call(
        _outffn_body,
        grid=(n_tiles + 1,),
        in_specs=[pl.BlockSpec((tm, d), row),
                  pl.BlockSpec((tm, 256), row), pl.BlockSpec((tm, 256), row), pl.BlockSpec((tm, 256), row),
                  pl.BlockSpec((1, 256), const),
                  pl.BlockSpec((tm, 256), row), pl.BlockSpec((tm, 256), row),
                  pl.BlockSpec((1, 256), const),
                  pl.BlockSpec((256, 256), const), pl.BlockSpec((1, 256), const),
                  pl.BlockSpec((tm, DA_WIDTH), row),
                  pl.BlockSpec((1, 8, d), seg),
                  pl.BlockSpec((1, 8, d), seg_done),
                  pl.BlockSpec((1, d), const),
                  pl.BlockSpec((d, d), const, pipeline_mode=one),
                  pl.BlockSpec((d, dff), const, pipeline_mode=one),
                  pl.BlockSpec((d, dff), const, pipeline_mode=one),
                  pl.BlockSpec((dff, d), const, pipeline_mode=one)],
        out_specs=pl.BlockSpec((tm, d), done),
        out_shape=jax.ShapeDtypeStruct((t, d), F32),
        scratch_shapes=[pltpu.VMEM((2, tm, d), F32), pltpu.VMEM((2, tm, d), BF16)],
        compiler_params=_cparams("arbitrary"),
        name="out_proj_ffn",
    )(xs, y_f, y_b, z, g_ssd, y_s5, u, d5, glu_w, glu_b, y_da, modl, modl, g2, w_o, w1, w3, w2)


def _rope_tables(n_lat, n_ctx):
    pos = jnp.arange(n_lat)
    lane = jnp.arange(LANES)
    d = lane % DA_HEAD_DIM
    axis = d // (2 * ROPE_FREQS)
    half = (d % (2 * ROPE_FREQS)) // ROPE_FREQS
    freqs = ROPE_THETA ** (-jnp.arange(ROPE_FREQS, dtype=F32) / ROPE_FREQS)
    coord = jnp.where(axis[None, :] == 0, (pos // GRID_W)[:, None], (pos % GRID_W)[:, None]).astype(F32)
    ang = coord * freqs[d % ROPE_FREQS][None, :]
    cos, sin = jnp.cos(ang), jnp.sin(ang)
    sa = jnp.where(half[None, :] == 0, -sin, 0.0)
    sb = jnp.where(half[None, :] == 1, sin, 0.0)
    pad = lambda a, v: jnp.concatenate([a, jnp.full((n_ctx, LANES), v, F32)], axis=0)
    return pad(cos, 1.0), pad(sa, 0.0), pad(sb, 0.0)


def _lane_row(vals, width=LANES):
    vals = vals.reshape(-1).astype(F32)
    return jnp.zeros((1, width), F32).at[0, :vals.shape[0]].set(vals)


def kernel(x, c, ctx, c_ctx, w_mod, b_mod, norm1, norm2, w_in, w_out, ssd_conv_w, ssd_conv_b, ssd_a_log, ssd_dt_bias, ssd_d, ssd_norm, s5_lam_re, s5_lam_im, s5_log_step, s5_b_re, s5_b_im, s5_c_re, s5_c_im, s5_d, s5_glu_w, s5_glu_b, da_q_norm, da_k_norm, da_lambda, da_sub_norm, ffn_w1, ffn_w3, ffn_w2):
    depth = w_mod.shape[0]
    bsz, n_lat, d = x.shape
    n_ctx = ctx.shape[1]
    t = n_ctx + n_lat
    tk = _pick_tile(t, (ATTN_TK, 1024, 640, 512, 256))
    assert bsz == 1 and n_ctx % ROW_TILE == 0 and n_lat % ROW_TILE == 0 and tk % ROW_TILE == 0
    assert n_ctx % (S5_CHUNK * S5_SCAN_TILE) == 0 and n_lat % (S5_CHUNK * S5_SCAN_TILE) == 0
    assert n_lat % n_ctx == 0 and tk % n_ctx == 0
    n_lat_tiles = n_lat // ROW_TILE

    xs = jnp.concatenate([x[0], ctx[0]], axis=0)
    cv = jnp.zeros((8, d), F32).at[0].set(c[0]).at[1].set(c_ctx)
    mod_all = _mod_call(cv, w_mod, b_mod)
    cos_t, sa_t, sb_t = _rope_tables(n_lat, n_ctx)
    gidx = jnp.arange(DA_WIDTH) // DA_HEAD_DIM
    gmat = (gidx[:, None] == gidx[None, :]).astype(BF16) * (1.0 / DA_HEAD_DIM)
    s5_w = _s5_weights(s5_lam_re, s5_lam_im, s5_log_step, s5_b_re, s5_b_im, s5_c_re, s5_c_im)

    for i in range(depth):
        lam_init = 0.8 - 0.6 * math.exp(-0.3 * i)
        modl = jnp.zeros((2, 8, d), F32).at[:, :6].set(mod_all[i, :2].reshape(2, 6, d))

        wi = w_in[i]
        w_r = jnp.concatenate([wi[:, 0:768], wi[:, 776:1032], wi[:, 1032:2568], wi[:, 768:776],
                               jnp.zeros((d, IN_PAD - 2568), F32)], axis=1).astype(BF16)
        gq = jnp.tile(da_q_norm[i], DA_WIDTH // DA_HEAD_DIM)[None, :]
        gk = jnp.tile(da_k_norm[i], DA_WIDTH // DA_HEAD_DIM)[None, :]
        z, xbc_raw, u, dt_raw, qh, kzb, vb = _inproj_call(
            xs, norm1[i][None, :], modl, w_r, gq, gk, gmat, cos_t, sa_t, sb_t, n_lat_tiles, tk)

        dt_r = dt_raw[:, :8].T
        cw = jnp.zeros((8, SSD_CONV_DIM), F32).at[:3].set(ssd_conv_w[i])
        a_neg = -jnp.exp(ssd_a_log[i])
        bias_c = _lane_row(ssd_dt_bias[i])
        a_c = _lane_row(a_neg)
        bias_r = jnp.broadcast_to(ssd_dt_bias[i].reshape(8, 1), (8, SSD_STEP_CHUNKS * SSD_CHUNK))
        a_r = jnp.broadcast_to(a_neg.reshape(8, 1), (8, SSD_STEP_CHUNKS * SSD_CHUNK))
        dsk = jnp.repeat(ssd_d[i], SSD_HEAD_DIM)[None, :]
        y_f, y_b = _ssd_call(xbc_raw, dt_raw, dt_r, cw, ssd_conv_b[i][None, :], bias_c, a_c, bias_r, a_r, dsk,
                             n_lat)

        y_s5 = _s5_mix(u, *(w[i] for w in s5_w), n_lat)

        lf = da_lambda[i]
        lam = (jnp.exp(jnp.sum(lf[0] * lf[1])) - jnp.exp(jnp.sum(lf[2] * lf[3])) + lam_init).reshape(1)
        g_sub = da_sub_norm[i][None, :]
        score_bound = LOG2E * math.sqrt(DA_HEAD_DIM) * jnp.max(jnp.abs(da_q_norm[i])) * jnp.max(jnp.abs(da_k_norm[i]))
        flag = (score_bound <= SCORE_LOG2_LIMIT).astype(jnp.int32).reshape(1)
        y_da = _attn_call(flag, lam, qh, kzb, vb, g_sub, 1.0 - lam_init, n_lat, 0, None, None, None)
        y_da = _attn_call(flag, lam, qh, kzb, vb, g_sub, 1.0 - lam_init, n_ctx, n_lat,
                          n_lat // tk, n_ctx, n_lat % tk, y_rest=y_da)

        xs = _outffn_call(xs, y_f, y_b, z, ssd_norm[i][None, :], y_s5, u, s5_d[i][None, :], s5_glu_w[i],
                          s5_glu_b[i][None, :], y_da, modl, norm2[i][None, :], w_out[i].astype(BF16),
                          ffn_w1[i].astype(BF16), ffn_w3[i].astype(BF16), ffn_w2[i].astype(BF16), n_lat)
    return xs[:n_lat][None]
```

```python
import functools
import math

import jax
import jax.numpy as jnp
from jax import lax
from jax.experimental import pallas as pl
from jax.experimental.pallas import tpu as pltpu

F32 = jnp.float32
BF16 = jnp.bfloat16
HI = lax.Precision.HIGHEST

EPS = 1e-6
GRID_W = 64
ROPE_THETA = 10000.0
ROPE_FREQS = 8

SSD_HEADS = 4
SSD_HEAD_DIM = 64
SSD_INNER = 256
SSD_GROUPS = 2
SSD_STATE = 64
SSD_CHUNK = 128
SSD_STEP_CHUNKS = 2
SSD_CONV_DIM = 512

S5_WIDTH = 256
S5_GROUP = 16
S5_GROUPS = 16
S5_STATE = 64
S5_CHUNK = 16
S5_HALF_GROUPS = S5_GROUPS // 2
S5_HALF_STATE = S5_HALF_GROUPS * S5_STATE
S5_SCAN_TILE = 16
S5_TOEPLITZ_BLOCKS = 2 * S5_CHUNK - 1

DA_HEADS = 8
DA_HEAD_DIM = 32
DA_V_DIM = 64
DA_WIDTH = 512
ATTN_HEADS_PER_STEP = 2
ATTN_TQ = 2048
ATTN_TK = 1280
LOG2E = 1.4426950408889634
SCORE_LOG2_LIMIT = 100.0

LANES = 128
ROW_TILE = 256
FFN_ROW_TILE = 256
IN_PAD = 2688
VMEM_LIMIT = 56 * 1024 * 1024


def _cparams(*sem):
    return pltpu.CompilerParams(dimension_semantics=sem, vmem_limit_bytes=VMEM_LIMIT)


def _dot(a, b, precision=None):
    return jnp.dot(a, b, preferred_element_type=F32, precision=precision)


def _dot_nt(a, b, precision=None):
    return lax.dot_general(a, b, (((1,), (1,)), ((), ())), preferred_element_type=F32,
                           precision=precision)


def _dot_tn(a, b, precision=None):
    return lax.dot_general(a, b, (((0,), (0,)), ((), ())), preferred_element_type=F32,
                           precision=precision)


def _silu(x):
    return x * jax.nn.sigmoid(x)


def _pick_tile(n, candidates):
    for c in candidates:
        if n % c == 0:
            return c
    return n


def _segment_order(i, n_lat, n_all, reverse):
    n_ctx = n_all - n_lat
    if not reverse:
        return jnp.where(i < n_ctx, n_lat + i, i - n_ctx)
    return jnp.where(i < n_ctx, n_all - 1 - i, n_lat - 1 - (i - n_ctx))


def _mod_body(cv_ref, w_ref, b_ref, o_ref):
    o_ref[0] = _dot(_silu(cv_ref[...]), w_ref[0], HI) + b_ref[0]


def _mod_call(cv, w_mod, b_mod):
    depth, d, n = w_mod.shape
    tn = 1024
    return pl.pallas_call(
        _mod_body,
        grid=(depth, n // tn),
        in_specs=[pl.BlockSpec((8, d), lambda l, j: (0, 0)),
                  pl.BlockSpec((1, d, tn), lambda l, j: (l, 0, j)),
                  pl.BlockSpec((1, 1, tn), lambda l, j: (l, 0, j))],
        out_specs=pl.BlockSpec((1, 8, tn), lambda l, j: (l, 0, j)),
        out_shape=jax.ShapeDtypeStruct((depth, 8, n), F32),
        compiler_params=_cparams("arbitrary", "arbitrary"),
        name="adaln_mod",
    )(cv, w_mod, b_mod.reshape(depth, 1, n))


def _inproj_project(x_ref, g_ref, mod_ref, w_ref, z_ref, xbc_ref, u_ref, dt_ref, qkv_ref):
    x = x_ref[...]
    mod = mod_ref[0]
    ms = jnp.mean(x * x, axis=-1, keepdims=True)
    h = x * lax.rsqrt(ms + EPS) * g_ref[...] * (1.0 + mod[1:2]) + mod[0:1]
    p = _dot(h.astype(BF16), w_ref[...])
    z_ref[...] = p[:, 0:256]
    xbc_ref[...] = p[:, 256:768]
    u_ref[...] = p[:, 768:1024]
    dt_ref[...] = p[:, 2560:2688]
    qkv_ref[...] = p[:, 1024:2560]


def _inproj_attn_operands(qkv_ref, gq_ref, gk_ref, gm_ref, cos_ref, sa_ref, sb_ref, q_ref, k_ref, v_ref):
    p = qkv_ref[...]
    rows = p.shape[0]
    gm = gm_ref[...]
    cos = cos_ref[...]
    sa = sa_ref[...]
    sb = sb_ref[...]

    def norm_rope(t, gain, scale):
        ms32 = _dot((t * t).astype(BF16), gm)
        tn = t * lax.rsqrt(ms32 + EPS) * gain
        outs = []
        for j in range(DA_WIDTH // LANES):
            tb = tn[:, j * LANES:(j + 1) * LANES]
            ob = tb * cos + pltpu.roll(tb, LANES - ROPE_FREQS, 1) * sa + pltpu.roll(tb, ROPE_FREQS, 1) * sb
            outs.append(ob * scale)
        return jnp.concatenate(outs, axis=1)

    qn = norm_rope(p[:, 0:512], gq_ref[...], LOG2E * DA_HEAD_DIM ** -0.5).astype(BF16)
    kt = norm_rope(p[:, 512:1024], gk_ref[...], 1.0).T.astype(BF16)
    vv = p[:, 1024:1536].astype(BF16)
    lane = lax.broadcasted_iota(jnp.int32, (rows, LANES - DA_V_DIM), 1)
    ones_col = jnp.where(lane == 0, 1.0, 0.0).astype(BF16)
    zeros_k = jnp.zeros((DA_HEAD_DIM, rows), BF16)
    for hd in range(DA_HEADS):
        lo_, mid, hi_ = hd * DA_V_DIM, hd * DA_V_DIM + DA_HEAD_DIM, (hd + 1) * DA_V_DIM
        q_ref[hd] = qn[:, lo_:hi_]
        k_ref[hd, 0, 0, 0:DA_HEAD_DIM, :] = kt[lo_:mid]
        k_ref[hd, 0, 0, DA_HEAD_DIM:, :] = zeros_k
        k_ref[hd, 0, 1, 0:DA_HEAD_DIM, :] = zeros_k
        k_ref[hd, 0, 1, DA_HEAD_DIM:, :] = kt[mid:hi_]
        v_ref[hd, 0] = jnp.concatenate([vv[:, lo_:hi_], ones_col], axis=1)


def _inproj_body(x_ref, g_ref, mod_ref, w_ref, gq_ref, gk_ref, gm_ref, cos_ref, sa_ref, sb_ref,
                 z_ref, xbc_ref, u_ref, dt_ref, q_ref, k_ref, v_ref, qkv_a_ref, qkv_b_ref):
    i = pl.program_id(0)

    @pl.when(i == 0)
    def _():
        qkv_b_ref[...] = jnp.zeros_like(qkv_b_ref)

    def step(cur_ref, prev_ref):
        _inproj_attn_operands(prev_ref, gq_ref, gk_ref, gm_ref, cos_ref, sa_ref, sb_ref, q_ref, k_ref, v_ref)
        _inproj_project(x_ref, g_ref, mod_ref, w_ref, z_ref, xbc_ref, u_ref, dt_ref, cur_ref)

    @pl.when(i % 2 == 0)
    def _():
        step(qkv_a_ref, qkv_b_ref)

    @pl.when(i % 2 == 1)
    def _():
        step(qkv_b_ref, qkv_a_ref)


def _inproj_call(xs, g1, modl, w_r, gq, gk, gmat, cos_t, sa_t, sb_t, n_lat_tiles, tk):
    t, d = xs.shape
    tm = ROW_TILE
    n_tiles = t // tm
    per = tk // tm
    row = lambda i: (jnp.minimum(i, n_tiles - 1), 0)
    done = lambda i: (jnp.maximum(i - 1, 0), 0)
    const = lambda i: (0, 0)
    seg = lambda i: ((jnp.minimum(i, n_tiles - 1) >= n_lat_tiles).astype(jnp.int32), 0, 0)
    flat = [(256, F32), (512, F32), (256, F32), (LANES, F32)]

    def k_block(i):
        j = jnp.maximum(i - 1, 0)
        return (0, j // per, 0, 0, j % per)

    def v_block(i):
        j = jnp.maximum(i - 1, 0)
        return (0, j // per, j % per, 0)

    return pl.pallas_call(
        _inproj_body,
        grid=(n_tiles + 1,),
        in_specs=[pl.BlockSpec((tm, d), row),
                  pl.BlockSpec((1, d), const),
                  pl.BlockSpec((1, 8, d), seg),
                  pl.BlockSpec((d, IN_PAD), const),
                  pl.BlockSpec((1, DA_WIDTH), const),
                  pl.BlockSpec((1, DA_WIDTH), const),
                  pl.BlockSpec((DA_WIDTH, DA_WIDTH), const),
                  pl.BlockSpec((tm, LANES), done),
                  pl.BlockSpec((tm, LANES), done),
                  pl.BlockSpec((tm, LANES), done)],
        out_specs=[pl.BlockSpec((tm, w), row) for w, _ in flat]
                  + [pl.BlockSpec((DA_HEADS, tm, 2 * DA_HEAD_DIM), lambda i: (0, jnp.maximum(i - 1, 0), 0)),
                     pl.BlockSpec((DA_HEADS, 1, 2, 2 * DA_HEAD_DIM, tm), k_block),
                     pl.BlockSpec((DA_HEADS, 1, tm, LANES), v_block)],
        out_shape=[jax.ShapeDtypeStruct((t, w), dt) for w, dt in flat]
                  + [jax.ShapeDtypeStruct((DA_HEADS, t, 2 * DA_HEAD_DIM), BF16),
                     jax.ShapeDtypeStruct((DA_HEADS, t // tk, 2, 2 * DA_HEAD_DIM, tk), BF16),
                     jax.ShapeDtypeStruct((DA_HEADS, t // tk, tk, LANES), BF16)],
        scratch_shapes=[pltpu.VMEM((tm, 3 * DA_WIDTH), F32)] * 2,
        compiler_params=_cparams("arbitrary"),
        name="in_proj",
    )(xs, g1, modl, w_r, gq, gk, gmat, cos_t, sa_t, sb_t)


def _softplus(x):
    return jnp.maximum(x, 0.0) + jnp.log1p(jnp.exp(-jnp.abs(x)))


def _ssd_block(c, n_lat, n_blocks, reverse, xc_ref, xp_ref, xn_ref, dtc_ref, dtr_ref, cw_ref, cb_ref,
               bias_c_ref, a_c_ref, bias_r_ref, a_r_ref, st_ref):
    L = SSD_CHUNK
    rows = xc_ref.shape[0]
    x = xc_ref[...]
    seg_first = jnp.logical_or(c == 0, c == n_lat)
    seg_last = jnp.logical_or(c == n_lat - 1, c == n_blocks - 1)
    prow = jnp.where(seg_first, 0.0, xp_ref[7:8, :])
    nrow = jnp.where(seg_last, 0.0, xn_ref[0:1, :])
    ridx = lax.broadcasted_iota(jnp.int32, (rows, 1), 0)
    xprev = jnp.where(ridx == 0, prow, pltpu.roll(x, 1, 0))
    xnext = jnp.where(ridx == rows - 1, nrow, pltpu.roll(x, rows - 1, 0))
    cw = cw_ref[...]
    conv = xprev * cw[0:1] + x * cw[1:2] + xnext * cw[2:3] + cb_ref[...]
    xbc = _silu(conv)

    dt_c = _softplus(dtc_ref[...] + bias_c_ref[...])
    dt_r = _softplus(dtr_ref[...] + bias_r_ref[...])
    adt_c = dt_c * a_c_ref[...]
    adt_r = dt_r * a_r_ref[...]

    li = lax.broadcasted_iota(jnp.int32, (L, L), 0)
    si = lax.broadcasted_iota(jnp.int32, (L, L), 1)
    mask = (li <= si) if reverse else (li >= si)
    mask_t = (li >= si) if reverse else (li <= si)
    n_sub = rows // L
    ys = [None] * n_sub
    for s in (reversed(range(n_sub)) if reverse else range(n_sub)):
        r = slice(s * L, (s + 1) * L)
        ys[s] = _ssd_chunk(xbc[r], dt_c[r], adt_c[r], adt_r[:, r], mask, mask_t, st_ref, reverse)
    return jnp.concatenate(ys, axis=0), xbc[:, :SSD_INNER]


def _split3(x):
    hi = x.astype(BF16)
    r1 = x - hi.astype(F32)
    mid = r1.astype(BF16)
    lo = (r1 - mid.astype(F32)).astype(BF16)
    return hi, mid, lo


def _ssd_chunk(xbc, dt_c, adt_c, adt_r, mask, mask_t, st_ref, reverse):
    L = SSD_CHUNK
    hpg = SSD_HEADS // SSD_GROUPS
    off = SSD_HEADS if reverse else 0
    m_b, mt_b = mask.astype(BF16), mask_t.astype(BF16)
    cs_c = sum(_dot(m_b, part) for part in _split3(adt_c))
    cs_r = sum(_dot(part, mt_b) for part in _split3(adt_r))
    edge = 0 if reverse else L - 1
    tot = cs_c[edge:edge + 1, :]

    kk = lax.broadcasted_iota(jnp.int32, (LANES, SSD_INNER), 0)
    jj = lax.broadcasted_iota(jnp.int32, (LANES, SSD_INNER), 1)
    spread = (kk == off + jj // SSD_HEAD_DIM).astype(BF16)
    cols = jnp.concatenate([dt_c, jnp.exp(cs_c), jnp.exp(tot - cs_c)], axis=0)
    c_hi = cols.astype(BF16)
    c_lo = (cols - c_hi.astype(F32)).astype(BF16)
    wide = _dot(c_hi, spread) + _dot(c_lo, spread)
    dt_w, ecs_w, dst_w = wide[0:L], wide[L:2 * L], wide[2 * L:3 * L]

    xs = xbc[:, :SSD_INNER]
    b_all = xbc[:, SSD_INNER:SSD_INNER + LANES]
    c_all = xbc[:, SSD_INNER + LANES:SSD_INNER + 2 * LANES]
    xd = xs * dt_w
    xd_b = xd.astype(BF16)
    b_b = b_all.astype(BF16)
    c_b = c_all.astype(BF16)
    lane = lax.broadcasted_iota(jnp.int32, (L, LANES), 1)
    col = lax.broadcasted_iota(jnp.int32, (L, SSD_INNER), 1)

    y = ecs_w * _dot(c_b, st_ref[...].astype(BF16))
    for g in range(SSD_GROUPS):
        c_g = jnp.where(lane // SSD_STATE == g, c_b, jnp.zeros_like(c_b))
        gmat = _dot_nt(c_g, b_b)
        for hh in range(hpg):
            h = g * hpg + hh
            dec = jnp.exp(jnp.where(mask, cs_c[:, off + h:off + h + 1] - cs_r[off + h:off + h + 1, :], -jnp.inf))
            xd_h = jnp.where(col // SSD_HEAD_DIM == h, xd_b, jnp.zeros_like(xd_b))
            y += _dot((gmat * dec).astype(BF16), xd_h)

    kr = lax.broadcasted_iota(jnp.int32, (LANES, SSD_INNER), 0)
    kc = lax.broadcasted_iota(jnp.int32, (LANES, SSD_INNER), 1)
    block = kr // SSD_STATE == kc // (hpg * SSD_HEAD_DIM)
    upd = _dot(b_all.T.astype(BF16), (xd * dst_w).astype(BF16))
    st_ref[...] = st_ref[...] * ecs_w[edge:edge + 1, :] + jnp.where(block, upd, 0.0)
    return y


def _ssd_body(n_lat, n_chunks, *refs):
    fwd_in, bwd_in = refs[0:5], refs[5:10]
    cw_ref, cb_ref, bias_c_ref, a_c_ref, bias_r_ref, a_r_ref, dsk_ref, yf_ref, yb_ref, st_ref = refs[10:]
    i = pl.program_id(0)

    @pl.when(i == 0)
    def _():
        st_ref[...] = jnp.zeros_like(st_ref)

    shared = (cw_ref, cb_ref, bias_c_ref, a_c_ref, bias_r_ref, a_r_ref)
    y, xs = _ssd_block(_segment_order(i, n_lat, n_chunks, False), n_lat, n_chunks, False, *fwd_in, *shared,
                       st_ref.at[0])
    yf_ref[...] = y + dsk_ref[...] * xs
    y, _ = _ssd_block(_segment_order(i, n_lat, n_chunks, True), n_lat, n_chunks, True, *bwd_in, *shared,
                      st_ref.at[1])
    yb_ref[...] = y


def _ssd_call(xbc_raw, dt_c, dt_r, cw, cb, bias_c, a_c, bias_r, a_r, dsk, n_lat_rows):
    t = xbc_raw.shape[0]
    L = SSD_STEP_CHUNKS * SSD_CHUNK
    assert t % L == 0 and n_lat_rows % L == 0
    n_chunks = t // L
    n_lat = n_lat_rows // L
    sub = L // 8
    n_sub = t // 8
    const = lambda i: (0, 0)

    def chunk_specs(reverse):
        cidx = functools.partial(_segment_order, n_lat=n_lat, n_all=n_chunks, reverse=reverse)
        return [pl.BlockSpec((L, SSD_CONV_DIM), lambda i: (cidx(i), 0)),
                pl.BlockSpec((8, SSD_CONV_DIM), lambda i: (jnp.maximum(cidx(i) * sub - 1, 0), 0)),
                pl.BlockSpec((8, SSD_CONV_DIM), lambda i: (jnp.minimum((cidx(i) + 1) * sub, n_sub - 1), 0)),
                pl.BlockSpec((L, LANES), lambda i: (cidx(i), 0)),
                pl.BlockSpec((8, L), lambda i: (0, cidx(i)))]

    def out_spec(reverse):
        cidx = functools.partial(_segment_order, n_lat=n_lat, n_all=n_chunks, reverse=reverse)
        return pl.BlockSpec((L, SSD_INNER), lambda i: (cidx(i), 0))

    chunk_args = (xbc_raw, xbc_raw, xbc_raw, dt_c, dt_r)
    return pl.pallas_call(
        functools.partial(_ssd_body, n_lat, n_chunks),
        grid=(n_chunks,),
        in_specs=chunk_specs(False) + chunk_specs(True)
                 + [pl.BlockSpec((8, SSD_CONV_DIM), const),
                    pl.BlockSpec((1, SSD_CONV_DIM), const),
                    pl.BlockSpec((1, LANES), const),
                    pl.BlockSpec((1, LANES), const),
                    pl.BlockSpec((8, L), const),
                    pl.BlockSpec((8, L), const),
                    pl.BlockSpec((1, SSD_INNER), const)],
        out_specs=[out_spec(False), out_spec(True)],
        out_shape=[jax.ShapeDtypeStruct((t, SSD_INNER), F32)] * 2,
        scratch_shapes=[pltpu.VMEM((2, SSD_GROUPS * SSD_STATE, SSD_INNER), F32)],
        compiler_params=_cparams("arbitrary"),
        name="ssd_scan",
    )(*chunk_args, *chunk_args, cw, cb, bias_c, a_c, bias_r, a_r, dsk)


def _cmul(ar, ai, br, bi):
    return ar * br - ai * bi, ar * bi + ai * br


def _s5_step_pair(u_ref, a):
    return jnp.concatenate([u_ref[:, 2 * a, :], u_ref[:, 2 * a + 1, :]], axis=1).astype(BF16)


def _s5_in_body(u_ref, w_ref, lam_ref, efr_ref, efi_ref, ebr_ref, ebi_ref, wp_ref):
    hs = S5_HALF_STATE

    @pl.when(pl.program_id(1) == 0)
    def _():
        lam = lam_ref[0]
        w = w_ref[0]
        rows = lambda a: jnp.broadcast_to(a, (LANES, hs))
        fr, fi, br, bi = rows(lam[0:1]), rows(lam[1:2]), rows(lam[2:3]), rows(lam[3:4])
        ar, ai = w[:, 0:hs], w[:, hs:2 * hs]
        for i in reversed(range(S5_CHUNK)):
            wp_ref[i * LANES:(i + 1) * LANES, 0:hs] = ar.astype(BF16)
            wp_ref[i * LANES:(i + 1) * LANES, hs:2 * hs] = ai.astype(BF16)
            ar, ai = _cmul(ar, ai, fr, fi)
        ar, ai = w[:, 2 * hs:3 * hs], w[:, 3 * hs:4 * hs]
        for i in range(S5_CHUNK):
            wp_ref[i * LANES:(i + 1) * LANES, 2 * hs:3 * hs] = ar.astype(BF16)
            wp_ref[i * LANES:(i + 1) * LANES, 3 * hs:4 * hs] = ai.astype(BF16)
            ar, ai = _cmul(ar, ai, br, bi)

    e = None
    for a in range(S5_CHUNK // 2):
        part = _dot(_s5_step_pair(u_ref, a), wp_ref[2 * a * LANES:2 * (a + 1) * LANES, :])
        e = part if e is None else e + part
    efr_ref[...] = e[:, 0:hs]
    efi_ref[...] = e[:, hs:2 * hs]
    ebr_ref[...] = e[:, 2 * hs:3 * hs]
    ebi_ref[...] = e[:, 3 * hs:4 * hs]


def _s5_in_call(u3, w_b, lam_rows):
    nch = u3.shape[0]
    tn = _pick_tile(nch, (208, 80, 40))
    hs = S5_HALF_STATE
    col = lambda h, n: (n, h)
    return pl.pallas_call(
        _s5_in_body,
        grid=(2, nch // tn),
        in_specs=[pl.BlockSpec((tn, S5_CHUNK, LANES), lambda h, n: (n, 0, h)),
                  pl.BlockSpec((1, LANES, 4 * hs), lambda h, n: (h, 0, 0)),
                  pl.BlockSpec((1, 4, hs), lambda h, n: (h, 0, 0))],
        out_specs=[pl.BlockSpec((tn, hs), col)] * 4,
        out_shape=[jax.ShapeDtypeStruct((nch, 2 * hs), F32)] * 4,
        scratch_shapes=[pltpu.VMEM((S5_CHUNK * LANES, 4 * hs), BF16)],
        compiler_params=_cparams("arbitrary", "arbitrary"),
        name="s5_chunk_in",
    )(u3, w_b, lam_rows)


def _s5_scan_body(a_ref, efr_ref, efi_ref, ebr_ref, ebi_ref, hfr_ref, hfi_ref, hbr_ref, hbi_ref, st_ref):
    @pl.when(pl.program_id(0) == 0)
    def _():
        st_ref[...] = jnp.zeros_like(st_ref)

    afr, afi, abr, abi = a_ref[0], a_ref[1], a_ref[2], a_ref[3]
    fr, fi, br, bi = st_ref[0], st_ref[1], st_ref[2], st_ref[3]
    for j in range(S5_SCAN_TILE):
        hfr_ref[j] = fr
        hfi_ref[j] = fi
        fr, fi = afr * fr - afi * fi + efr_ref[j], afr * fi + afi * fr + efi_ref[j]
        jb = S5_SCAN_TILE - 1 - j
        hbr_ref[jb] = br
        hbi_ref[jb] = bi
        br, bi = abr * br - abi * bi + ebr_ref[jb], abr * bi + abi * br + ebi_ref[jb]
    st_ref[0] = fr
    st_ref[1] = fi
    st_ref[2] = br
    st_ref[3] = bi


def _s5_scan_call(a_pow, efr, efi, ebr, ebi, n_lat_tiles):
    nch, sub, _ = efr.shape
    n_tiles = nch // S5_SCAN_TILE
    blk = (S5_SCAN_TILE, sub, LANES)
    fwd = lambda i: (_segment_order(i, n_lat_tiles, n_tiles, False), 0, 0)
    bwd = lambda i: (_segment_order(i, n_lat_tiles, n_tiles, True), 0, 0)
    shp = jax.ShapeDtypeStruct((nch, sub, LANES), F32)
    return pl.pallas_call(
        _s5_scan_body,
        grid=(n_tiles,),
        in_specs=[pl.BlockSpec((4, sub, LANES), lambda i: (0, 0, 0)),
                  pl.BlockSpec(blk, fwd), pl.BlockSpec(blk, fwd),
                  pl.BlockSpec(blk, bwd), pl.BlockSpec(blk, bwd)],
        out_specs=[pl.BlockSpec(blk, fwd), pl.BlockSpec(blk, fwd),
                   pl.BlockSpec(blk, bwd), pl.BlockSpec(blk, bwd)],
        out_shape=[shp] * 4,
        scratch_shapes=[pltpu.VMEM((4, sub, LANES), F32)],
        compiler_params=_cparams("arbitrary"),
        name="s5_state_scan",
    )(a_pow, efr, efi, ebr, ebi)


def _s5_out_body(u_ref, hfr_ref, hfi_ref, hbr_ref, hbi_ref, z_ref, c_ref, lam_ref, y_ref, wq_ref):
    hs = S5_HALF_STATE
    width = S5_CHUNK * LANES

    @pl.when(pl.program_id(1) == 0)
    def _():
        c = c_ref[0]
        lam = lam_ref[0]
        for d, order in ((0, range(S5_CHUNK)), (1, reversed(range(S5_CHUNK)))):
            lr, li = lam[2 * d], lam[2 * d + 1]
            a, b = c[2 * d * hs:(2 * d + 1) * hs], c[(2 * d + 1) * hs:(2 * d + 2) * hs]
            for j in order:
                a, b = a * lr + b * li, b * lr - a * li
                wq_ref[2 * d * hs:(2 * d + 1) * hs, j * LANES:(j + 1) * LANES] = a.astype(BF16)
                wq_ref[(2 * d + 1) * hs:(2 * d + 2) * hs, j * LANES:(j + 1) * LANES] = b.astype(BF16)

    h_all = jnp.concatenate([hfr_ref[...], hfi_ref[...], hbr_ref[...], hbi_ref[...]], axis=1).astype(BF16)
    y = _dot(h_all, wq_ref[...])
    for a in range(S5_CHUNK // 2):
        lo = (S5_CHUNK - 2 - 2 * a) * LANES
        y += _dot(_s5_step_pair(u_ref, a), z_ref[0, :, lo:lo + width])
    for j in range(S5_CHUNK):
        y_ref[:, j, :] = y[:, j * LANES:(j + 1) * LANES]


def _s5_out_call(u3, hfr, hfi, hbr, hbi, w_z2, w_c, lam_cols):
    nch = u3.shape[0]
    tn = _pick_tile(nch, (208, 80, 40))
    hs = S5_HALF_STATE
    col = lambda h, n: (n, h)
    blk3 = pl.BlockSpec((tn, S5_CHUNK, LANES), lambda h, n: (n, 0, h))
    return pl.pallas_call(
        _s5_out_body,
        grid=(2, nch // tn),
        in_specs=[blk3] + [pl.BlockSpec((tn, hs), col)] * 4
                 + [pl.BlockSpec((1, 2 * LANES, (S5_TOEPLITZ_BLOCKS - 1) * LANES), lambda h, n: (h, 0, 0)),
                    pl.BlockSpec((1, 4 * hs, LANES), lambda h, n: (h, 0, 0)),
                    pl.BlockSpec((1, 4, hs, LANES), lambda h, n: (h, 0, 0, 0))],
        out_specs=blk3,
        out_shape=jax.ShapeDtypeStruct(u3.shape, F32),
        scratch_shapes=[pltpu.VMEM((4 * hs, S5_CHUNK * LANES), BF16)],
        compiler_params=_cparams("arbitrary", "arbitrary"),
        name="s5_chunk_out",
    )(u3, hfr, hfi, hbr, hbi, w_z2, w_c, lam_cols)


def _s5_weights(lam_re, lam_im, log_step, b_re, b_im, c_re, c_im):
    tc = S5_CHUNK
    nl = lam_re.shape[0]
    hg = S5_HALF_GROUPS
    ein = functools.partial(jnp.einsum, precision=HI)
    step = jnp.exp(log_step)[..., None]
    er, ei = lam_re * step, lam_im * step
    mag = jnp.exp(er)
    lbr, lbi = mag * jnp.cos(ei), mag * jnp.sin(ei)
    den = lam_re * lam_re + lam_im * lam_im
    nr, ni = lbr - 1.0, lbi
    cr, ci = (nr * lam_re + ni * lam_im) / den, (ni * lam_re - nr * lam_im) / den
    bbr = cr[..., None] * b_re - ci[..., None] * b_im
    bbi = cr[..., None] * b_im + ci[..., None] * b_re
    taus = jnp.arange(tc + 1, dtype=F32)[:, None, None]
    pmag = jnp.exp(er[:, :, None] * taus)
    pr, pi = pmag * jnp.cos(ei[:, :, None] * taus), pmag * jnp.sin(ei[:, :, None] * taus)

    wr = pr[:, :, :tc, :, :, None] * bbr[:, :, None] - pi[:, :, :tc, :, :, None] * bbi[:, :, None]
    wi = pr[:, :, :tc, :, :, None] * bbi[:, :, None] + pi[:, :, :tc, :, :, None] * bbr[:, :, None]
    kk = ein('ldgcp,ldtgpk->ldtgck', c_re, wr) - ein('ldgcp,ldtgpk->ldtgck', c_im, wi)
    eye = jnp.eye(hg, dtype=F32)
    kk = kk.reshape(nl, 2, tc, 2, hg, S5_GROUP, S5_GROUP)
    bd = jnp.einsum('ldthgck,gj->ldthgkjc', kk, eye).reshape(nl, 2, tc, 2, LANES, LANES)
    bdf, bdb = bd[:, 0], bd[:, 1]
    strip = jnp.concatenate([bdb[:, :0:-1], (bdf[:, :1] + bdb[:, :1]), bdf[:, 1:]], axis=1)
    w_z = strip.transpose(0, 2, 3, 1, 4).reshape(nl, 2, LANES, S5_TOEPLITZ_BLOCKS * LANES)
    w_z2 = jnp.concatenate([w_z[..., LANES:], w_z[..., :-LANES]], axis=2)

    planes_b = jnp.stack([bbr[:, 0], bbi[:, 0], bbr[:, 1], bbi[:, 1]], axis=1)
    planes_b = planes_b.reshape(nl, 4, 2, hg, S5_STATE, S5_GROUP)
    w_b = jnp.einsum('lqhgpc,gj->lhgcqjp', planes_b, eye).reshape(nl, 2, LANES, 4 * S5_HALF_STATE)
    planes_c = jnp.stack([c_re[:, 0], -c_im[:, 0], c_re[:, 1], -c_im[:, 1]], axis=1)
    planes_c = planes_c.reshape(nl, 4, 2, hg, S5_GROUP, S5_STATE)
    w_c = jnp.einsum('lqhgcp,gj->lhqjpgc', planes_c, eye).reshape(nl, 2, 4 * S5_HALF_STATE, LANES)

    halves = lambda a: a.reshape(nl, 2, S5_HALF_STATE)
    lam_rows = jnp.stack([halves(lbr[:, 0]), halves(lbi[:, 0]), halves(lbr[:, 1]), halves(lbi[:, 1])], axis=2)
    lam_cols = jnp.broadcast_to(lam_rows[..., None], lam_rows.shape + (LANES,))
    full = lambda a: a.reshape(nl, S5_GROUPS * S5_STATE // LANES, LANES)
    a_pow = jnp.stack([full(pr[:, 0, tc]), full(pi[:, 0, tc]), full(pr[:, 1, tc]), full(pi[:, 1, tc])], axis=1)
    return w_b, w_z2.astype(BF16), w_c, lam_rows, lam_cols, a_pow


def _s5_mix(u, w_b, w_z2, w_c, lam_rows, lam_cols, a_pow, n_lat):
    t = u.shape[0]
    nch = t // S5_CHUNK
    sub = S5_GROUPS * S5_STATE // LANES
    u3 = u.reshape(nch, S5_CHUNK, S5_WIDTH)
    es = _s5_in_call(u3, w_b, lam_rows)
    es = [e.reshape(nch, sub, LANES) for e in es]
    hs = _s5_scan_call(a_pow, *es, n_lat // (S5_CHUNK * S5_SCAN_TILE))
    hs = [h.reshape(nch, sub * LANES) for h in hs]
    return _s5_out_call(u3, *hs, w_z2, w_c, lam_cols).reshape(t, S5_WIDTH)


def _attn_body(flag_ref, lam_ref, q_ref, k_ref, v_ref, g_ref, o_ref, acc_ref, m_ref, oh_ref, *, n_kv, post_scale):
    bounded = flag_ref[0] == 1

    def head(hh, carry):
        q = q_ref[hh]
        acc_ref[...] = jnp.zeros_like(acc_ref)

        @pl.when(bounded)
        def _():
            def step(b, carry2):
                v = v_ref[hh, b]
                for c in range(2):
                    p = jnp.exp2(_dot(q, k_ref[hh, b, c])).astype(BF16)
                    acc_ref[c] += _dot(p, v)
                return carry2
            lax.fori_loop(0, n_kv, step, 0)

        @pl.when(jnp.logical_not(bounded))
        def _():
            m_ref[...] = jnp.full_like(m_ref, -jnp.inf)

            def step(b, carry2):
                v = v_ref[hh, b]
                for c in range(2):
                    s = _dot(q, k_ref[hh, b, c])
                    m_old = m_ref[c]
                    m_new = jnp.maximum(m_old, jnp.max(s, axis=1, keepdims=True))
                    p = jnp.exp2(s - m_new[:, 0:1]).astype(BF16)
                    acc_ref[c] = jnp.exp2(m_old - m_new) * acc_ref[c] + _dot(p, v)
                    m_ref[c] = m_new
                return carry2
            lax.fori_loop(0, n_kv, step, 0)

        a0 = acc_ref[0]
        a1 = acc_ref[1]
        o = a0[:, :DA_V_DIM] / a0[:, DA_V_DIM:DA_V_DIM + 1] \
            - lam_ref[0] * (a1[:, :DA_V_DIM] / a1[:, DA_V_DIM:DA_V_DIM + 1])
        ms = jnp.mean(o * o, axis=-1, keepdims=True)
        oh_ref[hh] = o * lax.rsqrt(ms + EPS) * g_ref[...] * post_scale
        return carry

    lax.fori_loop(0, ATTN_HEADS_PER_STEP, head, 0)
    o_ref[...] = jnp.concatenate([oh_ref[hh] for hh in range(ATTN_HEADS_PER_STEP)], axis=1)


def _attn_body_keep_rest(flag_ref, lam_ref, q_ref, k_ref, v_ref, g_ref, rest_ref, *refs, **kw):
    del rest_ref
    _attn_body(flag_ref, lam_ref, q_ref, k_ref, v_ref, g_ref, *refs, **kw)


def _attn_call(flag, lam, qh, kzb, vb, g_sub, post_scale, q_rows, q_row0, kv_block, kv_cols, kv_col0, y_rest=None):
    nh = qh.shape[0]
    hp = ATTN_HEADS_PER_STEP
    tk = kzb.shape[4]
    tq = _pick_tile(q_rows, (ATTN_TQ, 256))
    assert q_row0 % tq == 0
    q0 = q_row0 // tq
    if kv_block is None:
        n_kv = kzb.shape[1]
        one = pl.Buffered(1)
        k_spec = pl.BlockSpec((hp, n_kv, 2, 2 * DA_HEAD_DIM, tk), lambda h, i: (h, 0, 0, 0, 0), pipeline_mode=one)
        v_spec = pl.BlockSpec((hp, n_kv, tk, LANES), lambda h, i: (h, 0, 0, 0), pipeline_mode=one)
    else:
        assert kv_col0 % kv_cols == 0 and tk % kv_cols == 0
        n_kv, cb = 1, kv_col0 // kv_cols
        k_spec = pl.BlockSpec((hp, 1, 2, 2 * DA_HEAD_DIM, kv_cols), lambda h, i: (h, kv_block, 0, 0, cb))
        v_spec = pl.BlockSpec((hp, 1, kv_cols, LANES), lambda h, i: (h, kv_block, cb, 0))
    smem = pl.BlockSpec(memory_space=pltpu.SMEM)
    in_specs = [smem, smem,
                pl.BlockSpec((hp, tq, 2 * DA_HEAD_DIM), lambda h, i: (h, q0 + i, 0)),
                k_spec, v_spec,
                pl.BlockSpec((1, DA_V_DIM), lambda h, i: (0, 0))]
    args = (flag, lam, qh, kzb, vb, g_sub)
    body = functools.partial(_attn_body, n_kv=n_kv, post_scale=post_scale)
    aliases = {}
    if y_rest is not None:
        in_specs.append(pl.BlockSpec(memory_space=pl.ANY))
        args += (y_rest,)
        body = functools.partial(_attn_body_keep_rest, n_kv=n_kv, post_scale=post_scale)
        aliases = {len(args) - 1: 0}
    return pl.pallas_call(
        body,
        grid=(nh // hp, q_rows // tq),
        in_specs=in_specs,
        out_specs=pl.BlockSpec((tq, hp * DA_V_DIM), lambda h, i: (q0 + i, h)),
        out_shape=jax.ShapeDtypeStruct((qh.shape[1], nh * DA_V_DIM), F32),
        scratch_shapes=[pltpu.VMEM((2, tq, LANES), F32), pltpu.VMEM((2, tq, LANES), F32),
                        pltpu.VMEM((hp, tq, DA_V_DIM), F32)],
        input_output_aliases=aliases,
        compiler_params=_cparams("arbitrary", "arbitrary"),
        name="diff_attn",
    )(*args)


def _gelu_tanh(x):
    return 0.5 * x * (1.0 + jnp.tanh(math.sqrt(2.0 / math.pi) * (x + 0.044715 * (x * x * x))))


def _outffn_body(x_ref, yf_ref, yb_ref, z_ref, gssd_ref, ys5_ref, u_ref, d5_ref, gw_ref, gb_ref,
                 yda_ref, mod_ref, modp_ref, g2_ref, wo_ref, w1_ref, w3_ref, w2_ref, o_ref, xm_ref, hb_ref):
    i = pl.program_id(0)
    cur = i % 2
    prev = 1 - cur

    @pl.when(i == 0)
    def _():
        xm_ref[...] = jnp.zeros_like(xm_ref)
        hb_ref[...] = jnp.zeros_like(hb_ref)

    hp = hb_ref[prev]
    f = _silu(_dot(hp, w1_ref[...])) * _dot(hp, w3_ref[...])
    o_ref[...] = xm_ref[prev] + modp_ref[0][5:6] * _dot(f.astype(BF16), w2_ref[...])

    mod = mod_ref[0]
    y = (yf_ref[...] + yb_ref[...]) * _silu(z_ref[...])
    y_ssd = y * lax.rsqrt(jnp.mean(y * y, axis=-1, keepdims=True) + EPS) * gssd_ref[...]
    y5 = _gelu_tanh(ys5_ref[...] + d5_ref[...] * u_ref[...])
    y5 = y5 * jax.nn.sigmoid(_dot(y5.astype(BF16), gw_ref[...].astype(BF16)) + gb_ref[...])
    wo = wo_ref[...]
    o = _dot(y_ssd.astype(BF16), wo[0:256])
    o += _dot(y5.astype(BF16), wo[256:512])
    o += _dot(yda_ref[...].astype(BF16), wo[512:1024])
    x = x_ref[...] + mod[2:3] * o
    h = x * lax.rsqrt(jnp.mean(x * x, axis=-1, keepdims=True) + EPS) * g2_ref[...] * (1.0 + mod[4:5]) + mod[3:4]
    xm_ref[cur] = x
    hb_ref[cur] = h.astype(BF16)


def _outffn_call(xs, y_f, y_b, z, g_ssd, y_s5, u, d5, glu_w, glu_b, y_da, modl, g2, w_o, w1, w3, w2, n_lat):
    t, d = xs.shape
    dff = w1.shape[1]
    tm = FFN_ROW_TILE
    assert n_lat % tm == 0 and t % tm == 0
    n_lat_tiles = n_lat // tm
    n_tiles = t // tm
    row = lambda i: (jnp.minimum(i, n_tiles - 1), 0)
    done = lambda i: (jnp.maximum(i - 1, 0), 0)
    const = lambda i: (0, 0)
    seg = lambda i: ((jnp.minimum(i, n_tiles - 1) >= n_lat_tiles).astype(jnp.int32), 0, 0)
    seg_done = lambda i: ((i - 1 >= n_lat_tiles).astype(jnp.int32), 0, 0)
    one = pl.Buffered(1)
    return pl.pallas_call(
        _outffn_body,
        grid=(n_tiles + 1,),
        in_specs=[pl.BlockSpec((tm, d), row),
                  pl.BlockSpec((tm, 256), row), pl.BlockSpec((tm, 256), row), pl.BlockSpec((tm, 256), row),
                  pl.BlockSpec((1, 256), const),
                  pl.BlockSpec((tm, 256), row), pl.BlockSpec((tm, 256), row),
                  pl.BlockSpec((1, 256), const),
                  pl.BlockSpec((256, 256), const), pl.BlockSpec((1, 256), const),
                  pl.BlockSpec((tm, DA_WIDTH), row),
                  pl.BlockSpec((1, 8, d), seg),
                  pl.BlockSpec((1, 8, d), seg_done),
                  pl.BlockSpec((1, d), const),
                  pl.BlockSpec((d, d), const, pipeline_mode=one),
                  pl.BlockSpec((d, dff), const, pipeline_mode=one),
                  pl.BlockSpec((d, dff), const, pipeline_mode=one),
                  pl.BlockSpec((dff, d), const, pipeline_mode=one)],
        out_specs=pl.BlockSpec((tm, d), done),
        out_shape=jax.ShapeDtypeStruct((t, d), F32),
        scratch_shapes=[pltpu.VMEM((2, tm, d), F32), pltpu.VMEM((2, tm, d), BF16)],
        compiler_params=_cparams("arbitrary"),
        name="out_proj_ffn",
    )(xs, y_f, y_b, z, g_ssd, y_s5, u, d5, glu_w, glu_b, y_da, modl, modl, g2, w_o, w1, w3, w2)


def _rope_tables(n_lat, n_ctx):
    pos = jnp.arange(n_lat)
    lane = jnp.arange(LANES)
    d = lane % DA_HEAD_DIM
    axis = d // (2 * ROPE_FREQS)
    half = (d % (2 * ROPE_FREQS)) // ROPE_FREQS
    freqs = ROPE_THETA ** (-jnp.arange(ROPE_FREQS, dtype=F32) / ROPE_FREQS)
    coord = jnp.where(axis[None, :] == 0, (pos // GRID_W)[:, None], (pos % GRID_W)[:, None]).astype(F32)
    ang = coord * freqs[d % ROPE_FREQS][None, :]
    cos, sin = jnp.cos(ang), jnp.sin(ang)
    sa = jnp.where(half[None, :] == 0, -sin, 0.0)
    sb = jnp.where(half[None, :] == 1, sin, 0.0)
    pad = lambda a, v: jnp.concatenate([a, jnp.full((n_ctx, LANES), v, F32)], axis=0)
    return pad(cos, 1.0), pad(sa, 0.0), pad(sb, 0.0)


def _lane_row(vals, width=LANES):
    vals = vals.reshape(-1).astype(F32)
    return jnp.zeros((1, width), F32).at[0, :vals.shape[0]].set(vals)


def kernel(x, c, ctx, c_ctx, w_mod, b_mod, norm1, norm2, w_in, w_out, ssd_conv_w, ssd_conv_b, ssd_a_log, ssd_dt_bias, ssd_d, ssd_norm, s5_lam_re, s5_lam_im, s5_log_step, s5_b_re, s5_b_im, s5_c_re, s5_c_im, s5_d, s5_glu_w, s5_glu_b, da_q_norm, da_k_norm, da_lambda, da_sub_norm, ffn_w1, ffn_w3, ffn_w2):
    depth = w_mod.shape[0]
    bsz, n_lat, d = x.shape
    n_ctx = ctx.shape[1]
    t = n_ctx + n_lat
    tk = _pick_tile(t, (ATTN_TK, 1024, 640, 512, 256))
    assert bsz == 1 and n_ctx % ROW_TILE == 0 and n_lat % ROW_TILE == 0 and tk % ROW_TILE == 0
    assert n_ctx % (S5_CHUNK * S5_SCAN_TILE) == 0 and n_lat % (S5_CHUNK * S5_SCAN_TILE) == 0
    assert n_lat % n_ctx == 0 and tk % n_ctx == 0
    n_lat_tiles = n_lat // ROW_TILE

    xs = jnp.concatenate([x[0], ctx[0]], axis=0)
    cv = jnp.zeros((8, d), F32).at[0].set(c[0]).at[1].set(c_ctx)
    mod_all = _mod_call(cv, w_mod, b_mod)
    cos_t, sa_t, sb_t = _rope_tables(n_lat, n_ctx)
    gidx = jnp.arange(DA_WIDTH) // DA_HEAD_DIM
    gmat = (gidx[:, None] == gidx[None, :]).astype(BF16) * (1.0 / DA_HEAD_DIM)
    s5_w = _s5_weights(s5_lam_re, s5_lam_im, s5_log_step, s5_b_re, s5_b_im, s5_c_re, s5_c_im)

    for i in range(depth):
        lam_init = 0.8 - 0.6 * math.exp(-0.3 * i)
        modl = jnp.zeros((2, 8, d), F32).at[:, :6].set(mod_all[i, :2].reshape(2, 6, d))

        wi = w_in[i]
        w_r = jnp.concatenate([wi[:, 0:768], wi[:, 776:1032], wi[:, 1032:2568], wi[:, 768:776],
                               jnp.zeros((d, IN_PAD - 2568), F32)], axis=1).astype(BF16)
        gq = jnp.tile(da_q_norm[i], DA_WIDTH // DA_HEAD_DIM)[None, :]
        gk = jnp.tile(da_k_norm[i], DA_WIDTH // DA_HEAD_DIM)[None, :]
        z, xbc_raw, u, dt_raw, qh, kzb, vb = _inproj_call(
            xs, norm1[i][None, :], modl, w_r, gq, gk, gmat, cos_t, sa_t, sb_t, n_lat_tiles, tk)

        dt_r = dt_raw[:, :8].T
        cw = jnp.zeros((8, SSD_CONV_DIM), F32).at[:3].set(ssd_conv_w[i])
        a_neg = -jnp.exp(ssd_a_log[i])
        bias_c = _lane_row(ssd_dt_bias[i])
        a_c = _lane_row(a_neg)
        bias_r = jnp.broadcast_to(ssd_dt_bias[i].reshape(8, 1), (8, SSD_STEP_CHUNKS * SSD_CHUNK))
        a_r = jnp.broadcast_to(a_neg.reshape(8, 1), (8, SSD_STEP_CHUNKS * SSD_CHUNK))
        dsk = jnp.repeat(ssd_d[i], SSD_HEAD_DIM)[None, :]
        y_f, y_b = _ssd_call(xbc_raw, dt_raw, dt_r, cw, ssd_conv_b[i][None, :], bias_c, a_c, bias_r, a_r, dsk,
                             n_lat)

        y_s5 = _s5_mix(u, *(w[i] for w in s5_w), n_lat)

        lf = da_lambda[i]
        lam = (jnp.exp(jnp.sum(lf[0] * lf[1])) - jnp.exp(jnp.sum(lf[2] * lf[3])) + lam_init).reshape(1)
        g_sub = da_sub_norm[i][None, :]
        score_bound = LOG2E * math.sqrt(DA_HEAD_DIM) * jnp.max(jnp.abs(da_q_norm[i])) * jnp.max(jnp.abs(da_k_norm[i]))
        flag = (score_bound <= SCORE_LOG2_LIMIT).astype(jnp.int32).reshape(1)
        y_da = _attn_call(flag, lam, qh, kzb, vb, g_sub, 1.0 - lam_init, n_lat, 0, None, None, None)
        y_da = _attn_call(flag, lam, qh, kzb, vb, g_sub, 1.0 - lam_init, n_ctx, n_lat,
                          n_lat // tk, n_ctx, n_lat % tk, y_rest=y_da)

        xs = _outffn_call(xs, y_f, y_b, z, ssd_norm[i][None, :], y_s5, u, s5_d[i][None, :], s5_glu_w[i],
                          s5_glu_b[i][None, :], y_da, modl, norm2[i][None, :], w_out[i].astype(BF16),
                          ffn_w1[i].astype(BF16), ffn_w3[i].astype(BF16), ffn_w2[i].astype(BF16), n_lat)
    return xs[:n_lat][None]
```

```python
import functools
import math

import jax
import jax.numpy as jnp
from jax import lax
from jax.experimental import pallas as pl
from jax.experimental.pallas import tpu as pltpu

F32 = jnp.float32
BF16 = jnp.bfloat16
HI = lax.Precision.HIGHEST

EPS = 1e-6
GRID_W = 64
ROPE_THETA = 10000.0
ROPE_FREQS = 8

SSD_HEADS = 4
SSD_HEAD_DIM = 64
SSD_INNER = 256
SSD_GROUPS = 2
SSD_STATE = 64
SSD_CHUNK = 128
SSD_STEP_CHUNKS = 2
SSD_CONV_DIM = 512

S5_WIDTH = 256
S5_GROUP = 16
S5_GROUPS = 16
S5_STATE = 64
S5_CHUNK = 16
S5_HALF_GROUPS = S5_GROUPS // 2
S5_HALF_STATE = S5_HALF_GROUPS * S5_STATE
S5_SCAN_TILE = 16
S5_TOEPLITZ_BLOCKS = 2 * S5_CHUNK - 1
S5_KCOLS = 512

DA_HEADS = 8
DA_HEAD_DIM = 32
DA_V_DIM = 64
DA_WIDTH = 512
ATTN_HEADS_PER_STEP = 2
ATTN_TQ = 2048
ATTN_TK = 1280
LOG2E = 1.4426950408889634
SCORE_LOG2_LIMIT = 100.0

LANES = 128
ROW_TILE = 256
FFN_ROW_TILE = 256
IN_PAD = 2688
VMEM_LIMIT = 56 * 1024 * 1024


def _cparams(*sem):
    return pltpu.CompilerParams(dimension_semantics=sem, vmem_limit_bytes=VMEM_LIMIT)


def _dot(a, b, precision=None):
    return jnp.dot(a, b, preferred_element_type=F32, precision=precision)


def _dot_nt(a, b, precision=None):
    return lax.dot_general(a, b, (((1,), (1,)), ((), ())), preferred_element_type=F32,
                           precision=precision)


def _dot_tn(a, b, precision=None):
    return lax.dot_general(a, b, (((0,), (0,)), ((), ())), preferred_element_type=F32,
                           precision=precision)


def _silu(x):
    return x * jax.nn.sigmoid(x)


def _pick_tile(n, candidates):
    for c in candidates:
        if n % c == 0:
            return c
    return n


def _segment_order(i, n_lat, n_all, reverse):
    n_ctx = n_all - n_lat
    if not reverse:
        return jnp.where(i < n_ctx, n_lat + i, i - n_ctx)
    return jnp.where(i < n_ctx, n_all - 1 - i, n_lat - 1 - (i - n_ctx))


def _mod_body(cv_ref, w_ref, b_ref, o_ref):
    o_ref[0] = _dot(_silu(cv_ref[...]), w_ref[0], HI) + b_ref[0]


def _mod_call(cv, w_mod, b_mod):
    depth, d, n = w_mod.shape
    tn = 1024
    return pl.pallas_call(
        _mod_body,
        grid=(depth, n // tn),
        in_specs=[pl.BlockSpec((8, d), lambda l, j: (0, 0)),
                  pl.BlockSpec((1, d, tn), lambda l, j: (l, 0, j)),
                  pl.BlockSpec((1, 1, tn), lambda l, j: (l, 0, j))],
        out_specs=pl.BlockSpec((1, 8, tn), lambda l, j: (l, 0, j)),
        out_shape=jax.ShapeDtypeStruct((depth, 8, n), F32),
        compiler_params=_cparams("arbitrary", "arbitrary"),
        name="adaln_mod",
    )(cv, w_mod, b_mod.reshape(depth, 1, n))


def _inproj_project(x_ref, g_ref, mod_ref, w_ref, z_ref, xbc_ref, u_ref, dt_ref, qkv_ref):
    x = x_ref[...]
    mod = mod_ref[0]
    ms = jnp.mean(x * x, axis=-1, keepdims=True)
    h = x * lax.rsqrt(ms + EPS) * g_ref[...] * (1.0 + mod[1:2]) + mod[0:1]
    p = _dot(h.astype(BF16), w_ref[...])
    z_ref[...] = p[:, 0:256]
    xbc_ref[...] = p[:, 256:768]
    u_ref[...] = p[:, 768:1024]
    dt_ref[...] = p[:, 2560:2688]
    qkv_ref[...] = p[:, 1024:2560]


def _inproj_attn_operands(qkv_ref, gq_ref, gk_ref, gm_ref, cos_ref, sa_ref, sb_ref, q_ref, k_ref, v_ref):
    p = qkv_ref[...]
    rows = p.shape[0]
    gm = gm_ref[...]
    cos = cos_ref[...]
    sa = sa_ref[...]
    sb = sb_ref[...]

    def norm_rope(t, gain, scale):
        ms32 = _dot((t * t).astype(BF16), gm)
        tn = t * lax.rsqrt(ms32 + EPS) * gain
        outs = []
        for j in range(DA_WIDTH // LANES):
            tb = tn[:, j * LANES:(j + 1) * LANES]
            ob = tb * cos + pltpu.roll(tb, LANES - ROPE_FREQS, 1) * sa + pltpu.roll(tb, ROPE_FREQS, 1) * sb
            outs.append(ob * scale)
        return jnp.concatenate(outs, axis=1)

    qn = norm_rope(p[:, 0:512], gq_ref[...], LOG2E * DA_HEAD_DIM ** -0.5).astype(BF16)
    kt = norm_rope(p[:, 512:1024], gk_ref[...], 1.0).T.astype(BF16)
    vv = p[:, 1024:1536].astype(BF16)
    lane = lax.broadcasted_iota(jnp.int32, (rows, LANES - DA_V_DIM), 1)
    ones_col = jnp.where(lane == 0, 1.0, 0.0).astype(BF16)
    zeros_k = jnp.zeros((DA_HEAD_DIM, rows), BF16)
    for hd in range(DA_HEADS):
        lo_, mid, hi_ = hd * DA_V_DIM, hd * DA_V_DIM + DA_HEAD_DIM, (hd + 1) * DA_V_DIM
        q_ref[hd] = qn[:, lo_:hi_]
        k_ref[hd, 0, 0, 0:DA_HEAD_DIM, :] = kt[lo_:mid]
        k_ref[hd, 0, 0, DA_HEAD_DIM:, :] = zeros_k
        k_ref[hd, 0, 1, 0:DA_HEAD_DIM, :] = zeros_k
        k_ref[hd, 0, 1, DA_HEAD_DIM:, :] = kt[mid:hi_]
        v_ref[hd, 0] = jnp.concatenate([vv[:, lo_:hi_], ones_col], axis=1)


def _inproj_body(x_ref, g_ref, mod_ref, w_ref, gq_ref, gk_ref, gm_ref, cos_ref, sa_ref, sb_ref,
                 z_ref, xbc_ref, u_ref, dt_ref, q_ref, k_ref, v_ref, qkv_a_ref, qkv_b_ref):
    i = pl.program_id(0)

    @pl.when(i == 0)
    def _():
        qkv_b_ref[...] = jnp.zeros_like(qkv_b_ref)

    def step(cur_ref, prev_ref):
        _inproj_attn_operands(prev_ref, gq_ref, gk_ref, gm_ref, cos_ref, sa_ref, sb_ref, q_ref, k_ref, v_ref)
        _inproj_project(x_ref, g_ref, mod_ref, w_ref, z_ref, xbc_ref, u_ref, dt_ref, cur_ref)

    @pl.when(i % 2 == 0)
    def _():
        step(qkv_a_ref, qkv_b_ref)

    @pl.when(i % 2 == 1)
    def _():
        step(qkv_b_ref, qkv_a_ref)


def _inproj_call(xs, g1, modl, w_r, gq, gk, gmat, cos_t, sa_t, sb_t, n_lat_tiles, tk):
    t, d = xs.shape
    tm = ROW_TILE
    n_tiles = t // tm
    per = tk // tm
    row = lambda i: (jnp.minimum(i, n_tiles - 1), 0)
    done = lambda i: (jnp.maximum(i - 1, 0), 0)
    const = lambda i: (0, 0)
    seg = lambda i: ((jnp.minimum(i, n_tiles - 1) >= n_lat_tiles).astype(jnp.int32), 0, 0)
    flat = [(256, F32), (512, F32), (256, F32), (LANES, F32)]

    def k_block(i):
        j = jnp.maximum(i - 1, 0)
        return (0, j // per, 0, 0, j % per)

    def v_block(i):
        j = jnp.maximum(i - 1, 0)
        return (0, j // per, j % per, 0)

    return pl.pallas_call(
        _inproj_body,
        grid=(n_tiles + 1,),
        in_specs=[pl.BlockSpec((tm, d), row),
                  pl.BlockSpec((1, d), const),
                  pl.BlockSpec((1, 8, d), seg),
                  pl.BlockSpec((d, IN_PAD), const),
                  pl.BlockSpec((1, DA_WIDTH), const),
                  pl.BlockSpec((1, DA_WIDTH), const),
                  pl.BlockSpec((DA_WIDTH, DA_WIDTH), const),
                  pl.BlockSpec((tm, LANES), done),
                  pl.BlockSpec((tm, LANES), done),
                  pl.BlockSpec((tm, LANES), done)],
        out_specs=[pl.BlockSpec((tm, w), row) for w, _ in flat]
                  + [pl.BlockSpec((DA_HEADS, tm, 2 * DA_HEAD_DIM), lambda i: (0, jnp.maximum(i - 1, 0), 0)),
                     pl.BlockSpec((DA_HEADS, 1, 2, 2 * DA_HEAD_DIM, tm), k_block),
                     pl.BlockSpec((DA_HEADS, 1, tm, LANES), v_block)],
        out_shape=[jax.ShapeDtypeStruct((t, w), dt) for w, dt in flat]
                  + [jax.ShapeDtypeStruct((DA_HEADS, t, 2 * DA_HEAD_DIM), BF16),
                     jax.ShapeDtypeStruct((DA_HEADS, t // tk, 2, 2 * DA_HEAD_DIM, tk), BF16),
                     jax.ShapeDtypeStruct((DA_HEADS, t // tk, tk, LANES), BF16)],
        scratch_shapes=[pltpu.VMEM((tm, 3 * DA_WIDTH), F32)] * 2,
        compiler_params=_cparams("arbitrary"),
        name="in_proj",
    )(xs, g1, modl, w_r, gq, gk, gmat, cos_t, sa_t, sb_t)


def _softplus(x):
    return jnp.maximum(x, 0.0) + jnp.log1p(jnp.exp(-jnp.abs(x)))


def _ssd_block(c, n_lat, n_blocks, reverse, xc_ref, xp_ref, xn_ref, dtc_ref, dtr_ref, cw_ref, cb_ref,
               bias_c_ref, a_c_ref, bias_r_ref, a_r_ref, st_ref):
    L = SSD_CHUNK
    rows = xc_ref.shape[0]
    x = xc_ref[...]
    seg_first = jnp.logical_or(c == 0, c == n_lat)
    seg_last = jnp.logical_or(c == n_lat - 1, c == n_blocks - 1)
    prow = jnp.where(seg_first, 0.0, xp_ref[7:8, :])
    nrow = jnp.where(seg_last, 0.0, xn_ref[0:1, :])
    ridx = lax.broadcasted_iota(jnp.int32, (rows, 1), 0)
    xprev = jnp.where(ridx == 0, prow, pltpu.roll(x, 1, 0))
    xnext = jnp.where(ridx == rows - 1, nrow, pltpu.roll(x, rows - 1, 0))
    cw = cw_ref[...]
    conv = xprev * cw[0:1] + x * cw[1:2] + xnext * cw[2:3] + cb_ref[...]
    xbc = _silu(conv)

    dt_c = _softplus(dtc_ref[...] + bias_c_ref[...])
    dt_r = _softplus(dtr_ref[...] + bias_r_ref[...])
    adt_c = dt_c * a_c_ref[...]
    adt_r = dt_r * a_r_ref[...]

    li = lax.broadcasted_iota(jnp.int32, (L, L), 0)
    si = lax.broadcasted_iota(jnp.int32, (L, L), 1)
    mask = (li <= si) if reverse else (li >= si)
    mask_t = (li >= si) if reverse else (li <= si)
    n_sub = rows // L
    ys = [None] * n_sub
    for s in (reversed(range(n_sub)) if reverse else range(n_sub)):
        r = slice(s * L, (s + 1) * L)
        ys[s] = _ssd_chunk(xbc[r], dt_c[r], adt_c[r], adt_r[:, r], mask, mask_t, st_ref, reverse)
    return jnp.concatenate(ys, axis=0), xbc[:, :SSD_INNER]


def _split3(x):
    hi = x.astype(BF16)
    r1 = x - hi.astype(F32)
    mid = r1.astype(BF16)
    lo = (r1 - mid.astype(F32)).astype(BF16)
    return hi, mid, lo


def _ssd_chunk(xbc, dt_c, adt_c, adt_r, mask, mask_t, st_ref, reverse):
    L = SSD_CHUNK
    hpg = SSD_HEADS // SSD_GROUPS
    off = SSD_HEADS if reverse else 0
    m_b, mt_b = mask.astype(BF16), mask_t.astype(BF16)
    cs_c = sum(_dot(m_b, part) for part in _split3(adt_c))
    cs_r = sum(_dot(part, mt_b) for part in _split3(adt_r))
    edge = 0 if reverse else L - 1
    tot = cs_c[edge:edge + 1, :]

    kk = lax.broadcasted_iota(jnp.int32, (LANES, SSD_INNER), 0)
    jj = lax.broadcasted_iota(jnp.int32, (LANES, SSD_INNER), 1)
    spread = (kk == off + jj // SSD_HEAD_DIM).astype(BF16)
    cols = jnp.concatenate([dt_c, jnp.exp(cs_c), jnp.exp(tot - cs_c)], axis=0)
    c_hi = cols.astype(BF16)
    c_lo = (cols - c_hi.astype(F32)).astype(BF16)
    wide = _dot(c_hi, spread) + _dot(c_lo, spread)
    dt_w, ecs_w, dst_w = wide[0:L], wide[L:2 * L], wide[2 * L:3 * L]

    xs = xbc[:, :SSD_INNER]
    b_all = xbc[:, SSD_INNER:SSD_INNER + LANES]
    c_all = xbc[:, SSD_INNER + LANES:SSD_INNER + 2 * LANES]
    xd = xs * dt_w
    xd_b = xd.astype(BF16)
    b_b = b_all.astype(BF16)
    c_b = c_all.astype(BF16)
    lane = lax.broadcasted_iota(jnp.int32, (L, LANES), 1)
    col = lax.broadcasted_iota(jnp.int32, (L, SSD_INNER), 1)

    y = ecs_w * _dot(c_b, st_ref[...].astype(BF16))
    for g in range(SSD_GROUPS):
        c_g = jnp.where(lane // SSD_STATE == g, c_b, jnp.zeros_like(c_b))
        gmat = _dot_nt(c_g, b_b)
        for hh in range(hpg):
            h = g * hpg + hh
            dec = jnp.exp(jnp.where(mask, cs_c[:, off + h:off + h + 1] - cs_r[off + h:off + h + 1, :], -jnp.inf))
            xd_h = jnp.where(col // SSD_HEAD_DIM == h, xd_b, jnp.zeros_like(xd_b))
            y += _dot((gmat * dec).astype(BF16), xd_h)

    kr = lax.broadcasted_iota(jnp.int32, (LANES, SSD_INNER), 0)
    kc = lax.broadcasted_iota(jnp.int32, (LANES, SSD_INNER), 1)
    block = kr // SSD_STATE == kc // (hpg * SSD_HEAD_DIM)
    upd = _dot(b_all.T.astype(BF16), (xd * dst_w).astype(BF16))
    st_ref[...] = st_ref[...] * ecs_w[edge:edge + 1, :] + jnp.where(block, upd, 0.0)
    return y


def _ssd_body(n_lat, n_chunks, *refs):
    fwd_in, bwd_in = refs[0:5], refs[5:10]
    cw_ref, cb_ref, bias_c_ref, a_c_ref, bias_r_ref, a_r_ref, dsk_ref, yf_ref, yb_ref, st_ref = refs[10:]
    i = pl.program_id(0)

    @pl.when(i == 0)
    def _():
        st_ref[...] = jnp.zeros_like(st_ref)

    shared = (cw_ref, cb_ref, bias_c_ref, a_c_ref, bias_r_ref, a_r_ref)
    y, xs = _ssd_block(_segment_order(i, n_lat, n_chunks, False), n_lat, n_chunks, False, *fwd_in, *shared,
                       st_ref.at[0])
    yf_ref[...] = y + dsk_ref[...] * xs
    y, _ = _ssd_block(_segment_order(i, n_lat, n_chunks, True), n_lat, n_chunks, True, *bwd_in, *shared,
                      st_ref.at[1])
    yb_ref[...] = y


def _ssd_call(xbc_raw, dt_c, dt_r, cw, cb, bias_c, a_c, bias_r, a_r, dsk, n_lat_rows):
    t = xbc_raw.shape[0]
    L = SSD_STEP_CHUNKS * SSD_CHUNK
    assert t % L == 0 and n_lat_rows % L == 0
    n_chunks = t // L
    n_lat = n_lat_rows // L
    sub = L // 8
    n_sub = t // 8
    const = lambda i: (0, 0)

    def chunk_specs(reverse):
        cidx = functools.partial(_segment_order, n_lat=n_lat, n_all=n_chunks, reverse=reverse)
        return [pl.BlockSpec((L, SSD_CONV_DIM), lambda i: (cidx(i), 0)),
                pl.BlockSpec((8, SSD_CONV_DIM), lambda i: (jnp.maximum(cidx(i) * sub - 1, 0), 0)),
                pl.BlockSpec((8, SSD_CONV_DIM), lambda i: (jnp.minimum((cidx(i) + 1) * sub, n_sub - 1), 0)),
                pl.BlockSpec((L, LANES), lambda i: (cidx(i), 0)),
                pl.BlockSpec((8, L), lambda i: (0, cidx(i)))]

    def out_spec(reverse):
        cidx = functools.partial(_segment_order, n_lat=n_lat, n_all=n_chunks, reverse=reverse)
        return pl.BlockSpec((L, SSD_INNER), lambda i: (cidx(i), 0))

    chunk_args = (xbc_raw, xbc_raw, xbc_raw, dt_c, dt_r)
    return pl.pallas_call(
        functools.partial(_ssd_body, n_lat, n_chunks),
        grid=(n_chunks,),
        in_specs=chunk_specs(False) + chunk_specs(True)
                 + [pl.BlockSpec((8, SSD_CONV_DIM), const),
                    pl.BlockSpec((1, SSD_CONV_DIM), const),
                    pl.BlockSpec((1, LANES), const),
                    pl.BlockSpec((1, LANES), const),
                    pl.BlockSpec((8, L), const),
                    pl.BlockSpec((8, L), const),
                    pl.BlockSpec((1, SSD_INNER), const)],
        out_specs=[out_spec(False), out_spec(True)],
        out_shape=[jax.ShapeDtypeStruct((t, SSD_INNER), F32)] * 2,
        scratch_shapes=[pltpu.VMEM((2, SSD_GROUPS * SSD_STATE, SSD_INNER), F32)],
        compiler_params=_cparams("arbitrary"),
        name="ssd_scan",
    )(*chunk_args, *chunk_args, cw, cb, bias_c, a_c, bias_r, a_r, dsk)


def _cmul(ar, ai, br, bi):
    return ar * br - ai * bi, ar * bi + ai * br


def _s5_step_pair(u_ref, a):
    return jnp.concatenate([u_ref[:, 2 * a, :], u_ref[:, 2 * a + 1, :]], axis=1).astype(BF16)


def _s5_in_body(u_ref, w_ref, lam_ref, efr_ref, efi_ref, ebr_ref, ebi_ref, wp_ref):
    hs = S5_HALF_STATE

    @pl.when(pl.program_id(1) == 0)
    def _():
        lam = lam_ref[0]
        wc = w_ref[0]
        rows = lambda a: jnp.broadcast_to(a, (LANES, hs))
        fr, fi, br, bi = rows(lam[0:1]), rows(lam[1:2]), rows(lam[2:3]), rows(lam[3:4])
        rgrp = lax.broadcasted_iota(jnp.int32, (LANES, LANES), 0) // S5_GROUP
        lgrp = lax.broadcasted_iota(jnp.int32, (LANES, LANES), 1) // S5_STATE
        planes = []
        for q in range(4):
            cq = wc[:, q * S5_STATE:(q + 1) * S5_STATE]
            cq2 = jnp.concatenate([cq, cq], axis=1)
            planes.append(jnp.concatenate(
                [jnp.where(rgrp == 2 * g4 + lgrp, cq2, 0.0) for g4 in range(S5_HALF_GROUPS // 2)], axis=1))
        ar, ai = planes[0], planes[1]
        for i in reversed(range(S5_CHUNK)):
            wp_ref[i * LANES:(i + 1) * LANES, 0:hs] = ar.astype(BF16)
            wp_ref[i * LANES:(i + 1) * LANES, hs:2 * hs] = ai.astype(BF16)
            ar, ai = _cmul(ar, ai, fr, fi)
        ar, ai = planes[2], planes[3]
        for i in range(S5_CHUNK):
            wp_ref[i * LANES:(i + 1) * LANES, 2 * hs:3 * hs] = ar.astype(BF16)
            wp_ref[i * LANES:(i + 1) * LANES, 3 * hs:4 * hs] = ai.astype(BF16)
            ar, ai = _cmul(ar, ai, br, bi)

    e = None
    for a in range(S5_CHUNK // 2):
        part = _dot(_s5_step_pair(u_ref, a), wp_ref[2 * a * LANES:2 * (a + 1) * LANES, :])
        e = part if e is None else e + part
    efr_ref[...] = e[:, 0:hs]
    efi_ref[...] = e[:, hs:2 * hs]
    ebr_ref[...] = e[:, 2 * hs:3 * hs]
    ebi_ref[...] = e[:, 3 * hs:4 * hs]


def _s5_in_call(u3, w_b, lam_rows):
    nch = u3.shape[0]
    tn = _pick_tile(nch, (208, 80, 40))
    hs = S5_HALF_STATE
    col = lambda h, n: (n, h)
    return pl.pallas_call(
        _s5_in_body,
        grid=(2, nch // tn),
        in_specs=[pl.BlockSpec((tn, S5_CHUNK, LANES), lambda h, n: (n, 0, h)),
                  pl.BlockSpec((1, LANES, 4 * S5_STATE), lambda h, n: (h, 0, 0)),
                  pl.BlockSpec((1, 4, hs), lambda h, n: (h, 0, 0))],
        out_specs=[pl.BlockSpec((tn, hs), col)] * 4,
        out_shape=[jax.ShapeDtypeStruct((nch, 2 * hs), F32)] * 4,
        scratch_shapes=[pltpu.VMEM((S5_CHUNK * LANES, 4 * hs), BF16)],
        compiler_params=_cparams("arbitrary", "arbitrary"),
        name="s5_chunk_in",
    )(u3, w_b, lam_rows)


def _s5_scan_body(a_ref, efr_ref, efi_ref, ebr_ref, ebi_ref, hfr_ref, hfi_ref, hbr_ref, hbi_ref, st_ref):
    @pl.when(pl.program_id(0) == 0)
    def _():
        st_ref[...] = jnp.zeros_like(st_ref)

    row = lambda ref, j: ref[j:j + 1, :]
    afr, afi, abr, abi = row(a_ref, 0), row(a_ref, 1), row(a_ref, 2), row(a_ref, 3)
    fr, fi, br, bi = row(st_ref, 0), row(st_ref, 1), row(st_ref, 2), row(st_ref, 3)
    for j in range(S5_SCAN_TILE):
        hfr_ref[j:j + 1, :] = fr
        hfi_ref[j:j + 1, :] = fi
        fr, fi = afr * fr - afi * fi + row(efr_ref, j), afr * fi + afi * fr + row(efi_ref, j)
        jb = S5_SCAN_TILE - 1 - j
        hbr_ref[jb:jb + 1, :] = br
        hbi_ref[jb:jb + 1, :] = bi
        br, bi = abr * br - abi * bi + row(ebr_ref, jb), abr * bi + abi * br + row(ebi_ref, jb)
    st_ref[0:1, :] = fr
    st_ref[1:2, :] = fi
    st_ref[2:3, :] = br
    st_ref[3:4, :] = bi


def _s5_scan_call(a_pow, efr, efi, ebr, ebi, n_lat_tiles):
    nch, width = efr.shape
    n_tiles = nch // S5_SCAN_TILE
    blk = (S5_SCAN_TILE, width)
    fwd = lambda i: (_segment_order(i, n_lat_tiles, n_tiles, False), 0)
    bwd = lambda i: (_segment_order(i, n_lat_tiles, n_tiles, True), 0)
    shp = jax.ShapeDtypeStruct((nch, width), F32)
    return pl.pallas_call(
        _s5_scan_body,
        grid=(n_tiles,),
        in_specs=[pl.BlockSpec((4, width), lambda i: (0, 0)),
                  pl.BlockSpec(blk, fwd), pl.BlockSpec(blk, fwd),
                  pl.BlockSpec(blk, bwd), pl.BlockSpec(blk, bwd)],
        out_specs=[pl.BlockSpec(blk, fwd), pl.BlockSpec(blk, fwd),
                   pl.BlockSpec(blk, bwd), pl.BlockSpec(blk, bwd)],
        out_shape=[shp] * 4,
        scratch_shapes=[pltpu.VMEM((4, width), F32)],
        compiler_params=_cparams("arbitrary"),
        name="s5_state_scan",
    )(a_pow, efr, efi, ebr, ebi)


def _s5_out_body(u_ref, hfr_ref, hfi_ref, hbr_ref, hbi_ref, k_ref, t_ref, c_ref, lam_ref, y_ref, wq_ref, z_ref):
    hs = S5_HALF_STATE
    width = S5_CHUNK * LANES

    @pl.when(pl.program_id(1) == 0)
    def _():
        strip = _dot(k_ref[0].astype(BF16), t_ref[...])
        rgrp = lax.broadcasted_iota(jnp.int32, strip.shape, 0) // S5_GROUP
        cgrp = (lax.broadcasted_iota(jnp.int32, strip.shape, 1) % LANES) // S5_GROUP
        strip = jnp.where(rgrp == cgrp, strip, 0.0).astype(BF16)
        z_ref[0:LANES, :] = strip[:, LANES:]
        z_ref[LANES:, :] = strip[:, :-LANES]

        cc = c_ref[0]
        ogrp = lax.broadcasted_iota(jnp.int32, (S5_STATE, LANES), 1) // S5_GROUP

        def expand(q):
            cq = cc[q * S5_STATE:(q + 1) * S5_STATE]
            return jnp.concatenate([jnp.where(ogrp == g, cq, 0.0) for g in range(S5_HALF_GROUPS)], axis=0)

        lam = lam_ref[0]
        for d, order in ((0, range(S5_CHUNK)), (1, reversed(range(S5_CHUNK)))):
            lr, li = lam[2 * d], lam[2 * d + 1]
            a, b = expand(2 * d), expand(2 * d + 1)
            for j in order:
                a, b = a * lr + b * li, b * lr - a * li
                wq_ref[2 * d * hs:(2 * d + 1) * hs, j * LANES:(j + 1) * LANES] = a.astype(BF16)
                wq_ref[(2 * d + 1) * hs:(2 * d + 2) * hs, j * LANES:(j + 1) * LANES] = b.astype(BF16)

    h_all = jnp.concatenate([hfr_ref[...], hfi_ref[...], hbr_ref[...], hbi_ref[...]], axis=1).astype(BF16)
    y = _dot(h_all, wq_ref[...])
    for a in range(S5_CHUNK // 2):
        lo = (S5_CHUNK - 2 - 2 * a) * LANES
        y += _dot(_s5_step_pair(u_ref, a), z_ref[:, lo:lo + width])
    for j in range(S5_CHUNK):
        y_ref[:, j, :] = y[:, j * LANES:(j + 1) * LANES]


def _s5_out_call(u3, hfr, hfi, hbr, hbi, w_k, w_t, w_c, lam_cols):
    nch = u3.shape[0]
    tn = _pick_tile(nch, (208, 80, 40))
    hs = S5_HALF_STATE
    col = lambda h, n: (n, h)
    blk3 = pl.BlockSpec((tn, S5_CHUNK, LANES), lambda h, n: (n, 0, h))
    return pl.pallas_call(
        _s5_out_body,
        grid=(2, nch // tn),
        in_specs=[blk3] + [pl.BlockSpec((tn, hs), col)] * 4
                 + [pl.BlockSpec((1, LANES) + w_k.shape[2:], lambda h, n: (h, 0, 0)),
                    pl.BlockSpec(w_t.shape, lambda h, n: (0, 0)),
                    pl.BlockSpec((1, 4 * S5_STATE, LANES), lambda h, n: (h, 0, 0)),
                    pl.BlockSpec((1, 4, hs, LANES), lambda h, n: (h, 0, 0, 0))],
        out_specs=blk3,
        out_shape=jax.ShapeDtypeStruct(u3.shape, F32),
        scratch_shapes=[pltpu.VMEM((4 * hs, S5_CHUNK * LANES), BF16),
                        pltpu.VMEM((2 * LANES, (S5_TOEPLITZ_BLOCKS - 1) * LANES), BF16)],
        compiler_params=_cparams("arbitrary", "arbitrary"),
        name="s5_chunk_out",
    )(u3, hfr, hfi, hbr, hbi, w_k, w_t, w_c, lam_cols)


def _s5_weights(lam_re, lam_im, log_step, b_re, b_im, c_re, c_im):
    tc = S5_CHUNK
    nl = lam_re.shape[0]
    hg = S5_HALF_GROUPS
    ein = functools.partial(jnp.einsum, precision=HI)
    step = jnp.exp(log_step)[..., None]
    er, ei = lam_re * step, lam_im * step
    mag = jnp.exp(er)
    lbr, lbi = mag * jnp.cos(ei), mag * jnp.sin(ei)
    den = lam_re * lam_re + lam_im * lam_im
    nr, ni = lbr - 1.0, lbi
    cr, ci = (nr * lam_re + ni * lam_im) / den, (ni * lam_re - nr * lam_im) / den
    bbr = cr[..., None] * b_re - ci[..., None] * b_im
    bbi = cr[..., None] * b_im + ci[..., None] * b_re
    taus = jnp.arange(tc + 1, dtype=F32)[:, None, None]
    pmag = jnp.exp(er[:, :, None] * taus)
    pr, pi = pmag * jnp.cos(ei[:, :, None] * taus), pmag * jnp.sin(ei[:, :, None] * taus)

    wr = pr[:, :, :tc, :, :, None] * bbr[:, :, None] - pi[:, :, :tc, :, :, None] * bbi[:, :, None]
    wi = pr[:, :, :tc, :, :, None] * bbi[:, :, None] + pi[:, :, :tc, :, :, None] * bbr[:, :, None]
    kk = ein('ldgcp,ldtgpk->ldtgck', c_re, wr) - ein('ldgcp,ldtgpk->ldtgck', c_im, wi)
    kf, kb = kk[:, 0], kk[:, 1]
    seq = jnp.concatenate([kb[:, :0:-1], kf[:, :1] + kb[:, :1], kf[:, 1:]], axis=1)
    seq = seq.reshape(nl, S5_TOEPLITZ_BLOCKS, 2, hg, S5_GROUP, S5_GROUP)
    w_k = seq.transpose(0, 2, 3, 5, 1, 4).reshape(nl, 2, LANES, S5_TOEPLITZ_BLOCKS * S5_GROUP)
    w_k = jnp.pad(w_k, ((0, 0), (0, 0), (0, 0), (0, S5_KCOLS - S5_TOEPLITZ_BLOCKS * S5_GROUP)))
    r = jnp.arange(S5_KCOLS)
    s = jnp.arange(S5_TOEPLITZ_BLOCKS * LANES)
    w_t = ((r[:, None] // S5_GROUP == s[None, :] // LANES)
           & (r[:, None] % S5_GROUP == s[None, :] % S5_GROUP)).astype(BF16)

    planes_b = jnp.stack([bbr[:, 0], bbi[:, 0], bbr[:, 1], bbi[:, 1]], axis=1)
    planes_b = planes_b.reshape(nl, 4, 2, hg, S5_STATE, S5_GROUP)
    w_b = planes_b.transpose(0, 2, 3, 5, 1, 4).reshape(nl, 2, LANES, 4 * S5_STATE)
    planes_c = jnp.stack([c_re[:, 0], -c_im[:, 0], c_re[:, 1], -c_im[:, 1]], axis=1)
    planes_c = planes_c.reshape(nl, 4, 2, hg, S5_GROUP, S5_STATE)
    w_c = planes_c.transpose(0, 2, 1, 5, 3, 4).reshape(nl, 2, 4 * S5_STATE, LANES)

    halves = lambda a: a.reshape(nl, 2, S5_HALF_STATE)
    lam_rows = jnp.stack([halves(lbr[:, 0]), halves(lbi[:, 0]), halves(lbr[:, 1]), halves(lbi[:, 1])], axis=2)
    lam_cols = jnp.broadcast_to(lam_rows[..., None], lam_rows.shape + (LANES,))
    full = lambda a: a.reshape(nl, S5_GROUPS * S5_STATE)
    a_pow = jnp.stack([full(pr[:, 0, tc]), full(pi[:, 0, tc]), full(pr[:, 1, tc]), full(pi[:, 1, tc])], axis=1)
    return (w_b, w_k, w_c, lam_rows, lam_cols, a_pow), w_t


def _s5_mix(u, w_b, w_k, w_c, lam_rows, lam_cols, a_pow, w_t, n_lat):
    t = u.shape[0]
    nch = t // S5_CHUNK
    u3 = u.reshape(nch, S5_CHUNK, S5_WIDTH)
    es = _s5_in_call(u3, w_b, lam_rows)
    hs = _s5_scan_call(a_pow, *es, n_lat // (S5_CHUNK * S5_SCAN_TILE))
    return _s5_out_call(u3, *hs, w_k, w_t, w_c, lam_cols).reshape(t, S5_WIDTH)


def _attn_body(flag_ref, lam_ref, q_ref, k_ref, v_ref, g_ref, o_ref, acc_ref, m_ref, oh_ref, *, n_kv, post_scale):
    bounded = flag_ref[0] == 1

    def head(hh, carry):
        q = q_ref[hh]
        acc_ref[...] = jnp.zeros_like(acc_ref)

        @pl.when(bounded)
        def _():
            def step(b, carry2):
                v = v_ref[hh, b]
                for c in range(2):
                    p = jnp.exp2(_dot(q, k_ref[hh, b, c])).astype(BF16)
                    acc_ref[c] += _dot(p, v)
                return carry2
            lax.fori_loop(0, n_kv, step, 0)

        @pl.when(jnp.logical_not(bounded))
        def _():
            m_ref[...] = jnp.full_like(m_ref, -jnp.inf)

            def step(b, carry2):
                v = v_ref[hh, b]
                for c in range(2):
                    s = _dot(q, k_ref[hh, b, c])
                    m_old = m_ref[c]
                    m_new = jnp.maximum(m_old, jnp.max(s, axis=1, keepdims=True))
                    p = jnp.exp2(s - m_new[:, 0:1]).astype(BF16)
                    acc_ref[c] = jnp.exp2(m_old - m_new) * acc_ref[c] + _dot(p, v)
                    m_ref[c] = m_new
                return carry2
            lax.fori_loop(0, n_kv, step, 0)

        a0 = acc_ref[0]
        a1 = acc_ref[1]
        o = a0[:, :DA_V_DIM] / a0[:, DA_V_DIM:DA_V_DIM + 1] \
            - lam_ref[0] * (a1[:, :DA_V_DIM] / a1[:, DA_V_DIM:DA_V_DIM + 1])
        ms = jnp.mean(o * o, axis=-1, keepdims=True)
        oh_ref[hh] = o * lax.rsqrt(ms + EPS) * g_ref[...] * post_scale
        return carry

    lax.fori_loop(0, ATTN_HEADS_PER_STEP, head, 0)
    o_ref[...] = jnp.concatenate([oh_ref[hh] for hh in range(ATTN_HEADS_PER_STEP)], axis=1)


def _attn_body_keep_rest(flag_ref, lam_ref, q_ref, k_ref, v_ref, g_ref, rest_ref, *refs, **kw):
    del rest_ref
    _attn_body(flag_ref, lam_ref, q_ref, k_ref, v_ref, g_ref, *refs, **kw)


def _attn_call(flag, lam, qh, kzb, vb, g_sub, post_scale, q_rows, q_row0, kv_block, kv_cols, kv_col0, y_rest=None):
    nh = qh.shape[0]
    hp = ATTN_HEADS_PER_STEP
    tk = kzb.shape[4]
    tq = _pick_tile(q_rows, (ATTN_TQ, 256))
    assert q_row0 % tq == 0
    q0 = q_row0 // tq
    if kv_block is None:
        n_kv = kzb.shape[1]
        one = pl.Buffered(1)
        k_spec = pl.BlockSpec((hp, n_kv, 2, 2 * DA_HEAD_DIM, tk), lambda h, i: (h, 0, 0, 0, 0), pipeline_mode=one)
        v_spec = pl.BlockSpec((hp, n_kv, tk, LANES), lambda h, i: (h, 0, 0, 0), pipeline_mode=one)
    else:
        assert kv_col0 % kv_cols == 0 and tk % kv_cols == 0
        n_kv, cb = 1, kv_col0 // kv_cols
        k_spec = pl.BlockSpec((hp, 1, 2, 2 * DA_HEAD_DIM, kv_cols), lambda h, i: (h, kv_block, 0, 0, cb))
        v_spec = pl.BlockSpec((hp, 1, kv_cols, LANES), lambda h, i: (h, kv_block, cb, 0))
    smem = pl.BlockSpec(memory_space=pltpu.SMEM)
    in_specs = [smem, smem,
                pl.BlockSpec((hp, tq, 2 * DA_HEAD_DIM), lambda h, i: (h, q0 + i, 0)),
                k_spec, v_spec,
                pl.BlockSpec((1, DA_V_DIM), lambda h, i: (0, 0))]
    args = (flag, lam, qh, kzb, vb, g_sub)
    body = functools.partial(_attn_body, n_kv=n_kv, post_scale=post_scale)
    aliases = {}
    if y_rest is not None:
        in_specs.append(pl.BlockSpec(memory_space=pl.ANY))
        args += (y_rest,)
        body = functools.partial(_attn_body_keep_rest, n_kv=n_kv, post_scale=post_scale)
        aliases = {len(args) - 1: 0}
    return pl.pallas_call(
        body,
        grid=(nh // hp, q_rows // tq),
        in_specs=in_specs,
        out_specs=pl.BlockSpec((tq, hp * DA_V_DIM), lambda h, i: (q0 + i, h)),
        out_shape=jax.ShapeDtypeStruct((qh.shape[1], nh * DA_V_DIM), F32),
        scratch_shapes=[pltpu.VMEM((2, tq, LANES), F32), pltpu.VMEM((2, tq, LANES), F32),
                        pltpu.VMEM((hp, tq, DA_V_DIM), F32)],
        input_output_aliases=aliases,
        compiler_params=_cparams("arbitrary", "arbitrary"),
        name="diff_attn",
    )(*args)


def _gelu_tanh(x):
    return 0.5 * x * (1.0 + jnp.tanh(math.sqrt(2.0 / math.pi) * (x + 0.044715 * (x * x * x))))


def _outffn_body(x_ref, yf_ref, yb_ref, z_ref, gssd_ref, ys5_ref, u_ref, d5_ref, gw_ref, gb_ref,
                 yda_ref, mod_ref, modp_ref, g2_ref, wo_ref, w1_ref, w3_ref, w2_ref, o_ref, xm_ref, hb_ref):
    i = pl.program_id(0)
    cur = i % 2
    prev = 1 - cur

    @pl.when(i == 0)
    def _():
        xm_ref[...] = jnp.zeros_like(xm_ref)
        hb_ref[...] = jnp.zeros_like(hb_ref)

    hp = hb_ref[prev]
    f = _silu(_dot(hp, w1_ref[...])) * _dot(hp, w3_ref[...])
    o_ref[...] = xm_ref[prev] + modp_ref[0][5:6] * _dot(f.astype(BF16), w2_ref[...])

    mod = mod_ref[0]
    y = (yf_ref[...] + yb_ref[...]) * _silu(z_ref[...])
    y_ssd = y * lax.rsqrt(jnp.mean(y * y, axis=-1, keepdims=True) + EPS) * gssd_ref[...]
    y5 = _gelu_tanh(ys5_ref[...] + d5_ref[...] * u_ref[...])
    y5 = y5 * jax.nn.sigmoid(_dot(y5.astype(BF16), gw_ref[...].astype(BF16)) + gb_ref[...])
    wo = wo_ref[...]
    o = _dot(y_ssd.astype(BF16), wo[0:256])
    o += _dot(y5.astype(BF16), wo[256:512])
    o += _dot(yda_ref[...].astype(BF16), wo[512:1024])
    x = x_ref[...] + mod[2:3] * o
    h = x * lax.rsqrt(jnp.mean(x * x, axis=-1, keepdims=True) + EPS) * g2_ref[...] * (1.0 + mod[4:5]) + mod[3:4]
    xm_ref[cur] = x
    hb_ref[cur] = h.astype(BF16)


def _outffn_call(xs, y_f, y_b, z, g_ssd, y_s5, u, d5, glu_w, glu_b, y_da, modl, g2, w_o, w1, w3, w2, n_lat,
                 latent_only):
    t, d = xs.shape
    if latent_only:
        t = n_lat
    dff = w1.shape[1]
    tm = FFN_ROW_TILE
    assert n_lat % tm == 0 and t % tm == 0
    n_lat_tiles = n_lat // tm
    n_tiles = t // tm
    row = lambda i: (jnp.minimum(i, n_tiles - 1), 0)
    done = lambda i: (jnp.maximum(i - 1, 0), 0)
    const = lambda i: (0, 0)
    seg = lambda i: ((jnp.minimum(i, n_tiles - 1) >= n_lat_tiles).astype(jnp.int32), 0, 0)
    seg_done = lambda i: ((i - 1 >= n_lat_tiles).astype(jnp.int32), 0, 0)
    one = pl.Buffered(1)
    return pl.pallas_call(
        _outffn_body,
        grid=(n_tiles + 1,),
        in_specs=[pl.BlockSpec((tm, d), row),
                  pl.BlockSpec((tm, 256), row), pl.BlockSpec((tm, 256), row), pl.BlockSpec((tm, 256), row),
                  pl.BlockSpec((1, 256), const),
                  pl.BlockSpec((tm, 256), row), pl.BlockSpec((tm, 256), row),
                  pl.BlockSpec((1, 256), const),
                  pl.BlockSpec((256, 256), const), pl.BlockSpec((1, 256), const),
                  pl.BlockSpec((tm, DA_WIDTH), row),
                  pl.BlockSpec((1, 8, d), seg),
                  pl.BlockSpec((1, 8, d), seg_done),
                  pl.BlockSpec((1, d), const),
                  pl.BlockSpec((d, d), const, pipeline_mode=one),
                  pl.BlockSpec((d, dff), const, pipeline_mode=one),
                  pl.BlockSpec((d, dff), const, pipeline_mode=one),
                  pl.BlockSpec((dff, d), const, pipeline_mode=one)],
        out_specs=pl.BlockSpec((tm, d), done),
        out_shape=jax.ShapeDtypeStruct((t, d), F32),
        scratch_shapes=[pltpu.VMEM((2, tm, d), F32), pltpu.VMEM((2, tm, d), BF16)],
        compiler_params=_cparams("arbitrary"),
        name="out_proj_ffn",
    )(xs, y_f, y_b, z, g_ssd, y_s5, u, d5, glu_w, glu_b, y_da, modl, modl, g2, w_o, w1, w3, w2)


def _rope_tables(n_lat, n_ctx):
    pos = jnp.arange(n_lat)
    lane = jnp.arange(LANES)
    d = lane % DA_HEAD_DIM
    axis = d // (2 * ROPE_FREQS)
    half = (d % (2 * ROPE_FREQS)) // ROPE_FREQS
    freqs = ROPE_THETA ** (-jnp.arange(ROPE_FREQS, dtype=F32) / ROPE_FREQS)
    coord = jnp.where(axis[None, :] == 0, (pos // GRID_W)[:, None], (pos % GRID_W)[:, None]).astype(F32)
    ang = coord * freqs[d % ROPE_FREQS][None, :]
    cos, sin = jnp.cos(ang), jnp.sin(ang)
    sa = jnp.where(half[None, :] == 0, -sin, 0.0)
    sb = jnp.where(half[None, :] == 1, sin, 0.0)
    pad = lambda a, v: jnp.concatenate([a, jnp.full((n_ctx, LANES), v, F32)], axis=0)
    return pad(cos, 1.0), pad(sa, 0.0), pad(sb, 0.0)


def _lane_row(vals, width=LANES):
    vals = vals.reshape(-1).astype(F32)
    return jnp.zeros((1, width), F32).at[0, :vals.shape[0]].set(vals)


def kernel(x, c, ctx, c_ctx, w_mod, b_mod, norm1, norm2, w_in, w_out, ssd_conv_w, ssd_conv_b, ssd_a_log, ssd_dt_bias, ssd_d, ssd_norm, s5_lam_re, s5_lam_im, s5_log_step, s5_b_re, s5_b_im, s5_c_re, s5_c_im, s5_d, s5_glu_w, s5_glu_b, da_q_norm, da_k_norm, da_lambda, da_sub_norm, ffn_w1, ffn_w3, ffn_w2):
    depth = w_mod.shape[0]
    bsz, n_lat, d = x.shape
    n_ctx = ctx.shape[1]
    t = n_ctx + n_lat
    tk = _pick_tile(t, (ATTN_TK, 1024, 640, 512, 256))
    assert bsz == 1 and n_ctx % ROW_TILE == 0 and n_lat % ROW_TILE == 0 and tk % ROW_TILE == 0
    assert n_ctx % (S5_CHUNK * S5_SCAN_TILE) == 0 and n_lat % (S5_CHUNK * S5_SCAN_TILE) == 0
    assert n_lat % n_ctx == 0 and tk % n_ctx == 0
    n_lat_tiles = n_lat // ROW_TILE

    xs = jnp.concatenate([x[0], ctx[0]], axis=0)
    cv = jnp.zeros((8, d), F32).at[0].set(c[0]).at[1].set(c_ctx)
    mod_all = _mod_call(cv, w_mod, b_mod)
    cos_t, sa_t, sb_t = _rope_tables(n_lat, n_ctx)
    gidx = jnp.arange(DA_WIDTH) // DA_HEAD_DIM
    gmat = (gidx[:, None] == gidx[None, :]).astype(BF16) * (1.0 / DA_HEAD_DIM)
    s5_w, s5_spread = _s5_weights(s5_lam_re, s5_lam_im, s5_log_step, s5_b_re, s5_b_im, s5_c_re, s5_c_im)

    for i in range(depth):
        lam_init = 0.8 - 0.6 * math.exp(-0.3 * i)
        modl = jnp.zeros((2, 8, d), F32).at[:, :6].set(mod_all[i, :2].reshape(2, 6, d))

        wi = w_in[i]
        w_r = jnp.concatenate([wi[:, 0:768], wi[:, 776:1032], wi[:, 1032:2568], wi[:, 768:776],
                               jnp.zeros((d, IN_PAD - 2568), F32)], axis=1).astype(BF16)
        gq = jnp.tile(da_q_norm[i], DA_WIDTH // DA_HEAD_DIM)[None, :]
        gk = jnp.tile(da_k_norm[i], DA_WIDTH // DA_HEAD_DIM)[None, :]
        z, xbc_raw, u, dt_raw, qh, kzb, vb = _inproj_call(
            xs, norm1[i][None, :], modl, w_r, gq, gk, gmat, cos_t, sa_t, sb_t, n_lat_tiles, tk)

        dt_r = dt_raw[:, :8].T
        cw = jnp.zeros((8, SSD_CONV_DIM), F32).at[:3].set(ssd_conv_w[i])
        a_neg = -jnp.exp(ssd_a_log[i])
        bias_c = _lane_row(ssd_dt_bias[i])
        a_c = _lane_row(a_neg)
        bias_r = jnp.broadcast_to(ssd_dt_bias[i].reshape(8, 1), (8, SSD_STEP_CHUNKS * SSD_CHUNK))
        a_r = jnp.broadcast_to(a_neg.reshape(8, 1), (8, SSD_STEP_CHUNKS * SSD_CHUNK))
        dsk = jnp.repeat(ssd_d[i], SSD_HEAD_DIM)[None, :]
        y_f, y_b = _ssd_call(xbc_raw, dt_raw, dt_r, cw, ssd_conv_b[i][None, :], bias_c, a_c, bias_r, a_r, dsk,
                             n_lat)

        y_s5 = _s5_mix(u, *(w[i] for w in s5_w), s5_spread, n_lat)

        lf = da_lambda[i]
        lam = (jnp.exp(jnp.sum(lf[0] * lf[1])) - jnp.exp(jnp.sum(lf[2] * lf[3])) + lam_init).reshape(1)
        g_sub = da_sub_norm[i][None, :]
        score_bound = LOG2E * math.sqrt(DA_HEAD_DIM) * jnp.max(jnp.abs(da_q_norm[i])) * jnp.max(jnp.abs(da_k_norm[i]))
        flag = (score_bound <= SCORE_LOG2_LIMIT).astype(jnp.int32).reshape(1)
        y_da = _attn_call(flag, lam, qh, kzb, vb, g_sub, 1.0 - lam_init, n_lat, 0, None, None, None)
        last = i == depth - 1
        if not last:
            y_da = _attn_call(flag, lam, qh, kzb, vb, g_sub, 1.0 - lam_init, n_ctx, n_lat,
                              n_lat // tk, n_ctx, n_lat % tk, y_rest=y_da)

        xs = _outffn_call(xs, y_f, y_b, z, ssd_norm[i][None, :], y_s5, u, s5_d[i][None, :], s5_glu_w[i],
                          s5_glu_b[i][None, :], y_da, modl, norm2[i][None, :], w_out[i].astype(BF16),
                          ffn_w1[i].astype(BF16), ffn_w3[i].astype(BF16), ffn_w2[i].astype(BF16), n_lat,
                          latent_only=last)
    return xs[None]
```

```python
import functools
import math

import jax
import jax.numpy as jnp
from jax import lax
from jax.experimental import pallas as pl
from jax.experimental.pallas import tpu as pltpu

F32 = jnp.float32
BF16 = jnp.bfloat16
HI = lax.Precision.HIGHEST

EPS = 1e-6
GRID_W = 64
ROPE_THETA = 10000.0
ROPE_FREQS = 8

SSD_HEADS = 4
SSD_HEAD_DIM = 64
SSD_INNER = 256
SSD_GROUPS = 2
SSD_STATE = 64
SSD_CHUNK = 128
SSD_STEP_CHUNKS = 2
SSD_CONV_DIM = 512

S5_WIDTH = 256
S5_GROUP = 16
S5_GROUPS = 16
S5_STATE = 64
S5_CHUNK = 16
S5_HALF_GROUPS = S5_GROUPS // 2
S5_HALF_STATE = S5_HALF_GROUPS * S5_STATE
S5_SCAN_TILE = 16
S5_TOEPLITZ_BLOCKS = 2 * S5_CHUNK - 1
S5_KCOLS = 512

DA_HEADS = 8
DA_HEAD_DIM = 32
DA_V_DIM = 64
DA_WIDTH = 512
ATTN_HEADS_PER_STEP = 2
ATTN_TQ = 2048
ATTN_TK = 1280
LOG2E = 1.4426950408889634
SCORE_LOG2_LIMIT = 100.0

LANES = 128
ROW_TILE = 256
FFN_ROW_TILE = 256
IN_PAD = 2688
VMEM_LIMIT = 56 * 1024 * 1024


def _cparams(*sem):
    return pltpu.CompilerParams(dimension_semantics=sem, vmem_limit_bytes=VMEM_LIMIT)


def _dot(a, b, precision=None):
    return jnp.dot(a, b, preferred_element_type=F32, precision=precision)


def _dot_nt(a, b, precision=None):
    return lax.dot_general(a, b, (((1,), (1,)), ((), ())), preferred_element_type=F32,
                           precision=precision)


def _dot_tn(a, b, precision=None):
    return lax.dot_general(a, b, (((0,), (0,)), ((), ())), preferred_element_type=F32,
                           precision=precision)


def _silu(x):
    return x * jax.nn.sigmoid(x)


def _pick_tile(n, candidates):
    for c in candidates:
        if n % c == 0:
            return c
    return n


def _interleave(*stages):
    results = [None] * len(stages)
    live = dict(enumerate(stages))
    while live:
        for k in list(live):
            try:
                next(live[k])
            except StopIteration as done:
                results[k] = done.value
                del live[k]
        yield
    return results


def _run(stage):
    while True:
        try:
            next(stage)
        except StopIteration as done:
            return done.value


def _segment_order(i, n_lat, n_all, reverse):
    n_ctx = n_all - n_lat
    if not reverse:
        return jnp.where(i < n_ctx, n_lat + i, i - n_ctx)
    return jnp.where(i < n_ctx, n_all - 1 - i, n_lat - 1 - (i - n_ctx))


def _mod_body(cv_ref, w_ref, b_ref, o_ref):
    o_ref[0] = _dot(_silu(cv_ref[...]), w_ref[0], HI) + b_ref[0]


def _mod_call(cv, w_mod, b_mod):
    depth, d, n = w_mod.shape
    tn = 1024
    return pl.pallas_call(
        _mod_body,
        grid=(depth, n // tn),
        in_specs=[pl.BlockSpec((8, d), lambda l, j: (0, 0)),
                  pl.BlockSpec((1, d, tn), lambda l, j: (l, 0, j)),
                  pl.BlockSpec((1, 1, tn), lambda l, j: (l, 0, j))],
        out_specs=pl.BlockSpec((1, 8, tn), lambda l, j: (l, 0, j)),
        out_shape=jax.ShapeDtypeStruct((depth, 8, n), F32),
        compiler_params=_cparams("arbitrary", "arbitrary"),
        name="adaln_mod",
    )(cv, w_mod, b_mod.reshape(depth, 1, n))


def _inproj_project(x_ref, g_ref, mod_ref, w_ref, z_ref, xbc_ref, u_ref, dt_ref, qkv_ref):
    x = x_ref[...]
    mod = mod_ref[0]
    ms = jnp.mean(x * x, axis=-1, keepdims=True)
    h = x * lax.rsqrt(ms + EPS) * g_ref[...] * (1.0 + mod[1:2]) + mod[0:1]
    p = _dot(h.astype(BF16), w_ref[...])
    z_ref[...] = p[:, 0:256]
    xbc_ref[...] = p[:, 256:768]
    u_ref[...] = p[:, 768:1024]
    dt_ref[...] = p[:, 2560:2688]
    qkv_ref[...] = p[:, 1024:2560]


def _inproj_attn_operands(qkv_ref, gq_ref, gk_ref, gm_ref, cos_ref, sa_ref, sb_ref, q_ref, k_ref, v_ref):
    p = qkv_ref[...]
    rows = p.shape[0]
    gm = gm_ref[...]
    cos = cos_ref[...]
    sa = sa_ref[...]
    sb = sb_ref[...]

    def norm_rope(t, gain, scale):
        ms32 = _dot((t * t).astype(BF16), gm)
        tn = t * lax.rsqrt(ms32 + EPS) * gain
        outs = []
        for j in range(DA_WIDTH // LANES):
            tb = tn[:, j * LANES:(j + 1) * LANES]
            ob = tb * cos + pltpu.roll(tb, LANES - ROPE_FREQS, 1) * sa + pltpu.roll(tb, ROPE_FREQS, 1) * sb
            outs.append(ob * scale)
        return jnp.concatenate(outs, axis=1)

    qn = norm_rope(p[:, 0:512], gq_ref[...], LOG2E * DA_HEAD_DIM ** -0.5).astype(BF16)
    kt = norm_rope(p[:, 512:1024], gk_ref[...], 1.0).T.astype(BF16)
    vv = p[:, 1024:1536].astype(BF16)
    lane = lax.broadcasted_iota(jnp.int32, (rows, LANES - DA_V_DIM), 1)
    ones_col = jnp.where(lane == 0, 1.0, 0.0).astype(BF16)
    zeros_k = jnp.zeros((DA_HEAD_DIM, rows), BF16)
    for hd in range(DA_HEADS):
        lo_, mid, hi_ = hd * DA_V_DIM, hd * DA_V_DIM + DA_HEAD_DIM, (hd + 1) * DA_V_DIM
        q_ref[hd] = qn[:, lo_:hi_]
        k_ref[hd, 0, 0, 0:DA_HEAD_DIM, :] = kt[lo_:mid]
        k_ref[hd, 0, 0, DA_HEAD_DIM:, :] = zeros_k
        k_ref[hd, 0, 1, 0:DA_HEAD_DIM, :] = zeros_k
        k_ref[hd, 0, 1, DA_HEAD_DIM:, :] = kt[mid:hi_]
        v_ref[hd, 0] = jnp.concatenate([vv[:, lo_:hi_], ones_col], axis=1)


def _inproj_body(x_ref, g_ref, mod_ref, w_ref, gq_ref, gk_ref, gm_ref, cos_ref, sa_ref, sb_ref,
                 z_ref, xbc_ref, u_ref, dt_ref, q_ref, k_ref, v_ref, qkv_a_ref, qkv_b_ref):
    i = pl.program_id(0)

    @pl.when(i == 0)
    def _():
        qkv_b_ref[...] = jnp.zeros_like(qkv_b_ref)

    def step(cur_ref, prev_ref):
        _inproj_attn_operands(prev_ref, gq_ref, gk_ref, gm_ref, cos_ref, sa_ref, sb_ref, q_ref, k_ref, v_ref)
        _inproj_project(x_ref, g_ref, mod_ref, w_ref, z_ref, xbc_ref, u_ref, dt_ref, cur_ref)

    @pl.when(i % 2 == 0)
    def _():
        step(qkv_a_ref, qkv_b_ref)

    @pl.when(i % 2 == 1)
    def _():
        step(qkv_b_ref, qkv_a_ref)


def _inproj_call(xs, g1, modl, w_r, gq, gk, gmat, cos_t, sa_t, sb_t, n_lat_tiles, tk):
    t, d = xs.shape
    tm = ROW_TILE
    n_tiles = t // tm
    per = tk // tm
    row = lambda i: (jnp.minimum(i, n_tiles - 1), 0)
    done = lambda i: (jnp.maximum(i - 1, 0), 0)
    const = lambda i: (0, 0)
    seg = lambda i: ((jnp.minimum(i, n_tiles - 1) >= n_lat_tiles).astype(jnp.int32), 0, 0)
    flat = [(256, F32), (512, F32), (256, F32), (LANES, F32)]

    def k_block(i):
        j = jnp.maximum(i - 1, 0)
        return (0, j // per, 0, 0, j % per)

    def v_block(i):
        j = jnp.maximum(i - 1, 0)
        return (0, j // per, j % per, 0)

    return pl.pallas_call(
        _inproj_body,
        grid=(n_tiles + 1,),
        in_specs=[pl.BlockSpec((tm, d), row),
                  pl.BlockSpec((1, d), const),
                  pl.BlockSpec((1, 8, d), seg),
                  pl.BlockSpec((d, IN_PAD), const),
                  pl.BlockSpec((1, DA_WIDTH), const),
                  pl.BlockSpec((1, DA_WIDTH), const),
                  pl.BlockSpec((DA_WIDTH, DA_WIDTH), const),
                  pl.BlockSpec((tm, LANES), done),
                  pl.BlockSpec((tm, LANES), done),
                  pl.BlockSpec((tm, LANES), done)],
        out_specs=[pl.BlockSpec((tm, w), row) for w, _ in flat]
                  + [pl.BlockSpec((DA_HEADS, tm, 2 * DA_HEAD_DIM), lambda i: (0, jnp.maximum(i - 1, 0), 0)),
                     pl.BlockSpec((DA_HEADS, 1, 2, 2 * DA_HEAD_DIM, tm), k_block),
                     pl.BlockSpec((DA_HEADS, 1, tm, LANES), v_block)],
        out_shape=[jax.ShapeDtypeStruct((t, w), dt) for w, dt in flat]
                  + [jax.ShapeDtypeStruct((DA_HEADS, t, 2 * DA_HEAD_DIM), BF16),
                     jax.ShapeDtypeStruct((DA_HEADS, t // tk, 2, 2 * DA_HEAD_DIM, tk), BF16),
                     jax.ShapeDtypeStruct((DA_HEADS, t // tk, tk, LANES), BF16)],
        scratch_shapes=[pltpu.VMEM((tm, 3 * DA_WIDTH), F32)] * 2,
        compiler_params=_cparams("arbitrary"),
        name="in_proj",
    )(xs, g1, modl, w_r, gq, gk, gmat, cos_t, sa_t, sb_t)


def _softplus(x):
    return jnp.maximum(x, 0.0) + jnp.log1p(jnp.exp(-jnp.abs(x)))


def _ssd_block(c, n_lat, n_blocks, reverse, xc_ref, xp_ref, xn_ref, dtc_ref, dtr_ref, cw_ref, cb_ref,
               bias_c_ref, a_c_ref, bias_r_ref, a_r_ref, st_ref):
    L = SSD_CHUNK
    rows = xc_ref.shape[0]
    x = xc_ref[...]
    seg_first = jnp.logical_or(c == 0, c == n_lat)
    seg_last = jnp.logical_or(c == n_lat - 1, c == n_blocks - 1)
    prow = jnp.where(seg_first, 0.0, xp_ref[7:8, :])
    nrow = jnp.where(seg_last, 0.0, xn_ref[0:1, :])
    ridx = lax.broadcasted_iota(jnp.int32, (rows, 1), 0)
    xprev = jnp.where(ridx == 0, prow, pltpu.roll(x, 1, 0))
    xnext = jnp.where(ridx == rows - 1, nrow, pltpu.roll(x, rows - 1, 0))
    cw = cw_ref[...]
    conv = xprev * cw[0:1] + x * cw[1:2] + xnext * cw[2:3] + cb_ref[...]
    xbc = _silu(conv)

    dt_c = _softplus(dtc_ref[...] + bias_c_ref[...])
    dt_r = _softplus(dtr_ref[...] + bias_r_ref[...])
    adt_c = dt_c * a_c_ref[...]
    adt_r = dt_r * a_r_ref[...]

    li = lax.broadcasted_iota(jnp.int32, (L, L), 0)
    si = lax.broadcasted_iota(jnp.int32, (L, L), 1)
    mask = (li <= si) if reverse else (li >= si)
    mask_t = (li >= si) if reverse else (li <= si)
    n_sub = rows // L
    yield
    parts = yield from _interleave(*[
        _ssd_chunk_local(xbc[s * L:(s + 1) * L], dt_c[s * L:(s + 1) * L], adt_c[s * L:(s + 1) * L],
                         adt_r[:, s * L:(s + 1) * L], mask, mask_t, reverse) for s in range(n_sub)])
    ys = [None] * n_sub
    for s in (reversed(range(n_sub)) if reverse else range(n_sub)):
        ys[s] = _ssd_chunk_state(*parts[s], st_ref, reverse)
        yield
    return jnp.concatenate(ys, axis=0), xbc[:, :SSD_INNER]


def _split3(x):
    hi = x.astype(BF16)
    r1 = x - hi.astype(F32)
    mid = r1.astype(BF16)
    lo = (r1 - mid.astype(F32)).astype(BF16)
    return hi, mid, lo


def _ssd_chunk_local(xbc, dt_c, adt_c, adt_r, mask, mask_t, reverse):
    L = SSD_CHUNK
    hpg = SSD_HEADS // SSD_GROUPS
    off = SSD_HEADS if reverse else 0
    m_b, mt_b = mask.astype(BF16), mask_t.astype(BF16)
    cs_c = sum(_dot(m_b, part) for part in _split3(adt_c))
    cs_r = sum(_dot(part, mt_b) for part in _split3(adt_r))
    edge = 0 if reverse else L - 1
    tot = cs_c[edge:edge + 1, :]
    yield

    kk = lax.broadcasted_iota(jnp.int32, (LANES, SSD_INNER), 0)
    jj = lax.broadcasted_iota(jnp.int32, (LANES, SSD_INNER), 1)
    spread = (kk == off + jj // SSD_HEAD_DIM).astype(BF16)
    cols = jnp.concatenate([dt_c, jnp.exp(cs_c), jnp.exp(tot - cs_c)], axis=0)
    c_hi = cols.astype(BF16)
    c_lo = (cols - c_hi.astype(F32)).astype(BF16)
    wide = _dot(c_hi, spread) + _dot(c_lo, spread)
    dt_w, ecs_w, dst_w = wide[0:L], wide[L:2 * L], wide[2 * L:3 * L]
    yield

    xs = xbc[:, :SSD_INNER]
    b_all = xbc[:, SSD_INNER:SSD_INNER + LANES]
    c_all = xbc[:, SSD_INNER + LANES:SSD_INNER + 2 * LANES]
    xd = xs * dt_w
    xd_b = xd.astype(BF16)
    b_b = b_all.astype(BF16)
    c_b = c_all.astype(BF16)
    lane = lax.broadcasted_iota(jnp.int32, (L, LANES), 1)
    col = lax.broadcasted_iota(jnp.int32, (L, SSD_INNER), 1)

    y = None
    for g in range(SSD_GROUPS):
        c_g = jnp.where(lane // SSD_STATE == g, c_b, jnp.zeros_like(c_b))
        gmat = _dot_nt(c_g, b_b)
        for hh in range(hpg):
            h = g * hpg + hh
            dec = jnp.exp(jnp.where(mask, cs_c[:, off + h:off + h + 1] - cs_r[off + h:off + h + 1, :], -jnp.inf))
            xd_h = jnp.where(col // SSD_HEAD_DIM == h, xd_b, jnp.zeros_like(xd_b))
            part = _dot((gmat * dec).astype(BF16), xd_h)
            y = part if y is None else y + part
            yield
    return y, c_b, ecs_w, b_all.T.astype(BF16), (xd * dst_w).astype(BF16)


def _ssd_chunk_state(y_local, c_b, ecs_w, bt_b, xds_b, st_ref, reverse):
    edge = 0 if reverse else SSD_CHUNK - 1
    y = y_local + ecs_w * _dot(c_b, st_ref[...].astype(BF16))
    kr = lax.broadcasted_iota(jnp.int32, (LANES, SSD_INNER), 0)
    kc = lax.broadcasted_iota(jnp.int32, (LANES, SSD_INNER), 1)
    block = kr // SSD_STATE == kc // (SSD_HEADS // SSD_GROUPS * SSD_HEAD_DIM)
    st_ref[...] = st_ref[...] * ecs_w[edge:edge + 1, :] + jnp.where(block, _dot(bt_b, xds_b), 0.0)
    return y


def _ssd_body(n_lat, n_chunks, *refs):
    fwd_in, bwd_in = refs[0:5], refs[5:10]
    cw_ref, cb_ref, bias_c_ref, a_c_ref, bias_r_ref, a_r_ref, dsk_ref, yf_ref, yb_ref, st_ref = refs[10:]
    i = pl.program_id(0)

    @pl.when(i == 0)
    def _():
        st_ref[...] = jnp.zeros_like(st_ref)

    shared = (cw_ref, cb_ref, bias_c_ref, a_c_ref, bias_r_ref, a_r_ref)
    (y_f, xs), (y_b, _) = _run(_interleave(
        _ssd_block(_segment_order(i, n_lat, n_chunks, False), n_lat, n_chunks, False, *fwd_in, *shared,
                   st_ref.at[0]),
        _ssd_block(_segment_order(i, n_lat, n_chunks, True), n_lat, n_chunks, True, *bwd_in, *shared,
                   st_ref.at[1])))
    yf_ref[...] = y_f + dsk_ref[...] * xs
    yb_ref[...] = y_b


def _ssd_call(xbc_raw, dt_c, dt_r, cw, cb, bias_c, a_c, bias_r, a_r, dsk, n_lat_rows):
    t = xbc_raw.shape[0]
    L = SSD_STEP_CHUNKS * SSD_CHUNK
    assert t % L == 0 and n_lat_rows % L == 0
    n_chunks = t // L
    n_lat = n_lat_rows // L
    sub = L // 8
    n_sub = t // 8
    const = lambda i: (0, 0)

    def chunk_specs(reverse):
        cidx = functools.partial(_segment_order, n_lat=n_lat, n_all=n_chunks, reverse=reverse)
        return [pl.BlockSpec((L, SSD_CONV_DIM), lambda i: (cidx(i), 0)),
                pl.BlockSpec((8, SSD_CONV_DIM), lambda i: (jnp.maximum(cidx(i) * sub - 1, 0), 0)),
                pl.BlockSpec((8, SSD_CONV_DIM), lambda i: (jnp.minimum((cidx(i) + 1) * sub, n_sub - 1), 0)),
                pl.BlockSpec((L, LANES), lambda i: (cidx(i), 0)),
                pl.BlockSpec((8, L), lambda i: (0, cidx(i)))]

    def out_spec(reverse):
        cidx = functools.partial(_segment_order, n_lat=n_lat, n_all=n_chunks, reverse=reverse)
        return pl.BlockSpec((L, SSD_INNER), lambda i: (cidx(i), 0))

    chunk_args = (xbc_raw, xbc_raw, xbc_raw, dt_c, dt_r)
    return pl.pallas_call(
        functools.partial(_ssd_body, n_lat, n_chunks),
        grid=(n_chunks,),
        in_specs=chunk_specs(False) + chunk_specs(True)
                 + [pl.BlockSpec((8, SSD_CONV_DIM), const),
                    pl.BlockSpec((1, SSD_CONV_DIM), const),
                    pl.BlockSpec((1, LANES), const),
                    pl.BlockSpec((1, LANES), const),
                    pl.BlockSpec((8, L), const),
                    pl.BlockSpec((8, L), const),
                    pl.BlockSpec((1, SSD_INNER), const)],
        out_specs=[out_spec(False), out_spec(True)],
        out_shape=[jax.ShapeDtypeStruct((t, SSD_INNER), F32)] * 2,
        scratch_shapes=[pltpu.VMEM((2, SSD_GROUPS * SSD_STATE, SSD_INNER), F32)],
        compiler_params=_cparams("arbitrary"),
        name="ssd_scan",
    )(*chunk_args, *chunk_args, cw, cb, bias_c, a_c, bias_r, a_r, dsk)


def _cmul(ar, ai, br, bi):
    return ar * br - ai * bi, ar * bi + ai * br


def _s5_step_pair(u_ref, a):
    return jnp.concatenate([u_ref[:, 2 * a, :], u_ref[:, 2 * a + 1, :]], axis=1).astype(BF16)


def _s5_in_body(u_ref, w_ref, lam_ref, efr_ref, efi_ref, ebr_ref, ebi_ref, wp_ref):
    hs = S5_HALF_STATE

    @pl.when(pl.program_id(1) == 0)
    def _():
        lam = lam_ref[0]
        wc = w_ref[0]
        rows = lambda a: jnp.broadcast_to(a, (LANES, hs))
        fr, fi, br, bi = rows(lam[0:1]), rows(lam[1:2]), rows(lam[2:3]), rows(lam[3:4])
        rgrp = lax.broadcasted_iota(jnp.int32, (LANES, LANES), 0) // S5_GROUP
        lgrp = lax.broadcasted_iota(jnp.int32, (LANES, LANES), 1) // S5_STATE
        planes = []
        for q in range(4):
            cq = wc[:, q * S5_STATE:(q + 1) * S5_STATE]
            cq2 = jnp.concatenate([cq, cq], axis=1)
            planes.append(jnp.concatenate(
                [jnp.where(rgrp == 2 * g4 + lgrp, cq2, 0.0) for g4 in range(S5_HALF_GROUPS // 2)], axis=1))
        ar, ai = planes[0], planes[1]
        for i in reversed(range(S5_CHUNK)):
            wp_ref[i * LANES:(i + 1) * LANES, 0:hs] = ar.astype(BF16)
            wp_ref[i * LANES:(i + 1) * LANES, hs:2 * hs] = ai.astype(BF16)
            ar, ai = _cmul(ar, ai, fr, fi)
        ar, ai = planes[2], planes[3]
        for i in range(S5_CHUNK):
            wp_ref[i * LANES:(i + 1) * LANES, 2 * hs:3 * hs] = ar.astype(BF16)
            wp_ref[i * LANES:(i + 1) * LANES, 3 * hs:4 * hs] = ai.astype(BF16)
            ar, ai = _cmul(ar, ai, br, bi)

    e = None
    for a in range(S5_CHUNK // 2):
        part = _dot(_s5_step_pair(u_ref, a), wp_ref[2 * a * LANES:2 * (a + 1) * LANES, :])
        e = part if e is None else e + part
    efr_ref[...] = e[:, 0:hs]
    efi_ref[...] = e[:, hs:2 * hs]
    ebr_ref[...] = e[:, 2 * hs:3 * hs]
    ebi_ref[...] = e[:, 3 * hs:4 * hs]


def _s5_in_call(u3, w_b, lam_rows):
    nch = u3.shape[0]
    tn = _pick_tile(nch, (208, 80, 40))
    hs = S5_HALF_STATE
    col = lambda h, n: (n, h)
    return pl.pallas_call(
        _s5_in_body,
        grid=(2, nch // tn),
        in_specs=[pl.BlockSpec((tn, S5_CHUNK, LANES), lambda h, n: (n, 0, h)),
                  pl.BlockSpec((1, LANES, 4 * S5_STATE), lambda h, n: (h, 0, 0)),
                  pl.BlockSpec((1, 4, hs), lambda h, n: (h, 0, 0))],
        out_specs=[pl.BlockSpec((tn, hs), col)] * 4,
        out_shape=[jax.ShapeDtypeStruct((nch, 2 * hs), F32)] * 4,
        scratch_shapes=[pltpu.VMEM((S5_CHUNK * LANES, 4 * hs), BF16)],
        compiler_params=_cparams("arbitrary", "arbitrary"),
        name="s5_chunk_in",
    )(u3, w_b, lam_rows)


def _s5_scan_body(a_ref, efr_ref, efi_ref, ebr_ref, ebi_ref, hfr_ref, hfi_ref, hbr_ref, hbi_ref, st_ref):
    @pl.when(pl.program_id(0) == 0)
    def _():
        st_ref[...] = jnp.zeros_like(st_ref)

    row = lambda ref, j: ref[j:j + 1, :]
    afr, afi, abr, abi = row(a_ref, 0), row(a_ref, 1), row(a_ref, 2), row(a_ref, 3)
    fr, fi, br, bi = row(st_ref, 0), row(st_ref, 1), row(st_ref, 2), row(st_ref, 3)
    for j in range(S5_SCAN_TILE):
        hfr_ref[j:j + 1, :] = fr
        hfi_ref[j:j + 1, :] = fi
        fr, fi = afr * fr - afi * fi + row(efr_ref, j), afr * fi + afi * fr + row(efi_ref, j)
        jb = S5_SCAN_TILE - 1 - j
        hbr_ref[jb:jb + 1, :] = br
        hbi_ref[jb:jb + 1, :] = bi
        br, bi = abr * br - abi * bi + row(ebr_ref, jb), abr * bi + abi * br + row(ebi_ref, jb)
    st_ref[0:1, :] = fr
    st_ref[1:2, :] = fi
    st_ref[2:3, :] = br
    st_ref[3:4, :] = bi


def _s5_scan_call(a_pow, efr, efi, ebr, ebi, n_lat_tiles):
    nch, width = efr.shape
    n_tiles = nch // S5_SCAN_TILE
    blk = (S5_SCAN_TILE, width)
    fwd = lambda i: (_segment_order(i, n_lat_tiles, n_tiles, False), 0)
    bwd = lambda i: (_segment_order(i, n_lat_tiles, n_tiles, True), 0)
    shp = jax.ShapeDtypeStruct((nch, width), F32)
    return pl.pallas_call(
        _s5_scan_body,
        grid=(n_tiles,),
        in_specs=[pl.BlockSpec((4, width), lambda i: (0, 0)),
                  pl.BlockSpec(blk, fwd), pl.BlockSpec(blk, fwd),
                  pl.BlockSpec(blk, bwd), pl.BlockSpec(blk, bwd)],
        out_specs=[pl.BlockSpec(blk, fwd), pl.BlockSpec(blk, fwd),
                   pl.BlockSpec(blk, bwd), pl.BlockSpec(blk, bwd)],
        out_shape=[shp] * 4,
        scratch_shapes=[pltpu.VMEM((4, width), F32)],
        compiler_params=_cparams("arbitrary"),
        name="s5_state_scan",
    )(a_pow, efr, efi, ebr, ebi)


def _s5_out_body(u_ref, hfr_ref, hfi_ref, hbr_ref, hbi_ref, k_ref, t_ref, c_ref, lam_ref, y_ref, wq_ref, z_ref):
    hs = S5_HALF_STATE
    width = S5_CHUNK * LANES

    @pl.when(pl.program_id(1) == 0)
    def _():
        strip = _dot(k_ref[0].astype(BF16), t_ref[...])
        rgrp = lax.broadcasted_iota(jnp.int32, strip.shape, 0) // S5_GROUP
        cgrp = (lax.broadcasted_iota(jnp.int32, strip.shape, 1) % LANES) // S5_GROUP
        strip = jnp.where(rgrp == cgrp, strip, 0.0).astype(BF16)
        z_ref[0:LANES, :] = strip[:, LANES:]
        z_ref[LANES:, :] = strip[:, :-LANES]

        cc = c_ref[0]
        ogrp = lax.broadcasted_iota(jnp.int32, (S5_STATE, LANES), 1) // S5_GROUP

        def expand(q):
            cq = cc[q * S5_STATE:(q + 1) * S5_STATE]
            return jnp.concatenate([jnp.where(ogrp == g, cq, 0.0) for g in range(S5_HALF_GROUPS)], axis=0)

        lam = lam_ref[0]
        for d, order in ((0, range(S5_CHUNK)), (1, reversed(range(S5_CHUNK)))):
            lr, li = lam[2 * d], lam[2 * d + 1]
            a, b = expand(2 * d), expand(2 * d + 1)
            for j in order:
                a, b = a * lr + b * li, b * lr - a * li
                wq_ref[2 * d * hs:(2 * d + 1) * hs, j * LANES:(j + 1) * LANES] = a.astype(BF16)
                wq_ref[(2 * d + 1) * hs:(2 * d + 2) * hs, j * LANES:(j + 1) * LANES] = b.astype(BF16)

    h_all = jnp.concatenate([hfr_ref[...], hfi_ref[...], hbr_ref[...], hbi_ref[...]], axis=1).astype(BF16)
    y = _dot(h_all, wq_ref[...])
    for a in range(S5_CHUNK // 2):
        lo = (S5_CHUNK - 2 - 2 * a) * LANES
        y += _dot(_s5_step_pair(u_ref, a), z_ref[:, lo:lo + width])
    for j in range(S5_CHUNK):
        y_ref[:, j, :] = y[:, j * LANES:(j + 1) * LANES]


def _s5_out_call(u3, hfr, hfi, hbr, hbi, w_k, w_t, w_c, lam_cols):
    nch = u3.shape[0]
    tn = _pick_tile(nch, (208, 80, 40))
    hs = S5_HALF_STATE
    col = lambda h, n: (n, h)
    blk3 = pl.BlockSpec((tn, S5_CHUNK, LANES), lambda h, n: (n, 0, h))
    return pl.pallas_call(
        _s5_out_body,
        grid=(2, nch // tn),
        in_specs=[blk3] + [pl.BlockSpec((tn, hs), col)] * 4
                 + [pl.BlockSpec((1, LANES) + w_k.shape[2:], lambda h, n: (h, 0, 0)),
                    pl.BlockSpec(w_t.shape, lambda h, n: (0, 0)),
                    pl.BlockSpec((1, 4 * S5_STATE, LANES), lambda h, n: (h, 0, 0)),
                    pl.BlockSpec((1, 4, hs, LANES), lambda h, n: (h, 0, 0, 0))],
        out_specs=blk3,
        out_shape=jax.ShapeDtypeStruct(u3.shape, F32),
        scratch_shapes=[pltpu.VMEM((4 * hs, S5_CHUNK * LANES), BF16),
                        pltpu.VMEM((2 * LANES, (S5_TOEPLITZ_BLOCKS - 1) * LANES), BF16)],
        compiler_params=_cparams("arbitrary", "arbitrary"),
        name="s5_chunk_out",
    )(u3, hfr, hfi, hbr, hbi, w_k, w_t, w_c, lam_cols)


def _s5_weights(lam_re, lam_im, log_step, b_re, b_im, c_re, c_im):
    tc = S5_CHUNK
    nl = lam_re.shape[0]
    hg = S5_HALF_GROUPS
    ein = functools.partial(jnp.einsum, precision=HI)
    step = jnp.exp(log_step)[..., None]
    er, ei = lam_re * step, lam_im * step
    mag = jnp.exp(er)
    lbr, lbi = mag * jnp.cos(ei), mag * jnp.sin(ei)
    den = lam_re * lam_re + lam_im * lam_im
    nr, ni = lbr - 1.0, lbi
    cr, ci = (nr * lam_re + ni * lam_im) / den, (ni * lam_re - nr * lam_im) / den
    bbr = cr[..., None] * b_re - ci[..., None] * b_im
    bbi = cr[..., None] * b_im + ci[..., None] * b_re
    taus = jnp.arange(tc + 1, dtype=F32)[:, None, None]
    pmag = jnp.exp(er[:, :, None] * taus)
    pr, pi = pmag * jnp.cos(ei[:, :, None] * taus), pmag * jnp.sin(ei[:, :, None] * taus)

    wr = pr[:, :, :tc, :, :, None] * bbr[:, :, None] - pi[:, :, :tc, :, :, None] * bbi[:, :, None]
    wi = pr[:, :, :tc, :, :, None] * bbi[:, :, None] + pi[:, :, :tc, :, :, None] * bbr[:, :, None]
    kk = ein('ldgcp,ldtgpk->ldtgck', c_re, wr) - ein('ldgcp,ldtgpk->ldtgck', c_im, wi)
    kf, kb = kk[:, 0], kk[:, 1]
    seq = jnp.concatenate([kb[:, :0:-1], kf[:, :1] + kb[:, :1], kf[:, 1:]], axis=1)
    seq = seq.reshape(nl, S5_TOEPLITZ_BLOCKS, 2, hg, S5_GROUP, S5_GROUP)
    w_k = seq.transpose(0, 2, 3, 5, 1, 4).reshape(nl, 2, LANES, S5_TOEPLITZ_BLOCKS * S5_GROUP)
    w_k = jnp.pad(w_k, ((0, 0), (0, 0), (0, 0), (0, S5_KCOLS - S5_TOEPLITZ_BLOCKS * S5_GROUP)))
    r = jnp.arange(S5_KCOLS)
    s = jnp.arange(S5_TOEPLITZ_BLOCKS * LANES)
    w_t = ((r[:, None] // S5_GROUP == s[None, :] // LANES)
           & (r[:, None] % S5_GROUP == s[None, :] % S5_GROUP)).astype(BF16)

    planes_b = jnp.stack([bbr[:, 0], bbi[:, 0], bbr[:, 1], bbi[:, 1]], axis=1)
    planes_b = planes_b.reshape(nl, 4, 2, hg, S5_STATE, S5_GROUP)
    w_b = planes_b.transpose(0, 2, 3, 5, 1, 4).reshape(nl, 2, LANES, 4 * S5_STATE)
    planes_c = jnp.stack([c_re[:, 0], -c_im[:, 0], c_re[:, 1], -c_im[:, 1]], axis=1)
    planes_c = planes_c.reshape(nl, 4, 2, hg, S5_GROUP, S5_STATE)
    w_c = planes_c.transpose(0, 2, 1, 5, 3, 4).reshape(nl, 2, 4 * S5_STATE, LANES)

    halves = lambda a: a.reshape(nl, 2, S5_HALF_STATE)
    lam_rows = jnp.stack([halves(lbr[:, 0]), halves(lbi[:, 0]), halves(lbr[:, 1]), halves(lbi[:, 1])], axis=2)
    lam_cols = jnp.broadcast_to(lam_rows[..., None], lam_rows.shape + (LANES,))
    full = lambda a: a.reshape(nl, S5_GROUPS * S5_STATE)
    a_pow = jnp.stack([full(pr[:, 0, tc]), full(pi[:, 0, tc]), full(pr[:, 1, tc]), full(pi[:, 1, tc])], axis=1)
    return (w_b, w_k, w_c, lam_rows, lam_cols, a_pow), w_t


def _s5_mix(u, w_b, w_k, w_c, lam_rows, lam_cols, a_pow, w_t, n_lat):
    t = u.shape[0]
    nch = t // S5_CHUNK
    u3 = u.reshape(nch, S5_CHUNK, S5_WIDTH)
    es = _s5_in_call(u3, w_b, lam_rows)
    hs = _s5_scan_call(a_pow, *es, n_lat // (S5_CHUNK * S5_SCAN_TILE))
    return _s5_out_call(u3, *hs, w_k, w_t, w_c, lam_cols).reshape(t, S5_WIDTH)


def _attn_body(flag_ref, lam_ref, q_ref, k_ref, v_ref, g_ref, o_ref, acc_ref, m_ref, oh_ref, *, n_kv, post_scale):
    bounded = flag_ref[0] == 1

    def head(hh, carry):
        q = q_ref[hh]
        acc_ref[...] = jnp.zeros_like(acc_ref)

        @pl.when(bounded)
        def _():
            def step(b, carry2):
                v = v_ref[hh, b]
                for c in range(2):
                    p = jnp.exp2(_dot(q, k_ref[hh, b, c])).astype(BF16)
                    acc_ref[c] += _dot(p, v)
                return carry2
            lax.fori_loop(0, n_kv, step, 0)

        @pl.when(jnp.logical_not(bounded))
        def _():
            m_ref[...] = jnp.full_like(m_ref, -jnp.inf)

            def step(b, carry2):
                v = v_ref[hh, b]
                for c in range(2):
                    s = _dot(q, k_ref[hh, b, c])
                    m_old = m_ref[c]
                    m_new = jnp.maximum(m_old, jnp.max(s, axis=1, keepdims=True))
                    p = jnp.exp2(s - m_new[:, 0:1]).astype(BF16)
                    acc_ref[c] = jnp.exp2(m_old - m_new) * acc_ref[c] + _dot(p, v)
                    m_ref[c] = m_new
                return carry2
            lax.fori_loop(0, n_kv, step, 0)

        a0 = acc_ref[0]
        a1 = acc_ref[1]
        o = a0[:, :DA_V_DIM] / a0[:, DA_V_DIM:DA_V_DIM + 1] \
            - lam_ref[0] * (a1[:, :DA_V_DIM] / a1[:, DA_V_DIM:DA_V_DIM + 1])
        ms = jnp.mean(o * o, axis=-1, keepdims=True)
        oh_ref[hh] = o * lax.rsqrt(ms + EPS) * g_ref[...] * post_scale
        return carry

    lax.fori_loop(0, ATTN_HEADS_PER_STEP, head, 0)
    o_ref[...] = jnp.concatenate([oh_ref[hh] for hh in range(ATTN_HEADS_PER_STEP)], axis=1)


def _attn_body_keep_rest(flag_ref, lam_ref, q_ref, k_ref, v_ref, g_ref, rest_ref, *refs, **kw):
    del rest_ref
    _attn_body(flag_ref, lam_ref, q_ref, k_ref, v_ref, g_ref, *refs, **kw)


def _attn_call(flag, lam, qh, kzb, vb, g_sub, post_scale, q_rows, q_row0, kv_block, kv_cols, kv_col0, y_rest=None):
    nh = qh.shape[0]
    hp = ATTN_HEADS_PER_STEP
    tk = kzb.shape[4]
    tq = _pick_tile(q_rows, (ATTN_TQ, 256))
    assert q_row0 % tq == 0
    q0 = q_row0 // tq
    if kv_block is None:
        n_kv = kzb.shape[1]
        one = pl.Buffered(1)
        k_spec = pl.BlockSpec((hp, n_kv, 2, 2 * DA_HEAD_DIM, tk), lambda h, i: (h, 0, 0, 0, 0), pipeline_mode=one)
        v_spec = pl.BlockSpec((hp, n_kv, tk, LANES), lambda h, i: (h, 0, 0, 0), pipeline_mode=one)
    else:
        assert kv_col0 % kv_cols == 0 and tk % kv_cols == 0
        n_kv, cb = 1, kv_col0 // kv_cols
        k_spec = pl.BlockSpec((hp, 1, 2, 2 * DA_HEAD_DIM, kv_cols), lambda h, i: (h, kv_block, 0, 0, cb))
        v_spec = pl.BlockSpec((hp, 1, kv_cols, LANES), lambda h, i: (h, kv_block, cb, 0))
    smem = pl.BlockSpec(memory_space=pltpu.SMEM)
    in_specs = [smem, smem,
                pl.BlockSpec((hp, tq, 2 * DA_HEAD_DIM), lambda h, i: (h, q0 + i, 0)),
                k_spec, v_spec,
                pl.BlockSpec((1, DA_V_DIM), lambda h, i: (0, 0))]
    args = (flag, lam, qh, kzb, vb, g_sub)
    body = functools.partial(_attn_body, n_kv=n_kv, post_scale=post_scale)
    aliases = {}
    if y_rest is not None:
        in_specs.append(pl.BlockSpec(memory_space=pl.ANY))
        args += (y_rest,)
        body = functools.partial(_attn_body_keep_rest, n_kv=n_kv, post_scale=post_scale)
        aliases = {len(args) - 1: 0}
    return pl.pallas_call(
        body,
        grid=(nh // hp, q_rows // tq),
        in_specs=in_specs,
        out_specs=pl.BlockSpec((tq, hp * DA_V_DIM), lambda h, i: (q0 + i, h)),
        out_shape=jax.ShapeDtypeStruct((qh.shape[1], nh * DA_V_DIM), F32),
        scratch_shapes=[pltpu.VMEM((2, tq, LANES), F32), pltpu.VMEM((2, tq, LANES), F32),
                        pltpu.VMEM((hp, tq, DA_V_DIM), F32)],
        input_output_aliases=aliases,
        compiler_params=_cparams("arbitrary", "arbitrary"),
        name="diff_attn",
    )(*args)


def _gelu_tanh(x):
    return 0.5 * x * (1.0 + jnp.tanh(math.sqrt(2.0 / math.pi) * (x + 0.044715 * (x * x * x))))


def _outffn_body(x_ref, yf_ref, yb_ref, z_ref, gssd_ref, ys5_ref, u_ref, d5_ref, gw_ref, gb_ref,
                 yda_ref, mod_ref, modp_ref, g2_ref, wo_ref, w1_ref, w3_ref, w2_ref, o_ref, xm_ref, hb_ref):
    i = pl.program_id(0)
    cur = i % 2
    prev = 1 - cur

    @pl.when(i == 0)
    def _():
        xm_ref[...] = jnp.zeros_like(xm_ref)
        hb_ref[...] = jnp.zeros_like(hb_ref)

    hp = hb_ref[prev]
    f = _silu(_dot(hp, w1_ref[...])) * _dot(hp, w3_ref[...])
    o_ref[...] = xm_ref[prev] + modp_ref[0][5:6] * _dot(f.astype(BF16), w2_ref[...])

    mod = mod_ref[0]
    y = (yf_ref[...] + yb_ref[...]) * _silu(z_ref[...])
    y_ssd = y * lax.rsqrt(jnp.mean(y * y, axis=-1, keepdims=True) + EPS) * gssd_ref[...]
    y5 = _gelu_tanh(ys5_ref[...] + d5_ref[...] * u_ref[...])
    y5 = y5 * jax.nn.sigmoid(_dot(y5.astype(BF16), gw_ref[...].astype(BF16)) + gb_ref[...])
    wo = wo_ref[...]
    o = _dot(y_ssd.astype(BF16), wo[0:256])
    o += _dot(y5.astype(BF16), wo[256:512])
    o += _dot(yda_ref[...].astype(BF16), wo[512:1024])
    x = x_ref[...] + mod[2:3] * o
    h = x * lax.rsqrt(jnp.mean(x * x, axis=-1, keepdims=True) + EPS) * g2_ref[...] * (1.0 + mod[4:5]) + mod[3:4]
    xm_ref[cur] = x
    hb_ref[cur] = h.astype(BF16)


def _outffn_call(xs, y_f, y_b, z, g_ssd, y_s5, u, d5, glu_w, glu_b, y_da, modl, g2, w_o, w1, w3, w2, layer, n_lat,
                 latent_only):
    t, d = xs.shape
    if latent_only:
        t = n_lat
    dff = w1.shape[2]
    layer_block = lambda i: (layer, 0, 0)
    tm = FFN_ROW_TILE
    assert n_lat % tm == 0 and t % tm == 0
    n_lat_tiles = n_lat // tm
    n_tiles = t // tm
    row = lambda i: (jnp.minimum(i, n_tiles - 1), 0)
    done = lambda i: (jnp.maximum(i - 1, 0), 0)
    const = lambda i: (0, 0)
    seg = lambda i: ((jnp.minimum(i, n_tiles - 1) >= n_lat_tiles).astype(jnp.int32), 0, 0)
    seg_done = lambda i: ((i - 1 >= n_lat_tiles).astype(jnp.int32), 0, 0)
    one = pl.Buffered(1)
    return pl.pallas_call(
        _outffn_body,
        grid=(n_tiles + 1,),
        in_specs=[pl.BlockSpec((tm, d), row),
                  pl.BlockSpec((tm, 256), row), pl.BlockSpec((tm, 256), row), pl.BlockSpec((tm, 256), row),
                  pl.BlockSpec((1, 256), const),
                  pl.BlockSpec((tm, 256), row), pl.BlockSpec((tm, 256), row),
                  pl.BlockSpec((1, 256), const),
                  pl.BlockSpec((256, 256), const), pl.BlockSpec((1, 256), const),
                  pl.BlockSpec((tm, DA_WIDTH), row),
                  pl.BlockSpec((1, 8, d), seg),
                  pl.BlockSpec((1, 8, d), seg_done),
                  pl.BlockSpec((1, d), const),
                  pl.BlockSpec((None, d, d), layer_block, pipeline_mode=one),
                  pl.BlockSpec((None, d, dff), layer_block, pipeline_mode=one),
                  pl.BlockSpec((None, d, dff), layer_block, pipeline_mode=one),
                  pl.BlockSpec((None, dff, d), layer_block, pipeline_mode=one)],
        out_specs=pl.BlockSpec((tm, d), done),
        out_shape=jax.ShapeDtypeStruct((t, d), F32),
        scratch_shapes=[pltpu.VMEM((2, tm, d), F32), pltpu.VMEM((2, tm, d), BF16)],
        compiler_params=_cparams("arbitrary"),
        name="out_proj_ffn",
    )(xs, y_f, y_b, z, g_ssd, y_s5, u, d5, glu_w, glu_b, y_da, modl, modl, g2, w_o, w1, w3, w2)


def _cast_body(w_ref, o_ref):
    o_ref[...] = w_ref[...].astype(o_ref.dtype)


def _to_bf16(w):
    nl, rows, cols = w.shape
    tr = _pick_tile(rows, (512, 704, 256))
    blk = pl.BlockSpec((1, tr, cols), lambda l, r: (l, r, 0))
    return pl.pallas_call(
        _cast_body,
        grid=(nl, rows // tr),
        in_specs=[blk],
        out_specs=blk,
        out_shape=jax.ShapeDtypeStruct(w.shape, BF16),
        compiler_params=_cparams("arbitrary", "arbitrary"),
        name="weights_to_bf16",
    )(w)


def _rope_tables(n_lat, n_ctx):
    pos = jnp.arange(n_lat)
    lane = jnp.arange(LANES)
    d = lane % DA_HEAD_DIM
    axis = d // (2 * ROPE_FREQS)
    half = (d % (2 * ROPE_FREQS)) // ROPE_FREQS
    freqs = ROPE_THETA ** (-jnp.arange(ROPE_FREQS, dtype=F32) / ROPE_FREQS)
    coord = jnp.where(axis[None, :] == 0, (pos // GRID_W)[:, None], (pos % GRID_W)[:, None]).astype(F32)
    ang = coord * freqs[d % ROPE_FREQS][None, :]
    cos, sin = jnp.cos(ang), jnp.sin(ang)
    sa = jnp.where(half[None, :] == 0, -sin, 0.0)
    sb = jnp.where(half[None, :] == 1, sin, 0.0)
    pad = lambda a, v: jnp.concatenate([a, jnp.full((n_ctx, LANES), v, F32)], axis=0)
    return pad(cos, 1.0), pad(sa, 0.0), pad(sb, 0.0)


def _lane_row(vals, width=LANES):
    vals = vals.reshape(-1).astype(F32)
    return jnp.zeros((1, width), F32).at[0, :vals.shape[0]].set(vals)


def kernel(x, c, ctx, c_ctx, w_mod, b_mod, norm1, norm2, w_in, w_out, ssd_conv_w, ssd_conv_b, ssd_a_log, ssd_dt_bias, ssd_d, ssd_norm, s5_lam_re, s5_lam_im, s5_log_step, s5_b_re, s5_b_im, s5_c_re, s5_c_im, s5_d, s5_glu_w, s5_glu_b, da_q_norm, da_k_norm, da_lambda, da_sub_norm, ffn_w1, ffn_w3, ffn_w2):
    depth = w_mod.shape[0]
    bsz, n_lat, d = x.shape
    n_ctx = ctx.shape[1]
    t = n_ctx + n_lat
    tk = _pick_tile(t, (ATTN_TK, 1024, 640, 512, 256))
    assert bsz == 1 and n_ctx % ROW_TILE == 0 and n_lat % ROW_TILE == 0 and tk % ROW_TILE == 0
    assert n_ctx % (S5_CHUNK * S5_SCAN_TILE) == 0 and n_lat % (S5_CHUNK * S5_SCAN_TILE) == 0
    assert n_lat % n_ctx == 0 and tk % n_ctx == 0
    n_lat_tiles = n_lat // ROW_TILE

    xs = jnp.concatenate([x[0], ctx[0]], axis=0)
    cv = jnp.zeros((8, d), F32).at[0].set(c[0]).at[1].set(c_ctx)
    mod_all = _mod_call(cv, w_mod, b_mod)
    cos_t, sa_t, sb_t = _rope_tables(n_lat, n_ctx)
    gidx = jnp.arange(DA_WIDTH) // DA_HEAD_DIM
    gmat = (gidx[:, None] == gidx[None, :]).astype(BF16) * (1.0 / DA_HEAD_DIM)
    s5_w, s5_spread = _s5_weights(s5_lam_re, s5_lam_im, s5_log_step, s5_b_re, s5_b_im, s5_c_re, s5_c_im)
    w_out_b, w1_b, w3_b, w2_b = _to_bf16(w_out), _to_bf16(ffn_w1), _to_bf16(ffn_w3), _to_bf16(ffn_w2)

    for i in range(depth):
        lam_init = 0.8 - 0.6 * math.exp(-0.3 * i)
        modl = jnp.zeros((2, 8, d), F32).at[:, :6].set(mod_all[i, :2].reshape(2, 6, d))

        wi = w_in[i]
        w_r = jnp.concatenate([wi[:, 0:768], wi[:, 776:1032], wi[:, 1032:2568], wi[:, 768:776],
                               jnp.zeros((d, IN_PAD - 2568), F32)], axis=1).astype(BF16)
        gq = jnp.tile(da_q_norm[i], DA_WIDTH // DA_HEAD_DIM)[None, :]
        gk = jnp.tile(da_k_norm[i], DA_WIDTH // DA_HEAD_DIM)[None, :]
        z, xbc_raw, u, dt_raw, qh, kzb, vb = _inproj_call(
            xs, norm1[i][None, :], modl, w_r, gq, gk, gmat, cos_t, sa_t, sb_t, n_lat_tiles, tk)

        dt_r = dt_raw[:, :8].T
        cw = jnp.zeros((8, SSD_CONV_DIM), F32).at[:3].set(ssd_conv_w[i])
        a_neg = -jnp.exp(ssd_a_log[i])
        bias_c = _lane_row(ssd_dt_bias[i])
        a_c = _lane_row(a_neg)
        bias_r = jnp.broadcast_to(ssd_dt_bias[i].reshape(8, 1), (8, SSD_STEP_CHUNKS * SSD_CHUNK))
        a_r = jnp.broadcast_to(a_neg.reshape(8, 1), (8, SSD_STEP_CHUNKS * SSD_CHUNK))
        dsk = jnp.repeat(ssd_d[i], SSD_HEAD_DIM)[None, :]
        y_f, y_b = _ssd_call(xbc_raw, dt_raw, dt_r, cw, ssd_conv_b[i][None, :], bias_c, a_c, bias_r, a_r, dsk,
                             n_lat)

        y_s5 = _s5_mix(u, *(w[i] for w in s5_w), s5_spread, n_lat)

        lf = da_lambda[i]
        lam = (jnp.exp(jnp.sum(lf[0] * lf[1])) - jnp.exp(jnp.sum(lf[2] * lf[3])) + lam_init).reshape(1)
        g_sub = da_sub_norm[i][None, :]
        score_bound = LOG2E * math.sqrt(DA_HEAD_DIM) * jnp.max(jnp.abs(da_q_norm[i])) * jnp.max(jnp.abs(da_k_norm[i]))
        flag = (score_bound <= SCORE_LOG2_LIMIT).astype(jnp.int32).reshape(1)
        y_da = _attn_call(flag, lam, qh, kzb, vb, g_sub, 1.0 - lam_init, n_lat, 0, None, None, None)
        last = i == depth - 1
        if not last:
            y_da = _attn_call(flag, lam, qh, kzb, vb, g_sub, 1.0 - lam_init, n_ctx, n_lat,
                              n_lat // tk, n_ctx, n_lat % tk, y_rest=y_da)

        xs = _outffn_call(xs, y_f, y_b, z, ssd_norm[i][None, :], y_s5, u, s5_d[i][None, :], s5_glu_w[i],
                          s5_glu_b[i][None, :], y_da, modl, norm2[i][None, :], w_out_b, w1_b, w3_b, w2_b, i,
                          n_lat, latent_only=last)
    return xs[None]
```

```python
import functools
import math

import jax
import jax.numpy as jnp
from jax import lax
from jax.experimental import pallas as pl
from jax.experimental.pallas import tpu as pltpu

F32 = jnp.float32
BF16 = jnp.bfloat16
HI = lax.Precision.HIGHEST

EPS = 1e-6
GRID_W = 64
ROPE_THETA = 10000.0
ROPE_FREQS = 8

SSD_HEADS = 4
SSD_HEAD_DIM = 64
SSD_INNER = 256
SSD_GROUPS = 2
SSD_STATE = 64
SSD_CHUNK = 128
SSD_STEP_CHUNKS = 2
SSD_CONV_DIM = 512

S5_WIDTH = 256
S5_GROUP = 16
S5_GROUPS = 16
S5_STATE = 64
S5_CHUNK = 16
S5_HALF_GROUPS = S5_GROUPS // 2
S5_HALF_STATE = S5_HALF_GROUPS * S5_STATE
S5_SCAN_TILE = 16
S5_TOEPLITZ_BLOCKS = 2 * S5_CHUNK - 1
S5_KCOLS = 512

DA_HEADS = 8
DA_HEAD_DIM = 32
DA_V_DIM = 64
DA_WIDTH = 512
ATTN_HEADS_PER_STEP = 2
ATTN_TQ = 2048
ATTN_TK = 1280
LOG2E = 1.4426950408889634
SCORE_LOG2_LIMIT = 100.0

LANES = 128
ROW_TILE = 256
FFN_ROW_TILE = 256
IN_PAD = 2688
VMEM_LIMIT = 56 * 1024 * 1024


def _cparams(*sem):
    return pltpu.CompilerParams(dimension_semantics=sem, vmem_limit_bytes=VMEM_LIMIT)


def _dot(a, b, precision=None):
    return jnp.dot(a, b, preferred_element_type=F32, precision=precision)


def _dot_nt(a, b, precision=None):
    return lax.dot_general(a, b, (((1,), (1,)), ((), ())), preferred_element_type=F32,
                           precision=precision)


def _dot_tn(a, b, precision=None):
    return lax.dot_general(a, b, (((0,), (0,)), ((), ())), preferred_element_type=F32,
                           precision=precision)


def _silu(x):
    return x * jax.nn.sigmoid(x)


def _pick_tile(n, candidates):
    for c in candidates:
        if n % c == 0:
            return c
    return n


def _interleave(*stages):
    results = [None] * len(stages)
    live = dict(enumerate(stages))
    while live:
        for k in list(live):
            try:
                next(live[k])
            except StopIteration as done:
                results[k] = done.value
                del live[k]
        yield
    return results


def _run(stage):
    while True:
        try:
            next(stage)
        except StopIteration as done:
            return done.value


def _segment_order(i, n_lat, n_all, reverse):
    n_ctx = n_all - n_lat
    if not reverse:
        return jnp.where(i < n_ctx, n_lat + i, i - n_ctx)
    return jnp.where(i < n_ctx, n_all - 1 - i, n_lat - 1 - (i - n_ctx))


def _mod_body(cv_ref, w_ref, b_ref, o_ref):
    o_ref[0] = _dot(_silu(cv_ref[...]), w_ref[0], HI) + b_ref[0]


def _mod_call(cv, w_mod, b_mod):
    depth, d, n = w_mod.shape
    tn = 1024
    return pl.pallas_call(
        _mod_body,
        grid=(depth, n // tn),
        in_specs=[pl.BlockSpec((8, d), lambda l, j: (0, 0)),
                  pl.BlockSpec((1, d, tn), lambda l, j: (l, 0, j)),
                  pl.BlockSpec((1, 1, tn), lambda l, j: (l, 0, j))],
        out_specs=pl.BlockSpec((1, 8, tn), lambda l, j: (l, 0, j)),
        out_shape=jax.ShapeDtypeStruct((depth, 8, n), F32),
        compiler_params=_cparams("arbitrary", "arbitrary"),
        name="adaln_mod",
    )(cv, w_mod, b_mod.reshape(depth, 1, n))


def _inproj_project(x_ref, g_ref, mod_ref, w_ref, z_ref, xbc_ref, u_ref, dt_ref, qkv_ref):
    x = x_ref[...]
    mod = mod_ref[0]
    ms = jnp.mean(x * x, axis=-1, keepdims=True)
    h = x * lax.rsqrt(ms + EPS) * g_ref[...] * (1.0 + mod[1:2]) + mod[0:1]
    p = _dot(h.astype(BF16), w_ref[...])
    z_ref[...] = p[:, 0:256]
    xbc_ref[...] = p[:, 256:768]
    u_ref[...] = p[:, 768:1024]
    dt_ref[...] = p[:, 2560:2688]
    qkv_ref[...] = p[:, 1024:2560]


def _inproj_attn_operands(qkv_ref, gq_ref, gk_ref, gm_ref, cos_ref, sa_ref, sb_ref, q_ref, k_ref, v_ref):
    p = qkv_ref[...]
    rows = p.shape[0]
    gm = gm_ref[...]
    cos = cos_ref[...]
    sa = sa_ref[...]
    sb = sb_ref[...]

    def norm_rope(t, gain, scale):
        ms32 = _dot((t * t).astype(BF16), gm)
        tn = t * lax.rsqrt(ms32 + EPS) * gain
        outs = []
        for j in range(DA_WIDTH // LANES):
            tb = tn[:, j * LANES:(j + 1) * LANES]
            ob = tb * cos + pltpu.roll(tb, LANES - ROPE_FREQS, 1) * sa + pltpu.roll(tb, ROPE_FREQS, 1) * sb
            outs.append(ob * scale)
        return jnp.concatenate(outs, axis=1)

    qn = norm_rope(p[:, 0:512], gq_ref[...], LOG2E * DA_HEAD_DIM ** -0.5).astype(BF16)
    kt = norm_rope(p[:, 512:1024], gk_ref[...], 1.0).T.astype(BF16)
    vv = p[:, 1024:1536].astype(BF16)
    lane = lax.broadcasted_iota(jnp.int32, (rows, LANES - DA_V_DIM), 1)
    ones_col = jnp.where(lane == 0, 1.0, 0.0).astype(BF16)
    zeros_k = jnp.zeros((DA_HEAD_DIM, rows), BF16)
    for hd in range(DA_HEADS):
        lo_, mid, hi_ = hd * DA_V_DIM, hd * DA_V_DIM + DA_HEAD_DIM, (hd + 1) * DA_V_DIM
        q_ref[hd] = qn[:, lo_:hi_]
        k_ref[hd, 0, 0, 0:DA_HEAD_DIM, :] = kt[lo_:mid]
        k_ref[hd, 0, 0, DA_HEAD_DIM:, :] = zeros_k
        k_ref[hd, 0, 1, 0:DA_HEAD_DIM, :] = zeros_k
        k_ref[hd, 0, 1, DA_HEAD_DIM:, :] = kt[mid:hi_]
        v_ref[hd, 0] = jnp.concatenate([vv[:, lo_:hi_], ones_col], axis=1)


def _inproj_body(x_ref, g_ref, mod_ref, w_ref, gq_ref, gk_ref, gm_ref, cos_ref, sa_ref, sb_ref,
                 z_ref, xbc_ref, u_ref, dt_ref, q_ref, k_ref, v_ref, qkv_a_ref, qkv_b_ref):
    i = pl.program_id(0)

    @pl.when(i == 0)
    def _():
        qkv_b_ref[...] = jnp.zeros_like(qkv_b_ref)

    def step(cur_ref, prev_ref):
        _inproj_attn_operands(prev_ref, gq_ref, gk_ref, gm_ref, cos_ref, sa_ref, sb_ref, q_ref, k_ref, v_ref)
        _inproj_project(x_ref, g_ref, mod_ref, w_ref, z_ref, xbc_ref, u_ref, dt_ref, cur_ref)

    @pl.when(i % 2 == 0)
    def _():
        step(qkv_a_ref, qkv_b_ref)

    @pl.when(i % 2 == 1)
    def _():
        step(qkv_b_ref, qkv_a_ref)


def _inproj_call(xs, g1, modl, w_r, gq, gk, gmat, cos_t, sa_t, sb_t, n_lat_tiles, tk):
    t, d = xs.shape
    tm = ROW_TILE
    n_tiles = t // tm
    per = tk // tm
    row = lambda i: (jnp.minimum(i, n_tiles - 1), 0)
    done = lambda i: (jnp.maximum(i - 1, 0), 0)
    const = lambda i: (0, 0)
    seg = lambda i: ((jnp.minimum(i, n_tiles - 1) >= n_lat_tiles).astype(jnp.int32), 0, 0)
    flat = [(256, F32), (512, F32), (256, F32), (LANES, F32)]

    def k_block(i):
        j = jnp.maximum(i - 1, 0)
        return (0, j // per, 0, 0, j % per)

    def v_block(i):
        j = jnp.maximum(i - 1, 0)
        return (0, j // per, j % per, 0)

    return pl.pallas_call(
        _inproj_body,
        grid=(n_tiles + 1,),
        in_specs=[pl.BlockSpec((tm, d), row),
                  pl.BlockSpec((1, d), const),
                  pl.BlockSpec((1, 8, d), seg),
                  pl.BlockSpec((d, IN_PAD), const),
                  pl.BlockSpec((1, DA_WIDTH), const),
                  pl.BlockSpec((1, DA_WIDTH), const),
                  pl.BlockSpec((DA_WIDTH, DA_WIDTH), const),
                  pl.BlockSpec((tm, LANES), done),
                  pl.BlockSpec((tm, LANES), done),
                  pl.BlockSpec((tm, LANES), done)],
        out_specs=[pl.BlockSpec((tm, w), row) for w, _ in flat]
                  + [pl.BlockSpec((DA_HEADS, tm, 2 * DA_HEAD_DIM), lambda i: (0, jnp.maximum(i - 1, 0), 0)),
                     pl.BlockSpec((DA_HEADS, 1, 2, 2 * DA_HEAD_DIM, tm), k_block),
                     pl.BlockSpec((DA_HEADS, 1, tm, LANES), v_block)],
        out_shape=[jax.ShapeDtypeStruct((t, w), dt) for w, dt in flat]
                  + [jax.ShapeDtypeStruct((DA_HEADS, t, 2 * DA_HEAD_DIM), BF16),
                     jax.ShapeDtypeStruct((DA_HEADS, t // tk, 2, 2 * DA_HEAD_DIM, tk), BF16),
                     jax.ShapeDtypeStruct((DA_HEADS, t // tk, tk, LANES), BF16)],
        scratch_shapes=[pltpu.VMEM((tm, 3 * DA_WIDTH), F32)] * 2,
        compiler_params=_cparams("arbitrary"),
        name="in_proj",
    )(xs, g1, modl, w_r, gq, gk, gmat, cos_t, sa_t, sb_t)


def _softplus(x):
    return jnp.maximum(x, 0.0) + jnp.log1p(jnp.exp(-jnp.abs(x)))


def _ssd_block(c, n_lat, n_blocks, reverse, xc_ref, xp_ref, xn_ref, dtc_ref, dtr_ref, cw_ref, cb_ref,
               bias_c_ref, a_c_ref, bias_r_ref, a_r_ref, st_ref):
    L = SSD_CHUNK
    rows = xc_ref.shape[0]
    x = xc_ref[...]
    seg_first = jnp.logical_or(c == 0, c == n_lat)
    seg_last = jnp.logical_or(c == n_lat - 1, c == n_blocks - 1)
    prow = jnp.where(seg_first, 0.0, xp_ref[7:8, :])
    nrow = jnp.where(seg_last, 0.0, xn_ref[0:1, :])
    ridx = lax.broadcasted_iota(jnp.int32, (rows, 1), 0)
    xprev = jnp.where(ridx == 0, prow, pltpu.roll(x, 1, 0))
    xnext = jnp.where(ridx == rows - 1, nrow, pltpu.roll(x, rows - 1, 0))
    cw = cw_ref[...]
    conv = xprev * cw[0:1] + x * cw[1:2] + xnext * cw[2:3] + cb_ref[...]
    xbc = _silu(conv)

    dt_c = _softplus(dtc_ref[...] + bias_c_ref[...])
    dt_r = _softplus(dtr_ref[...] + bias_r_ref[...])
    adt_c = dt_c * a_c_ref[...]
    adt_r = dt_r * a_r_ref[...]

    li = lax.broadcasted_iota(jnp.int32, (L, L), 0)
    si = lax.broadcasted_iota(jnp.int32, (L, L), 1)
    mask = (li <= si) if reverse else (li >= si)
    mask_t = (li >= si) if reverse else (li <= si)
    n_sub = rows // L
    yield
    parts = yield from _interleave(*[
        _ssd_chunk_local(xbc[s * L:(s + 1) * L], dt_c[s * L:(s + 1) * L], adt_c[s * L:(s + 1) * L],
                         adt_r[:, s * L:(s + 1) * L], mask, mask_t, reverse) for s in range(n_sub)])
    ys = [None] * n_sub
    for s in (reversed(range(n_sub)) if reverse else range(n_sub)):
        ys[s] = _ssd_chunk_state(*parts[s], st_ref, reverse)
        yield
    return jnp.concatenate(ys, axis=0), xbc[:, :SSD_INNER]


def _split3(x):
    hi = x.astype(BF16)
    r1 = x - hi.astype(F32)
    mid = r1.astype(BF16)
    lo = (r1 - mid.astype(F32)).astype(BF16)
    return hi, mid, lo


def _ssd_chunk_local(xbc, dt_c, adt_c, adt_r, mask, mask_t, reverse):
    L = SSD_CHUNK
    hpg = SSD_HEADS // SSD_GROUPS
    off = SSD_HEADS if reverse else 0
    m_b, mt_b = mask.astype(BF16), mask_t.astype(BF16)
    cs_c = sum(_dot(m_b, part) for part in _split3(adt_c))
    cs_r = sum(_dot(part, mt_b) for part in _split3(adt_r))
    edge = 0 if reverse else L - 1
    tot = cs_c[edge:edge + 1, :]
    yield

    kk = lax.broadcasted_iota(jnp.int32, (LANES, SSD_INNER), 0)
    jj = lax.broadcasted_iota(jnp.int32, (LANES, SSD_INNER), 1)
    spread = (kk == off + jj // SSD_HEAD_DIM).astype(BF16)
    cols = jnp.concatenate([dt_c, jnp.exp(cs_c), jnp.exp(tot - cs_c)], axis=0)
    c_hi = cols.astype(BF16)
    c_lo = (cols - c_hi.astype(F32)).astype(BF16)
    wide = _dot(c_hi, spread) + _dot(c_lo, spread)
    dt_w, ecs_w, dst_w = wide[0:L], wide[L:2 * L], wide[2 * L:3 * L]
    yield

    xs = xbc[:, :SSD_INNER]
    b_all = xbc[:, SSD_INNER:SSD_INNER + LANES]
    c_all = xbc[:, SSD_INNER + LANES:SSD_INNER + 2 * LANES]
    xd = xs * dt_w
    xd_b = xd.astype(BF16)
    b_b = b_all.astype(BF16)
    c_b = c_all.astype(BF16)
    lane = lax.broadcasted_iota(jnp.int32, (L, LANES), 1)
    col = lax.broadcasted_iota(jnp.int32, (L, SSD_INNER), 1)

    y = None
    for g in range(SSD_GROUPS):
        c_g = jnp.where(lane // SSD_STATE == g, c_b, jnp.zeros_like(c_b))
        gmat = _dot_nt(c_g, b_b)
        for hh in range(hpg):
            h = g * hpg + hh
            dec = jnp.exp(jnp.where(mask, cs_c[:, off + h:off + h + 1] - cs_r[off + h:off + h + 1, :], -jnp.inf))
            xd_h = jnp.where(col // SSD_HEAD_DIM == h, xd_b, jnp.zeros_like(xd_b))
            part = _dot((gmat * dec).astype(BF16), xd_h)
            y = part if y is None else y + part
            yield
    return y, c_b, ecs_w, b_all.T.astype(BF16), (xd * dst_w).astype(BF16)


def _ssd_chunk_state(y_local, c_b, ecs_w, bt_b, xds_b, st_ref, reverse):
    edge = 0 if reverse else SSD_CHUNK - 1
    y = y_local + ecs_w * _dot(c_b, st_ref[...].astype(BF16))
    kr = lax.broadcasted_iota(jnp.int32, (LANES, SSD_INNER), 0)
    kc = lax.broadcasted_iota(jnp.int32, (LANES, SSD_INNER), 1)
    block = kr // SSD_STATE == kc // (SSD_HEADS // SSD_GROUPS * SSD_HEAD_DIM)
    st_ref[...] = st_ref[...] * ecs_w[edge:edge + 1, :] + jnp.where(block, _dot(bt_b, xds_b), 0.0)
    return y


def _ssd_body(n_lat, n_chunks, *refs):
    fwd_in, bwd_in = refs[0:5], refs[5:10]
    cw_ref, cb_ref, bias_c_ref, a_c_ref, bias_r_ref, a_r_ref, dsk_ref, yf_ref, yb_ref, st_ref = refs[10:]
    i = pl.program_id(0)

    @pl.when(i == 0)
    def _():
        st_ref[...] = jnp.zeros_like(st_ref)

    shared = (cw_ref, cb_ref, bias_c_ref, a_c_ref, bias_r_ref, a_r_ref)
    (y_f, xs), (y_b, _) = _run(_interleave(
        _ssd_block(_segment_order(i, n_lat, n_chunks, False), n_lat, n_chunks, False, *fwd_in, *shared,
                   st_ref.at[0]),
        _ssd_block(_segment_order(i, n_lat, n_chunks, True), n_lat, n_chunks, True, *bwd_in, *shared,
                   st_ref.at[1])))
    yf_ref[...] = y_f + dsk_ref[...] * xs
    yb_ref[...] = y_b


def _ssd_call(xbc_raw, dt_c, dt_r, cw, cb, bias_c, a_c, bias_r, a_r, dsk, n_lat_rows):
    t = xbc_raw.shape[0]
    L = SSD_STEP_CHUNKS * SSD_CHUNK
    assert t % L == 0 and n_lat_rows % L == 0
    n_chunks = t // L
    n_lat = n_lat_rows // L
    sub = L // 8
    n_sub = t // 8
    const = lambda i: (0, 0)

    def chunk_specs(reverse):
        cidx = functools.partial(_segment_order, n_lat=n_lat, n_all=n_chunks, reverse=reverse)
        return [pl.BlockSpec((L, SSD_CONV_DIM), lambda i: (cidx(i), 0)),
                pl.BlockSpec((8, SSD_CONV_DIM), lambda i: (jnp.maximum(cidx(i) * sub - 1, 0), 0)),
                pl.BlockSpec((8, SSD_CONV_DIM), lambda i: (jnp.minimum((cidx(i) + 1) * sub, n_sub - 1), 0)),
                pl.BlockSpec((L, LANES), lambda i: (cidx(i), 0)),
                pl.BlockSpec((8, L), lambda i: (0, cidx(i)))]

    def out_spec(reverse):
        cidx = functools.partial(_segment_order, n_lat=n_lat, n_all=n_chunks, reverse=reverse)
        return pl.BlockSpec((L, SSD_INNER), lambda i: (cidx(i), 0))

    chunk_args = (xbc_raw, xbc_raw, xbc_raw, dt_c, dt_r)
    return pl.pallas_call(
        functools.partial(_ssd_body, n_lat, n_chunks),
        grid=(n_chunks,),
        in_specs=chunk_specs(False) + chunk_specs(True)
                 + [pl.BlockSpec((8, SSD_CONV_DIM), const),
                    pl.BlockSpec((1, SSD_CONV_DIM), const),
                    pl.BlockSpec((1, LANES), const),
                    pl.BlockSpec((1, LANES), const),
                    pl.BlockSpec((8, L), const),
                    pl.BlockSpec((8, L), const),
                    pl.BlockSpec((1, SSD_INNER), const)],
        out_specs=[out_spec(False), out_spec(True)],
        out_shape=[jax.ShapeDtypeStruct((t, SSD_INNER), F32)] * 2,
        scratch_shapes=[pltpu.VMEM((2, SSD_GROUPS * SSD_STATE, SSD_INNER), F32)],
        compiler_params=_cparams("arbitrary"),
        name="ssd_scan",
    )(*chunk_args, *chunk_args, cw, cb, bias_c, a_c, bias_r, a_r, dsk)


def _cmul(ar, ai, br, bi):
    return ar * br - ai * bi, ar * bi + ai * br


def _s5_step_pair(u_ref, a):
    return jnp.concatenate([u_ref[:, 2 * a, :], u_ref[:, 2 * a + 1, :]], axis=1).astype(BF16)


def _s5_in_body(u_ref, w_ref, lam_ref, efr_ref, efi_ref, ebr_ref, ebi_ref, wp_ref):
    hs = S5_HALF_STATE

    @pl.when(pl.program_id(1) == 0)
    def _():
        lam = lam_ref[0]
        wc = w_ref[0]
        rows = lambda a: jnp.broadcast_to(a, (LANES, hs))
        fr, fi, br, bi = rows(lam[0:1]), rows(lam[1:2]), rows(lam[2:3]), rows(lam[3:4])
        rgrp = lax.broadcasted_iota(jnp.int32, (LANES, LANES), 0) // S5_GROUP
        lgrp = lax.broadcasted_iota(jnp.int32, (LANES, LANES), 1) // S5_STATE
        planes = []
        for q in range(4):
            cq = wc[:, q * S5_STATE:(q + 1) * S5_STATE]
            cq2 = jnp.concatenate([cq, cq], axis=1)
            planes.append(jnp.concatenate(
                [jnp.where(rgrp == 2 * g4 + lgrp, cq2, 0.0) for g4 in range(S5_HALF_GROUPS // 2)], axis=1))
        ar, ai = planes[0], planes[1]
        for i in reversed(range(S5_CHUNK)):
            wp_ref[i * LANES:(i + 1) * LANES, 0:hs] = ar.astype(BF16)
            wp_ref[i * LANES:(i + 1) * LANES, hs:2 * hs] = ai.astype(BF16)
            ar, ai = _cmul(ar, ai, fr, fi)
        ar, ai = planes[2], planes[3]
        for i in range(S5_CHUNK):
            wp_ref[i * LANES:(i + 1) * LANES, 2 * hs:3 * hs] = ar.astype(BF16)
            wp_ref[i * LANES:(i + 1) * LANES, 3 * hs:4 * hs] = ai.astype(BF16)
            ar, ai = _cmul(ar, ai, br, bi)

    e = None
    for a in range(S5_CHUNK // 2):
        part = _dot(_s5_step_pair(u_ref, a), wp_ref[2 * a * LANES:2 * (a + 1) * LANES, :])
        e = part if e is None else e + part
    efr_ref[...] = e[:, 0:hs]
    efi_ref[...] = e[:, hs:2 * hs]
    ebr_ref[...] = e[:, 2 * hs:3 * hs]
    ebi_ref[...] = e[:, 3 * hs:4 * hs]


def _s5_in_call(u3, w_b, lam_rows):
    nch = u3.shape[0]
    tn = _pick_tile(nch, (208, 80, 40))
    hs = S5_HALF_STATE
    col = lambda h, n: (n, h)
    return pl.pallas_call(
        _s5_in_body,
        grid=(2, nch // tn),
        in_specs=[pl.BlockSpec((tn, S5_CHUNK, LANES), lambda h, n: (n, 0, h)),
                  pl.BlockSpec((1, LANES, 4 * S5_STATE), lambda h, n: (h, 0, 0)),
                  pl.BlockSpec((1, 4, hs), lambda h, n: (h, 0, 0))],
        out_specs=[pl.BlockSpec((tn, hs), col)] * 4,
        out_shape=[jax.ShapeDtypeStruct((nch, 2 * hs), F32)] * 4,
        scratch_shapes=[pltpu.VMEM((S5_CHUNK * LANES, 4 * hs), BF16)],
        compiler_params=_cparams("arbitrary", "arbitrary"),
        name="s5_chunk_in",
    )(u3, w_b, lam_rows)


def _s5_scan_body(a_ref, efr_ref, efi_ref, ebr_ref, ebi_ref, hfr_ref, hfi_ref, hbr_ref, hbi_ref, st_ref):
    @pl.when(pl.program_id(0) == 0)
    def _():
        st_ref[...] = jnp.zeros_like(st_ref)

    row = lambda ref, j: ref[j:j + 1, :]
    afr, afi, abr, abi = row(a_ref, 0), row(a_ref, 1), row(a_ref, 2), row(a_ref, 3)
    fr, fi, br, bi = row(st_ref, 0), row(st_ref, 1), row(st_ref, 2), row(st_ref, 3)
    for j in range(S5_SCAN_TILE):
        hfr_ref[j:j + 1, :] = fr
        hfi_ref[j:j + 1, :] = fi
        fr, fi = afr * fr - afi * fi + row(efr_ref, j), afr * fi + afi * fr + row(efi_ref, j)
        jb = S5_SCAN_TILE - 1 - j
        hbr_ref[jb:jb + 1, :] = br
        hbi_ref[jb:jb + 1, :] = bi
        br, bi = abr * br - abi * bi + row(ebr_ref, jb), abr * bi + abi * br + row(ebi_ref, jb)
    st_ref[0:1, :] = fr
    st_ref[1:2, :] = fi
    st_ref[2:3, :] = br
    st_ref[3:4, :] = bi


def _s5_scan_call(a_pow, efr, efi, ebr, ebi, n_lat_tiles):
    nch, width = efr.shape
    n_tiles = nch // S5_SCAN_TILE
    blk = (S5_SCAN_TILE, width)
    fwd = lambda i: (_segment_order(i, n_lat_tiles, n_tiles, False), 0)
    bwd = lambda i: (_segment_order(i, n_lat_tiles, n_tiles, True), 0)
    shp = jax.ShapeDtypeStruct((nch, width), F32)
    return pl.pallas_call(
        _s5_scan_body,
        grid=(n_tiles,),
        in_specs=[pl.BlockSpec((4, width), lambda i: (0, 0)),
                  pl.BlockSpec(blk, fwd), pl.BlockSpec(blk, fwd),
                  pl.BlockSpec(blk, bwd), pl.BlockSpec(blk, bwd)],
        out_specs=[pl.BlockSpec(blk, fwd), pl.BlockSpec(blk, fwd),
                   pl.BlockSpec(blk, bwd), pl.BlockSpec(blk, bwd)],
        out_shape=[shp] * 4,
        scratch_shapes=[pltpu.VMEM((4, width), F32)],
        compiler_params=_cparams("arbitrary"),
        name="s5_state_scan",
    )(a_pow, efr, efi, ebr, ebi)


def _s5_out_body(u_ref, hfr_ref, hfi_ref, hbr_ref, hbi_ref, k_ref, t_ref, c_ref, lam_ref, y_ref, wq_ref, z_ref):
    hs = S5_HALF_STATE
    width = S5_CHUNK * LANES

    @pl.when(pl.program_id(1) == 0)
    def _():
        strip = _dot(k_ref[0].astype(BF16), t_ref[...])
        rgrp = lax.broadcasted_iota(jnp.int32, strip.shape, 0) // S5_GROUP
        cgrp = (lax.broadcasted_iota(jnp.int32, strip.shape, 1) % LANES) // S5_GROUP
        strip = jnp.where(rgrp == cgrp, strip, 0.0).astype(BF16)
        z_ref[0:LANES, :] = strip[:, LANES:]
        z_ref[LANES:, :] = strip[:, :-LANES]

        cc = c_ref[0]
        ogrp = lax.broadcasted_iota(jnp.int32, (S5_STATE, LANES), 1) // S5_GROUP

        def expand(q):
            cq = cc[q * S5_STATE:(q + 1) * S5_STATE]
            return jnp.concatenate([jnp.where(ogrp == g, cq, 0.0) for g in range(S5_HALF_GROUPS)], axis=0)

        lam = lam_ref[0]
        for d, order in ((0, range(S5_CHUNK)), (1, reversed(range(S5_CHUNK)))):
            lr, li = lam[2 * d], lam[2 * d + 1]
            a, b = expand(2 * d), expand(2 * d + 1)
            for j in order:
                a, b = a * lr + b * li, b * lr - a * li
                wq_ref[2 * d * hs:(2 * d + 1) * hs, j * LANES:(j + 1) * LANES] = a.astype(BF16)
                wq_ref[(2 * d + 1) * hs:(2 * d + 2) * hs, j * LANES:(j + 1) * LANES] = b.astype(BF16)

    h_all = jnp.concatenate([hfr_ref[...], hfi_ref[...], hbr_ref[...], hbi_ref[...]], axis=1).astype(BF16)
    y = _dot(h_all, wq_ref[...])
    for a in range(S5_CHUNK // 2):
        lo = (S5_CHUNK - 2 - 2 * a) * LANES
        y += _dot(_s5_step_pair(u_ref, a), z_ref[:, lo:lo + width])
    for j in range(S5_CHUNK):
        y_ref[:, j, :] = y[:, j * LANES:(j + 1) * LANES]


def _s5_out_call(u3, hfr, hfi, hbr, hbi, w_k, w_t, w_c, lam_cols):
    nch = u3.shape[0]
    tn = _pick_tile(nch, (208, 80, 40))
    hs = S5_HALF_STATE
    col = lambda h, n: (n, h)
    blk3 = pl.BlockSpec((tn, S5_CHUNK, LANES), lambda h, n: (n, 0, h))
    return pl.pallas_call(
        _s5_out_body,
        grid=(2, nch // tn),
        in_specs=[blk3] + [pl.BlockSpec((tn, hs), col)] * 4
                 + [pl.BlockSpec((1, LANES) + w_k.shape[2:], lambda h, n: (h, 0, 0)),
                    pl.BlockSpec(w_t.shape, lambda h, n: (0, 0)),
                    pl.BlockSpec((1, 4 * S5_STATE, LANES), lambda h, n: (h, 0, 0)),
                    pl.BlockSpec((1, 4, hs, LANES), lambda h, n: (h, 0, 0, 0))],
        out_specs=blk3,
        out_shape=jax.ShapeDtypeStruct(u3.shape, F32),
        scratch_shapes=[pltpu.VMEM((4 * hs, S5_CHUNK * LANES), BF16),
                        pltpu.VMEM((2 * LANES, (S5_TOEPLITZ_BLOCKS - 1) * LANES), BF16)],
        compiler_params=_cparams("arbitrary", "arbitrary"),
        name="s5_chunk_out",
    )(u3, hfr, hfi, hbr, hbi, w_k, w_t, w_c, lam_cols)


def _s5_weights(lam_re, lam_im, log_step, b_re, b_im, c_re, c_im):
    tc = S5_CHUNK
    nl = lam_re.shape[0]
    hg = S5_HALF_GROUPS
    ein = functools.partial(jnp.einsum, precision=HI)
    step = jnp.exp(log_step)[..., None]
    er, ei = lam_re * step, lam_im * step
    mag = jnp.exp(er)
    lbr, lbi = mag * jnp.cos(ei), mag * jnp.sin(ei)
    den = lam_re * lam_re + lam_im * lam_im
    nr, ni = lbr - 1.0, lbi
    cr, ci = (nr * lam_re + ni * lam_im) / den, (ni * lam_re - nr * lam_im) / den
    bbr = cr[..., None] * b_re - ci[..., None] * b_im
    bbi = cr[..., None] * b_im + ci[..., None] * b_re
    taus = jnp.arange(tc + 1, dtype=F32)[:, None, None]
    pmag = jnp.exp(er[:, :, None] * taus)
    pr, pi = pmag * jnp.cos(ei[:, :, None] * taus), pmag * jnp.sin(ei[:, :, None] * taus)

    wr = pr[:, :, :tc, :, :, None] * bbr[:, :, None] - pi[:, :, :tc, :, :, None] * bbi[:, :, None]
    wi = pr[:, :, :tc, :, :, None] * bbi[:, :, None] + pi[:, :, :tc, :, :, None] * bbr[:, :, None]
    kk = ein('ldgcp,ldtgpk->ldtgck', c_re, wr) - ein('ldgcp,ldtgpk->ldtgck', c_im, wi)
    kf, kb = kk[:, 0], kk[:, 1]
    seq = jnp.concatenate([kb[:, :0:-1], kf[:, :1] + kb[:, :1], kf[:, 1:]], axis=1)
    seq = seq.reshape(nl, S5_TOEPLITZ_BLOCKS, 2, hg, S5_GROUP, S5_GROUP)
    w_k = seq.transpose(0, 2, 3, 5, 1, 4).reshape(nl, 2, LANES, S5_TOEPLITZ_BLOCKS * S5_GROUP)
    w_k = jnp.pad(w_k, ((0, 0), (0, 0), (0, 0), (0, S5_KCOLS - S5_TOEPLITZ_BLOCKS * S5_GROUP)))
    r = jnp.arange(S5_KCOLS)
    s = jnp.arange(S5_TOEPLITZ_BLOCKS * LANES)
    w_t = ((r[:, None] // S5_GROUP == s[None, :] // LANES)
           & (r[:, None] % S5_GROUP == s[None, :] % S5_GROUP)).astype(BF16)

    planes_b = jnp.stack([bbr[:, 0], bbi[:, 0], bbr[:, 1], bbi[:, 1]], axis=1)
    planes_b = planes_b.reshape(nl, 4, 2, hg, S5_STATE, S5_GROUP)
    w_b = planes_b.transpose(0, 2, 3, 5, 1, 4).reshape(nl, 2, LANES, 4 * S5_STATE)
    planes_c = jnp.stack([c_re[:, 0], -c_im[:, 0], c_re[:, 1], -c_im[:, 1]], axis=1)
    planes_c = planes_c.reshape(nl, 4, 2, hg, S5_GROUP, S5_STATE)
    w_c = planes_c.transpose(0, 2, 1, 5, 3, 4).reshape(nl, 2, 4 * S5_STATE, LANES)

    halves = lambda a: a.reshape(nl, 2, S5_HALF_STATE)
    lam_rows = jnp.stack([halves(lbr[:, 0]), halves(lbi[:, 0]), halves(lbr[:, 1]), halves(lbi[:, 1])], axis=2)
    lam_cols = jnp.broadcast_to(lam_rows[..., None], lam_rows.shape + (LANES,))
    full = lambda a: a.reshape(nl, S5_GROUPS * S5_STATE)
    a_pow = jnp.stack([full(pr[:, 0, tc]), full(pi[:, 0, tc]), full(pr[:, 1, tc]), full(pi[:, 1, tc])], axis=1)
    return (w_b, w_k, w_c, lam_rows, lam_cols, a_pow), w_t


def _s5_mix(u, w_b, w_k, w_c, lam_rows, lam_cols, a_pow, w_t, n_lat):
    t = u.shape[0]
    nch = t // S5_CHUNK
    u3 = u.reshape(nch, S5_CHUNK, S5_WIDTH)
    es = _s5_in_call(u3, w_b, lam_rows)
    hs = _s5_scan_call(a_pow, *es, n_lat // (S5_CHUNK * S5_SCAN_TILE))
    return _s5_out_call(u3, *hs, w_k, w_t, w_c, lam_cols).reshape(t, S5_WIDTH)


def _attn_body(flag_ref, lam_ref, q_ref, k_ref, v_ref, g_ref, o_ref, acc_ref, m_ref, oh_ref, *, n_kv, post_scale):
    bounded = flag_ref[0] == 1

    def head(hh, carry):
        q = q_ref[hh]
        acc_ref[...] = jnp.zeros_like(acc_ref)

        @pl.when(bounded)
        def _():
            def step(b, carry2):
                v = v_ref[hh, b]
                for c in range(2):
                    p = jnp.exp2(_dot(q, k_ref[hh, b, c])).astype(BF16)
                    acc_ref[c] += _dot(p, v)
                return carry2
            lax.fori_loop(0, n_kv, step, 0)

        @pl.when(jnp.logical_not(bounded))
        def _():
            m_ref[...] = jnp.full_like(m_ref, -jnp.inf)

            def step(b, carry2):
                v = v_ref[hh, b]
                for c in range(2):
                    s = _dot(q, k_ref[hh, b, c])
                    m_old = m_ref[c]
                    m_new = jnp.maximum(m_old, jnp.max(s, axis=1, keepdims=True))
                    p = jnp.exp2(s - m_new[:, 0:1]).astype(BF16)
                    acc_ref[c] = jnp.exp2(m_old - m_new) * acc_ref[c] + _dot(p, v)
                    m_ref[c] = m_new
                return carry2
            lax.fori_loop(0, n_kv, step, 0)

        a0 = acc_ref[0]
        a1 = acc_ref[1]
        o = a0[:, :DA_V_DIM] / a0[:, DA_V_DIM:DA_V_DIM + 1] \
            - lam_ref[0] * (a1[:, :DA_V_DIM] / a1[:, DA_V_DIM:DA_V_DIM + 1])
        ms = jnp.mean(o * o, axis=-1, keepdims=True)
        oh_ref[hh] = o * lax.rsqrt(ms + EPS) * g_ref[...] * post_scale
        return carry

    lax.fori_loop(0, ATTN_HEADS_PER_STEP, head, 0)
    o_ref[...] = jnp.concatenate([oh_ref[hh] for hh in range(ATTN_HEADS_PER_STEP)], axis=1)


def _attn_body_keep_rest(flag_ref, lam_ref, q_ref, k_ref, v_ref, g_ref, rest_ref, *refs, **kw):
    del rest_ref
    _attn_body(flag_ref, lam_ref, q_ref, k_ref, v_ref, g_ref, *refs, **kw)


def _attn_call(flag, lam, qh, kzb, vb, g_sub, post_scale, q_rows, q_row0, kv_block, kv_cols, kv_col0, y_rest=None):
    nh = qh.shape[0]
    hp = ATTN_HEADS_PER_STEP
    tk = kzb.shape[4]
    tq = _pick_tile(q_rows, (ATTN_TQ, 256))
    assert q_row0 % tq == 0
    q0 = q_row0 // tq
    if kv_block is None:
        n_kv = kzb.shape[1]
        one = pl.Buffered(1)
        k_spec = pl.BlockSpec((hp, n_kv, 2, 2 * DA_HEAD_DIM, tk), lambda h, i: (h, 0, 0, 0, 0), pipeline_mode=one)
        v_spec = pl.BlockSpec((hp, n_kv, tk, LANES), lambda h, i: (h, 0, 0, 0), pipeline_mode=one)
    else:
        assert kv_col0 % kv_cols == 0 and tk % kv_cols == 0
        n_kv, cb = 1, kv_col0 // kv_cols
        k_spec = pl.BlockSpec((hp, 1, 2, 2 * DA_HEAD_DIM, kv_cols), lambda h, i: (h, kv_block, 0, 0, cb))
        v_spec = pl.BlockSpec((hp, 1, kv_cols, LANES), lambda h, i: (h, kv_block, cb, 0))
    smem = pl.BlockSpec(memory_space=pltpu.SMEM)
    in_specs = [smem, smem,
                pl.BlockSpec((hp, tq, 2 * DA_HEAD_DIM), lambda h, i: (h, q0 + i, 0)),
                k_spec, v_spec,
                pl.BlockSpec((1, DA_V_DIM), lambda h, i: (0, 0))]
    args = (flag, lam, qh, kzb, vb, g_sub)
    body = functools.partial(_attn_body, n_kv=n_kv, post_scale=post_scale)
    aliases = {}
    if y_rest is not None:
        in_specs.append(pl.BlockSpec(memory_space=pl.ANY))
        args += (y_rest,)
        body = functools.partial(_attn_body_keep_rest, n_kv=n_kv, post_scale=post_scale)
        aliases = {len(args) - 1: 0}
    return pl.pallas_call(
        body,
        grid=(nh // hp, q_rows // tq),
        in_specs=in_specs,
        out_specs=pl.BlockSpec((tq, hp * DA_V_DIM), lambda h, i: (q0 + i, h)),
        out_shape=jax.ShapeDtypeStruct((qh.shape[1], nh * DA_V_DIM), F32),
        scratch_shapes=[pltpu.VMEM((2, tq, LANES), F32), pltpu.VMEM((2, tq, LANES), F32),
                        pltpu.VMEM((hp, tq, DA_V_DIM), F32)],
        input_output_aliases=aliases,
        compiler_params=_cparams("arbitrary", "arbitrary"),
        name="diff_attn",
    )(*args)


def _gelu_tanh(x):
    return 0.5 * x * (1.0 + jnp.tanh(math.sqrt(2.0 / math.pi) * (x + 0.044715 * (x * x * x))))


def _outffn_body(x_ref, yf_ref, yb_ref, z_ref, gssd_ref, ys5_ref, u_ref, d5_ref, gw_ref, gb_ref,
                 yda_ref, mod_ref, modp_ref, g2_ref, wo_ref, w1_ref, w3_ref, w2_ref, o_ref, xm_ref, hb_ref):
    i = pl.program_id(0)
    cur = i % 2
    prev = 1 - cur

    @pl.when(i == 0)
    def _():
        xm_ref[...] = jnp.zeros_like(xm_ref)
        hb_ref[...] = jnp.zeros_like(hb_ref)

    def finish():
        hp = hb_ref[prev]
        a1 = _dot(hp, w1_ref[...])
        yield
        a3 = _dot(hp, w3_ref[...])
        yield
        f = (_silu(a1) * a3).astype(BF16)
        yield
        o_ref[...] = xm_ref[prev] + modp_ref[0][5:6] * _dot(f, w2_ref[...])

    def prepare():
        mod = mod_ref[0]
        y = (yf_ref[...] + yb_ref[...]) * _silu(z_ref[...])
        y_ssd = y * lax.rsqrt(jnp.mean(y * y, axis=-1, keepdims=True) + EPS) * gssd_ref[...]
        yield
        y5 = _gelu_tanh(ys5_ref[...] + d5_ref[...] * u_ref[...])
        yield
        y5 = y5 * jax.nn.sigmoid(_dot(y5.astype(BF16), gw_ref[...].astype(BF16)) + gb_ref[...])
        yield
        o = _dot(y_ssd.astype(BF16), wo_ref[0:256, :])
        yield
        o += _dot(y5.astype(BF16), wo_ref[256:512, :])
        yield
        o += _dot(yda_ref[...].astype(BF16), wo_ref[512:1024, :])
        yield
        x = x_ref[...] + mod[2:3] * o
        xm_ref[cur] = x
        yield
        h = x * lax.rsqrt(jnp.mean(x * x, axis=-1, keepdims=True) + EPS) * g2_ref[...] * (1.0 + mod[4:5]) \
            + mod[3:4]
        hb_ref[cur] = h.astype(BF16)

    _run(_interleave(finish(), prepare()))


def _outffn_call(xs, y_f, y_b, z, g_ssd, y_s5, u, d5, glu_w, glu_b, y_da, modl, g2, w_o, w1, w3, w2, layer, n_lat,
                 latent_only):
    t, d = xs.shape
    if latent_only:
        t = n_lat
    dff = w1.shape[2]
    layer_block = lambda i: (layer, 0, 0)
    tm = FFN_ROW_TILE
    assert n_lat % tm == 0 and t % tm == 0
    n_lat_tiles = n_lat // tm
    n_tiles = t // tm
    row = lambda i: (jnp.minimum(i, n_tiles - 1), 0)
    done = lambda i: (jnp.maximum(i - 1, 0), 0)
    const = lambda i: (0, 0)
    seg = lambda i: ((jnp.minimum(i, n_tiles - 1) >= n_lat_tiles).astype(jnp.int32), 0, 0)
    seg_done = lambda i: ((i - 1 >= n_lat_tiles).astype(jnp.int32), 0, 0)
    one = pl.Buffered(1)
    return pl.pallas_call(
        _outffn_body,
        grid=(n_tiles + 1,),
        in_specs=[pl.BlockSpec((tm, d), row),
                  pl.BlockSpec((tm, 256), row), pl.BlockSpec((tm, 256), row), pl.BlockSpec((tm, 256), row),
                  pl.BlockSpec((1, 256), const),
                  pl.BlockSpec((tm, 256), row), pl.BlockSpec((tm, 256), row),
                  pl.BlockSpec((1, 256), const),
                  pl.BlockSpec((256, 256), const), pl.BlockSpec((1, 256), const),
                  pl.BlockSpec((tm, DA_WIDTH), row),
                  pl.BlockSpec((1, 8, d), seg),
                  pl.BlockSpec((1, 8, d), seg_done),
                  pl.BlockSpec((1, d), const),
                  pl.BlockSpec((None, d, d), layer_block, pipeline_mode=one),
                  pl.BlockSpec((None, d, dff), layer_block, pipeline_mode=one),
                  pl.BlockSpec((None, d, dff), layer_block, pipeline_mode=one),
                  pl.BlockSpec((None, dff, d), layer_block, pipeline_mode=one)],
        out_specs=pl.BlockSpec((tm, d), done),
        out_shape=jax.ShapeDtypeStruct((t, d), F32),
        scratch_shapes=[pltpu.VMEM((2, tm, d), F32), pltpu.VMEM((2, tm, d), BF16)],
        compiler_params=_cparams("arbitrary"),
        name="out_proj_ffn",
    )(xs, y_f, y_b, z, g_ssd, y_s5, u, d5, glu_w, glu_b, y_da, modl, modl, g2, w_o, w1, w3, w2)


def _cast_body(w_ref, o_ref):
    o_ref[...] = w_ref[...].astype(o_ref.dtype)


def _to_bf16(w):
    nl, rows, cols = w.shape
    tr = _pick_tile(rows, (512, 704, 256))
    blk = pl.BlockSpec((1, tr, cols), lambda l, r: (l, r, 0))
    return pl.pallas_call(
        _cast_body,
        grid=(nl, rows // tr),
        in_specs=[blk],
        out_specs=blk,
        out_shape=jax.ShapeDtypeStruct(w.shape, BF16),
        compiler_params=_cparams("arbitrary", "arbitrary"),
        name="weights_to_bf16",
    )(w)


def _rope_tables(n_lat, n_ctx):
    pos = jnp.arange(n_lat)
    lane = jnp.arange(LANES)
    d = lane % DA_HEAD_DIM
    axis = d // (2 * ROPE_FREQS)
    half = (d % (2 * ROPE_FREQS)) // ROPE_FREQS
    freqs = ROPE_THETA ** (-jnp.arange(ROPE_FREQS, dtype=F32) / ROPE_FREQS)
    coord = jnp.where(axis[None, :] == 0, (pos // GRID_W)[:, None], (pos % GRID_W)[:, None]).astype(F32)
    ang = coord * freqs[d % ROPE_FREQS][None, :]
    cos, sin = jnp.cos(ang), jnp.sin(ang)
    sa = jnp.where(half[None, :] == 0, -sin, 0.0)
    sb = jnp.where(half[None, :] == 1, sin, 0.0)
    pad = lambda a, v: jnp.concatenate([a, jnp.full((n_ctx, LANES), v, F32)], axis=0)
    return pad(cos, 1.0), pad(sa, 0.0), pad(sb, 0.0)


def _lane_row(vals, width=LANES):
    vals = vals.reshape(-1).astype(F32)
    return jnp.zeros((1, width), F32).at[0, :vals.shape[0]].set(vals)


def kernel(x, c, ctx, c_ctx, w_mod, b_mod, norm1, norm2, w_in, w_out, ssd_conv_w, ssd_conv_b, ssd_a_log, ssd_dt_bias, ssd_d, ssd_norm, s5_lam_re, s5_lam_im, s5_log_step, s5_b_re, s5_b_im, s5_c_re, s5_c_im, s5_d, s5_glu_w, s5_glu_b, da_q_norm, da_k_norm, da_lambda, da_sub_norm, ffn_w1, ffn_w3, ffn_w2):
    depth = w_mod.shape[0]
    bsz, n_lat, d = x.shape
    n_ctx = ctx.shape[1]
    t = n_ctx + n_lat
    tk = _pick_tile(t, (ATTN_TK, 1024, 640, 512, 256))
    assert bsz == 1 and n_ctx % ROW_TILE == 0 and n_lat % ROW_TILE == 0 and tk % ROW_TILE == 0
    assert n_ctx % (S5_CHUNK * S5_SCAN_TILE) == 0 and n_lat % (S5_CHUNK * S5_SCAN_TILE) == 0
    assert n_lat % n_ctx == 0 and tk % n_ctx == 0
    n_lat_tiles = n_lat // ROW_TILE

    xs = jnp.concatenate([x[0], ctx[0]], axis=0)
    cv = jnp.zeros((8, d), F32).at[0].set(c[0]).at[1].set(c_ctx)
    mod_all = _mod_call(cv, w_mod, b_mod)
    cos_t, sa_t, sb_t = _rope_tables(n_lat, n_ctx)
    gidx = jnp.arange(DA_WIDTH) // DA_HEAD_DIM
    gmat = (gidx[:, None] == gidx[None, :]).astype(BF16) * (1.0 / DA_HEAD_DIM)
    s5_w, s5_spread = _s5_weights(s5_lam_re, s5_lam_im, s5_log_step, s5_b_re, s5_b_im, s5_c_re, s5_c_im)
    w_out_b, w1_b, w3_b, w2_b = _to_bf16(w_out), _to_bf16(ffn_w1), _to_bf16(ffn_w3), _to_bf16(ffn_w2)

    for i in range(depth):
        lam_init = 0.8 - 0.6 * math.exp(-0.3 * i)
        modl = jnp.zeros((2, 8, d), F32).at[:, :6].set(mod_all[i, :2].reshape(2, 6, d))

        wi = w_in[i]
        w_r = jnp.concatenate([wi[:, 0:768], wi[:, 776:1032], wi[:, 1032:2568], wi[:, 768:776],
                               jnp.zeros((d, IN_PAD - 2568), F32)], axis=1).astype(BF16)
        gq = jnp.tile(da_q_norm[i], DA_WIDTH // DA_HEAD_DIM)[None, :]
        gk = jnp.tile(da_k_norm[i], DA_WIDTH // DA_HEAD_DIM)[None, :]
        z, xbc_raw, u, dt_raw, qh, kzb, vb = _inproj_call(
            xs, norm1[i][None, :], modl, w_r, gq, gk, gmat, cos_t, sa_t, sb_t, n_lat_tiles, tk)

        dt_r = dt_raw[:, :8].T
        cw = jnp.zeros((8, SSD_CONV_DIM), F32).at[:3].set(ssd_conv_w[i])
        a_neg = -jnp.exp(ssd_a_log[i])
        bias_c = _lane_row(ssd_dt_bias[i])
        a_c = _lane_row(a_neg)
        bias_r = jnp.broadcast_to(ssd_dt_bias[i].reshape(8, 1), (8, SSD_STEP_CHUNKS * SSD_CHUNK))
        a_r = jnp.broadcast_to(a_neg.reshape(8, 1), (8, SSD_STEP_CHUNKS * SSD_CHUNK))
        dsk = jnp.repeat(ssd_d[i], SSD_HEAD_DIM)[None, :]
        y_f, y_b = _ssd_call(xbc_raw, dt_raw, dt_r, cw, ssd_conv_b[i][None, :], bias_c, a_c, bias_r, a_r, dsk,
                             n_lat)

        y_s5 = _s5_mix(u, *(w[i] for w in s5_w), s5_spread, n_lat)

        lf = da_lambda[i]
        lam = (jnp.exp(jnp.sum(lf[0] * lf[1])) - jnp.exp(jnp.sum(lf[2] * lf[3])) + lam_init).reshape(1)
        g_sub = da_sub_norm[i][None, :]
        score_bound = LOG2E * math.sqrt(DA_HEAD_DIM) * jnp.max(jnp.abs(da_q_norm[i])) * jnp.max(jnp.abs(da_k_norm[i]))
        flag = (score_bound <= SCORE_LOG2_LIMIT).astype(jnp.int32).reshape(1)
        y_da = _attn_call(flag, lam, qh, kzb, vb, g_sub, 1.0 - lam_init, n_lat, 0, None, None, None)
        last = i == depth - 1
        if not last:
            y_da = _attn_call(flag, lam, qh, kzb, vb, g_sub, 1.0 - lam_init, n_ctx, n_lat,
                              n_lat // tk, n_ctx, n_lat % tk, y_rest=y_da)

        xs = _outffn_call(xs, y_f, y_b, z, ssd_norm[i][None, :], y_s5, u, s5_d[i][None, :], s5_glu_w[i],
                          s5_glu_b[i][None, :], y_da, modl, norm2[i][None, :], w_out_b, w1_b, w3_b, w2_b, i,
                          n_lat, latent_only=last)
    return xs[None]
```

```python
import functools
import math

import jax
import jax.numpy as jnp
from jax import lax
from jax.experimental import pallas as pl
from jax.experimental.pallas import tpu as pltpu

F32 = jnp.float32
BF16 = jnp.bfloat16
HI = lax.Precision.HIGHEST

EPS = 1e-6
GRID_W = 64
ROPE_THETA = 10000.0
ROPE_FREQS = 8

SSD_HEADS = 4
SSD_HEAD_DIM = 64
SSD_INNER = 256
SSD_GROUPS = 2
SSD_STATE = 64
SSD_CHUNK = 128
SSD_STEP_CHUNKS = 2
SSD_CONV_DIM = 512

S5_WIDTH = 256
S5_GROUP = 16
S5_GROUPS = 16
S5_STATE = 64
S5_CHUNK = 16
S5_HALF_GROUPS = S5_GROUPS // 2
S5_HALF_STATE = S5_HALF_GROUPS * S5_STATE
S5_SCAN_TILE = 16
S5_TOEPLITZ_BLOCKS = 2 * S5_CHUNK - 1
S5_KCOLS = 512

DA_HEADS = 8
DA_HEAD_DIM = 32
DA_V_DIM = 64
DA_WIDTH = 512
ATTN_HEADS_PER_STEP = 2
ATTN_TQ = 2048
ATTN_TK = 1280
LOG2E = 1.4426950408889634
SCORE_LOG2_LIMIT = 100.0

LANES = 128
ROW_TILE = 256
FFN_ROW_TILE = 256
IN_PAD = 2688
VMEM_LIMIT = 56 * 1024 * 1024


def _cparams(*sem):
    return pltpu.CompilerParams(dimension_semantics=sem, vmem_limit_bytes=VMEM_LIMIT)


def _dot(a, b, precision=None):
    return jnp.dot(a, b, preferred_element_type=F32, precision=precision)


def _dot_nt(a, b, precision=None):
    return lax.dot_general(a, b, (((1,), (1,)), ((), ())), preferred_element_type=F32,
                           precision=precision)


def _dot_tn(a, b, precision=None):
    return lax.dot_general(a, b, (((0,), (0,)), ((), ())), preferred_element_type=F32,
                           precision=precision)


def _silu(x):
    return x * jax.nn.sigmoid(x)


def _pick_tile(n, candidates):
    for c in candidates:
        if n % c == 0:
            return c
    return n


def _interleave(*stages):
    results = [None] * len(stages)
    live = dict(enumerate(stages))
    while live:
        for k in list(live):
            try:
                next(live[k])
            except StopIteration as done:
                results[k] = done.value
                del live[k]
        yield
    return results


def _run(stage):
    while True:
        try:
            next(stage)
        except StopIteration as done:
            return done.value


def _segment_order(i, n_lat, n_all, reverse):
    n_ctx = n_all - n_lat
    if not reverse:
        return jnp.where(i < n_ctx, n_lat + i, i - n_ctx)
    return jnp.where(i < n_ctx, n_all - 1 - i, n_lat - 1 - (i - n_ctx))


def _mod_body(cv_ref, w_ref, b_ref, o_ref):
    o_ref[0] = _dot(_silu(cv_ref[...]), w_ref[0], HI) + b_ref[0]


def _mod_call(cv, w_mod, b_mod):
    depth, d, n = w_mod.shape
    tn = 1024
    return pl.pallas_call(
        _mod_body,
        grid=(depth, n // tn),
        in_specs=[pl.BlockSpec((8, d), lambda l, j: (0, 0)),
                  pl.BlockSpec((1, d, tn), lambda l, j: (l, 0, j)),
                  pl.BlockSpec((1, 1, tn), lambda l, j: (l, 0, j))],
        out_specs=pl.BlockSpec((1, 8, tn), lambda l, j: (l, 0, j)),
        out_shape=jax.ShapeDtypeStruct((depth, 8, n), F32),
        compiler_params=_cparams("arbitrary", "arbitrary"),
        name="adaln_mod",
    )(cv, w_mod, b_mod.reshape(depth, 1, n))


def _inproj_project(x_ref, g_ref, mod_ref, w_ref, z_ref, xbc_ref, u_ref, dt_ref, qkv_ref):
    x = x_ref[...]
    mod = mod_ref[0]
    ms = jnp.mean(x * x, axis=-1, keepdims=True)
    h = x * lax.rsqrt(ms + EPS) * g_ref[...] * (1.0 + mod[1:2]) + mod[0:1]
    p = _dot(h.astype(BF16), w_ref[...])
    z_ref[...] = p[:, 0:256]
    xbc_ref[...] = p[:, 256:768]
    u_ref[...] = p[:, 768:1024]
    dt_ref[...] = p[:, 2560:2688]
    qkv_ref[...] = p[:, 1024:2560]


def _inproj_attn_operands(qkv_ref, gq_ref, gk_ref, gm_ref, cos_ref, sa_ref, sb_ref, q_ref, k_ref, v_ref):
    p = qkv_ref[...]
    rows = p.shape[0]
    gm = gm_ref[...]
    cos = cos_ref[...]
    sa = sa_ref[...]
    sb = sb_ref[...]

    def norm_rope(t, gain, scale):
        ms32 = _dot((t * t).astype(BF16), gm)
        tn = t * lax.rsqrt(ms32 + EPS) * gain
        outs = []
        for j in range(DA_WIDTH // LANES):
            tb = tn[:, j * LANES:(j + 1) * LANES]
            ob = tb * cos + pltpu.roll(tb, LANES - ROPE_FREQS, 1) * sa + pltpu.roll(tb, ROPE_FREQS, 1) * sb
            outs.append(ob * scale)
        return jnp.concatenate(outs, axis=1)

    qt = norm_rope(p[:, 0:512], gq_ref[...], LOG2E * DA_HEAD_DIM ** -0.5).T.astype(BF16)
    kn = norm_rope(p[:, 512:1024], gk_ref[...], 1.0).astype(BF16)
    vt = p[:, 1024:1536].T.astype(BF16)
    srow = lax.broadcasted_iota(jnp.int32, (LANES - DA_V_DIM, rows), 0)
    ones_row = jnp.where(srow == 0, 1.0, 0.0).astype(BF16)
    zeros_q = jnp.zeros((DA_HEAD_DIM, rows), BF16)
    for hd in range(DA_HEADS):
        lo_, mid, hi_ = hd * DA_V_DIM, hd * DA_V_DIM + DA_HEAD_DIM, (hd + 1) * DA_V_DIM
        k_ref[hd, 0] = kn[:, lo_:hi_]
        q_ref[hd, 0, 0:DA_HEAD_DIM, :] = qt[lo_:mid]
        q_ref[hd, 0, DA_HEAD_DIM:, :] = zeros_q
        q_ref[hd, 1, 0:DA_HEAD_DIM, :] = zeros_q
        q_ref[hd, 1, DA_HEAD_DIM:, :] = qt[mid:hi_]
        v_ref[hd, 0, 0:DA_V_DIM, :] = vt[lo_:hi_]
        v_ref[hd, 0, DA_V_DIM:, :] = ones_row


def _inproj_body(x_ref, g_ref, mod_ref, w_ref, gq_ref, gk_ref, gm_ref, cos_ref, sa_ref, sb_ref,
                 z_ref, xbc_ref, u_ref, dt_ref, q_ref, k_ref, v_ref, qkv_a_ref, qkv_b_ref):
    i = pl.program_id(0)

    @pl.when(i == 0)
    def _():
        qkv_b_ref[...] = jnp.zeros_like(qkv_b_ref)

    def step(cur_ref, prev_ref):
        _inproj_attn_operands(prev_ref, gq_ref, gk_ref, gm_ref, cos_ref, sa_ref, sb_ref, q_ref, k_ref, v_ref)
        _inproj_project(x_ref, g_ref, mod_ref, w_ref, z_ref, xbc_ref, u_ref, dt_ref, cur_ref)

    @pl.when(i % 2 == 0)
    def _():
        step(qkv_a_ref, qkv_b_ref)

    @pl.when(i % 2 == 1)
    def _():
        step(qkv_b_ref, qkv_a_ref)


def _inproj_call(xs, g1, modl, w_r, gq, gk, gmat, cos_t, sa_t, sb_t, n_lat_tiles, tk):
    t, d = xs.shape
    tm = ROW_TILE
    n_tiles = t // tm
    per = tk // tm
    row = lambda i: (jnp.minimum(i, n_tiles - 1), 0)
    done = lambda i: (jnp.maximum(i - 1, 0), 0)
    const = lambda i: (0, 0)
    seg = lambda i: ((jnp.minimum(i, n_tiles - 1) >= n_lat_tiles).astype(jnp.int32), 0, 0)
    flat = [(256, F32), (512, F32), (256, F32), (LANES, F32)]

    def k_block(i):
        j = jnp.maximum(i - 1, 0)
        return (0, j // per, j % per, 0)

    def v_block(i):
        j = jnp.maximum(i - 1, 0)
        return (0, j // per, 0, j % per)

    return pl.pallas_call(
        _inproj_body,
        grid=(n_tiles + 1,),
        in_specs=[pl.BlockSpec((tm, d), row),
                  pl.BlockSpec((1, d), const),
                  pl.BlockSpec((1, 8, d), seg),
                  pl.BlockSpec((d, IN_PAD), const),
                  pl.BlockSpec((1, DA_WIDTH), const),
                  pl.BlockSpec((1, DA_WIDTH), const),
                  pl.BlockSpec((DA_WIDTH, DA_WIDTH), const),
                  pl.BlockSpec((tm, LANES), done),
                  pl.BlockSpec((tm, LANES), done),
                  pl.BlockSpec((tm, LANES), done)],
        out_specs=[pl.BlockSpec((tm, w), row) for w, _ in flat]
                  + [pl.BlockSpec((DA_HEADS, 2, 2 * DA_HEAD_DIM, tm), lambda i: (0, 0, 0, jnp.maximum(i - 1, 0))),
                     pl.BlockSpec((DA_HEADS, 1, tm, 2 * DA_HEAD_DIM), k_block),
                     pl.BlockSpec((DA_HEADS, 1, LANES, tm), v_block)],
        out_shape=[jax.ShapeDtypeStruct((t, w), dt) for w, dt in flat]
                  + [jax.ShapeDtypeStruct((DA_HEADS, 2, 2 * DA_HEAD_DIM, t), BF16),
                     jax.ShapeDtypeStruct((DA_HEADS, t // tk, tk, 2 * DA_HEAD_DIM), BF16),
                     jax.ShapeDtypeStruct((DA_HEADS, t // tk, LANES, tk), BF16)],
        scratch_shapes=[pltpu.VMEM((tm, 3 * DA_WIDTH), F32)] * 2,
        compiler_params=_cparams("arbitrary"),
        name="in_proj",
    )(xs, g1, modl, w_r, gq, gk, gmat, cos_t, sa_t, sb_t)


def _softplus(x):
    return jnp.maximum(x, 0.0) + jnp.log1p(jnp.exp(-jnp.abs(x)))


def _ssd_block(c, n_lat, n_blocks, reverse, xc_ref, xp_ref, xn_ref, dtc_ref, dtr_ref, cw_ref, cb_ref,
               bias_c_ref, a_c_ref, bias_r_ref, a_r_ref, st_ref):
    L = SSD_CHUNK
    rows = xc_ref.shape[0]
    x = xc_ref[...]
    seg_first = jnp.logical_or(c == 0, c == n_lat)
    seg_last = jnp.logical_or(c == n_lat - 1, c == n_blocks - 1)
    prow = jnp.where(seg_first, 0.0, xp_ref[7:8, :])
    nrow = jnp.where(seg_last, 0.0, xn_ref[0:1, :])
    ridx = lax.broadcasted_iota(jnp.int32, (rows, 1), 0)
    xprev = jnp.where(ridx == 0, prow, pltpu.roll(x, 1, 0))
    xnext = jnp.where(ridx == rows - 1, nrow, pltpu.roll(x, rows - 1, 0))
    cw = cw_ref[...]
    conv = xprev * cw[0:1] + x * cw[1:2] + xnext * cw[2:3] + cb_ref[...]
    xbc = _silu(conv)

    dt_c = _softplus(dtc_ref[...] + bias_c_ref[...])
    dt_r = _softplus(dtr_ref[...] + bias_r_ref[...])
    adt_c = dt_c * a_c_ref[...]
    adt_r = dt_r * a_r_ref[...]

    li = lax.broadcasted_iota(jnp.int32, (L, L), 0)
    si = lax.broadcasted_iota(jnp.int32, (L, L), 1)
    mask = (li <= si) if reverse else (li >= si)
    mask_t = (li >= si) if reverse else (li <= si)
    n_sub = rows // L
    yield
    parts = yield from _interleave(*[
        _ssd_chunk_local(xbc[s * L:(s + 1) * L], dt_c[s * L:(s + 1) * L], adt_c[s * L:(s + 1) * L],
                         adt_r[:, s * L:(s + 1) * L], mask, mask_t, reverse) for s in range(n_sub)])
    ys = [None] * n_sub
    for s in (reversed(range(n_sub)) if reverse else range(n_sub)):
        ys[s] = _ssd_chunk_state(*parts[s], st_ref, reverse)
        yield
    return jnp.concatenate(ys, axis=0), xbc[:, :SSD_INNER]


def _split3(x):
    hi = x.astype(BF16)
    r1 = x - hi.astype(F32)
    mid = r1.astype(BF16)
    lo = (r1 - mid.astype(F32)).astype(BF16)
    return hi, mid, lo


def _ssd_chunk_local(xbc, dt_c, adt_c, adt_r, mask, mask_t, reverse):
    L = SSD_CHUNK
    hpg = SSD_HEADS // SSD_GROUPS
    off = SSD_HEADS if reverse else 0
    m_b, mt_b = mask.astype(BF16), mask_t.astype(BF16)
    cs_c = sum(_dot(m_b, part) for part in _split3(adt_c))
    cs_r = sum(_dot(part, mt_b) for part in _split3(adt_r))
    edge = 0 if reverse else L - 1
    tot = cs_c[edge:edge + 1, :]
    yield

    kk = lax.broadcasted_iota(jnp.int32, (LANES, SSD_INNER), 0)
    jj = lax.broadcasted_iota(jnp.int32, (LANES, SSD_INNER), 1)
    spread = (kk == off + jj // SSD_HEAD_DIM).astype(BF16)
    cols = jnp.concatenate([dt_c, jnp.exp(cs_c), jnp.exp(tot - cs_c)], axis=0)
    c_hi = cols.astype(BF16)
    c_lo = (cols - c_hi.astype(F32)).astype(BF16)
    wide = _dot(c_hi, spread) + _dot(c_lo, spread)
    dt_w, ecs_w, dst_w = wide[0:L], wide[L:2 * L], wide[2 * L:3 * L]
    yield

    xs = xbc[:, :SSD_INNER]
    b_all = xbc[:, SSD_INNER:SSD_INNER + LANES]
    c_all = xbc[:, SSD_INNER + LANES:SSD_INNER + 2 * LANES]
    xd = xs * dt_w
    xd_b = xd.astype(BF16)
    b_b = b_all.astype(BF16)
    c_b = c_all.astype(BF16)
    lane = lax.broadcasted_iota(jnp.int32, (L, LANES), 1)
    col = lax.broadcasted_iota(jnp.int32, (L, SSD_INNER), 1)

    y = None
    for g in range(SSD_GROUPS):
        c_g = jnp.where(lane // SSD_STATE == g, c_b, jnp.zeros_like(c_b))
        gmat = _dot_nt(c_g, b_b)
        for hh in range(hpg):
            h = g * hpg + hh
            dec = jnp.exp(jnp.where(mask, cs_c[:, off + h:off + h + 1] - cs_r[off + h:off + h + 1, :], -jnp.inf))
            xd_h = jnp.where(col // SSD_HEAD_DIM == h, xd_b, jnp.zeros_like(xd_b))
            part = _dot((gmat * dec).astype(BF16), xd_h)
            y = part if y is None else y + part
            yield
    return y, c_b, ecs_w, b_all.T.astype(BF16), (xd * dst_w).astype(BF16)


def _ssd_chunk_state(y_local, c_b, ecs_w, bt_b, xds_b, st_ref, reverse):
    edge = 0 if reverse else SSD_CHUNK - 1
    y = y_local + ecs_w * _dot(c_b, st_ref[...].astype(BF16))
    kr = lax.broadcasted_iota(jnp.int32, (LANES, SSD_INNER), 0)
    kc = lax.broadcasted_iota(jnp.int32, (LANES, SSD_INNER), 1)
    block = kr // SSD_STATE == kc // (SSD_HEADS // SSD_GROUPS * SSD_HEAD_DIM)
    st_ref[...] = st_ref[...] * ecs_w[edge:edge + 1, :] + jnp.where(block, _dot(bt_b, xds_b), 0.0)
    return y


def _ssd_body(n_lat, n_chunks, *refs):
    fwd_in, bwd_in = refs[0:5], refs[5:10]
    cw_ref, cb_ref, bias_c_ref, a_c_ref, bias_r_ref, a_r_ref, dsk_ref, yf_ref, yb_ref, st_ref = refs[10:]
    i = pl.program_id(0)

    @pl.when(i == 0)
    def _():
        st_ref[...] = jnp.zeros_like(st_ref)

    shared = (cw_ref, cb_ref, bias_c_ref, a_c_ref, bias_r_ref, a_r_ref)
    (y_f, xs), (y_b, _) = _run(_interleave(
        _ssd_block(_segment_order(i, n_lat, n_chunks, False), n_lat, n_chunks, False, *fwd_in, *shared,
                   st_ref.at[0]),
        _ssd_block(_segment_order(i, n_lat, n_chunks, True), n_lat, n_chunks, True, *bwd_in, *shared,
                   st_ref.at[1])))
    yf_ref[...] = y_f + dsk_ref[...] * xs
    yb_ref[...] = y_b


def _ssd_call(xbc_raw, dt_c, dt_r, cw, cb, bias_c, a_c, bias_r, a_r, dsk, n_lat_rows):
    t = xbc_raw.shape[0]
    L = SSD_STEP_CHUNKS * SSD_CHUNK
    assert t % L == 0 and n_lat_rows % L == 0
    n_chunks = t // L
    n_lat = n_lat_rows // L
    sub = L // 8
    n_sub = t // 8
    const = lambda i: (0, 0)

    def chunk_specs(reverse):
        cidx = functools.partial(_segment_order, n_lat=n_lat, n_all=n_chunks, reverse=reverse)
        return [pl.BlockSpec((L, SSD_CONV_DIM), lambda i: (cidx(i), 0)),
                pl.BlockSpec((8, SSD_CONV_DIM), lambda i: (jnp.maximum(cidx(i) * sub - 1, 0), 0)),
                pl.BlockSpec((8, SSD_CONV_DIM), lambda i: (jnp.minimum((cidx(i) + 1) * sub, n_sub - 1), 0)),
                pl.BlockSpec((L, LANES), lambda i: (cidx(i), 0)),
                pl.BlockSpec((8, L), lambda i: (0, cidx(i)))]

    def out_spec(reverse):
        cidx = functools.partial(_segment_order, n_lat=n_lat, n_all=n_chunks, reverse=reverse)
        return pl.BlockSpec((L, SSD_INNER), lambda i: (cidx(i), 0))

    chunk_args = (xbc_raw, xbc_raw, xbc_raw, dt_c, dt_r)
    return pl.pallas_call(
        functools.partial(_ssd_body, n_lat, n_chunks),
        grid=(n_chunks,),
        in_specs=chunk_specs(False) + chunk_specs(True)
                 + [pl.BlockSpec((8, SSD_CONV_DIM), const),
                    pl.BlockSpec((1, SSD_CONV_DIM), const),
                    pl.BlockSpec((1, LANES), const),
                    pl.BlockSpec((1, LANES), const),
                    pl.BlockSpec((8, L), const),
                    pl.BlockSpec((8, L), const),
                    pl.BlockSpec((1, SSD_INNER), const)],
        out_specs=[out_spec(False), out_spec(True)],
        out_shape=[jax.ShapeDtypeStruct((t, SSD_INNER), F32)] * 2,
        scratch_shapes=[pltpu.VMEM((2, SSD_GROUPS * SSD_STATE, SSD_INNER), F32)],
        compiler_params=_cparams("arbitrary"),
        name="ssd_scan",
    )(*chunk_args, *chunk_args, cw, cb, bias_c, a_c, bias_r, a_r, dsk)


def _cmul(ar, ai, br, bi):
    return ar * br - ai * bi, ar * bi + ai * br


def _s5_step_pair(u_ref, a):
    return jnp.concatenate([u_ref[:, 2 * a, :], u_ref[:, 2 * a + 1, :]], axis=1).astype(BF16)


def _s5_in_body(u_ref, w_ref, lam_ref, efr_ref, efi_ref, ebr_ref, ebi_ref, wp_ref):
    hs = S5_HALF_STATE

    @pl.when(pl.program_id(1) == 0)
    def _():
        lam = lam_ref[0]
        wc = w_ref[0]
        rows = lambda a: jnp.broadcast_to(a, (LANES, hs))
        fr, fi, br, bi = rows(lam[0:1]), rows(lam[1:2]), rows(lam[2:3]), rows(lam[3:4])
        rgrp = lax.broadcasted_iota(jnp.int32, (LANES, LANES), 0) // S5_GROUP
        lgrp = lax.broadcasted_iota(jnp.int32, (LANES, LANES), 1) // S5_STATE
        planes = []
        for q in range(4):
            cq = wc[:, q * S5_STATE:(q + 1) * S5_STATE]
            cq2 = jnp.concatenate([cq, cq], axis=1)
            planes.append(jnp.concatenate(
                [jnp.where(rgrp == 2 * g4 + lgrp, cq2, 0.0) for g4 in range(S5_HALF_GROUPS // 2)], axis=1))
        ar, ai = planes[0], planes[1]
        for i in reversed(range(S5_CHUNK)):
            wp_ref[i * LANES:(i + 1) * LANES, 0:hs] = ar.astype(BF16)
            wp_ref[i * LANES:(i + 1) * LANES, hs:2 * hs] = ai.astype(BF16)
            ar, ai = _cmul(ar, ai, fr, fi)
        ar, ai = planes[2], planes[3]
        for i in range(S5_CHUNK):
            wp_ref[i * LANES:(i + 1) * LANES, 2 * hs:3 * hs] = ar.astype(BF16)
            wp_ref[i * LANES:(i + 1) * LANES, 3 * hs:4 * hs] = ai.astype(BF16)
            ar, ai = _cmul(ar, ai, br, bi)

    e = None
    for a in range(S5_CHUNK // 2):
        part = _dot(_s5_step_pair(u_ref, a), wp_ref[2 * a * LANES:2 * (a + 1) * LANES, :])
        e = part if e is None else e + part
    efr_ref[...] = e[:, 0:hs]
    efi_ref[...] = e[:, hs:2 * hs]
    ebr_ref[...] = e[:, 2 * hs:3 * hs]
    ebi_ref[...] = e[:, 3 * hs:4 * hs]


def _s5_in_call(u3, w_b, lam_rows):
    nch = u3.shape[0]
    tn = _pick_tile(nch, (208, 80, 40))
    hs = S5_HALF_STATE
    col = lambda h, n: (n, h)
    return pl.pallas_call(
        _s5_in_body,
        grid=(2, nch // tn),
        in_specs=[pl.BlockSpec((tn, S5_CHUNK, LANES), lambda h, n: (n, 0, h)),
                  pl.BlockSpec((1, LANES, 4 * S5_STATE), lambda h, n: (h, 0, 0)),
                  pl.BlockSpec((1, 4, hs), lambda h, n: (h, 0, 0))],
        out_specs=[pl.BlockSpec((tn, hs), col)] * 4,
        out_shape=[jax.ShapeDtypeStruct((nch, 2 * hs), F32)] * 4,
        scratch_shapes=[pltpu.VMEM((S5_CHUNK * LANES, 4 * hs), BF16)],
        compiler_params=_cparams("arbitrary", "arbitrary"),
        name="s5_chunk_in",
    )(u3, w_b, lam_rows)


def _s5_scan_body(a_ref, efr_ref, efi_ref, ebr_ref, ebi_ref, hfr_ref, hfi_ref, hbr_ref, hbi_ref, st_ref):
    @pl.when(pl.program_id(0) == 0)
    def _():
        st_ref[...] = jnp.zeros_like(st_ref)

    row = lambda ref, j: ref[j:j + 1, :]
    afr, afi, abr, abi = row(a_ref, 0), row(a_ref, 1), row(a_ref, 2), row(a_ref, 3)
    fr, fi, br, bi = row(st_ref, 0), row(st_ref, 1), row(st_ref, 2), row(st_ref, 3)
    for j in range(S5_SCAN_TILE):
        hfr_ref[j:j + 1, :] = fr
        hfi_ref[j:j + 1, :] = fi
        fr, fi = afr * fr - afi * fi + row(efr_ref, j), afr * fi + afi * fr + row(efi_ref, j)
        jb = S5_SCAN_TILE - 1 - j
        hbr_ref[jb:jb + 1, :] = br
        hbi_ref[jb:jb + 1, :] = bi
        br, bi = abr * br - abi * bi + row(ebr_ref, jb), abr * bi + abi * br + row(ebi_ref, jb)
    st_ref[0:1, :] = fr
    st_ref[1:2, :] = fi
    st_ref[2:3, :] = br
    st_ref[3:4, :] = bi


def _s5_scan_call(a_pow, efr, efi, ebr, ebi, n_lat_tiles):
    nch, width = efr.shape
    n_tiles = nch // S5_SCAN_TILE
    blk = (S5_SCAN_TILE, width)
    fwd = lambda i: (_segment_order(i, n_lat_tiles, n_tiles, False), 0)
    bwd = lambda i: (_segment_order(i, n_lat_tiles, n_tiles, True), 0)
    shp = jax.ShapeDtypeStruct((nch, width), F32)
    return pl.pallas_call(
        _s5_scan_body,
        grid=(n_tiles,),
        in_specs=[pl.BlockSpec((4, width), lambda i: (0, 0)),
                  pl.BlockSpec(blk, fwd), pl.BlockSpec(blk, fwd),
                  pl.BlockSpec(blk, bwd), pl.BlockSpec(blk, bwd)],
        out_specs=[pl.BlockSpec(blk, fwd), pl.BlockSpec(blk, fwd),
                   pl.BlockSpec(blk, bwd), pl.BlockSpec(blk, bwd)],
        out_shape=[shp] * 4,
        scratch_shapes=[pltpu.VMEM((4, width), F32)],
        compiler_params=_cparams("arbitrary"),
        name="s5_state_scan",
    )(a_pow, efr, efi, ebr, ebi)


def _s5_out_body(u_ref, hfr_ref, hfi_ref, hbr_ref, hbi_ref, k_ref, t_ref, c_ref, lam_ref, y_ref, wq_ref, z_ref):
    hs = S5_HALF_STATE
    width = S5_CHUNK * LANES

    @pl.when(pl.program_id(1) == 0)
    def _():
        strip = _dot(k_ref[0].astype(BF16), t_ref[...])
        rgrp = lax.broadcasted_iota(jnp.int32, strip.shape, 0) // S5_GROUP
        cgrp = (lax.broadcasted_iota(jnp.int32, strip.shape, 1) % LANES) // S5_GROUP
        strip = jnp.where(rgrp == cgrp, strip, 0.0).astype(BF16)
        z_ref[0:LANES, :] = strip[:, LANES:]
        z_ref[LANES:, :] = strip[:, :-LANES]

        cc = c_ref[0]
        ogrp = lax.broadcasted_iota(jnp.int32, (S5_STATE, LANES), 1) // S5_GROUP

        def expand(q):
            cq = cc[q * S5_STATE:(q + 1) * S5_STATE]
            return jnp.concatenate([jnp.where(ogrp == g, cq, 0.0) for g in range(S5_HALF_GROUPS)], axis=0)

        lam = lam_ref[0]
        for d, order in ((0, range(S5_CHUNK)), (1, reversed(range(S5_CHUNK)))):
            lr, li = lam[2 * d], lam[2 * d + 1]
            a, b = expand(2 * d), expand(2 * d + 1)
            for j in order:
                a, b = a * lr + b * li, b * lr - a * li
                wq_ref[2 * d * hs:(2 * d + 1) * hs, j * LANES:(j + 1) * LANES] = a.astype(BF16)
                wq_ref[(2 * d + 1) * hs:(2 * d + 2) * hs, j * LANES:(j + 1) * LANES] = b.astype(BF16)

    h_all = jnp.concatenate([hfr_ref[...], hfi_ref[...], hbr_ref[...], hbi_ref[...]], axis=1).astype(BF16)
    y = _dot(h_all, wq_ref[...])
    for a in range(S5_CHUNK // 2):
        lo = (S5_CHUNK - 2 - 2 * a) * LANES
        y += _dot(_s5_step_pair(u_ref, a), z_ref[:, lo:lo + width])
    for j in range(S5_CHUNK):
        y_ref[:, j, :] = y[:, j * LANES:(j + 1) * LANES]


def _s5_out_call(u3, hfr, hfi, hbr, hbi, w_k, w_t, w_c, lam_cols):
    nch = u3.shape[0]
    tn = _pick_tile(nch, (208, 80, 40))
    hs = S5_HALF_STATE
    col = lambda h, n: (n, h)
    blk3 = pl.BlockSpec((tn, S5_CHUNK, LANES), lambda h, n: (n, 0, h))
    return pl.pallas_call(
        _s5_out_body,
        grid=(2, nch // tn),
        in_specs=[blk3] + [pl.BlockSpec((tn, hs), col)] * 4
                 + [pl.BlockSpec((1, LANES) + w_k.shape[2:], lambda h, n: (h, 0, 0)),
                    pl.BlockSpec(w_t.shape, lambda h, n: (0, 0)),
                    pl.BlockSpec((1, 4 * S5_STATE, LANES), lambda h, n: (h, 0, 0)),
                    pl.BlockSpec((1, 4, hs, LANES), lambda h, n: (h, 0, 0, 0))],
        out_specs=blk3,
        out_shape=jax.ShapeDtypeStruct(u3.shape, F32),
        scratch_shapes=[pltpu.VMEM((4 * hs, S5_CHUNK * LANES), BF16),
                        pltpu.VMEM((2 * LANES, (S5_TOEPLITZ_BLOCKS - 1) * LANES), BF16)],
        compiler_params=_cparams("arbitrary", "arbitrary"),
        name="s5_chunk_out",
    )(u3, hfr, hfi, hbr, hbi, w_k, w_t, w_c, lam_cols)


def _s5_weights(lam_re, lam_im, log_step, b_re, b_im, c_re, c_im):
    tc = S5_CHUNK
    nl = lam_re.shape[0]
    hg = S5_HALF_GROUPS
    ein = functools.partial(jnp.einsum, precision=HI)
    step = jnp.exp(log_step)[..., None]
    er, ei = lam_re * step, lam_im * step
    mag = jnp.exp(er)
    lbr, lbi = mag * jnp.cos(ei), mag * jnp.sin(ei)
    den = lam_re * lam_re + lam_im * lam_im
    nr, ni = lbr - 1.0, lbi
    cr, ci = (nr * lam_re + ni * lam_im) / den, (ni * lam_re - nr * lam_im) / den
    bbr = cr[..., None] * b_re - ci[..., None] * b_im
    bbi = cr[..., None] * b_im + ci[..., None] * b_re
    taus = jnp.arange(tc + 1, dtype=F32)[:, None, None]
    pmag = jnp.exp(er[:, :, None] * taus)
    pr, pi = pmag * jnp.cos(ei[:, :, None] * taus), pmag * jnp.sin(ei[:, :, None] * taus)

    wr = pr[:, :, :tc, :, :, None] * bbr[:, :, None] - pi[:, :, :tc, :, :, None] * bbi[:, :, None]
    wi = pr[:, :, :tc, :, :, None] * bbi[:, :, None] + pi[:, :, :tc, :, :, None] * bbr[:, :, None]
    kk = ein('ldgcp,ldtgpk->ldtgck', c_re, wr) - ein('ldgcp,ldtgpk->ldtgck', c_im, wi)
    kf, kb = kk[:, 0], kk[:, 1]
    seq = jnp.concatenate([kb[:, :0:-1], kf[:, :1] + kb[:, :1], kf[:, 1:]], axis=1)
    seq = seq.reshape(nl, S5_TOEPLITZ_BLOCKS, 2, hg, S5_GROUP, S5_GROUP)
    w_k = seq.transpose(0, 2, 3, 5, 1, 4).reshape(nl, 2, LANES, S5_TOEPLITZ_BLOCKS * S5_GROUP)
    w_k = jnp.pad(w_k, ((0, 0), (0, 0), (0, 0), (0, S5_KCOLS - S5_TOEPLITZ_BLOCKS * S5_GROUP)))
    r = jnp.arange(S5_KCOLS)
    s = jnp.arange(S5_TOEPLITZ_BLOCKS * LANES)
    w_t = ((r[:, None] // S5_GROUP == s[None, :] // LANES)
           & (r[:, None] % S5_GROUP == s[None, :] % S5_GROUP)).astype(BF16)

    planes_b = jnp.stack([bbr[:, 0], bbi[:, 0], bbr[:, 1], bbi[:, 1]], axis=1)
    planes_b = planes_b.reshape(nl, 4, 2, hg, S5_STATE, S5_GROUP)
    w_b = planes_b.transpose(0, 2, 3, 5, 1, 4).reshape(nl, 2, LANES, 4 * S5_STATE)
    planes_c = jnp.stack([c_re[:, 0], -c_im[:, 0], c_re[:, 1], -c_im[:, 1]], axis=1)
    planes_c = planes_c.reshape(nl, 4, 2, hg, S5_GROUP, S5_STATE)
    w_c = planes_c.transpose(0, 2, 1, 5, 3, 4).reshape(nl, 2, 4 * S5_STATE, LANES)

    halves = lambda a: a.reshape(nl, 2, S5_HALF_STATE)
    lam_rows = jnp.stack([halves(lbr[:, 0]), halves(lbi[:, 0]), halves(lbr[:, 1]), halves(lbi[:, 1])], axis=2)
    lam_cols = jnp.broadcast_to(lam_rows[..., None], lam_rows.shape + (LANES,))
    full = lambda a: a.reshape(nl, S5_GROUPS * S5_STATE)
    a_pow = jnp.stack([full(pr[:, 0, tc]), full(pi[:, 0, tc]), full(pr[:, 1, tc]), full(pi[:, 1, tc])], axis=1)
    return (w_b, w_k, w_c, lam_rows, lam_cols, a_pow), w_t


def _s5_mix(u, w_b, w_k, w_c, lam_rows, lam_cols, a_pow, w_t, n_lat):
    t = u.shape[0]
    nch = t // S5_CHUNK
    u3 = u.reshape(nch, S5_CHUNK, S5_WIDTH)
    es = _s5_in_call(u3, w_b, lam_rows)
    hs = _s5_scan_call(a_pow, *es, n_lat // (S5_CHUNK * S5_SCAN_TILE))
    return _s5_out_call(u3, *hs, w_k, w_t, w_c, lam_cols).reshape(t, S5_WIDTH)


def _attn_body(flag_ref, lam_ref, q_ref, k_ref, v_ref, g_ref, o_ref, acc_ref, m_ref, oh_ref, *, n_kv, post_scale):
    bounded = flag_ref[0] == 1

    def head(hh, carry):
        acc_ref[...] = jnp.zeros_like(acc_ref)

        @pl.when(bounded)
        def _():
            def step(b, carry2):
                k = k_ref[hh, b]
                vt = v_ref[hh, b]
                for c in range(2):
                    p = jnp.exp2(_dot(k, q_ref[hh, c])).astype(BF16)
                    acc_ref[c] += _dot(vt, p)
                return carry2
            lax.fori_loop(0, n_kv, step, 0)

        @pl.when(jnp.logical_not(bounded))
        def _():
            m_ref[...] = jnp.full_like(m_ref, -jnp.inf)

            def step(b, carry2):
                k = k_ref[hh, b]
                vt = v_ref[hh, b]
                for c in range(2):
                    s = _dot(k, q_ref[hh, c])
                    m_old = m_ref[c]
                    m_new = jnp.maximum(m_old, jnp.max(s, axis=0, keepdims=True))
                    p = jnp.exp2(s - m_new[0:1]).astype(BF16)
                    acc_ref[c] = jnp.exp2(m_old - m_new)[0:1] * acc_ref[c] + _dot(vt, p)
                    m_ref[c] = m_new
                return carry2
            lax.fori_loop(0, n_kv, step, 0)

        a0 = acc_ref[0]
        a1 = acc_ref[1]
        o = a0 / a0[DA_V_DIM:DA_V_DIM + 1] - lam_ref[0] * (a1 / a1[DA_V_DIM:DA_V_DIM + 1])
        vrow = lax.broadcasted_iota(jnp.int32, o.shape, 0) < DA_V_DIM
        o = jnp.where(vrow, o, 0.0)
        ms = jnp.sum(o * o, axis=0, keepdims=True) * (1.0 / DA_V_DIM)
        ot = (o * lax.rsqrt(ms + EPS)).T
        oh_ref[hh] = ot[:, :DA_V_DIM] * g_ref[...] * post_scale
        return carry

    lax.fori_loop(0, ATTN_HEADS_PER_STEP, head, 0)
    o_ref[...] = jnp.concatenate([oh_ref[hh] for hh in range(ATTN_HEADS_PER_STEP)], axis=1)


def _attn_body_keep_rest(flag_ref, lam_ref, q_ref, k_ref, v_ref, g_ref, rest_ref, *refs, **kw):
    del rest_ref
    _attn_body(flag_ref, lam_ref, q_ref, k_ref, v_ref, g_ref, *refs, **kw)


def _attn_call(flag, lam, qt, kb, vtb, g_sub, post_scale, q_rows, q_row0, kv_block, kv_cols, kv_col0, y_rest=None):
    nh = qt.shape[0]
    hp = ATTN_HEADS_PER_STEP
    tk = kb.shape[2]
    tq = _pick_tile(q_rows, (ATTN_TQ, 256))
    assert q_row0 % tq == 0
    q0 = q_row0 // tq
    if kv_block is None:
        n_kv = kb.shape[1]
        one = pl.Buffered(1)
        k_spec = pl.BlockSpec((hp, n_kv, tk, 2 * DA_HEAD_DIM), lambda h, i: (h, 0, 0, 0), pipeline_mode=one)
        v_spec = pl.BlockSpec((hp, n_kv, LANES, tk), lambda h, i: (h, 0, 0, 0), pipeline_mode=one)
    else:
        assert kv_col0 % kv_cols == 0 and tk % kv_cols == 0
        n_kv, cb = 1, kv_col0 // kv_cols
        k_spec = pl.BlockSpec((hp, 1, kv_cols, 2 * DA_HEAD_DIM), lambda h, i: (h, kv_block, cb, 0))
        v_spec = pl.BlockSpec((hp, 1, LANES, kv_cols), lambda h, i: (h, kv_block, 0, cb))
    smem = pl.BlockSpec(memory_space=pltpu.SMEM)
    in_specs = [smem, smem,
                pl.BlockSpec((hp, 2, 2 * DA_HEAD_DIM, tq), lambda h, i: (h, 0, 0, q0 + i)),
                k_spec, v_spec,
                pl.BlockSpec((1, DA_V_DIM), lambda h, i: (0, 0))]
    args = (flag, lam, qt, kb, vtb, g_sub)
    body = functools.partial(_attn_body, n_kv=n_kv, post_scale=post_scale)
    aliases = {}
    if y_rest is not None:
        in_specs.append(pl.BlockSpec(memory_space=pl.ANY))
        args += (y_rest,)
        body = functools.partial(_attn_body_keep_rest, n_kv=n_kv, post_scale=post_scale)
        aliases = {len(args) - 1: 0}
    return pl.pallas_call(
        body,
        grid=(nh // hp, q_rows // tq),
        in_specs=in_specs,
        out_specs=pl.BlockSpec((tq, hp * DA_V_DIM), lambda h, i: (q0 + i, h)),
        out_shape=jax.ShapeDtypeStruct((qt.shape[3], nh * DA_V_DIM), F32),
        scratch_shapes=[pltpu.VMEM((2, LANES, tq), F32), pltpu.VMEM((2, 8, tq), F32),
                        pltpu.VMEM((hp, tq, DA_V_DIM), F32)],
        input_output_aliases=aliases,
        compiler_params=_cparams("arbitrary", "arbitrary"),
        name="diff_attn",
    )(*args)


def _gelu_tanh(x):
    return 0.5 * x * (1.0 + jnp.tanh(math.sqrt(2.0 / math.pi) * (x + 0.044715 * (x * x * x))))


def _outffn_body(x_ref, yf_ref, yb_ref, z_ref, gssd_ref, ys5_ref, u_ref, d5_ref, gw_ref, gb_ref,
                 yda_ref, mod_ref, modp_ref, g2_ref, wo_ref, w1_ref, w3_ref, w2_ref, o_ref, xm_ref, hb_ref):
    i = pl.program_id(0)
    cur = i % 2
    prev = 1 - cur

    @pl.when(i == 0)
    def _():
        xm_ref[...] = jnp.zeros_like(xm_ref)
        hb_ref[...] = jnp.zeros_like(hb_ref)

    def finish():
        hp = hb_ref[prev]
        a1 = _dot(hp, w1_ref[...])
        yield
        a3 = _dot(hp, w3_ref[...])
        yield
        f = (_silu(a1) * a3).astype(BF16)
        yield
        o_ref[...] = xm_ref[prev] + modp_ref[0][5:6] * _dot(f, w2_ref[...])

    def prepare():
        mod = mod_ref[0]
        y = (yf_ref[...] + yb_ref[...]) * _silu(z_ref[...])
        y_ssd = y * lax.rsqrt(jnp.mean(y * y, axis=-1, keepdims=True) + EPS) * gssd_ref[...]
        yield
        y5 = _gelu_tanh(ys5_ref[...] + d5_ref[...] * u_ref[...])
        yield
        y5 = y5 * jax.nn.sigmoid(_dot(y5.astype(BF16), gw_ref[...].astype(BF16)) + gb_ref[...])
        yield
        o = _dot(y_ssd.astype(BF16), wo_ref[0:256, :])
        yield
        o += _dot(y5.astype(BF16), wo_ref[256:512, :])
        yield
        o += _dot(yda_ref[...].astype(BF16), wo_ref[512:1024, :])
        yield
        x = x_ref[...] + mod[2:3] * o
        xm_ref[cur] = x
        yield
        h = x * lax.rsqrt(jnp.mean(x * x, axis=-1, keepdims=True) + EPS) * g2_ref[...] * (1.0 + mod[4:5]) \
            + mod[3:4]
        hb_ref[cur] = h.astype(BF16)

    _run(_interleave(finish(), prepare()))


def _outffn_call(xs, y_f, y_b, z, g_ssd, y_s5, u, d5, glu_w, glu_b, y_da, modl, g2, w_o, w1, w3, w2, layer, n_lat,
                 latent_only):
    t, d = xs.shape
    if latent_only:
        t = n_lat
    dff = w1.shape[2]
    layer_block = lambda i: (layer, 0, 0)
    tm = FFN_ROW_TILE
    assert n_lat % tm == 0 and t % tm == 0
    n_lat_tiles = n_lat // tm
    n_tiles = t // tm
    row = lambda i: (jnp.minimum(i, n_tiles - 1), 0)
    done = lambda i: (jnp.maximum(i - 1, 0), 0)
    const = lambda i: (0, 0)
    seg = lambda i: ((jnp.minimum(i, n_tiles - 1) >= n_lat_tiles).astype(jnp.int32), 0, 0)
    seg_done = lambda i: ((i - 1 >= n_lat_tiles).astype(jnp.int32), 0, 0)
    one = pl.Buffered(1)
    return pl.pallas_call(
        _outffn_body,
        grid=(n_tiles + 1,),
        in_specs=[pl.BlockSpec((tm, d), row),
                  pl.BlockSpec((tm, 256), row), pl.BlockSpec((tm, 256), row), pl.BlockSpec((tm, 256), row),
                  pl.BlockSpec((1, 256), const),
                  pl.BlockSpec((tm, 256), row), pl.BlockSpec((tm, 256), row),
                  pl.BlockSpec((1, 256), const),
                  pl.BlockSpec((256, 256), const), pl.BlockSpec((1, 256), const),
                  pl.BlockSpec((tm, DA_WIDTH), row),
                  pl.BlockSpec((1, 8, d), seg),
                  pl.BlockSpec((1, 8, d), seg_done),
                  pl.BlockSpec((1, d), const),
                  pl.BlockSpec((None, d, d), layer_block, pipeline_mode=one),
                  pl.BlockSpec((None, d, dff), layer_block, pipeline_mode=one),
                  pl.BlockSpec((None, d, dff), layer_block, pipeline_mode=one),
                  pl.BlockSpec((None, dff, d), layer_block, pipeline_mode=one)],
        out_specs=pl.BlockSpec((tm, d), done),
        out_shape=jax.ShapeDtypeStruct((t, d), F32),
        scratch_shapes=[pltpu.VMEM((2, tm, d), F32), pltpu.VMEM((2, tm, d), BF16)],
        compiler_params=_cparams("arbitrary"),
        name="out_proj_ffn",
    )(xs, y_f, y_b, z, g_ssd, y_s5, u, d5, glu_w, glu_b, y_da, modl, modl, g2, w_o, w1, w3, w2)


def _cast_body(w_ref, o_ref):
    o_ref[...] = w_ref[...].astype(o_ref.dtype)


def _to_bf16(w):
    nl, rows, cols = w.shape
    tr = _pick_tile(rows, (512, 704, 256))
    blk = pl.BlockSpec((1, tr, cols), lambda l, r: (l, r, 0))
    return pl.pallas_call(
        _cast_body,
        grid=(nl, rows // tr),
        in_specs=[blk],
        out_specs=blk,
        out_shape=jax.ShapeDtypeStruct(w.shape, BF16),
        compiler_params=_cparams("arbitrary", "arbitrary"),
        name="weights_to_bf16",
    )(w)


def _rope_tables(n_lat, n_ctx):
    pos = jnp.arange(n_lat)
    lane = jnp.arange(LANES)
    d = lane % DA_HEAD_DIM
    axis = d // (2 * ROPE_FREQS)
    half = (d % (2 * ROPE_FREQS)) // ROPE_FREQS
    freqs = ROPE_THETA ** (-jnp.arange(ROPE_FREQS, dtype=F32) / ROPE_FREQS)
    coord = jnp.where(axis[None, :] == 0, (pos // GRID_W)[:, None], (pos % GRID_W)[:, None]).astype(F32)
    ang = coord * freqs[d % ROPE_FREQS][None, :]
    cos, sin = jnp.cos(ang), jnp.sin(ang)
    sa = jnp.where(half[None, :] == 0, -sin, 0.0)
    sb = jnp.where(half[None, :] == 1, sin, 0.0)
    pad = lambda a, v: jnp.concatenate([a, jnp.full((n_ctx, LANES), v, F32)], axis=0)
    return pad(cos, 1.0), pad(sa, 0.0), pad(sb, 0.0)


def _lane_row(vals, width=LANES):
    vals = vals.reshape(-1).astype(F32)
    return jnp.zeros((1, width), F32).at[0, :vals.shape[0]].set(vals)


def kernel(x, c, ctx, c_ctx, w_mod, b_mod, norm1, norm2, w_in, w_out, ssd_conv_w, ssd_conv_b, ssd_a_log, ssd_dt_bias, ssd_d, ssd_norm, s5_lam_re, s5_lam_im, s5_log_step, s5_b_re, s5_b_im, s5_c_re, s5_c_im, s5_d, s5_glu_w, s5_glu_b, da_q_norm, da_k_norm, da_lambda, da_sub_norm, ffn_w1, ffn_w3, ffn_w2):
    depth = w_mod.shape[0]
    bsz, n_lat, d = x.shape
    n_ctx = ctx.shape[1]
    t = n_ctx + n_lat
    tk = _pick_tile(t, (ATTN_TK, 1024, 640, 512, 256))
    assert bsz == 1 and n_ctx % ROW_TILE == 0 and n_lat % ROW_TILE == 0 and tk % ROW_TILE == 0
    assert n_ctx % (S5_CHUNK * S5_SCAN_TILE) == 0 and n_lat % (S5_CHUNK * S5_SCAN_TILE) == 0
    assert n_lat % n_ctx == 0 and tk % n_ctx == 0
    n_lat_tiles = n_lat // ROW_TILE

    xs = jnp.concatenate([x[0], ctx[0]], axis=0)
    cv = jnp.zeros((8, d), F32).at[0].set(c[0]).at[1].set(c_ctx)
    mod_all = _mod_call(cv, w_mod, b_mod)
    cos_t, sa_t, sb_t = _rope_tables(n_lat, n_ctx)
    gidx = jnp.arange(DA_WIDTH) // DA_HEAD_DIM
    gmat = (gidx[:, None] == gidx[None, :]).astype(BF16) * (1.0 / DA_HEAD_DIM)
    s5_w, s5_spread = _s5_weights(s5_lam_re, s5_lam_im, s5_log_step, s5_b_re, s5_b_im, s5_c_re, s5_c_im)
    w_out_b, w1_b, w3_b, w2_b = _to_bf16(w_out), _to_bf16(ffn_w1), _to_bf16(ffn_w3), _to_bf16(ffn_w2)

    for i in range(depth):
        lam_init = 0.8 - 0.6 * math.exp(-0.3 * i)
        modl = jnp.zeros((2, 8, d), F32).at[:, :6].set(mod_all[i, :2].reshape(2, 6, d))

        wi = w_in[i]
        w_r = jnp.concatenate([wi[:, 0:768], wi[:, 776:1032], wi[:, 1032:2568], wi[:, 768:776],
                               jnp.zeros((d, IN_PAD - 2568), F32)], axis=1).astype(BF16)
        gq = jnp.tile(da_q_norm[i], DA_WIDTH // DA_HEAD_DIM)[None, :]
        gk = jnp.tile(da_k_norm[i], DA_WIDTH // DA_HEAD_DIM)[None, :]
        z, xbc_raw, u, dt_raw, qh, kzb, vb = _inproj_call(
            xs, norm1[i][None, :], modl, w_r, gq, gk, gmat, cos_t, sa_t, sb_t, n_lat_tiles, tk)

        dt_r = dt_raw[:, :8].T
        cw = jnp.zeros((8, SSD_CONV_DIM), F32).at[:3].set(ssd_conv_w[i])
        a_neg = -jnp.exp(ssd_a_log[i])
        bias_c = _lane_row(ssd_dt_bias[i])
        a_c = _lane_row(a_neg)
        bias_r = jnp.broadcast_to(ssd_dt_bias[i].reshape(8, 1), (8, SSD_STEP_CHUNKS * SSD_CHUNK))
        a_r = jnp.broadcast_to(a_neg.reshape(8, 1), (8, SSD_STEP_CHUNKS * SSD_CHUNK))
        dsk = jnp.repeat(ssd_d[i], SSD_HEAD_DIM)[None, :]
        y_f, y_b = _ssd_call(xbc_raw, dt_raw, dt_r, cw, ssd_conv_b[i][None, :], bias_c, a_c, bias_r, a_r, dsk,
                             n_lat)

        y_s5 = _s5_mix(u, *(w[i] for w in s5_w), s5_spread, n_lat)

        lf = da_lambda[i]
        lam = (jnp.exp(jnp.sum(lf[0] * lf[1])) - jnp.exp(jnp.sum(lf[2] * lf[3])) + lam_init).reshape(1)
        g_sub = da_sub_norm[i][None, :]
        score_bound = LOG2E * math.sqrt(DA_HEAD_DIM) * jnp.max(jnp.abs(da_q_norm[i])) * jnp.max(jnp.abs(da_k_norm[i]))
        flag = (score_bound <= SCORE_LOG2_LIMIT).astype(jnp.int32).reshape(1)
        y_da = _attn_call(flag, lam, qh, kzb, vb, g_sub, 1.0 - lam_init, n_lat, 0, None, None, None)
        last = i == depth - 1
        if not last:
            y_da = _attn_call(flag, lam, qh, kzb, vb, g_sub, 1.0 - lam_init, n_ctx, n_lat,
                              n_lat // tk, n_ctx, n_lat % tk, y_rest=y_da)

        xs = _outffn_call(xs, y_f, y_b, z, ssd_norm[i][None, :], y_s5, u, s5_d[i][None, :], s5_glu_w[i],
                          s5_glu_b[i][None, :], y_da, modl, norm2[i][None, :], w_out_b, w1_b, w3_b, w2_b, i,
                          n_lat, latent_only=last)
    return xs[None]
```

```python
import functools
import math

import jax
import jax.numpy as jnp
from jax import lax
from jax.experimental import pallas as pl
from jax.experimental.pallas import tpu as pltpu

F32 = jnp.float32
BF16 = jnp.bfloat16
HI = lax.Precision.HIGHEST

EPS = 1e-6
GRID_W = 64
ROPE_THETA = 10000.0
ROPE_FREQS = 8

SSD_HEADS = 4
SSD_HEAD_DIM = 64
SSD_INNER = 256
SSD_GROUPS = 2
SSD_STATE = 64
SSD_CHUNK = 128
SSD_STEP_CHUNKS = 2
SSD_CONV_DIM = 512

S5_WIDTH = 256
S5_GROUP = 16
S5_GROUPS = 16
S5_STATE = 64
S5_CHUNK = 16
S5_HALF_GROUPS = S5_GROUPS // 2
S5_HALF_STATE = S5_HALF_GROUPS * S5_STATE
S5_SCAN_TILE = 16
S5_TOEPLITZ_BLOCKS = 2 * S5_CHUNK - 1
S5_KCOLS = 512

DA_HEADS = 8
DA_HEAD_DIM = 32
DA_V_DIM = 64
DA_WIDTH = 512
DA_VT_ROWS = 80
ATTN_HEADS_PER_STEP = 2
ATTN_TQ = 2048
ATTN_TK = 1280
LOG2E = 1.4426950408889634
SCORE_LOG2_LIMIT = 100.0

LANES = 128
ROW_TILE = 256
FFN_ROW_TILE = 256
IN_PAD = 2688
VMEM_LIMIT = 56 * 1024 * 1024


def _cparams(*sem):
    return pltpu.CompilerParams(dimension_semantics=sem, vmem_limit_bytes=VMEM_LIMIT)


def _dot(a, b, precision=None):
    return jnp.dot(a, b, preferred_element_type=F32, precision=precision)


def _dot_nt(a, b, precision=None):
    return lax.dot_general(a, b, (((1,), (1,)), ((), ())), preferred_element_type=F32,
                           precision=precision)


def _dot_tn(a, b, precision=None):
    return lax.dot_general(a, b, (((0,), (0,)), ((), ())), preferred_element_type=F32,
                           precision=precision)


def _silu(x):
    return x * jax.nn.sigmoid(x)


def _pick_tile(n, candidates):
    for c in candidates:
        if n % c == 0:
            return c
    return n


def _interleave(*stages):
    results = [None] * len(stages)
    live = dict(enumerate(stages))
    while live:
        for k in list(live):
            try:
                next(live[k])
            except StopIteration as done:
                results[k] = done.value
                del live[k]
        yield
    return results


def _run(stage):
    while True:
        try:
            next(stage)
        except StopIteration as done:
            return done.value


def _segment_order(i, n_lat, n_all, reverse):
    n_ctx = n_all - n_lat
    if not reverse:
        return jnp.where(i < n_ctx, n_lat + i, i - n_ctx)
    return jnp.where(i < n_ctx, n_all - 1 - i, n_lat - 1 - (i - n_ctx))


def _mod_body(cv_ref, w_ref, b_ref, o_ref):
    o_ref[0] = _dot(_silu(cv_ref[...]), w_ref[0], HI) + b_ref[0]


def _mod_call(cv, w_mod, b_mod):
    depth, d, n = w_mod.shape
    tn = 1024
    return pl.pallas_call(
        _mod_body,
        grid=(depth, n // tn),
        in_specs=[pl.BlockSpec((8, d), lambda l, j: (0, 0)),
                  pl.BlockSpec((1, d, tn), lambda l, j: (l, 0, j)),
                  pl.BlockSpec((1, 1, tn), lambda l, j: (l, 0, j))],
        out_specs=pl.BlockSpec((1, 8, tn), lambda l, j: (l, 0, j)),
        out_shape=jax.ShapeDtypeStruct((depth, 8, n), F32),
        compiler_params=_cparams("arbitrary", "arbitrary"),
        name="adaln_mod",
    )(cv, w_mod, b_mod.reshape(depth, 1, n))


def _inproj_project(x_ref, g_ref, mod_ref, w_ref, z_ref, xbc_ref, u_ref, dt_ref, qkv_ref):
    x = x_ref[...]
    mod = mod_ref[0]
    ms = jnp.mean(x * x, axis=-1, keepdims=True)
    h = x * lax.rsqrt(ms + EPS) * g_ref[...] * (1.0 + mod[1:2]) + mod[0:1]
    p = _dot(h.astype(BF16), w_ref[...])
    z_ref[...] = p[:, 0:256]
    xbc_ref[...] = p[:, 256:768]
    u_ref[...] = p[:, 768:1024]
    dt_ref[...] = p[:, 2560:2688]
    qkv_ref[...] = p[:, 1024:2560]


def _inproj_attn_operands(qkv_ref, gq_ref, gk_ref, gm_ref, cos_ref, sa_ref, sb_ref, q_ref, k_ref, v_ref):
    p = qkv_ref[...]
    rows = p.shape[0]
    gm = gm_ref[...]
    cos = cos_ref[...]
    sa = sa_ref[...]
    sb = sb_ref[...]

    def norm_rope(t, gain, scale):
        ms32 = _dot((t * t).astype(BF16), gm)
        tn = t * lax.rsqrt(ms32 + EPS) * gain
        outs = []
        for j in range(DA_WIDTH // LANES):
            tb = tn[:, j * LANES:(j + 1) * LANES]
            ob = tb * cos + pltpu.roll(tb, LANES - ROPE_FREQS, 1) * sa + pltpu.roll(tb, ROPE_FREQS, 1) * sb
            outs.append(ob * scale)
        return jnp.concatenate(outs, axis=1)

    qt = norm_rope(p[:, 0:512], gq_ref[...], LOG2E * DA_HEAD_DIM ** -0.5).T.astype(BF16)
    kn = norm_rope(p[:, 512:1024], gk_ref[...], 1.0).astype(BF16)
    vt = p[:, 1024:1536].T.astype(BF16)
    srow = lax.broadcasted_iota(jnp.int32, (DA_VT_ROWS - DA_V_DIM, rows), 0)
    ones_row = jnp.where(srow == 0, 1.0, 0.0).astype(BF16)
    zeros_q = jnp.zeros((DA_HEAD_DIM, rows), BF16)
    for hd in range(DA_HEADS):
        lo_, mid, hi_ = hd * DA_V_DIM, hd * DA_V_DIM + DA_HEAD_DIM, (hd + 1) * DA_V_DIM
        k_ref[hd, 0] = kn[:, lo_:hi_]
        q_ref[hd, 0, 0:DA_HEAD_DIM, :] = qt[lo_:mid]
        q_ref[hd, 0, DA_HEAD_DIM:, :] = zeros_q
        q_ref[hd, 1, 0:DA_HEAD_DIM, :] = zeros_q
        q_ref[hd, 1, DA_HEAD_DIM:, :] = qt[mid:hi_]
        v_ref[hd, 0, 0:DA_V_DIM, :] = vt[lo_:hi_]
        v_ref[hd, 0, DA_V_DIM:, :] = ones_row


def _inproj_body(x_ref, g_ref, mod_ref, w_ref, gq_ref, gk_ref, gm_ref, cos_ref, sa_ref, sb_ref,
                 z_ref, xbc_ref, u_ref, dt_ref, q_ref, k_ref, v_ref, qkv_a_ref, qkv_b_ref):
    i = pl.program_id(0)

    @pl.when(i == 0)
    def _():
        qkv_b_ref[...] = jnp.zeros_like(qkv_b_ref)

    def step(cur_ref, prev_ref):
        _inproj_attn_operands(prev_ref, gq_ref, gk_ref, gm_ref, cos_ref, sa_ref, sb_ref, q_ref, k_ref, v_ref)
        _inproj_project(x_ref, g_ref, mod_ref, w_ref, z_ref, xbc_ref, u_ref, dt_ref, cur_ref)

    @pl.when(i % 2 == 0)
    def _():
        step(qkv_a_ref, qkv_b_ref)

    @pl.when(i % 2 == 1)
    def _():
        step(qkv_b_ref, qkv_a_ref)


def _inproj_call(xs, g1, modl, w_r, gq, gk, gmat, cos_t, sa_t, sb_t, n_lat_tiles, tk):
    t, d = xs.shape
    tm = ROW_TILE
    n_tiles = t // tm
    per = tk // tm
    row = lambda i: (jnp.minimum(i, n_tiles - 1), 0)
    done = lambda i: (jnp.maximum(i - 1, 0), 0)
    const = lambda i: (0, 0)
    seg = lambda i: ((jnp.minimum(i, n_tiles - 1) >= n_lat_tiles).astype(jnp.int32), 0, 0)
    flat = [(256, F32), (512, F32), (256, F32), (LANES, F32)]

    def k_block(i):
        j = jnp.maximum(i - 1, 0)
        return (0, j // per, j % per, 0)

    def v_block(i):
        j = jnp.maximum(i - 1, 0)
        return (0, j // per, 0, j % per)

    return pl.pallas_call(
        _inproj_body,
        grid=(n_tiles + 1,),
        in_specs=[pl.BlockSpec((tm, d), row),
                  pl.BlockSpec((1, d), const),
                  pl.BlockSpec((1, 8, d), seg),
                  pl.BlockSpec((d, IN_PAD), const),
                  pl.BlockSpec((1, DA_WIDTH), const),
                  pl.BlockSpec((1, DA_WIDTH), const),
                  pl.BlockSpec((DA_WIDTH, DA_WIDTH), const),
                  pl.BlockSpec((tm, LANES), done),
                  pl.BlockSpec((tm, LANES), done),
                  pl.BlockSpec((tm, LANES), done)],
        out_specs=[pl.BlockSpec((tm, w), row) for w, _ in flat]
                  + [pl.BlockSpec((DA_HEADS, 2, 2 * DA_HEAD_DIM, tm), lambda i: (0, 0, 0, jnp.maximum(i - 1, 0))),
                     pl.BlockSpec((DA_HEADS, 1, tm, 2 * DA_HEAD_DIM), k_block),
                     pl.BlockSpec((DA_HEADS, 1, DA_VT_ROWS, tm), v_block)],
        out_shape=[jax.ShapeDtypeStruct((t, w), dt) for w, dt in flat]
                  + [jax.ShapeDtypeStruct((DA_HEADS, 2, 2 * DA_HEAD_DIM, t), BF16),
                     jax.ShapeDtypeStruct((DA_HEADS, t // tk, tk, 2 * DA_HEAD_DIM), BF16),
                     jax.ShapeDtypeStruct((DA_HEADS, t // tk, DA_VT_ROWS, tk), BF16)],
        scratch_shapes=[pltpu.VMEM((tm, 3 * DA_WIDTH), F32)] * 2,
        compiler_params=_cparams("arbitrary"),
        name="in_proj",
    )(xs, g1, modl, w_r, gq, gk, gmat, cos_t, sa_t, sb_t)


def _softplus(x):
    return jnp.maximum(x, 0.0) + jnp.log1p(jnp.exp(-jnp.abs(x)))


def _ssd_block(c, n_lat, n_blocks, reverse, xc_ref, xp_ref, xn_ref, dtc_ref, dtr_ref, cw_ref, cb_ref,
               bias_c_ref, a_c_ref, bias_r_ref, a_r_ref, st_ref):
    L = SSD_CHUNK
    rows = xc_ref.shape[0]
    x = xc_ref[...]
    seg_first = jnp.logical_or(c == 0, c == n_lat)
    seg_last = jnp.logical_or(c == n_lat - 1, c == n_blocks - 1)
    prow = jnp.where(seg_first, 0.0, xp_ref[7:8, :])
    nrow = jnp.where(seg_last, 0.0, xn_ref[0:1, :])
    ridx = lax.broadcasted_iota(jnp.int32, (rows, 1), 0)
    xprev = jnp.where(ridx == 0, prow, pltpu.roll(x, 1, 0))
    xnext = jnp.where(ridx == rows - 1, nrow, pltpu.roll(x, rows - 1, 0))
    cw = cw_ref[...]
    conv = xprev * cw[0:1] + x * cw[1:2] + xnext * cw[2:3] + cb_ref[...]
    xbc = _silu(conv)

    dt_c = _softplus(dtc_ref[...] + bias_c_ref[...])
    dt_r = _softplus(dtr_ref[...] + bias_r_ref[...])
    adt_c = dt_c * a_c_ref[...]
    adt_r = dt_r * a_r_ref[...]

    li = lax.broadcasted_iota(jnp.int32, (L, L), 0)
    si = lax.broadcasted_iota(jnp.int32, (L, L), 1)
    mask = (li <= si) if reverse else (li >= si)
    mask_t = (li >= si) if reverse else (li <= si)
    n_sub = rows // L
    yield
    parts = yield from _interleave(*[
        _ssd_chunk_local(xbc[s * L:(s + 1) * L], dt_c[s * L:(s + 1) * L], adt_c[s * L:(s + 1) * L],
                         adt_r[:, s * L:(s + 1) * L], mask, mask_t, reverse) for s in range(n_sub)])
    ys = [None] * n_sub
    for s in (reversed(range(n_sub)) if reverse else range(n_sub)):
        ys[s] = _ssd_chunk_state(*parts[s], st_ref, reverse)
        yield
    return jnp.concatenate(ys, axis=0), xbc[:, :SSD_INNER]


def _split3(x):
    hi = x.astype(BF16)
    r1 = x - hi.astype(F32)
    mid = r1.astype(BF16)
    lo = (r1 - mid.astype(F32)).astype(BF16)
    return hi, mid, lo


def _ssd_chunk_local(xbc, dt_c, adt_c, adt_r, mask, mask_t, reverse):
    L = SSD_CHUNK
    hpg = SSD_HEADS // SSD_GROUPS
    off = SSD_HEADS if reverse else 0
    m_b, mt_b = mask.astype(BF16), mask_t.astype(BF16)
    cs_c = sum(_dot(m_b, part) for part in _split3(adt_c))
    cs_r = sum(_dot(part, mt_b) for part in _split3(adt_r))
    edge = 0 if reverse else L - 1
    tot = cs_c[edge:edge + 1, :]
    yield

    kk = lax.broadcasted_iota(jnp.int32, (LANES, SSD_INNER), 0)
    jj = lax.broadcasted_iota(jnp.int32, (LANES, SSD_INNER), 1)
    spread = (kk == off + jj // SSD_HEAD_DIM).astype(BF16)
    cols = jnp.concatenate([dt_c, jnp.exp(cs_c), jnp.exp(tot - cs_c)], axis=0)
    c_hi = cols.astype(BF16)
    c_lo = (cols - c_hi.astype(F32)).astype(BF16)
    wide = _dot(c_hi, spread) + _dot(c_lo, spread)
    dt_w, ecs_w, dst_w = wide[0:L], wide[L:2 * L], wide[2 * L:3 * L]
    yield

    xs = xbc[:, :SSD_INNER]
    b_all = xbc[:, SSD_INNER:SSD_INNER + LANES]
    c_all = xbc[:, SSD_INNER + LANES:SSD_INNER + 2 * LANES]
    xd = xs * dt_w
    xd_b = xd.astype(BF16)
    b_b = b_all.astype(BF16)
    c_b = c_all.astype(BF16)
    lane = lax.broadcasted_iota(jnp.int32, (L, LANES), 1)
    col = lax.broadcasted_iota(jnp.int32, (L, SSD_INNER), 1)

    y = None
    for g in range(SSD_GROUPS):
        c_g = jnp.where(lane // SSD_STATE == g, c_b, jnp.zeros_like(c_b))
        gmat = _dot_nt(c_g, b_b)
        for hh in range(hpg):
            h = g * hpg + hh
            dec = jnp.exp(jnp.where(mask, cs_c[:, off + h:off + h + 1] - cs_r[off + h:off + h + 1, :], -jnp.inf))
            xd_h = jnp.where(col // SSD_HEAD_DIM == h, xd_b, jnp.zeros_like(xd_b))
            part = _dot((gmat * dec).astype(BF16), xd_h)
            y = part if y is None else y + part
            yield
    return y, c_b, ecs_w, b_all.T.astype(BF16), (xd * dst_w).astype(BF16)


def _ssd_chunk_state(y_local, c_b, ecs_w, bt_b, xds_b, st_ref, reverse):
    edge = 0 if reverse else SSD_CHUNK - 1
    y = y_local + ecs_w * _dot(c_b, st_ref[...].astype(BF16))
    kr = lax.broadcasted_iota(jnp.int32, (LANES, SSD_INNER), 0)
    kc = lax.broadcasted_iota(jnp.int32, (LANES, SSD_INNER), 1)
    block = kr // SSD_STATE == kc // (SSD_HEADS // SSD_GROUPS * SSD_HEAD_DIM)
    st_ref[...] = st_ref[...] * ecs_w[edge:edge + 1, :] + jnp.where(block, _dot(bt_b, xds_b), 0.0)
    return y


def _ssd_body(n_lat, n_chunks, *refs):
    fwd_in, bwd_in = refs[0:5], refs[5:10]
    cw_ref, cb_ref, bias_c_ref, a_c_ref, bias_r_ref, a_r_ref, dsk_ref, yf_ref, yb_ref, st_ref = refs[10:]
    i = pl.program_id(0)

    @pl.when(i == 0)
    def _():
        st_ref[...] = jnp.zeros_like(st_ref)

    shared = (cw_ref, cb_ref, bias_c_ref, a_c_ref, bias_r_ref, a_r_ref)
    (y_f, xs), (y_b, _) = _run(_interleave(
        _ssd_block(_segment_order(i, n_lat, n_chunks, False), n_lat, n_chunks, False, *fwd_in, *shared,
                   st_ref.at[0]),
        _ssd_block(_segment_order(i, n_lat, n_chunks, True), n_lat, n_chunks, True, *bwd_in, *shared,
                   st_ref.at[1])))
    yf_ref[...] = y_f + dsk_ref[...] * xs
    yb_ref[...] = y_b


def _ssd_call(xbc_raw, dt_c, dt_r, cw, cb, bias_c, a_c, bias_r, a_r, dsk, n_lat_rows):
    t = xbc_raw.shape[0]
    L = SSD_STEP_CHUNKS * SSD_CHUNK
    assert t % L == 0 and n_lat_rows % L == 0
    n_chunks = t // L
    n_lat = n_lat_rows // L
    sub = L // 8
    n_sub = t // 8
    const = lambda i: (0, 0)

    def chunk_specs(reverse):
        cidx = functools.partial(_segment_order, n_lat=n_lat, n_all=n_chunks, reverse=reverse)
        return [pl.BlockSpec((L, SSD_CONV_DIM), lambda i: (cidx(i), 0)),
                pl.BlockSpec((8, SSD_CONV_DIM), lambda i: (jnp.maximum(cidx(i) * sub - 1, 0), 0)),
                pl.BlockSpec((8, SSD_CONV_DIM), lambda i: (jnp.minimum((cidx(i) + 1) * sub, n_sub - 1), 0)),
                pl.BlockSpec((L, LANES), lambda i: (cidx(i), 0)),
                pl.BlockSpec((8, L), lambda i: (0, cidx(i)))]

    def out_spec(reverse):
        cidx = functools.partial(_segment_order, n_lat=n_lat, n_all=n_chunks, reverse=reverse)
        return pl.BlockSpec((L, SSD_INNER), lambda i: (cidx(i), 0))

    chunk_args = (xbc_raw, xbc_raw, xbc_raw, dt_c, dt_r)
    return pl.pallas_call(
        functools.partial(_ssd_body, n_lat, n_chunks),
        grid=(n_chunks,),
        in_specs=chunk_specs(False) + chunk_specs(True)
                 + [pl.BlockSpec((8, SSD_CONV_DIM), const),
                    pl.BlockSpec((1, SSD_CONV_DIM), const),
                    pl.BlockSpec((1, LANES), const),
                    pl.BlockSpec((1, LANES), const),
                    pl.BlockSpec((8, L), const),
                    pl.BlockSpec((8, L), const),
                    pl.BlockSpec((1, SSD_INNER), const)],
        out_specs=[out_spec(False), out_spec(True)],
        out_shape=[jax.ShapeDtypeStruct((t, SSD_INNER), F32)] * 2,
        scratch_shapes=[pltpu.VMEM((2, SSD_GROUPS * SSD_STATE, SSD_INNER), F32)],
        compiler_params=_cparams("arbitrary"),
        name="ssd_scan",
    )(*chunk_args, *chunk_args, cw, cb, bias_c, a_c, bias_r, a_r, dsk)


def _cmul(ar, ai, br, bi):
    return ar * br - ai * bi, ar * bi + ai * br


def _s5_step_pair(u_ref, a):
    return jnp.concatenate([u_ref[:, 2 * a, :], u_ref[:, 2 * a + 1, :]], axis=1).astype(BF16)


def _s5_in_body(u_ref, w_ref, lam_ref, efr_ref, efi_ref, ebr_ref, ebi_ref, wp_ref):
    hs = S5_HALF_STATE

    @pl.when(pl.program_id(1) == 0)
    def _():
        lam = lam_ref[0]
        wc = w_ref[0]
        rows = lambda a: jnp.broadcast_to(a, (LANES, hs))
        fr, fi, br, bi = rows(lam[0:1]), rows(lam[1:2]), rows(lam[2:3]), rows(lam[3:4])
        rgrp = lax.broadcasted_iota(jnp.int32, (LANES, LANES), 0) // S5_GROUP
        lgrp = lax.broadcasted_iota(jnp.int32, (LANES, LANES), 1) // S5_STATE
        planes = []
        for q in range(4):
            cq = wc[:, q * S5_STATE:(q + 1) * S5_STATE]
            cq2 = jnp.concatenate([cq, cq], axis=1)
            planes.append(jnp.concatenate(
                [jnp.where(rgrp == 2 * g4 + lgrp, cq2, 0.0) for g4 in range(S5_HALF_GROUPS // 2)], axis=1))
        ar, ai = planes[0], planes[1]
        for i in reversed(range(S5_CHUNK)):
            wp_ref[i * LANES:(i + 1) * LANES, 0:hs] = ar.astype(BF16)
            wp_ref[i * LANES:(i + 1) * LANES, hs:2 * hs] = ai.astype(BF16)
            ar, ai = _cmul(ar, ai, fr, fi)
        ar, ai = planes[2], planes[3]
        for i in range(S5_CHUNK):
            wp_ref[i * LANES:(i + 1) * LANES, 2 * hs:3 * hs] = ar.astype(BF16)
            wp_ref[i * LANES:(i + 1) * LANES, 3 * hs:4 * hs] = ai.astype(BF16)
            ar, ai = _cmul(ar, ai, br, bi)

    e = None
    for a in range(S5_CHUNK // 2):
        part = _dot(_s5_step_pair(u_ref, a), wp_ref[2 * a * LANES:2 * (a + 1) * LANES, :])
        e = part if e is None else e + part
    efr_ref[...] = e[:, 0:hs]
    efi_ref[...] = e[:, hs:2 * hs]
    ebr_ref[...] = e[:, 2 * hs:3 * hs]
    ebi_ref[...] = e[:, 3 * hs:4 * hs]


def _s5_in_call(u3, w_b, lam_rows):
    nch = u3.shape[0]
    tn = _pick_tile(nch, (208, 80, 40))
    hs = S5_HALF_STATE
    col = lambda h, n: (n, h)
    return pl.pallas_call(
        _s5_in_body,
        grid=(2, nch // tn),
        in_specs=[pl.BlockSpec((tn, S5_CHUNK, LANES), lambda h, n: (n, 0, h)),
                  pl.BlockSpec((1, LANES, 4 * S5_STATE), lambda h, n: (h, 0, 0)),
                  pl.BlockSpec((1, 4, hs), lambda h, n: (h, 0, 0))],
        out_specs=[pl.BlockSpec((tn, hs), col)] * 4,
        out_shape=[jax.ShapeDtypeStruct((nch, 2 * hs), F32)] * 4,
        scratch_shapes=[pltpu.VMEM((S5_CHUNK * LANES, 4 * hs), BF16)],
        compiler_params=_cparams("arbitrary", "arbitrary"),
        name="s5_chunk_in",
    )(u3, w_b, lam_rows)


def _s5_scan_body(a_ref, efr_ref, efi_ref, ebr_ref, ebi_ref, hfr_ref, hfi_ref, hbr_ref, hbi_ref, st_ref):
    @pl.when(pl.program_id(0) == 0)
    def _():
        st_ref[...] = jnp.zeros_like(st_ref)

    row = lambda ref, j: ref[j:j + 1, :]
    afr, afi, abr, abi = row(a_ref, 0), row(a_ref, 1), row(a_ref, 2), row(a_ref, 3)
    fr, fi, br, bi = row(st_ref, 0), row(st_ref, 1), row(st_ref, 2), row(st_ref, 3)
    for j in range(S5_SCAN_TILE):
        hfr_ref[j:j + 1, :] = fr
        hfi_ref[j:j + 1, :] = fi
        fr, fi = afr * fr - afi * fi + row(efr_ref, j), afr * fi + afi * fr + row(efi_ref, j)
        jb = S5_SCAN_TILE - 1 - j
        hbr_ref[jb:jb + 1, :] = br
        hbi_ref[jb:jb + 1, :] = bi
        br, bi = abr * br - abi * bi + row(ebr_ref, jb), abr * bi + abi * br + row(ebi_ref, jb)
    st_ref[0:1, :] = fr
    st_ref[1:2, :] = fi
    st_ref[2:3, :] = br
    st_ref[3:4, :] = bi


def _s5_scan_call(a_pow, efr, efi, ebr, ebi, n_lat_tiles):
    nch, width = efr.shape
    n_tiles = nch // S5_SCAN_TILE
    blk = (S5_SCAN_TILE, width)
    fwd = lambda i: (_segment_order(i, n_lat_tiles, n_tiles, False), 0)
    bwd = lambda i: (_segment_order(i, n_lat_tiles, n_tiles, True), 0)
    shp = jax.ShapeDtypeStruct((nch, width), F32)
    return pl.pallas_call(
        _s5_scan_body,
        grid=(n_tiles,),
        in_specs=[pl.BlockSpec((4, width), lambda i: (0, 0)),
                  pl.BlockSpec(blk, fwd), pl.BlockSpec(blk, fwd),
                  pl.BlockSpec(blk, bwd), pl.BlockSpec(blk, bwd)],
        out_specs=[pl.BlockSpec(blk, fwd), pl.BlockSpec(blk, fwd),
                   pl.BlockSpec(blk, bwd), pl.BlockSpec(blk, bwd)],
        out_shape=[shp] * 4,
        scratch_shapes=[pltpu.VMEM((4, width), F32)],
        compiler_params=_cparams("arbitrary"),
        name="s5_state_scan",
    )(a_pow, efr, efi, ebr, ebi)


def _s5_out_body(u_ref, hfr_ref, hfi_ref, hbr_ref, hbi_ref, k_ref, t_ref, c_ref, lam_ref, y_ref, wq_ref, z_ref):
    hs = S5_HALF_STATE
    width = S5_CHUNK * LANES

    @pl.when(pl.program_id(1) == 0)
    def _():
        strip = _dot(k_ref[0].astype(BF16), t_ref[...])
        rgrp = lax.broadcasted_iota(jnp.int32, strip.shape, 0) // S5_GROUP
        cgrp = (lax.broadcasted_iota(jnp.int32, strip.shape, 1) % LANES) // S5_GROUP
        strip = jnp.where(rgrp == cgrp, strip, 0.0).astype(BF16)
        z_ref[0:LANES, :] = strip[:, LANES:]
        z_ref[LANES:, :] = strip[:, :-LANES]

        cc = c_ref[0]
        ogrp = lax.broadcasted_iota(jnp.int32, (S5_STATE, LANES), 1) // S5_GROUP

        def expand(q):
            cq = cc[q * S5_STATE:(q + 1) * S5_STATE]
            return jnp.concatenate([jnp.where(ogrp == g, cq, 0.0) for g in range(S5_HALF_GROUPS)], axis=0)

        lam = lam_ref[0]
        for d, order in ((0, range(S5_CHUNK)), (1, reversed(range(S5_CHUNK)))):
            lr, li = lam[2 * d], lam[2 * d + 1]
            a, b = expand(2 * d), expand(2 * d + 1)
            for j in order:
                a, b = a * lr + b * li, b * lr - a * li
                wq_ref[2 * d * hs:(2 * d + 1) * hs, j * LANES:(j + 1) * LANES] = a.astype(BF16)
                wq_ref[(2 * d + 1) * hs:(2 * d + 2) * hs, j * LANES:(j + 1) * LANES] = b.astype(BF16)

    h_all = jnp.concatenate([hfr_ref[...], hfi_ref[...], hbr_ref[...], hbi_ref[...]], axis=1).astype(BF16)
    y = _dot(h_all, wq_ref[...])
    for a in range(S5_CHUNK // 2):
        lo = (S5_CHUNK - 2 - 2 * a) * LANES
        y += _dot(_s5_step_pair(u_ref, a), z_ref[:, lo:lo + width])
    for j in range(S5_CHUNK):
        y_ref[:, j, :] = y[:, j * LANES:(j + 1) * LANES]


def _s5_out_call(u3, hfr, hfi, hbr, hbi, w_k, w_t, w_c, lam_cols):
    nch = u3.shape[0]
    tn = _pick_tile(nch, (208, 80, 40))
    hs = S5_HALF_STATE
    col = lambda h, n: (n, h)
    blk3 = pl.BlockSpec((tn, S5_CHUNK, LANES), lambda h, n: (n, 0, h))
    return pl.pallas_call(
        _s5_out_body,
        grid=(2, nch // tn),
        in_specs=[blk3] + [pl.BlockSpec((tn, hs), col)] * 4
                 + [pl.BlockSpec((1, LANES) + w_k.shape[2:], lambda h, n: (h, 0, 0)),
                    pl.BlockSpec(w_t.shape, lambda h, n: (0, 0)),
                    pl.BlockSpec((1, 4 * S5_STATE, LANES), lambda h, n: (h, 0, 0)),
                    pl.BlockSpec((1, 4, hs, LANES), lambda h, n: (h, 0, 0, 0))],
        out_specs=blk3,
        out_shape=jax.ShapeDtypeStruct(u3.shape, F32),
        scratch_shapes=[pltpu.VMEM((4 * hs, S5_CHUNK * LANES), BF16),
                        pltpu.VMEM((2 * LANES, (S5_TOEPLITZ_BLOCKS - 1) * LANES), BF16)],
        compiler_params=_cparams("arbitrary", "arbitrary"),
        name="s5_chunk_out",
    )(u3, hfr, hfi, hbr, hbi, w_k, w_t, w_c, lam_cols)


def _s5_weights(lam_re, lam_im, log_step, b_re, b_im, c_re, c_im):
    tc = S5_CHUNK
    nl = lam_re.shape[0]
    hg = S5_HALF_GROUPS
    ein = functools.partial(jnp.einsum, precision=HI)
    step = jnp.exp(log_step)[..., None]
    er, ei = lam_re * step, lam_im * step
    mag = jnp.exp(er)
    lbr, lbi = mag * jnp.cos(ei), mag * jnp.sin(ei)
    den = lam_re * lam_re + lam_im * lam_im
    nr, ni = lbr - 1.0, lbi
    cr, ci = (nr * lam_re + ni * lam_im) / den, (ni * lam_re - nr * lam_im) / den
    bbr = cr[..., None] * b_re - ci[..., None] * b_im
    bbi = cr[..., None] * b_im + ci[..., None] * b_re
    taus = jnp.arange(tc + 1, dtype=F32)[:, None, None]
    pmag = jnp.exp(er[:, :, None] * taus)
    pr, pi = pmag * jnp.cos(ei[:, :, None] * taus), pmag * jnp.sin(ei[:, :, None] * taus)

    wr = pr[:, :, :tc, :, :, None] * bbr[:, :, None] - pi[:, :, :tc, :, :, None] * bbi[:, :, None]
    wi = pr[:, :, :tc, :, :, None] * bbi[:, :, None] + pi[:, :, :tc, :, :, None] * bbr[:, :, None]
    kk = ein('ldgcp,ldtgpk->ldtgck', c_re, wr) - ein('ldgcp,ldtgpk->ldtgck', c_im, wi)
    kf, kb = kk[:, 0], kk[:, 1]
    seq = jnp.concatenate([kb[:, :0:-1], kf[:, :1] + kb[:, :1], kf[:, 1:]], axis=1)
    seq = seq.reshape(nl, S5_TOEPLITZ_BLOCKS, 2, hg, S5_GROUP, S5_GROUP)
    w_k = seq.transpose(0, 2, 3, 5, 1, 4).reshape(nl, 2, LANES, S5_TOEPLITZ_BLOCKS * S5_GROUP)
    w_k = jnp.pad(w_k, ((0, 0), (0, 0), (0, 0), (0, S5_KCOLS - S5_TOEPLITZ_BLOCKS * S5_GROUP)))
    r = jnp.arange(S5_KCOLS)
    s = jnp.arange(S5_TOEPLITZ_BLOCKS * LANES)
    w_t = ((r[:, None] // S5_GROUP == s[None, :] // LANES)
           & (r[:, None] % S5_GROUP == s[None, :] % S5_GROUP)).astype(BF16)

    planes_b = jnp.stack([bbr[:, 0], bbi[:, 0], bbr[:, 1], bbi[:, 1]], axis=1)
    planes_b = planes_b.reshape(nl, 4, 2, hg, S5_STATE, S5_GROUP)
    w_b = planes_b.transpose(0, 2, 3, 5, 1, 4).reshape(nl, 2, LANES, 4 * S5_STATE)
    planes_c = jnp.stack([c_re[:, 0], -c_im[:, 0], c_re[:, 1], -c_im[:, 1]], axis=1)
    planes_c = planes_c.reshape(nl, 4, 2, hg, S5_GROUP, S5_STATE)
    w_c = planes_c.transpose(0, 2, 1, 5, 3, 4).reshape(nl, 2, 4 * S5_STATE, LANES)

    halves = lambda a: a.reshape(nl, 2, S5_HALF_STATE)
    lam_rows = jnp.stack([halves(lbr[:, 0]), halves(lbi[:, 0]), halves(lbr[:, 1]), halves(lbi[:, 1])], axis=2)
    lam_cols = jnp.broadcast_to(lam_rows[..., None], lam_rows.shape + (LANES,))
    full = lambda a: a.reshape(nl, S5_GROUPS * S5_STATE)
    a_pow = jnp.stack([full(pr[:, 0, tc]), full(pi[:, 0, tc]), full(pr[:, 1, tc]), full(pi[:, 1, tc])], axis=1)
    return (w_b, w_k, w_c, lam_rows, lam_cols, a_pow), w_t


def _s5_mix(u, w_b, w_k, w_c, lam_rows, lam_cols, a_pow, w_t, n_lat):
    t = u.shape[0]
    nch = t // S5_CHUNK
    u3 = u.reshape(nch, S5_CHUNK, S5_WIDTH)
    es = _s5_in_call(u3, w_b, lam_rows)
    hs = _s5_scan_call(a_pow, *es, n_lat // (S5_CHUNK * S5_SCAN_TILE))
    return _s5_out_call(u3, *hs, w_k, w_t, w_c, lam_cols).reshape(t, S5_WIDTH)


def _attn_body(flag_ref, lam_ref, q_ref, k_ref, v_ref, g_ref, o_ref, acc_ref, m_ref, oh_ref, *, n_kv, post_scale):
    bounded = flag_ref[0] == 1

    def head(hh, carry):
        acc_ref[...] = jnp.zeros_like(acc_ref)

        @pl.when(bounded)
        def _():
            def step(b, carry2):
                k = k_ref[hh, b]
                vt = v_ref[hh, b]
                for c in range(2):
                    p = jnp.exp2(_dot(k, q_ref[hh, c])).astype(BF16)
                    acc_ref[c] += _dot(vt, p)
                return carry2
            lax.fori_loop(0, n_kv, step, 0)

        @pl.when(jnp.logical_not(bounded))
        def _():
            m_ref[...] = jnp.full_like(m_ref, -jnp.inf)

            def step(b, carry2):
                k = k_ref[hh, b]
                vt = v_ref[hh, b]
                for c in range(2):
                    s = _dot(k, q_ref[hh, c])
                    m_old = m_ref[c]
                    m_new = jnp.maximum(m_old, jnp.max(s, axis=0, keepdims=True))
                    p = jnp.exp2(s - m_new[0:1]).astype(BF16)
                    acc_ref[c] = jnp.exp2(m_old - m_new)[0:1] * acc_ref[c] + _dot(vt, p)
                    m_ref[c] = m_new
                return carry2
            lax.fori_loop(0, n_kv, step, 0)

        a0 = acc_ref[0]
        a1 = acc_ref[1]
        o = a0[:DA_V_DIM] / a0[DA_V_DIM:DA_V_DIM + 1] \
            - lam_ref[0] * (a1[:DA_V_DIM] / a1[DA_V_DIM:DA_V_DIM + 1])
        ms = jnp.mean(o * o, axis=0, keepdims=True)
        o = o * lax.rsqrt(ms + EPS)
        ot = jnp.concatenate([o, jnp.zeros_like(o)], axis=0).T
        oh_ref[hh] = ot[:, :DA_V_DIM] * g_ref[...] * post_scale
        return carry

    lax.fori_loop(0, ATTN_HEADS_PER_STEP, head, 0)
    o_ref[...] = jnp.concatenate([oh_ref[hh] for hh in range(ATTN_HEADS_PER_STEP)], axis=1)


def _attn_body_keep_rest(flag_ref, lam_ref, q_ref, k_ref, v_ref, g_ref, rest_ref, *refs, **kw):
    del rest_ref
    _attn_body(flag_ref, lam_ref, q_ref, k_ref, v_ref, g_ref, *refs, **kw)


def _attn_call(flag, lam, qt, kb, vtb, g_sub, post_scale, q_rows, q_row0, kv_block, kv_cols, kv_col0, y_rest=None):
    nh = qt.shape[0]
    hp = ATTN_HEADS_PER_STEP
    tk = kb.shape[2]
    tq = _pick_tile(q_rows, (ATTN_TQ, 256))
    assert q_row0 % tq == 0
    q0 = q_row0 // tq
    if kv_block is None:
        n_kv = kb.shape[1]
        one = pl.Buffered(1)
        k_spec = pl.BlockSpec((hp, n_kv, tk, 2 * DA_HEAD_DIM), lambda h, i: (h, 0, 0, 0), pipeline_mode=one)
        v_spec = pl.BlockSpec((hp, n_kv, DA_VT_ROWS, tk), lambda h, i: (h, 0, 0, 0), pipeline_mode=one)
    else:
        assert kv_col0 % kv_cols == 0 and tk % kv_cols == 0
        n_kv, cb = 1, kv_col0 // kv_cols
        k_spec = pl.BlockSpec((hp, 1, kv_cols, 2 * DA_HEAD_DIM), lambda h, i: (h, kv_block, cb, 0))
        v_spec = pl.BlockSpec((hp, 1, DA_VT_ROWS, kv_cols), lambda h, i: (h, kv_block, 0, cb))
    smem = pl.BlockSpec(memory_space=pltpu.SMEM)
    in_specs = [smem, smem,
                pl.BlockSpec((hp, 2, 2 * DA_HEAD_DIM, tq), lambda h, i: (h, 0, 0, q0 + i)),
                k_spec, v_spec,
                pl.BlockSpec((1, DA_V_DIM), lambda h, i: (0, 0))]
    args = (flag, lam, qt, kb, vtb, g_sub)
    body = functools.partial(_attn_body, n_kv=n_kv, post_scale=post_scale)
    aliases = {}
    if y_rest is not None:
        in_specs.append(pl.BlockSpec(memory_space=pl.ANY))
        args += (y_rest,)
        body = functools.partial(_attn_body_keep_rest, n_kv=n_kv, post_scale=post_scale)
        aliases = {len(args) - 1: 0}
    return pl.pallas_call(
        body,
        grid=(nh // hp, q_rows // tq),
        in_specs=in_specs,
        out_specs=pl.BlockSpec((tq, hp * DA_V_DIM), lambda h, i: (q0 + i, h)),
        out_shape=jax.ShapeDtypeStruct((qt.shape[3], nh * DA_V_DIM), F32),
        scratch_shapes=[pltpu.VMEM((2, DA_VT_ROWS, tq), F32), pltpu.VMEM((2, 8, tq), F32),
                        pltpu.VMEM((hp, tq, DA_V_DIM), F32)],
        input_output_aliases=aliases,
        compiler_params=_cparams("arbitrary", "arbitrary"),
        name="diff_attn",
    )(*args)


def _gelu_tanh(x):
    return 0.5 * x * (1.0 + jnp.tanh(math.sqrt(2.0 / math.pi) * (x + 0.044715 * (x * x * x))))


def _outffn_body(x_ref, yf_ref, yb_ref, z_ref, gssd_ref, ys5_ref, u_ref, d5_ref, gw_ref, gb_ref,
                 yda_ref, mod_ref, modp_ref, g2_ref, wo_ref, w1_ref, w3_ref, w2_ref, o_ref, xm_ref, hb_ref):
    i = pl.program_id(0)
    cur = i % 2
    prev = 1 - cur

    @pl.when(i == 0)
    def _():
        xm_ref[...] = jnp.zeros_like(xm_ref)
        hb_ref[...] = jnp.zeros_like(hb_ref)

    def finish():
        hp = hb_ref[prev]
        a1 = _dot(hp, w1_ref[...])
        yield
        a3 = _dot(hp, w3_ref[...])
        yield
        f = (_silu(a1) * a3).astype(BF16)
        yield
        o_ref[...] = xm_ref[prev] + modp_ref[0][5:6] * _dot(f, w2_ref[...])

    def prepare():
        mod = mod_ref[0]
        y = (yf_ref[...] + yb_ref[...]) * _silu(z_ref[...])
        y_ssd = y * lax.rsqrt(jnp.mean(y * y, axis=-1, keepdims=True) + EPS) * gssd_ref[...]
        yield
        y5 = _gelu_tanh(ys5_ref[...] + d5_ref[...] * u_ref[...])
        yield
        y5 = y5 * jax.nn.sigmoid(_dot(y5.astype(BF16), gw_ref[...].astype(BF16)) + gb_ref[...])
        yield
        o = _dot(y_ssd.astype(BF16), wo_ref[0:256, :])
        yield
        o += _dot(y5.astype(BF16), wo_ref[256:512, :])
        yield
        o += _dot(yda_ref[...].astype(BF16), wo_ref[512:1024, :])
        yield
        x = x_ref[...] + mod[2:3] * o
        xm_ref[cur] = x
        yield
        h = x * lax.rsqrt(jnp.mean(x * x, axis=-1, keepdims=True) + EPS) * g2_ref[...] * (1.0 + mod[4:5]) \
            + mod[3:4]
        hb_ref[cur] = h.astype(BF16)

    _run(_interleave(finish(), prepare()))


def _outffn_call(xs, y_f, y_b, z, g_ssd, y_s5, u, d5, glu_w, glu_b, y_da, modl, g2, w_o, w1, w3, w2, layer, n_lat,
                 latent_only):
    t, d = xs.shape
    if latent_only:
        t = n_lat
    dff = w1.shape[2]
    layer_block = lambda i: (layer, 0, 0)
    tm = FFN_ROW_TILE
    assert n_lat % tm == 0 and t % tm == 0
    n_lat_tiles = n_lat // tm
    n_tiles = t // tm
    row = lambda i: (jnp.minimum(i, n_tiles - 1), 0)
    done = lambda i: (jnp.maximum(i - 1, 0), 0)
    const = lambda i: (0, 0)
    seg = lambda i: ((jnp.minimum(i, n_tiles - 1) >= n_lat_tiles).astype(jnp.int32), 0, 0)
    seg_done = lambda i: ((i - 1 >= n_lat_tiles).astype(jnp.int32), 0, 0)
    one = pl.Buffered(1)
    return pl.pallas_call(
        _outffn_body,
        grid=(n_tiles + 1,),
        in_specs=[pl.BlockSpec((tm, d), row),
                  pl.BlockSpec((tm, 256), row), pl.BlockSpec((tm, 256), row), pl.BlockSpec((tm, 256), row),
                  pl.BlockSpec((1, 256), const),
                  pl.BlockSpec((tm, 256), row), pl.BlockSpec((tm, 256), row),
                  pl.BlockSpec((1, 256), const),
                  pl.BlockSpec((256, 256), const), pl.BlockSpec((1, 256), const),
                  pl.BlockSpec((tm, DA_WIDTH), row),
                  pl.BlockSpec((1, 8, d), seg),
                  pl.BlockSpec((1, 8, d), seg_done),
                  pl.BlockSpec((1, d), const),
                  pl.BlockSpec((None, d, d), layer_block, pipeline_mode=one),
                  pl.BlockSpec((None, d, dff), layer_block, pipeline_mode=one),
                  pl.BlockSpec((None, d, dff), layer_block, pipeline_mode=one),
                  pl.BlockSpec((None, dff, d), layer_block, pipeline_mode=one)],
        out_specs=pl.BlockSpec((tm, d), done),
        out_shape=jax.ShapeDtypeStruct((t, d), F32),
        scratch_shapes=[pltpu.VMEM((2, tm, d), F32), pltpu.VMEM((2, tm, d), BF16)],
        compiler_params=_cparams("arbitrary"),
        name="out_proj_ffn",
    )(xs, y_f, y_b, z, g_ssd, y_s5, u, d5, glu_w, glu_b, y_da, modl, modl, g2, w_o, w1, w3, w2)


def _cast_body(w_ref, o_ref):
    o_ref[...] = w_ref[...].astype(o_ref.dtype)


def _to_bf16(w):
    nl, rows, cols = w.shape
    tr = _pick_tile(rows, (512, 704, 256))
    blk = pl.BlockSpec((1, tr, cols), lambda l, r: (l, r, 0))
    return pl.pallas_call(
        _cast_body,
        grid=(nl, rows // tr),
        in_specs=[blk],
        out_specs=blk,
        out_shape=jax.ShapeDtypeStruct(w.shape, BF16),
        compiler_params=_cparams("arbitrary", "arbitrary"),
        name="weights_to_bf16",
    )(w)


def _rope_tables(n_lat, n_ctx):
    pos = jnp.arange(n_lat)
    lane = jnp.arange(LANES)
    d = lane % DA_HEAD_DIM
    axis = d // (2 * ROPE_FREQS)
    half = (d % (2 * ROPE_FREQS)) // ROPE_FREQS
    freqs = ROPE_THETA ** (-jnp.arange(ROPE_FREQS, dtype=F32) / ROPE_FREQS)
    coord = jnp.where(axis[None, :] == 0, (pos // GRID_W)[:, None], (pos % GRID_W)[:, None]).astype(F32)
    ang = coord * freqs[d % ROPE_FREQS][None, :]
    cos, sin = jnp.cos(ang), jnp.sin(ang)
    sa = jnp.where(half[None, :] == 0, -sin, 0.0)
    sb = jnp.where(half[None, :] == 1, sin, 0.0)
    pad = lambda a, v: jnp.concatenate([a, jnp.full((n_ctx, LANES), v, F32)], axis=0)
    return pad(cos, 1.0), pad(sa, 0.0), pad(sb, 0.0)


def _lane_row(vals, width=LANES):
    vals = vals.reshape(-1).astype(F32)
    return jnp.zeros((1, width), F32).at[0, :vals.shape[0]].set(vals)


def kernel(x, c, ctx, c_ctx, w_mod, b_mod, norm1, norm2, w_in, w_out, ssd_conv_w, ssd_conv_b, ssd_a_log, ssd_dt_bias, ssd_d, ssd_norm, s5_lam_re, s5_lam_im, s5_log_step, s5_b_re, s5_b_im, s5_c_re, s5_c_im, s5_d, s5_glu_w, s5_glu_b, da_q_norm, da_k_norm, da_lambda, da_sub_norm, ffn_w1, ffn_w3, ffn_w2):
    depth = w_mod.shape[0]
    bsz, n_lat, d = x.shape
    n_ctx = ctx.shape[1]
    t = n_ctx + n_lat
    tk = _pick_tile(t, (ATTN_TK, 1024, 640, 512, 256))
    assert bsz == 1 and n_ctx % ROW_TILE == 0 and n_lat % ROW_TILE == 0 and tk % ROW_TILE == 0
    assert n_ctx % (S5_CHUNK * S5_SCAN_TILE) == 0 and n_lat % (S5_CHUNK * S5_SCAN_TILE) == 0
    assert n_lat % n_ctx == 0 and tk % n_ctx == 0
    n_lat_tiles = n_lat // ROW_TILE

    xs = jnp.concatenate([x[0], ctx[0]], axis=0)
    cv = jnp.zeros((8, d), F32).at[0].set(c[0]).at[1].set(c_ctx)
    mod_all = _mod_call(cv, w_mod, b_mod)
    cos_t, sa_t, sb_t = _rope_tables(n_lat, n_ctx)
    gidx = jnp.arange(DA_WIDTH) // DA_HEAD_DIM
    gmat = (gidx[:, None] == gidx[None, :]).astype(BF16) * (1.0 / DA_HEAD_DIM)
    s5_w, s5_spread = _s5_weights(s5_lam_re, s5_lam_im, s5_log_step, s5_b_re, s5_b_im, s5_c_re, s5_c_im)
    w_out_b, w1_b, w3_b, w2_b = _to_bf16(w_out), _to_bf16(ffn_w1), _to_bf16(ffn_w3), _to_bf16(ffn_w2)

    for i in range(depth):
        lam_init = 0.8 - 0.6 * math.exp(-0.3 * i)
        modl = jnp.zeros((2, 8, d), F32).at[:, :6].set(mod_all[i, :2].reshape(2, 6, d))

        wi = w_in[i]
        w_r = jnp.concatenate([wi[:, 0:768], wi[:, 776:1032], wi[:, 1032:2568], wi[:, 768:776],
                               jnp.zeros((d, IN_PAD - 2568), F32)], axis=1).astype(BF16)
        gq = jnp.tile(da_q_norm[i], DA_WIDTH // DA_HEAD_DIM)[None, :]
        gk = jnp.tile(da_k_norm[i], DA_WIDTH // DA_HEAD_DIM)[None, :]
        z, xbc_raw, u, dt_raw, qh, kzb, vb = _inproj_call(
            xs, norm1[i][None, :], modl, w_r, gq, gk, gmat, cos_t, sa_t, sb_t, n_lat_tiles, tk)

        dt_r = dt_raw[:, :8].T
        cw = jnp.zeros((8, SSD_CONV_DIM), F32).at[:3].set(ssd_conv_w[i])
        a_neg = -jnp.exp(ssd_a_log[i])
        bias_c = _lane_row(ssd_dt_bias[i])
        a_c = _lane_row(a_neg)
        bias_r = jnp.broadcast_to(ssd_dt_bias[i].reshape(8, 1), (8, SSD_STEP_CHUNKS * SSD_CHUNK))
        a_r = jnp.broadcast_to(a_neg.reshape(8, 1), (8, SSD_STEP_CHUNKS * SSD_CHUNK))
        dsk = jnp.repeat(ssd_d[i], SSD_HEAD_DIM)[None, :]
        y_f, y_b = _ssd_call(xbc_raw, dt_raw, dt_r, cw, ssd_conv_b[i][None, :], bias_c, a_c, bias_r, a_r, dsk,
                             n_lat)

        y_s5 = _s5_mix(u, *(w[i] for w in s5_w), s5_spread, n_lat)

        lf = da_lambda[i]
        lam = (jnp.exp(jnp.sum(lf[0] * lf[1])) - jnp.exp(jnp.sum(lf[2] * lf[3])) + lam_init).reshape(1)
        g_sub = da_sub_norm[i][None, :]
        score_bound = LOG2E * math.sqrt(DA_HEAD_DIM) * jnp.max(jnp.abs(da_q_norm[i])) * jnp.max(jnp.abs(da_k_norm[i]))
        flag = (score_bound <= SCORE_LOG2_LIMIT).astype(jnp.int32).reshape(1)
        y_da = _attn_call(flag, lam, qh, kzb, vb, g_sub, 1.0 - lam_init, n_lat, 0, None, None, None)
        last = i == depth - 1
        if not last:
            y_da = _attn_call(flag, lam, qh, kzb, vb, g_sub, 1.0 - lam_init, n_ctx, n_lat,
                              n_lat // tk, n_ctx, n_lat % tk, y_rest=y_da)

        xs = _outffn_call(xs, y_f, y_b, z, ssd_norm[i][None, :], y_s5, u, s5_d[i][None, :], s5_glu_w[i],
                          s5_glu_b[i][None, :], y_da, modl, norm2[i][None, :], w_out_b, w1_b, w3_b, w2_b, i,
                          n_lat, latent_only=last)
    return xs[None]
```

```python
import functools
import math

import jax
import jax.numpy as jnp
from jax import lax
from jax.experimental import pallas as pl
from jax.experimental.pallas import tpu as pltpu

F32 = jnp.float32
BF16 = jnp.bfloat16
HI = lax.Precision.HIGHEST

EPS = 1e-6
GRID_W = 64
ROPE_THETA = 10000.0
ROPE_FREQS = 8

SSD_HEADS = 4
SSD_HEAD_DIM = 64
SSD_INNER = 256
SSD_GROUPS = 2
SSD_STATE = 64
SSD_CHUNK = 128
SSD_STEP_CHUNKS = 2
SSD_CONV_DIM = 512

S5_WIDTH = 256
S5_GROUP = 16
S5_GROUPS = 16
S5_STATE = 64
S5_CHUNK = 16
S5_HALF_GROUPS = S5_GROUPS // 2
S5_HALF_STATE = S5_HALF_GROUPS * S5_STATE
S5_SCAN_ROWS = 64
S5_TOEPLITZ_BLOCKS = 2 * S5_CHUNK - 1
S5_KCOLS = 512

DA_HEADS = 8
DA_HEAD_DIM = 32
DA_V_DIM = 64
DA_WIDTH = 512
ATTN_HEADS_PER_STEP = 2
ATTN_TQ = 2048
ATTN_TK = 1280
LOG2E = 1.4426950408889634
SCORE_LOG2_LIMIT = 100.0

LANES = 128
ROW_TILE = 256
FFN_ROW_TILE = 256
IN_PAD = 2688
VMEM_LIMIT = 56 * 1024 * 1024


def _cparams(*sem):
    return pltpu.CompilerParams(dimension_semantics=sem, vmem_limit_bytes=VMEM_LIMIT)


def _dot(a, b, precision=None):
    return jnp.dot(a, b, preferred_element_type=F32, precision=precision)


def _dot_nt(a, b, precision=None):
    return lax.dot_general(a, b, (((1,), (1,)), ((), ())), preferred_element_type=F32,
                           precision=precision)


def _dot_tn(a, b, precision=None):
    return lax.dot_general(a, b, (((0,), (0,)), ((), ())), preferred_element_type=F32,
                           precision=precision)


def _silu(x):
    return x * jax.nn.sigmoid(x)


def _pick_tile(n, candidates):
    for c in candidates:
        if n % c == 0:
            return c
    return n


def _interleave(*stages):
    results = [None] * len(stages)
    live = dict(enumerate(stages))
    while live:
        for k in list(live):
            try:
                next(live[k])
            except StopIteration as done:
                results[k] = done.value
                del live[k]
        yield
    return results


def _run(stage):
    while True:
        try:
            next(stage)
        except StopIteration as done:
            return done.value


def _segment_order(i, n_lat, n_all, reverse):
    n_ctx = n_all - n_lat
    if not reverse:
        return jnp.where(i < n_ctx, n_lat + i, i - n_ctx)
    return jnp.where(i < n_ctx, n_all - 1 - i, n_lat - 1 - (i - n_ctx))


def _mod_body(cv_ref, w_ref, b_ref, o_ref):
    o_ref[0] = _dot(_silu(cv_ref[...]), w_ref[0], HI) + b_ref[0]


def _mod_call(cv, w_mod, b_mod):
    depth, d, n = w_mod.shape
    tn = 2048
    return pl.pallas_call(
        _mod_body,
        grid=(depth, n // tn),
        in_specs=[pl.BlockSpec((8, d), lambda l, j: (0, 0)),
                  pl.BlockSpec((1, d, tn), lambda l, j: (l, 0, j)),
                  pl.BlockSpec((1, 1, tn), lambda l, j: (l, 0, j))],
        out_specs=pl.BlockSpec((1, 8, tn), lambda l, j: (l, 0, j)),
        out_shape=jax.ShapeDtypeStruct((depth, 8, n), F32),
        compiler_params=_cparams("arbitrary", "arbitrary"),
        name="adaln_mod",
    )(cv, w_mod, b_mod.reshape(depth, 1, n))


def _inproj_project(x_ref, g_ref, mod_ref, w_ref, z_ref, xbc_ref, u_ref, dt_ref, qkv_ref):
    x = x_ref[...]
    mod = mod_ref[0]
    ms = jnp.mean(x * x, axis=-1, keepdims=True)
    h = x * lax.rsqrt(ms + EPS) * g_ref[...] * (1.0 + mod[1:2]) + mod[0:1]
    p = _dot(h.astype(BF16), w_ref[...])
    z_ref[...] = p[:, 0:256]
    xbc_ref[...] = p[:, 256:768]
    u_ref[...] = p[:, 768:1024]
    dt_ref[...] = p[:, 2560:2688]
    qkv_ref[...] = p[:, 1024:2560]


def _inproj_attn_operands(qkv_ref, gq_ref, gk_ref, gm_ref, cos_ref, sa_ref, sb_ref, q_ref, k_ref, v_ref):
    p = qkv_ref[...]
    rows = p.shape[0]
    gm = gm_ref[...]
    cos = cos_ref[...]
    sa = sa_ref[...]
    sb = sb_ref[...]

    def norm_rope(t, gain, scale):
        ms32 = _dot((t * t).astype(BF16), gm)
        tn = t * lax.rsqrt(ms32 + EPS) * gain
        outs = []
        for j in range(DA_WIDTH // LANES):
            tb = tn[:, j * LANES:(j + 1) * LANES]
            ob = tb * cos + pltpu.roll(tb, LANES - ROPE_FREQS, 1) * sa + pltpu.roll(tb, ROPE_FREQS, 1) * sb
            outs.append(ob * scale)
        return jnp.concatenate(outs, axis=1)

    qt = norm_rope(p[:, 0:512], gq_ref[...], LOG2E * DA_HEAD_DIM ** -0.5).T.astype(BF16)
    kn = norm_rope(p[:, 512:1024], gk_ref[...], 1.0).astype(BF16)
    vt = p[:, 1024:1536].T.astype(BF16)
    srow = lax.broadcasted_iota(jnp.int32, (LANES - DA_V_DIM, rows), 0)
    ones_row = jnp.where(srow == 0, 1.0, 0.0).astype(BF16)
    zeros_q = jnp.zeros((DA_HEAD_DIM, rows), BF16)
    for hd in range(DA_HEADS):
        lo_, mid, hi_ = hd * DA_V_DIM, hd * DA_V_DIM + DA_HEAD_DIM, (hd + 1) * DA_V_DIM
        k_ref[hd, 0] = kn[:, lo_:hi_]
        q_ref[hd, 0, 0:DA_HEAD_DIM, :] = qt[lo_:mid]
        q_ref[hd, 0, DA_HEAD_DIM:, :] = zeros_q
        q_ref[hd, 1, 0:DA_HEAD_DIM, :] = zeros_q
        q_ref[hd, 1, DA_HEAD_DIM:, :] = qt[mid:hi_]
        v_ref[hd, 0, 0:DA_V_DIM, :] = vt[lo_:hi_]
        v_ref[hd, 0, DA_V_DIM:, :] = ones_row


def _inproj_body(x_ref, g_ref, mod_ref, w_ref, gq_ref, gk_ref, gm_ref, cos_ref, sa_ref, sb_ref,
                 z_ref, xbc_ref, u_ref, dt_ref, q_ref, k_ref, v_ref, qkv_a_ref, qkv_b_ref):
    i = pl.program_id(0)

    @pl.when(i == 0)
    def _():
        qkv_b_ref[...] = jnp.zeros_like(qkv_b_ref)

    def step(cur_ref, prev_ref):
        _inproj_attn_operands(prev_ref, gq_ref, gk_ref, gm_ref, cos_ref, sa_ref, sb_ref, q_ref, k_ref, v_ref)
        _inproj_project(x_ref, g_ref, mod_ref, w_ref, z_ref, xbc_ref, u_ref, dt_ref, cur_ref)

    @pl.when(i % 2 == 0)
    def _():
        step(qkv_a_ref, qkv_b_ref)

    @pl.when(i % 2 == 1)
    def _():
        step(qkv_b_ref, qkv_a_ref)


def _inproj_call(xs, g1, modl, w_r, gq, gk, gmat, cos_t, sa_t, sb_t, n_lat_tiles, tk):
    t, d = xs.shape
    tm = ROW_TILE
    n_tiles = t // tm
    per = tk // tm
    row = lambda i: (jnp.minimum(i, n_tiles - 1), 0)
    done = lambda i: (jnp.maximum(i - 1, 0), 0)
    const = lambda i: (0, 0)
    seg = lambda i: ((jnp.minimum(i, n_tiles - 1) >= n_lat_tiles).astype(jnp.int32), 0, 0)
    flat = [(256, F32), (512, F32), (256, F32), (LANES, F32)]

    def k_block(i):
        j = jnp.maximum(i - 1, 0)
        return (0, j // per, j % per, 0)

    def v_block(i):
        j = jnp.maximum(i - 1, 0)
        return (0, j // per, 0, j % per)

    return pl.pallas_call(
        _inproj_body,
        grid=(n_tiles + 1,),
        in_specs=[pl.BlockSpec((tm, d), row),
                  pl.BlockSpec((1, d), const),
                  pl.BlockSpec((1, 8, d), seg),
                  pl.BlockSpec((d, IN_PAD), const),
                  pl.BlockSpec((1, DA_WIDTH), const),
                  pl.BlockSpec((1, DA_WIDTH), const),
                  pl.BlockSpec((DA_WIDTH, DA_WIDTH), const),
                  pl.BlockSpec((tm, LANES), done),
                  pl.BlockSpec((tm, LANES), done),
                  pl.BlockSpec((tm, LANES), done)],
        out_specs=[pl.BlockSpec((tm, w), row) for w, _ in flat]
                  + [pl.BlockSpec((DA_HEADS, 2, 2 * DA_HEAD_DIM, tm), lambda i: (0, 0, 0, jnp.maximum(i - 1, 0))),
                     pl.BlockSpec((DA_HEADS, 1, tm, 2 * DA_HEAD_DIM), k_block),
                     pl.BlockSpec((DA_HEADS, 1, LANES, tm), v_block)],
        out_shape=[jax.ShapeDtypeStruct((t, w), dt) for w, dt in flat]
                  + [jax.ShapeDtypeStruct((DA_HEADS, 2, 2 * DA_HEAD_DIM, t), BF16),
                     jax.ShapeDtypeStruct((DA_HEADS, t // tk, tk, 2 * DA_HEAD_DIM), BF16),
                     jax.ShapeDtypeStruct((DA_HEADS, t // tk, LANES, tk), BF16)],
        scratch_shapes=[pltpu.VMEM((tm, 3 * DA_WIDTH), F32)] * 2,
        compiler_params=_cparams("arbitrary"),
        name="in_proj",
    )(xs, g1, modl, w_r, gq, gk, gmat, cos_t, sa_t, sb_t)


def _softplus(x):
    return jnp.maximum(x, 0.0) + jnp.log1p(jnp.exp(-jnp.abs(x)))


def _ssd_block(c, n_lat, n_blocks, reverse, xc_ref, xp_ref, xn_ref, dtc_ref, dtr_ref, cw_ref, cb_ref,
               bias_c_ref, a_c_ref, bias_r_ref, a_r_ref, st_ref):
    L = SSD_CHUNK
    rows = xc_ref.shape[0]
    x = xc_ref[...]
    seg_first = jnp.logical_or(c == 0, c == n_lat)
    seg_last = jnp.logical_or(c == n_lat - 1, c == n_blocks - 1)
    prow = jnp.where(seg_first, 0.0, xp_ref[7:8, :])
    nrow = jnp.where(seg_last, 0.0, xn_ref[0:1, :])
    ridx = lax.broadcasted_iota(jnp.int32, (rows, 1), 0)
    xprev = jnp.where(ridx == 0, prow, pltpu.roll(x, 1, 0))
    xnext = jnp.where(ridx == rows - 1, nrow, pltpu.roll(x, rows - 1, 0))
    cw = cw_ref[...]
    conv = xprev * cw[0:1] + x * cw[1:2] + xnext * cw[2:3] + cb_ref[...]
    xbc = _silu(conv)

    dt_c = _softplus(dtc_ref[...] + bias_c_ref[...])
    dt_r = _softplus(dtr_ref[...] + bias_r_ref[...])
    adt_c = dt_c * a_c_ref[...]
    adt_r = dt_r * a_r_ref[...]

    li = lax.broadcasted_iota(jnp.int32, (L, L), 0)
    si = lax.broadcasted_iota(jnp.int32, (L, L), 1)
    mask = (li <= si) if reverse else (li >= si)
    mask_t = (li >= si) if reverse else (li <= si)
    n_sub = rows // L
    yield
    parts = yield from _interleave(*[
        _ssd_chunk_local(xbc[s * L:(s + 1) * L], dt_c[s * L:(s + 1) * L], adt_c[s * L:(s + 1) * L],
                         adt_r[:, s * L:(s + 1) * L], mask, mask_t, reverse) for s in range(n_sub)])
    ys = [None] * n_sub
    for s in (reversed(range(n_sub)) if reverse else range(n_sub)):
        ys[s] = _ssd_chunk_state(*parts[s], st_ref, reverse)
        yield
    return jnp.concatenate(ys, axis=0), xbc[:, :SSD_INNER]


def _split3(x):
    hi = x.astype(BF16)
    r1 = x - hi.astype(F32)
    mid = r1.astype(BF16)
    lo = (r1 - mid.astype(F32)).astype(BF16)
    return hi, mid, lo


def _ssd_chunk_local(xbc, dt_c, adt_c, adt_r, mask, mask_t, reverse):
    L = SSD_CHUNK
    hpg = SSD_HEADS // SSD_GROUPS
    off = SSD_HEADS if reverse else 0
    m_b, mt_b = mask.astype(BF16), mask_t.astype(BF16)
    cs_c = sum(_dot(m_b, part) for part in _split3(adt_c))
    cs_r = sum(_dot(part, mt_b) for part in _split3(adt_r))
    edge = 0 if reverse else L - 1
    tot = cs_c[edge:edge + 1, :]
    yield

    kk = lax.broadcasted_iota(jnp.int32, (LANES, SSD_INNER), 0)
    jj = lax.broadcasted_iota(jnp.int32, (LANES, SSD_INNER), 1)
    spread = (kk == off + jj // SSD_HEAD_DIM).astype(BF16)
    cols = jnp.concatenate([dt_c, jnp.exp(cs_c), jnp.exp(tot - cs_c)], axis=0)
    c_hi = cols.astype(BF16)
    c_lo = (cols - c_hi.astype(F32)).astype(BF16)
    wide = _dot(c_hi, spread) + _dot(c_lo, spread)
    dt_w, ecs_w, dst_w = wide[0:L], wide[L:2 * L], wide[2 * L:3 * L]
    yield

    xs = xbc[:, :SSD_INNER]
    b_all = xbc[:, SSD_INNER:SSD_INNER + LANES]
    c_all = xbc[:, SSD_INNER + LANES:SSD_INNER + 2 * LANES]
    xd = xs * dt_w
    xd_b = xd.astype(BF16)
    b_b = b_all.astype(BF16)
    c_b = c_all.astype(BF16)
    lane = lax.broadcasted_iota(jnp.int32, (L, LANES), 1)
    col = lax.broadcasted_iota(jnp.int32, (L, SSD_INNER), 1)

    y = None
    for g in range(SSD_GROUPS):
        c_g = jnp.where(lane // SSD_STATE == g, c_b, jnp.zeros_like(c_b))
        gmat = _dot_nt(c_g, b_b)
        for hh in range(hpg):
            h = g * hpg + hh
            dec = jnp.exp(jnp.where(mask, cs_c[:, off + h:off + h + 1] - cs_r[off + h:off + h + 1, :], -jnp.inf))
            xd_h = jnp.where(col // SSD_HEAD_DIM == h, xd_b, jnp.zeros_like(xd_b))
            part = _dot((gmat * dec).astype(BF16), xd_h)
            y = part if y is None else y + part
            yield
    return y, c_b, ecs_w, b_all.T.astype(BF16), (xd * dst_w).astype(BF16)


def _ssd_chunk_state(y_local, c_b, ecs_w, bt_b, xds_b, st_ref, reverse):
    edge = 0 if reverse else SSD_CHUNK - 1
    y = y_local + ecs_w * _dot(c_b, st_ref[...].astype(BF16))
    kr = lax.broadcasted_iota(jnp.int32, (LANES, SSD_INNER), 0)
    kc = lax.broadcasted_iota(jnp.int32, (LANES, SSD_INNER), 1)
    block = kr // SSD_STATE == kc // (SSD_HEADS // SSD_GROUPS * SSD_HEAD_DIM)
    st_ref[...] = st_ref[...] * ecs_w[edge:edge + 1, :] + jnp.where(block, _dot(bt_b, xds_b), 0.0)
    return y


def _ssd_body(n_lat, n_chunks, *refs):
    fwd_in, bwd_in = refs[0:5], refs[5:10]
    cw_ref, cb_ref, bias_c_ref, a_c_ref, bias_r_ref, a_r_ref, dsk_ref, yf_ref, yb_ref, st_ref = refs[10:]
    i = pl.program_id(0)

    @pl.when(i == 0)
    def _():
        st_ref[...] = jnp.zeros_like(st_ref)

    shared = (cw_ref, cb_ref, bias_c_ref, a_c_ref, bias_r_ref, a_r_ref)
    (y_f, xs), (y_b, _) = _run(_interleave(
        _ssd_block(_segment_order(i, n_lat, n_chunks, False), n_lat, n_chunks, False, *fwd_in, *shared,
                   st_ref.at[0]),
        _ssd_block(_segment_order(i, n_lat, n_chunks, True), n_lat, n_chunks, True, *bwd_in, *shared,
                   st_ref.at[1])))
    yf_ref[...] = y_f + dsk_ref[...] * xs
    yb_ref[...] = y_b


def _ssd_call(xbc_raw, dt_c, dt_r, cw, cb, bias_c, a_c, bias_r, a_r, dsk, n_lat_rows):
    t = xbc_raw.shape[0]
    L = SSD_STEP_CHUNKS * SSD_CHUNK
    assert t % L == 0 and n_lat_rows % L == 0
    n_chunks = t // L
    n_lat = n_lat_rows // L
    sub = L // 8
    n_sub = t // 8
    const = lambda i: (0, 0)

    def chunk_specs(reverse):
        cidx = functools.partial(_segment_order, n_lat=n_lat, n_all=n_chunks, reverse=reverse)
        return [pl.BlockSpec((L, SSD_CONV_DIM), lambda i: (cidx(i), 0)),
                pl.BlockSpec((8, SSD_CONV_DIM), lambda i: (jnp.maximum(cidx(i) * sub - 1, 0), 0)),
                pl.BlockSpec((8, SSD_CONV_DIM), lambda i: (jnp.minimum((cidx(i) + 1) * sub, n_sub - 1), 0)),
                pl.BlockSpec((L, LANES), lambda i: (cidx(i), 0)),
                pl.BlockSpec((8, L), lambda i: (0, cidx(i)))]

    def out_spec(reverse):
        cidx = functools.partial(_segment_order, n_lat=n_lat, n_all=n_chunks, reverse=reverse)
        return pl.BlockSpec((L, SSD_INNER), lambda i: (cidx(i), 0))

    chunk_args = (xbc_raw, xbc_raw, xbc_raw, dt_c, dt_r)
    return pl.pallas_call(
        functools.partial(_ssd_body, n_lat, n_chunks),
        grid=(n_chunks,),
        in_specs=chunk_specs(False) + chunk_specs(True)
                 + [pl.BlockSpec((8, SSD_CONV_DIM), const),
                    pl.BlockSpec((1, SSD_CONV_DIM), const),
                    pl.BlockSpec((1, LANES), const),
                    pl.BlockSpec((1, LANES), const),
                    pl.BlockSpec((8, L), const),
                    pl.BlockSpec((8, L), const),
                    pl.BlockSpec((1, SSD_INNER), const)],
        out_specs=[out_spec(False), out_spec(True)],
        out_shape=[jax.ShapeDtypeStruct((t, SSD_INNER), F32)] * 2,
        scratch_shapes=[pltpu.VMEM((2, SSD_GROUPS * SSD_STATE, SSD_INNER), F32)],
        compiler_params=_cparams("arbitrary"),
        name="ssd_scan",
    )(*chunk_args, *chunk_args, cw, cb, bias_c, a_c, bias_r, a_r, dsk)


def _cmul(ar, ai, br, bi):
    return ar * br - ai * bi, ar * bi + ai * br


def _s5_step_pair(u_ref, a):
    return jnp.concatenate([u_ref[:, 2 * a, :], u_ref[:, 2 * a + 1, :]], axis=1).astype(BF16)


def _s5_in_body(u_ref, w_ref, lam_ref, efr_ref, efi_ref, ebr_ref, ebi_ref, wp_ref):
    hs = S5_HALF_STATE

    @pl.when(pl.program_id(1) == 0)
    def _():
        lam = lam_ref[0]
        wc = w_ref[0]
        rows = lambda a: jnp.broadcast_to(a, (LANES, hs))
        fr, fi, br, bi = rows(lam[0:1]), rows(lam[1:2]), rows(lam[2:3]), rows(lam[3:4])
        rgrp = lax.broadcasted_iota(jnp.int32, (LANES, LANES), 0) // S5_GROUP
        lgrp = lax.broadcasted_iota(jnp.int32, (LANES, LANES), 1) // S5_STATE
        planes = []
        for q in range(4):
            cq = wc[:, q * S5_STATE:(q + 1) * S5_STATE]
            cq2 = jnp.concatenate([cq, cq], axis=1)
            planes.append(jnp.concatenate(
                [jnp.where(rgrp == 2 * g4 + lgrp, cq2, 0.0) for g4 in range(S5_HALF_GROUPS // 2)], axis=1))
        ar, ai = planes[0], planes[1]
        for i in reversed(range(S5_CHUNK)):
            wp_ref[i * LANES:(i + 1) * LANES, 0:hs] = ar.astype(BF16)
            wp_ref[i * LANES:(i + 1) * LANES, hs:2 * hs] = ai.astype(BF16)
            ar, ai = _cmul(ar, ai, fr, fi)
        ar, ai = planes[2], planes[3]
        for i in range(S5_CHUNK):
            wp_ref[i * LANES:(i + 1) * LANES, 2 * hs:3 * hs] = ar.astype(BF16)
            wp_ref[i * LANES:(i + 1) * LANES, 3 * hs:4 * hs] = ai.astype(BF16)
            ar, ai = _cmul(ar, ai, br, bi)

    e = None
    for a in range(S5_CHUNK // 2):
        part = _dot(_s5_step_pair(u_ref, a), wp_ref[2 * a * LANES:2 * (a + 1) * LANES, :])
        e = part if e is None else e + part
    efr_ref[...] = e[:, 0:hs]
    efi_ref[...] = e[:, hs:2 * hs]
    ebr_ref[...] = e[:, 2 * hs:3 * hs]
    ebi_ref[...] = e[:, 3 * hs:4 * hs]


def _s5_in_call(u3, w_b, lam_rows):
    nch = u3.shape[0]
    tn = _pick_tile(nch, (208, 80, 40))
    hs = S5_HALF_STATE
    col = lambda h, n: (n, h)
    return pl.pallas_call(
        _s5_in_body,
        grid=(2, nch // tn),
        in_specs=[pl.BlockSpec((tn, S5_CHUNK, LANES), lambda h, n: (n, 0, h)),
                  pl.BlockSpec((1, LANES, 4 * S5_STATE), lambda h, n: (h, 0, 0)),
                  pl.BlockSpec((1, 4, hs), lambda h, n: (h, 0, 0))],
        out_specs=[pl.BlockSpec((tn, hs), col)] * 4,
        out_shape=[jax.ShapeDtypeStruct((nch, 2 * hs), F32)] * 4,
        scratch_shapes=[pltpu.VMEM((S5_CHUNK * LANES, 4 * hs), BF16)],
        compiler_params=_cparams("arbitrary", "arbitrary"),
        name="s5_chunk_in",
    )(u3, w_b, lam_rows)


def _s5_scan_body(n_lat_blocks, rows, ctx_rows, a_ref, efr_ref, efi_ref, ebr_ref, ebi_ref,
                  hfr_ref, hfi_ref, hbr_ref, hbi_ref, st_ref):
    i = pl.program_id(0)

    @pl.when(i == 0)
    def _():
        st_ref[...] = jnp.zeros_like(st_ref)

    n_blocks = n_lat_blocks + 1
    live_f = jnp.where(_segment_order(i, n_lat_blocks, n_blocks, False) == n_lat_blocks, ctx_rows, rows)
    live_b = jnp.where(_segment_order(i, n_lat_blocks, n_blocks, True) == n_lat_blocks, ctx_rows, rows)
    row = lambda ref, j: ref[j:j + 1, :]
    afr, afi, abr, abi = row(a_ref, 0), row(a_ref, 1), row(a_ref, 2), row(a_ref, 3)
    fr, fi, br, bi = row(st_ref, 0), row(st_ref, 1), row(st_ref, 2), row(st_ref, 3)
    for j in range(rows):
        hfr_ref[j:j + 1, :] = fr
        hfi_ref[j:j + 1, :] = fi
        nr, ni = afr * fr - afi * fi + row(efr_ref, j), afr * fi + afi * fr + row(efi_ref, j)
        fr, fi = jnp.where(j < live_f, nr, fr), jnp.where(j < live_f, ni, fi)
        jb = rows - 1 - j
        hbr_ref[jb:jb + 1, :] = br
        hbi_ref[jb:jb + 1, :] = bi
        nr, ni = abr * br - abi * bi + row(ebr_ref, jb), abr * bi + abi * br + row(ebi_ref, jb)
        br, bi = jnp.where(jb < live_b, nr, br), jnp.where(jb < live_b, ni, bi)
    st_ref[0:1, :] = fr
    st_ref[1:2, :] = fi
    st_ref[2:3, :] = br
    st_ref[3:4, :] = bi


def _s5_scan_call(a_pow, efr, efi, ebr, ebi, n_lat_rows):
    nch, width = efr.shape
    rows = _pick_tile(n_lat_rows, (S5_SCAN_ROWS, 32, 16))
    ctx_rows = nch - n_lat_rows
    assert ctx_rows <= rows
    n_lat_blocks = n_lat_rows // rows
    n_tiles = n_lat_blocks + 1
    blk = (rows, width)
    fwd = lambda i: (_segment_order(i, n_lat_blocks, n_tiles, False), 0)
    bwd = lambda i: (_segment_order(i, n_lat_blocks, n_tiles, True), 0)
    shp = jax.ShapeDtypeStruct((nch, width), F32)
    return pl.pallas_call(
        functools.partial(_s5_scan_body, n_lat_blocks, rows, ctx_rows),
        grid=(n_tiles,),
        in_specs=[pl.BlockSpec((4, width), lambda i: (0, 0)),
                  pl.BlockSpec(blk, fwd), pl.BlockSpec(blk, fwd),
                  pl.BlockSpec(blk, bwd), pl.BlockSpec(blk, bwd)],
        out_specs=[pl.BlockSpec(blk, fwd), pl.BlockSpec(blk, fwd),
                   pl.BlockSpec(blk, bwd), pl.BlockSpec(blk, bwd)],
        out_shape=[shp] * 4,
        scratch_shapes=[pltpu.VMEM((4, width), F32)],
        compiler_params=_cparams("arbitrary"),
        name="s5_state_scan",
    )(a_pow, efr, efi, ebr, ebi)


def _s5_out_body(u_ref, hfr_ref, hfi_ref, hbr_ref, hbi_ref, k_ref, t_ref, c_ref, lam_ref, y_ref, wq_ref, z_ref):
    hs = S5_HALF_STATE
    width = S5_CHUNK * LANES

    @pl.when(pl.program_id(1) == 0)
    def _():
        strip = _dot(k_ref[0].astype(BF16), t_ref[...])
        rgrp = lax.broadcasted_iota(jnp.int32, strip.shape, 0) // S5_GROUP
        cgrp = (lax.broadcasted_iota(jnp.int32, strip.shape, 1) % LANES) // S5_GROUP
        strip = jnp.where(rgrp == cgrp, strip, 0.0).astype(BF16)
        z_ref[0:LANES, :] = strip[:, LANES:]
        z_ref[LANES:, :] = strip[:, :-LANES]

        cc = c_ref[0]
        ogrp = lax.broadcasted_iota(jnp.int32, (S5_STATE, LANES), 1) // S5_GROUP

        def expand(q):
            cq = cc[q * S5_STATE:(q + 1) * S5_STATE]
            return jnp.concatenate([jnp.where(ogrp == g, cq, 0.0) for g in range(S5_HALF_GROUPS)], axis=0)

        lam = lam_ref[0]
        for d, order in ((0, range(S5_CHUNK)), (1, reversed(range(S5_CHUNK)))):
            lr, li = lam[2 * d], lam[2 * d + 1]
            a, b = expand(2 * d), expand(2 * d + 1)
            for j in order:
                a, b = a * lr + b * li, b * lr - a * li
                wq_ref[2 * d * hs:(2 * d + 1) * hs, j * LANES:(j + 1) * LANES] = a.astype(BF16)
                wq_ref[(2 * d + 1) * hs:(2 * d + 2) * hs, j * LANES:(j + 1) * LANES] = b.astype(BF16)

    h_all = jnp.concatenate([hfr_ref[...], hfi_ref[...], hbr_ref[...], hbi_ref[...]], axis=1).astype(BF16)
    y = _dot(h_all, wq_ref[...])
    for a in range(S5_CHUNK // 2):
        lo = (S5_CHUNK - 2 - 2 * a) * LANES
        y += _dot(_s5_step_pair(u_ref, a), z_ref[:, lo:lo + width])
    for j in range(S5_CHUNK):
        y_ref[:, j, :] = y[:, j * LANES:(j + 1) * LANES]


def _s5_out_call(u3, hfr, hfi, hbr, hbi, w_k, w_t, w_c, lam_cols):
    nch = u3.shape[0]
    tn = _pick_tile(nch, (208, 80, 40))
    hs = S5_HALF_STATE
    col = lambda h, n: (n, h)
    blk3 = pl.BlockSpec((tn, S5_CHUNK, LANES), lambda h, n: (n, 0, h))
    return pl.pallas_call(
        _s5_out_body,
        grid=(2, nch // tn),
        in_specs=[blk3] + [pl.BlockSpec((tn, hs), col)] * 4
                 + [pl.BlockSpec((1, LANES) + w_k.shape[2:], lambda h, n: (h, 0, 0)),
                    pl.BlockSpec(w_t.shape, lambda h, n: (0, 0)),
                    pl.BlockSpec((1, 4 * S5_STATE, LANES), lambda h, n: (h, 0, 0)),
                    pl.BlockSpec((1, 4, hs, LANES), lambda h, n: (h, 0, 0, 0))],
        out_specs=blk3,
        out_shape=jax.ShapeDtypeStruct(u3.shape, F32),
        scratch_shapes=[pltpu.VMEM((4 * hs, S5_CHUNK * LANES), BF16),
                        pltpu.VMEM((2 * LANES, (S5_TOEPLITZ_BLOCKS - 1) * LANES), BF16)],
        compiler_params=_cparams("arbitrary", "arbitrary"),
        name="s5_chunk_out",
    )(u3, hfr, hfi, hbr, hbi, w_k, w_t, w_c, lam_cols)


def _s5_weights(lam_re, lam_im, log_step, b_re, b_im, c_re, c_im):
    tc = S5_CHUNK
    nl = lam_re.shape[0]
    hg = S5_HALF_GROUPS
    ein = functools.partial(jnp.einsum, precision=HI)
    step = jnp.exp(log_step)[..., None]
    er, ei = lam_re * step, lam_im * step
    mag = jnp.exp(er)
    lbr, lbi = mag * jnp.cos(ei), mag * jnp.sin(ei)
    den = lam_re * lam_re + lam_im * lam_im
    nr, ni = lbr - 1.0, lbi
    cr, ci = (nr * lam_re + ni * lam_im) / den, (ni * lam_re - nr * lam_im) / den
    bbr = cr[..., None] * b_re - ci[..., None] * b_im
    bbi = cr[..., None] * b_im + ci[..., None] * b_re
    taus = jnp.arange(tc + 1, dtype=F32)[:, None, None]
    pmag = jnp.exp(er[:, :, None] * taus)
    pr, pi = pmag * jnp.cos(ei[:, :, None] * taus), pmag * jnp.sin(ei[:, :, None] * taus)

    wr = pr[:, :, :tc, :, :, None] * bbr[:, :, None] - pi[:, :, :tc, :, :, None] * bbi[:, :, None]
    wi = pr[:, :, :tc, :, :, None] * bbi[:, :, None] + pi[:, :, :tc, :, :, None] * bbr[:, :, None]
    kk = ein('ldgcp,ldtgpk->ldtgck', c_re, wr) - ein('ldgcp,ldtgpk->ldtgck', c_im, wi)
    kf, kb = kk[:, 0], kk[:, 1]
    seq = jnp.concatenate([kb[:, :0:-1], kf[:, :1] + kb[:, :1], kf[:, 1:]], axis=1)
    seq = seq.reshape(nl, S5_TOEPLITZ_BLOCKS, 2, hg, S5_GROUP, S5_GROUP)
    w_k = seq.transpose(0, 2, 3, 5, 1, 4).reshape(nl, 2, LANES, S5_TOEPLITZ_BLOCKS * S5_GROUP)
    w_k = jnp.pad(w_k, ((0, 0), (0, 0), (0, 0), (0, S5_KCOLS - S5_TOEPLITZ_BLOCKS * S5_GROUP)))
    r = jnp.arange(S5_KCOLS)
    s = jnp.arange(S5_TOEPLITZ_BLOCKS * LANES)
    w_t = ((r[:, None] // S5_GROUP == s[None, :] // LANES)
           & (r[:, None] % S5_GROUP == s[None, :] % S5_GROUP)).astype(BF16)

    planes_b = jnp.stack([bbr[:, 0], bbi[:, 0], bbr[:, 1], bbi[:, 1]], axis=1)
    planes_b = planes_b.reshape(nl, 4, 2, hg, S5_STATE, S5_GROUP)
    w_b = planes_b.transpose(0, 2, 3, 5, 1, 4).reshape(nl, 2, LANES, 4 * S5_STATE)
    planes_c = jnp.stack([c_re[:, 0], -c_im[:, 0], c_re[:, 1], -c_im[:, 1]], axis=1)
    planes_c = planes_c.reshape(nl, 4, 2, hg, S5_GROUP, S5_STATE)
    w_c = planes_c.transpose(0, 2, 1, 5, 3, 4).reshape(nl, 2, 4 * S5_STATE, LANES)

    halves = lambda a: a.reshape(nl, 2, S5_HALF_STATE)
    lam_rows = jnp.stack([halves(lbr[:, 0]), halves(lbi[:, 0]), halves(lbr[:, 1]), halves(lbi[:, 1])], axis=2)
    lam_cols = jnp.broadcast_to(lam_rows[..., None], lam_rows.shape + (LANES,))
    full = lambda a: a.reshape(nl, S5_GROUPS * S5_STATE)
    a_pow = jnp.stack([full(pr[:, 0, tc]), full(pi[:, 0, tc]), full(pr[:, 1, tc]), full(pi[:, 1, tc])], axis=1)
    return (w_b, w_k, w_c, lam_rows, lam_cols, a_pow), w_t


def _s5_mix(u, w_b, w_k, w_c, lam_rows, lam_cols, a_pow, w_t, n_lat):
    t = u.shape[0]
    nch = t // S5_CHUNK
    u3 = u.reshape(nch, S5_CHUNK, S5_WIDTH)
    es = _s5_in_call(u3, w_b, lam_rows)
    hs = _s5_scan_call(a_pow, *es, n_lat // S5_CHUNK)
    return _s5_out_call(u3, *hs, w_k, w_t, w_c, lam_cols).reshape(t, S5_WIDTH)


def _attn_body(flag_ref, lam_ref, q_ref, k_ref, v_ref, g_ref, o_ref, acc_ref, m_ref, oh_ref, *, n_kv, post_scale):
    bounded = flag_ref[0] == 1

    def head(hh, carry):
        acc_ref[...] = jnp.zeros_like(acc_ref)

        @pl.when(bounded)
        def _():
            def step(b, carry2):
                k = k_ref[hh, b]
                vt = v_ref[hh, b]
                for c in range(2):
                    p = jnp.exp2(_dot(k, q_ref[hh, c])).astype(BF16)
                    acc_ref[c] += _dot(vt, p)
                return carry2
            lax.fori_loop(0, n_kv, step, 0)

        @pl.when(jnp.logical_not(bounded))
        def _():
            m_ref[...] = jnp.full_like(m_ref, -jnp.inf)

            def step(b, carry2):
                k = k_ref[hh, b]
                vt = v_ref[hh, b]
                for c in range(2):
                    s = _dot(k, q_ref[hh, c])
                    m_old = m_ref[c]
                    m_new = jnp.maximum(m_old, jnp.max(s, axis=0, keepdims=True))
                    p = jnp.exp2(s - m_new[0:1]).astype(BF16)
                    acc_ref[c] = jnp.exp2(m_old - m_new)[0:1] * acc_ref[c] + _dot(vt, p)
                    m_ref[c] = m_new
                return carry2
            lax.fori_loop(0, n_kv, step, 0)

        a0 = acc_ref[0]
        a1 = acc_ref[1]
        o = a0 / a0[DA_V_DIM:DA_V_DIM + 1] - lam_ref[0] * (a1 / a1[DA_V_DIM:DA_V_DIM + 1])
        vrow = lax.broadcasted_iota(jnp.int32, o.shape, 0) < DA_V_DIM
        o = jnp.where(vrow, o, 0.0)
        ms = jnp.sum(o * o, axis=0, keepdims=True) * (1.0 / DA_V_DIM)
        ot = (o * lax.rsqrt(ms + EPS)).T
        oh_ref[hh] = ot[:, :DA_V_DIM] * g_ref[...] * post_scale
        return carry

    lax.fori_loop(0, ATTN_HEADS_PER_STEP, head, 0)
    o_ref[...] = jnp.concatenate([oh_ref[hh] for hh in range(ATTN_HEADS_PER_STEP)], axis=1)


def _attn_body_keep_rest(flag_ref, lam_ref, q_ref, k_ref, v_ref, g_ref, rest_ref, *refs, **kw):
    del rest_ref
    _attn_body(flag_ref, lam_ref, q_ref, k_ref, v_ref, g_ref, *refs, **kw)


def _attn_call(flag, lam, qt, kb, vtb, g_sub, post_scale, q_rows, q_row0, kv_block, kv_cols, kv_col0, y_rest=None):
    nh = qt.shape[0]
    hp = ATTN_HEADS_PER_STEP
    tk = kb.shape[2]
    tq = _pick_tile(q_rows, (ATTN_TQ, 256))
    assert q_row0 % tq == 0
    q0 = q_row0 // tq
    if kv_block is None:
        n_kv = kb.shape[1]
        one = pl.Buffered(1)
        k_spec = pl.BlockSpec((hp, n_kv, tk, 2 * DA_HEAD_DIM), lambda h, i: (h, 0, 0, 0), pipeline_mode=one)
        v_spec = pl.BlockSpec((hp, n_kv, LANES, tk), lambda h, i: (h, 0, 0, 0), pipeline_mode=one)
    else:
        assert kv_col0 % kv_cols == 0 and tk % kv_cols == 0
        n_kv, cb = 1, kv_col0 // kv_cols
        k_spec = pl.BlockSpec((hp, 1, kv_cols, 2 * DA_HEAD_DIM), lambda h, i: (h, kv_block, cb, 0))
        v_spec = pl.BlockSpec((hp, 1, LANES, kv_cols), lambda h, i: (h, kv_block, 0, cb))
    smem = pl.BlockSpec(memory_space=pltpu.SMEM)
    in_specs = [smem, smem,
                pl.BlockSpec((hp, 2, 2 * DA_HEAD_DIM, tq), lambda h, i: (h, 0, 0, q0 + i)),
                k_spec, v_spec,
                pl.BlockSpec((1, DA_V_DIM), lambda h, i: (0, 0))]
    args = (flag, lam, qt, kb, vtb, g_sub)
    body = functools.partial(_attn_body, n_kv=n_kv, post_scale=post_scale)
    aliases = {}
    if y_rest is not None:
        in_specs.append(pl.BlockSpec(memory_space=pl.ANY))
        args += (y_rest,)
        body = functools.partial(_attn_body_keep_rest, n_kv=n_kv, post_scale=post_scale)
        aliases = {len(args) - 1: 0}
    return pl.pallas_call(
        body,
        grid=(nh // hp, q_rows // tq),
        in_specs=in_specs,
        out_specs=pl.BlockSpec((tq, hp * DA_V_DIM), lambda h, i: (q0 + i, h)),
        out_shape=jax.ShapeDtypeStruct((qt.shape[3], nh * DA_V_DIM), F32),
        scratch_shapes=[pltpu.VMEM((2, LANES, tq), F32), pltpu.VMEM((2, 8, tq), F32),
                        pltpu.VMEM((hp, tq, DA_V_DIM), F32)],
        input_output_aliases=aliases,
        compiler_params=_cparams("arbitrary", "arbitrary"),
        name="diff_attn",
    )(*args)


def _gelu_tanh(x):
    return 0.5 * x * (1.0 + jnp.tanh(math.sqrt(2.0 / math.pi) * (x + 0.044715 * (x * x * x))))


def _outffn_body(x_ref, yf_ref, yb_ref, z_ref, gssd_ref, ys5_ref, u_ref, d5_ref, gw_ref, gb_ref,
                 yda_ref, mod_ref, modp_ref, g2_ref, wo_ref, w1_ref, w3_ref, w2_ref, o_ref, xm_ref, hb_ref):
    i = pl.program_id(0)
    cur = i % 2
    prev = 1 - cur

    @pl.when(i == 0)
    def _():
        xm_ref[...] = jnp.zeros_like(xm_ref)
        hb_ref[...] = jnp.zeros_like(hb_ref)

    def finish():
        hp = hb_ref[prev]
        a1 = _dot(hp, w1_ref[...])
        yield
        a3 = _dot(hp, w3_ref[...])
        yield
        f = (_silu(a1) * a3).astype(BF16)
        yield
        o_ref[...] = xm_ref[prev] + modp_ref[0][5:6] * _dot(f, w2_ref[...])

    def prepare():
        mod = mod_ref[0]
        y = (yf_ref[...] + yb_ref[...]) * _silu(z_ref[...])
        y_ssd = y * lax.rsqrt(jnp.mean(y * y, axis=-1, keepdims=True) + EPS) * gssd_ref[...]
        yield
        y5 = _gelu_tanh(ys5_ref[...] + d5_ref[...] * u_ref[...])
        yield
        y5 = y5 * jax.nn.sigmoid(_dot(y5.astype(BF16), gw_ref[...].astype(BF16)) + gb_ref[...])
        yield
        o = _dot(y_ssd.astype(BF16), wo_ref[0:256, :])
        yield
        o += _dot(y5.astype(BF16), wo_ref[256:512, :])
        yield
        o += _dot(yda_ref[...].astype(BF16), wo_ref[512:1024, :])
        yield
        x = x_ref[...] + mod[2:3] * o
        xm_ref[cur] = x
        yield
        h = x * lax.rsqrt(jnp.mean(x * x, axis=-1, keepdims=True) + EPS) * g2_ref[...] * (1.0 + mod[4:5]) \
            + mod[3:4]
        hb_ref[cur] = h.astype(BF16)

    _run(_interleave(finish(), prepare()))


def _outffn_call(xs, y_f, y_b, z, g_ssd, y_s5, u, d5, glu_w, glu_b, y_da, modl, g2, w_o, w1, w3, w2, layer, n_lat,
                 latent_only):
    t, d = xs.shape
    if latent_only:
        t = n_lat
    dff = w1.shape[2]
    layer_block = lambda i: (layer, 0, 0)
    tm = FFN_ROW_TILE
    assert n_lat % tm == 0 and t % tm == 0
    n_lat_tiles = n_lat // tm
    n_tiles = t // tm
    row = lambda i: (jnp.minimum(i, n_tiles - 1), 0)
    done = lambda i: (jnp.maximum(i - 1, 0), 0)
    const = lambda i: (0, 0)
    seg = lambda i: ((jnp.minimum(i, n_tiles - 1) >= n_lat_tiles).astype(jnp.int32), 0, 0)
    seg_done = lambda i: ((i - 1 >= n_lat_tiles).astype(jnp.int32), 0, 0)
    one = pl.Buffered(1)
    return pl.pallas_call(
        _outffn_body,
        grid=(n_tiles + 1,),
        in_specs=[pl.BlockSpec((tm, d), row),
                  pl.BlockSpec((tm, 256), row), pl.BlockSpec((tm, 256), row), pl.BlockSpec((tm, 256), row),
                  pl.BlockSpec((1, 256), const),
                  pl.BlockSpec((tm, 256), row), pl.BlockSpec((tm, 256), row),
                  pl.BlockSpec((1, 256), const),
                  pl.BlockSpec((256, 256), const), pl.BlockSpec((1, 256), const),
                  pl.BlockSpec((tm, DA_WIDTH), row),
                  pl.BlockSpec((1, 8, d), seg),
                  pl.BlockSpec((1, 8, d), seg_done),
                  pl.BlockSpec((1, d), const),
                  pl.BlockSpec((None, d, d), layer_block, pipeline_mode=one),
                  pl.BlockSpec((None, d, dff), layer_block, pipeline_mode=one),
                  pl.BlockSpec((None, d, dff), layer_block, pipeline_mode=one),
                  pl.BlockSpec((None, dff, d), layer_block, pipeline_mode=one)],
        out_specs=pl.BlockSpec((tm, d), done),
        out_shape=jax.ShapeDtypeStruct((t, d), F32),
        scratch_shapes=[pltpu.VMEM((2, tm, d), F32), pltpu.VMEM((2, tm, d), BF16)],
        compiler_params=_cparams("arbitrary"),
        name="out_proj_ffn",
    )(xs, y_f, y_b, z, g_ssd, y_s5, u, d5, glu_w, glu_b, y_da, modl, modl, g2, w_o, w1, w3, w2)


def _cast_body(w_ref, o_ref):
    o_ref[...] = w_ref[...].astype(o_ref.dtype)


def _to_bf16(w):
    nl, rows, cols = w.shape
    tr = _pick_tile(rows, (512, 704, 256))
    blk = pl.BlockSpec((1, tr, cols), lambda l, r: (l, r, 0))
    return pl.pallas_call(
        _cast_body,
        grid=(nl, rows // tr),
        in_specs=[blk],
        out_specs=blk,
        out_shape=jax.ShapeDtypeStruct(w.shape, BF16),
        compiler_params=_cparams("arbitrary", "arbitrary"),
        name="weights_to_bf16",
    )(w)


def _rope_tables(n_lat, n_ctx):
    pos = jnp.arange(n_lat)
    lane = jnp.arange(LANES)
    d = lane % DA_HEAD_DIM
    axis = d // (2 * ROPE_FREQS)
    half = (d % (2 * ROPE_FREQS)) // ROPE_FREQS
    freqs = ROPE_THETA ** (-jnp.arange(ROPE_FREQS, dtype=F32) / ROPE_FREQS)
    coord = jnp.where(axis[None, :] == 0, (pos // GRID_W)[:, None], (pos % GRID_W)[:, None]).astype(F32)
    ang = coord * freqs[d % ROPE_FREQS][None, :]
    cos, sin = jnp.cos(ang), jnp.sin(ang)
    sa = jnp.where(half[None, :] == 0, -sin, 0.0)
    sb = jnp.where(half[None, :] == 1, sin, 0.0)
    pad = lambda a, v: jnp.concatenate([a, jnp.full((n_ctx, LANES), v, F32)], axis=0)
    return pad(cos, 1.0), pad(sa, 0.0), pad(sb, 0.0)


def _lane_row(vals, width=LANES):
    vals = vals.reshape(-1).astype(F32)
    return jnp.zeros((1, width), F32).at[0, :vals.shape[0]].set(vals)


def kernel(x, c, ctx, c_ctx, w_mod, b_mod, norm1, norm2, w_in, w_out, ssd_conv_w, ssd_conv_b, ssd_a_log, ssd_dt_bias, ssd_d, ssd_norm, s5_lam_re, s5_lam_im, s5_log_step, s5_b_re, s5_b_im, s5_c_re, s5_c_im, s5_d, s5_glu_w, s5_glu_b, da_q_norm, da_k_norm, da_lambda, da_sub_norm, ffn_w1, ffn_w3, ffn_w2):
    depth = w_mod.shape[0]
    bsz, n_lat, d = x.shape
    n_ctx = ctx.shape[1]
    t = n_ctx + n_lat
    tk = _pick_tile(t, (ATTN_TK, 1024, 640, 512, 256))
    assert bsz == 1 and n_ctx % ROW_TILE == 0 and n_lat % ROW_TILE == 0 and tk % ROW_TILE == 0
    assert n_ctx % (S5_CHUNK * 16) == 0 and n_lat % (S5_CHUNK * 16) == 0
    assert n_lat % n_ctx == 0 and tk % n_ctx == 0
    n_lat_tiles = n_lat // ROW_TILE

    xs = jnp.concatenate([x[0], ctx[0]], axis=0)
    cv = jnp.zeros((8, d), F32).at[0].set(c[0]).at[1].set(c_ctx)
    mod_all = _mod_call(cv, w_mod, b_mod)
    cos_t, sa_t, sb_t = _rope_tables(n_lat, n_ctx)
    gidx = jnp.arange(DA_WIDTH) // DA_HEAD_DIM
    gmat = (gidx[:, None] == gidx[None, :]).astype(BF16) * (1.0 / DA_HEAD_DIM)
    s5_w, s5_spread = _s5_weights(s5_lam_re, s5_lam_im, s5_log_step, s5_b_re, s5_b_im, s5_c_re, s5_c_im)
    w_out_b, w1_b, w3_b, w2_b = _to_bf16(w_out), _to_bf16(ffn_w1), _to_bf16(ffn_w3), _to_bf16(ffn_w2)

    for i in range(depth):
        lam_init = 0.8 - 0.6 * math.exp(-0.3 * i)
        modl = jnp.zeros((2, 8, d), F32).at[:, :6].set(mod_all[i, :2].reshape(2, 6, d))

        wi = w_in[i]
        w_r = jnp.concatenate([wi[:, 0:768], wi[:, 776:1032], wi[:, 1032:2568], wi[:, 768:776],
                               jnp.zeros((d, IN_PAD - 2568), F32)], axis=1).astype(BF16)
        gq = jnp.tile(da_q_norm[i], DA_WIDTH // DA_HEAD_DIM)[None, :]
        gk = jnp.tile(da_k_norm[i], DA_WIDTH // DA_HEAD_DIM)[None, :]
        z, xbc_raw, u, dt_raw, qh, kzb, vb = _inproj_call(
            xs, norm1[i][None, :], modl, w_r, gq, gk, gmat, cos_t, sa_t, sb_t, n_lat_tiles, tk)

        dt_r = dt_raw[:, :8].T
        cw = jnp.zeros((8, SSD_CONV_DIM), F32).at[:3].set(ssd_conv_w[i])
        a_neg = -jnp.exp(ssd_a_log[i])
        bias_c = _lane_row(ssd_dt_bias[i])
        a_c = _lane_row(a_neg)
        bias_r = jnp.broadcast_to(ssd_dt_bias[i].reshape(8, 1), (8, SSD_STEP_CHUNKS * SSD_CHUNK))
        a_r = jnp.broadcast_to(a_neg.reshape(8, 1), (8, SSD_STEP_CHUNKS * SSD_CHUNK))
        dsk = jnp.repeat(ssd_d[i], SSD_HEAD_DIM)[None, :]
        y_f, y_b = _ssd_call(xbc_raw, dt_raw, dt_r, cw, ssd_conv_b[i][None, :], bias_c, a_c, bias_r, a_r, dsk,
                             n_lat)

        y_s5 = _s5_mix(u, *(w[i] for w in s5_w), s5_spread, n_lat)

        lf = da_lambda[i]
        lam = (jnp.exp(jnp.sum(lf[0] * lf[1])) - jnp.exp(jnp.sum(lf[2] * lf[3])) + lam_init).reshape(1)
        g_sub = da_sub_norm[i][None, :]
        score_bound = LOG2E * math.sqrt(DA_HEAD_DIM) * jnp.max(jnp.abs(da_q_norm[i])) * jnp.max(jnp.abs(da_k_norm[i]))
        flag = (score_bound <= SCORE_LOG2_LIMIT).astype(jnp.int32).reshape(1)
        y_da = _attn_call(flag, lam, qh, kzb, vb, g_sub, 1.0 - lam_init, n_lat, 0, None, None, None)
        last = i == depth - 1
        if not last:
            y_da = _attn_call(flag, lam, qh, kzb, vb, g_sub, 1.0 - lam_init, n_ctx, n_lat,
                              n_lat // tk, n_ctx, n_lat % tk, y_rest=y_da)

        xs = _outffn_call(xs, y_f, y_b, z, ssd_norm[i][None, :], y_s5, u, s5_d[i][None, :], s5_glu_w[i],
                          s5_glu_b[i][None, :], y_da, modl, norm2[i][None, :], w_out_b, w1_b, w3_b, w2_b, i,
                          n_lat, latent_only=last)
    return xs[None]
```

```python
import functools
import math

import jax
import jax.numpy as jnp
from jax import lax
from jax.experimental import pallas as pl
from jax.experimental.pallas import tpu as pltpu

F32 = jnp.float32
BF16 = jnp.bfloat16
HI = lax.Precision.HIGHEST

EPS = 1e-6
GRID_W = 64
ROPE_THETA = 10000.0
ROPE_FREQS = 8

SSD_HEADS = 4
SSD_HEAD_DIM = 64
SSD_INNER = 256
SSD_GROUPS = 2
SSD_STATE = 64
SSD_CHUNK = 128
SSD_STEP_CHUNKS = 2
SSD_CONV_DIM = 512

S5_WIDTH = 256
S5_GROUP = 16
S5_GROUPS = 16
S5_STATE = 64
S5_CHUNK = 16
S5_HALF_GROUPS = S5_GROUPS // 2
S5_HALF_STATE = S5_HALF_GROUPS * S5_STATE
S5_SCAN_ROWS = 64
S5_TOEPLITZ_BLOCKS = 2 * S5_CHUNK - 1
S5_KCOLS = 512

DA_HEADS = 8
DA_HEAD_DIM = 32
DA_V_DIM = 64
DA_WIDTH = 512
ATTN_HEADS_PER_STEP = 2
ATTN_TQ = 2048
ATTN_TK = 1280
LOG2E = 1.4426950408889634
SCORE_LOG2_LIMIT = 100.0

LANES = 128
ROW_TILE = 256
FFN_ROW_TILE = 256
IN_PAD = 2688
VMEM_LIMIT = 56 * 1024 * 1024


def _cparams(*sem):
    return pltpu.CompilerParams(dimension_semantics=sem, vmem_limit_bytes=VMEM_LIMIT)


def _dot(a, b, precision=None):
    return jnp.dot(a, b, preferred_element_type=F32, precision=precision)


def _dot_nt(a, b, precision=None):
    return lax.dot_general(a, b, (((1,), (1,)), ((), ())), preferred_element_type=F32,
                           precision=precision)


def _dot_tn(a, b, precision=None):
    return lax.dot_general(a, b, (((0,), (0,)), ((), ())), preferred_element_type=F32,
                           precision=precision)


def _silu(x):
    return x * jax.nn.sigmoid(x)


def _pick_tile(n, candidates):
    for c in candidates:
        if n % c == 0:
            return c
    return n


def _interleave(*stages):
    results = [None] * len(stages)
    live = dict(enumerate(stages))
    while live:
        for k in list(live):
            try:
                next(live[k])
            except StopIteration as done:
                results[k] = done.value
                del live[k]
        yield
    return results


def _run(stage):
    while True:
        try:
            next(stage)
        except StopIteration as done:
            return done.value


def _segment_order(i, n_lat, n_all, reverse):
    n_ctx = n_all - n_lat
    if not reverse:
        return jnp.where(i < n_ctx, n_lat + i, i - n_ctx)
    return jnp.where(i < n_ctx, n_all - 1 - i, n_lat - 1 - (i - n_ctx))


def _mod_body(cv_ref, w_ref, b_ref, o_ref):
    o_ref[0] = _dot(_silu(cv_ref[...]), w_ref[0], HI) + b_ref[0]


def _mod_call(cv, w_mod, b_mod):
    depth, d, n = w_mod.shape
    tn = 2048
    return pl.pallas_call(
        _mod_body,
        grid=(depth, n // tn),
        in_specs=[pl.BlockSpec((8, d), lambda l, j: (0, 0)),
                  pl.BlockSpec((1, d, tn), lambda l, j: (l, 0, j)),
                  pl.BlockSpec((1, 1, tn), lambda l, j: (l, 0, j))],
        out_specs=pl.BlockSpec((1, 8, tn), lambda l, j: (l, 0, j)),
        out_shape=jax.ShapeDtypeStruct((depth, 8, n), F32),
        compiler_params=_cparams("arbitrary", "arbitrary"),
        name="adaln_mod",
    )(cv, w_mod, b_mod.reshape(depth, 1, n))


def _inproj_project(x_ref, g_ref, mod_ref, w_ref, z_ref, xbc_ref, u_ref, dt_ref, qkv_ref):
    x = x_ref[...]
    mod = mod_ref[0]
    ms = jnp.mean(x * x, axis=-1, keepdims=True)
    h = x * lax.rsqrt(ms + EPS) * g_ref[...] * (1.0 + mod[1:2]) + mod[0:1]
    p = _dot(h.astype(BF16), w_ref[...])
    z_ref[...] = p[:, 0:256]
    xbc_ref[...] = p[:, 256:768]
    u_ref[...] = p[:, 768:1024]
    dt_ref[...] = p[:, 2560:2688]
    qkv_ref[...] = p[:, 1024:2560]


def _inproj_attn_operands(qkv_ref, gq_ref, gk_ref, gm_ref, cos_ref, sa_ref, sb_ref, q_ref, k_ref, v_ref):
    p = qkv_ref[...]
    rows = p.shape[0]
    gm = gm_ref[...]
    cos = cos_ref[...]
    sa = sa_ref[...]
    sb = sb_ref[...]

    def norm_rope(t, gain, scale):
        ms32 = _dot((t * t).astype(BF16), gm)
        tn = t * lax.rsqrt(ms32 + EPS) * gain
        outs = []
        for j in range(DA_WIDTH // LANES):
            tb = tn[:, j * LANES:(j + 1) * LANES]
            ob = tb * cos + pltpu.roll(tb, LANES - ROPE_FREQS, 1) * sa + pltpu.roll(tb, ROPE_FREQS, 1) * sb
            outs.append(ob * scale)
        return jnp.concatenate(outs, axis=1)

    qt = norm_rope(p[:, 0:512], gq_ref[...], LOG2E * DA_HEAD_DIM ** -0.5).T.astype(BF16)
    kn = norm_rope(p[:, 512:1024], gk_ref[...], 1.0).astype(BF16)
    vt = p[:, 1024:1536].T.astype(BF16)
    srow = lax.broadcasted_iota(jnp.int32, (LANES - DA_V_DIM, rows), 0)
    ones_row = jnp.where(srow == 0, 1.0, 0.0).astype(BF16)
    zeros_q = jnp.zeros((DA_HEAD_DIM, rows), BF16)
    for hd in range(DA_HEADS):
        lo_, mid, hi_ = hd * DA_V_DIM, hd * DA_V_DIM + DA_HEAD_DIM, (hd + 1) * DA_V_DIM
        k_ref[hd, 0] = kn[:, lo_:hi_]
        q_ref[hd, 0, 0:DA_HEAD_DIM, :] = qt[lo_:mid]
        q_ref[hd, 0, DA_HEAD_DIM:, :] = zeros_q
        q_ref[hd, 1, 0:DA_HEAD_DIM, :] = zeros_q
        q_ref[hd, 1, DA_HEAD_DIM:, :] = qt[mid:hi_]
        v_ref[hd, 0, 0:DA_V_DIM, :] = vt[lo_:hi_]
        v_ref[hd, 0, DA_V_DIM:, :] = ones_row


def _inproj_body(x_ref, g_ref, mod_ref, w_ref, gq_ref, gk_ref, gm_ref, cos_ref, sa_ref, sb_ref,
                 z_ref, xbc_ref, u_ref, dt_ref, q_ref, k_ref, v_ref, qkv_a_ref, qkv_b_ref):
    i = pl.program_id(0)

    @pl.when(i == 0)
    def _():
        qkv_b_ref[...] = jnp.zeros_like(qkv_b_ref)

    def step(cur_ref, prev_ref):
        _inproj_attn_operands(prev_ref, gq_ref, gk_ref, gm_ref, cos_ref, sa_ref, sb_ref, q_ref, k_ref, v_ref)
        _inproj_project(x_ref, g_ref, mod_ref, w_ref, z_ref, xbc_ref, u_ref, dt_ref, cur_ref)

    @pl.when(i % 2 == 0)
    def _():
        step(qkv_a_ref, qkv_b_ref)

    @pl.when(i % 2 == 1)
    def _():
        step(qkv_b_ref, qkv_a_ref)


def _inproj_call(xs, g1, modl, w_r, gq, gk, gmat, cos_t, sa_t, sb_t, n_lat_tiles, tk):
    t, d = xs.shape
    tm = ROW_TILE
    n_tiles = t // tm
    per = tk // tm
    row = lambda i: (jnp.minimum(i, n_tiles - 1), 0)
    done = lambda i: (jnp.maximum(i - 1, 0), 0)
    const = lambda i: (0, 0)
    seg = lambda i: ((jnp.minimum(i, n_tiles - 1) >= n_lat_tiles).astype(jnp.int32), 0, 0)
    flat = [(256, F32), (512, F32), (256, F32), (LANES, F32)]

    def k_block(i):
        j = jnp.maximum(i - 1, 0)
        return (0, j // per, j % per, 0)

    def v_block(i):
        j = jnp.maximum(i - 1, 0)
        return (0, j // per, 0, j % per)

    return pl.pallas_call(
        _inproj_body,
        grid=(n_tiles + 1,),
        in_specs=[pl.BlockSpec((tm, d), row),
                  pl.BlockSpec((1, d), const),
                  pl.BlockSpec((1, 8, d), seg),
                  pl.BlockSpec((d, IN_PAD), const),
                  pl.BlockSpec((1, DA_WIDTH), const),
                  pl.BlockSpec((1, DA_WIDTH), const),
                  pl.BlockSpec((DA_WIDTH, DA_WIDTH), const),
                  pl.BlockSpec((tm, LANES), done),
                  pl.BlockSpec((tm, LANES), done),
                  pl.BlockSpec((tm, LANES), done)],
        out_specs=[pl.BlockSpec((tm, w), row) for w, _ in flat]
                  + [pl.BlockSpec((DA_HEADS, 2, 2 * DA_HEAD_DIM, tm), lambda i: (0, 0, 0, jnp.maximum(i - 1, 0))),
                     pl.BlockSpec((DA_HEADS, 1, tm, 2 * DA_HEAD_DIM), k_block),
                     pl.BlockSpec((DA_HEADS, 1, LANES, tm), v_block)],
        out_shape=[jax.ShapeDtypeStruct((t, w), dt) for w, dt in flat]
                  + [jax.ShapeDtypeStruct((DA_HEADS, 2, 2 * DA_HEAD_DIM, t), BF16),
                     jax.ShapeDtypeStruct((DA_HEADS, t // tk, tk, 2 * DA_HEAD_DIM), BF16),
                     jax.ShapeDtypeStruct((DA_HEADS, t // tk, LANES, tk), BF16)],
        scratch_shapes=[pltpu.VMEM((tm, 3 * DA_WIDTH), F32)] * 2,
        compiler_params=_cparams("arbitrary"),
        name="in_proj",
    )(xs, g1, modl, w_r, gq, gk, gmat, cos_t, sa_t, sb_t)


def _softplus(x):
    return jnp.maximum(x, 0.0) + jnp.log1p(jnp.exp(-jnp.abs(x)))


def _ssd_block(c, n_lat, n_blocks, reverse, xc_ref, xp_ref, xn_ref, dtc_ref, dtr_ref, cw_ref, cb_ref,
               bias_c_ref, a_c_ref, bias_r_ref, a_r_ref, st_ref):
    L = SSD_CHUNK
    rows = xc_ref.shape[0]
    x = xc_ref[...]
    seg_first = jnp.logical_or(c == 0, c == n_lat)
    seg_last = jnp.logical_or(c == n_lat - 1, c == n_blocks - 1)
    prow = jnp.where(seg_first, 0.0, xp_ref[7:8, :])
    nrow = jnp.where(seg_last, 0.0, xn_ref[0:1, :])
    ridx = lax.broadcasted_iota(jnp.int32, (rows, 1), 0)
    xprev = jnp.where(ridx == 0, prow, pltpu.roll(x, 1, 0))
    xnext = jnp.where(ridx == rows - 1, nrow, pltpu.roll(x, rows - 1, 0))
    cw = cw_ref[...]
    conv = xprev * cw[0:1] + x * cw[1:2] + xnext * cw[2:3] + cb_ref[...]
    xbc = _silu(conv)

    dt_c = _softplus(dtc_ref[...] + bias_c_ref[...])
    dt_r = _softplus(dtr_ref[...] + bias_r_ref[...])
    adt_c = dt_c * a_c_ref[...]
    adt_r = dt_r * a_r_ref[...]

    li = lax.broadcasted_iota(jnp.int32, (L, L), 0)
    si = lax.broadcasted_iota(jnp.int32, (L, L), 1)
    mask = (li <= si) if reverse else (li >= si)
    mask_t = (li >= si) if reverse else (li <= si)
    n_sub = rows // L
    yield
    parts = yield from _interleave(*[
        _ssd_chunk_local(xbc[s * L:(s + 1) * L], dt_c[s * L:(s + 1) * L], adt_c[s * L:(s + 1) * L],
                         adt_r[:, s * L:(s + 1) * L], mask, mask_t, reverse) for s in range(n_sub)])
    ys = [None] * n_sub
    for s in (reversed(range(n_sub)) if reverse else range(n_sub)):
        ys[s] = _ssd_chunk_state(*parts[s], st_ref, reverse)
        yield
    return jnp.concatenate(ys, axis=0), xbc[:, :SSD_INNER]


def _split3(x):
    hi = x.astype(BF16)
    r1 = x - hi.astype(F32)
    mid = r1.astype(BF16)
    lo = (r1 - mid.astype(F32)).astype(BF16)
    return hi, mid, lo


def _ssd_chunk_local(xbc, dt_c, adt_c, adt_r, mask, mask_t, reverse):
    L = SSD_CHUNK
    hpg = SSD_HEADS // SSD_GROUPS
    off = SSD_HEADS if reverse else 0
    m_b, mt_b = mask.astype(BF16), mask_t.astype(BF16)
    cs_c = sum(_dot(m_b, part) for part in _split3(adt_c))
    cs_r = sum(_dot(part, mt_b) for part in _split3(adt_r))
    edge = 0 if reverse else L - 1
    tot = cs_c[edge:edge + 1, :]
    yield

    kk = lax.broadcasted_iota(jnp.int32, (LANES, SSD_INNER), 0)
    jj = lax.broadcasted_iota(jnp.int32, (LANES, SSD_INNER), 1)
    spread = (kk == off + jj // SSD_HEAD_DIM).astype(BF16)
    cols = jnp.concatenate([dt_c, jnp.exp(cs_c), jnp.exp(tot - cs_c)], axis=0)
    c_hi = cols.astype(BF16)
    c_lo = (cols - c_hi.astype(F32)).astype(BF16)
    wide = _dot(c_hi, spread) + _dot(c_lo, spread)
    dt_w, ecs_w, dst_w = wide[0:L], wide[L:2 * L], wide[2 * L:3 * L]
    yield

    xs = xbc[:, :SSD_INNER]
    b_all = xbc[:, SSD_INNER:SSD_INNER + LANES]
    c_all = xbc[:, SSD_INNER + LANES:SSD_INNER + 2 * LANES]
    xd = xs * dt_w
    xd_b = xd.astype(BF16)
    b_b = b_all.astype(BF16)
    c_b = c_all.astype(BF16)
    lane = lax.broadcasted_iota(jnp.int32, (L, LANES), 1)
    col = lax.broadcasted_iota(jnp.int32, (L, SSD_INNER), 1)

    y = None
    for g in range(SSD_GROUPS):
        c_g = jnp.where(lane // SSD_STATE == g, c_b, jnp.zeros_like(c_b))
        gmat = _dot_nt(c_g, b_b)
        for hh in range(hpg):
            h = g * hpg + hh
            dec = jnp.exp(jnp.where(mask, cs_c[:, off + h:off + h + 1] - cs_r[off + h:off + h + 1, :], -jnp.inf))
            xd_h = jnp.where(col // SSD_HEAD_DIM == h, xd_b, jnp.zeros_like(xd_b))
            part = _dot((gmat * dec).astype(BF16), xd_h)
            y = part if y is None else y + part
            yield
    return y, c_b, ecs_w, b_all.T.astype(BF16), (xd * dst_w).astype(BF16)


def _ssd_chunk_state(y_local, c_b, ecs_w, bt_b, xds_b, st_ref, reverse):
    edge = 0 if reverse else SSD_CHUNK - 1
    y = y_local + ecs_w * _dot(c_b, st_ref[...].astype(BF16))
    kr = lax.broadcasted_iota(jnp.int32, (LANES, SSD_INNER), 0)
    kc = lax.broadcasted_iota(jnp.int32, (LANES, SSD_INNER), 1)
    block = kr // SSD_STATE == kc // (SSD_HEADS // SSD_GROUPS * SSD_HEAD_DIM)
    st_ref[...] = st_ref[...] * ecs_w[edge:edge + 1, :] + jnp.where(block, _dot(bt_b, xds_b), 0.0)
    return y


def _ssd_body(n_lat, n_chunks, *refs):
    fwd_in, bwd_in = refs[0:5], refs[5:10]
    cw_ref, cb_ref, bias_c_ref, a_c_ref, bias_r_ref, a_r_ref, dsk_ref, yf_ref, yb_ref, st_ref = refs[10:]
    i = pl.program_id(0)

    @pl.when(i == 0)
    def _():
        st_ref[...] = jnp.zeros_like(st_ref)

    shared = (cw_ref, cb_ref, bias_c_ref, a_c_ref, bias_r_ref, a_r_ref)
    (y_f, xs), (y_b, _) = _run(_interleave(
        _ssd_block(_segment_order(i, n_lat, n_chunks, False), n_lat, n_chunks, False, *fwd_in, *shared,
                   st_ref.at[0]),
        _ssd_block(_segment_order(i, n_lat, n_chunks, True), n_lat, n_chunks, True, *bwd_in, *shared,
                   st_ref.at[1])))
    yf_ref[...] = y_f + dsk_ref[...] * xs
    yb_ref[...] = y_b


def _ssd_call(xbc_raw, dt_c, dt_r, cw, cb, bias_c, a_c, bias_r, a_r, dsk, n_lat_rows):
    t = xbc_raw.shape[0]
    L = SSD_STEP_CHUNKS * SSD_CHUNK
    assert t % L == 0 and n_lat_rows % L == 0
    n_chunks = t // L
    n_lat = n_lat_rows // L
    sub = L // 8
    n_sub = t // 8
    const = lambda i: (0, 0)

    def chunk_specs(reverse):
        cidx = functools.partial(_segment_order, n_lat=n_lat, n_all=n_chunks, reverse=reverse)
        return [pl.BlockSpec((L, SSD_CONV_DIM), lambda i: (cidx(i), 0)),
                pl.BlockSpec((8, SSD_CONV_DIM), lambda i: (jnp.maximum(cidx(i) * sub - 1, 0), 0)),
                pl.BlockSpec((8, SSD_CONV_DIM), lambda i: (jnp.minimum((cidx(i) + 1) * sub, n_sub - 1), 0)),
                pl.BlockSpec((L, LANES), lambda i: (cidx(i), 0)),
                pl.BlockSpec((8, L), lambda i: (0, cidx(i)))]

    def out_spec(reverse):
        cidx = functools.partial(_segment_order, n_lat=n_lat, n_all=n_chunks, reverse=reverse)
        return pl.BlockSpec((L, SSD_INNER), lambda i: (cidx(i), 0))

    chunk_args = (xbc_raw, xbc_raw, xbc_raw, dt_c, dt_r)
    return pl.pallas_call(
        functools.partial(_ssd_body, n_lat, n_chunks),
        grid=(n_chunks,),
        in_specs=chunk_specs(False) + chunk_specs(True)
                 + [pl.BlockSpec((8, SSD_CONV_DIM), const),
                    pl.BlockSpec((1, SSD_CONV_DIM), const),
                    pl.BlockSpec((1, LANES), const),
                    pl.BlockSpec((1, LANES), const),
                    pl.BlockSpec((8, L), const),
                    pl.BlockSpec((8, L), const),
                    pl.BlockSpec((1, SSD_INNER), const)],
        out_specs=[out_spec(False), out_spec(True)],
        out_shape=[jax.ShapeDtypeStruct((t, SSD_INNER), F32)] * 2,
        scratch_shapes=[pltpu.VMEM((2, SSD_GROUPS * SSD_STATE, SSD_INNER), F32)],
        compiler_params=_cparams("arbitrary"),
        name="ssd_scan",
    )(*chunk_args, *chunk_args, cw, cb, bias_c, a_c, bias_r, a_r, dsk)


def _cmul(ar, ai, br, bi):
    return ar * br - ai * bi, ar * bi + ai * br


def _s5_step_pair(u_ref, a):
    return jnp.concatenate([u_ref[:, 2 * a, :], u_ref[:, 2 * a + 1, :]], axis=1).astype(BF16)


def _s5_in_body(u_ref, w_ref, lam_ref, efr_ref, efi_ref, ebr_ref, ebi_ref, wp_ref):
    hs = S5_HALF_STATE

    @pl.when(pl.program_id(1) == 0)
    def _():
        lam = lam_ref[0]
        wc = w_ref[0]
        rows = lambda a: jnp.broadcast_to(a, (LANES, hs))
        fr, fi, br, bi = rows(lam[0:1]), rows(lam[1:2]), rows(lam[2:3]), rows(lam[3:4])
        rgrp = lax.broadcasted_iota(jnp.int32, (LANES, LANES), 0) // S5_GROUP
        lgrp = lax.broadcasted_iota(jnp.int32, (LANES, LANES), 1) // S5_STATE
        planes = []
        for q in range(4):
            cq = wc[:, q * S5_STATE:(q + 1) * S5_STATE]
            cq2 = jnp.concatenate([cq, cq], axis=1)
            planes.append(jnp.concatenate(
                [jnp.where(rgrp == 2 * g4 + lgrp, cq2, 0.0) for g4 in range(S5_HALF_GROUPS // 2)], axis=1))
        ar, ai = planes[0], planes[1]
        for i in reversed(range(S5_CHUNK)):
            wp_ref[i * LANES:(i + 1) * LANES, 0:hs] = ar.astype(BF16)
            wp_ref[i * LANES:(i + 1) * LANES, hs:2 * hs] = ai.astype(BF16)
            ar, ai = _cmul(ar, ai, fr, fi)
        ar, ai = planes[2], planes[3]
        for i in range(S5_CHUNK):
            wp_ref[i * LANES:(i + 1) * LANES, 2 * hs:3 * hs] = ar.astype(BF16)
            wp_ref[i * LANES:(i + 1) * LANES, 3 * hs:4 * hs] = ai.astype(BF16)
            ar, ai = _cmul(ar, ai, br, bi)

    e = None
    for a in range(S5_CHUNK // 2):
        part = _dot(_s5_step_pair(u_ref, a), wp_ref[2 * a * LANES:2 * (a + 1) * LANES, :])
        e = part if e is None else e + part
    efr_ref[...] = e[:, 0:hs]
    efi_ref[...] = e[:, hs:2 * hs]
    ebr_ref[...] = e[:, 2 * hs:3 * hs]
    ebi_ref[...] = e[:, 3 * hs:4 * hs]


def _s5_in_call(u3, w_b, lam_rows):
    nch = u3.shape[0]
    tn = _pick_tile(nch, (208, 80, 40))
    hs = S5_HALF_STATE
    col = lambda h, n: (n, h)
    return pl.pallas_call(
        _s5_in_body,
        grid=(2, nch // tn),
        in_specs=[pl.BlockSpec((tn, S5_CHUNK, LANES), lambda h, n: (n, 0, h)),
                  pl.BlockSpec((1, LANES, 4 * S5_STATE), lambda h, n: (h, 0, 0)),
                  pl.BlockSpec((1, 4, hs), lambda h, n: (h, 0, 0))],
        out_specs=[pl.BlockSpec((tn, hs), col)] * 4,
        out_shape=[jax.ShapeDtypeStruct((nch, 2 * hs), F32)] * 4,
        scratch_shapes=[pltpu.VMEM((S5_CHUNK * LANES, 4 * hs), BF16)],
        compiler_params=_cparams("arbitrary", "arbitrary"),
        name="s5_chunk_in",
    )(u3, w_b, lam_rows)


def _s5_scan_body(n_lat_blocks, rows, ctx_rows, a_ref, efr_ref, efi_ref, ebr_ref, ebi_ref,
                  hfr_ref, hfi_ref, hbr_ref, hbi_ref, st_ref):
    i = pl.program_id(0)

    @pl.when(i == 0)
    def _():
        st_ref[...] = jnp.zeros_like(st_ref)

    n_blocks = n_lat_blocks + 1
    live_f = jnp.where(_segment_order(i, n_lat_blocks, n_blocks, False) == n_lat_blocks, ctx_rows, rows)
    live_b = jnp.where(_segment_order(i, n_lat_blocks, n_blocks, True) == n_lat_blocks, ctx_rows, rows)
    row = lambda ref, j: ref[j:j + 1, :]
    afr, afi, abr, abi = row(a_ref, 0), row(a_ref, 1), row(a_ref, 2), row(a_ref, 3)
    fr, fi, br, bi = row(st_ref, 0), row(st_ref, 1), row(st_ref, 2), row(st_ref, 3)
    for j in range(rows):
        hfr_ref[j:j + 1, :] = fr
        hfi_ref[j:j + 1, :] = fi
        nr, ni = afr * fr - afi * fi + row(efr_ref, j), afr * fi + afi * fr + row(efi_ref, j)
        fr, fi = jnp.where(j < live_f, nr, fr), jnp.where(j < live_f, ni, fi)
        jb = rows - 1 - j
        hbr_ref[jb:jb + 1, :] = br
        hbi_ref[jb:jb + 1, :] = bi
        nr, ni = abr * br - abi * bi + row(ebr_ref, jb), abr * bi + abi * br + row(ebi_ref, jb)
        br, bi = jnp.where(jb < live_b, nr, br), jnp.where(jb < live_b, ni, bi)
    st_ref[0:1, :] = fr
    st_ref[1:2, :] = fi
    st_ref[2:3, :] = br
    st_ref[3:4, :] = bi


def _s5_scan_call(a_pow, efr, efi, ebr, ebi, n_lat_rows):
    nch, width = efr.shape
    rows = _pick_tile(n_lat_rows, (S5_SCAN_ROWS, 32, 16))
    ctx_rows = nch - n_lat_rows
    assert ctx_rows <= rows
    n_lat_blocks = n_lat_rows // rows
    n_tiles = n_lat_blocks + 1
    blk = (rows, width)
    fwd = lambda i: (_segment_order(i, n_lat_blocks, n_tiles, False), 0)
    bwd = lambda i: (_segment_order(i, n_lat_blocks, n_tiles, True), 0)
    shp = jax.ShapeDtypeStruct((nch, width), F32)
    return pl.pallas_call(
        functools.partial(_s5_scan_body, n_lat_blocks, rows, ctx_rows),
        grid=(n_tiles,),
        in_specs=[pl.BlockSpec((4, width), lambda i: (0, 0)),
                  pl.BlockSpec(blk, fwd), pl.BlockSpec(blk, fwd),
                  pl.BlockSpec(blk, bwd), pl.BlockSpec(blk, bwd)],
        out_specs=[pl.BlockSpec(blk, fwd), pl.BlockSpec(blk, fwd),
                   pl.BlockSpec(blk, bwd), pl.BlockSpec(blk, bwd)],
        out_shape=[shp] * 4,
        scratch_shapes=[pltpu.VMEM((4, width), F32)],
        compiler_params=_cparams("arbitrary"),
        name="s5_state_scan",
    )(a_pow, efr, efi, ebr, ebi)


def _s5_out_body(u_ref, hfr_ref, hfi_ref, hbr_ref, hbi_ref, k_ref, t_ref, c_ref, lam_ref, y_ref, wq_ref, z_ref):
    hs = S5_HALF_STATE
    width = S5_CHUNK * LANES

    @pl.when(pl.program_id(1) == 0)
    def _():
        strip = _dot(k_ref[0].astype(BF16), t_ref[...])
        rgrp = lax.broadcasted_iota(jnp.int32, strip.shape, 0) // S5_GROUP
        cgrp = (lax.broadcasted_iota(jnp.int32, strip.shape, 1) % LANES) // S5_GROUP
        strip = jnp.where(rgrp == cgrp, strip, 0.0).astype(BF16)
        z_ref[0:LANES, :] = strip[:, LANES:]
        z_ref[LANES:, :] = strip[:, :-LANES]

        cc = c_ref[0]
        ogrp = lax.broadcasted_iota(jnp.int32, (S5_STATE, LANES), 1) // S5_GROUP

        def expand(q):
            cq = cc[q * S5_STATE:(q + 1) * S5_STATE]
            return jnp.concatenate([jnp.where(ogrp == g, cq, 0.0) for g in range(S5_HALF_GROUPS)], axis=0)

        lam = lam_ref[0]
        for d, order in ((0, range(S5_CHUNK)), (1, reversed(range(S5_CHUNK)))):
            lr, li = lam[2 * d], lam[2 * d + 1]
            a, b = expand(2 * d), expand(2 * d + 1)
            for j in order:
                a, b = a * lr + b * li, b * lr - a * li
                wq_ref[2 * d * hs:(2 * d + 1) * hs, j * LANES:(j + 1) * LANES] = a.astype(BF16)
                wq_ref[(2 * d + 1) * hs:(2 * d + 2) * hs, j * LANES:(j + 1) * LANES] = b.astype(BF16)

    h_all = jnp.concatenate([hfr_ref[...], hfi_ref[...], hbr_ref[...], hbi_ref[...]], axis=1).astype(BF16)
    y = _dot(h_all, wq_ref[...])
    for a in range(S5_CHUNK // 2):
        lo = (S5_CHUNK - 2 - 2 * a) * LANES
        y += _dot(_s5_step_pair(u_ref, a), z_ref[:, lo:lo + width])
    for j in range(S5_CHUNK):
        y_ref[:, j, :] = y[:, j * LANES:(j + 1) * LANES]


def _s5_out_call(u3, hfr, hfi, hbr, hbi, w_k, w_t, w_c, lam_cols):
    nch = u3.shape[0]
    tn = _pick_tile(nch, (208, 80, 40))
    hs = S5_HALF_STATE
    col = lambda h, n: (n, h)
    blk3 = pl.BlockSpec((tn, S5_CHUNK, LANES), lambda h, n: (n, 0, h))
    return pl.pallas_call(
        _s5_out_body,
        grid=(2, nch // tn),
        in_specs=[blk3] + [pl.BlockSpec((tn, hs), col)] * 4
                 + [pl.BlockSpec((1, LANES) + w_k.shape[2:], lambda h, n: (h, 0, 0)),
                    pl.BlockSpec(w_t.shape, lambda h, n: (0, 0)),
                    pl.BlockSpec((1, 4 * S5_STATE, LANES), lambda h, n: (h, 0, 0)),
                    pl.BlockSpec((1, 4, hs, LANES), lambda h, n: (h, 0, 0, 0))],
        out_specs=blk3,
        out_shape=jax.ShapeDtypeStruct(u3.shape, F32),
        scratch_shapes=[pltpu.VMEM((4 * hs, S5_CHUNK * LANES), BF16),
                        pltpu.VMEM((2 * LANES, (S5_TOEPLITZ_BLOCKS - 1) * LANES), BF16)],
        compiler_params=_cparams("arbitrary", "arbitrary"),
        name="s5_chunk_out",
    )(u3, hfr, hfi, hbr, hbi, w_k, w_t, w_c, lam_cols)


def _s5_weights(lam_re, lam_im, log_step, b_re, b_im, c_re, c_im):
    tc = S5_CHUNK
    nl = lam_re.shape[0]
    hg = S5_HALF_GROUPS
    ein = functools.partial(jnp.einsum, precision=HI)
    step = jnp.exp(log_step)[..., None]
    er, ei = lam_re * step, lam_im * step
    mag = jnp.exp(er)
    lbr, lbi = mag * jnp.cos(ei), mag * jnp.sin(ei)
    den = lam_re * lam_re + lam_im * lam_im
    nr, ni = lbr - 1.0, lbi
    cr, ci = (nr * lam_re + ni * lam_im) / den, (ni * lam_re - nr * lam_im) / den
    bbr = cr[..., None] * b_re - ci[..., None] * b_im
    bbi = cr[..., None] * b_im + ci[..., None] * b_re
    taus = jnp.arange(tc + 1, dtype=F32)[:, None, None]
    pmag = jnp.exp(er[:, :, None] * taus)
    pr, pi = pmag * jnp.cos(ei[:, :, None] * taus), pmag * jnp.sin(ei[:, :, None] * taus)

    wr = pr[:, :, :tc, :, :, None] * bbr[:, :, None] - pi[:, :, :tc, :, :, None] * bbi[:, :, None]
    wi = pr[:, :, :tc, :, :, None] * bbi[:, :, None] + pi[:, :, :tc, :, :, None] * bbr[:, :, None]
    kk = ein('ldgcp,ldtgpk->ldtgck', c_re, wr) - ein('ldgcp,ldtgpk->ldtgck', c_im, wi)
    kf, kb = kk[:, 0], kk[:, 1]
    seq = jnp.concatenate([kb[:, :0:-1], kf[:, :1] + kb[:, :1], kf[:, 1:]], axis=1)
    seq = seq.reshape(nl, S5_TOEPLITZ_BLOCKS, 2, hg, S5_GROUP, S5_GROUP)
    w_k = seq.transpose(0, 2, 3, 5, 1, 4).reshape(nl, 2, LANES, S5_TOEPLITZ_BLOCKS * S5_GROUP)
    w_k = jnp.pad(w_k, ((0, 0), (0, 0), (0, 0), (0, S5_KCOLS - S5_TOEPLITZ_BLOCKS * S5_GROUP)))
    r = jnp.arange(S5_KCOLS)
    s = jnp.arange(S5_TOEPLITZ_BLOCKS * LANES)
    w_t = ((r[:, None] // S5_GROUP == s[None, :] // LANES)
           & (r[:, None] % S5_GROUP == s[None, :] % S5_GROUP)).astype(BF16)

    planes_b = jnp.stack([bbr[:, 0], bbi[:, 0], bbr[:, 1], bbi[:, 1]], axis=1)
    planes_b = planes_b.reshape(nl, 4, 2, hg, S5_STATE, S5_GROUP)
    w_b = planes_b.transpose(0, 2, 3, 5, 1, 4).reshape(nl, 2, LANES, 4 * S5_STATE)
    planes_c = jnp.stack([c_re[:, 0], -c_im[:, 0], c_re[:, 1], -c_im[:, 1]], axis=1)
    planes_c = planes_c.reshape(nl, 4, 2, hg, S5_GROUP, S5_STATE)
    w_c = planes_c.transpose(0, 2, 1, 5, 3, 4).reshape(nl, 2, 4 * S5_STATE, LANES)

    halves = lambda a: a.reshape(nl, 2, S5_HALF_STATE)
    lam_rows = jnp.stack([halves(lbr[:, 0]), halves(lbi[:, 0]), halves(lbr[:, 1]), halves(lbi[:, 1])], axis=2)
    lam_cols = jnp.broadcast_to(lam_rows[..., None], lam_rows.shape + (LANES,))
    full = lambda a: a.reshape(nl, S5_GROUPS * S5_STATE)
    a_pow = jnp.stack([full(pr[:, 0, tc]), full(pi[:, 0, tc]), full(pr[:, 1, tc]), full(pi[:, 1, tc])], axis=1)
    return (w_b, w_k, w_c, lam_rows, lam_cols, a_pow), w_t


def _s5_mix(u, w_b, w_k, w_c, lam_rows, lam_cols, a_pow, w_t, n_lat):
    t = u.shape[0]
    nch = t // S5_CHUNK
    u3 = u.reshape(nch, S5_CHUNK, S5_WIDTH)
    es = _s5_in_call(u3, w_b, lam_rows)
    hs = _s5_scan_call(a_pow, *es, n_lat // S5_CHUNK)
    return _s5_out_call(u3, *hs, w_k, w_t, w_c, lam_cols).reshape(t, S5_WIDTH)


def _attn_body(flag_ref, lam_ref, q_ref, k_ref, v_ref, g_ref, o_ref, acc_ref, m_ref, oh_ref, *, n_kv, post_scale):
    bounded = flag_ref[0] == 1

    def head(hh, carry):
        acc_ref[...] = jnp.zeros_like(acc_ref)

        @pl.when(bounded)
        def _():
            def step(b, carry2):
                k = k_ref[hh, b]
                vt = v_ref[hh, b]
                for c in range(2):
                    p = jnp.exp2(_dot(k, q_ref[hh, c])).astype(BF16)
                    acc_ref[c] += _dot(vt, p)
                return carry2
            lax.fori_loop(0, n_kv, step, 0, unroll=2)

        @pl.when(jnp.logical_not(bounded))
        def _():
            m_ref[...] = jnp.full_like(m_ref, -jnp.inf)

            def step(b, carry2):
                k = k_ref[hh, b]
                vt = v_ref[hh, b]
                for c in range(2):
                    s = _dot(k, q_ref[hh, c])
                    m_old = m_ref[c]
                    m_new = jnp.maximum(m_old, jnp.max(s, axis=0, keepdims=True))
                    p = jnp.exp2(s - m_new[0:1]).astype(BF16)
                    acc_ref[c] = jnp.exp2(m_old - m_new)[0:1] * acc_ref[c] + _dot(vt, p)
                    m_ref[c] = m_new
                return carry2
            lax.fori_loop(0, n_kv, step, 0)

        a0 = acc_ref[0]
        a1 = acc_ref[1]
        o = a0 / a0[DA_V_DIM:DA_V_DIM + 1] - lam_ref[0] * (a1 / a1[DA_V_DIM:DA_V_DIM + 1])
        vrow = lax.broadcasted_iota(jnp.int32, o.shape, 0) < DA_V_DIM
        o = jnp.where(vrow, o, 0.0)
        ms = jnp.sum(o * o, axis=0, keepdims=True) * (1.0 / DA_V_DIM)
        ot = (o * lax.rsqrt(ms + EPS)).T
        oh_ref[hh] = ot[:, :DA_V_DIM] * g_ref[...] * post_scale
        return carry

    lax.fori_loop(0, ATTN_HEADS_PER_STEP, head, 0)
    o_ref[...] = jnp.concatenate([oh_ref[hh] for hh in range(ATTN_HEADS_PER_STEP)], axis=1)


def _attn_body_keep_rest(flag_ref, lam_ref, q_ref, k_ref, v_ref, g_ref, rest_ref, *refs, **kw):
    del rest_ref
    _attn_body(flag_ref, lam_ref, q_ref, k_ref, v_ref, g_ref, *refs, **kw)


def _attn_call(flag, lam, qt, kb, vtb, g_sub, post_scale, q_rows, q_row0, kv_block, kv_cols, kv_col0, y_rest=None):
    nh = qt.shape[0]
    hp = ATTN_HEADS_PER_STEP
    tk = kb.shape[2]
    tq = _pick_tile(q_rows, (ATTN_TQ, 256))
    assert q_row0 % tq == 0
    q0 = q_row0 // tq
    if kv_block is None:
        n_kv = kb.shape[1]
        one = pl.Buffered(1)
        k_spec = pl.BlockSpec((hp, n_kv, tk, 2 * DA_HEAD_DIM), lambda h, i: (h, 0, 0, 0), pipeline_mode=one)
        v_spec = pl.BlockSpec((hp, n_kv, LANES, tk), lambda h, i: (h, 0, 0, 0), pipeline_mode=one)
    else:
        assert kv_col0 % kv_cols == 0 and tk % kv_cols == 0
        n_kv, cb = 1, kv_col0 // kv_cols
        k_spec = pl.BlockSpec((hp, 1, kv_cols, 2 * DA_HEAD_DIM), lambda h, i: (h, kv_block, cb, 0))
        v_spec = pl.BlockSpec((hp, 1, LANES, kv_cols), lambda h, i: (h, kv_block, 0, cb))
    smem = pl.BlockSpec(memory_space=pltpu.SMEM)
    in_specs = [smem, smem,
                pl.BlockSpec((hp, 2, 2 * DA_HEAD_DIM, tq), lambda h, i: (h, 0, 0, q0 + i)),
                k_spec, v_spec,
                pl.BlockSpec((1, DA_V_DIM), lambda h, i: (0, 0))]
    args = (flag, lam, qt, kb, vtb, g_sub)
    body = functools.partial(_attn_body, n_kv=n_kv, post_scale=post_scale)
    aliases = {}
    if y_rest is not None:
        in_specs.append(pl.BlockSpec(memory_space=pl.ANY))
        args += (y_rest,)
        body = functools.partial(_attn_body_keep_rest, n_kv=n_kv, post_scale=post_scale)
        aliases = {len(args) - 1: 0}
    return pl.pallas_call(
        body,
        grid=(nh // hp, q_rows // tq),
        in_specs=in_specs,
        out_specs=pl.BlockSpec((tq, hp * DA_V_DIM), lambda h, i: (q0 + i, h)),
        out_shape=jax.ShapeDtypeStruct((qt.shape[3], nh * DA_V_DIM), F32),
        scratch_shapes=[pltpu.VMEM((2, LANES, tq), F32), pltpu.VMEM((2, 8, tq), F32),
                        pltpu.VMEM((hp, tq, DA_V_DIM), F32)],
        input_output_aliases=aliases,
        compiler_params=_cparams("arbitrary", "arbitrary"),
        name="diff_attn",
    )(*args)


def _gelu_tanh(x):
    return 0.5 * x * (1.0 + jnp.tanh(math.sqrt(2.0 / math.pi) * (x + 0.044715 * (x * x * x))))


def _outffn_body(x_ref, yf_ref, yb_ref, z_ref, gssd_ref, ys5_ref, u_ref, d5_ref, gw_ref, gb_ref,
                 yda_ref, mod_ref, modp_ref, g2_ref, wo_ref, w1_ref, w3_ref, w2_ref, o_ref, xm_ref, hb_ref):
    i = pl.program_id(0)
    cur = i % 2
    prev = 1 - cur

    @pl.when(i == 0)
    def _():
        xm_ref[...] = jnp.zeros_like(xm_ref)
        hb_ref[...] = jnp.zeros_like(hb_ref)

    def finish():
        hp = hb_ref[prev]
        a1 = _dot(hp, w1_ref[...])
        yield
        a3 = _dot(hp, w3_ref[...])
        yield
        f = (_silu(a1) * a3).astype(BF16)
        yield
        o_ref[...] = xm_ref[prev] + modp_ref[0][5:6] * _dot(f, w2_ref[...])

    def prepare():
        mod = mod_ref[0]
        y = (yf_ref[...] + yb_ref[...]) * _silu(z_ref[...])
        y_ssd = y * lax.rsqrt(jnp.mean(y * y, axis=-1, keepdims=True) + EPS) * gssd_ref[...]
        yield
        y5 = _gelu_tanh(ys5_ref[...] + d5_ref[...] * u_ref[...])
        yield
        y5 = y5 * jax.nn.sigmoid(_dot(y5.astype(BF16), gw_ref[...].astype(BF16)) + gb_ref[...])
        yield
        o = _dot(y_ssd.astype(BF16), wo_ref[0:256, :])
        yield
        o += _dot(y5.astype(BF16), wo_ref[256:512, :])
        yield
        o += _dot(yda_ref[...].astype(BF16), wo_ref[512:1024, :])
        yield
        x = x_ref[...] + mod[2:3] * o
        xm_ref[cur] = x
        yield
        h = x * lax.rsqrt(jnp.mean(x * x, axis=-1, keepdims=True) + EPS) * g2_ref[...] * (1.0 + mod[4:5]) \
            + mod[3:4]
        hb_ref[cur] = h.astype(BF16)

    _run(_interleave(finish(), prepare()))


def _outffn_call(xs, y_f, y_b, z, g_ssd, y_s5, u, d5, glu_w, glu_b, y_da, modl, g2, w_o, w1, w3, w2, layer, n_lat,
                 latent_only):
    t, d = xs.shape
    if latent_only:
        t = n_lat
    dff = w1.shape[2]
    layer_block = lambda i: (layer, 0, 0)
    tm = FFN_ROW_TILE
    assert n_lat % tm == 0 and t % tm == 0
    n_lat_tiles = n_lat // tm
    n_tiles = t // tm
    row = lambda i: (jnp.minimum(i, n_tiles - 1), 0)
    done = lambda i: (jnp.maximum(i - 1, 0), 0)
    const = lambda i: (0, 0)
    seg = lambda i: ((jnp.minimum(i, n_tiles - 1) >= n_lat_tiles).astype(jnp.int32), 0, 0)
    seg_done = lambda i: ((i - 1 >= n_lat_tiles).astype(jnp.int32), 0, 0)
    one = pl.Buffered(1)
    return pl.pallas_call(
        _outffn_body,
        grid=(n_tiles + 1,),
        in_specs=[pl.BlockSpec((tm, d), row),
                  pl.BlockSpec((tm, 256), row), pl.BlockSpec((tm, 256), row), pl.BlockSpec((tm, 256), row),
                  pl.BlockSpec((1, 256), const),
                  pl.BlockSpec((tm, 256), row), pl.BlockSpec((tm, 256), row),
                  pl.BlockSpec((1, 256), const),
                  pl.BlockSpec((256, 256), const), pl.BlockSpec((1, 256), const),
                  pl.BlockSpec((tm, DA_WIDTH), row),
                  pl.BlockSpec((1, 8, d), seg),
                  pl.BlockSpec((1, 8, d), seg_done),
                  pl.BlockSpec((1, d), const),
                  pl.BlockSpec((None, d, d), layer_block, pipeline_mode=one),
                  pl.BlockSpec((None, d, dff), layer_block, pipeline_mode=one),
                  pl.BlockSpec((None, d, dff), layer_block, pipeline_mode=one),
                  pl.BlockSpec((None, dff, d), layer_block, pipeline_mode=one)],
        out_specs=pl.BlockSpec((tm, d), done),
        out_shape=jax.ShapeDtypeStruct((t, d), F32),
        scratch_shapes=[pltpu.VMEM((2, tm, d), F32), pltpu.VMEM((2, tm, d), BF16)],
        compiler_params=_cparams("arbitrary"),
        name="out_proj_ffn",
    )(xs, y_f, y_b, z, g_ssd, y_s5, u, d5, glu_w, glu_b, y_da, modl, modl, g2, w_o, w1, w3, w2)


def _cast_body(w_ref, o_ref):
    o_ref[...] = w_ref[...].astype(o_ref.dtype)


def _to_bf16(w):
    nl, rows, cols = w.shape
    tr = _pick_tile(rows, (512, 704, 256))
    blk = pl.BlockSpec((1, tr, cols), lambda l, r: (l, r, 0))
    return pl.pallas_call(
        _cast_body,
        grid=(nl, rows // tr),
        in_specs=[blk],
        out_specs=blk,
        out_shape=jax.ShapeDtypeStruct(w.shape, BF16),
        compiler_params=_cparams("arbitrary", "arbitrary"),
        name="weights_to_bf16",
    )(w)


def _rope_tables(n_lat, n_ctx):
    pos = jnp.arange(n_lat)
    lane = jnp.arange(LANES)
    d = lane % DA_HEAD_DIM
    axis = d // (2 * ROPE_FREQS)
    half = (d % (2 * ROPE_FREQS)) // ROPE_FREQS
    freqs = ROPE_THETA ** (-jnp.arange(ROPE_FREQS, dtype=F32) / ROPE_FREQS)
    coord = jnp.where(axis[None, :] == 0, (pos // GRID_W)[:, None], (pos % GRID_W)[:, None]).astype(F32)
    ang = coord * freqs[d % ROPE_FREQS][None, :]
    cos, sin = jnp.cos(ang), jnp.sin(ang)
    sa = jnp.where(half[None, :] == 0, -sin, 0.0)
    sb = jnp.where(half[None, :] == 1, sin, 0.0)
    pad = lambda a, v: jnp.concatenate([a, jnp.full((n_ctx, LANES), v, F32)], axis=0)
    return pad(cos, 1.0), pad(sa, 0.0), pad(sb, 0.0)


def _lane_row(vals, width=LANES):
    vals = vals.reshape(-1).astype(F32)
    return jnp.zeros((1, width), F32).at[0, :vals.shape[0]].set(vals)


def kernel(x, c, ctx, c_ctx, w_mod, b_mod, norm1, norm2, w_in, w_out, ssd_conv_w, ssd_conv_b, ssd_a_log, ssd_dt_bias, ssd_d, ssd_norm, s5_lam_re, s5_lam_im, s5_log_step, s5_b_re, s5_b_im, s5_c_re, s5_c_im, s5_d, s5_glu_w, s5_glu_b, da_q_norm, da_k_norm, da_lambda, da_sub_norm, ffn_w1, ffn_w3, ffn_w2):
    depth = w_mod.shape[0]
    bsz, n_lat, d = x.shape
    n_ctx = ctx.shape[1]
    t = n_ctx + n_lat
    tk = _pick_tile(t, (ATTN_TK, 1024, 640, 512, 256))
    assert bsz == 1 and n_ctx % ROW_TILE == 0 and n_lat % ROW_TILE == 0 and tk % ROW_TILE == 0
    assert n_ctx % (S5_CHUNK * 16) == 0 and n_lat % (S5_CHUNK * 16) == 0
    assert n_lat % n_ctx == 0 and tk % n_ctx == 0
    n_lat_tiles = n_lat // ROW_TILE

    xs = jnp.concatenate([x[0], ctx[0]], axis=0)
    cv = jnp.zeros((8, d), F32).at[0].set(c[0]).at[1].set(c_ctx)
    mod_all = _mod_call(cv, w_mod, b_mod)
    cos_t, sa_t, sb_t = _rope_tables(n_lat, n_ctx)
    gidx = jnp.arange(DA_WIDTH) // DA_HEAD_DIM
    gmat = (gidx[:, None] == gidx[None, :]).astype(BF16) * (1.0 / DA_HEAD_DIM)
    s5_w, s5_spread = _s5_weights(s5_lam_re, s5_lam_im, s5_log_step, s5_b_re, s5_b_im, s5_c_re, s5_c_im)
    w_out_b, w1_b, w3_b, w2_b = _to_bf16(w_out), _to_bf16(ffn_w1), _to_bf16(ffn_w3), _to_bf16(ffn_w2)

    for i in range(depth):
        lam_init = 0.8 - 0.6 * math.exp(-0.3 * i)
        modl = jnp.zeros((2, 8, d), F32).at[:, :6].set(mod_all[i, :2].reshape(2, 6, d))

        wi = w_in[i]
        w_r = jnp.concatenate([wi[:, 0:768], wi[:, 776:1032], wi[:, 1032:2568], wi[:, 768:776],
                               jnp.zeros((d, IN_PAD - 2568), F32)], axis=1).astype(BF16)
        gq = jnp.tile(da_q_norm[i], DA_WIDTH // DA_HEAD_DIM)[None, :]
        gk = jnp.tile(da_k_norm[i], DA_WIDTH // DA_HEAD_DIM)[None, :]
        z, xbc_raw, u, dt_raw, qh, kzb, vb = _inproj_call(
            xs, norm1[i][None, :], modl, w_r, gq, gk, gmat, cos_t, sa_t, sb_t, n_lat_tiles, tk)

        dt_r = dt_raw[:, :8].T
        cw = jnp.zeros((8, SSD_CONV_DIM), F32).at[:3].set(ssd_conv_w[i])
        a_neg = -jnp.exp(ssd_a_log[i])
        bias_c = _lane_row(ssd_dt_bias[i])
        a_c = _lane_row(a_neg)
        bias_r = jnp.broadcast_to(ssd_dt_bias[i].reshape(8, 1), (8, SSD_STEP_CHUNKS * SSD_CHUNK))
        a_r = jnp.broadcast_to(a_neg.reshape(8, 1), (8, SSD_STEP_CHUNKS * SSD_CHUNK))
        dsk = jnp.repeat(ssd_d[i], SSD_HEAD_DIM)[None, :]
        y_f, y_b = _ssd_call(xbc_raw, dt_raw, dt_r, cw, ssd_conv_b[i][None, :], bias_c, a_c, bias_r, a_r, dsk,
                             n_lat)

        y_s5 = _s5_mix(u, *(w[i] for w in s5_w), s5_spread, n_lat)

        lf = da_lambda[i]
        lam = (jnp.exp(jnp.sum(lf[0] * lf[1])) - jnp.exp(jnp.sum(lf[2] * lf[3])) + lam_init).reshape(1)
        g_sub = da_sub_norm[i][None, :]
        score_bound = LOG2E * math.sqrt(DA_HEAD_DIM) * jnp.max(jnp.abs(da_q_norm[i])) * jnp.max(jnp.abs(da_k_norm[i]))
        flag = (score_bound <= SCORE_LOG2_LIMIT).astype(jnp.int32).reshape(1)
        y_da = _attn_call(flag, lam, qh, kzb, vb, g_sub, 1.0 - lam_init, n_lat, 0, None, None, None)
        last = i == depth - 1
        if not last:
            y_da = _attn_call(flag, lam, qh, kzb, vb, g_sub, 1.0 - lam_init, n_ctx, n_lat,
                              n_lat // tk, n_ctx, n_lat % tk, y_rest=y_da)

        xs = _outffn_call(xs, y_f, y_b, z, ssd_norm[i][None, :], y_s5, u, s5_d[i][None, :], s5_glu_w[i],
                          s5_glu_b[i][None, :], y_da, modl, norm2[i][None, :], w_out_b, w1_b, w3_b, w2_b, i,
                          n_lat, latent_only=last)
    return xs[None]
```

```python
import functools
import math

import jax
import jax.numpy as jnp
from jax import lax
from jax.experimental import pallas as pl
from jax.experimental.pallas import tpu as pltpu

F32 = jnp.float32
BF16 = jnp.bfloat16
HI = lax.Precision.HIGHEST

EPS = 1e-6
GRID_W = 64
ROPE_THETA = 10000.0
ROPE_FREQS = 8

SSD_HEADS = 4
SSD_HEAD_DIM = 64
SSD_INNER = 256
SSD_GROUPS = 2
SSD_STATE = 64
SSD_CHUNK = 128
SSD_STEP_CHUNKS = 2
SSD_CONV_DIM = 512

S5_WIDTH = 256
S5_GROUP = 16
S5_GROUPS = 16
S5_STATE = 64
S5_CHUNK = 16
S5_HALF_GROUPS = S5_GROUPS // 2
S5_HALF_STATE = S5_HALF_GROUPS * S5_STATE
S5_SCAN_ROWS = 64
S5_TOEPLITZ_BLOCKS = 2 * S5_CHUNK - 1
S5_KCOLS = 512

DA_HEADS = 8
DA_HEAD_DIM = 32
DA_V_DIM = 64
DA_WIDTH = 512
ATTN_HEADS_PER_STEP = 2
ATTN_TQ = 2048
ATTN_TK = 1280
LOG2E = 1.4426950408889634
SCORE_LOG2_LIMIT = 100.0

LANES = 128
ROW_TILE = 256
FFN_ROW_TILE = 256
IN_PAD = 2688
VMEM_LIMIT = 56 * 1024 * 1024


def _cparams(*sem):
    return pltpu.CompilerParams(dimension_semantics=sem, vmem_limit_bytes=VMEM_LIMIT)


def _dot(a, b, precision=None):
    return jnp.dot(a, b, preferred_element_type=F32, precision=precision)


def _dot_nt(a, b, precision=None):
    return lax.dot_general(a, b, (((1,), (1,)), ((), ())), preferred_element_type=F32,
                           precision=precision)


def _dot_tn(a, b, precision=None):
    return lax.dot_general(a, b, (((0,), (0,)), ((), ())), preferred_element_type=F32,
                           precision=precision)


def _silu(x):
    return x * jax.nn.sigmoid(x)


def _pick_tile(n, candidates):
    for c in candidates:
        if n % c == 0:
            return c
    return n


def _interleave(*stages):
    results = [None] * len(stages)
    live = dict(enumerate(stages))
    while live:
        for k in list(live):
            try:
                next(live[k])
            except StopIteration as done:
                results[k] = done.value
                del live[k]
        yield
    return results


def _run(stage):
    while True:
        try:
            next(stage)
        except StopIteration as done:
            return done.value


def _segment_order(i, n_lat, n_all, reverse):
    n_ctx = n_all - n_lat
    if not reverse:
        return jnp.where(i < n_ctx, n_lat + i, i - n_ctx)
    return jnp.where(i < n_ctx, n_all - 1 - i, n_lat - 1 - (i - n_ctx))


def _mod_body(cv_ref, w_ref, b_ref, o_ref):
    o_ref[0] = _dot(_silu(cv_ref[...]), w_ref[0], HI) + b_ref[0]


def _mod_call(cv, w_mod, b_mod):
    depth, d, n = w_mod.shape
    tn = 2048
    return pl.pallas_call(
        _mod_body,
        grid=(depth, n // tn),
        in_specs=[pl.BlockSpec((8, d), lambda l, j: (0, 0)),
                  pl.BlockSpec((1, d, tn), lambda l, j: (l, 0, j)),
                  pl.BlockSpec((1, 1, tn), lambda l, j: (l, 0, j))],
        out_specs=pl.BlockSpec((1, 8, tn), lambda l, j: (l, 0, j)),
        out_shape=jax.ShapeDtypeStruct((depth, 8, n), F32),
        compiler_params=_cparams("arbitrary", "arbitrary"),
        name="adaln_mod",
    )(cv, w_mod, b_mod.reshape(depth, 1, n))


def _inproj_project(x_ref, g_ref, mod_ref, w_ref, z_ref, xbc_ref, u_ref, dt_ref, qkv_ref):
    x = x_ref[...]
    mod = mod_ref[0]
    ms = jnp.mean(x * x, axis=-1, keepdims=True)
    h = x * lax.rsqrt(ms + EPS) * g_ref[...] * (1.0 + mod[1:2]) + mod[0:1]
    p = _dot(h.astype(BF16), w_ref[...])
    z_ref[...] = p[:, 0:256]
    xbc_ref[...] = p[:, 256:768]
    u_ref[...] = p[:, 768:1024]
    dt_ref[...] = p[:, 2560:2688]
    qkv_ref[...] = p[:, 1024:2560]


def _inproj_attn_operands(qkv_ref, gq_ref, gk_ref, gm_ref, cos_ref, sa_ref, sb_ref, q_ref, k_ref, v_ref):
    p = qkv_ref[...]
    rows = p.shape[0]
    gm = gm_ref[...]
    cos = cos_ref[...]
    sa = sa_ref[...]
    sb = sb_ref[...]

    def norm_rope(t, gain, scale):
        ms32 = _dot((t * t).astype(BF16), gm)
        tn = t * lax.rsqrt(ms32 + EPS) * gain
        outs = []
        for j in range(DA_WIDTH // LANES):
            tb = tn[:, j * LANES:(j + 1) * LANES]
            ob = tb * cos + pltpu.roll(tb, LANES - ROPE_FREQS, 1) * sa + pltpu.roll(tb, ROPE_FREQS, 1) * sb
            outs.append(ob * scale)
        return jnp.concatenate(outs, axis=1)

    qt = norm_rope(p[:, 0:512], gq_ref[...], LOG2E * DA_HEAD_DIM ** -0.5).T.astype(BF16)
    kn = norm_rope(p[:, 512:1024], gk_ref[...], 1.0).astype(BF16)
    vt = p[:, 1024:1536].T.astype(BF16)
    srow = lax.broadcasted_iota(jnp.int32, (LANES - DA_V_DIM, rows), 0)
    ones_row = jnp.where(srow == 0, 1.0, 0.0).astype(BF16)
    zeros_q = jnp.zeros((DA_HEAD_DIM, rows), BF16)
    for hd in range(DA_HEADS):
        lo_, mid, hi_ = hd * DA_V_DIM, hd * DA_V_DIM + DA_HEAD_DIM, (hd + 1) * DA_V_DIM
        k_ref[hd, 0] = kn[:, lo_:hi_]
        q_ref[hd, 0, 0:DA_HEAD_DIM, :] = qt[lo_:mid]
        q_ref[hd, 0, DA_HEAD_DIM:, :] = zeros_q
        q_ref[hd, 1, 0:DA_HEAD_DIM, :] = zeros_q
        q_ref[hd, 1, DA_HEAD_DIM:, :] = qt[mid:hi_]
        v_ref[hd, 0, 0:DA_V_DIM, :] = vt[lo_:hi_]
        v_ref[hd, 0, DA_V_DIM:, :] = ones_row


def _inproj_body(x_ref, g_ref, mod_ref, w_ref, gq_ref, gk_ref, gm_ref, cos_ref, sa_ref, sb_ref,
                 z_ref, xbc_ref, u_ref, dt_ref, q_ref, k_ref, v_ref, qkv_a_ref, qkv_b_ref):
    i = pl.program_id(0)

    @pl.when(i == 0)
    def _():
        qkv_b_ref[...] = jnp.zeros_like(qkv_b_ref)

    def step(cur_ref, prev_ref):
        _inproj_attn_operands(prev_ref, gq_ref, gk_ref, gm_ref, cos_ref, sa_ref, sb_ref, q_ref, k_ref, v_ref)
        _inproj_project(x_ref, g_ref, mod_ref, w_ref, z_ref, xbc_ref, u_ref, dt_ref, cur_ref)

    @pl.when(i % 2 == 0)
    def _():
        step(qkv_a_ref, qkv_b_ref)

    @pl.when(i % 2 == 1)
    def _():
        step(qkv_b_ref, qkv_a_ref)


def _inproj_call(xs, g1, modl, w_r, gq, gk, gmat, cos_t, sa_t, sb_t, n_lat_tiles, tk):
    t, d = xs.shape
    tm = ROW_TILE
    n_tiles = t // tm
    per = tk // tm
    row = lambda i: (jnp.minimum(i, n_tiles - 1), 0)
    done = lambda i: (jnp.maximum(i - 1, 0), 0)
    const = lambda i: (0, 0)
    seg = lambda i: ((jnp.minimum(i, n_tiles - 1) >= n_lat_tiles).astype(jnp.int32), 0, 0)
    flat = [(256, F32), (512, F32), (256, F32), (LANES, F32)]

    def k_block(i):
        j = jnp.maximum(i - 1, 0)
        return (0, j // per, j % per, 0)

    def v_block(i):
        j = jnp.maximum(i - 1, 0)
        return (0, j // per, 0, j % per)

    return pl.pallas_call(
        _inproj_body,
        grid=(n_tiles + 1,),
        in_specs=[pl.BlockSpec((tm, d), row),
                  pl.BlockSpec((1, d), const),
                  pl.BlockSpec((1, 8, d), seg),
                  pl.BlockSpec((d, IN_PAD), const),
                  pl.BlockSpec((1, DA_WIDTH), const),
                  pl.BlockSpec((1, DA_WIDTH), const),
                  pl.BlockSpec((DA_WIDTH, DA_WIDTH), const),
                  pl.BlockSpec((tm, LANES), done),
                  pl.BlockSpec((tm, LANES), done),
                  pl.BlockSpec((tm, LANES), done)],
        out_specs=[pl.BlockSpec((tm, w), row) for w, _ in flat]
                  + [pl.BlockSpec((DA_HEADS, 2, 2 * DA_HEAD_DIM, tm), lambda i: (0, 0, 0, jnp.maximum(i - 1, 0))),
                     pl.BlockSpec((DA_HEADS, 1, tm, 2 * DA_HEAD_DIM), k_block),
                     pl.BlockSpec((DA_HEADS, 1, LANES, tm), v_block)],
        out_shape=[jax.ShapeDtypeStruct((t, w), dt) for w, dt in flat]
                  + [jax.ShapeDtypeStruct((DA_HEADS, 2, 2 * DA_HEAD_DIM, t), BF16),
                     jax.ShapeDtypeStruct((DA_HEADS, t // tk, tk, 2 * DA_HEAD_DIM), BF16),
                     jax.ShapeDtypeStruct((DA_HEADS, t // tk, LANES, tk), BF16)],
        scratch_shapes=[pltpu.VMEM((tm, 3 * DA_WIDTH), F32)] * 2,
        compiler_params=_cparams("arbitrary"),
        name="in_proj",
    )(xs, g1, modl, w_r, gq, gk, gmat, cos_t, sa_t, sb_t)


def _softplus(x):
    return jnp.maximum(x, 0.0) + jnp.log1p(jnp.exp(-jnp.abs(x)))


def _ssd_conv_body(n_lat, n_blocks, xc_ref, xp_ref, xn_ref, cw_ref, cb_ref, o_ref):
    c = pl.program_id(0)
    rows = xc_ref.shape[0]
    x = xc_ref[...]
    seg_first = jnp.logical_or(c == 0, c == n_lat)
    seg_last = jnp.logical_or(c == n_lat - 1, c == n_blocks - 1)
    prow = jnp.where(seg_first, 0.0, xp_ref[7:8, :])
    nrow = jnp.where(seg_last, 0.0, xn_ref[0:1, :])
    ridx = lax.broadcasted_iota(jnp.int32, (rows, 1), 0)
    xprev = jnp.where(ridx == 0, prow, pltpu.roll(x, 1, 0))
    xnext = jnp.where(ridx == rows - 1, nrow, pltpu.roll(x, rows - 1, 0))
    cw = cw_ref[...]
    o_ref[...] = _silu(xprev * cw[0:1] + x * cw[1:2] + xnext * cw[2:3] + cb_ref[...])


def _ssd_conv_call(xbc_raw, cw, cb, n_lat_rows):
    t, width = xbc_raw.shape
    rows = SSD_STEP_CHUNKS * SSD_CHUNK
    assert t % rows == 0 and n_lat_rows % rows == 0
    n_blocks = t // rows
    sub = rows // 8
    n_sub = t // 8
    const = lambda i: (0, 0)
    return pl.pallas_call(
        functools.partial(_ssd_conv_body, n_lat_rows // rows, n_blocks),
        grid=(n_blocks,),
        in_specs=[pl.BlockSpec((rows, width), lambda i: (i, 0)),
                  pl.BlockSpec((8, width), lambda i: (jnp.maximum(i * sub - 1, 0), 0)),
                  pl.BlockSpec((8, width), lambda i: (jnp.minimum((i + 1) * sub, n_sub - 1), 0)),
                  pl.BlockSpec((8, width), const),
                  pl.BlockSpec((1, width), const)],
        out_specs=pl.BlockSpec((rows, width), lambda i: (i, 0)),
        out_shape=jax.ShapeDtypeStruct((t, width), F32),
        compiler_params=_cparams("arbitrary"),
        name="ssd_conv",
    )(xbc_raw, xbc_raw, xbc_raw, cw, cb)


def _ssd_block(reverse, xc_ref, dtc_ref, dtr_ref, bias_c_ref, a_c_ref, bias_r_ref, a_r_ref, st_ref):
    L = SSD_CHUNK
    rows = xc_ref.shape[0]
    xbc = xc_ref[...]

    dt_c = _softplus(dtc_ref[...] + bias_c_ref[...])
    dt_r = _softplus(dtr_ref[...] + bias_r_ref[...])
    adt_c = dt_c * a_c_ref[...]
    adt_r = dt_r * a_r_ref[...]

    li = lax.broadcasted_iota(jnp.int32, (L, L), 0)
    si = lax.broadcasted_iota(jnp.int32, (L, L), 1)
    mask = (li <= si) if reverse else (li >= si)
    mask_t = (li >= si) if reverse else (li <= si)
    n_sub = rows // L
    yield
    parts = yield from _interleave(*[
        _ssd_chunk_local(xbc[s * L:(s + 1) * L], dt_c[s * L:(s + 1) * L], adt_c[s * L:(s + 1) * L],
                         adt_r[:, s * L:(s + 1) * L], mask, mask_t, reverse) for s in range(n_sub)])
    ys = [None] * n_sub
    for s in (reversed(range(n_sub)) if reverse else range(n_sub)):
        ys[s] = _ssd_chunk_state(*parts[s], st_ref, reverse)
        yield
    return jnp.concatenate(ys, axis=0), xbc[:, :SSD_INNER]


def _split3(x):
    hi = x.astype(BF16)
    r1 = x - hi.astype(F32)
    mid = r1.astype(BF16)
    lo = (r1 - mid.astype(F32)).astype(BF16)
    return hi, mid, lo


def _ssd_chunk_local(xbc, dt_c, adt_c, adt_r, mask, mask_t, reverse):
    L = SSD_CHUNK
    hpg = SSD_HEADS // SSD_GROUPS
    off = SSD_HEADS if reverse else 0
    m_b, mt_b = mask.astype(BF16), mask_t.astype(BF16)
    cs_c = sum(_dot(m_b, part) for part in _split3(adt_c))
    cs_r = sum(_dot(part, mt_b) for part in _split3(adt_r))
    edge = 0 if reverse else L - 1
    tot = cs_c[edge:edge + 1, :]
    yield

    kk = lax.broadcasted_iota(jnp.int32, (LANES, SSD_INNER), 0)
    jj = lax.broadcasted_iota(jnp.int32, (LANES, SSD_INNER), 1)
    spread = (kk == off + jj // SSD_HEAD_DIM).astype(BF16)
    cols = jnp.concatenate([dt_c, jnp.exp(cs_c), jnp.exp(tot - cs_c)], axis=0)
    c_hi = cols.astype(BF16)
    c_lo = (cols - c_hi.astype(F32)).astype(BF16)
    wide = _dot(c_hi, spread) + _dot(c_lo, spread)
    dt_w, ecs_w, dst_w = wide[0:L], wide[L:2 * L], wide[2 * L:3 * L]
    yield

    xs = xbc[:, :SSD_INNER]
    b_all = xbc[:, SSD_INNER:SSD_INNER + LANES]
    c_all = xbc[:, SSD_INNER + LANES:SSD_INNER + 2 * LANES]
    xd = xs * dt_w
    xd_b = xd.astype(BF16)
    b_b = b_all.astype(BF16)
    c_b = c_all.astype(BF16)
    lane = lax.broadcasted_iota(jnp.int32, (L, LANES), 1)
    col = lax.broadcasted_iota(jnp.int32, (L, SSD_INNER), 1)

    y = None
    for g in range(SSD_GROUPS):
        c_g = jnp.where(lane // SSD_STATE == g, c_b, jnp.zeros_like(c_b))
        gmat = _dot_nt(c_g, b_b)
        for hh in range(hpg):
            h = g * hpg + hh
            dec = jnp.exp(jnp.where(mask, cs_c[:, off + h:off + h + 1] - cs_r[off + h:off + h + 1, :], -jnp.inf))
            xd_h = jnp.where(col // SSD_HEAD_DIM == h, xd_b, jnp.zeros_like(xd_b))
            part = _dot((gmat * dec).astype(BF16), xd_h)
            y = part if y is None else y + part
            yield
    return y, c_b, ecs_w, b_all.T.astype(BF16), (xd * dst_w).astype(BF16)


def _ssd_chunk_state(y_local, c_b, ecs_w, bt_b, xds_b, st_ref, reverse):
    edge = 0 if reverse else SSD_CHUNK - 1
    y = y_local + ecs_w * _dot(c_b, st_ref[...].astype(BF16))
    kr = lax.broadcasted_iota(jnp.int32, (LANES, SSD_INNER), 0)
    kc = lax.broadcasted_iota(jnp.int32, (LANES, SSD_INNER), 1)
    block = kr // SSD_STATE == kc // (SSD_HEADS // SSD_GROUPS * SSD_HEAD_DIM)
    st_ref[...] = st_ref[...] * ecs_w[edge:edge + 1, :] + jnp.where(block, _dot(bt_b, xds_b), 0.0)
    return y


def _ssd_body(*refs):
    fwd_in, bwd_in = refs[0:3], refs[3:6]
    bias_c_ref, a_c_ref, bias_r_ref, a_r_ref, dsk_ref, yf_ref, yb_ref, st_ref = refs[6:]

    @pl.when(pl.program_id(0) == 0)
    def _():
        st_ref[...] = jnp.zeros_like(st_ref)

    shared = (bias_c_ref, a_c_ref, bias_r_ref, a_r_ref)
    (y_f, xs), (y_b, _) = _run(_interleave(_ssd_block(False, *fwd_in, *shared, st_ref.at[0]),
                                           _ssd_block(True, *bwd_in, *shared, st_ref.at[1])))
    yf_ref[...] = y_f + dsk_ref[...] * xs
    yb_ref[...] = y_b


def _ssd_call(xbc, dt_c, dt_r, bias_c, a_c, bias_r, a_r, dsk, n_lat_rows):
    t = xbc.shape[0]
    L = SSD_STEP_CHUNKS * SSD_CHUNK
    assert t % L == 0 and n_lat_rows % L == 0
    n_chunks = t // L
    n_lat = n_lat_rows // L
    const = lambda i: (0, 0)

    def chunk_specs(reverse):
        cidx = functools.partial(_segment_order, n_lat=n_lat, n_all=n_chunks, reverse=reverse)
        return [pl.BlockSpec((L, SSD_CONV_DIM), lambda i: (cidx(i), 0)),
                pl.BlockSpec((L, LANES), lambda i: (cidx(i), 0)),
                pl.BlockSpec((8, L), lambda i: (0, cidx(i)))]

    def out_spec(reverse):
        cidx = functools.partial(_segment_order, n_lat=n_lat, n_all=n_chunks, reverse=reverse)
        return pl.BlockSpec((L, SSD_INNER), lambda i: (cidx(i), 0))

    chunk_args = (xbc, dt_c, dt_r)
    return pl.pallas_call(
        _ssd_body,
        grid=(n_chunks,),
        in_specs=chunk_specs(False) + chunk_specs(True)
                 + [pl.BlockSpec((1, LANES), const),
                    pl.BlockSpec((1, LANES), const),
                    pl.BlockSpec((8, L), const),
                    pl.BlockSpec((8, L), const),
                    pl.BlockSpec((1, SSD_INNER), const)],
        out_specs=[out_spec(False), out_spec(True)],
        out_shape=[jax.ShapeDtypeStruct((t, SSD_INNER), F32)] * 2,
        scratch_shapes=[pltpu.VMEM((2, SSD_GROUPS * SSD_STATE, SSD_INNER), F32)],
        compiler_params=_cparams("arbitrary"),
        name="ssd_scan",
    )(*chunk_args, *chunk_args, bias_c, a_c, bias_r, a_r, dsk)


def _cmul(ar, ai, br, bi):
    return ar * br - ai * bi, ar * bi + ai * br


def _s5_step_pair(u_ref, a):
    return jnp.concatenate([u_ref[:, 2 * a, :], u_ref[:, 2 * a + 1, :]], axis=1).astype(BF16)


def _s5_in_body(u_ref, w_ref, lam_ref, efr_ref, efi_ref, ebr_ref, ebi_ref, wp_ref):
    hs = S5_HALF_STATE

    @pl.when(pl.program_id(1) == 0)
    def _():
        lam = lam_ref[0]
        wc = w_ref[0]
        rows = lambda a: jnp.broadcast_to(a, (LANES, hs))
        fr, fi, br, bi = rows(lam[0:1]), rows(lam[1:2]), rows(lam[2:3]), rows(lam[3:4])
        rgrp = lax.broadcasted_iota(jnp.int32, (LANES, LANES), 0) // S5_GROUP
        lgrp = lax.broadcasted_iota(jnp.int32, (LANES, LANES), 1) // S5_STATE
        planes = []
        for q in range(4):
            cq = wc[:, q * S5_STATE:(q + 1) * S5_STATE]
            cq2 = jnp.concatenate([cq, cq], axis=1)
            planes.append(jnp.concatenate(
                [jnp.where(rgrp == 2 * g4 + lgrp, cq2, 0.0) for g4 in range(S5_HALF_GROUPS // 2)], axis=1))
        ar, ai = planes[0], planes[1]
        for i in reversed(range(S5_CHUNK)):
            wp_ref[i * LANES:(i + 1) * LANES, 0:hs] = ar.astype(BF16)
            wp_ref[i * LANES:(i + 1) * LANES, hs:2 * hs] = ai.astype(BF16)
            ar, ai = _cmul(ar, ai, fr, fi)
        ar, ai = planes[2], planes[3]
        for i in range(S5_CHUNK):
            wp_ref[i * LANES:(i + 1) * LANES, 2 * hs:3 * hs] = ar.astype(BF16)
            wp_ref[i * LANES:(i + 1) * LANES, 3 * hs:4 * hs] = ai.astype(BF16)
            ar, ai = _cmul(ar, ai, br, bi)

    e = None
    for a in range(S5_CHUNK // 2):
        part = _dot(_s5_step_pair(u_ref, a), wp_ref[2 * a * LANES:2 * (a + 1) * LANES, :])
        e = part if e is None else e + part
    efr_ref[...] = e[:, 0:hs]
    efi_ref[...] = e[:, hs:2 * hs]
    ebr_ref[...] = e[:, 2 * hs:3 * hs]
    ebi_ref[...] = e[:, 3 * hs:4 * hs]


def _s5_in_call(u3, w_b, lam_rows):
    nch = u3.shape[0]
    tn = _pick_tile(nch, (208, 80, 40))
    hs = S5_HALF_STATE
    col = lambda h, n: (n, h)
    return pl.pallas_call(
        _s5_in_body,
        grid=(2, nch // tn),
        in_specs=[pl.BlockSpec((tn, S5_CHUNK, LANES), lambda h, n: (n, 0, h)),
                  pl.BlockSpec((1, LANES, 4 * S5_STATE), lambda h, n: (h, 0, 0)),
                  pl.BlockSpec((1, 4, hs), lambda h, n: (h, 0, 0))],
        out_specs=[pl.BlockSpec((tn, hs), col)] * 4,
        out_shape=[jax.ShapeDtypeStruct((nch, 2 * hs), F32)] * 4,
        scratch_shapes=[pltpu.VMEM((S5_CHUNK * LANES, 4 * hs), BF16)],
        compiler_params=_cparams("arbitrary", "arbitrary"),
        name="s5_chunk_in",
    )(u3, w_b, lam_rows)


def _s5_scan_body(n_lat_blocks, rows, ctx_rows, a_ref, efr_ref, efi_ref, ebr_ref, ebi_ref,
                  hfr_ref, hfi_ref, hbr_ref, hbi_ref, st_ref):
    i = pl.program_id(0)

    @pl.when(i == 0)
    def _():
        st_ref[...] = jnp.zeros_like(st_ref)

    n_blocks = n_lat_blocks + 1
    live_f = jnp.where(_segment_order(i, n_lat_blocks, n_blocks, False) == n_lat_blocks, ctx_rows, rows)
    live_b = jnp.where(_segment_order(i, n_lat_blocks, n_blocks, True) == n_lat_blocks, ctx_rows, rows)
    row = lambda ref, j: ref[j:j + 1, :]
    afr, afi, abr, abi = row(a_ref, 0), row(a_ref, 1), row(a_ref, 2), row(a_ref, 3)
    fr, fi, br, bi = row(st_ref, 0), row(st_ref, 1), row(st_ref, 2), row(st_ref, 3)
    for j in range(rows):
        hfr_ref[j:j + 1, :] = fr
        hfi_ref[j:j + 1, :] = fi
        nr, ni = afr * fr - afi * fi + row(efr_ref, j), afr * fi + afi * fr + row(efi_ref, j)
        fr, fi = jnp.where(j < live_f, nr, fr), jnp.where(j < live_f, ni, fi)
        jb = rows - 1 - j
        hbr_ref[jb:jb + 1, :] = br
        hbi_ref[jb:jb + 1, :] = bi
        nr, ni = abr * br - abi * bi + row(ebr_ref, jb), abr * bi + abi * br + row(ebi_ref, jb)
        br, bi = jnp.where(jb < live_b, nr, br), jnp.where(jb < live_b, ni, bi)
    st_ref[0:1, :] = fr
    st_ref[1:2, :] = fi
    st_ref[2:3, :] = br
    st_ref[3:4, :] = bi


def _s5_scan_call(a_pow, efr, efi, ebr, ebi, n_lat_rows):
    nch, width = efr.shape
    rows = _pick_tile(n_lat_rows, (S5_SCAN_ROWS, 32, 16))
    ctx_rows = nch - n_lat_rows
    assert ctx_rows <= rows
    n_lat_blocks = n_lat_rows // rows
    n_tiles = n_lat_blocks + 1
    blk = (rows, width)
    fwd = lambda i: (_segment_order(i, n_lat_blocks, n_tiles, False), 0)
    bwd = lambda i: (_segment_order(i, n_lat_blocks, n_tiles, True), 0)
    shp = jax.ShapeDtypeStruct((nch, width), F32)
    return pl.pallas_call(
        functools.partial(_s5_scan_body, n_lat_blocks, rows, ctx_rows),
        grid=(n_tiles,),
        in_specs=[pl.BlockSpec((4, width), lambda i: (0, 0)),
                  pl.BlockSpec(blk, fwd), pl.BlockSpec(blk, fwd),
                  pl.BlockSpec(blk, bwd), pl.BlockSpec(blk, bwd)],
        out_specs=[pl.BlockSpec(blk, fwd), pl.BlockSpec(blk, fwd),
                   pl.BlockSpec(blk, bwd), pl.BlockSpec(blk, bwd)],
        out_shape=[shp] * 4,
        scratch_shapes=[pltpu.VMEM((4, width), F32)],
        compiler_params=_cparams("arbitrary"),
        name="s5_state_scan",
    )(a_pow, efr, efi, ebr, ebi)


def _s5_out_body(u_ref, hfr_ref, hfi_ref, hbr_ref, hbi_ref, k_ref, t_ref, c_ref, lam_ref, y_ref, wq_ref, z_ref):
    hs = S5_HALF_STATE
    width = S5_CHUNK * LANES

    @pl.when(pl.program_id(1) == 0)
    def _():
        strip = _dot(k_ref[0].astype(BF16), t_ref[...])
        rgrp = lax.broadcasted_iota(jnp.int32, strip.shape, 0) // S5_GROUP
        cgrp = (lax.broadcasted_iota(jnp.int32, strip.shape, 1) % LANES) // S5_GROUP
        strip = jnp.where(rgrp == cgrp, strip, 0.0).astype(BF16)
        z_ref[0:LANES, :] = strip[:, LANES:]
        z_ref[LANES:, :] = strip[:, :-LANES]

        cc = c_ref[0]
        ogrp = lax.broadcasted_iota(jnp.int32, (S5_STATE, LANES), 1) // S5_GROUP

        def expand(q):
            cq = cc[q * S5_STATE:(q + 1) * S5_STATE]
            return jnp.concatenate([jnp.where(ogrp == g, cq, 0.0) for g in range(S5_HALF_GROUPS)], axis=0)

        lam = lam_ref[0]
        for d, order in ((0, range(S5_CHUNK)), (1, reversed(range(S5_CHUNK)))):
            lr, li = lam[2 * d], lam[2 * d + 1]
            a, b = expand(2 * d), expand(2 * d + 1)
            for j in order:
                a, b = a * lr + b * li, b * lr - a * li
                wq_ref[2 * d * hs:(2 * d + 1) * hs, j * LANES:(j + 1) * LANES] = a.astype(BF16)
                wq_ref[(2 * d + 1) * hs:(2 * d + 2) * hs, j * LANES:(j + 1) * LANES] = b.astype(BF16)

    h_all = jnp.concatenate([hfr_ref[...], hfi_ref[...], hbr_ref[...], hbi_ref[...]], axis=1).astype(BF16)
    y = _dot(h_all, wq_ref[...])
    for a in range(S5_CHUNK // 2):
        lo = (S5_CHUNK - 2 - 2 * a) * LANES
        y += _dot(_s5_step_pair(u_ref, a), z_ref[:, lo:lo + width])
    for j in range(S5_CHUNK):
        y_ref[:, j, :] = y[:, j * LANES:(j + 1) * LANES]


def _s5_out_call(u3, hfr, hfi, hbr, hbi, w_k, w_t, w_c, lam_cols):
    nch = u3.shape[0]
    tn = _pick_tile(nch, (208, 80, 40))
    hs = S5_HALF_STATE
    col = lambda h, n: (n, h)
    blk3 = pl.BlockSpec((tn, S5_CHUNK, LANES), lambda h, n: (n, 0, h))
    return pl.pallas_call(
        _s5_out_body,
        grid=(2, nch // tn),
        in_specs=[blk3] + [pl.BlockSpec((tn, hs), col)] * 4
                 + [pl.BlockSpec((1, LANES) + w_k.shape[2:], lambda h, n: (h, 0, 0)),
                    pl.BlockSpec(w_t.shape, lambda h, n: (0, 0)),
                    pl.BlockSpec((1, 4 * S5_STATE, LANES), lambda h, n: (h, 0, 0)),
                    pl.BlockSpec((1, 4, hs, LANES), lambda h, n: (h, 0, 0, 0))],
        out_specs=blk3,
        out_shape=jax.ShapeDtypeStruct(u3.shape, F32),
        scratch_shapes=[pltpu.VMEM((4 * hs, S5_CHUNK * LANES), BF16),
                        pltpu.VMEM((2 * LANES, (S5_TOEPLITZ_BLOCKS - 1) * LANES), BF16)],
        compiler_params=_cparams("arbitrary", "arbitrary"),
        name="s5_chunk_out",
    )(u3, hfr, hfi, hbr, hbi, w_k, w_t, w_c, lam_cols)


def _s5_weights(lam_re, lam_im, log_step, b_re, b_im, c_re, c_im):
    tc = S5_CHUNK
    nl = lam_re.shape[0]
    hg = S5_HALF_GROUPS
    ein = functools.partial(jnp.einsum, precision=HI)
    step = jnp.exp(log_step)[..., None]
    er, ei = lam_re * step, lam_im * step
    mag = jnp.exp(er)
    lbr, lbi = mag * jnp.cos(ei), mag * jnp.sin(ei)
    den = lam_re * lam_re + lam_im * lam_im
    nr, ni = lbr - 1.0, lbi
    cr, ci = (nr * lam_re + ni * lam_im) / den, (ni * lam_re - nr * lam_im) / den
    bbr = cr[..., None] * b_re - ci[..., None] * b_im
    bbi = cr[..., None] * b_im + ci[..., None] * b_re
    taus = jnp.arange(tc + 1, dtype=F32)[:, None, None]
    pmag = jnp.exp(er[:, :, None] * taus)
    pr, pi = pmag * jnp.cos(ei[:, :, None] * taus), pmag * jnp.sin(ei[:, :, None] * taus)

    wr = pr[:, :, :tc, :, :, None] * bbr[:, :, None] - pi[:, :, :tc, :, :, None] * bbi[:, :, None]
    wi = pr[:, :, :tc, :, :, None] * bbi[:, :, None] + pi[:, :, :tc, :, :, None] * bbr[:, :, None]
    kk = ein('ldgcp,ldtgpk->ldtgck', c_re, wr) - ein('ldgcp,ldtgpk->ldtgck', c_im, wi)
    kf, kb = kk[:, 0], kk[:, 1]
    seq = jnp.concatenate([kb[:, :0:-1], kf[:, :1] + kb[:, :1], kf[:, 1:]], axis=1)
    seq = seq.reshape(nl, S5_TOEPLITZ_BLOCKS, 2, hg, S5_GROUP, S5_GROUP)
    w_k = seq.transpose(0, 2, 3, 5, 1, 4).reshape(nl, 2, LANES, S5_TOEPLITZ_BLOCKS * S5_GROUP)
    w_k = jnp.pad(w_k, ((0, 0), (0, 0), (0, 0), (0, S5_KCOLS - S5_TOEPLITZ_BLOCKS * S5_GROUP)))
    r = jnp.arange(S5_KCOLS)
    s = jnp.arange(S5_TOEPLITZ_BLOCKS * LANES)
    w_t = ((r[:, None] // S5_GROUP == s[None, :] // LANES)
           & (r[:, None] % S5_GROUP == s[None, :] % S5_GROUP)).astype(BF16)

    planes_b = jnp.stack([bbr[:, 0], bbi[:, 0], bbr[:, 1], bbi[:, 1]], axis=1)
    planes_b = planes_b.reshape(nl, 4, 2, hg, S5_STATE, S5_GROUP)
    w_b = planes_b.transpose(0, 2, 3, 5, 1, 4).reshape(nl, 2, LANES, 4 * S5_STATE)
    planes_c = jnp.stack([c_re[:, 0], -c_im[:, 0], c_re[:, 1], -c_im[:, 1]], axis=1)
    planes_c = planes_c.reshape(nl, 4, 2, hg, S5_GROUP, S5_STATE)
    w_c = planes_c.transpose(0, 2, 1, 5, 3, 4).reshape(nl, 2, 4 * S5_STATE, LANES)

    halves = lambda a: a.reshape(nl, 2, S5_HALF_STATE)
    lam_rows = jnp.stack([halves(lbr[:, 0]), halves(lbi[:, 0]), halves(lbr[:, 1]), halves(lbi[:, 1])], axis=2)
    lam_cols = jnp.broadcast_to(lam_rows[..., None], lam_rows.shape + (LANES,))
    full = lambda a: a.reshape(nl, S5_GROUPS * S5_STATE)
    a_pow = jnp.stack([full(pr[:, 0, tc]), full(pi[:, 0, tc]), full(pr[:, 1, tc]), full(pi[:, 1, tc])], axis=1)
    return (w_b, w_k, w_c, lam_rows, lam_cols, a_pow), w_t


def _s5_mix(u, w_b, w_k, w_c, lam_rows, lam_cols, a_pow, w_t, n_lat):
    t = u.shape[0]
    nch = t // S5_CHUNK
    u3 = u.reshape(nch, S5_CHUNK, S5_WIDTH)
    es = _s5_in_call(u3, w_b, lam_rows)
    hs = _s5_scan_call(a_pow, *es, n_lat // S5_CHUNK)
    return _s5_out_call(u3, *hs, w_k, w_t, w_c, lam_cols).reshape(t, S5_WIDTH)


def _attn_body(flag_ref, lam_ref, q_ref, k_ref, v_ref, g_ref, o_ref, acc_ref, m_ref, oh_ref, *, n_kv, post_scale):
    bounded = flag_ref[0] == 1

    def head(hh, carry):
        acc_ref[...] = jnp.zeros_like(acc_ref)

        @pl.when(bounded)
        def _():
            def step(b, carry2):
                k = k_ref[hh, b]
                vt = v_ref[hh, b]
                for c in range(2):
                    p = jnp.exp2(_dot(k, q_ref[hh, c])).astype(BF16)
                    acc_ref[c] += _dot(vt, p)
                return carry2
            lax.fori_loop(0, n_kv, step, 0, unroll=2)

        @pl.when(jnp.logical_not(bounded))
        def _():
            m_ref[...] = jnp.full_like(m_ref, -jnp.inf)

            def step(b, carry2):
                k = k_ref[hh, b]
                vt = v_ref[hh, b]
                for c in range(2):
                    s = _dot(k, q_ref[hh, c])
                    m_old = m_ref[c]
                    m_new = jnp.maximum(m_old, jnp.max(s, axis=0, keepdims=True))
                    p = jnp.exp2(s - m_new[0:1]).astype(BF16)
                    acc_ref[c] = jnp.exp2(m_old - m_new)[0:1] * acc_ref[c] + _dot(vt, p)
                    m_ref[c] = m_new
                return carry2
            lax.fori_loop(0, n_kv, step, 0)

        a0 = acc_ref[0]
        a1 = acc_ref[1]
        o = a0 / a0[DA_V_DIM:DA_V_DIM + 1] - lam_ref[0] * (a1 / a1[DA_V_DIM:DA_V_DIM + 1])
        vrow = lax.broadcasted_iota(jnp.int32, o.shape, 0) < DA_V_DIM
        o = jnp.where(vrow, o, 0.0)
        ms = jnp.sum(o * o, axis=0, keepdims=True) * (1.0 / DA_V_DIM)
        ot = (o * lax.rsqrt(ms + EPS)).T
        oh_ref[hh] = ot[:, :DA_V_DIM] * g_ref[...] * post_scale
        return carry

    lax.fori_loop(0, ATTN_HEADS_PER_STEP, head, 0)
    o_ref[...] = jnp.concatenate([oh_ref[hh] for hh in range(ATTN_HEADS_PER_STEP)], axis=1)


def _attn_body_keep_rest(flag_ref, lam_ref, q_ref, k_ref, v_ref, g_ref, rest_ref, *refs, **kw):
    del rest_ref
    _attn_body(flag_ref, lam_ref, q_ref, k_ref, v_ref, g_ref, *refs, **kw)


def _attn_call(flag, lam, qt, kb, vtb, g_sub, post_scale, q_rows, q_row0, kv_block, kv_cols, kv_col0, y_rest=None):
    nh = qt.shape[0]
    hp = ATTN_HEADS_PER_STEP
    tk = kb.shape[2]
    tq = _pick_tile(q_rows, (ATTN_TQ, 256))
    assert q_row0 % tq == 0
    q0 = q_row0 // tq
    if kv_block is None:
        n_kv = kb.shape[1]
        one = pl.Buffered(1)
        k_spec = pl.BlockSpec((hp, n_kv, tk, 2 * DA_HEAD_DIM), lambda h, i: (h, 0, 0, 0), pipeline_mode=one)
        v_spec = pl.BlockSpec((hp, n_kv, LANES, tk), lambda h, i: (h, 0, 0, 0), pipeline_mode=one)
    else:
        assert kv_col0 % kv_cols == 0 and tk % kv_cols == 0
        n_kv, cb = 1, kv_col0 // kv_cols
        k_spec = pl.BlockSpec((hp, 1, kv_cols, 2 * DA_HEAD_DIM), lambda h, i: (h, kv_block, cb, 0))
        v_spec = pl.BlockSpec((hp, 1, LANES, kv_cols), lambda h, i: (h, kv_block, 0, cb))
    smem = pl.BlockSpec(memory_space=pltpu.SMEM)
    in_specs = [smem, smem,
                pl.BlockSpec((hp, 2, 2 * DA_HEAD_DIM, tq), lambda h, i: (h, 0, 0, q0 + i)),
                k_spec, v_spec,
                pl.BlockSpec((1, DA_V_DIM), lambda h, i: (0, 0))]
    args = (flag, lam, qt, kb, vtb, g_sub)
    body = functools.partial(_attn_body, n_kv=n_kv, post_scale=post_scale)
    aliases = {}
    if y_rest is not None:
        in_specs.append(pl.BlockSpec(memory_space=pl.ANY))
        args += (y_rest,)
        body = functools.partial(_attn_body_keep_rest, n_kv=n_kv, post_scale=post_scale)
        aliases = {len(args) - 1: 0}
    return pl.pallas_call(
        body,
        grid=(nh // hp, q_rows // tq),
        in_specs=in_specs,
        out_specs=pl.BlockSpec((tq, hp * DA_V_DIM), lambda h, i: (q0 + i, h)),
        out_shape=jax.ShapeDtypeStruct((qt.shape[3], nh * DA_V_DIM), F32),
        scratch_shapes=[pltpu.VMEM((2, LANES, tq), F32), pltpu.VMEM((2, 8, tq), F32),
                        pltpu.VMEM((hp, tq, DA_V_DIM), F32)],
        input_output_aliases=aliases,
        compiler_params=_cparams("arbitrary", "arbitrary"),
        name="diff_attn",
    )(*args)


def _gelu_tanh(x):
    return 0.5 * x * (1.0 + jnp.tanh(math.sqrt(2.0 / math.pi) * (x + 0.044715 * (x * x * x))))


def _outffn_body(x_ref, yf_ref, yb_ref, z_ref, gssd_ref, ys5_ref, u_ref, d5_ref, gw_ref, gb_ref,
                 yda_ref, mod_ref, modp_ref, g2_ref, wo_ref, w1_ref, w3_ref, w2_ref, o_ref, xm_ref, hb_ref):
    i = pl.program_id(0)
    cur = i % 2
    prev = 1 - cur

    @pl.when(i == 0)
    def _():
        xm_ref[...] = jnp.zeros_like(xm_ref)
        hb_ref[...] = jnp.zeros_like(hb_ref)

    def finish():
        hp = hb_ref[prev]
        a1 = _dot(hp, w1_ref[...])
        yield
        a3 = _dot(hp, w3_ref[...])
        yield
        f = (_silu(a1) * a3).astype(BF16)
        yield
        o_ref[...] = xm_ref[prev] + modp_ref[0][5:6] * _dot(f, w2_ref[...])

    def prepare():
        mod = mod_ref[0]
        y = (yf_ref[...] + yb_ref[...]) * _silu(z_ref[...])
        y_ssd = y * lax.rsqrt(jnp.mean(y * y, axis=-1, keepdims=True) + EPS) * gssd_ref[...]
        yield
        y5 = _gelu_tanh(ys5_ref[...] + d5_ref[...] * u_ref[...])
        yield
        y5 = y5 * jax.nn.sigmoid(_dot(y5.astype(BF16), gw_ref[...].astype(BF16)) + gb_ref[...])
        yield
        o = _dot(y_ssd.astype(BF16), wo_ref[0:256, :])
        yield
        o += _dot(y5.astype(BF16), wo_ref[256:512, :])
        yield
        o += _dot(yda_ref[...].astype(BF16), wo_ref[512:1024, :])
        yield
        x = x_ref[...] + mod[2:3] * o
        xm_ref[cur] = x
        yield
        h = x * lax.rsqrt(jnp.mean(x * x, axis=-1, keepdims=True) + EPS) * g2_ref[...] * (1.0 + mod[4:5]) \
            + mod[3:4]
        hb_ref[cur] = h.astype(BF16)

    _run(_interleave(finish(), prepare()))


def _outffn_call(xs, y_f, y_b, z, g_ssd, y_s5, u, d5, glu_w, glu_b, y_da, modl, g2, w_o, w1, w3, w2, layer, n_lat,
                 latent_only):
    t, d = xs.shape
    if latent_only:
        t = n_lat
    dff = w1.shape[2]
    layer_block = lambda i: (layer, 0, 0)
    tm = FFN_ROW_TILE
    assert n_lat % tm == 0 and t % tm == 0
    n_lat_tiles = n_lat // tm
    n_tiles = t // tm
    row = lambda i: (jnp.minimum(i, n_tiles - 1), 0)
    done = lambda i: (jnp.maximum(i - 1, 0), 0)
    const = lambda i: (0, 0)
    seg = lambda i: ((jnp.minimum(i, n_tiles - 1) >= n_lat_tiles).astype(jnp.int32), 0, 0)
    seg_done = lambda i: ((i - 1 >= n_lat_tiles).astype(jnp.int32), 0, 0)
    one = pl.Buffered(1)
    return pl.pallas_call(
        _outffn_body,
        grid=(n_tiles + 1,),
        in_specs=[pl.BlockSpec((tm, d), row),
                  pl.BlockSpec((tm, 256), row), pl.BlockSpec((tm, 256), row), pl.BlockSpec((tm, 256), row),
                  pl.BlockSpec((1, 256), const),
                  pl.BlockSpec((tm, 256), row), pl.BlockSpec((tm, 256), row),
                  pl.BlockSpec((1, 256), const),
                  pl.BlockSpec((256, 256), const), pl.BlockSpec((1, 256), const),
                  pl.BlockSpec((tm, DA_WIDTH), row),
                  pl.BlockSpec((1, 8, d), seg),
                  pl.BlockSpec((1, 8, d), seg_done),
                  pl.BlockSpec((1, d), const),
                  pl.BlockSpec((None, d, d), layer_block, pipeline_mode=one),
                  pl.BlockSpec((None, d, dff), layer_block, pipeline_mode=one),
                  pl.BlockSpec((None, d, dff), layer_block, pipeline_mode=one),
                  pl.BlockSpec((None, dff, d), layer_block, pipeline_mode=one)],
        out_specs=pl.BlockSpec((tm, d), done),
        out_shape=jax.ShapeDtypeStruct((t, d), F32),
        scratch_shapes=[pltpu.VMEM((2, tm, d), F32), pltpu.VMEM((2, tm, d), BF16)],
        compiler_params=_cparams("arbitrary"),
        name="out_proj_ffn",
    )(xs, y_f, y_b, z, g_ssd, y_s5, u, d5, glu_w, glu_b, y_da, modl, modl, g2, w_o, w1, w3, w2)


def _cast_body(w_ref, o_ref):
    o_ref[...] = w_ref[...].astype(o_ref.dtype)


def _to_bf16(w):
    nl, rows, cols = w.shape
    tr = _pick_tile(rows, (512, 704, 256))
    blk = pl.BlockSpec((1, tr, cols), lambda l, r: (l, r, 0))
    return pl.pallas_call(
        _cast_body,
        grid=(nl, rows // tr),
        in_specs=[blk],
        out_specs=blk,
        out_shape=jax.ShapeDtypeStruct(w.shape, BF16),
        compiler_params=_cparams("arbitrary", "arbitrary"),
        name="weights_to_bf16",
    )(w)


def _rope_tables(n_lat, n_ctx):
    pos = jnp.arange(n_lat)
    lane = jnp.arange(LANES)
    d = lane % DA_HEAD_DIM
    axis = d // (2 * ROPE_FREQS)
    half = (d % (2 * ROPE_FREQS)) // ROPE_FREQS
    freqs = ROPE_THETA ** (-jnp.arange(ROPE_FREQS, dtype=F32) / ROPE_FREQS)
    coord = jnp.where(axis[None, :] == 0, (pos // GRID_W)[:, None], (pos % GRID_W)[:, None]).astype(F32)
    ang = coord * freqs[d % ROPE_FREQS][None, :]
    cos, sin = jnp.cos(ang), jnp.sin(ang)
    sa = jnp.where(half[None, :] == 0, -sin, 0.0)
    sb = jnp.where(half[None, :] == 1, sin, 0.0)
    pad = lambda a, v: jnp.concatenate([a, jnp.full((n_ctx, LANES), v, F32)], axis=0)
    return pad(cos, 1.0), pad(sa, 0.0), pad(sb, 0.0)


def _lane_row(vals, width=LANES):
    vals = vals.reshape(-1).astype(F32)
    return jnp.zeros((1, width), F32).at[0, :vals.shape[0]].set(vals)


def kernel(x, c, ctx, c_ctx, w_mod, b_mod, norm1, norm2, w_in, w_out, ssd_conv_w, ssd_conv_b, ssd_a_log, ssd_dt_bias, ssd_d, ssd_norm, s5_lam_re, s5_lam_im, s5_log_step, s5_b_re, s5_b_im, s5_c_re, s5_c_im, s5_d, s5_glu_w, s5_glu_b, da_q_norm, da_k_norm, da_lambda, da_sub_norm, ffn_w1, ffn_w3, ffn_w2):
    depth = w_mod.shape[0]
    bsz, n_lat, d = x.shape
    n_ctx = ctx.shape[1]
    t = n_ctx + n_lat
    tk = _pick_tile(t, (ATTN_TK, 1024, 640, 512, 256))
    assert bsz == 1 and n_ctx % ROW_TILE == 0 and n_lat % ROW_TILE == 0 and tk % ROW_TILE == 0
    assert n_ctx % (S5_CHUNK * 16) == 0 and n_lat % (S5_CHUNK * 16) == 0
    assert n_lat % n_ctx == 0 and tk % n_ctx == 0
    n_lat_tiles = n_lat // ROW_TILE

    xs = jnp.concatenate([x[0], ctx[0]], axis=0)
    cv = jnp.zeros((8, d), F32).at[0].set(c[0]).at[1].set(c_ctx)
    mod_all = _mod_call(cv, w_mod, b_mod)
    cos_t, sa_t, sb_t = _rope_tables(n_lat, n_ctx)
    gidx = jnp.arange(DA_WIDTH) // DA_HEAD_DIM
    gmat = (gidx[:, None] == gidx[None, :]).astype(BF16) * (1.0 / DA_HEAD_DIM)
    s5_w, s5_spread = _s5_weights(s5_lam_re, s5_lam_im, s5_log_step, s5_b_re, s5_b_im, s5_c_re, s5_c_im)
    w_out_b, w1_b, w3_b, w2_b = _to_bf16(w_out), _to_bf16(ffn_w1), _to_bf16(ffn_w3), _to_bf16(ffn_w2)

    for i in range(depth):
        lam_init = 0.8 - 0.6 * math.exp(-0.3 * i)
        modl = jnp.zeros((2, 8, d), F32).at[:, :6].set(mod_all[i, :2].reshape(2, 6, d))

        wi = w_in[i]
        w_r = jnp.concatenate([wi[:, 0:768], wi[:, 776:1032], wi[:, 1032:2568], wi[:, 768:776],
                               jnp.zeros((d, IN_PAD - 2568), F32)], axis=1).astype(BF16)
        gq = jnp.tile(da_q_norm[i], DA_WIDTH // DA_HEAD_DIM)[None, :]
        gk = jnp.tile(da_k_norm[i], DA_WIDTH // DA_HEAD_DIM)[None, :]
        z, xbc_raw, u, dt_raw, qh, kzb, vb = _inproj_call(
            xs, norm1[i][None, :], modl, w_r, gq, gk, gmat, cos_t, sa_t, sb_t, n_lat_tiles, tk)

        dt_r = dt_raw[:, :8].T
        cw = jnp.zeros((8, SSD_CONV_DIM), F32).at[:3].set(ssd_conv_w[i])
        a_neg = -jnp.exp(ssd_a_log[i])
        bias_c = _lane_row(ssd_dt_bias[i])
        a_c = _lane_row(a_neg)
        bias_r = jnp.broadcast_to(ssd_dt_bias[i].reshape(8, 1), (8, SSD_STEP_CHUNKS * SSD_CHUNK))
        a_r = jnp.broadcast_to(a_neg.reshape(8, 1), (8, SSD_STEP_CHUNKS * SSD_CHUNK))
        dsk = jnp.repeat(ssd_d[i], SSD_HEAD_DIM)[None, :]
        xbc = _ssd_conv_call(xbc_raw, cw, ssd_conv_b[i][None, :], n_lat)
        y_f, y_b = _ssd_call(xbc, dt_raw, dt_r, bias_c, a_c, bias_r, a_r, dsk, n_lat)

        y_s5 = _s5_mix(u, *(w[i] for w in s5_w), s5_spread, n_lat)

        lf = da_lambda[i]
        lam = (jnp.exp(jnp.sum(lf[0] * lf[1])) - jnp.exp(jnp.sum(lf[2] * lf[3])) + lam_init).reshape(1)
        g_sub = da_sub_norm[i][None, :]
        score_bound = LOG2E * math.sqrt(DA_HEAD_DIM) * jnp.max(jnp.abs(da_q_norm[i])) * jnp.max(jnp.abs(da_k_norm[i]))
        flag = (score_bound <= SCORE_LOG2_LIMIT).astype(jnp.int32).reshape(1)
        y_da = _attn_call(flag, lam, qh, kzb, vb, g_sub, 1.0 - lam_init, n_lat, 0, None, None, None)
        last = i == depth - 1
        if not last:
            y_da = _attn_call(flag, lam, qh, kzb, vb, g_sub, 1.0 - lam_init, n_ctx, n_lat,
                              n_lat // tk, n_ctx, n_lat % tk, y_rest=y_da)

        xs = _outffn_call(xs, y_f, y_b, z, ssd_norm[i][None, :], y_s5, u, s5_d[i][None, :], s5_glu_w[i],
                          s5_glu_b[i][None, :], y_da, modl, norm2[i][None, :], w_out_b, w1_b, w3_b, w2_b, i,
                          n_lat, latent_only=last)
    return xs[None]
```

```python
import functools
import math

import jax
import jax.numpy as jnp
from jax import lax
from jax.experimental import pallas as pl
from jax.experimental.pallas import tpu as pltpu

F32 = jnp.float32
BF16 = jnp.bfloat16
HI = lax.Precision.HIGHEST

EPS = 1e-6
GRID_W = 64
ROPE_THETA = 10000.0
ROPE_FREQS = 8

SSD_HEADS = 4
SSD_HEAD_DIM = 64
SSD_INNER = 256
SSD_GROUPS = 2
SSD_STATE = 64
SSD_CHUNK = 128
SSD_STEP_CHUNKS = 2
SSD_CONV_DIM = 512

S5_WIDTH = 256
S5_GROUP = 16
S5_GROUPS = 16
S5_STATE = 64
S5_CHUNK = 16
S5_HALF_GROUPS = S5_GROUPS // 2
S5_HALF_STATE = S5_HALF_GROUPS * S5_STATE
S5_SCAN_ROWS = 64
S5_TOEPLITZ_BLOCKS = 2 * S5_CHUNK - 1
S5_KCOLS = 512

DA_HEADS = 8
DA_HEAD_DIM = 32
DA_V_DIM = 64
DA_WIDTH = 512
ATTN_HEADS_PER_STEP = 2
ATTN_TQ = 2048
ATTN_TK = 1280
LOG2E = 1.4426950408889634
SCORE_LOG2_LIMIT = 100.0

LANES = 128
ROW_TILE = 256
FFN_ROW_TILE = 256
IN_PAD = 2688
VMEM_LIMIT = 56 * 1024 * 1024


def _cparams(*sem):
    return pltpu.CompilerParams(dimension_semantics=sem, vmem_limit_bytes=VMEM_LIMIT)


def _dot(a, b, precision=None):
    return jnp.dot(a, b, preferred_element_type=F32, precision=precision)


def _dot_nt(a, b, precision=None):
    return lax.dot_general(a, b, (((1,), (1,)), ((), ())), preferred_element_type=F32,
                           precision=precision)


def _dot_tn(a, b, precision=None):
    return lax.dot_general(a, b, (((0,), (0,)), ((), ())), preferred_element_type=F32,
                           precision=precision)


def _silu(x):
    return x * jax.nn.sigmoid(x)


def _pick_tile(n, candidates):
    for c in candidates:
        if n % c == 0:
            return c
    return n


def _interleave(*stages):
    results = [None] * len(stages)
    live = dict(enumerate(stages))
    while live:
        for k in list(live):
            try:
                next(live[k])
            except StopIteration as done:
                results[k] = done.value
                del live[k]
        yield
    return results


def _run(stage):
    while True:
        try:
            next(stage)
        except StopIteration as done:
            return done.value


def _segment_order(i, n_lat, n_all, reverse):
    n_ctx = n_all - n_lat
    if not reverse:
        return jnp.where(i < n_ctx, n_lat + i, i - n_ctx)
    return jnp.where(i < n_ctx, n_all - 1 - i, n_lat - 1 - (i - n_ctx))


def _mod_body(cv_ref, w_ref, b_ref, o_ref):
    o_ref[0] = _dot(_silu(cv_ref[...]), w_ref[0], HI) + b_ref[0]


def _mod_call(cv, w_mod, b_mod):
    depth, d, n = w_mod.shape
    tn = 2048
    return pl.pallas_call(
        _mod_body,
        grid=(depth, n // tn),
        in_specs=[pl.BlockSpec((8, d), lambda l, j: (0, 0)),
                  pl.BlockSpec((1, d, tn), lambda l, j: (l, 0, j)),
                  pl.BlockSpec((1, 1, tn), lambda l, j: (l, 0, j))],
        out_specs=pl.BlockSpec((1, 8, tn), lambda l, j: (l, 0, j)),
        out_shape=jax.ShapeDtypeStruct((depth, 8, n), F32),
        compiler_params=_cparams("arbitrary", "arbitrary"),
        name="adaln_mod",
    )(cv, w_mod, b_mod.reshape(depth, 1, n))


def _inproj_project(x_ref, g_ref, mod_ref, w_ref, z_ref, xbc_ref, u_ref, dt_ref, qkv_ref):
    x = x_ref[...]
    mod = mod_ref[0]
    ms = jnp.mean(x * x, axis=-1, keepdims=True)
    h = x * lax.rsqrt(ms + EPS) * g_ref[...] * (1.0 + mod[1:2]) + mod[0:1]
    p = _dot(h.astype(BF16), w_ref[...])
    z_ref[...] = p[:, 0:256]
    xbc_ref[...] = p[:, 256:768]
    u_ref[...] = p[:, 768:1024]
    dt_ref[...] = p[:, 2560:2688]
    qkv_ref[...] = p[:, 1024:2560]


def _inproj_attn_operands(qkv_ref, gq_ref, gk_ref, gm_ref, cos_ref, sa_ref, sb_ref, q_ref, k_ref, v_ref):
    p = qkv_ref[...]
    rows = p.shape[0]
    gm = gm_ref[...]
    cos = cos_ref[...]
    sa = sa_ref[...]
    sb = sb_ref[...]

    def norm_rope(t, gain, scale):
        ms32 = _dot((t * t).astype(BF16), gm)
        tn = t * lax.rsqrt(ms32 + EPS) * gain
        outs = []
        for j in range(DA_WIDTH // LANES):
            tb = tn[:, j * LANES:(j + 1) * LANES]
            ob = tb * cos + pltpu.roll(tb, LANES - ROPE_FREQS, 1) * sa + pltpu.roll(tb, ROPE_FREQS, 1) * sb
            outs.append(ob * scale)
        return jnp.concatenate(outs, axis=1)

    qt = norm_rope(p[:, 0:512], gq_ref[...], LOG2E * DA_HEAD_DIM ** -0.5).T.astype(BF16)
    kn = norm_rope(p[:, 512:1024], gk_ref[...], 1.0).astype(BF16)
    vt = p[:, 1024:1536].T.astype(BF16)
    srow = lax.broadcasted_iota(jnp.int32, (LANES - DA_V_DIM, rows), 0)
    ones_row = jnp.where(srow == 0, 1.0, 0.0).astype(BF16)
    zeros_q = jnp.zeros((DA_HEAD_DIM, rows), BF16)
    for hd in range(DA_HEADS):
        lo_, mid, hi_ = hd * DA_V_DIM, hd * DA_V_DIM + DA_HEAD_DIM, (hd + 1) * DA_V_DIM
        k_ref[hd, 0] = kn[:, lo_:hi_]
        q_ref[hd, 0, 0:DA_HEAD_DIM, :] = qt[lo_:mid]
        q_ref[hd, 0, DA_HEAD_DIM:, :] = zeros_q
        q_ref[hd, 1, 0:DA_HEAD_DIM, :] = zeros_q
        q_ref[hd, 1, DA_HEAD_DIM:, :] = qt[mid:hi_]
        v_ref[hd, 0, 0:DA_V_DIM, :] = vt[lo_:hi_]
        v_ref[hd, 0, DA_V_DIM:, :] = ones_row


def _inproj_body(x_ref, g_ref, mod_ref, w_ref, gq_ref, gk_ref, gm_ref, cos_ref, sa_ref, sb_ref,
                 z_ref, xbc_ref, u_ref, dt_ref, q_ref, k_ref, v_ref, qkv_a_ref, qkv_b_ref):
    i = pl.program_id(0)

    @pl.when(i == 0)
    def _():
        qkv_b_ref[...] = jnp.zeros_like(qkv_b_ref)

    def step(cur_ref, prev_ref):
        _inproj_attn_operands(prev_ref, gq_ref, gk_ref, gm_ref, cos_ref, sa_ref, sb_ref, q_ref, k_ref, v_ref)
        _inproj_project(x_ref, g_ref, mod_ref, w_ref, z_ref, xbc_ref, u_ref, dt_ref, cur_ref)

    @pl.when(i % 2 == 0)
    def _():
        step(qkv_a_ref, qkv_b_ref)

    @pl.when(i % 2 == 1)
    def _():
        step(qkv_b_ref, qkv_a_ref)


def _inproj_call(xs, g1, modl, w_r, gq, gk, gmat, cos_t, sa_t, sb_t, n_lat_tiles, tk):
    t, d = xs.shape
    tm = ROW_TILE
    n_tiles = t // tm
    per = tk // tm
    row = lambda i: (jnp.minimum(i, n_tiles - 1), 0)
    done = lambda i: (jnp.maximum(i - 1, 0), 0)
    const = lambda i: (0, 0)
    seg = lambda i: ((jnp.minimum(i, n_tiles - 1) >= n_lat_tiles).astype(jnp.int32), 0, 0)
    flat = [(256, F32), (512, F32), (256, F32), (LANES, F32)]

    def k_block(i):
        j = jnp.maximum(i - 1, 0)
        return (0, j // per, j % per, 0)

    def v_block(i):
        j = jnp.maximum(i - 1, 0)
        return (0, j // per, 0, j % per)

    return pl.pallas_call(
        _inproj_body,
        grid=(n_tiles + 1,),
        in_specs=[pl.BlockSpec((tm, d), row),
                  pl.BlockSpec((1, d), const),
                  pl.BlockSpec((1, 8, d), seg),
                  pl.BlockSpec((d, IN_PAD), const),
                  pl.BlockSpec((1, DA_WIDTH), const),
                  pl.BlockSpec((1, DA_WIDTH), const),
                  pl.BlockSpec((DA_WIDTH, DA_WIDTH), const),
                  pl.BlockSpec((tm, LANES), done),
                  pl.BlockSpec((tm, LANES), done),
                  pl.BlockSpec((tm, LANES), done)],
        out_specs=[pl.BlockSpec((tm, w), row) for w, _ in flat]
                  + [pl.BlockSpec((DA_HEADS, 2, 2 * DA_HEAD_DIM, tm), lambda i: (0, 0, 0, jnp.maximum(i - 1, 0))),
                     pl.BlockSpec((DA_HEADS, 1, tm, 2 * DA_HEAD_DIM), k_block),
                     pl.BlockSpec((DA_HEADS, 1, LANES, tm), v_block)],
        out_shape=[jax.ShapeDtypeStruct((t, w), dt) for w, dt in flat]
                  + [jax.ShapeDtypeStruct((DA_HEADS, 2, 2 * DA_HEAD_DIM, t), BF16),
                     jax.ShapeDtypeStruct((DA_HEADS, t // tk, tk, 2 * DA_HEAD_DIM), BF16),
                     jax.ShapeDtypeStruct((DA_HEADS, t // tk, LANES, tk), BF16)],
        scratch_shapes=[pltpu.VMEM((tm, 3 * DA_WIDTH), F32)] * 2,
        compiler_params=_cparams("arbitrary"),
        name="in_proj",
    )(xs, g1, modl, w_r, gq, gk, gmat, cos_t, sa_t, sb_t)


def _softplus(x):
    return jnp.maximum(x, 0.0) + jnp.log1p(jnp.exp(-jnp.abs(x)))


def _ssd_block(c, n_lat, n_blocks, reverse, xc_ref, xp_ref, xn_ref, dtc_ref, dtr_ref, cw_ref, cb_ref,
               bias_c_ref, a_c_ref, bias_r_ref, a_r_ref, st_ref):
    L = SSD_CHUNK
    rows = xc_ref.shape[0]
    x = xc_ref[...]
    seg_first = jnp.logical_or(c == 0, c == n_lat)
    seg_last = jnp.logical_or(c == n_lat - 1, c == n_blocks - 1)
    prow = jnp.where(seg_first, 0.0, xp_ref[7:8, :])
    nrow = jnp.where(seg_last, 0.0, xn_ref[0:1, :])
    ridx = lax.broadcasted_iota(jnp.int32, (rows, 1), 0)
    xprev = jnp.where(ridx == 0, prow, pltpu.roll(x, 1, 0))
    xnext = jnp.where(ridx == rows - 1, nrow, pltpu.roll(x, rows - 1, 0))
    cw = cw_ref[...]
    conv = xprev * cw[0:1] + x * cw[1:2] + xnext * cw[2:3] + cb_ref[...]
    xbc = _silu(conv)

    dt_c = _softplus(dtc_ref[...] + bias_c_ref[...])
    dt_r = _softplus(dtr_ref[...] + bias_r_ref[...])
    adt_c = dt_c * a_c_ref[...]
    adt_r = dt_r * a_r_ref[...]

    li = lax.broadcasted_iota(jnp.int32, (L, L), 0)
    si = lax.broadcasted_iota(jnp.int32, (L, L), 1)
    mask = (li <= si) if reverse else (li >= si)
    mask_t = (li >= si) if reverse else (li <= si)
    n_sub = rows // L
    yield
    parts = yield from _interleave(*[
        _ssd_chunk_local(xbc[s * L:(s + 1) * L], dt_c[s * L:(s + 1) * L], adt_c[s * L:(s + 1) * L],
                         adt_r[:, s * L:(s + 1) * L], mask, mask_t, reverse) for s in range(n_sub)])
    ys = [None] * n_sub
    for s in (reversed(range(n_sub)) if reverse else range(n_sub)):
        ys[s] = _ssd_chunk_state(*parts[s], st_ref, reverse)
        yield
    return jnp.concatenate(ys, axis=0), xbc[:, :SSD_INNER]


def _split3(x):
    hi = x.astype(BF16)
    r1 = x - hi.astype(F32)
    mid = r1.astype(BF16)
    lo = (r1 - mid.astype(F32)).astype(BF16)
    return hi, mid, lo


def _ssd_chunk_local(xbc, dt_c, adt_c, adt_r, mask, mask_t, reverse):
    L = SSD_CHUNK
    hpg = SSD_HEADS // SSD_GROUPS
    off = SSD_HEADS if reverse else 0
    m_b, mt_b = mask.astype(BF16), mask_t.astype(BF16)
    cs_c = sum(_dot(m_b, part) for part in _split3(adt_c))
    cs_r = sum(_dot(part, mt_b) for part in _split3(adt_r))
    edge = 0 if reverse else L - 1
    tot = cs_c[edge:edge + 1, :]
    yield

    kk = lax.broadcasted_iota(jnp.int32, (LANES, SSD_INNER), 0)
    jj = lax.broadcasted_iota(jnp.int32, (LANES, SSD_INNER), 1)
    spread = (kk == off + jj // SSD_HEAD_DIM).astype(BF16)
    cols = jnp.concatenate([dt_c, jnp.exp(cs_c), jnp.exp(tot - cs_c)], axis=0)
    c_hi = cols.astype(BF16)
    c_lo = (cols - c_hi.astype(F32)).astype(BF16)
    wide = _dot(c_hi, spread) + _dot(c_lo, spread)
    dt_w, ecs_w, dst_w = wide[0:L], wide[L:2 * L], wide[2 * L:3 * L]
    yield

    xs = xbc[:, :SSD_INNER]
    b_all = xbc[:, SSD_INNER:SSD_INNER + LANES]
    c_all = xbc[:, SSD_INNER + LANES:SSD_INNER + 2 * LANES]
    xd = xs * dt_w
    xd_b = xd.astype(BF16)
    b_b = b_all.astype(BF16)
    c_b = c_all.astype(BF16)
    lane = lax.broadcasted_iota(jnp.int32, (L, LANES), 1)
    col = lax.broadcasted_iota(jnp.int32, (L, SSD_INNER), 1)

    y = None
    for g in range(SSD_GROUPS):
        c_g = jnp.where(lane // SSD_STATE == g, c_b, jnp.zeros_like(c_b))
        gmat = _dot_nt(c_g, b_b)
        for hh in range(hpg):
            h = g * hpg + hh
            dec = jnp.exp(jnp.where(mask, cs_c[:, off + h:off + h + 1] - cs_r[off + h:off + h + 1, :], -jnp.inf))
            xd_h = jnp.where(col // SSD_HEAD_DIM == h, xd_b, jnp.zeros_like(xd_b))
            part = _dot((gmat * dec).astype(BF16), xd_h)
            y = part if y is None else y + part
            yield
    return y, c_b, ecs_w, b_all.T.astype(BF16), (xd * dst_w).astype(BF16)


def _ssd_chunk_state(y_local, c_b, ecs_w, bt_b, xds_b, st_ref, reverse):
    edge = 0 if reverse else SSD_CHUNK - 1
    y = y_local + ecs_w * _dot(c_b, st_ref[...].astype(BF16))
    kr = lax.broadcasted_iota(jnp.int32, (LANES, SSD_INNER), 0)
    kc = lax.broadcasted_iota(jnp.int32, (LANES, SSD_INNER), 1)
    block = kr // SSD_STATE == kc // (SSD_HEADS // SSD_GROUPS * SSD_HEAD_DIM)
    st_ref[...] = st_ref[...] * ecs_w[edge:edge + 1, :] + jnp.where(block, _dot(bt_b, xds_b), 0.0)
    return y


def _ssd_body(n_lat, n_chunks, *refs):
    fwd_in, bwd_in = refs[0:5], refs[5:10]
    cw_ref, cb_ref, bias_c_ref, a_c_ref, bias_r_ref, a_r_ref, dsk_ref, yf_ref, yb_ref, st_ref = refs[10:]
    i = pl.program_id(0)

    @pl.when(i == 0)
    def _():
        st_ref[...] = jnp.zeros_like(st_ref)

    shared = (cw_ref, cb_ref, bias_c_ref, a_c_ref, bias_r_ref, a_r_ref)
    (y_f, xs), (y_b, _) = _run(_interleave(
        _ssd_block(_segment_order(i, n_lat, n_chunks, False), n_lat, n_chunks, False, *fwd_in, *shared,
                   st_ref.at[0]),
        _ssd_block(_segment_order(i, n_lat, n_chunks, True), n_lat, n_chunks, True, *bwd_in, *shared,
                   st_ref.at[1])))
    yf_ref[...] = y_f + dsk_ref[...] * xs
    yb_ref[...] = y_b


def _ssd_call(xbc_raw, dt_c, dt_r, cw, cb, bias_c, a_c, bias_r, a_r, dsk, n_lat_rows):
    t = xbc_raw.shape[0]
    L = SSD_STEP_CHUNKS * SSD_CHUNK
    assert t % L == 0 and n_lat_rows % L == 0
    n_chunks = t // L
    n_lat = n_lat_rows // L
    sub = L // 8
    n_sub = t // 8
    const = lambda i: (0, 0)

    def chunk_specs(reverse):
        cidx = functools.partial(_segment_order, n_lat=n_lat, n_all=n_chunks, reverse=reverse)
        return [pl.BlockSpec((L, SSD_CONV_DIM), lambda i: (cidx(i), 0)),
                pl.BlockSpec((8, SSD_CONV_DIM), lambda i: (jnp.maximum(cidx(i) * sub - 1, 0), 0)),
                pl.BlockSpec((8, SSD_CONV_DIM), lambda i: (jnp.minimum((cidx(i) + 1) * sub, n_sub - 1), 0)),
                pl.BlockSpec((L, LANES), lambda i: (cidx(i), 0)),
                pl.BlockSpec((8, L), lambda i: (0, cidx(i)))]

    def out_spec(reverse):
        cidx = functools.partial(_segment_order, n_lat=n_lat, n_all=n_chunks, reverse=reverse)
        return pl.BlockSpec((L, SSD_INNER), lambda i: (cidx(i), 0))

    chunk_args = (xbc_raw, xbc_raw, xbc_raw, dt_c, dt_r)
    return pl.pallas_call(
        functools.partial(_ssd_body, n_lat, n_chunks),
        grid=(n_chunks,),
        in_specs=chunk_specs(False) + chunk_specs(True)
                 + [pl.BlockSpec((8, SSD_CONV_DIM), const),
                    pl.BlockSpec((1, SSD_CONV_DIM), const),
                    pl.BlockSpec((1, LANES), const),
                    pl.BlockSpec((1, LANES), const),
                    pl.BlockSpec((8, L), const),
                    pl.BlockSpec((8, L), const),
                    pl.BlockSpec((1, SSD_INNER), const)],
        out_specs=[out_spec(False), out_spec(True)],
        out_shape=[jax.ShapeDtypeStruct((t, SSD_INNER), F32)] * 2,
        scratch_shapes=[pltpu.VMEM((2, SSD_GROUPS * SSD_STATE, SSD_INNER), F32)],
        compiler_params=_cparams("arbitrary"),
        name="ssd_scan",
    )(*chunk_args, *chunk_args, cw, cb, bias_c, a_c, bias_r, a_r, dsk)


def _cmul(ar, ai, br, bi):
    return ar * br - ai * bi, ar * bi + ai * br


def _s5_step_pair(u_ref, a):
    return jnp.concatenate([u_ref[:, 2 * a, :], u_ref[:, 2 * a + 1, :]], axis=1).astype(BF16)


def _s5_in_body(u_ref, w_ref, lam_ref, efr_ref, efi_ref, ebr_ref, ebi_ref, wp_ref):
    hs = S5_HALF_STATE

    @pl.when(pl.program_id(1) == 0)
    def _():
        lam = lam_ref[0]
        wc = w_ref[0]
        rows = lambda a: jnp.broadcast_to(a, (LANES, hs))
        fr, fi, br, bi = rows(lam[0:1]), rows(lam[1:2]), rows(lam[2:3]), rows(lam[3:4])
        rgrp = lax.broadcasted_iota(jnp.int32, (LANES, LANES), 0) // S5_GROUP
        lgrp = lax.broadcasted_iota(jnp.int32, (LANES, LANES), 1) // S5_STATE
        planes = []
        for q in range(4):
            cq = wc[:, q * S5_STATE:(q + 1) * S5_STATE]
            cq2 = jnp.concatenate([cq, cq], axis=1)
            planes.append(jnp.concatenate(
                [jnp.where(rgrp == 2 * g4 + lgrp, cq2, 0.0) for g4 in range(S5_HALF_GROUPS // 2)], axis=1))
        ar, ai = planes[0], planes[1]
        for i in reversed(range(S5_CHUNK)):
            wp_ref[i * LANES:(i + 1) * LANES, 0:hs] = ar.astype(BF16)
            wp_ref[i * LANES:(i + 1) * LANES, hs:2 * hs] = ai.astype(BF16)
            ar, ai = _cmul(ar, ai, fr, fi)
        ar, ai = planes[2], planes[3]
        for i in range(S5_CHUNK):
            wp_ref[i * LANES:(i + 1) * LANES, 2 * hs:3 * hs] = ar.astype(BF16)
            wp_ref[i * LANES:(i + 1) * LANES, 3 * hs:4 * hs] = ai.astype(BF16)
            ar, ai = _cmul(ar, ai, br, bi)

    e = None
    for a in range(S5_CHUNK // 2):
        part = _dot(_s5_step_pair(u_ref, a), wp_ref[2 * a * LANES:2 * (a + 1) * LANES, :])
        e = part if e is None else e + part
    efr_ref[...] = e[:, 0:hs]
    efi_ref[...] = e[:, hs:2 * hs]
    ebr_ref[...] = e[:, 2 * hs:3 * hs]
    ebi_ref[...] = e[:, 3 * hs:4 * hs]


def _s5_in_call(u3, w_b, lam_rows):
    nch = u3.shape[0]
    tn = _pick_tile(nch, (208, 80, 40))
    hs = S5_HALF_STATE
    col = lambda h, n: (n, h)
    return pl.pallas_call(
        _s5_in_body,
        grid=(2, nch // tn),
        in_specs=[pl.BlockSpec((tn, S5_CHUNK, LANES), lambda h, n: (n, 0, h)),
                  pl.BlockSpec((1, LANES, 4 * S5_STATE), lambda h, n: (h, 0, 0)),
                  pl.BlockSpec((1, 4, hs), lambda h, n: (h, 0, 0))],
        out_specs=[pl.BlockSpec((tn, hs), col)] * 4,
        out_shape=[jax.ShapeDtypeStruct((nch, 2 * hs), F32)] * 4,
        scratch_shapes=[pltpu.VMEM((S5_CHUNK * LANES, 4 * hs), BF16)],
        compiler_params=_cparams("arbitrary", "arbitrary"),
        name="s5_chunk_in",
    )(u3, w_b, lam_rows)


def _s5_scan_body(n_lat_blocks, rows, ctx_rows, a_ref, efr_ref, efi_ref, ebr_ref, ebi_ref,
                  hfr_ref, hfi_ref, hbr_ref, hbi_ref, st_ref):
    i = pl.program_id(0)

    @pl.when(i == 0)
    def _():
        st_ref[...] = jnp.zeros_like(st_ref)

    n_blocks = n_lat_blocks + 1
    live_f = jnp.where(_segment_order(i, n_lat_blocks, n_blocks, False) == n_lat_blocks, ctx_rows, rows)
    live_b = jnp.where(_segment_order(i, n_lat_blocks, n_blocks, True) == n_lat_blocks, ctx_rows, rows)
    row = lambda ref, j: ref[j:j + 1, :]
    afr, afi, abr, abi = row(a_ref, 0), row(a_ref, 1), row(a_ref, 2), row(a_ref, 3)
    fr, fi, br, bi = row(st_ref, 0), row(st_ref, 1), row(st_ref, 2), row(st_ref, 3)
    for j in range(rows):
        hfr_ref[j:j + 1, :] = fr
        hfi_ref[j:j + 1, :] = fi
        nr, ni = afr * fr - afi * fi + row(efr_ref, j), afr * fi + afi * fr + row(efi_ref, j)
        fr, fi = jnp.where(j < live_f, nr, fr), jnp.where(j < live_f, ni, fi)
        jb = rows - 1 - j
        hbr_ref[jb:jb + 1, :] = br
        hbi_ref[jb:jb + 1, :] = bi
        nr, ni = abr * br - abi * bi + row(ebr_ref, jb), abr * bi + abi * br + row(ebi_ref, jb)
        br, bi = jnp.where(jb < live_b, nr, br), jnp.where(jb < live_b, ni, bi)
    st_ref[0:1, :] = fr
    st_ref[1:2, :] = fi
    st_ref[2:3, :] = br
    st_ref[3:4, :] = bi


def _s5_scan_call(a_pow, efr, efi, ebr, ebi, n_lat_rows):
    nch, width = efr.shape
    rows = _pick_tile(n_lat_rows, (S5_SCAN_ROWS, 32, 16))
    ctx_rows = nch - n_lat_rows
    assert ctx_rows <= rows
    n_lat_blocks = n_lat_rows // rows
    n_tiles = n_lat_blocks + 1
    blk = (rows, width)
    fwd = lambda i: (_segment_order(i, n_lat_blocks, n_tiles, False), 0)
    bwd = lambda i: (_segment_order(i, n_lat_blocks, n_tiles, True), 0)
    shp = jax.ShapeDtypeStruct((nch, width), F32)
    return pl.pallas_call(
        functools.partial(_s5_scan_body, n_lat_blocks, rows, ctx_rows),
        grid=(n_tiles,),
        in_specs=[pl.BlockSpec((4, width), lambda i: (0, 0)),
                  pl.BlockSpec(blk, fwd), pl.BlockSpec(blk, fwd),
                  pl.BlockSpec(blk, bwd), pl.BlockSpec(blk, bwd)],
        out_specs=[pl.BlockSpec(blk, fwd), pl.BlockSpec(blk, fwd),
                   pl.BlockSpec(blk, bwd), pl.BlockSpec(blk, bwd)],
        out_shape=[shp] * 4,
        scratch_shapes=[pltpu.VMEM((4, width), F32)],
        compiler_params=_cparams("arbitrary"),
        name="s5_state_scan",
    )(a_pow, efr, efi, ebr, ebi)


def _s5_out_body(u_ref, hfr_ref, hfi_ref, hbr_ref, hbi_ref, k_ref, t_ref, c_ref, lam_ref, y_ref, wq_ref, z_ref):
    hs = S5_HALF_STATE
    width = S5_CHUNK * LANES

    @pl.when(pl.program_id(1) == 0)
    def _():
        strip = _dot(k_ref[0].astype(BF16), t_ref[...])
        rgrp = lax.broadcasted_iota(jnp.int32, strip.shape, 0) // S5_GROUP
        cgrp = (lax.broadcasted_iota(jnp.int32, strip.shape, 1) % LANES) // S5_GROUP
        strip = jnp.where(rgrp == cgrp, strip, 0.0).astype(BF16)
        z_ref[0:LANES, :] = strip[:, LANES:]
        z_ref[LANES:, :] = strip[:, :-LANES]

        cc = c_ref[0]
        ogrp = lax.broadcasted_iota(jnp.int32, (S5_STATE, LANES), 1) // S5_GROUP

        def expand(q):
            cq = cc[q * S5_STATE:(q + 1) * S5_STATE]
            return jnp.concatenate([jnp.where(ogrp == g, cq, 0.0) for g in range(S5_HALF_GROUPS)], axis=0)

        lam = lam_ref[0]
        for d, order in ((0, range(S5_CHUNK)), (1, reversed(range(S5_CHUNK)))):
            lr, li = lam[2 * d], lam[2 * d + 1]
            a, b = expand(2 * d), expand(2 * d + 1)
            for j in order:
                a, b = a * lr + b * li, b * lr - a * li
                wq_ref[2 * d * hs:(2 * d + 1) * hs, j * LANES:(j + 1) * LANES] = a.astype(BF16)
                wq_ref[(2 * d + 1) * hs:(2 * d + 2) * hs, j * LANES:(j + 1) * LANES] = b.astype(BF16)

    h_all = jnp.concatenate([hfr_ref[...], hfi_ref[...], hbr_ref[...], hbi_ref[...]], axis=1).astype(BF16)
    y = _dot(h_all, wq_ref[...])
    for a in range(S5_CHUNK // 2):
        lo = (S5_CHUNK - 2 - 2 * a) * LANES
        y += _dot(_s5_step_pair(u_ref, a), z_ref[:, lo:lo + width])
    for j in range(S5_CHUNK):
        y_ref[:, j, :] = y[:, j * LANES:(j + 1) * LANES]


def _s5_out_call(u3, hfr, hfi, hbr, hbi, w_k, w_t, w_c, lam_cols):
    nch = u3.shape[0]
    tn = _pick_tile(nch, (208, 80, 40))
    hs = S5_HALF_STATE
    col = lambda h, n: (n, h)
    blk3 = pl.BlockSpec((tn, S5_CHUNK, LANES), lambda h, n: (n, 0, h))
    return pl.pallas_call(
        _s5_out_body,
        grid=(2, nch // tn),
        in_specs=[blk3] + [pl.BlockSpec((tn, hs), col)] * 4
                 + [pl.BlockSpec((1, LANES) + w_k.shape[2:], lambda h, n: (h, 0, 0)),
                    pl.BlockSpec(w_t.shape, lambda h, n: (0, 0)),
                    pl.BlockSpec((1, 4 * S5_STATE, LANES), lambda h, n: (h, 0, 0)),
                    pl.BlockSpec((1, 4, hs, LANES), lambda h, n: (h, 0, 0, 0))],
        out_specs=blk3,
        out_shape=jax.ShapeDtypeStruct(u3.shape, F32),
        scratch_shapes=[pltpu.VMEM((4 * hs, S5_CHUNK * LANES), BF16),
                        pltpu.VMEM((2 * LANES, (S5_TOEPLITZ_BLOCKS - 1) * LANES), BF16)],
        compiler_params=_cparams("arbitrary", "arbitrary"),
        name="s5_chunk_out",
    )(u3, hfr, hfi, hbr, hbi, w_k, w_t, w_c, lam_cols)


def _s5_weights(lam_re, lam_im, log_step, b_re, b_im, c_re, c_im):
    tc = S5_CHUNK
    nl = lam_re.shape[0]
    hg = S5_HALF_GROUPS
    ein = functools.partial(jnp.einsum, precision=HI)
    step = jnp.exp(log_step)[..., None]
    er, ei = lam_re * step, lam_im * step
    mag = jnp.exp(er)
    lbr, lbi = mag * jnp.cos(ei), mag * jnp.sin(ei)
    den = lam_re * lam_re + lam_im * lam_im
    nr, ni = lbr - 1.0, lbi
    cr, ci = (nr * lam_re + ni * lam_im) / den, (ni * lam_re - nr * lam_im) / den
    bbr = cr[..., None] * b_re - ci[..., None] * b_im
    bbi = cr[..., None] * b_im + ci[..., None] * b_re
    taus = jnp.arange(tc + 1, dtype=F32)[:, None, None]
    pmag = jnp.exp(er[:, :, None] * taus)
    pr, pi = pmag * jnp.cos(ei[:, :, None] * taus), pmag * jnp.sin(ei[:, :, None] * taus)

    wr = pr[:, :, :tc, :, :, None] * bbr[:, :, None] - pi[:, :, :tc, :, :, None] * bbi[:, :, None]
    wi = pr[:, :, :tc, :, :, None] * bbi[:, :, None] + pi[:, :, :tc, :, :, None] * bbr[:, :, None]
    kk = ein('ldgcp,ldtgpk->ldtgck', c_re, wr) - ein('ldgcp,ldtgpk->ldtgck', c_im, wi)
    kf, kb = kk[:, 0], kk[:, 1]
    seq = jnp.concatenate([kb[:, :0:-1], kf[:, :1] + kb[:, :1], kf[:, 1:]], axis=1)
    seq = seq.reshape(nl, S5_TOEPLITZ_BLOCKS, 2, hg, S5_GROUP, S5_GROUP)
    w_k = seq.transpose(0, 2, 3, 5, 1, 4).reshape(nl, 2, LANES, S5_TOEPLITZ_BLOCKS * S5_GROUP)
    w_k = jnp.pad(w_k, ((0, 0), (0, 0), (0, 0), (0, S5_KCOLS - S5_TOEPLITZ_BLOCKS * S5_GROUP)))
    r = jnp.arange(S5_KCOLS)
    s = jnp.arange(S5_TOEPLITZ_BLOCKS * LANES)
    w_t = ((r[:, None] // S5_GROUP == s[None, :] // LANES)
           & (r[:, None] % S5_GROUP == s[None, :] % S5_GROUP)).astype(BF16)

    planes_b = jnp.stack([bbr[:, 0], bbi[:, 0], bbr[:, 1], bbi[:, 1]], axis=1)
    planes_b = planes_b.reshape(nl, 4, 2, hg, S5_STATE, S5_GROUP)
    w_b = planes_b.transpose(0, 2, 3, 5, 1, 4).reshape(nl, 2, LANES, 4 * S5_STATE)
    planes_c = jnp.stack([c_re[:, 0], -c_im[:, 0], c_re[:, 1], -c_im[:, 1]], axis=1)
    planes_c = planes_c.reshape(nl, 4, 2, hg, S5_GROUP, S5_STATE)
    w_c = planes_c.transpose(0, 2, 1, 5, 3, 4).reshape(nl, 2, 4 * S5_STATE, LANES)

    halves = lambda a: a.reshape(nl, 2, S5_HALF_STATE)
    lam_rows = jnp.stack([halves(lbr[:, 0]), halves(lbi[:, 0]), halves(lbr[:, 1]), halves(lbi[:, 1])], axis=2)
    lam_cols = jnp.broadcast_to(lam_rows[..., None], lam_rows.shape + (LANES,))
    full = lambda a: a.reshape(nl, S5_GROUPS * S5_STATE)
    a_pow = jnp.stack([full(pr[:, 0, tc]), full(pi[:, 0, tc]), full(pr[:, 1, tc]), full(pi[:, 1, tc])], axis=1)
    return (w_b, w_k, w_c, lam_rows, lam_cols, a_pow), w_t


def _s5_mix(u, w_b, w_k, w_c, lam_rows, lam_cols, a_pow, w_t, n_lat):
    t = u.shape[0]
    nch = t // S5_CHUNK
    u3 = u.reshape(nch, S5_CHUNK, S5_WIDTH)
    es = _s5_in_call(u3, w_b, lam_rows)
    hs = _s5_scan_call(a_pow, *es, n_lat // S5_CHUNK)
    return _s5_out_call(u3, *hs, w_k, w_t, w_c, lam_cols).reshape(t, S5_WIDTH)


def _attn_body(flag_ref, lam_ref, q_ref, k_ref, v_ref, g_ref, o_ref, acc_ref, m_ref, oh_ref, *, n_kv, post_scale):
    bounded = flag_ref[0] == 1

    def head(hh, carry):
        acc_ref[...] = jnp.zeros_like(acc_ref)

        @pl.when(bounded)
        def _():
            def step(b, carry2):
                k = k_ref[hh, b]
                vt = v_ref[hh, b]
                for c in range(2):
                    p = jnp.exp2(_dot(k, q_ref[hh, c])).astype(BF16)
                    acc_ref[c] += _dot(vt, p)
                return carry2
            lax.fori_loop(0, n_kv, step, 0, unroll=3)

        @pl.when(jnp.logical_not(bounded))
        def _():
            m_ref[...] = jnp.full_like(m_ref, -jnp.inf)

            def step(b, carry2):
                k = k_ref[hh, b]
                vt = v_ref[hh, b]
                for c in range(2):
                    s = _dot(k, q_ref[hh, c])
                    m_old = m_ref[c]
                    m_new = jnp.maximum(m_old, jnp.max(s, axis=0, keepdims=True))
                    p = jnp.exp2(s - m_new[0:1]).astype(BF16)
                    acc_ref[c] = jnp.exp2(m_old - m_new)[0:1] * acc_ref[c] + _dot(vt, p)
                    m_ref[c] = m_new
                return carry2
            lax.fori_loop(0, n_kv, step, 0)

        a0 = acc_ref[0]
        a1 = acc_ref[1]
        o = a0 / a0[DA_V_DIM:DA_V_DIM + 1] - lam_ref[0] * (a1 / a1[DA_V_DIM:DA_V_DIM + 1])
        vrow = lax.broadcasted_iota(jnp.int32, o.shape, 0) < DA_V_DIM
        o = jnp.where(vrow, o, 0.0)
        ms = jnp.sum(o * o, axis=0, keepdims=True) * (1.0 / DA_V_DIM)
        ot = (o * lax.rsqrt(ms + EPS)).T
        oh_ref[hh] = ot[:, :DA_V_DIM] * g_ref[...] * post_scale
        return carry

    lax.fori_loop(0, ATTN_HEADS_PER_STEP, head, 0)
    o_ref[...] = jnp.concatenate([oh_ref[hh] for hh in range(ATTN_HEADS_PER_STEP)], axis=1)


def _attn_body_keep_rest(flag_ref, lam_ref, q_ref, k_ref, v_ref, g_ref, rest_ref, *refs, **kw):
    del rest_ref
    _attn_body(flag_ref, lam_ref, q_ref, k_ref, v_ref, g_ref, *refs, **kw)


def _attn_call(flag, lam, qt, kb, vtb, g_sub, post_scale, q_rows, q_row0, kv_block, kv_cols, kv_col0, y_rest=None):
    nh = qt.shape[0]
    hp = ATTN_HEADS_PER_STEP
    tk = kb.shape[2]
    tq = _pick_tile(q_rows, (ATTN_TQ, 256))
    assert q_row0 % tq == 0
    q0 = q_row0 // tq
    if kv_block is None:
        n_kv = kb.shape[1]
        one = pl.Buffered(1)
        k_spec = pl.BlockSpec((hp, n_kv, tk, 2 * DA_HEAD_DIM), lambda h, i: (h, 0, 0, 0), pipeline_mode=one)
        v_spec = pl.BlockSpec((hp, n_kv, LANES, tk), lambda h, i: (h, 0, 0, 0), pipeline_mode=one)
    else:
        assert kv_col0 % kv_cols == 0 and tk % kv_cols == 0
        n_kv, cb = 1, kv_col0 // kv_cols
        k_spec = pl.BlockSpec((hp, 1, kv_cols, 2 * DA_HEAD_DIM), lambda h, i: (h, kv_block, cb, 0))
        v_spec = pl.BlockSpec((hp, 1, LANES, kv_cols), lambda h, i: (h, kv_block, 0, cb))
    smem = pl.BlockSpec(memory_space=pltpu.SMEM)
    in_specs = [smem, smem,
                pl.BlockSpec((hp, 2, 2 * DA_HEAD_DIM, tq), lambda h, i: (h, 0, 0, q0 + i)),
                k_spec, v_spec,
                pl.BlockSpec((1, DA_V_DIM), lambda h, i: (0, 0))]
    args = (flag, lam, qt, kb, vtb, g_sub)
    body = functools.partial(_attn_body, n_kv=n_kv, post_scale=post_scale)
    aliases = {}
    if y_rest is not None:
        in_specs.append(pl.BlockSpec(memory_space=pl.ANY))
        args += (y_rest,)
        body = functools.partial(_attn_body_keep_rest, n_kv=n_kv, post_scale=post_scale)
        aliases = {len(args) - 1: 0}
    return pl.pallas_call(
        body,
        grid=(nh // hp, q_rows // tq),
        in_specs=in_specs,
        out_specs=pl.BlockSpec((tq, hp * DA_V_DIM), lambda h, i: (q0 + i, h)),
        out_shape=jax.ShapeDtypeStruct((qt.shape[3], nh * DA_V_DIM), F32),
        scratch_shapes=[pltpu.VMEM((2, LANES, tq), F32), pltpu.VMEM((2, 8, tq), F32),
                        pltpu.VMEM((hp, tq, DA_V_DIM), F32)],
        input_output_aliases=aliases,
        compiler_params=_cparams("arbitrary", "arbitrary"),
        name="diff_attn",
    )(*args)


def _gelu_tanh(x):
    return 0.5 * x * (1.0 + jnp.tanh(math.sqrt(2.0 / math.pi) * (x + 0.044715 * (x * x * x))))


def _outffn_body(x_ref, yf_ref, yb_ref, z_ref, gssd_ref, ys5_ref, u_ref, d5_ref, gw_ref, gb_ref,
                 yda_ref, mod_ref, modp_ref, g2_ref, wo_ref, w1_ref, w3_ref, w2_ref, o_ref, xm_ref, hb_ref):
    i = pl.program_id(0)
    cur = i % 2
    prev = 1 - cur

    @pl.when(i == 0)
    def _():
        xm_ref[...] = jnp.zeros_like(xm_ref)
        hb_ref[...] = jnp.zeros_like(hb_ref)

    def finish():
        hp = hb_ref[prev]
        a1 = _dot(hp, w1_ref[...])
        yield
        a3 = _dot(hp, w3_ref[...])
        yield
        f = (_silu(a1) * a3).astype(BF16)
        yield
        o_ref[...] = xm_ref[prev] + modp_ref[0][5:6] * _dot(f, w2_ref[...])

    def prepare():
        mod = mod_ref[0]
        y = (yf_ref[...] + yb_ref[...]) * _silu(z_ref[...])
        y_ssd = y * lax.rsqrt(jnp.mean(y * y, axis=-1, keepdims=True) + EPS) * gssd_ref[...]
        yield
        y5 = _gelu_tanh(ys5_ref[...] + d5_ref[...] * u_ref[...])
        yield
        y5 = y5 * jax.nn.sigmoid(_dot(y5.astype(BF16), gw_ref[...].astype(BF16)) + gb_ref[...])
        yield
        o = _dot(y_ssd.astype(BF16), wo_ref[0:256, :])
        yield
        o += _dot(y5.astype(BF16), wo_ref[256:512, :])
        yield
        o += _dot(yda_ref[...].astype(BF16), wo_ref[512:1024, :])
        yield
        x = x_ref[...] + mod[2:3] * o
        xm_ref[cur] = x
        yield
        h = x * lax.rsqrt(jnp.mean(x * x, axis=-1, keepdims=True) + EPS) * g2_ref[...] * (1.0 + mod[4:5]) \
            + mod[3:4]
        hb_ref[cur] = h.astype(BF16)

    _run(_interleave(finish(), prepare()))


def _outffn_call(xs, y_f, y_b, z, g_ssd, y_s5, u, d5, glu_w, glu_b, y_da, modl, g2, w_o, w1, w3, w2, layer, n_lat,
                 latent_only):
    t, d = xs.shape
    if latent_only:
        t = n_lat
    dff = w1.shape[2]
    layer_block = lambda i: (layer, 0, 0)
    tm = FFN_ROW_TILE
    assert n_lat % tm == 0 and t % tm == 0
    n_lat_tiles = n_lat // tm
    n_tiles = t // tm
    row = lambda i: (jnp.minimum(i, n_tiles - 1), 0)
    done = lambda i: (jnp.maximum(i - 1, 0), 0)
    const = lambda i: (0, 0)
    seg = lambda i: ((jnp.minimum(i, n_tiles - 1) >= n_lat_tiles).astype(jnp.int32), 0, 0)
    seg_done = lambda i: ((i - 1 >= n_lat_tiles).astype(jnp.int32), 0, 0)
    one = pl.Buffered(1)
    return pl.pallas_call(
        _outffn_body,
        grid=(n_tiles + 1,),
        in_specs=[pl.BlockSpec((tm, d), row),
                  pl.BlockSpec((tm, 256), row), pl.BlockSpec((tm, 256), row), pl.BlockSpec((tm, 256), row),
                  pl.BlockSpec((1, 256), const),
                  pl.BlockSpec((tm, 256), row), pl.BlockSpec((tm, 256), row),
                  pl.BlockSpec((1, 256), const),
                  pl.BlockSpec((256, 256), const), pl.BlockSpec((1, 256), const),
                  pl.BlockSpec((tm, DA_WIDTH), row),
                  pl.BlockSpec((1, 8, d), seg),
                  pl.BlockSpec((1, 8, d), seg_done),
                  pl.BlockSpec((1, d), const),
                  pl.BlockSpec((None, d, d), layer_block, pipeline_mode=one),
                  pl.BlockSpec((None, d, dff), layer_block, pipeline_mode=one),
                  pl.BlockSpec((None, d, dff), layer_block, pipeline_mode=one),
                  pl.BlockSpec((None, dff, d), layer_block, pipeline_mode=one)],
        out_specs=pl.BlockSpec((tm, d), done),
        out_shape=jax.ShapeDtypeStruct((t, d), F32),
        scratch_shapes=[pltpu.VMEM((2, tm, d), F32), pltpu.VMEM((2, tm, d), BF16)],
        compiler_params=_cparams("arbitrary"),
        name="out_proj_ffn",
    )(xs, y_f, y_b, z, g_ssd, y_s5, u, d5, glu_w, glu_b, y_da, modl, modl, g2, w_o, w1, w3, w2)


def _cast_body(w_ref, o_ref):
    o_ref[...] = w_ref[...].astype(o_ref.dtype)


def _to_bf16(w):
    nl, rows, cols = w.shape
    tr = _pick_tile(rows, (512, 704, 256))
    blk = pl.BlockSpec((1, tr, cols), lambda l, r: (l, r, 0))
    return pl.pallas_call(
        _cast_body,
        grid=(nl, rows // tr),
        in_specs=[blk],
        out_specs=blk,
        out_shape=jax.ShapeDtypeStruct(w.shape, BF16),
        compiler_params=_cparams("arbitrary", "arbitrary"),
        name="weights_to_bf16",
    )(w)


def _rope_tables(n_lat, n_ctx):
    pos = jnp.arange(n_lat)
    lane = jnp.arange(LANES)
    d = lane % DA_HEAD_DIM
    axis = d // (2 * ROPE_FREQS)
    half = (d % (2 * ROPE_FREQS)) // ROPE_FREQS
    freqs = ROPE_THETA ** (-jnp.arange(ROPE_FREQS, dtype=F32) / ROPE_FREQS)
    coord = jnp.where(axis[None, :] == 0, (pos // GRID_W)[:, None], (pos % GRID_W)[:, None]).astype(F32)
    ang = coord * freqs[d % ROPE_FREQS][None, :]
    cos, sin = jnp.cos(ang), jnp.sin(ang)
    sa = jnp.where(half[None, :] == 0, -sin, 0.0)
    sb = jnp.where(half[None, :] == 1, sin, 0.0)
    pad = lambda a, v: jnp.concatenate([a, jnp.full((n_ctx, LANES), v, F32)], axis=0)
    return pad(cos, 1.0), pad(sa, 0.0), pad(sb, 0.0)


def _lane_row(vals, width=LANES):
    vals = vals.reshape(-1).astype(F32)
    return jnp.zeros((1, width), F32).at[0, :vals.shape[0]].set(vals)


def kernel(x, c, ctx, c_ctx, w_mod, b_mod, norm1, norm2, w_in, w_out, ssd_conv_w, ssd_conv_b, ssd_a_log, ssd_dt_bias, ssd_d, ssd_norm, s5_lam_re, s5_lam_im, s5_log_step, s5_b_re, s5_b_im, s5_c_re, s5_c_im, s5_d, s5_glu_w, s5_glu_b, da_q_norm, da_k_norm, da_lambda, da_sub_norm, ffn_w1, ffn_w3, ffn_w2):
    depth = w_mod.shape[0]
    bsz, n_lat, d = x.shape
    n_ctx = ctx.shape[1]
    t = n_ctx + n_lat
    tk = _pick_tile(t, (ATTN_TK, 1024, 640, 512, 256))
    assert bsz == 1 and n_ctx % ROW_TILE == 0 and n_lat % ROW_TILE == 0 and tk % ROW_TILE == 0
    assert n_ctx % (S5_CHUNK * 16) == 0 and n_lat % (S5_CHUNK * 16) == 0
    assert n_lat % n_ctx == 0 and tk % n_ctx == 0
    n_lat_tiles = n_lat // ROW_TILE

    xs = jnp.concatenate([x[0], ctx[0]], axis=0)
    cv = jnp.zeros((8, d), F32).at[0].set(c[0]).at[1].set(c_ctx)
    mod_all = _mod_call(cv, w_mod, b_mod)
    cos_t, sa_t, sb_t = _rope_tables(n_lat, n_ctx)
    gidx = jnp.arange(DA_WIDTH) // DA_HEAD_DIM
    gmat = (gidx[:, None] == gidx[None, :]).astype(BF16) * (1.0 / DA_HEAD_DIM)
    s5_w, s5_spread = _s5_weights(s5_lam_re, s5_lam_im, s5_log_step, s5_b_re, s5_b_im, s5_c_re, s5_c_im)
    w_out_b, w1_b, w3_b, w2_b = _to_bf16(w_out), _to_bf16(ffn_w1), _to_bf16(ffn_w3), _to_bf16(ffn_w2)

    for i in range(depth):
        lam_init = 0.8 - 0.6 * math.exp(-0.3 * i)
        modl = jnp.zeros((2, 8, d), F32).at[:, :6].set(mod_all[i, :2].reshape(2, 6, d))

        wi = w_in[i]
        w_r = jnp.concatenate([wi[:, 0:768], wi[:, 776:1032], wi[:, 1032:2568], wi[:, 768:776],
                               jnp.zeros((d, IN_PAD - 2568), F32)], axis=1).astype(BF16)
        gq = jnp.tile(da_q_norm[i], DA_WIDTH // DA_HEAD_DIM)[None, :]
        gk = jnp.tile(da_k_norm[i], DA_WIDTH // DA_HEAD_DIM)[None, :]
        z, xbc_raw, u, dt_raw, qh, kzb, vb = _inproj_call(
            xs, norm1[i][None, :], modl, w_r, gq, gk, gmat, cos_t, sa_t, sb_t, n_lat_tiles, tk)

        dt_r = dt_raw[:, :8].T
        cw = jnp.zeros((8, SSD_CONV_DIM), F32).at[:3].set(ssd_conv_w[i])
        a_neg = -jnp.exp(ssd_a_log[i])
        bias_c = _lane_row(ssd_dt_bias[i])
        a_c = _lane_row(a_neg)
        bias_r = jnp.broadcast_to(ssd_dt_bias[i].reshape(8, 1), (8, SSD_STEP_CHUNKS * SSD_CHUNK))
        a_r = jnp.broadcast_to(a_neg.reshape(8, 1), (8, SSD_STEP_CHUNKS * SSD_CHUNK))
        dsk = jnp.repeat(ssd_d[i], SSD_HEAD_DIM)[None, :]
        y_f, y_b = _ssd_call(xbc_raw, dt_raw, dt_r, cw, ssd_conv_b[i][None, :], bias_c, a_c, bias_r, a_r, dsk,
                             n_lat)

        y_s5 = _s5_mix(u, *(w[i] for w in s5_w), s5_spread, n_lat)

        lf = da_lambda[i]
        lam = (jnp.exp(jnp.sum(lf[0] * lf[1])) - jnp.exp(jnp.sum(lf[2] * lf[3])) + lam_init).reshape(1)
        g_sub = da_sub_norm[i][None, :]
        score_bound = LOG2E * math.sqrt(DA_HEAD_DIM) * jnp.max(jnp.abs(da_q_norm[i])) * jnp.max(jnp.abs(da_k_norm[i]))
        flag = (score_bound <= SCORE_LOG2_LIMIT).astype(jnp.int32).reshape(1)
        y_da = _attn_call(flag, lam, qh, kzb, vb, g_sub, 1.0 - lam_init, n_lat, 0, None, None, None)
        last = i == depth - 1
        if not last:
            y_da = _attn_call(flag, lam, qh, kzb, vb, g_sub, 1.0 - lam_init, n_ctx, n_lat,
                              n_lat // tk, n_ctx, n_lat % tk, y_rest=y_da)

        xs = _outffn_call(xs, y_f, y_b, z, ssd_norm[i][None, :], y_s5, u, s5_d[i][None, :], s5_glu_w[i],
                          s5_glu_b[i][None, :], y_da, modl, norm2[i][None, :], w_out_b, w1_b, w3_b, w2_b, i,
                          n_lat, latent_only=last)
    return xs[None]
```

```python
import functools
import math

import jax
import jax.numpy as jnp
from jax import lax
from jax.experimental import pallas as pl
from jax.experimental.pallas import tpu as pltpu

F32 = jnp.float32
BF16 = jnp.bfloat16
HI = lax.Precision.HIGHEST

EPS = 1e-6
GRID_W = 64
ROPE_THETA = 10000.0
ROPE_FREQS = 8

SSD_HEADS = 4
SSD_HEAD_DIM = 64
SSD_INNER = 256
SSD_GROUPS = 2
SSD_STATE = 64
SSD_CHUNK = 128
SSD_STEP_CHUNKS = 2
SSD_CONV_DIM = 512

S5_WIDTH = 256
S5_GROUP = 16
S5_GROUPS = 16
S5_STATE = 64
S5_CHUNK = 16
S5_HALF_GROUPS = S5_GROUPS // 2
S5_HALF_STATE = S5_HALF_GROUPS * S5_STATE
S5_SCAN_ROWS = 64
S5_TOEPLITZ_BLOCKS = 2 * S5_CHUNK - 1
S5_KCOLS = 512

DA_HEADS = 8
DA_HEAD_DIM = 32
DA_V_DIM = 64
DA_WIDTH = 512
ATTN_HEADS_PER_STEP = 2
ATTN_TQ = 2048
ATTN_TK = 1280
LOG2E = 1.4426950408889634
SCORE_LOG2_LIMIT = 100.0

LANES = 128
ROW_TILE = 256
FFN_ROW_TILE = 256
IN_PAD = 2688
VMEM_LIMIT = 56 * 1024 * 1024


def _cparams(*sem):
    return pltpu.CompilerParams(dimension_semantics=sem, vmem_limit_bytes=VMEM_LIMIT)


def _dot(a, b, precision=None):
    return jnp.dot(a, b, preferred_element_type=F32, precision=precision)


def _dot_nt(a, b, precision=None):
    return lax.dot_general(a, b, (((1,), (1,)), ((), ())), preferred_element_type=F32,
                           precision=precision)


def _dot_tn(a, b, precision=None):
    return lax.dot_general(a, b, (((0,), (0,)), ((), ())), preferred_element_type=F32,
                           precision=precision)


def _silu(x):
    return x * jax.nn.sigmoid(x)


def _pick_tile(n, candidates):
    for c in candidates:
        if n % c == 0:
            return c
    return n


def _interleave(*stages):
    results = [None] * len(stages)
    live = dict(enumerate(stages))
    while live:
        for k in list(live):
            try:
                next(live[k])
            except StopIteration as done:
                results[k] = done.value
                del live[k]
        yield
    return results


def _run(stage):
    while True:
        try:
            next(stage)
        except StopIteration as done:
            return done.value


def _segment_order(i, n_lat, n_all, reverse):
    n_ctx = n_all - n_lat
    if not reverse:
        return jnp.where(i < n_ctx, n_lat + i, i - n_ctx)
    return jnp.where(i < n_ctx, n_all - 1 - i, n_lat - 1 - (i - n_ctx))


def _mod_body(cv_ref, w_ref, b_ref, o_ref):
    o_ref[0] = _dot(_silu(cv_ref[...]), w_ref[0], HI) + b_ref[0]


def _mod_call(cv, w_mod, b_mod):
    depth, d, n = w_mod.shape
    tn = 2048
    return pl.pallas_call(
        _mod_body,
        grid=(depth, n // tn),
        in_specs=[pl.BlockSpec((8, d), lambda l, j: (0, 0)),
                  pl.BlockSpec((1, d, tn), lambda l, j: (l, 0, j)),
                  pl.BlockSpec((1, 1, tn), lambda l, j: (l, 0, j))],
        out_specs=pl.BlockSpec((1, 8, tn), lambda l, j: (l, 0, j)),
        out_shape=jax.ShapeDtypeStruct((depth, 8, n), F32),
        compiler_params=_cparams("arbitrary", "arbitrary"),
        name="adaln_mod",
    )(cv, w_mod, b_mod.reshape(depth, 1, n))


def _inproj_project(x_ref, g_ref, mod_ref, w_ref, z_ref, xbc_ref, u_ref, dt_ref, qkv_ref):
    x = x_ref[...]
    mod = mod_ref[0]
    ms = jnp.mean(x * x, axis=-1, keepdims=True)
    h = x * lax.rsqrt(ms + EPS) * g_ref[...] * (1.0 + mod[1:2]) + mod[0:1]
    p = _dot(h.astype(BF16), w_ref[...])
    z_ref[...] = p[:, 0:256]
    xbc_ref[...] = p[:, 256:768]
    u_ref[...] = p[:, 768:1024]
    dt_ref[...] = p[:, 2560:2688]
    qkv_ref[...] = p[:, 1024:2560]


def _inproj_attn_operands(qkv_ref, gq_ref, gk_ref, gm_ref, cos_ref, sa_ref, sb_ref, q_ref, k_ref, v_ref):
    p = qkv_ref[...]
    rows = p.shape[0]
    gm = gm_ref[...]
    cos = cos_ref[...]
    sa = sa_ref[...]
    sb = sb_ref[...]

    def norm_rope(t, gain, scale):
        ms32 = _dot((t * t).astype(BF16), gm)
        tn = t * lax.rsqrt(ms32 + EPS) * gain
        outs = []
        for j in range(DA_WIDTH // LANES):
            tb = tn[:, j * LANES:(j + 1) * LANES]
            ob = tb * cos + pltpu.roll(tb, LANES - ROPE_FREQS, 1) * sa + pltpu.roll(tb, ROPE_FREQS, 1) * sb
            outs.append(ob * scale)
        return jnp.concatenate(outs, axis=1)

    qt = norm_rope(p[:, 0:512], gq_ref[...], LOG2E * DA_HEAD_DIM ** -0.5).T.astype(BF16)
    kn = norm_rope(p[:, 512:1024], gk_ref[...], 1.0).astype(BF16)
    vt = p[:, 1024:1536].T.astype(BF16)
    srow = lax.broadcasted_iota(jnp.int32, (LANES - DA_V_DIM, rows), 0)
    ones_row = jnp.where(srow == 0, 1.0, 0.0).astype(BF16)
    zeros_q = jnp.zeros((DA_HEAD_DIM, rows), BF16)
    for hd in range(DA_HEADS):
        lo_, mid, hi_ = hd * DA_V_DIM, hd * DA_V_DIM + DA_HEAD_DIM, (hd + 1) * DA_V_DIM
        k_ref[hd, 0] = kn[:, lo_:hi_]
        q_ref[hd, 0, 0:DA_HEAD_DIM, :] = qt[lo_:mid]
        q_ref[hd, 0, DA_HEAD_DIM:, :] = zeros_q
        q_ref[hd, 1, 0:DA_HEAD_DIM, :] = zeros_q
        q_ref[hd, 1, DA_HEAD_DIM:, :] = qt[mid:hi_]
        v_ref[hd, 0, 0:DA_V_DIM, :] = vt[lo_:hi_]
        v_ref[hd, 0, DA_V_DIM:, :] = ones_row


def _inproj_body(x_ref, g_ref, mod_ref, w_ref, gq_ref, gk_ref, gm_ref, cos_ref, sa_ref, sb_ref,
                 z_ref, xbc_ref, u_ref, dt_ref, q_ref, k_ref, v_ref, qkv_a_ref, qkv_b_ref):
    i = pl.program_id(0)

    @pl.when(i == 0)
    def _():
        qkv_b_ref[...] = jnp.zeros_like(qkv_b_ref)

    def step(cur_ref, prev_ref):
        _inproj_attn_operands(prev_ref, gq_ref, gk_ref, gm_ref, cos_ref, sa_ref, sb_ref, q_ref, k_ref, v_ref)
        _inproj_project(x_ref, g_ref, mod_ref, w_ref, z_ref, xbc_ref, u_ref, dt_ref, cur_ref)

    @pl.when(i % 2 == 0)
    def _():
        step(qkv_a_ref, qkv_b_ref)

    @pl.when(i % 2 == 1)
    def _():
        step(qkv_b_ref, qkv_a_ref)


def _inproj_call(xs, g1, modl, w_r, gq, gk, gmat, cos_t, sa_t, sb_t, n_lat_tiles, tk):
    t, d = xs.shape
    tm = ROW_TILE
    n_tiles = t // tm
    per = tk // tm
    row = lambda i: (jnp.minimum(i, n_tiles - 1), 0)
    done = lambda i: (jnp.maximum(i - 1, 0), 0)
    const = lambda i: (0, 0)
    seg = lambda i: ((jnp.minimum(i, n_tiles - 1) >= n_lat_tiles).astype(jnp.int32), 0, 0)
    flat = [(256, F32), (512, F32), (256, F32), (LANES, F32)]

    def k_block(i):
        j = jnp.maximum(i - 1, 0)
        return (0, j // per, j % per, 0)

    def v_block(i):
        j = jnp.maximum(i - 1, 0)
        return (0, j // per, 0, j % per)

    return pl.pallas_call(
        _inproj_body,
        grid=(n_tiles + 1,),
        in_specs=[pl.BlockSpec((tm, d), row),
                  pl.BlockSpec((1, d), const),
                  pl.BlockSpec((1, 8, d), seg),
                  pl.BlockSpec((d, IN_PAD), const),
                  pl.BlockSpec((1, DA_WIDTH), const),
                  pl.BlockSpec((1, DA_WIDTH), const),
                  pl.BlockSpec((DA_WIDTH, DA_WIDTH), const),
                  pl.BlockSpec((tm, LANES), done),
                  pl.BlockSpec((tm, LANES), done),
                  pl.BlockSpec((tm, LANES), done)],
        out_specs=[pl.BlockSpec((tm, w), row) for w, _ in flat]
                  + [pl.BlockSpec((DA_HEADS, 2, 2 * DA_HEAD_DIM, tm), lambda i: (0, 0, 0, jnp.maximum(i - 1, 0))),
                     pl.BlockSpec((DA_HEADS, 1, tm, 2 * DA_HEAD_DIM), k_block),
                     pl.BlockSpec((DA_HEADS, 1, LANES, tm), v_block)],
        out_shape=[jax.ShapeDtypeStruct((t, w), dt) for w, dt in flat]
                  + [jax.ShapeDtypeStruct((DA_HEADS, 2, 2 * DA_HEAD_DIM, t), BF16),
                     jax.ShapeDtypeStruct((DA_HEADS, t // tk, tk, 2 * DA_HEAD_DIM), BF16),
                     jax.ShapeDtypeStruct((DA_HEADS, t // tk, LANES, tk), BF16)],
        scratch_shapes=[pltpu.VMEM((tm, 3 * DA_WIDTH), F32)] * 2,
        compiler_params=_cparams("arbitrary"),
        name="in_proj",
    )(xs, g1, modl, w_r, gq, gk, gmat, cos_t, sa_t, sb_t)


def _softplus(x):
    return jnp.maximum(x, 0.0) + jnp.log1p(jnp.exp(-jnp.abs(x)))


def _ssd_block(c, n_lat, n_blocks, reverse, xc_ref, xp_ref, xn_ref, dtc_ref, dtr_ref, cw_ref, cb_ref,
               bias_c_ref, a_c_ref, bias_r_ref, a_r_ref, st_ref):
    L = SSD_CHUNK
    rows = xc_ref.shape[0]
    x = xc_ref[...]
    seg_first = jnp.logical_or(c == 0, c == n_lat)
    seg_last = jnp.logical_or(c == n_lat - 1, c == n_blocks - 1)
    prow = jnp.where(seg_first, 0.0, xp_ref[7:8, :])
    nrow = jnp.where(seg_last, 0.0, xn_ref[0:1, :])
    ridx = lax.broadcasted_iota(jnp.int32, (rows, 1), 0)
    xprev = jnp.where(ridx == 0, prow, pltpu.roll(x, 1, 0))
    xnext = jnp.where(ridx == rows - 1, nrow, pltpu.roll(x, rows - 1, 0))
    cw = cw_ref[...]
    conv = xprev * cw[0:1] + x * cw[1:2] + xnext * cw[2:3] + cb_ref[...]
    xbc = _silu(conv)

    dt_c = _softplus(dtc_ref[...] + bias_c_ref[...])
    dt_r = _softplus(dtr_ref[...] + bias_r_ref[...])
    adt_c = dt_c * a_c_ref[...]
    adt_r = dt_r * a_r_ref[...]

    li = lax.broadcasted_iota(jnp.int32, (L, L), 0)
    si = lax.broadcasted_iota(jnp.int32, (L, L), 1)
    mask = (li <= si) if reverse else (li >= si)
    mask_t = (li >= si) if reverse else (li <= si)
    n_sub = rows // L
    yield
    parts = yield from _interleave(*[
        _ssd_chunk_local(xbc[s * L:(s + 1) * L], dt_c[s * L:(s + 1) * L], adt_c[s * L:(s + 1) * L],
                         adt_r[:, s * L:(s + 1) * L], mask, mask_t, reverse) for s in range(n_sub)])
    ys = [None] * n_sub
    for s in (reversed(range(n_sub)) if reverse else range(n_sub)):
        ys[s] = _ssd_chunk_state(*parts[s], st_ref, reverse)
        yield
    return jnp.concatenate(ys, axis=0), xbc[:, :SSD_INNER]


def _split3(x):
    hi = x.astype(BF16)
    r1 = x - hi.astype(F32)
    mid = r1.astype(BF16)
    lo = (r1 - mid.astype(F32)).astype(BF16)
    return hi, mid, lo


def _ssd_chunk_local(xbc, dt_c, adt_c, adt_r, mask, mask_t, reverse):
    L = SSD_CHUNK
    hpg = SSD_HEADS // SSD_GROUPS
    off = SSD_HEADS if reverse else 0
    m_b, mt_b = mask.astype(BF16), mask_t.astype(BF16)
    cs_c = sum(_dot(m_b, part) for part in _split3(adt_c))
    cs_r = sum(_dot(part, mt_b) for part in _split3(adt_r))
    edge = 0 if reverse else L - 1
    tot = cs_c[edge:edge + 1, :]
    yield

    kk = lax.broadcasted_iota(jnp.int32, (LANES, SSD_INNER), 0)
    jj = lax.broadcasted_iota(jnp.int32, (LANES, SSD_INNER), 1)
    spread = (kk == off + jj // SSD_HEAD_DIM).astype(BF16)
    cols = jnp.concatenate([dt_c, jnp.exp(cs_c), jnp.exp(tot - cs_c)], axis=0)
    c_hi = cols.astype(BF16)
    c_lo = (cols - c_hi.astype(F32)).astype(BF16)
    wide = _dot(c_hi, spread) + _dot(c_lo, spread)
    dt_w, ecs_w, dst_w = wide[0:L], wide[L:2 * L], wide[2 * L:3 * L]
    yield

    xs = xbc[:, :SSD_INNER]
    b_all = xbc[:, SSD_INNER:SSD_INNER + LANES]
    c_all = xbc[:, SSD_INNER + LANES:SSD_INNER + 2 * LANES]
    xd = xs * dt_w
    xd_b = xd.astype(BF16)
    b_b = b_all.astype(BF16)
    c_b = c_all.astype(BF16)
    lane = lax.broadcasted_iota(jnp.int32, (L, LANES), 1)
    col = lax.broadcasted_iota(jnp.int32, (L, SSD_INNER), 1)

    y = None
    for g in range(SSD_GROUPS):
        c_g = jnp.where(lane // SSD_STATE == g, c_b, jnp.zeros_like(c_b))
        gmat = _dot_nt(c_g, b_b)
        for hh in range(hpg):
            h = g * hpg + hh
            dec = jnp.exp(jnp.where(mask, cs_c[:, off + h:off + h + 1] - cs_r[off + h:off + h + 1, :], -jnp.inf))
            xd_h = jnp.where(col // SSD_HEAD_DIM == h, xd_b, jnp.zeros_like(xd_b))
            part = _dot((gmat * dec).astype(BF16), xd_h)
            y = part if y is None else y + part
            yield
    return y, c_b, ecs_w, b_all.T.astype(BF16), (xd * dst_w).astype(BF16)


def _ssd_chunk_state(y_local, c_b, ecs_w, bt_b, xds_b, st_ref, reverse):
    edge = 0 if reverse else SSD_CHUNK - 1
    y = y_local + ecs_w * _dot(c_b, st_ref[...].astype(BF16))
    kr = lax.broadcasted_iota(jnp.int32, (LANES, SSD_INNER), 0)
    kc = lax.broadcasted_iota(jnp.int32, (LANES, SSD_INNER), 1)
    block = kr // SSD_STATE == kc // (SSD_HEADS // SSD_GROUPS * SSD_HEAD_DIM)
    st_ref[...] = st_ref[...] * ecs_w[edge:edge + 1, :] + jnp.where(block, _dot(bt_b, xds_b), 0.0)
    return y


def _ssd_body(n_lat, n_chunks, *refs):
    fwd_in, bwd_in = refs[0:5], refs[5:10]
    cw_ref, cb_ref, bias_c_ref, a_c_ref, bias_r_ref, a_r_ref, dsk_ref, yf_ref, yb_ref, st_ref = refs[10:]
    i = pl.program_id(0)

    @pl.when(i == 0)
    def _():
        st_ref[...] = jnp.zeros_like(st_ref)

    shared = (cw_ref, cb_ref, bias_c_ref, a_c_ref, bias_r_ref, a_r_ref)
    (y_f, xs), (y_b, _) = _run(_interleave(
        _ssd_block(_segment_order(i, n_lat, n_chunks, False), n_lat, n_chunks, False, *fwd_in, *shared,
                   st_ref.at[0]),
        _ssd_block(_segment_order(i, n_lat, n_chunks, True), n_lat, n_chunks, True, *bwd_in, *shared,
                   st_ref.at[1])))
    yf_ref[...] = y_f + dsk_ref[...] * xs
    yb_ref[...] = y_b


def _ssd_call(xbc_raw, dt_c, dt_r, cw, cb, bias_c, a_c, bias_r, a_r, dsk, n_lat_rows):
    t = xbc_raw.shape[0]
    L = SSD_STEP_CHUNKS * SSD_CHUNK
    assert t % L == 0 and n_lat_rows % L == 0
    n_chunks = t // L
    n_lat = n_lat_rows // L
    sub = L // 8
    n_sub = t // 8
    const = lambda i: (0, 0)

    def chunk_specs(reverse):
        cidx = functools.partial(_segment_order, n_lat=n_lat, n_all=n_chunks, reverse=reverse)
        return [pl.BlockSpec((L, SSD_CONV_DIM), lambda i: (cidx(i), 0)),
                pl.BlockSpec((8, SSD_CONV_DIM), lambda i: (jnp.maximum(cidx(i) * sub - 1, 0), 0)),
                pl.BlockSpec((8, SSD_CONV_DIM), lambda i: (jnp.minimum((cidx(i) + 1) * sub, n_sub - 1), 0)),
                pl.BlockSpec((L, LANES), lambda i: (cidx(i), 0)),
                pl.BlockSpec((8, L), lambda i: (0, cidx(i)))]

    def out_spec(reverse):
        cidx = functools.partial(_segment_order, n_lat=n_lat, n_all=n_chunks, reverse=reverse)
        return pl.BlockSpec((L, SSD_INNER), lambda i: (cidx(i), 0))

    chunk_args = (xbc_raw, xbc_raw, xbc_raw, dt_c, dt_r)
    return pl.pallas_call(
        functools.partial(_ssd_body, n_lat, n_chunks),
        grid=(n_chunks,),
        in_specs=chunk_specs(False) + chunk_specs(True)
                 + [pl.BlockSpec((8, SSD_CONV_DIM), const),
                    pl.BlockSpec((1, SSD_CONV_DIM), const),
                    pl.BlockSpec((1, LANES), const),
                    pl.BlockSpec((1, LANES), const),
                    pl.BlockSpec((8, L), const),
                    pl.BlockSpec((8, L), const),
                    pl.BlockSpec((1, SSD_INNER), const)],
        out_specs=[out_spec(False), out_spec(True)],
        out_shape=[jax.ShapeDtypeStruct((t, SSD_INNER), F32)] * 2,
        scratch_shapes=[pltpu.VMEM((2, SSD_GROUPS * SSD_STATE, SSD_INNER), F32)],
        compiler_params=_cparams("arbitrary"),
        name="ssd_scan",
    )(*chunk_args, *chunk_args, cw, cb, bias_c, a_c, bias_r, a_r, dsk)


def _cmul(ar, ai, br, bi):
    return ar * br - ai * bi, ar * bi + ai * br


def _s5_step_pair(u_ref, a):
    return jnp.concatenate([u_ref[:, 2 * a, :], u_ref[:, 2 * a + 1, :]], axis=1).astype(BF16)


def _s5_in_body(u_ref, w_ref, lam_ref, efr_ref, efi_ref, ebr_ref, ebi_ref, wp_ref):
    hs = S5_HALF_STATE

    @pl.when(pl.program_id(1) == 0)
    def _():
        lam = lam_ref[0]
        wc = w_ref[0]
        rows = lambda a: jnp.broadcast_to(a, (LANES, hs))
        fr, fi, br, bi = rows(lam[0:1]), rows(lam[1:2]), rows(lam[2:3]), rows(lam[3:4])
        rgrp = lax.broadcasted_iota(jnp.int32, (LANES, LANES), 0) // S5_GROUP
        lgrp = lax.broadcasted_iota(jnp.int32, (LANES, LANES), 1) // S5_STATE
        planes = []
        for q in range(4):
            cq = wc[:, q * S5_STATE:(q + 1) * S5_STATE]
            cq2 = jnp.concatenate([cq, cq], axis=1)
            planes.append(jnp.concatenate(
                [jnp.where(rgrp == 2 * g4 + lgrp, cq2, 0.0) for g4 in range(S5_HALF_GROUPS // 2)], axis=1))
        ar, ai = planes[0], planes[1]
        for i in reversed(range(S5_CHUNK)):
            wp_ref[i * LANES:(i + 1) * LANES, 0:hs] = ar.astype(BF16)
            wp_ref[i * LANES:(i + 1) * LANES, hs:2 * hs] = ai.astype(BF16)
            ar, ai = _cmul(ar, ai, fr, fi)
        ar, ai = planes[2], planes[3]
        for i in range(S5_CHUNK):
            wp_ref[i * LANES:(i + 1) * LANES, 2 * hs:3 * hs] = ar.astype(BF16)
            wp_ref[i * LANES:(i + 1) * LANES, 3 * hs:4 * hs] = ai.astype(BF16)
            ar, ai = _cmul(ar, ai, br, bi)

    e = None
    for a in range(S5_CHUNK // 2):
        part = _dot(_s5_step_pair(u_ref, a), wp_ref[2 * a * LANES:2 * (a + 1) * LANES, :])
        e = part if e is None else e + part
    efr_ref[...] = e[:, 0:hs]
    efi_ref[...] = e[:, hs:2 * hs]
    ebr_ref[...] = e[:, 2 * hs:3 * hs]
    ebi_ref[...] = e[:, 3 * hs:4 * hs]


def _s5_in_call(u3, w_b, lam_rows):
    nch = u3.shape[0]
    tn = _pick_tile(nch, (208, 80, 40))
    hs = S5_HALF_STATE
    col = lambda h, n: (n, h)
    return pl.pallas_call(
        _s5_in_body,
        grid=(2, nch // tn),
        in_specs=[pl.BlockSpec((tn, S5_CHUNK, LANES), lambda h, n: (n, 0, h)),
                  pl.BlockSpec((1, LANES, 4 * S5_STATE), lambda h, n: (h, 0, 0)),
                  pl.BlockSpec((1, 4, hs), lambda h, n: (h, 0, 0))],
        out_specs=[pl.BlockSpec((tn, hs), col)] * 4,
        out_shape=[jax.ShapeDtypeStruct((nch, 2 * hs), F32)] * 4,
        scratch_shapes=[pltpu.VMEM((S5_CHUNK * LANES, 4 * hs), BF16)],
        compiler_params=_cparams("arbitrary", "arbitrary"),
        name="s5_chunk_in",
    )(u3, w_b, lam_rows)


def _s5_scan_body(n_lat_blocks, rows, ctx_rows, a_ref, efr_ref, efi_ref, ebr_ref, ebi_ref,
                  hfr_ref, hfi_ref, hbr_ref, hbi_ref, st_ref):
    i = pl.program_id(0)

    @pl.when(i == 0)
    def _():
        st_ref[...] = jnp.zeros_like(st_ref)

    n_blocks = n_lat_blocks + 1
    live_f = jnp.where(_segment_order(i, n_lat_blocks, n_blocks, False) == n_lat_blocks, ctx_rows, rows)
    live_b = jnp.where(_segment_order(i, n_lat_blocks, n_blocks, True) == n_lat_blocks, ctx_rows, rows)
    row = lambda ref, j: ref[j:j + 1, :]
    afr, afi, abr, abi = row(a_ref, 0), row(a_ref, 1), row(a_ref, 2), row(a_ref, 3)
    fr, fi, br, bi = row(st_ref, 0), row(st_ref, 1), row(st_ref, 2), row(st_ref, 3)
    for j in range(rows):
        hfr_ref[j:j + 1, :] = fr
        hfi_ref[j:j + 1, :] = fi
        nr, ni = afr * fr - afi * fi + row(efr_ref, j), afr * fi + afi * fr + row(efi_ref, j)
        fr, fi = jnp.where(j < live_f, nr, fr), jnp.where(j < live_f, ni, fi)
        jb = rows - 1 - j
        hbr_ref[jb:jb + 1, :] = br
        hbi_ref[jb:jb + 1, :] = bi
        nr, ni = abr * br - abi * bi + row(ebr_ref, jb), abr * bi + abi * br + row(ebi_ref, jb)
        br, bi = jnp.where(jb < live_b, nr, br), jnp.where(jb < live_b, ni, bi)
    st_ref[0:1, :] = fr
    st_ref[1:2, :] = fi
    st_ref[2:3, :] = br
    st_ref[3:4, :] = bi


def _s5_scan_call(a_pow, efr, efi, ebr, ebi, n_lat_rows):
    nch, width = efr.shape
    rows = _pick_tile(n_lat_rows, (S5_SCAN_ROWS, 32, 16))
    ctx_rows = nch - n_lat_rows
    assert ctx_rows <= rows
    n_lat_blocks = n_lat_rows // rows
    n_tiles = n_lat_blocks + 1
    blk = (rows, width)
    fwd = lambda i: (_segment_order(i, n_lat_blocks, n_tiles, False), 0)
    bwd = lambda i: (_segment_order(i, n_lat_blocks, n_tiles, True), 0)
    shp = jax.ShapeDtypeStruct((nch, width), F32)
    return pl.pallas_call(
        functools.partial(_s5_scan_body, n_lat_blocks, rows, ctx_rows),
        grid=(n_tiles,),
        in_specs=[pl.BlockSpec((4, width), lambda i: (0, 0)),
                  pl.BlockSpec(blk, fwd), pl.BlockSpec(blk, fwd),
                  pl.BlockSpec(blk, bwd), pl.BlockSpec(blk, bwd)],
        out_specs=[pl.BlockSpec(blk, fwd), pl.BlockSpec(blk, fwd),
                   pl.BlockSpec(blk, bwd), pl.BlockSpec(blk, bwd)],
        out_shape=[shp] * 4,
        scratch_shapes=[pltpu.VMEM((4, width), F32)],
        compiler_params=_cparams("arbitrary"),
        name="s5_state_scan",
    )(a_pow, efr, efi, ebr, ebi)


def _s5_out_body(u_ref, hfr_ref, hfi_ref, hbr_ref, hbi_ref, k_ref, t_ref, c_ref, lam_ref, y_ref, wq_ref, z_ref):
    hs = S5_HALF_STATE
    width = S5_CHUNK * LANES

    @pl.when(pl.program_id(1) == 0)
    def _():
        strip = _dot(k_ref[0].astype(BF16), t_ref[...])
        rgrp = lax.broadcasted_iota(jnp.int32, strip.shape, 0) // S5_GROUP
        cgrp = (lax.broadcasted_iota(jnp.int32, strip.shape, 1) % LANES) // S5_GROUP
        strip = jnp.where(rgrp == cgrp, strip, 0.0).astype(BF16)
        z_ref[0:LANES, :] = strip[:, LANES:]
        z_ref[LANES:, :] = strip[:, :-LANES]

        cc = c_ref[0]
        ogrp = lax.broadcasted_iota(jnp.int32, (S5_STATE, LANES), 1) // S5_GROUP

        def expand(q):
            cq = cc[q * S5_STATE:(q + 1) * S5_STATE]
            return jnp.concatenate([jnp.where(ogrp == g, cq, 0.0) for g in range(S5_HALF_GROUPS)], axis=0)

        lam = lam_ref[0]
        for d, order in ((0, range(S5_CHUNK)), (1, reversed(range(S5_CHUNK)))):
            lr, li = lam[2 * d], lam[2 * d + 1]
            a, b = expand(2 * d), expand(2 * d + 1)
            for j in order:
                a, b = a * lr + b * li, b * lr - a * li
                wq_ref[2 * d * hs:(2 * d + 1) * hs, j * LANES:(j + 1) * LANES] = a.astype(BF16)
                wq_ref[(2 * d + 1) * hs:(2 * d + 2) * hs, j * LANES:(j + 1) * LANES] = b.astype(BF16)

    h_all = jnp.concatenate([hfr_ref[...], hfi_ref[...], hbr_ref[...], hbi_ref[...]], axis=1).astype(BF16)
    y = _dot(h_all, wq_ref[...])
    for a in range(S5_CHUNK // 2):
        lo = (S5_CHUNK - 2 - 2 * a) * LANES
        y += _dot(_s5_step_pair(u_ref, a), z_ref[:, lo:lo + width])
    for j in range(S5_CHUNK):
        y_ref[:, j, :] = y[:, j * LANES:(j + 1) * LANES]


def _s5_out_call(u3, hfr, hfi, hbr, hbi, w_k, w_t, w_c, lam_cols):
    nch = u3.shape[0]
    tn = _pick_tile(nch, (208, 80, 40))
    hs = S5_HALF_STATE
    col = lambda h, n: (n, h)
    blk3 = pl.BlockSpec((tn, S5_CHUNK, LANES), lambda h, n: (n, 0, h))
    return pl.pallas_call(
        _s5_out_body,
        grid=(2, nch // tn),
        in_specs=[blk3] + [pl.BlockSpec((tn, hs), col)] * 4
                 + [pl.BlockSpec((1, LANES) + w_k.shape[2:], lambda h, n: (h, 0, 0)),
                    pl.BlockSpec(w_t.shape, lambda h, n: (0, 0)),
                    pl.BlockSpec((1, 4 * S5_STATE, LANES), lambda h, n: (h, 0, 0)),
                    pl.BlockSpec((1, 4, hs, LANES), lambda h, n: (h, 0, 0, 0))],
        out_specs=blk3,
        out_shape=jax.ShapeDtypeStruct(u3.shape, F32),
        scratch_shapes=[pltpu.VMEM((4 * hs, S5_CHUNK * LANES), BF16),
                        pltpu.VMEM((2 * LANES, (S5_TOEPLITZ_BLOCKS - 1) * LANES), BF16)],
        compiler_params=_cparams("arbitrary", "arbitrary"),
        name="s5_chunk_out",
    )(u3, hfr, hfi, hbr, hbi, w_k, w_t, w_c, lam_cols)


def _s5_weights(lam_re, lam_im, log_step, b_re, b_im, c_re, c_im):
    tc = S5_CHUNK
    nl = lam_re.shape[0]
    hg = S5_HALF_GROUPS
    ein = functools.partial(jnp.einsum, precision=HI)
    step = jnp.exp(log_step)[..., None]
    er, ei = lam_re * step, lam_im * step
    mag = jnp.exp(er)
    lbr, lbi = mag * jnp.cos(ei), mag * jnp.sin(ei)
    den = lam_re * lam_re + lam_im * lam_im
    nr, ni = lbr - 1.0, lbi
    cr, ci = (nr * lam_re + ni * lam_im) / den, (ni * lam_re - nr * lam_im) / den
    bbr = cr[..., None] * b_re - ci[..., None] * b_im
    bbi = cr[..., None] * b_im + ci[..., None] * b_re
    taus = jnp.arange(tc + 1, dtype=F32)[:, None, None]
    pmag = jnp.exp(er[:, :, None] * taus)
    pr, pi = pmag * jnp.cos(ei[:, :, None] * taus), pmag * jnp.sin(ei[:, :, None] * taus)

    wr = pr[:, :, :tc, :, :, None] * bbr[:, :, None] - pi[:, :, :tc, :, :, None] * bbi[:, :, None]
    wi = pr[:, :, :tc, :, :, None] * bbi[:, :, None] + pi[:, :, :tc, :, :, None] * bbr[:, :, None]
    kk = ein('ldgcp,ldtgpk->ldtgck', c_re, wr) - ein('ldgcp,ldtgpk->ldtgck', c_im, wi)
    kf, kb = kk[:, 0], kk[:, 1]
    seq = jnp.concatenate([kb[:, :0:-1], kf[:, :1] + kb[:, :1], kf[:, 1:]], axis=1)
    seq = seq.reshape(nl, S5_TOEPLITZ_BLOCKS, 2, hg, S5_GROUP, S5_GROUP)
    w_k = seq.transpose(0, 2, 3, 5, 1, 4).reshape(nl, 2, LANES, S5_TOEPLITZ_BLOCKS * S5_GROUP)
    w_k = jnp.pad(w_k, ((0, 0), (0, 0), (0, 0), (0, S5_KCOLS - S5_TOEPLITZ_BLOCKS * S5_GROUP)))
    r = jnp.arange(S5_KCOLS)
    s = jnp.arange(S5_TOEPLITZ_BLOCKS * LANES)
    w_t = ((r[:, None] // S5_GROUP == s[None, :] // LANES)
           & (r[:, None] % S5_GROUP == s[None, :] % S5_GROUP)).astype(BF16)

    planes_b = jnp.stack([bbr[:, 0], bbi[:, 0], bbr[:, 1], bbi[:, 1]], axis=1)
    planes_b = planes_b.reshape(nl, 4, 2, hg, S5_STATE, S5_GROUP)
    w_b = planes_b.transpose(0, 2, 3, 5, 1, 4).reshape(nl, 2, LANES, 4 * S5_STATE)
    planes_c = jnp.stack([c_re[:, 0], -c_im[:, 0], c_re[:, 1], -c_im[:, 1]], axis=1)
    planes_c = planes_c.reshape(nl, 4, 2, hg, S5_GROUP, S5_STATE)
    w_c = planes_c.transpose(0, 2, 1, 5, 3, 4).reshape(nl, 2, 4 * S5_STATE, LANES)

    halves = lambda a: a.reshape(nl, 2, S5_HALF_STATE)
    lam_rows = jnp.stack([halves(lbr[:, 0]), halves(lbi[:, 0]), halves(lbr[:, 1]), halves(lbi[:, 1])], axis=2)
    lam_cols = jnp.broadcast_to(lam_rows[..., None], lam_rows.shape + (LANES,))
    full = lambda a: a.reshape(nl, S5_GROUPS * S5_STATE)
    a_pow = jnp.stack([full(pr[:, 0, tc]), full(pi[:, 0, tc]), full(pr[:, 1, tc]), full(pi[:, 1, tc])], axis=1)
    return (w_b, w_k, w_c, lam_rows, lam_cols, a_pow), w_t


def _s5_mix(u, w_b, w_k, w_c, lam_rows, lam_cols, a_pow, w_t, n_lat):
    t = u.shape[0]
    nch = t // S5_CHUNK
    u3 = u.reshape(nch, S5_CHUNK, S5_WIDTH)
    es = _s5_in_call(u3, w_b, lam_rows)
    hs = _s5_scan_call(a_pow, *es, n_lat // S5_CHUNK)
    return _s5_out_call(u3, *hs, w_k, w_t, w_c, lam_cols).reshape(t, S5_WIDTH)


def _attn_body(flag_ref, lam_ref, q_ref, k_ref, v_ref, g_ref, o_ref, acc_ref, m_ref, oh_ref, *, n_kv, post_scale):
    bounded = flag_ref[0] == 1

    def head(hh, carry):
        acc_ref[...] = jnp.zeros_like(acc_ref)

        @pl.when(bounded)
        def _():
            def step(b, carry2):
                k = k_ref[hh, b]
                vt = v_ref[hh, b]
                for c in range(2):
                    p = jnp.exp2(_dot(k, q_ref[hh, c])).astype(BF16)
                    acc_ref[c] += _dot(vt, p)
                return carry2
            lax.fori_loop(0, n_kv, step, 0, unroll=4)

        @pl.when(jnp.logical_not(bounded))
        def _():
            m_ref[...] = jnp.full_like(m_ref, -jnp.inf)

            def step(b, carry2):
                k = k_ref[hh, b]
                vt = v_ref[hh, b]
                for c in range(2):
                    s = _dot(k, q_ref[hh, c])
                    m_old = m_ref[c]
                    m_new = jnp.maximum(m_old, jnp.max(s, axis=0, keepdims=True))
                    p = jnp.exp2(s - m_new[0:1]).astype(BF16)
                    acc_ref[c] = jnp.exp2(m_old - m_new)[0:1] * acc_ref[c] + _dot(vt, p)
                    m_ref[c] = m_new
                return carry2
            lax.fori_loop(0, n_kv, step, 0)

        a0 = acc_ref[0]
        a1 = acc_ref[1]
        o = a0 / a0[DA_V_DIM:DA_V_DIM + 1] - lam_ref[0] * (a1 / a1[DA_V_DIM:DA_V_DIM + 1])
        vrow = lax.broadcasted_iota(jnp.int32, o.shape, 0) < DA_V_DIM
        o = jnp.where(vrow, o, 0.0)
        ms = jnp.sum(o * o, axis=0, keepdims=True) * (1.0 / DA_V_DIM)
        ot = (o * lax.rsqrt(ms + EPS)).T
        oh_ref[hh] = ot[:, :DA_V_DIM] * g_ref[...] * post_scale
        return carry

    lax.fori_loop(0, ATTN_HEADS_PER_STEP, head, 0)
    o_ref[...] = jnp.concatenate([oh_ref[hh] for hh in range(ATTN_HEADS_PER_STEP)], axis=1)


def _attn_body_keep_rest(flag_ref, lam_ref, q_ref, k_ref, v_ref, g_ref, rest_ref, *refs, **kw):
    del rest_ref
    _attn_body(flag_ref, lam_ref, q_ref, k_ref, v_ref, g_ref, *refs, **kw)


def _attn_call(flag, lam, qt, kb, vtb, g_sub, post_scale, q_rows, q_row0, kv_block, kv_cols, kv_col0, y_rest=None):
    nh = qt.shape[0]
    hp = ATTN_HEADS_PER_STEP
    tk = kb.shape[2]
    tq = _pick_tile(q_rows, (ATTN_TQ, 256))
    assert q_row0 % tq == 0
    q0 = q_row0 // tq
    if kv_block is None:
        n_kv = kb.shape[1]
        one = pl.Buffered(1)
        k_spec = pl.BlockSpec((hp, n_kv, tk, 2 * DA_HEAD_DIM), lambda h, i: (h, 0, 0, 0), pipeline_mode=one)
        v_spec = pl.BlockSpec((hp, n_kv, LANES, tk), lambda h, i: (h, 0, 0, 0), pipeline_mode=one)
    else:
        assert kv_col0 % kv_cols == 0 and tk % kv_cols == 0
        n_kv, cb = 1, kv_col0 // kv_cols
        k_spec = pl.BlockSpec((hp, 1, kv_cols, 2 * DA_HEAD_DIM), lambda h, i: (h, kv_block, cb, 0))
        v_spec = pl.BlockSpec((hp, 1, LANES, kv_cols), lambda h, i: (h, kv_block, 0, cb))
    smem = pl.BlockSpec(memory_space=pltpu.SMEM)
    in_specs = [smem, smem,
                pl.BlockSpec((hp, 2, 2 * DA_HEAD_DIM, tq), lambda h, i: (h, 0, 0, q0 + i)),
                k_spec, v_spec,
                pl.BlockSpec((1, DA_V_DIM), lambda h, i: (0, 0))]
    args = (flag, lam, qt, kb, vtb, g_sub)
    body = functools.partial(_attn_body, n_kv=n_kv, post_scale=post_scale)
    aliases = {}
    if y_rest is not None:
        in_specs.append(pl.BlockSpec(memory_space=pl.ANY))
        args += (y_rest,)
        body = functools.partial(_attn_body_keep_rest, n_kv=n_kv, post_scale=post_scale)
        aliases = {len(args) - 1: 0}
    return pl.pallas_call(
        body,
        grid=(nh // hp, q_rows // tq),
        in_specs=in_specs,
        out_specs=pl.BlockSpec((tq, hp * DA_V_DIM), lambda h, i: (q0 + i, h)),
        out_shape=jax.ShapeDtypeStruct((qt.shape[3], nh * DA_V_DIM), F32),
        scratch_shapes=[pltpu.VMEM((2, LANES, tq), F32), pltpu.VMEM((2, 8, tq), F32),
                        pltpu.VMEM((hp, tq, DA_V_DIM), F32)],
        input_output_aliases=aliases,
        compiler_params=_cparams("arbitrary", "arbitrary"),
        name="diff_attn",
    )(*args)


def _gelu_tanh(x):
    return 0.5 * x * (1.0 + jnp.tanh(math.sqrt(2.0 / math.pi) * (x + 0.044715 * (x * x * x))))


def _outffn_body(x_ref, yf_ref, yb_ref, z_ref, gssd_ref, ys5_ref, u_ref, d5_ref, gw_ref, gb_ref,
                 yda_ref, mod_ref, modp_ref, g2_ref, wo_ref, w1_ref, w3_ref, w2_ref, o_ref, xm_ref, hb_ref):
    i = pl.program_id(0)
    cur = i % 2
    prev = 1 - cur

    @pl.when(i == 0)
    def _():
        xm_ref[...] = jnp.zeros_like(xm_ref)
        hb_ref[...] = jnp.zeros_like(hb_ref)

    def finish():
        hp = hb_ref[prev]
        a1 = _dot(hp, w1_ref[...])
        yield
        a3 = _dot(hp, w3_ref[...])
        yield
        f = (_silu(a1) * a3).astype(BF16)
        yield
        o_ref[...] = xm_ref[prev] + modp_ref[0][5:6] * _dot(f, w2_ref[...])

    def prepare():
        mod = mod_ref[0]
        y = (yf_ref[...] + yb_ref[...]) * _silu(z_ref[...])
        y_ssd = y * lax.rsqrt(jnp.mean(y * y, axis=-1, keepdims=True) + EPS) * gssd_ref[...]
        yield
        y5 = _gelu_tanh(ys5_ref[...] + d5_ref[...] * u_ref[...])
        yield
        y5 = y5 * jax.nn.sigmoid(_dot(y5.astype(BF16), gw_ref[...].astype(BF16)) + gb_ref[...])
        yield
        o = _dot(y_ssd.astype(BF16), wo_ref[0:256, :])
        yield
        o += _dot(y5.astype(BF16), wo_ref[256:512, :])
        yield
        o += _dot(yda_ref[...].astype(BF16), wo_ref[512:1024, :])
        yield
        x = x_ref[...] + mod[2:3] * o
        xm_ref[cur] = x
        yield
        h = x * lax.rsqrt(jnp.mean(x * x, axis=-1, keepdims=True) + EPS) * g2_ref[...] * (1.0 + mod[4:5]) \
            + mod[3:4]
        hb_ref[cur] = h.astype(BF16)

    _run(_interleave(finish(), prepare()))


def _outffn_call(xs, y_f, y_b, z, g_ssd, y_s5, u, d5, glu_w, glu_b, y_da, modl, g2, w_o, w1, w3, w2, layer, n_lat,
                 latent_only):
    t, d = xs.shape
    if latent_only:
        t = n_lat
    dff = w1.shape[2]
    layer_block = lambda i: (layer, 0, 0)
    tm = FFN_ROW_TILE
    assert n_lat % tm == 0 and t % tm == 0
    n_lat_tiles = n_lat // tm
    n_tiles = t // tm
    row = lambda i: (jnp.minimum(i, n_tiles - 1), 0)
    done = lambda i: (jnp.maximum(i - 1, 0), 0)
    const = lambda i: (0, 0)
    seg = lambda i: ((jnp.minimum(i, n_tiles - 1) >= n_lat_tiles).astype(jnp.int32), 0, 0)
    seg_done = lambda i: ((i - 1 >= n_lat_tiles).astype(jnp.int32), 0, 0)
    one = pl.Buffered(1)
    return pl.pallas_call(
        _outffn_body,
        grid=(n_tiles + 1,),
        in_specs=[pl.BlockSpec((tm, d), row),
                  pl.BlockSpec((tm, 256), row), pl.BlockSpec((tm, 256), row), pl.BlockSpec((tm, 256), row),
                  pl.BlockSpec((1, 256), const),
                  pl.BlockSpec((tm, 256), row), pl.BlockSpec((tm, 256), row),
                  pl.BlockSpec((1, 256), const),
                  pl.BlockSpec((256, 256), const), pl.BlockSpec((1, 256), const),
                  pl.BlockSpec((tm, DA_WIDTH), row),
                  pl.BlockSpec((1, 8, d), seg),
                  pl.BlockSpec((1, 8, d), seg_done),
                  pl.BlockSpec((1, d), const),
                  pl.BlockSpec((None, d, d), layer_block, pipeline_mode=one),
                  pl.BlockSpec((None, d, dff), layer_block, pipeline_mode=one),
                  pl.BlockSpec((None, d, dff), layer_block, pipeline_mode=one),
                  pl.BlockSpec((None, dff, d), layer_block, pipeline_mode=one)],
        out_specs=pl.BlockSpec((tm, d), done),
        out_shape=jax.ShapeDtypeStruct((t, d), F32),
        scratch_shapes=[pltpu.VMEM((2, tm, d), F32), pltpu.VMEM((2, tm, d), BF16)],
        compiler_params=_cparams("arbitrary"),
        name="out_proj_ffn",
    )(xs, y_f, y_b, z, g_ssd, y_s5, u, d5, glu_w, glu_b, y_da, modl, modl, g2, w_o, w1, w3, w2)


def _cast_body(w_ref, o_ref):
    o_ref[...] = w_ref[...].astype(o_ref.dtype)


def _to_bf16(w):
    nl, rows, cols = w.shape
    tr = _pick_tile(rows, (512, 704, 256))
    blk = pl.BlockSpec((1, tr, cols), lambda l, r: (l, r, 0))
    return pl.pallas_call(
        _cast_body,
        grid=(nl, rows // tr),
        in_specs=[blk],
        out_specs=blk,
        out_shape=jax.ShapeDtypeStruct(w.shape, BF16),
        compiler_params=_cparams("arbitrary", "arbitrary"),
        name="weights_to_bf16",
    )(w)


def _rope_tables(n_lat, n_ctx):
    pos = jnp.arange(n_lat)
    lane = jnp.arange(LANES)
    d = lane % DA_HEAD_DIM
    axis = d // (2 * ROPE_FREQS)
    half = (d % (2 * ROPE_FREQS)) // ROPE_FREQS
    freqs = ROPE_THETA ** (-jnp.arange(ROPE_FREQS, dtype=F32) / ROPE_FREQS)
    coord = jnp.where(axis[None, :] == 0, (pos // GRID_W)[:, None], (pos % GRID_W)[:, None]).astype(F32)
    ang = coord * freqs[d % ROPE_FREQS][None, :]
    cos, sin = jnp.cos(ang), jnp.sin(ang)
    sa = jnp.where(half[None, :] == 0, -sin, 0.0)
    sb = jnp.where(half[None, :] == 1, sin, 0.0)
    pad = lambda a, v: jnp.concatenate([a, jnp.full((n_ctx, LANES), v, F32)], axis=0)
    return pad(cos, 1.0), pad(sa, 0.0), pad(sb, 0.0)


def _lane_row(vals, width=LANES):
    vals = vals.reshape(-1).astype(F32)
    return jnp.zeros((1, width), F32).at[0, :vals.shape[0]].set(vals)


def kernel(x, c, ctx, c_ctx, w_mod, b_mod, norm1, norm2, w_in, w_out, ssd_conv_w, ssd_conv_b, ssd_a_log, ssd_dt_bias, ssd_d, ssd_norm, s5_lam_re, s5_lam_im, s5_log_step, s5_b_re, s5_b_im, s5_c_re, s5_c_im, s5_d, s5_glu_w, s5_glu_b, da_q_norm, da_k_norm, da_lambda, da_sub_norm, ffn_w1, ffn_w3, ffn_w2):
    depth = w_mod.shape[0]
    bsz, n_lat, d = x.shape
    n_ctx = ctx.shape[1]
    t = n_ctx + n_lat
    tk = _pick_tile(t, (ATTN_TK, 1024, 640, 512, 256))
    assert bsz == 1 and n_ctx % ROW_TILE == 0 and n_lat % ROW_TILE == 0 and tk % ROW_TILE == 0
    assert n_ctx % (S5_CHUNK * 16) == 0 and n_lat % (S5_CHUNK * 16) == 0
    assert n_lat % n_ctx == 0 and tk % n_ctx == 0
    n_lat_tiles = n_lat // ROW_TILE

    xs = jnp.concatenate([x[0], ctx[0]], axis=0)
    cv = jnp.zeros((8, d), F32).at[0].set(c[0]).at[1].set(c_ctx)
    mod_all = _mod_call(cv, w_mod, b_mod)
    cos_t, sa_t, sb_t = _rope_tables(n_lat, n_ctx)
    gidx = jnp.arange(DA_WIDTH) // DA_HEAD_DIM
    gmat = (gidx[:, None] == gidx[None, :]).astype(BF16) * (1.0 / DA_HEAD_DIM)
    s5_w, s5_spread = _s5_weights(s5_lam_re, s5_lam_im, s5_log_step, s5_b_re, s5_b_im, s5_c_re, s5_c_im)
    w_out_b, w1_b, w3_b, w2_b = _to_bf16(w_out), _to_bf16(ffn_w1), _to_bf16(ffn_w3), _to_bf16(ffn_w2)

    for i in range(depth):
        lam_init = 0.8 - 0.6 * math.exp(-0.3 * i)
        modl = jnp.zeros((2, 8, d), F32).at[:, :6].set(mod_all[i, :2].reshape(2, 6, d))

        wi = w_in[i]
        w_r = jnp.concatenate([wi[:, 0:768], wi[:, 776:1032], wi[:, 1032:2568], wi[:, 768:776],
                               jnp.zeros((d, IN_PAD - 2568), F32)], axis=1).astype(BF16)
        gq = jnp.tile(da_q_norm[i], DA_WIDTH // DA_HEAD_DIM)[None, :]
        gk = jnp.tile(da_k_norm[i], DA_WIDTH // DA_HEAD_DIM)[None, :]
        z, xbc_raw, u, dt_raw, qh, kzb, vb = _inproj_call(
            xs, norm1[i][None, :], modl, w_r, gq, gk, gmat, cos_t, sa_t, sb_t, n_lat_tiles, tk)

        dt_r = dt_raw[:, :8].T
        cw = jnp.zeros((8, SSD_CONV_DIM), F32).at[:3].set(ssd_conv_w[i])
        a_neg = -jnp.exp(ssd_a_log[i])
        bias_c = _lane_row(ssd_dt_bias[i])
        a_c = _lane_row(a_neg)
        bias_r = jnp.broadcast_to(ssd_dt_bias[i].reshape(8, 1), (8, SSD_STEP_CHUNKS * SSD_CHUNK))
        a_r = jnp.broadcast_to(a_neg.reshape(8, 1), (8, SSD_STEP_CHUNKS * SSD_CHUNK))
        dsk = jnp.repeat(ssd_d[i], SSD_HEAD_DIM)[None, :]
        y_f, y_b = _ssd_call(xbc_raw, dt_raw, dt_r, cw, ssd_conv_b[i][None, :], bias_c, a_c, bias_r, a_r, dsk,
                             n_lat)

        y_s5 = _s5_mix(u, *(w[i] for w in s5_w), s5_spread, n_lat)

        lf = da_lambda[i]
        lam = (jnp.exp(jnp.sum(lf[0] * lf[1])) - jnp.exp(jnp.sum(lf[2] * lf[3])) + lam_init).reshape(1)
        g_sub = da_sub_norm[i][None, :]
        score_bound = LOG2E * math.sqrt(DA_HEAD_DIM) * jnp.max(jnp.abs(da_q_norm[i])) * jnp.max(jnp.abs(da_k_norm[i]))
        flag = (score_bound <= SCORE_LOG2_LIMIT).astype(jnp.int32).reshape(1)
        y_da = _attn_call(flag, lam, qh, kzb, vb, g_sub, 1.0 - lam_init, n_lat, 0, None, None, None)
        last = i == depth - 1
        if not last:
            y_da = _attn_call(flag, lam, qh, kzb, vb, g_sub, 1.0 - lam_init, n_ctx, n_lat,
                              n_lat // tk, n_ctx, n_lat % tk, y_rest=y_da)

        xs = _outffn_call(xs, y_f, y_b, z, ssd_norm[i][None, :], y_s5, u, s5_d[i][None, :], s5_glu_w[i],
                          s5_glu_b[i][None, :], y_da, modl, norm2[i][None, :], w_out_b, w1_b, w3_b, w2_b, i,
                          n_lat, latent_only=last)
    return xs[None]
```
